```python
import math
import jax, jax.numpy as jnp
from jax import lax
import numpy as np

D_MODEL = 1024
BATCH = 8
SEQ = 8192
DEPTH = 2

N_MIXERS = 2
D_FF = 2816
RMS_EPS = 1e-6
LN_EPS = 1e-5
DN_HEAD_DIM = 128
DN_HEADS = D_MODEL // DN_HEAD_DIM
DN_WIDTH = DN_HEADS * DN_HEAD_DIM
DN_CONV = 4
DN_CHUNK = 64
SG_WIDTH = 2 * D_MODEL
SG_GROUPS = 8
SG_CHUNK = 128
N_A = (DEPTH + 1) // 2
N_B = DEPTH // 2

kernel_name = 'hybrid_deltanet_spatialgate_macaron'


def rmsnorm(x, g, eps=RMS_EPS):
    xf = x.astype(jnp.float32)
    y = xf * lax.rsqrt(jnp.mean(xf * xf, axis=-1, keepdims=True) + eps)
    return (y * g.astype(jnp.float32)).astype(x.dtype)


def layernorm(x, g, b, eps=LN_EPS):
    xf = x.astype(jnp.float32)
    mu = jnp.mean(xf, axis=-1, keepdims=True)
    xc = xf - mu
    y = xc * lax.rsqrt(jnp.mean(xc * xc, axis=-1, keepdims=True) + eps)
    return (y * g.astype(jnp.float32) + b.astype(jnp.float32)).astype(x.dtype)


def l2norm(x, eps=1e-6):
    return x * lax.rsqrt(jnp.sum(x * x, axis=-1, keepdims=True) + eps)


def swiglu(h, w_gate, w_up, w_down):
    return (jax.nn.silu(h @ w_gate) * (h @ w_up)) @ w_down


def causal_short_conv(x, w):
    K = w.shape[0]
    S = x.shape[1]
    xp = jnp.pad(x, ((0, 0), (K - 1, 0), (0, 0)))
    return sum(xp[:, j:j + S, :] * w[j] for j in range(K))


def gated_delta_rule(q, k, v, g, beta):
    B, H, S, Dk = q.shape
    Dv = v.shape[-1]
    C = DN_CHUNK
    N = S // C
    q = q * (Dk ** -0.5)
    q = q.reshape(B, H, N, C, Dk)
    k = k.reshape(B, H, N, C, Dk)
    v = v.reshape(B, H, N, C, Dv)
    g = g.reshape(B, H, N, C)
    beta = beta.reshape(B, H, N, C)
    gc = jnp.cumsum(g, axis=-1)
    causal = jnp.tril(jnp.ones((C, C), dtype=bool))
    strict = jnp.tril(jnp.ones((C, C), dtype=bool), -1)
    diff = gc[..., :, None] - gc[..., None, :]
    decay = jnp.where(causal, jnp.exp(jnp.where(causal, diff, 0.0)), 0.0)
    k_beta = k * beta[..., None]
    v_beta = v * beta[..., None]
    L = jnp.where(strict, jnp.einsum('bhnid,bhnjd->bhnij', k_beta, k) * decay, 0.0)
    A = L + jnp.eye(C, dtype=jnp.float32)
    rhs = jnp.concatenate([v_beta, k_beta * jnp.exp(gc)[..., None]], axis=-1)
    sol = lax.linalg.triangular_solve(A, rhs, left_side=True, lower=True, unit_diagonal=True)
    u = sol[..., :Dv]
    w = sol[..., Dv:]
    attn = jnp.where(causal, jnp.einsum('bhnid,bhnjd->bhnij', q, k) * decay, 0.0)
    q_dec = q * jnp.exp(gc)[..., None]
    k_dec = k * jnp.exp(gc[..., -1:] - gc)[..., None]
    g_last = jnp.exp(gc[..., -1])

    def step(state, xs):
        u_i, w_i, attn_i, qd_i, kd_i, gl_i = xs
        v_new = u_i - jnp.einsum('bhck,bhkv->bhcv', w_i, state)
        o = jnp.einsum('bhck,bhkv->bhcv', qd_i, state) + jnp.einsum('bhij,bhjv->bhiv', attn_i, v_new)
        state = state * gl_i[..., None, None] + jnp.einsum('bhck,bhcv->bhkv', kd_i, v_new)
        return state, o

    xs = tuple(jnp.moveaxis(t, 2, 0) for t in (u, w, attn, q_dec, k_dec, g_last))
    s0 = jnp.zeros((B, H, Dk, Dv), jnp.float32)
    _, o = lax.scan(step, s0, xs)
    return jnp.moveaxis(o, 0, 2).reshape(B, H, S, Dv)


def gated_deltanet(h, w_in, conv_w, a_log, dt_bias, norm_g, w_out):
    B, S, _ = h.shape
    H, Dh, W = DN_HEADS, DN_HEAD_DIM, DN_WIDTH
    f32 = jnp.float32
    proj = h @ w_in
    qkv = proj[..., :3 * W]
    z = proj[..., 3 * W:4 * W]
    b_raw = proj[..., 4 * W:4 * W + H]
    a_raw = proj[..., 4 * W + H:]
    qkv = jax.nn.silu(causal_short_conv(qkv, conv_w))
    q, k, v = jnp.split(qkv, 3, axis=-1)
    to_heads = lambda t: t.reshape(B, S, H, Dh).transpose(0, 2, 1, 3).astype(f32)
    q = l2norm(to_heads(q))
    k = l2norm(to_heads(k))
    v = to_heads(v)
    beta = jax.nn.sigmoid(b_raw.astype(f32)).transpose(0, 2, 1)
    g = (-jnp.exp(a_log.astype(f32)) *
         jax.nn.softplus(a_raw.astype(f32) + dt_bias.astype(f32))).transpose(0, 2, 1)
    o = gated_delta_rule(q, k, v, g, beta).transpose(0, 2, 1, 3)
    o = rmsnorm(o, norm_g) * jax.nn.silu(z.reshape(B, S, H, Dh).astype(f32))
    return o.reshape(B, S, W).astype(h.dtype) @ w_out


def spatial_gating(h, w_in, b_in, ln_g, ln_b, w_s, b_s, w_out):
    B, S, _ = h.shape
    E, G, C = SG_WIDTH, SG_GROUPS, SG_CHUNK
    N = S // C
    zz = jax.nn.gelu(h @ w_in + b_in, approximate=False)
    u = zz[..., :E]
    v = layernorm(zz[..., E:], ln_g, ln_b)
    mask = jnp.tril(jnp.ones((C, C), dtype=bool))
    w_c = jnp.where(mask, w_s, 0.0)
    vg = v.reshape(B, N, C, G, E // G)
    mixed = jnp.einsum('gts,bnsgc->bntgc', w_c, vg) + b_s.T[None, None, :, :, None]
    return (u * mixed.reshape(B, S, E)) @ w_out


def _fwd_setup_inputs(seed: int = 0) -> dict:
    key = jax.random.key(seed)
    ks = jax.random.split(key, 20)
    D, F, H, Dh, W = D_MODEL, D_FF, DN_HEADS, DN_HEAD_DIM, DN_WIDTH
    E, G, C = SG_WIDTH, SG_GROUPS, SG_CHUNK
    nrm = jax.random.normal
    x = nrm(ks[0], (BATCH, SEQ, D), jnp.float32)
    norm_g = 1.0 + 0.02 * nrm(ks[1], (DEPTH, 6, D), jnp.float32)
    ffn_w_gate = nrm(ks[2], (DEPTH, 2, D, F), jnp.float32) * D ** -0.5
    ffn_w_up = nrm(ks[3], (DEPTH, 2, D, F), jnp.float32) * D ** -0.5
    ffn_w_down = nrm(ks[4], (DEPTH, 2, F, D), jnp.float32) * F ** -0.5
    dn_w_in = nrm(ks[5], (N_A, D, 4 * W + 2 * H), jnp.float32) * D ** -0.5
    dn_conv_w = nrm(ks[6], (N_A, DN_CONV, 3 * W), jnp.float32) * DN_CONV ** -0.5
    dn_a_log = jnp.log(jax.random.uniform(ks[7], (N_A, H), jnp.float32, minval=1.0, maxval=16.0))
    dt = jnp.exp(jax.random.uniform(ks[8], (N_A, H), jnp.float32,
                                    minval=math.log(1e-3), maxval=math.log(1e-1)))
    dn_dt_bias = dt + jnp.log(-jnp.expm1(-dt))
    dn_norm_g = 1.0 + 0.02 * nrm(ks[9], (N_A, Dh), jnp.float32)
    dn_w_out = nrm(ks[10], (N_A, W, D), jnp.float32) * W ** -0.5
    sg_w_in = nrm(ks[11], (N_B, D, 2 * E), jnp.float32) * D ** -0.5
    sg_b_in = 0.02 * nrm(ks[12], (N_B, 2 * E), jnp.float32)
    sg_ln_g = 1.0 + 0.02 * nrm(ks[13], (N_B, E), jnp.float32)
    sg_ln_b = 0.02 * nrm(ks[14], (N_B, E), jnp.float32)
    sg_w_s = nrm(ks[15], (N_B, G, C, C), jnp.float32) * C ** -0.5
    sg_b_s = 1.0 + 0.02 * nrm(ks[16], (N_B, G, C), jnp.float32)
    sg_w_out = nrm(ks[17], (N_B, E, D), jnp.float32) * E ** -0.5
    return {'x': x, 'norm_g': norm_g, 'ffn_w_gate': ffn_w_gate, 'ffn_w_up': ffn_w_up,
            'ffn_w_down': ffn_w_down, 'dn_w_in': dn_w_in, 'dn_conv_w': dn_conv_w,
            'dn_a_log': dn_a_log, 'dn_dt_bias': dn_dt_bias, 'dn_norm_g': dn_norm_g,
            'dn_w_out': dn_w_out, 'sg_w_in': sg_w_in, 'sg_b_in': sg_b_in, 'sg_ln_g': sg_ln_g,
            'sg_ln_b': sg_ln_b, 'sg_w_s': sg_w_s, 'sg_b_s': sg_b_s, 'sg_w_out': sg_w_out}


def _fwd_reference(x, norm_g, ffn_w_gate, ffn_w_up, ffn_w_down, dn_w_in, dn_conv_w, dn_a_log,
              dn_dt_bias, dn_norm_g, dn_w_out, sg_w_in, sg_b_in, sg_ln_g, sg_ln_b, sg_w_s,
              sg_b_s, sg_w_out):
    for i in range(DEPTH):
        ng = norm_g[i]
        h = rmsnorm(x, ng[0])
        x = x + 0.5 * rmsnorm(swiglu(h, ffn_w_gate[i, 0], ffn_w_up[i, 0], ffn_w_down[i, 0]), ng[1])
        h = rmsnorm(x, ng[2])
        j = i // N_MIXERS
        if i % N_MIXERS == 0:
            m = gated_deltanet(h, dn_w_in[j], dn_conv_w[j], dn_a_log[j], dn_dt_bias[j],
                               dn_norm_g[j], dn_w_out[j])
        else:
            m = spatial_gating(h, sg_w_in[j], sg_b_in[j], sg_ln_g[j], sg_ln_b[j], sg_w_s[j],
                               sg_b_s[j], sg_w_out[j])
        x = x + rmsnorm(m, ng[3])
        h = rmsnorm(x, ng[4])
        x = x + 0.5 * rmsnorm(swiglu(h, ffn_w_gate[i, 1], ffn_w_up[i, 1], ffn_w_down[i, 1]), ng[5])
    return x


import jax as _jax
import jax.numpy as _jnp

TWIN_FORMAT = 'train_step'
FWD_PARAMS = ['x', 'norm_g', 'ffn_w_gate', 'ffn_w_up', 'ffn_w_down', 'dn_w_in', 'dn_conv_w', 'dn_a_log', 'dn_dt_bias', 'dn_norm_g', 'dn_w_out', 'sg_w_in', 'sg_b_in', 'sg_ln_g', 'sg_ln_b', 'sg_w_s', 'sg_b_s', 'sg_w_out']
TWIN_WEIGHTS = ['norm_g', 'ffn_w_gate', 'ffn_w_up', 'ffn_w_down', 'dn_w_in', 'dn_conv_w', 'dn_a_log', 'dn_dt_bias', 'dn_norm_g', 'dn_w_out', 'sg_w_in', 'sg_b_in', 'sg_ln_g', 'sg_ln_b', 'sg_w_s', 'sg_b_s', 'sg_w_out']
TWIN_DIFF_INPUT = 'x'
TWIN_INPUTS = ['x', 'norm_g', 'ffn_w_gate', 'ffn_w_up', 'ffn_w_down', 'dn_w_in', 'dn_conv_w', 'dn_a_log', 'dn_dt_bias', 'dn_norm_g', 'dn_w_out', 'sg_w_in', 'sg_b_in', 'sg_ln_g', 'sg_ln_b', 'sg_w_s', 'sg_b_s', 'sg_w_out', 'loss_target', 'm_norm_g', 'm_ffn_w_gate', 'm_ffn_w_up', 'm_ffn_w_down', 'm_dn_w_in', 'm_dn_conv_w', 'm_dn_a_log', 'm_dn_dt_bias', 'm_dn_norm_g', 'm_dn_w_out', 'm_sg_w_in', 'm_sg_b_in', 'm_sg_ln_g', 'm_sg_ln_b', 'm_sg_w_s', 'm_sg_b_s', 'm_sg_w_out', 'v_norm_g', 'v_ffn_w_gate', 'v_ffn_w_up', 'v_ffn_w_down', 'v_dn_w_in', 'v_dn_conv_w', 'v_dn_a_log', 'v_dn_dt_bias', 'v_dn_norm_g', 'v_dn_w_out', 'v_sg_w_in', 'v_sg_b_in', 'v_sg_ln_g', 'v_sg_ln_b', 'v_sg_w_s', 'v_sg_b_s', 'v_sg_w_out']
TWIN_OUTPUTS = ['loss', 'grad_x', 'grad_norm_g', 'grad_ffn_w_gate', 'grad_ffn_w_up', 'grad_ffn_w_down', 'grad_dn_w_in', 'grad_dn_conv_w', 'grad_dn_a_log', 'grad_dn_dt_bias', 'grad_dn_norm_g', 'grad_dn_w_out', 'grad_sg_w_in', 'grad_sg_b_in', 'grad_sg_ln_g', 'grad_sg_ln_b', 'grad_sg_w_s', 'grad_sg_b_s', 'grad_sg_w_out', 'delta_norm_g', 'delta_ffn_w_gate', 'delta_ffn_w_up', 'delta_ffn_w_down', 'delta_dn_w_in', 'delta_dn_conv_w', 'delta_dn_a_log', 'delta_dn_dt_bias', 'delta_dn_norm_g', 'delta_dn_w_out', 'delta_sg_w_in', 'delta_sg_b_in', 'delta_sg_ln_g', 'delta_sg_ln_b', 'delta_sg_w_s', 'delta_sg_b_s', 'delta_sg_w_out', 'new_m_norm_g', 'new_m_ffn_w_gate', 'new_m_ffn_w_up', 'new_m_ffn_w_down', 'new_m_dn_w_in', 'new_m_dn_conv_w', 'new_m_dn_a_log', 'new_m_dn_dt_bias', 'new_m_dn_norm_g', 'new_m_dn_w_out', 'new_m_sg_w_in', 'new_m_sg_b_in', 'new_m_sg_ln_g', 'new_m_sg_ln_b', 'new_m_sg_w_s', 'new_m_sg_b_s', 'new_m_sg_w_out', 'new_v_norm_g', 'new_v_ffn_w_gate', 'new_v_ffn_w_up', 'new_v_ffn_w_down', 'new_v_dn_w_in', 'new_v_dn_conv_w', 'new_v_dn_a_log', 'new_v_dn_dt_bias', 'new_v_dn_norm_g', 'new_v_dn_w_out', 'new_v_sg_w_in', 'new_v_sg_b_in', 'new_v_sg_ln_g', 'new_v_sg_ln_b', 'new_v_sg_w_s', 'new_v_sg_b_s', 'new_v_sg_w_out']
TWIN_LEAF_KINDS = {'loss': 'loss', 'grad_x': 'grad_x', 'grad_norm_g': 'grad_w', 'grad_ffn_w_gate': 'grad_w', 'grad_ffn_w_up': 'grad_w', 'grad_ffn_w_down': 'grad_w', 'grad_dn_w_in': 'grad_w', 'grad_dn_conv_w': 'grad_w', 'grad_dn_a_log': 'grad_w', 'grad_dn_dt_bias': 'grad_w', 'grad_dn_norm_g': 'grad_w', 'grad_dn_w_out': 'grad_w', 'grad_sg_w_in': 'grad_w', 'grad_sg_b_in': 'grad_w', 'grad_sg_ln_g': 'grad_w', 'grad_sg_ln_b': 'grad_w', 'grad_sg_w_s': 'grad_w', 'grad_sg_b_s': 'grad_w', 'grad_sg_w_out': 'grad_w', 'delta_norm_g': 'delta_w', 'delta_ffn_w_gate': 'delta_w', 'delta_ffn_w_up': 'delta_w', 'delta_ffn_w_down': 'delta_w', 'delta_dn_w_in': 'delta_w', 'delta_dn_conv_w': 'delta_w', 'delta_dn_a_log': 'delta_w', 'delta_dn_dt_bias': 'delta_w', 'delta_dn_norm_g': 'delta_w', 'delta_dn_w_out': 'delta_w', 'delta_sg_w_in': 'delta_w', 'delta_sg_b_in': 'delta_w', 'delta_sg_ln_g': 'delta_w', 'delta_sg_ln_b': 'delta_w', 'delta_sg_w_s': 'delta_w', 'delta_sg_b_s': 'delta_w', 'delta_sg_w_out': 'delta_w', 'new_m_norm_g': 'new_m', 'new_m_ffn_w_gate': 'new_m', 'new_m_ffn_w_up': 'new_m', 'new_m_ffn_w_down': 'new_m', 'new_m_dn_w_in': 'new_m', 'new_m_dn_conv_w': 'new_m', 'new_m_dn_a_log': 'new_m', 'new_m_dn_dt_bias': 'new_m', 'new_m_dn_norm_g': 'new_m', 'new_m_dn_w_out': 'new_m', 'new_m_sg_w_in': 'new_m', 'new_m_sg_b_in': 'new_m', 'new_m_sg_ln_g': 'new_m', 'new_m_sg_ln_b': 'new_m', 'new_m_sg_w_s': 'new_m', 'new_m_sg_b_s': 'new_m', 'new_m_sg_w_out': 'new_m', 'new_v_norm_g': 'new_v', 'new_v_ffn_w_gate': 'new_v', 'new_v_ffn_w_up': 'new_v', 'new_v_ffn_w_down': 'new_v', 'new_v_dn_w_in': 'new_v', 'new_v_dn_conv_w': 'new_v', 'new_v_dn_a_log': 'new_v', 'new_v_dn_dt_bias': 'new_v', 'new_v_dn_norm_g': 'new_v', 'new_v_dn_w_out': 'new_v', 'new_v_sg_w_in': 'new_v', 'new_v_sg_b_in': 'new_v', 'new_v_sg_ln_g': 'new_v', 'new_v_sg_ln_b': 'new_v', 'new_v_sg_w_s': 'new_v', 'new_v_sg_b_s': 'new_v', 'new_v_sg_w_out': 'new_v'}


def _forward(args):
    return _fwd_reference(*[args[k] for k in FWD_PARAMS])


def _output_shape():
    def fwd():
        inp = _fwd_setup_inputs(0)
        return _fwd_reference(*[inp[k] for k in FWD_PARAMS])
    out = _jax.eval_shape(fwd)
    return out.shape, out.dtype

N_MICROBATCH = 1
ADAM_LR = 0.001
ADAM_B1 = 0.9
ADAM_B2 = 0.999
ADAM_EPS = 1e-08
ADAM_WD = 0.01
ADAM_STEP = 10
PER_EXAMPLE_BATCH_AXIS = {'x': 0, 'loss_target': 0}
SHARED_INPUTS = []
_WEIGHT_DTYPES = {'norm_g': _jnp.float32, 'ffn_w_gate': _jnp.float32, 'ffn_w_up': _jnp.float32, 'ffn_w_down': _jnp.float32, 'dn_w_in': _jnp.float32, 'dn_conv_w': _jnp.float32, 'dn_a_log': _jnp.float32, 'dn_dt_bias': _jnp.float32, 'dn_norm_g': _jnp.float32, 'dn_w_out': _jnp.float32, 'sg_w_in': _jnp.float32, 'sg_b_in': _jnp.float32, 'sg_ln_g': _jnp.float32, 'sg_ln_b': _jnp.float32, 'sg_w_s': _jnp.float32, 'sg_b_s': _jnp.float32, 'sg_w_out': _jnp.float32}
MOMENT_SCALE = {'norm_g': 2.794200e+01, 'ffn_w_gate': 3.898424e-01, 'ffn_w_up': 5.609837e-01, 'ffn_w_down': 9.385510e-01, 'dn_w_in': 8.376115e-01, 'dn_conv_w': 1.920625e+00, 'dn_a_log': 4.855548e+00, 'dn_dt_bias': 4.881904e+00, 'dn_norm_g': 1.700825e+01, 'dn_w_out': 5.486722e+00, 'sg_w_in': 1.111106e+00, 'sg_b_in': 4.787765e+00, 'sg_ln_g': 2.198947e-01, 'sg_ln_b': 2.356387e-01, 'sg_w_s': 3.023552e-01, 'sg_b_s': 5.273540e-01, 'sg_w_out': 5.949960e+00}


def _to_microbatches(a, axis):
    t = _jnp.moveaxis(a, axis, 0)
    t = t.reshape((N_MICROBATCH, t.shape[0] // N_MICROBATCH) + t.shape[1:])
    return _jnp.moveaxis(t, 1, axis + 1)


def setup_inputs(seed: int = 0) -> dict:
    inp = _fwd_setup_inputs(seed)
    key = _jax.random.fold_in(_jax.random.key(seed), 7919)
    shape, _ = _output_shape()
    out = dict(inp)
    out["loss_target"] = _jax.random.normal(_jax.random.fold_in(key, 0), shape, _jnp.float32)
    for i, name in enumerate(TWIN_WEIGHTS):
        w = inp[name].astype(_jnp.float32)
        if MOMENT_SCALE is None:
            s = _jnp.sqrt(_jnp.mean(_jnp.square(w)) + 1e-30)
        else:
            s = MOMENT_SCALE[name]
        km, kv = _jax.random.split(_jax.random.fold_in(key, i + 1))
        out[name] = w
        out["m_" + name] = s * _jax.random.normal(km, w.shape, _jnp.float32)
        out["v_" + name] = (s * s) * _jax.random.uniform(kv, w.shape, _jnp.float32, 0.5, 1.5)
    if N_MICROBATCH > 1:
        for name, axis in PER_EXAMPLE_BATCH_AXIS.items():
            out[name] = _to_microbatches(out[name], axis)
    return {'x': out['x'], 'norm_g': out['norm_g'], 'ffn_w_gate': out['ffn_w_gate'], 'ffn_w_up': out['ffn_w_up'], 'ffn_w_down': out['ffn_w_down'], 'dn_w_in': out['dn_w_in'], 'dn_conv_w': out['dn_conv_w'], 'dn_a_log': out['dn_a_log'], 'dn_dt_bias': out['dn_dt_bias'], 'dn_norm_g': out['dn_norm_g'], 'dn_w_out': out['dn_w_out'], 'sg_w_in': out['sg_w_in'], 'sg_b_in': out['sg_b_in'], 'sg_ln_g': out['sg_ln_g'], 'sg_ln_b': out['sg_ln_b'], 'sg_w_s': out['sg_w_s'], 'sg_b_s': out['sg_b_s'], 'sg_w_out': out['sg_w_out'], 'loss_target': out['loss_target'], 'm_norm_g': out['m_norm_g'], 'm_ffn_w_gate': out['m_ffn_w_gate'], 'm_ffn_w_up': out['m_ffn_w_up'], 'm_ffn_w_down': out['m_ffn_w_down'], 'm_dn_w_in': out['m_dn_w_in'], 'm_dn_conv_w': out['m_dn_conv_w'], 'm_dn_a_log': out['m_dn_a_log'], 'm_dn_dt_bias': out['m_dn_dt_bias'], 'm_dn_norm_g': out['m_dn_norm_g'], 'm_dn_w_out': out['m_dn_w_out'], 'm_sg_w_in': out['m_sg_w_in'], 'm_sg_b_in': out['m_sg_b_in'], 'm_sg_ln_g': out['m_sg_ln_g'], 'm_sg_ln_b': out['m_sg_ln_b'], 'm_sg_w_s': out['m_sg_w_s'], 'm_sg_b_s': out['m_sg_b_s'], 'm_sg_w_out': out['m_sg_w_out'], 'v_norm_g': out['v_norm_g'], 'v_ffn_w_gate': out['v_ffn_w_gate'], 'v_ffn_w_up': out['v_ffn_w_up'], 'v_ffn_w_down': out['v_ffn_w_down'], 'v_dn_w_in': out['v_dn_w_in'], 'v_dn_conv_w': out['v_dn_conv_w'], 'v_dn_a_log': out['v_dn_a_log'], 'v_dn_dt_bias': out['v_dn_dt_bias'], 'v_dn_norm_g': out['v_dn_norm_g'], 'v_dn_w_out': out['v_dn_w_out'], 'v_sg_w_in': out['v_sg_w_in'], 'v_sg_b_in': out['v_sg_b_in'], 'v_sg_ln_g': out['v_sg_ln_g'], 'v_sg_ln_b': out['v_sg_ln_b'], 'v_sg_w_s': out['v_sg_w_s'], 'v_sg_b_s': out['v_sg_b_s'], 'v_sg_w_out': out['v_sg_w_out']}


def _loss(weights, diff, rest, loss_target):
    with _jax.named_scope("forward"):
        args = {**rest, TWIN_DIFF_INPUT: diff, **{k: w.astype(_WEIGHT_DTYPES[k]) for k, w in weights.items()}}
        y = _forward(args)
    with _jax.named_scope("loss_head"):
        err = _jnp.square(y.astype(_jnp.float32) - loss_target)
        return 0.5 * _jnp.sum(_jnp.mean(err, axis=-1)) if err.ndim else 0.5 * err


def _adamw(w, g, m, v):
    m = ADAM_B1 * m + (1.0 - ADAM_B1) * g
    v = ADAM_B2 * v + (1.0 - ADAM_B2) * _jnp.square(g)
    m_hat = m / (1.0 - ADAM_B1 ** ADAM_STEP)
    v_hat = v / (1.0 - ADAM_B2 ** ADAM_STEP)
    delta = -ADAM_LR * (m_hat / (_jnp.sqrt(v_hat) + ADAM_EPS) + ADAM_WD * w)
    return delta, m, v


def reference(x, norm_g, ffn_w_gate, ffn_w_up, ffn_w_down, dn_w_in, dn_conv_w, dn_a_log, dn_dt_bias, dn_norm_g, dn_w_out, sg_w_in, sg_b_in, sg_ln_g, sg_ln_b, sg_w_s, sg_b_s, sg_w_out, loss_target, m_norm_g, m_ffn_w_gate, m_ffn_w_up, m_ffn_w_down, m_dn_w_in, m_dn_conv_w, m_dn_a_log, m_dn_dt_bias, m_dn_norm_g, m_dn_w_out, m_sg_w_in, m_sg_b_in, m_sg_ln_g, m_sg_ln_b, m_sg_w_s, m_sg_b_s, m_sg_w_out, v_norm_g, v_ffn_w_gate, v_ffn_w_up, v_ffn_w_down, v_dn_w_in, v_dn_conv_w, v_dn_a_log, v_dn_dt_bias, v_dn_norm_g, v_dn_w_out, v_sg_w_in, v_sg_b_in, v_sg_ln_g, v_sg_ln_b, v_sg_w_s, v_sg_b_s, v_sg_w_out):
    given = dict(x=x, norm_g=norm_g, ffn_w_gate=ffn_w_gate, ffn_w_up=ffn_w_up, ffn_w_down=ffn_w_down, dn_w_in=dn_w_in, dn_conv_w=dn_conv_w, dn_a_log=dn_a_log, dn_dt_bias=dn_dt_bias, dn_norm_g=dn_norm_g, dn_w_out=dn_w_out, sg_w_in=sg_w_in, sg_b_in=sg_b_in, sg_ln_g=sg_ln_g, sg_ln_b=sg_ln_b, sg_w_s=sg_w_s, sg_b_s=sg_b_s, sg_w_out=sg_w_out, loss_target=loss_target, m_norm_g=m_norm_g, m_ffn_w_gate=m_ffn_w_gate, m_ffn_w_up=m_ffn_w_up, m_ffn_w_down=m_ffn_w_down, m_dn_w_in=m_dn_w_in, m_dn_conv_w=m_dn_conv_w, m_dn_a_log=m_dn_a_log, m_dn_dt_bias=m_dn_dt_bias, m_dn_norm_g=m_dn_norm_g, m_dn_w_out=m_dn_w_out, m_sg_w_in=m_sg_w_in, m_sg_b_in=m_sg_b_in, m_sg_ln_g=m_sg_ln_g, m_sg_ln_b=m_sg_ln_b, m_sg_w_s=m_sg_w_s, m_sg_b_s=m_sg_b_s, m_sg_w_out=m_sg_w_out, v_norm_g=v_norm_g, v_ffn_w_gate=v_ffn_w_gate, v_ffn_w_up=v_ffn_w_up, v_ffn_w_down=v_ffn_w_down, v_dn_w_in=v_dn_w_in, v_dn_conv_w=v_dn_conv_w, v_dn_a_log=v_dn_a_log, v_dn_dt_bias=v_dn_dt_bias, v_dn_norm_g=v_dn_norm_g, v_dn_w_out=v_dn_w_out, v_sg_w_in=v_sg_w_in, v_sg_b_in=v_sg_b_in, v_sg_ln_g=v_sg_ln_g, v_sg_ln_b=v_sg_ln_b, v_sg_w_s=v_sg_w_s, v_sg_b_s=v_sg_b_s, v_sg_w_out=v_sg_w_out)
    weights = {n: given[n] for n in TWIN_WEIGHTS}
    shared = {n: given[n] for n in SHARED_INPUTS}
    per_example = {n: given[n] for n in ['x']}
    grad_fn = _jax.value_and_grad(_loss, argnums=(0, 1))

    def one_microbatch(ex, loss_target):
        ex = dict(ex)
        diff = ex.pop(TWIN_DIFF_INPUT)
        return grad_fn(weights, diff, {**shared, **ex}, loss_target)

    if N_MICROBATCH == 1:
        loss, (grad_w, grad_x) = one_microbatch(per_example, given["loss_target"])
    else:
        def body(carry, xs):
            loss_sum, grad_sum = carry
            l_k, (gw_k, gx_k) = one_microbatch(xs[0], xs[1])
            with _jax.named_scope("update"):
                return (loss_sum + l_k, _jax.tree.map(_jnp.add, grad_sum, gw_k)), gx_k

        init = (_jnp.zeros((), _jnp.float32), _jax.tree.map(_jnp.zeros_like, weights))
        (loss, grad_w), grad_x = _jax.lax.scan(body, init, (per_example, given["loss_target"]))
    with _jax.named_scope("update"):
        delta_w, new_m, new_v = {}, {}, {}
        for n in TWIN_WEIGHTS:
            delta_w[n], new_m[n], new_v[n] = _adamw(weights[n], grad_w[n], given["m_" + n], given["v_" + n])
    return (loss, grad_x, *[grad_w[n] for n in TWIN_WEIGHTS], *[delta_w[n] for n in TWIN_WEIGHTS],
            *[new_m[n] for n in TWIN_WEIGHTS], *[new_v[n] for n in TWIN_WEIGHTS])
```

```python
import functools
import math

import jax
import jax.numpy as jnp
from jax import lax
from jax.experimental import pallas as pl
from jax.experimental.pallas import tpu as pltpu

f32 = jnp.float32
MXU_DTYPE = jnp.bfloat16
N_DEV = 8
RMS_EPS = 1e-6
LN_EPS = 1e-5
L2_EPS = 1e-6
HEAD = 128
DN_CHUNK = 64
SG_CHUNK = 128
SG_GROUPS = 8
CONV_K = 4
ADAM_LR, ADAM_B1, ADAM_B2, ADAM_EPS, ADAM_WD, ADAM_STEP = 0.001, 0.9, 0.999, 1e-08, 0.01, 10
VMEM_LIMIT = 56 * 1024 * 1024
SDS = jax.ShapeDtypeStruct
HIGHEST = lax.Precision.HIGHEST
MESH = pl.DeviceIdType.MESH


def _params(n_grid):
    return pltpu.CompilerParams(dimension_semantics=("arbitrary",) * n_grid, vmem_limit_bytes=VMEM_LIMIT)


def _row_tile(s, want):
    t = min(s, want)
    assert s % t == 0, (s, t)
    return t


def _rms(x, g):
    return x * lax.rsqrt(jnp.mean(x * x, axis=-1, keepdims=True) + RMS_EPS) * g


def _rms_bwd(x, g, dy):
    _, vjp = jax.vjp(_rms, x, g)
    return vjp(dy)


def _silu(a):
    return a * jax.nn.sigmoid(a)


def _gelu(x):
    return 0.5 * x * (1.0 + lax.erf(x * 0.7071067811865476))


def _mm(a, b):
    return lax.dot_general(a, b, (((1,), (0,)), ((), ())), preferred_element_type=f32)


def _mm_nt(a, b):
    return lax.dot_general(a, b, (((1,), (1,)), ((), ())), preferred_element_type=f32)


def _mm_tn(a, b):
    return lax.dot_general(a, b, (((0,), (0,)), ((), ())), preferred_element_type=f32)


def _c(x):
    return x.astype(MXU_DTYPE)


def _hp(a, b, dims):
    return lax.dot_general(a, b, (dims, ((), ())), preferred_element_type=f32, precision=HIGHEST)


def _split(a):
    hi = a.astype(MXU_DTYPE)
    lo = (a - hi.astype(f32)).astype(MXU_DTYPE)
    return hi, lo


def _dot3(a, b, dims):
    ah, al = _split(a)
    bh, bl = _split(b)
    d = lambda p, q: lax.dot_general(p, q, (dims, ((), ())), preferred_element_type=f32)
    return d(ah, bh) + (d(ah, bl) + d(al, bh))


NN, NT, TN = ((1,), (0,)), ((1,), (1,)), ((0,), (0,))


def _slot(px, py, pc):
    return 4 * px + 2 * py + pc


def all_gather_multi(arrs, name):
    n = len(arrs)

    def body(*refs):
        ins, outs = refs[:n], refs[n:2 * n]
        send_sems, recv_sems, local_sems = refs[2 * n:]
        x, y, c = lax.axis_index("x"), lax.axis_index("y"), lax.axis_index("c")
        me, sibling = (x, y, c), (x, y, 1 - c)
        chips = [(1 - x, y), (x, 1 - y), (1 - x, 1 - y)]

        def copy(a, k, block, to, src=None):
            dst = outs[a].at[_slot(*block)]
            return pltpu.make_async_remote_copy(
                src_ref=dst if src is None else src, dst_ref=dst, send_sem=send_sems.at[a, k],
                recv_sem=recv_sems.at[a, k], device_id=to, device_id_type=MESH)

        started = []
        for a in range(n):
            mine = pltpu.make_async_copy(ins[a], outs[a].at[_slot(*me)], local_sems.at[a])
            mine.start()
            started.append(mine)
        sends = []
        for a in range(n):
            first = [copy(a, 0, me, sibling, src=ins[a])]
            first += [copy(a, 1 + j, me, (*chip, c), src=ins[a]) for j, chip in enumerate(chips)]
            for cp in first:
                cp.start()
            sends += first
        for a in range(n):
            for j, chip in enumerate(chips):
                copy(a, 1 + j, (*chip, c), me).wait_recv()
                fwd = copy(a, 4 + j, (*chip, c), sibling)
                fwd.start()
                sends.append(fwd)
        for a in range(n):
            copy(a, 0, sibling, me).wait_recv()
            for j, chip in enumerate(chips):
                copy(a, 4 + j, (*chip, 1 - c), me).wait_recv()
        for cp in sends:
            cp.wait_send()
        for mine in started:
            mine.wait()

    hbm = pl.BlockSpec(memory_space=pltpu.HBM)
    return pl.pallas_call(
        body, name=name,
        out_shape=tuple(SDS((N_DEV,) + a.shape, a.dtype) for a in arrs),
        in_specs=[hbm] * n, out_specs=tuple([hbm] * n),
        scratch_shapes=[pltpu.SemaphoreType.DMA((n, 7)), pltpu.SemaphoreType.DMA((n, 7)),
                        pltpu.SemaphoreType.DMA((n,))],
    )(*arrs)


def exchange_slots(arrs, name):
    n = len(arrs)

    def body(*refs):
        ins, outs = refs[:n], refs[n:2 * n]
        send_sems, recv_sems, local_sems = refs[2 * n:]
        x, y, c = lax.axis_index("x"), lax.axis_index("y"), lax.axis_index("c")
        me = _slot(x, y, c)
        peers = [(x ^ (k >> 2), y ^ ((k >> 1) & 1), c ^ (k & 1)) for k in range(1, N_DEV)]

        def copy(a, k):
            peer = peers[k - 1]
            return pltpu.make_async_remote_copy(
                src_ref=ins[a].at[_slot(*peer)], dst_ref=outs[a].at[me], send_sem=send_sems.at[a, k - 1],
                recv_sem=recv_sems.at[a, k - 1], device_id=peer, device_id_type=MESH)

        def landed(a, k):
            peer = peers[k - 1]
            return pltpu.make_async_remote_copy(
                src_ref=ins[a].at[me], dst_ref=outs[a].at[_slot(*peer)], send_sem=send_sems.at[a, k - 1],
                recv_sem=recv_sems.at[a, k - 1], device_id=peer, device_id_type=MESH)

        local = []
        for a in range(n):
            cp = pltpu.make_async_copy(ins[a].at[me], outs[a].at[me], local_sems.at[a])
            cp.start()
            local.append(cp)
        order = [6, 7, 2, 3, 4, 5, 1]
        for a in range(n):
            for k in order:
                copy(a, k).start()
        for a in range(n):
            for k in order:
                copy(a, k).wait_send()
                landed(a, k).wait_recv()
        for cp in local:
            cp.wait()

    hbm = pl.BlockSpec(memory_space=pltpu.HBM)
    return pl.pallas_call(
        body, name=name,
        out_shape=tuple(SDS(a.shape, a.dtype) for a in arrs),
        in_specs=[hbm] * n, out_specs=tuple([hbm] * n),
        scratch_shapes=[pltpu.SemaphoreType.DMA((n, 7)), pltpu.SemaphoreType.DMA((n, 7)),
                        pltpu.SemaphoreType.DMA((n,))],
    )(*arrs)


def ffn_fwd(x, gpre, gpost, wg, wu, wd, li, ab, name):
    S, D = x.shape
    F8 = wg.shape[-1]
    tm = _row_tile(S, 512)
    nj = N_DEV

    def body(x_ref, gpre_ref, gpost_ref, wg_ref, wu_ref, wd_ref, xo_ref, h_ref, a_ref, b_ref, y_ref):
        j = pl.program_id(1)

        @pl.when(j == 0)
        def _():
            h_ref[...] = _rms(x_ref[...], gpre_ref[...]).astype(h_ref.dtype)
            y_ref[...] = jnp.zeros_like(y_ref)

        h = h_ref[...]
        a = _mm(h, wg_ref[...]).astype(a_ref.dtype)
        b = _mm(h, wu_ref[...]).astype(b_ref.dtype)
        a_ref[...] = a
        b_ref[...] = b
        t = _silu(a.astype(f32)) * b.astype(f32)
        y_ref[...] += _mm(_c(t), wd_ref[...])

        @pl.when(j == nj - 1)
        def _():
            xo_ref[...] = x_ref[...] + 0.5 * _rms(y_ref[...], gpost_ref[...])

    row = pl.BlockSpec((tm, D), lambda i, j: (i, 0))
    vec = pl.BlockSpec((1, D), lambda i, j: (0, 0))
    wcol = pl.BlockSpec((None, None, None, D, F8), lambda i, j: (j, li, ab, 0, 0))
    wrow = pl.BlockSpec((None, None, None, F8, D), lambda i, j: (j, li, ab, 0, 0))
    hid = pl.BlockSpec((None, tm, F8), lambda i, j: (j, i, 0))
    return pl.pallas_call(
        body, name=name, grid=(S // tm, nj),
        in_specs=[row, vec, vec, wcol, wcol, wrow],
        out_specs=(row, row, hid, hid, row),
        out_shape=(SDS((S, D), f32), SDS((S, D), MXU_DTYPE), SDS((N_DEV, S, F8), MXU_DTYPE),
                   SDS((N_DEV, S, F8), MXU_DTYPE), SDS((S, D), f32)),
        compiler_params=_params(2),
    )(x, gpre, gpost, wg, wu, wd)


def ffn_bwd_dx(dxo, x, y, a, b, gpre, gpost, wg, wu, wd, li, ab, name):
    S, D = x.shape
    F8 = wg.shape[-1]
    tm = _row_tile(S, 512)
    nj = N_DEV

    def body(dxo_ref, x_ref, y_ref, a_ref, b_ref, gpre_ref, gpost_ref, wg_ref, wu_ref, wd_ref,
             dx_ref, da_ref, db_ref, dy_ref, dgpre_ref, dgpost_ref, dh_ref):
        i, j = pl.program_id(0), pl.program_id(1)

        @pl.when(j == 0)
        def _():
            @pl.when(i == 0)
            def _():
                dgpre_ref[...] = jnp.zeros_like(dgpre_ref)
                dgpost_ref[...] = jnp.zeros_like(dgpost_ref)

            dy, dg = _rms_bwd(y_ref[...], gpost_ref[...], 0.5 * dxo_ref[...])
            dy_ref[...] = dy.astype(dy_ref.dtype)
            dgpost_ref[...] += dg
            dh_ref[...] = jnp.zeros_like(dh_ref)

        dt = _mm_nt(dy_ref[...], wd_ref[...])
        af, bf = a_ref[...].astype(f32), b_ref[...].astype(f32)
        s = jax.nn.sigmoid(af)
        da = (dt * bf * (s * (1.0 + af * (1.0 - s)))).astype(da_ref.dtype)
        db = (dt * (af * s)).astype(db_ref.dtype)
        da_ref[...] = da
        db_ref[...] = db
        dh_ref[...] += _mm_nt(da, wg_ref[...]) + _mm_nt(db, wu_ref[...])

        @pl.when(j == nj - 1)
        def _():
            dxx, dg = _rms_bwd(x_ref[...], gpre_ref[...], dh_ref[...])
            dx_ref[...] = dxo_ref[...] + dxx
            dgpre_ref[...] += dg

    row = pl.BlockSpec((tm, D), lambda i, j: (i, 0))
    vec = pl.BlockSpec((1, D), lambda i, j: (0, 0))
    wcol = pl.BlockSpec((None, None, None, D, F8), lambda i, j: (j, li, ab, 0, 0))
    wrow = pl.BlockSpec((None, None, None, F8, D), lambda i, j: (j, li, ab, 0, 0))
    hid = pl.BlockSpec((None, tm, F8), lambda i, j: (j, i, 0))
    return pl.pallas_call(
        body, name=name, grid=(S // tm, nj),
        in_specs=[row, row, row, hid, hid, vec, vec, wcol, wcol, wrow],
        out_specs=(row, hid, hid, row, vec, vec),
        out_shape=(SDS((S, D), f32), SDS((N_DEV, S, F8), MXU_DTYPE), SDS((N_DEV, S, F8), MXU_DTYPE),
                   SDS((S, D), MXU_DTYPE), SDS((1, D), f32), SDS((1, D), f32)),
        scratch_shapes=[pltpu.VMEM((tm, D), f32)],
        compiler_params=_params(2),
    )(dxo, x, y, a, b, gpre, gpost, wg, wu, wd)


def ffn_bwd_dw(h, dy, a, b, da, db, name):
    S, D = h.shape
    F8 = a.shape[-1]
    tm = _row_tile(S, 512)
    ni = S // tm

    def body(h_ref, dy_ref, a_ref, b_ref, da_ref, db_ref, dwg_ref, dwu_ref, dwd_ref, accg, accu, accd):
        i = pl.program_id(1)

        @pl.when(i == 0)
        def _():
            accg[...] = jnp.zeros_like(accg)
            accu[...] = jnp.zeros_like(accu)
            accd[...] = jnp.zeros_like(accd)

        t = _c(_silu(a_ref[...].astype(f32)) * b_ref[...].astype(f32))
        hh = h_ref[...]
        accg[...] += _mm_tn(hh, da_ref[...])
        accu[...] += _mm_tn(hh, db_ref[...])
        accd[...] += _mm_tn(t, dy_ref[...])

        @pl.when(i == ni - 1)
        def _():
            dwg_ref[...] = accg[...].astype(dwg_ref.dtype)
            dwu_ref[...] = accu[...].astype(dwu_ref.dtype)
            dwd_ref[...] = accd[...].astype(dwd_ref.dtype)

    row = pl.BlockSpec((tm, D), lambda j, i: (i, 0))
    hid = pl.BlockSpec((None, tm, F8), lambda j, i: (j, i, 0))
    wcol = pl.BlockSpec((None, D, F8), lambda j, i: (j, 0, 0))
    wrow = pl.BlockSpec((None, F8, D), lambda j, i: (j, 0, 0))
    return pl.pallas_call(
        body, name=name, grid=(N_DEV, ni),
        in_specs=[row, row, hid, hid, hid, hid],
        out_specs=(wcol, wcol, wrow),
        out_shape=(SDS((N_DEV, D, F8), MXU_DTYPE), SDS((N_DEV, D, F8), MXU_DTYPE), SDS((N_DEV, F8, D), MXU_DTYPE)),
        scratch_shapes=[pltpu.VMEM((D, F8), f32), pltpu.VMEM((D, F8), f32), pltpu.VMEM((F8, D), f32)],
        compiler_params=_params(2),
    )(h, dy, a, b, da, db)


def rms_mm(x, g, w, w2, name, tn=1024):
    S, D = x.shape
    N = w.shape[1]
    tm = _row_tile(S, 512)
    tn = _row_tile(N, tn)
    has2 = w2 is not None

    def body(*refs):
        if has2:
            x_ref, g_ref, w_ref, w2_ref, h_ref, o_ref, o2_ref = refs
        else:
            x_ref, g_ref, w_ref, h_ref, o_ref = refs
        j = pl.program_id(1)

        @pl.when(j == 0)
        def _():
            h = _rms(x_ref[...], g_ref[...]).astype(h_ref.dtype)
            h_ref[...] = h
            if has2:
                o2_ref[...] = _mm(h, w2_ref[...])

        o_ref[...] = _mm(h_ref[...], w_ref[...])

    row = pl.BlockSpec((tm, D), lambda i, j: (i, 0))
    in_specs = [row, pl.BlockSpec((1, D), lambda i, j: (0, 0)), pl.BlockSpec((D, tn), lambda i, j: (0, j))]
    out_specs = [row, pl.BlockSpec((tm, tn), lambda i, j: (i, j))]
    out_shape = [SDS((S, D), MXU_DTYPE), SDS((S, N), f32)]
    args = [x, g, w]
    if has2:
        in_specs.append(pl.BlockSpec((D, w2.shape[1]), lambda i, j: (0, 0)))
        out_specs.append(pl.BlockSpec((tm, w2.shape[1]), lambda i, j: (i, 0)))
        out_shape.append(SDS((S, w2.shape[1]), f32))
        args.append(w2)
    return pl.pallas_call(
        body, name=name, grid=(S // tm, N // tn), in_specs=in_specs, out_specs=tuple(out_specs),
        out_shape=tuple(out_shape), compiler_params=_params(2),
    )(*args)


def mm_bwd_dx(dres, x, g, dy, w, dy2, w2, name, tk=1024):
    S, D = x.shape
    K = dy.shape[1]
    tm = _row_tile(S, 512)
    tk = _row_tile(K, tk)
    nk = K // tk
    has2 = dy2 is not None

    def body(*refs):
        if has2:
            dres_ref, x_ref, g_ref, dy_ref, w_ref, dy2_ref, w2_ref, dx_ref, dg_ref, dh_ref = refs
        else:
            dres_ref, x_ref, g_ref, dy_ref, w_ref, dx_ref, dg_ref, dh_ref = refs
        i, k = pl.program_id(0), pl.program_id(1)

        @pl.when(k == 0)
        def _():
            @pl.when(i == 0)
            def _():
                dg_ref[...] = jnp.zeros_like(dg_ref)

            if has2:
                dh_ref[...] = _mm_nt(dy2_ref[...], w2_ref[...])
            else:
                dh_ref[...] = jnp.zeros_like(dh_ref)

        dh_ref[...] += _mm_nt(dy_ref[...], w_ref[...])

        @pl.when(k == nk - 1)
        def _():
            dxx, dg = _rms_bwd(x_ref[...], g_ref[...], dh_ref[...])
            dx_ref[...] = dres_ref[...] + dxx
            dg_ref[...] += dg

    row = pl.BlockSpec((tm, D), lambda i, k: (i, 0))
    vec = pl.BlockSpec((1, D), lambda i, k: (0, 0))
    in_specs = [row, row, vec, pl.BlockSpec((tm, tk), lambda i, k: (i, k)), pl.BlockSpec((D, tk), lambda i, k: (0, k))]
    args = [dres, x, g, dy, w]
    if has2:
        in_specs += [pl.BlockSpec((tm, dy2.shape[1]), lambda i, k: (i, 0)),
                     pl.BlockSpec((D, w2.shape[1]), lambda i, k: (0, 0))]
        args += [dy2, w2]
    return pl.pallas_call(
        body, name=name, grid=(S // tm, nk), in_specs=in_specs, out_specs=(row, vec),
        out_shape=(SDS((S, D), f32), SDS((1, D), f32)),
        scratch_shapes=[pltpu.VMEM((tm, D), f32)], compiler_params=_params(2),
    )(*args)


def tn_mm(a, b, name, tn=512, slot_major=False):
    S, K1 = a.shape
    N = b.shape[1]
    tm = _row_tile(S, 512)
    tn = _row_tile(N, tn)
    ni = S // tm

    def body(a_ref, b_ref, o_ref, acc):
        i = pl.program_id(1)

        @pl.when(i == 0)
        def _():
            acc[...] = jnp.zeros_like(acc)

        acc[...] += _mm_tn(a_ref[...], b_ref[...])

        @pl.when(i == ni - 1)
        def _():
            o_ref[...] = acc[...].astype(o_ref.dtype)

    if slot_major:
        out_spec, out_shape = pl.BlockSpec((None, K1, tn), lambda j, i: (j, 0, 0)), SDS((N // tn, K1, tn), MXU_DTYPE)
    else:
        out_spec, out_shape = pl.BlockSpec((K1, tn), lambda j, i: (0, j)), SDS((K1, N), MXU_DTYPE)
    return pl.pallas_call(
        body, name=name, grid=(N // tn, ni),
        in_specs=[pl.BlockSpec((tm, K1), lambda j, i: (i, 0)), pl.BlockSpec((tm, tn), lambda j, i: (i, j))],
        out_specs=out_spec, out_shape=out_shape,
        scratch_shapes=[pltpu.VMEM((K1, tn), f32)], compiler_params=_params(2),
    )(a, b)


CONV_ROWS = 512


def _shift_down(cur, prev8, s):
    r = pltpu.roll(cur, s, 0)
    row = lax.broadcasted_iota(jnp.int32, (8, cur.shape[1]), 0)
    top = jnp.where(row < s, pltpu.roll(prev8, s, 0), r[0:8])
    return jnp.concatenate([top, r[8:]], axis=0)


def _shift_up(cur, next8, s):
    n = cur.shape[0]
    r = pltpu.roll(cur, n - s, 0)
    row = lax.broadcasted_iota(jnp.int32, (8, cur.shape[1]), 0)
    bot = jnp.where(row >= 8 - s, pltpu.roll(next8, 8 - s, 0), r[n - 8:])
    return jnp.concatenate([r[:n - 8], bot], axis=0)


def _conv_taps(cur, prev8):
    return [_shift_down(cur, prev8, 3), _shift_down(cur, prev8, 2), _shift_down(cur, prev8, 1), cur]


def _act_qk(c):
    a = _silu(c)
    return a * lax.rsqrt(jnp.sum(a * a, axis=-1, keepdims=True) + L2_EPS)


def dn_prep(proj, conv_w, name):
    S = proj.shape[0]
    W = conv_w.shape[1] // 3
    nh = W // HEAD
    R = _row_tile(S, CONV_ROWS)

    def body(p_ref, w_ref, o_ref):
        j = pl.program_id(0)
        w = w_ref[...]

        def rows(r, prev8):
            cur = p_ref[pl.ds(r, R), :]
            taps = _conv_taps(cur, prev8)
            cv = taps[0] * w[0:1] + taps[1] * w[1:2] + taps[2] * w[2:3] + taps[3] * w[3:4]

            @pl.when(j < 2 * nh)
            def _():
                o_ref[pl.ds(r, R), :] = _act_qk(cv)

            @pl.when(j >= 2 * nh)
            def _():
                o_ref[pl.ds(r, R), :] = _silu(cv)

        rows(0, jnp.zeros((8, HEAD), f32))

        @pl.loop(1, S // R)
        def _(t):
            r = pl.multiple_of(t * R, R)
            rows(r, p_ref[pl.ds(r - 8, 8), :])

    return pl.pallas_call(
        body, name=name, grid=(3 * nh,),
        in_specs=[pl.BlockSpec((S, HEAD), lambda j: (0, j)), pl.BlockSpec((CONV_K, HEAD), lambda j: (0, j))],
        out_specs=pl.BlockSpec((None, S, HEAD), lambda j: (j // nh, 0, j % nh)),
        out_shape=SDS((3, S, W), f32), compiler_params=_params(1),
    )(proj, conv_w)


def dn_prep_bwd(proj, conv_w, dqkv, dz, name):
    S = proj.shape[0]
    W = conv_w.shape[1] // 3
    nh = W // HEAD
    nq = 3 * nh
    R = _row_tile(S, CONV_ROWS)
    nr = S // R

    def body(p_ref, w_ref, dq_ref, dz_ref, dp_ref, dw_ref, dc_ref):
        j = pl.program_id(0)

        @pl.when(j >= nq)
        def _():
            dp_ref[...] = dz_ref[...].astype(dp_ref.dtype)

        @pl.when(j < nq)
        def _():
            w = w_ref[...]
            dw_ref[...] = jnp.zeros_like(dw_ref)

            def rows(r, prev8):
                cur = p_ref[pl.ds(r, R), :]
                taps = _conv_taps(cur, prev8)
                cv = taps[0] * w[0:1] + taps[1] * w[1:2] + taps[2] * w[2:3] + taps[3] * w[3:4]
                dn = dq_ref[pl.ds(r, R), :]

                @pl.when(j < 2 * nh)
                def _():
                    dc_ref[pl.ds(r, R), :] = jax.vjp(_act_qk, cv)[1](dn)[0]

                @pl.when(j >= 2 * nh)
                def _():
                    dc_ref[pl.ds(r, R), :] = jax.vjp(_silu, cv)[1](dn)[0]

                dc = dc_ref[pl.ds(r, R), :]
                dw_ref[...] += jnp.concatenate(
                    [jnp.sum(dc * taps[q], axis=0, keepdims=True) for q in range(CONV_K)], axis=0)

            rows(0, jnp.zeros((8, HEAD), f32))

            @pl.loop(1, nr)
            def _(t):
                r = pl.multiple_of(t * R, R)
                rows(r, p_ref[pl.ds(r - 8, 8), :])

            def back(r, next8):
                dc = dc_ref[pl.ds(r, R), :]
                dx = dc * w[3:4]
                for s in (1, 2, 3):
                    dx = dx + _shift_up(dc, next8, s) * w[3 - s:4 - s]
                dp_ref[pl.ds(r, R), :] = dx.astype(dp_ref.dtype)

            @pl.loop(0, nr - 1)
            def _(t):
                r = pl.multiple_of(t * R, R)
                back(r, dc_ref[pl.ds(r + R, 8), :])

            back((nr - 1) * R, jnp.zeros((8, HEAD), f32))

    clamp = lambda j: jnp.minimum(j, nq - 1)
    return pl.pallas_call(
        body, name=name, grid=(4 * nh,),
        in_specs=[pl.BlockSpec((S, HEAD), lambda j: (0, clamp(j))),
                  pl.BlockSpec((CONV_K, HEAD), lambda j: (0, clamp(j))),
                  pl.BlockSpec((None, S, HEAD), lambda j: (clamp(j) // nh, 0, clamp(j) % nh)),
                  pl.BlockSpec((S, HEAD), lambda j: (0, jnp.maximum(j - nq, 0)))],
        out_specs=(pl.BlockSpec((S, HEAD), lambda j: (0, j)), pl.BlockSpec((CONV_K, HEAD), lambda j: (0, clamp(j)))),
        out_shape=(SDS((S, 4 * W), MXU_DTYPE), SDS((CONV_K, 3 * W), f32)),
        scratch_shapes=[pltpu.VMEM((S, HEAD), f32)], compiler_params=_params(1),
    )(proj, conv_w, dqkv, dz)


def _gate_fns(braw, araw, alog, dtb):
    beta = jax.nn.sigmoid(braw)
    g = -jnp.exp(alog) * jax.nn.softplus(araw + dtb)
    return beta, g


def _lane_pick(x, lane):
    sel = lax.broadcasted_iota(jnp.int32, x.shape, 1) == lane
    return jnp.broadcast_to(jnp.sum(jnp.where(sel, x, 0.0), axis=1, keepdims=True), x.shape)


def dn_gates(pba, alog_b, dtb_b, name):
    S = pba.shape[0]
    H = alog_b.shape[0]

    def body(p_ref, al_ref, dt_ref, beta_ref, g_ref):
        h = pl.program_id(0)
        p = p_ref[...]
        beta, g = _gate_fns(_lane_pick(p, h), _lane_pick(p, H + h), al_ref[...], dt_ref[...])
        beta_ref[...] = beta
        g_ref[...] = g

    par = pl.BlockSpec((None, 1, HEAD), lambda h: (h, 0, 0))
    out = pl.BlockSpec((None, S, HEAD), lambda h: (h, 0, 0))
    return pl.pallas_call(
        body, name=name, grid=(H,), in_specs=[pl.BlockSpec((S, HEAD), lambda h: (0, 0)), par, par],
        out_specs=(out, out), out_shape=(SDS((H, S, HEAD), f32), SDS((H, S, HEAD), f32)), compiler_params=_params(1),
    )(pba, alog_b, dtb_b)


def dn_gates_bwd(pba, alog_b, dtb_b, dbeta, dg, name):
    S = pba.shape[0]
    H = alog_b.shape[0]

    def body(p_ref, al_ref, dt_ref, dbeta_ref, dg_ref, dp_ref, dal_ref, ddt_ref, acc):
        h = pl.program_id(0)

        @pl.when(h == 0)
        def _():
            acc[...] = jnp.zeros_like(acc)

        p = p_ref[...]
        db = jnp.broadcast_to(jnp.sum(dbeta_ref[...], axis=1, keepdims=True), p.shape)
        dgg = jnp.broadcast_to(jnp.sum(dg_ref[...], axis=1, keepdims=True), p.shape)
        _, vjp = jax.vjp(_gate_fns, _lane_pick(p, h), _lane_pick(p, H + h), al_ref[...], dt_ref[...])
        dbraw, daraw, dal, ddt = vjp((db, dgg))
        lane = lax.broadcasted_iota(jnp.int32, p.shape, 1)
        acc[...] += jnp.where(lane == h, dbraw, 0.0) + jnp.where(lane == H + h, daraw, 0.0)
        dal_ref[...] = dal
        ddt_ref[...] = ddt

        @pl.when(h == H - 1)
        def _():
            dp_ref[...] = acc[...].astype(dp_ref.dtype)

    par = pl.BlockSpec((None, 1, HEAD), lambda h: (h, 0, 0))
    big = pl.BlockSpec((None, S, HEAD), lambda h: (h, 0, 0))
    full = pl.BlockSpec((S, HEAD), lambda h: (0, 0))
    return pl.pallas_call(
        body, name=name, grid=(H,), in_specs=[full, par, par, big, big],
        out_specs=(full, par, par),
        out_shape=(SDS((S, HEAD), MXU_DTYPE), SDS((H, 1, HEAD), f32), SDS((H, 1, HEAD), f32)),
        scratch_shapes=[pltpu.VMEM((S, HEAD), f32)], compiler_params=_params(1),
    )(pba, alog_b, dtb_b, dbeta, dg)


def _bdot(dims):
    back = {NN: ((NT, 'gb'), (TN, 'ag')), NT: ((NN, 'gb'), (TN, 'ga')), TN: ((NT, 'bg'), (NN, 'ag'))}[dims]
    d = lambda p, q, dm: lax.dot_general(_c(p), _c(q), (dm, ((), ())), preferred_element_type=f32)

    @jax.custom_vjp
    def f(a, b):
        return d(a, b, dims)

    def fwd(a, b):
        return d(a, b, dims), (a, b)

    def bwd(res, g):
        v = {'a': res[0], 'b': res[1], 'g': g}
        (da_dims, da_ops), (db_dims, db_ops) = back
        return d(v[da_ops[0]], v[da_ops[1]], da_dims), d(v[db_ops[0]], v[db_ops[1]], db_dims)

    f.defvjp(fwd, bwd)
    return f, lambda a, b: d(a, b, dims)


_BDOT = {dims: _bdot(dims) for dims in (NN, NT, TN)}


def _tri_inv_impl(L):
    n = L.shape[0]
    eye = jnp.where(lax.broadcasted_iota(jnp.int32, (n, n), 0) == lax.broadcasted_iota(jnp.int32, (n, n), 1), 1.0, 0.0)
    P = -L
    T = eye + P
    for _ in range(int(math.log2(n)) - 1):
        P = _dot3(P, P, NN)
        T = T + _dot3(T, P, NN)
    return T


@jax.custom_vjp
def _tri_inv(L):
    return _tri_inv_impl(L)


def _tri_inv_fwd(L):
    T = _tri_inv_impl(L)
    return T, T


def _tri_inv_bwd(T, dT):
    return (-_dot3(T, _dot3(dT, T, NT), TN),)


_tri_inv.defvjp(_tri_inv_fwd, _tri_inv_bwd)


def _chunk_consts():
    C = DN_CHUNK
    io = lambda shape, ax: lax.broadcasted_iota(jnp.int32, shape, ax)
    one = lambda m: jnp.where(m, 1.0, 0.0).astype(f32)
    r, c = io((C, C), 0), io((C, C), 1)
    return dict(causal=r >= c, strict=r > c, tril=one(r >= c), sel_col=one(io((HEAD, C), 0) == 0),
                sel_row=one(io((C, HEAD), 1) == 0), last_c=one(c == C - 1), last_h=one(io((HEAD, C), 1) == C - 1))


def _chunk_fn(kc, diff, q, k, v, gB, bB, S0):
    i = 0 if diff else 1
    mm, mm_nt, mm_tn = _BDOT[NN][i], _BDOT[NT][i], _BDOT[TN][i]
    tri = _tri_inv if diff else _tri_inv_impl
    gcB = _hp(kc['tril'], gB, NN)
    dlt = _hp(gcB, kc['sel_col'], NN) - _hp(kc['sel_row'], gcB, NT)
    decay = jnp.where(kc['causal'], jnp.exp(jnp.where(kc['causal'], dlt, 0.0)), 0.0)
    kb, vb = k * bB, v * bB
    egc = jnp.exp(gcB)
    T = tri(jnp.where(kc['strict'], mm_nt(kb, k) * decay, 0.0))
    u, w = mm(T, vb), mm(T, kb * egc)
    qs = q * (HEAD ** -0.5)
    attn = jnp.where(kc['causal'], mm_nt(qs, k) * decay, 0.0)
    v_new = u - mm(w, S0)
    o = mm(qs * egc, S0) + mm(attn, v_new)
    kdec = k * jnp.exp(_hp(kc['last_c'], gcB, NN) - gcB)
    S1 = S0 * jnp.exp(_hp(kc['last_h'], gcB, NN)) + mm_tn(kdec, v_new)
    return o, S1


def _heads_per_block(H):
    return 4 if H % 4 == 0 else 1


def dn_chunk_fwd(qkv, gB, bB, name):
    _, S, W = qkv.shape
    H, C = W // HEAD, DN_CHUNK
    N, HB = S // C, _heads_per_block(H)

    def body(q_ref, k_ref, v_ref, g_ref, b_ref, o_ref, st_ref, s_scr):
        @pl.when(pl.program_id(1) == 0)
        def _():
            s_scr[...] = jnp.zeros_like(s_scr)

        kc = _chunk_consts()
        for hh in range(HB):
            sl = slice(hh * HEAD, (hh + 1) * HEAD)
            S0 = s_scr[hh]
            st_ref[hh] = S0
            o, S1 = _chunk_fn(kc, False, q_ref[:, sl], k_ref[:, sl], v_ref[:, sl], g_ref[hh], b_ref[hh], S0)
            o_ref[:, sl] = o
            s_scr[hh] = S1

    part = lambda p: pl.BlockSpec((None, C, HB * HEAD), lambda hb, n: (p, n, hb))
    gate = pl.BlockSpec((HB, C, HEAD), lambda hb, n: (hb, n, 0))
    return pl.pallas_call(
        body, name=name, grid=(H // HB, N), in_specs=[part(0), part(1), part(2), gate, gate],
        out_specs=(pl.BlockSpec((C, HB * HEAD), lambda hb, n: (n, hb)),
                   pl.BlockSpec((None, HB, HEAD, HEAD), lambda hb, n: (n, hb, 0, 0))),
        out_shape=(SDS((S, W), f32), SDS((N, H, HEAD, HEAD), f32)),
        scratch_shapes=[pltpu.VMEM((HB, HEAD, HEAD), f32)], compiler_params=_params(2),
    )(qkv, qkv, qkv, gB, bB)


def dn_chunk_bwd(qkv, gB, bB, states, do, name):
    _, S, W = qkv.shape
    H, C = W // HEAD, DN_CHUNK
    N, HB = S // C, _heads_per_block(H)

    def body(q_ref, k_ref, v_ref, g_ref, b_ref, st_ref, do_ref, dqkv_ref, dg_ref, db_ref, ds_scr):
        @pl.when(pl.program_id(1) == 0)
        def _():
            ds_scr[...] = jnp.zeros_like(ds_scr)

        kc = _chunk_consts()
        for hh in range(HB):
            sl = slice(hh * HEAD, (hh + 1) * HEAD)
            _, vjp = jax.vjp(functools.partial(_chunk_fn, kc, True), q_ref[:, sl], k_ref[:, sl], v_ref[:, sl],
                             g_ref[hh], b_ref[hh], st_ref[hh])
            dq, dk, dv, dg, db, dS0 = vjp((do_ref[:, sl], ds_scr[hh]))
            dqkv_ref[0, :, sl] = dq
            dqkv_ref[1, :, sl] = dk
            dqkv_ref[2, :, sl] = dv
            dg_ref[hh] = dg
            db_ref[hh] = db
            ds_scr[hh] = dS0

    rev = lambda n: N - 1 - n
    part = lambda p: pl.BlockSpec((None, C, HB * HEAD), lambda hb, n: (p, rev(n), hb))
    gate = pl.BlockSpec((HB, C, HEAD), lambda hb, n: (hb, rev(n), 0))
    return pl.pallas_call(
        body, name=name, grid=(H // HB, N),
        in_specs=[part(0), part(1), part(2), gate, gate,
                  pl.BlockSpec((None, HB, HEAD, HEAD), lambda hb, n: (rev(n), hb, 0, 0)),
                  pl.BlockSpec((C, HB * HEAD), lambda hb, n: (rev(n), hb))],
        out_specs=(pl.BlockSpec((3, C, HB * HEAD), lambda hb, n: (0, rev(n), hb)), gate, gate),
        out_shape=(SDS((3, S, W), f32), SDS((H, S, HEAD), f32), SDS((H, S, HEAD), f32)),
        scratch_shapes=[pltpu.VMEM((HB, HEAD, HEAD), f32)], compiler_params=_params(2),
    )(qkv, qkv, qkv, gB, bB, states, do)


def _gate_norm(o, z, ng):
    return _rms(o, ng) * _silu(z)


def dn_out(o, proj, ng, wout, x1, g3, name):
    S, W = o.shape
    D = x1.shape[1]
    nh = W // HEAD
    tm = _row_tile(S, 256)

    def body(o_ref, z_ref, ng_ref, w_ref, x_ref, g_ref, xo_ref, m_ref, og_ref):
        for h in range(nh):
            sl = slice(h * HEAD, (h + 1) * HEAD)
            og_ref[:, sl] = _gate_norm(o_ref[:, sl], z_ref[:, sl], ng_ref[...]).astype(og_ref.dtype)
        m = _mm(og_ref[...], w_ref[...])
        m_ref[...] = m
        xo_ref[...] = x_ref[...] + _rms(m, g_ref[...])

    rw = pl.BlockSpec((tm, W), lambda i: (i, 0))
    rd = pl.BlockSpec((tm, D), lambda i: (i, 0))
    return pl.pallas_call(
        body, name=name, grid=(S // tm,),
        in_specs=[rw, pl.BlockSpec((tm, W), lambda i: (i, 3)), pl.BlockSpec((1, HEAD), lambda i: (0, 0)),
                  pl.BlockSpec((W, D), lambda i: (0, 0)), rd, pl.BlockSpec((1, D), lambda i: (0, 0))],
        out_specs=(rd, rd, rw),
        out_shape=(SDS((S, D), f32), SDS((S, D), f32), SDS((S, W), MXU_DTYPE)), compiler_params=_params(1),
    )(o, proj, ng, wout, x1, g3)


def dn_out_bwd(dxo, m, g3, o, proj, ng, wout, name):
    S, W = o.shape
    D = m.shape[1]
    nh = W // HEAD
    tm = _row_tile(S, 256)

    def body(dxo_ref, m_ref, g_ref, o_ref, z_ref, ng_ref, w_ref, dm_ref, do_ref, dz_ref, dng_ref, dg_ref):
        @pl.when(pl.program_id(0) == 0)
        def _():
            dng_ref[...] = jnp.zeros_like(dng_ref)
            dg_ref[...] = jnp.zeros_like(dg_ref)

        dm, dg = _rms_bwd(m_ref[...], g_ref[...], dxo_ref[...])
        dg_ref[...] += dg
        dmc = dm.astype(dm_ref.dtype)
        dm_ref[...] = dmc
        dog = _mm_nt(dmc, w_ref[...])
        for h in range(nh):
            sl = slice(h * HEAD, (h + 1) * HEAD)
            _, vjp = jax.vjp(_gate_norm, o_ref[:, sl], z_ref[:, sl], ng_ref[...])
            do, dz, dng = vjp(dog[:, sl])
            do_ref[:, sl] = do
            dz_ref[:, sl] = dz.astype(dz_ref.dtype)
            dng_ref[...] += dng

    rw = pl.BlockSpec((tm, W), lambda i: (i, 0))
    rd = pl.BlockSpec((tm, D), lambda i: (i, 0))
    vd = pl.BlockSpec((1, D), lambda i: (0, 0))
    vh = pl.BlockSpec((1, HEAD), lambda i: (0, 0))
    return pl.pallas_call(
        body, name=name, grid=(S // tm,),
        in_specs=[rd, rd, vd, rw, pl.BlockSpec((tm, W), lambda i: (i, 3)), vh, pl.BlockSpec((W, D), lambda i: (0, 0))],
        out_specs=(rd, rw, rw, vh, vd),
        out_shape=(SDS((S, D), MXU_DTYPE), SDS((S, W), f32), SDS((S, W), MXU_DTYPE), SDS((1, HEAD), f32),
                   SDS((1, D), f32)),
        compiler_params=_params(1),
    )(dxo, m, g3, o, proj, ng, wout)


def _sg_stage1(pu, pv, bu, bv, lg, lb):
    u = _gelu(pu + bu)
    t = _gelu(pv + bv)
    tc = t - jnp.mean(t, axis=-1, keepdims=True)
    v = tc * lax.rsqrt(jnp.mean(tc * tc, axis=-1, keepdims=True) + LN_EPS) * lg + lb
    return u, v


def _causal_mask(n):
    return lax.broadcasted_iota(jnp.int32, (n, n), 0) >= lax.broadcasted_iota(jnp.int32, (n, n), 1)


def sg_mid(pre, b_in, ln_g, ln_b, w_s, bsT, wout, x1, g3, name):
    S = pre.shape[0]
    E, D = ln_g.shape[1], x1.shape[1]
    G, CH = SG_GROUPS, SG_CHUNK
    Cg = E // G
    tm = _row_tile(S, 256)

    def body(pu_ref, pv_ref, bu_ref, bv_ref, lg_ref, lb_ref, ws_ref, bs_ref, w_ref, x_ref, g_ref,
             xo_ref, m_ref, gt_ref):
        u, v = _sg_stage1(pu_ref[...], pv_ref[...], bu_ref[...], bv_ref[...], lg_ref[...], lb_ref[...])
        mask = _causal_mask(CH)
        for g in range(G):
            wc = _c(jnp.where(mask, ws_ref[g], 0.0))
            bcol = bs_ref[:, g:g + 1]
            cs = slice(g * Cg, (g + 1) * Cg)
            for ch in range(tm // CH):
                rs = slice(ch * CH, (ch + 1) * CH)
                mixed = _mm(wc, _c(v[rs, cs])) + bcol
                gt_ref[rs, cs] = (u[rs, cs] * mixed).astype(gt_ref.dtype)
        m = _mm(gt_ref[...], w_ref[...])
        m_ref[...] = m
        xo_ref[...] = x_ref[...] + _rms(m, g_ref[...])

    half = lambda p: pl.BlockSpec((tm, E), lambda i: (i, p))
    vhalf = lambda p: pl.BlockSpec((1, E), lambda i: (0, p))
    ve = pl.BlockSpec((1, E), lambda i: (0, 0))
    rd = pl.BlockSpec((tm, D), lambda i: (i, 0))
    return pl.pallas_call(
        body, name=name, grid=(S // tm,),
        in_specs=[half(0), half(1), vhalf(0), vhalf(1), ve, ve, pl.BlockSpec((G, CH, CH), lambda i: (0, 0, 0)),
                  pl.BlockSpec((CH, G), lambda i: (0, 0)), pl.BlockSpec((E, D), lambda i: (0, 0)), rd,
                  pl.BlockSpec((1, D), lambda i: (0, 0))],
        out_specs=(rd, rd, pl.BlockSpec((tm, E), lambda i: (i, 0))),
        out_shape=(SDS((S, D), f32), SDS((S, D), f32), SDS((S, E), MXU_DTYPE)), compiler_params=_params(1),
    )(pre, pre, b_in, b_in, ln_g, ln_b, w_s, bsT, wout, x1, g3)


def sg_mid_bwd(dxo, m, g3, pre, b_in, ln_g, ln_b, w_s, bsT, wout, name):
    S = pre.shape[0]
    E, D = ln_g.shape[1], m.shape[1]
    G, CH = SG_GROUPS, SG_CHUNK
    Cg = E // G
    tm = _row_tile(S, 256)

    def body(dxo_ref, m_ref, g_ref, pu_ref, pv_ref, bu_ref, bv_ref, lg_ref, lb_ref, ws_ref, bs_ref, w_ref,
             dm_ref, dpre_ref, dbin_ref, dlg_ref, dlb_ref, dws_ref, dbs_ref, dg_ref, du_scr, dv_scr):
        @pl.when(pl.program_id(0) == 0)
        def _():
            for r in (dbin_ref, dlg_ref, dlb_ref, dws_ref, dbs_ref, dg_ref):
                r[...] = jnp.zeros_like(r)

        dm, dg = _rms_bwd(m_ref[...], g_ref[...], dxo_ref[...])
        dg_ref[...] += dg
        dmc = dm.astype(dm_ref.dtype)
        dm_ref[...] = dmc
        dgated = _mm_nt(dmc, w_ref[...])
        (u, v), vjp1 = jax.vjp(_sg_stage1, pu_ref[...], pv_ref[...], bu_ref[...], bv_ref[...], lg_ref[...],
                               lb_ref[...])
        mask = _causal_mask(CH)
        lane = lax.broadcasted_iota(jnp.int32, (CH, CH), 1)
        for g in range(G):
            wc = _c(jnp.where(mask, ws_ref[g], 0.0))
            bcol = bs_ref[:, g:g + 1]
            cs = slice(g * Cg, (g + 1) * Cg)
            dws = jnp.zeros((CH, CH), f32)
            dbs = jnp.zeros((CH, 1), f32)
            for ch in range(tm // CH):
                rs = slice(ch * CH, (ch + 1) * CH)
                vs = _c(v[rs, cs])
                mixed = _mm(wc, vs) + bcol
                dgt = dgated[rs, cs]
                du_scr[rs, cs] = dgt * mixed
                dmixed = dgt * u[rs, cs]
                dmc2 = _c(dmixed)
                dv_scr[rs, cs] = _mm_tn(wc, dmc2)
                dws = dws + _mm_nt(dmc2, vs)
                dbs = dbs + jnp.sum(dmixed, axis=1, keepdims=True)
            dws_ref[g] += jnp.where(mask, dws, 0.0)
            dbs_ref[...] += jnp.where(lane == g, jnp.broadcast_to(dbs, (CH, CH)), 0.0)
        dpu, dpv, dbu, dbv, dlg, dlb = vjp1((du_scr[...], dv_scr[...]))
        dpre_ref[:, :E] = dpu.astype(dpre_ref.dtype)
        dpre_ref[:, E:] = dpv.astype(dpre_ref.dtype)
        dbin_ref[:, :E] += dbu
        dbin_ref[:, E:] += dbv
        dlg_ref[...] += dlg
        dlb_ref[...] += dlb

    half = lambda p: pl.BlockSpec((tm, E), lambda i: (i, p))
    vhalf = lambda p: pl.BlockSpec((1, E), lambda i: (0, p))
    ve = pl.BlockSpec((1, E), lambda i: (0, 0))
    rd = pl.BlockSpec((tm, D), lambda i: (i, 0))
    vd = pl.BlockSpec((1, D), lambda i: (0, 0))
    wsb = pl.BlockSpec((G, CH, CH), lambda i: (0, 0, 0))
    return pl.pallas_call(
        body, name=name, grid=(S // tm,),
        in_specs=[rd, rd, vd, half(0), half(1), vhalf(0), vhalf(1), ve, ve, wsb,
                  pl.BlockSpec((CH, G), lambda i: (0, 0)), pl.BlockSpec((E, D), lambda i: (0, 0))],
        out_specs=(rd, pl.BlockSpec((tm, 2 * E), lambda i: (i, 0)), pl.BlockSpec((1, 2 * E), lambda i: (0, 0)), ve, ve,
                   wsb, pl.BlockSpec((CH, CH), lambda i: (0, 0)), vd),
        out_shape=(SDS((S, D), MXU_DTYPE), SDS((S, 2 * E), MXU_DTYPE), SDS((1, 2 * E), f32), SDS((1, E), f32),
                   SDS((1, E), f32), SDS((G, CH, CH), f32), SDS((CH, CH), f32), SDS((1, D), f32)),
        scratch_shapes=[pltpu.VMEM((tm, E), f32), pltpu.VMEM((tm, E), f32)], compiler_params=_params(1),
    )(dxo, m, g3, pre, pre, b_in, b_in, ln_g, ln_b, w_s, bsT, wout)


def loss_head(y, target, name):
    S, D = y.shape
    tm = _row_tile(S, 512)

    def body(y_ref, t_ref, l_ref, d_ref):
        @pl.when(pl.program_id(0) == 0)
        def _():
            l_ref[...] = jnp.zeros_like(l_ref)

        e = y_ref[...] - t_ref[...]
        d_ref[...] = e * (1.0 / D)
        l_ref[...] += jnp.sum(e * e) * (0.5 / D)

    row = pl.BlockSpec((tm, D), lambda i: (i, 0))
    return pl.pallas_call(
        body, name=name, grid=(S // tm,), in_specs=[row, row],
        out_specs=(pl.BlockSpec((1, HEAD), lambda i: (0, 0)), row),
        out_shape=(SDS((1, HEAD), f32), SDS((S, D), f32)), compiler_params=_params(1),
    )(y, target)


def sum_slots(r, name):
    _, R, C = r.shape
    tr = _row_tile(R, 648 if R % 648 == 0 else R)

    def body(r_ref, o_ref):
        acc = r_ref[0].astype(f32)
        for s in range(1, N_DEV):
            acc = acc + r_ref[s].astype(f32)
        o_ref[...] = acc

    return pl.pallas_call(
        body, name=name, grid=(R // tr,), in_specs=[pl.BlockSpec((N_DEV, tr, C), lambda i: (0, i, 0))],
        out_specs=pl.BlockSpec((tr, C), lambda i: (i, 0)), out_shape=SDS((R, C), f32), compiler_params=_params(1),
    )(r)


def _adam_math(w, g, m, v):
    m = ADAM_B1 * m + (1.0 - ADAM_B1) * g
    v = ADAM_B2 * v + (1.0 - ADAM_B2) * (g * g)
    m_hat = m / (1.0 - ADAM_B1 ** ADAM_STEP)
    v_hat = v / (1.0 - ADAM_B2 ** ADAM_STEP)
    delta = -ADAM_LR * (m_hat / (jnp.sqrt(v_hat) + ADAM_EPS) + ADAM_WD * w)
    return delta, m, v


def adam_slots(w, r, m, v, name, tr):
    R, C = w.shape
    tr = _row_tile(R, tr)

    def body(w_ref, r_ref, m_ref, v_ref, g_ref, d_ref, mo_ref, vo_ref):
        g = r_ref[0].astype(f32)
        for s in range(1, N_DEV):
            g = g + r_ref[s].astype(f32)
        g_ref[...] = g
        d_ref[...], mo_ref[...], vo_ref[...] = _adam_math(w_ref[...], g, m_ref[...], v_ref[...])

    row = pl.BlockSpec((tr, C), lambda i: (i, 0))
    return pl.pallas_call(
        body, name=name, grid=(R // tr,), in_specs=[row, pl.BlockSpec((N_DEV, tr, C), lambda i: (0, i, 0)), row, row],
        out_specs=(row, row, row, row), out_shape=tuple(SDS((R, C), f32) for _ in range(4)),
        compiler_params=_params(1),
    )(w, r, m, v)


def adam_small(w, g, m, v, name):
    def body(w_ref, g_ref, m_ref, v_ref, d_ref, mo_ref, vo_ref):
        d_ref[...], mo_ref[...], vo_ref[...] = _adam_math(w_ref[...], g_ref[...], m_ref[...], v_ref[...])

    return pl.pallas_call(body, name=name, out_shape=tuple(SDS(w.shape, f32) for _ in range(3)))(w, g, m, v)


def _pack_rows(parts):
    rows, offs, r = [], [], 0
    for p in parts:
        flat = p.reshape(-1)
        n = -(-flat.shape[0] // HEAD)
        flat = jnp.pad(flat, (0, n * HEAD - flat.shape[0]))
        rows.append(flat.reshape(n, HEAD))
        offs.append((r, n))
        r += n
    pad = (-r) % 8
    if pad:
        rows.append(jnp.zeros((pad, HEAD), f32))
    return jnp.concatenate(rows, axis=0), offs


def kernel(x, norm_g, ffn_w_gate, ffn_w_up, ffn_w_down, dn_w_in, dn_conv_w, dn_a_log, dn_dt_bias, dn_norm_g, dn_w_out, sg_w_in, sg_b_in, sg_ln_g, sg_ln_b, sg_w_s, sg_b_s, sg_w_out, loss_target, m_norm_g, m_ffn_w_gate, m_ffn_w_up, m_ffn_w_down, m_dn_w_in, m_dn_conv_w, m_dn_a_log, m_dn_dt_bias, m_dn_norm_g, m_dn_w_out, m_sg_w_in, m_sg_b_in, m_sg_ln_g, m_sg_ln_b, m_sg_w_s, m_sg_b_s, m_sg_w_out, v_norm_g, v_ffn_w_gate, v_ffn_w_up, v_ffn_w_down, v_dn_w_in, v_dn_conv_w, v_dn_a_log, v_dn_dt_bias, v_dn_norm_g, v_dn_w_out, v_sg_w_in, v_sg_b_in, v_sg_ln_g, v_sg_ln_b, v_sg_w_s, v_sg_b_s, v_sg_w_out):
    weights = dict(norm_g=norm_g, ffn_w_gate=ffn_w_gate, ffn_w_up=ffn_w_up, ffn_w_down=ffn_w_down, dn_w_in=dn_w_in,
                   dn_conv_w=dn_conv_w, dn_a_log=dn_a_log, dn_dt_bias=dn_dt_bias, dn_norm_g=dn_norm_g,
                   dn_w_out=dn_w_out, sg_w_in=sg_w_in, sg_b_in=sg_b_in, sg_ln_g=sg_ln_g, sg_ln_b=sg_ln_b,
                   sg_w_s=sg_w_s, sg_b_s=sg_b_s, sg_w_out=sg_w_out)
    mom_m = dict(norm_g=m_norm_g, ffn_w_gate=m_ffn_w_gate, ffn_w_up=m_ffn_w_up, ffn_w_down=m_ffn_w_down,
                 dn_w_in=m_dn_w_in, dn_conv_w=m_dn_conv_w, dn_a_log=m_dn_a_log, dn_dt_bias=m_dn_dt_bias,
                 dn_norm_g=m_dn_norm_g, dn_w_out=m_dn_w_out, sg_w_in=m_sg_w_in, sg_b_in=m_sg_b_in,
                 sg_ln_g=m_sg_ln_g, sg_ln_b=m_sg_ln_b, sg_w_s=m_sg_w_s, sg_b_s=m_sg_b_s, sg_w_out=m_sg_w_out)
    mom_v = dict(norm_g=v_norm_g, ffn_w_gate=v_ffn_w_gate, ffn_w_up=v_ffn_w_up, ffn_w_down=v_ffn_w_down,
                 dn_w_in=v_dn_w_in, dn_conv_w=v_dn_conv_w, dn_a_log=v_dn_a_log, dn_dt_bias=v_dn_dt_bias,
                 dn_norm_g=v_dn_norm_g, dn_w_out=v_dn_w_out, sg_w_in=v_sg_w_in, sg_b_in=v_sg_b_in,
                 sg_ln_g=v_sg_ln_g, sg_ln_b=v_sg_ln_b, sg_w_s=v_sg_w_s, sg_b_s=v_sg_b_s, sg_w_out=v_sg_w_out)
    order = list(weights)

    xs = x[0]
    S, D = xs.shape
    F8 = ffn_w_gate.shape[-1]
    depth = norm_g.shape[0]
    W = dn_w_out.shape[1] * N_DEV
    H = W // HEAD
    E = sg_ln_g.shape[1] * N_DEV
    G, CH = sg_w_s.shape[1], sg_w_s.shape[2]
    c8 = dn_w_in.shape[2]
    me = _slot(lax.axis_index("x"), lax.axis_index("y"), lax.axis_index("c"))

    small_in, small_offs = _pack_rows([norm_g, dn_conv_w, sg_b_in, sg_ln_g, sg_ln_b])
    gathered = all_gather_multi(
        [_c(ffn_w_gate), _c(ffn_w_up), _c(ffn_w_down), _c(dn_w_in[0]), _c(dn_w_out[0]), _c(sg_w_in[0]),
         _c(sg_w_out[0]), small_in], name="gather_weights")
    wg_all, wu_all, wd_all, dnin_all, dnout_all, sgin_all, sgout_all, small_all = gathered

    def small_piece(i, shard_shape):
        r0, n = small_offs[i]
        sz = math.prod(shard_shape)
        return small_all[:, r0:r0 + n, :].reshape(N_DEV, n * HEAD)[:, :sz].reshape((N_DEV,) + tuple(shard_shape))

    ng_full = jnp.moveaxis(small_piece(0, norm_g.shape), 0, 2).reshape(depth, 6, D)
    conv_full = jnp.moveaxis(small_piece(1, dn_conv_w.shape[1:]), 0, 1).reshape(CONV_K, 3 * W)
    bin_full = small_piece(2, sg_b_in.shape[1:]).reshape(1, 2 * E)
    lng_full = small_piece(3, sg_ln_g.shape[1:]).reshape(1, E)
    lnb_full = small_piece(4, sg_ln_b.shape[1:]).reshape(1, E)
    dn_win = jnp.moveaxis(dnin_all, 0, 1).reshape(D, N_DEV * c8)
    dn_wmain = dn_win[:, :4 * W]
    dn_wba = jnp.pad(dn_win[:, 4 * W:], ((0, 0), (0, HEAD - 2 * H)))
    dn_wout = dnout_all.reshape(W, D)
    sg_win = jnp.moveaxis(sgin_all, 0, 1).reshape(D, 2 * E)
    sg_wout = sgout_all.reshape(E, D)
    alog_b = jnp.broadcast_to(dn_a_log.reshape(H, 1, 1), (H, 1, HEAD))
    dtb_b = jnp.broadcast_to(dn_dt_bias.reshape(H, 1, 1), (H, 1, HEAD))
    bsT = sg_b_s[0].T
    gvec = lambda l, k: ng_full[l, k].reshape(1, D)

    saved = []
    cur = xs
    for l in range(depth):
        sv = {}
        sv['x0'] = cur
        cur, sv['hA'], sv['aA'], sv['bA'], sv['yA'] = ffn_fwd(cur, gvec(l, 0), gvec(l, 1), wg_all, wu_all, wd_all, l, 0,
                                                              name=f"ffn_fwd_{l}a")
        sv['x1'] = cur
        if l % 2 == 0:
            sv['hM'], sv['proj'], sv['pba'] = rms_mm(cur, gvec(l, 2), dn_wmain, dn_wba, name=f"dn_in_{l}")
            sv['qkv'] = dn_prep(sv['proj'], conv_full, name=f"dn_prep_{l}")
            sv['beta'], sv['g'] = dn_gates(sv['pba'], alog_b, dtb_b, name=f"dn_gates_{l}")
            sv['o'], sv['states'] = dn_chunk_fwd(sv['qkv'], sv['g'], sv['beta'], name=f"dn_chunk_{l}")
            cur, sv['m'], sv['og'] = dn_out(sv['o'], sv['proj'], dn_norm_g, dn_wout, cur, gvec(l, 3), name=f"dn_out_{l}")
        else:
            sv['hM'], sv['pre'] = rms_mm(cur, gvec(l, 2), sg_win, None, name=f"sg_in_{l}")
            cur, sv['m'], sv['gated'] = sg_mid(sv['pre'], bin_full, lng_full, lnb_full, sg_w_s[0], bsT, sg_wout, cur,
                                               gvec(l, 3), name=f"sg_mid_{l}")
        sv['x2'] = cur
        cur, sv['hB'], sv['aB'], sv['bB'], sv['yB'] = ffn_fwd(cur, gvec(l, 4), gvec(l, 5), wg_all, wu_all, wd_all, l, 1,
                                                              name=f"ffn_fwd_{l}b")
        saved.append(sv)

    loss_blk, dcur = loss_head(cur, loss_target[0], name="loss_head")
    loss = lax.psum(loss_blk[0, 0], ("x", "y", "c"))

    dng = [[None] * 6 for _ in range(depth)]
    dwg = [[None, None] for _ in range(depth)]
    dwu = [[None, None] for _ in range(depth)]
    dwd = [[None, None] for _ in range(depth)]
    grads = {}
    for l in reversed(range(depth)):
        sv = saved[l]
        dcur, da, db, dy, dng[l][4], dng[l][5] = ffn_bwd_dx(dcur, sv['x2'], sv['yB'], sv['aB'], sv['bB'], gvec(l, 4),
                                                            gvec(l, 5), wg_all, wu_all, wd_all, l, 1, name=f"ffn_bwd_{l}b")
        dwg[l][1], dwu[l][1], dwd[l][1] = ffn_bwd_dw(sv['hB'], dy, sv['aB'], sv['bB'], da, db, name=f"ffn_dw_{l}b")
        if l % 2 == 0:
            dm, do, dz, grads['dn_norm_g'], dng[l][3] = dn_out_bwd(dcur, sv['m'], gvec(l, 3), sv['o'], sv['proj'],
                                                                  dn_norm_g, dn_wout, name=f"dn_out_bwd_{l}")
            grads['dn_w_out'] = tn_mm(sv['og'], dm, name=f"dn_wout_dw_{l}").reshape(N_DEV, W // N_DEV, D)
            dqkv, dgB, dbB = dn_chunk_bwd(sv['qkv'], sv['g'], sv['beta'], sv['states'], do, name=f"dn_chunk_bwd_{l}")
            dpba, dal, ddt = dn_gates_bwd(sv['pba'], alog_b, dtb_b, dbB, dgB, name=f"dn_gates_bwd_{l}")
            grads['dn_a_log'] = dal[:, 0, 0].reshape(1, H)
            grads['dn_dt_bias'] = ddt[:, 0, 0].reshape(1, H)
            dproj, grads['dn_conv_w'] = dn_prep_bwd(sv['proj'], conv_full, dqkv, dz, name=f"dn_prep_bwd_{l}")
            dw_main = tn_mm(sv['hM'], dproj, name=f"dn_win_dw_{l}")
            dw_ba = tn_mm(sv['hM'], dpba, name=f"dn_wba_dw_{l}", tn=HEAD)
            dw_in = jnp.concatenate([dw_main, dw_ba[:, :2 * H]], axis=1)
            grads['dn_w_in'] = jnp.moveaxis(dw_in.reshape(D, N_DEV, c8), 1, 0)
            dcur, dng[l][2] = mm_bwd_dx(dcur, sv['x1'], gvec(l, 2), dproj, dn_wmain, dpba, dn_wba, name=f"dn_in_bwd_{l}")
        else:
            dm, dpre, grads['sg_b_in'], grads['sg_ln_g'], grads['sg_ln_b'], grads['sg_w_s'], dbs, dng[l][3] = sg_mid_bwd(
                dcur, sv['m'], gvec(l, 3), sv['pre'], bin_full, lng_full, lnb_full, sg_w_s[0], bsT, sg_wout,
                name=f"sg_mid_bwd_{l}")
            grads['sg_b_s'] = dbs[:, :G].T
            grads['sg_w_out'] = tn_mm(sv['gated'], dm, name=f"sg_wout_dw_{l}").reshape(N_DEV, E // N_DEV, D)
            grads['sg_w_in'] = tn_mm(sv['hM'], dpre, name=f"sg_win_dw_{l}", tn=2 * E // N_DEV, slot_major=True)
            dcur, dng[l][2] = mm_bwd_dx(dcur, sv['x1'], gvec(l, 2), dpre, sg_win, None, None, name=f"sg_in_bwd_{l}")
        dcur, da, db, dy, dng[l][0], dng[l][1] = ffn_bwd_dx(dcur, sv['x0'], sv['yA'], sv['aA'], sv['bA'], gvec(l, 0),
                                                            gvec(l, 1), wg_all, wu_all, wd_all, l, 0, name=f"ffn_bwd_{l}a")
        dwg[l][0], dwu[l][0], dwd[l][0] = ffn_bwd_dw(sv['hA'], dy, sv['aA'], sv['bA'], da, db, name=f"ffn_dw_{l}a")
    grad_x = dcur[None]

    stack4 = lambda t: jnp.stack([jnp.stack(r, axis=1) for r in t], axis=1)
    big = [stack4(dwg), stack4(dwu), stack4(dwd), grads['dn_w_in'], grads['dn_w_out'], grads['sg_w_in'],
           grads['sg_w_out']]
    big_names = ['ffn_w_gate', 'ffn_w_up', 'ffn_w_down', 'dn_w_in', 'dn_w_out', 'sg_w_in', 'sg_w_out']
    slots = exchange_slots(big, name="exchange_grads")

    dng_full = jnp.stack([jnp.concatenate(r, axis=0) for r in dng], axis=0)
    small_names = ['norm_g', 'dn_conv_w', 'sg_b_in', 'sg_ln_g', 'sg_ln_b', 'sg_w_s', 'sg_b_s', 'dn_a_log',
                   'dn_dt_bias', 'dn_norm_g']
    small_parts = [dng_full, grads['dn_conv_w'], grads['sg_b_in'], grads['sg_ln_g'], grads['sg_ln_b'],
                   grads['sg_w_s'], grads['sg_b_s'], grads['dn_a_log'], grads['dn_dt_bias'], grads['dn_norm_g']]
    small_pack, offs = _pack_rows(small_parts)
    (small_slots,) = all_gather_multi([small_pack], name="gather_small_grads")
    small_sum = sum_slots(small_slots, name="sum_small_grads")

    def small_grad(i):
        r0, n = offs[i]
        p = small_parts[i]
        return small_sum[r0:r0 + n].reshape(-1)[:p.size].reshape(p.shape)

    def my_shard(full, axis, like):
        n = full.shape[axis] // N_DEV
        return lax.dynamic_slice_in_dim(full, me * n, n, axis).reshape(like.shape)

    g_small = {
        'norm_g': my_shard(small_grad(0), 2, norm_g),
        'dn_conv_w': my_shard(small_grad(1), 1, dn_conv_w),
        'sg_b_in': my_shard(small_grad(2), 1, sg_b_in),
        'sg_ln_g': my_shard(small_grad(3), 1, sg_ln_g),
        'sg_ln_b': my_shard(small_grad(4), 1, sg_ln_b),
        'sg_w_s': small_grad(5).reshape(sg_w_s.shape),
        'sg_b_s': small_grad(6).reshape(sg_b_s.shape),
        'dn_a_log': small_grad(7).reshape(dn_a_log.shape),
        'dn_dt_bias': small_grad(8).reshape(dn_dt_bias.shape),
        'dn_norm_g': small_grad(9).reshape(dn_norm_g.shape),
    }

    out_g, out_d, out_m, out_v = {}, {}, {}, {}
    for nm, r in zip(big_names, slots):
        w = weights[nm]
        cols = w.shape[-1]
        rows = w.size // cols
        tr = {'ffn_w_gate': 512, 'ffn_w_up': 512, 'ffn_w_down': F8, 'dn_w_in': 256, 'sg_w_in': 256}.get(nm, rows)
        g, d, m2, v2 = adam_slots(w.reshape(rows, cols), r.reshape(N_DEV, rows, cols), mom_m[nm].reshape(rows, cols),
                                  mom_v[nm].reshape(rows, cols), name=f"adam_{nm}", tr=tr)
        out_g[nm], out_d[nm], out_m[nm], out_v[nm] = (t.reshape(w.shape) for t in (g, d, m2, v2))
    for nm in small_names:
        w = weights[nm]
        cols = w.shape[-1]
        rows = w.size // cols
        two = lambda t: t.reshape(rows, cols)
        d, m2, v2 = adam_small(two(w), two(g_small[nm]), two(mom_m[nm]), two(mom_v[nm]), name=f"adam_{nm}")
        out_g[nm] = g_small[nm]
        out_d[nm], out_m[nm], out_v[nm] = (t.reshape(w.shape) for t in (d, m2, v2))

    return (loss, grad_x, *[out_g[n] for n in order], *[out_d[n] for n in order], *[out_m[n] for n in order],
            *[out_v[n] for n in order])
```

```python
import functools
import math

import jax
import jax.numpy as jnp
from jax import lax
from jax.experimental import pallas as pl
from jax.experimental.pallas import tpu as pltpu

f32 = jnp.float32
MXU_DTYPE = jnp.bfloat16
N_DEV = 8
RMS_EPS = 1e-6
LN_EPS = 1e-5
L2_EPS = 1e-6
HEAD = 128
DN_CHUNK = 64
SG_CHUNK = 128
SG_GROUPS = 8
CONV_K = 4
ADAM_LR, ADAM_B1, ADAM_B2, ADAM_EPS, ADAM_WD, ADAM_STEP = 0.001, 0.9, 0.999, 1e-08, 0.01, 10
VMEM_LIMIT = 56 * 1024 * 1024
SDS = jax.ShapeDtypeStruct
HIGHEST = lax.Precision.HIGHEST
MESH = pl.DeviceIdType.MESH


def _params(n_grid):
    return pltpu.CompilerParams(dimension_semantics=("arbitrary",) * n_grid, vmem_limit_bytes=VMEM_LIMIT)


def _row_tile(s, want):
    t = min(s, want)
    assert s % t == 0, (s, t)
    return t


def _rms(x, g):
    return x * lax.rsqrt(jnp.mean(x * x, axis=-1, keepdims=True) + RMS_EPS) * g


def _rms_bwd(x, g, dy):
    _, vjp = jax.vjp(_rms, x, g)
    return vjp(dy)


def _silu(a):
    return a * jax.nn.sigmoid(a)


def _gelu(x):
    return 0.5 * x * (1.0 + lax.erf(x * 0.7071067811865476))


def _mm(a, b):
    return lax.dot_general(a, b, (((1,), (0,)), ((), ())), preferred_element_type=f32)


def _mm_nt(a, b):
    return lax.dot_general(a, b, (((1,), (1,)), ((), ())), preferred_element_type=f32)


def _mm_tn(a, b):
    return lax.dot_general(a, b, (((0,), (0,)), ((), ())), preferred_element_type=f32)


def _c(x):
    return x.astype(MXU_DTYPE)


def _hp(a, b, dims):
    return lax.dot_general(a, b, (dims, ((), ())), preferred_element_type=f32, precision=HIGHEST)


def _split(a):
    hi = a.astype(MXU_DTYPE)
    lo = (a - hi.astype(f32)).astype(MXU_DTYPE)
    return hi, lo


def _dot3(a, b, dims):
    ah, al = _split(a)
    bh, bl = _split(b)
    d = lambda p, q: lax.dot_general(p, q, (dims, ((), ())), preferred_element_type=f32)
    return d(ah, bh) + (d(ah, bl) + d(al, bh))


NN, NT, TN = ((1,), (0,)), ((1,), (1,)), ((0,), (0,))


def _slot(px, py, pc):
    return 4 * px + 2 * py + pc


def all_gather_multi(arrs, name):
    n = len(arrs)

    def body(*refs):
        ins, outs = refs[:n], refs[n:2 * n]
        send_sems, recv_sems, local_sems = refs[2 * n:]
        x, y, c = lax.axis_index("x"), lax.axis_index("y"), lax.axis_index("c")
        me, sibling = (x, y, c), (x, y, 1 - c)
        chips = [(1 - x, y), (x, 1 - y), (1 - x, 1 - y)]

        def copy(a, k, block, to, src=None):
            dst = outs[a].at[_slot(*block)]
            return pltpu.make_async_remote_copy(
                src_ref=dst if src is None else src, dst_ref=dst, send_sem=send_sems.at[a, k],
                recv_sem=recv_sems.at[a, k], device_id=to, device_id_type=MESH)

        started = []
        for a in range(n):
            mine = pltpu.make_async_copy(ins[a], outs[a].at[_slot(*me)], local_sems.at[a])
            mine.start()
            started.append(mine)
        sends = []
        for a in range(n):
            first = [copy(a, 0, me, sibling, src=ins[a])]
            first += [copy(a, 1 + j, me, (*chip, c), src=ins[a]) for j, chip in enumerate(chips)]
            for cp in first:
                cp.start()
            sends += first
        for a in range(n):
            for j, chip in enumerate(chips):
                copy(a, 1 + j, (*chip, c), me).wait_recv()
                fwd = copy(a, 4 + j, (*chip, c), sibling)
                fwd.start()
                sends.append(fwd)
        for a in range(n):
            copy(a, 0, sibling, me).wait_recv()
            for j, chip in enumerate(chips):
                copy(a, 4 + j, (*chip, 1 - c), me).wait_recv()
        for cp in sends:
            cp.wait_send()
        for mine in started:
            mine.wait()

    hbm = pl.BlockSpec(memory_space=pltpu.HBM)
    return pl.pallas_call(
        body, name=name,
        out_shape=tuple(SDS((N_DEV,) + a.shape, a.dtype) for a in arrs),
        in_specs=[hbm] * n, out_specs=tuple([hbm] * n),
        scratch_shapes=[pltpu.SemaphoreType.DMA((n, 7)), pltpu.SemaphoreType.DMA((n, 7)),
                        pltpu.SemaphoreType.DMA((n,))],
    )(*arrs)


def exchange_slots(arrs, name):
    n = len(arrs)

    def body(*refs):
        ins, outs = refs[:n], refs[n:2 * n]
        send_sems, recv_sems, local_sems = refs[2 * n:]
        x, y, c = lax.axis_index("x"), lax.axis_index("y"), lax.axis_index("c")
        me = _slot(x, y, c)
        peers = [(x ^ (k >> 2), y ^ ((k >> 1) & 1), c ^ (k & 1)) for k in range(1, N_DEV)]

        def copy(a, k):
            peer = peers[k - 1]
            return pltpu.make_async_remote_copy(
                src_ref=ins[a].at[_slot(*peer)], dst_ref=outs[a].at[me], send_sem=send_sems.at[a, k - 1],
                recv_sem=recv_sems.at[a, k - 1], device_id=peer, device_id_type=MESH)

        def landed(a, k):
            peer = peers[k - 1]
            return pltpu.make_async_remote_copy(
                src_ref=ins[a].at[me], dst_ref=outs[a].at[_slot(*peer)], send_sem=send_sems.at[a, k - 1],
                recv_sem=recv_sems.at[a, k - 1], device_id=peer, device_id_type=MESH)

        local = []
        for a in range(n):
            cp = pltpu.make_async_copy(ins[a].at[me], outs[a].at[me], local_sems.at[a])
            cp.start()
            local.append(cp)
        order = [6, 7, 2, 3, 4, 5, 1]
        for a in range(n):
            for k in order:
                copy(a, k).start()
        for a in range(n):
            for k in order:
                copy(a, k).wait_send()
                landed(a, k).wait_recv()
        for cp in local:
            cp.wait()

    hbm = pl.BlockSpec(memory_space=pltpu.HBM)
    return pl.pallas_call(
        body, name=name,
        out_shape=tuple(SDS(a.shape, a.dtype) for a in arrs),
        in_specs=[hbm] * n, out_specs=tuple([hbm] * n),
        scratch_shapes=[pltpu.SemaphoreType.DMA((n, 7)), pltpu.SemaphoreType.DMA((n, 7)),
                        pltpu.SemaphoreType.DMA((n,))],
    )(*arrs)


def ffn_fwd(x, gpre, gpost, wg, wu, wd, li, ab, name):
    S, D = x.shape
    F8 = wg.shape[-1]
    tm = _row_tile(S, 512)
    nj = N_DEV

    def body(x_ref, gpre_ref, gpost_ref, wg_ref, wu_ref, wd_ref, xo_ref, h_ref, a_ref, b_ref, y_ref):
        j = pl.program_id(1)

        @pl.when(j == 0)
        def _():
            h_ref[...] = _rms(x_ref[...], gpre_ref[...]).astype(h_ref.dtype)
            y_ref[...] = jnp.zeros_like(y_ref)

        h = h_ref[...]
        a = _mm(h, wg_ref[...]).astype(a_ref.dtype)
        b = _mm(h, wu_ref[...]).astype(b_ref.dtype)
        a_ref[...] = a
        b_ref[...] = b
        t = _silu(a.astype(f32)) * b.astype(f32)
        y_ref[...] += _mm(_c(t), wd_ref[...])

        @pl.when(j == nj - 1)
        def _():
            xo_ref[...] = x_ref[...] + 0.5 * _rms(y_ref[...], gpost_ref[...])

    row = pl.BlockSpec((tm, D), lambda i, j: (i, 0))
    vec = pl.BlockSpec((1, D), lambda i, j: (0, 0))
    wcol = pl.BlockSpec((None, None, None, D, F8), lambda i, j: (j, li, ab, 0, 0))
    wrow = pl.BlockSpec((None, None, None, F8, D), lambda i, j: (j, li, ab, 0, 0))
    hid = pl.BlockSpec((None, tm, F8), lambda i, j: (j, i, 0))
    return pl.pallas_call(
        body, name=name, grid=(S // tm, nj),
        in_specs=[row, vec, vec, wcol, wcol, wrow],
        out_specs=(row, row, hid, hid, row),
        out_shape=(SDS((S, D), f32), SDS((S, D), MXU_DTYPE), SDS((N_DEV, S, F8), MXU_DTYPE),
                   SDS((N_DEV, S, F8), MXU_DTYPE), SDS((S, D), f32)),
        compiler_params=_params(2),
    )(x, gpre, gpost, wg, wu, wd)


def ffn_bwd_dx(dxo, x, y, a, b, gpre, gpost, wg, wu, wd, li, ab, name):
    S, D = x.shape
    F8 = wg.shape[-1]
    tm = _row_tile(S, 512)
    nj = N_DEV

    def body(dxo_ref, x_ref, y_ref, a_ref, b_ref, gpre_ref, gpost_ref, wg_ref, wu_ref, wd_ref,
             dx_ref, da_ref, db_ref, dy_ref, dgpre_ref, dgpost_ref, dh_ref):
        i, j = pl.program_id(0), pl.program_id(1)

        @pl.when(j == 0)
        def _():
            @pl.when(i == 0)
            def _():
                dgpre_ref[...] = jnp.zeros_like(dgpre_ref)
                dgpost_ref[...] = jnp.zeros_like(dgpost_ref)

            dy, dg = _rms_bwd(y_ref[...], gpost_ref[...], 0.5 * dxo_ref[...])
            dy_ref[...] = dy.astype(dy_ref.dtype)
            dgpost_ref[...] += dg
            dh_ref[...] = jnp.zeros_like(dh_ref)

        dt = _mm_nt(dy_ref[...], wd_ref[...])
        af, bf = a_ref[...].astype(f32), b_ref[...].astype(f32)
        s = jax.nn.sigmoid(af)
        da = (dt * bf * (s * (1.0 + af * (1.0 - s)))).astype(da_ref.dtype)
        db = (dt * (af * s)).astype(db_ref.dtype)
        da_ref[...] = da
        db_ref[...] = db
        dh_ref[...] += _mm_nt(da, wg_ref[...]) + _mm_nt(db, wu_ref[...])

        @pl.when(j == nj - 1)
        def _():
            dxx, dg = _rms_bwd(x_ref[...], gpre_ref[...], dh_ref[...])
            dx_ref[...] = dxo_ref[...] + dxx
            dgpre_ref[...] += dg

    row = pl.BlockSpec((tm, D), lambda i, j: (i, 0))
    vec = pl.BlockSpec((1, D), lambda i, j: (0, 0))
    wcol = pl.BlockSpec((None, None, None, D, F8), lambda i, j: (j, li, ab, 0, 0))
    wrow = pl.BlockSpec((None, None, None, F8, D), lambda i, j: (j, li, ab, 0, 0))
    hid = pl.BlockSpec((None, tm, F8), lambda i, j: (j, i, 0))
    return pl.pallas_call(
        body, name=name, grid=(S // tm, nj),
        in_specs=[row, row, row, hid, hid, vec, vec, wcol, wcol, wrow],
        out_specs=(row, hid, hid, row, vec, vec),
        out_shape=(SDS((S, D), f32), SDS((N_DEV, S, F8), MXU_DTYPE), SDS((N_DEV, S, F8), MXU_DTYPE),
                   SDS((S, D), MXU_DTYPE), SDS((1, D), f32), SDS((1, D), f32)),
        scratch_shapes=[pltpu.VMEM((tm, D), f32)],
        compiler_params=_params(2),
    )(dxo, x, y, a, b, gpre, gpost, wg, wu, wd)


def ffn_bwd_dw(h, dy, a, b, da, db, name):
    S, D = h.shape
    F8 = a.shape[-1]
    tm = _row_tile(S, 512)
    ni = S // tm

    def body(h_ref, dy_ref, a_ref, b_ref, da_ref, db_ref, dwg_ref, dwu_ref, dwd_ref, accg, accu, accd):
        i = pl.program_id(1)

        @pl.when(i == 0)
        def _():
            accg[...] = jnp.zeros_like(accg)
            accu[...] = jnp.zeros_like(accu)
            accd[...] = jnp.zeros_like(accd)

        t = _c(_silu(a_ref[...].astype(f32)) * b_ref[...].astype(f32))
        hh = h_ref[...]
        accg[...] += _mm_tn(hh, da_ref[...])
        accu[...] += _mm_tn(hh, db_ref[...])
        accd[...] += _mm_tn(t, dy_ref[...])

        @pl.when(i == ni - 1)
        def _():
            dwg_ref[...] = accg[...].astype(dwg_ref.dtype)
            dwu_ref[...] = accu[...].astype(dwu_ref.dtype)
            dwd_ref[...] = accd[...].astype(dwd_ref.dtype)

    row = pl.BlockSpec((tm, D), lambda j, i: (i, 0))
    hid = pl.BlockSpec((None, tm, F8), lambda j, i: (j, i, 0))
    wcol = pl.BlockSpec((None, D, F8), lambda j, i: (j, 0, 0))
    wrow = pl.BlockSpec((None, F8, D), lambda j, i: (j, 0, 0))
    return pl.pallas_call(
        body, name=name, grid=(N_DEV, ni),
        in_specs=[row, row, hid, hid, hid, hid],
        out_specs=(wcol, wcol, wrow),
        out_shape=(SDS((N_DEV, D, F8), MXU_DTYPE), SDS((N_DEV, D, F8), MXU_DTYPE), SDS((N_DEV, F8, D), MXU_DTYPE)),
        scratch_shapes=[pltpu.VMEM((D, F8), f32), pltpu.VMEM((D, F8), f32), pltpu.VMEM((F8, D), f32)],
        compiler_params=_params(2),
    )(h, dy, a, b, da, db)


def rms_mm(x, g, w, w2, name, tn=1024):
    S, D = x.shape
    N = w.shape[1]
    tm = _row_tile(S, 512)
    tn = _row_tile(N, tn)
    has2 = w2 is not None

    def body(*refs):
        if has2:
            x_ref, g_ref, w_ref, w2_ref, h_ref, o_ref, o2_ref = refs
        else:
            x_ref, g_ref, w_ref, h_ref, o_ref = refs
        j = pl.program_id(1)

        @pl.when(j == 0)
        def _():
            h = _rms(x_ref[...], g_ref[...]).astype(h_ref.dtype)
            h_ref[...] = h
            if has2:
                o2_ref[...] = _mm(h, w2_ref[...])

        o_ref[...] = _mm(h_ref[...], w_ref[...])

    row = pl.BlockSpec((tm, D), lambda i, j: (i, 0))
    in_specs = [row, pl.BlockSpec((1, D), lambda i, j: (0, 0)), pl.BlockSpec((D, tn), lambda i, j: (0, j))]
    out_specs = [row, pl.BlockSpec((tm, tn), lambda i, j: (i, j))]
    out_shape = [SDS((S, D), MXU_DTYPE), SDS((S, N), f32)]
    args = [x, g, w]
    if has2:
        in_specs.append(pl.BlockSpec((D, w2.shape[1]), lambda i, j: (0, 0)))
        out_specs.append(pl.BlockSpec((tm, w2.shape[1]), lambda i, j: (i, 0)))
        out_shape.append(SDS((S, w2.shape[1]), f32))
        args.append(w2)
    return pl.pallas_call(
        body, name=name, grid=(S // tm, N // tn), in_specs=in_specs, out_specs=tuple(out_specs),
        out_shape=tuple(out_shape), compiler_params=_params(2),
    )(*args)


def mm_bwd_dx(dres, x, g, dy, w, dy2, w2, name, tk=1024):
    S, D = x.shape
    K = dy.shape[1]
    tm = _row_tile(S, 512)
    tk = _row_tile(K, tk)
    nk = K // tk
    has2 = dy2 is not None

    def body(*refs):
        if has2:
            dres_ref, x_ref, g_ref, dy_ref, w_ref, dy2_ref, w2_ref, dx_ref, dg_ref, dh_ref = refs
        else:
            dres_ref, x_ref, g_ref, dy_ref, w_ref, dx_ref, dg_ref, dh_ref = refs
        i, k = pl.program_id(0), pl.program_id(1)

        @pl.when(k == 0)
        def _():
            @pl.when(i == 0)
            def _():
                dg_ref[...] = jnp.zeros_like(dg_ref)

            if has2:
                dh_ref[...] = _mm_nt(dy2_ref[...], w2_ref[...])
            else:
                dh_ref[...] = jnp.zeros_like(dh_ref)

        dh_ref[...] += _mm_nt(dy_ref[...], w_ref[...])

        @pl.when(k == nk - 1)
        def _():
            dxx, dg = _rms_bwd(x_ref[...], g_ref[...], dh_ref[...])
            dx_ref[...] = dres_ref[...] + dxx
            dg_ref[...] += dg

    row = pl.BlockSpec((tm, D), lambda i, k: (i, 0))
    vec = pl.BlockSpec((1, D), lambda i, k: (0, 0))
    in_specs = [row, row, vec, pl.BlockSpec((tm, tk), lambda i, k: (i, k)), pl.BlockSpec((D, tk), lambda i, k: (0, k))]
    args = [dres, x, g, dy, w]
    if has2:
        in_specs += [pl.BlockSpec((tm, dy2.shape[1]), lambda i, k: (i, 0)),
                     pl.BlockSpec((D, w2.shape[1]), lambda i, k: (0, 0))]
        args += [dy2, w2]
    return pl.pallas_call(
        body, name=name, grid=(S // tm, nk), in_specs=in_specs, out_specs=(row, vec),
        out_shape=(SDS((S, D), f32), SDS((1, D), f32)),
        scratch_shapes=[pltpu.VMEM((tm, D), f32)], compiler_params=_params(2),
    )(*args)


def tn_mm(a, b, name, tn=512, slot_major=False):
    S, K1 = a.shape
    N = b.shape[1]
    tm = _row_tile(S, 512)
    tn = _row_tile(N, tn)
    ni = S // tm

    def body(a_ref, b_ref, o_ref, acc):
        i = pl.program_id(1)

        @pl.when(i == 0)
        def _():
            acc[...] = jnp.zeros_like(acc)

        acc[...] += _mm_tn(a_ref[...], b_ref[...])

        @pl.when(i == ni - 1)
        def _():
            o_ref[...] = acc[...].astype(o_ref.dtype)

    if slot_major:
        out_spec, out_shape = pl.BlockSpec((None, K1, tn), lambda j, i: (j, 0, 0)), SDS((N // tn, K1, tn), MXU_DTYPE)
    else:
        out_spec, out_shape = pl.BlockSpec((K1, tn), lambda j, i: (0, j)), SDS((K1, N), MXU_DTYPE)
    return pl.pallas_call(
        body, name=name, grid=(N // tn, ni),
        in_specs=[pl.BlockSpec((tm, K1), lambda j, i: (i, 0)), pl.BlockSpec((tm, tn), lambda j, i: (i, j))],
        out_specs=out_spec, out_shape=out_shape,
        scratch_shapes=[pltpu.VMEM((K1, tn), f32)], compiler_params=_params(2),
    )(a, b)


CONV_ROWS = 512


def _shift_down(cur, prev8, s):
    r = pltpu.roll(cur, s, 0)
    row = lax.broadcasted_iota(jnp.int32, (8, cur.shape[1]), 0)
    top = jnp.where(row < s, pltpu.roll(prev8, s, 0), r[0:8])
    return jnp.concatenate([top, r[8:]], axis=0)


def _shift_up(cur, next8, s):
    n = cur.shape[0]
    r = pltpu.roll(cur, n - s, 0)
    row = lax.broadcasted_iota(jnp.int32, (8, cur.shape[1]), 0)
    bot = jnp.where(row >= 8 - s, pltpu.roll(next8, 8 - s, 0), r[n - 8:])
    return jnp.concatenate([r[:n - 8], bot], axis=0)


def _conv_taps(cur, prev8):
    return [_shift_down(cur, prev8, 3), _shift_down(cur, prev8, 2), _shift_down(cur, prev8, 1), cur]


def _act_qk(c):
    a = _silu(c)
    return a * lax.rsqrt(jnp.sum(a * a, axis=-1, keepdims=True) + L2_EPS)


def dn_prep(proj, conv_w, name):
    S = proj.shape[0]
    W = conv_w.shape[1] // 3
    nh = W // HEAD
    R = _row_tile(S, CONV_ROWS)

    def body(p_ref, w_ref, o_ref):
        j = pl.program_id(0)
        w = w_ref[...]

        def rows(r, prev8):
            cur = p_ref[pl.ds(r, R), :]
            taps = _conv_taps(cur, prev8)
            cv = taps[0] * w[0:1] + taps[1] * w[1:2] + taps[2] * w[2:3] + taps[3] * w[3:4]

            @pl.when(j < 2 * nh)
            def _():
                o_ref[pl.ds(r, R), :] = _act_qk(cv)

            @pl.when(j >= 2 * nh)
            def _():
                o_ref[pl.ds(r, R), :] = _silu(cv)

        rows(0, jnp.zeros((8, HEAD), f32))

        @pl.loop(1, S // R)
        def _(t):
            r = pl.multiple_of(t * R, R)
            rows(r, p_ref[pl.ds(r - 8, 8), :])

    return pl.pallas_call(
        body, name=name, grid=(3 * nh,),
        in_specs=[pl.BlockSpec((S, HEAD), lambda j: (0, j)), pl.BlockSpec((CONV_K, HEAD), lambda j: (0, j))],
        out_specs=pl.BlockSpec((None, S, HEAD), lambda j: (j // nh, 0, j % nh)),
        out_shape=SDS((3, S, W), f32), compiler_params=_params(1),
    )(proj, conv_w)


def dn_prep_bwd(proj, conv_w, dqkv, dz, name):
    S = proj.shape[0]
    W = conv_w.shape[1] // 3
    nh = W // HEAD
    nq = 3 * nh
    R = _row_tile(S, CONV_ROWS)
    nr = S // R

    def body(p_ref, w_ref, dq_ref, dz_ref, dp_ref, dw_ref, dc_ref):
        j = pl.program_id(0)

        @pl.when(j >= nq)
        def _():
            dp_ref[...] = dz_ref[...].astype(dp_ref.dtype)

        @pl.when(j < nq)
        def _():
            w = w_ref[...]
            dw_ref[...] = jnp.zeros_like(dw_ref)

            def rows(r, prev8):
                cur = p_ref[pl.ds(r, R), :]
                taps = _conv_taps(cur, prev8)
                cv = taps[0] * w[0:1] + taps[1] * w[1:2] + taps[2] * w[2:3] + taps[3] * w[3:4]
                dn = dq_ref[pl.ds(r, R), :]

                @pl.when(j < 2 * nh)
                def _():
                    dc_ref[pl.ds(r, R), :] = jax.vjp(_act_qk, cv)[1](dn)[0]

                @pl.when(j >= 2 * nh)
                def _():
                    dc_ref[pl.ds(r, R), :] = jax.vjp(_silu, cv)[1](dn)[0]

                dc = dc_ref[pl.ds(r, R), :]
                dw_ref[...] += jnp.concatenate(
                    [jnp.sum(dc * taps[q], axis=0, keepdims=True) for q in range(CONV_K)], axis=0)

            rows(0, jnp.zeros((8, HEAD), f32))

            @pl.loop(1, nr)
            def _(t):
                r = pl.multiple_of(t * R, R)
                rows(r, p_ref[pl.ds(r - 8, 8), :])

            def back(r, next8):
                dc = dc_ref[pl.ds(r, R), :]
                dx = dc * w[3:4]
                for s in (1, 2, 3):
                    dx = dx + _shift_up(dc, next8, s) * w[3 - s:4 - s]
                dp_ref[pl.ds(r, R), :] = dx.astype(dp_ref.dtype)

            @pl.loop(0, nr - 1)
            def _(t):
                r = pl.multiple_of(t * R, R)
                back(r, dc_ref[pl.ds(r + R, 8), :])

            back((nr - 1) * R, jnp.zeros((8, HEAD), f32))

    clamp = lambda j: jnp.minimum(j, nq - 1)
    return pl.pallas_call(
        body, name=name, grid=(4 * nh,),
        in_specs=[pl.BlockSpec((S, HEAD), lambda j: (0, clamp(j))),
                  pl.BlockSpec((CONV_K, HEAD), lambda j: (0, clamp(j))),
                  pl.BlockSpec((None, S, HEAD), lambda j: (clamp(j) // nh, 0, clamp(j) % nh)),
                  pl.BlockSpec((S, HEAD), lambda j: (0, jnp.maximum(j - nq, 0)))],
        out_specs=(pl.BlockSpec((S, HEAD), lambda j: (0, j)), pl.BlockSpec((CONV_K, HEAD), lambda j: (0, clamp(j)))),
        out_shape=(SDS((S, 4 * W), MXU_DTYPE), SDS((CONV_K, 3 * W), f32)),
        scratch_shapes=[pltpu.VMEM((S, HEAD), f32)], compiler_params=_params(1),
    )(proj, conv_w, dqkv, dz)


def _gate_fns(braw, araw, alog, dtb):
    beta = jax.nn.sigmoid(braw)
    g = -jnp.exp(alog) * jax.nn.softplus(araw + dtb)
    return beta, g


def _lane_pick(x, lane):
    sel = lax.broadcasted_iota(jnp.int32, x.shape, 1) == lane
    return jnp.broadcast_to(jnp.sum(jnp.where(sel, x, 0.0), axis=1, keepdims=True), x.shape)


CUM_ROWS = 512


def _chunk_cumsum_matrix(n, transpose):
    r, c = lax.broadcasted_iota(jnp.int32, (n, n), 0), lax.broadcasted_iota(jnp.int32, (n, n), 1)
    sh = int(math.log2(DN_CHUNK))
    same = (r >> sh) == (c >> sh)
    return jnp.where(same & ((r <= c) if transpose else (r >= c)), 1.0, 0.0).astype(f32)


def dn_gates(pba, alog_b, dtb_b, name):
    S = pba.shape[0]
    H = alog_b.shape[0]
    R = _row_tile(S, CUM_ROWS)

    def body(p_ref, al_ref, dt_ref, beta_ref, g_ref):
        h = pl.program_id(0)
        p = p_ref[...]
        beta, g = _gate_fns(_lane_pick(p, h), _lane_pick(p, H + h), al_ref[...], dt_ref[...])
        beta_ref[...] = beta
        g_ref[...] = g
        cum = _chunk_cumsum_matrix(R, False)

        @pl.loop(0, S // R)
        def _(t):
            r = pl.multiple_of(t * R, R)
            g_ref[pl.ds(r, R), :] = _hp(cum, g_ref[pl.ds(r, R), :], NN)

    par = pl.BlockSpec((None, 1, HEAD), lambda h: (h, 0, 0))
    out = pl.BlockSpec((None, S, HEAD), lambda h: (h, 0, 0))
    return pl.pallas_call(
        body, name=name, grid=(H,), in_specs=[pl.BlockSpec((S, HEAD), lambda h: (0, 0)), par, par],
        out_specs=(out, out), out_shape=(SDS((H, S, HEAD), f32), SDS((H, S, HEAD), f32)), compiler_params=_params(1),
    )(pba, alog_b, dtb_b)


def dn_gates_bwd(pba, alog_b, dtb_b, dbeta, dg, name):
    S = pba.shape[0]
    H = alog_b.shape[0]
    R = _row_tile(S, CUM_ROWS)

    def body(p_ref, al_ref, dt_ref, dbeta_ref, dg_ref, dp_ref, dal_ref, ddt_ref, acc, dgs):
        h = pl.program_id(0)

        @pl.when(h == 0)
        def _():
            acc[...] = jnp.zeros_like(acc)

        cum_t = _chunk_cumsum_matrix(R, True)

        @pl.loop(0, S // R)
        def _(t):
            r = pl.multiple_of(t * R, R)
            dgs[pl.ds(r, R), :] = _hp(cum_t, dg_ref[pl.ds(r, R), :], NN)

        p = p_ref[...]
        db = jnp.broadcast_to(jnp.sum(dbeta_ref[...], axis=1, keepdims=True), p.shape)
        dgg = jnp.broadcast_to(jnp.sum(dgs[...], axis=1, keepdims=True), p.shape)
        _, vjp = jax.vjp(_gate_fns, _lane_pick(p, h), _lane_pick(p, H + h), al_ref[...], dt_ref[...])
        dbraw, daraw, dal, ddt = vjp((db, dgg))
        lane = lax.broadcasted_iota(jnp.int32, p.shape, 1)
        acc[...] += jnp.where(lane == h, dbraw, 0.0) + jnp.where(lane == H + h, daraw, 0.0)
        dal_ref[...] = dal
        ddt_ref[...] = ddt

        @pl.when(h == H - 1)
        def _():
            dp_ref[...] = acc[...].astype(dp_ref.dtype)

    par = pl.BlockSpec((None, 1, HEAD), lambda h: (h, 0, 0))
    big = pl.BlockSpec((None, S, HEAD), lambda h: (h, 0, 0))
    full = pl.BlockSpec((S, HEAD), lambda h: (0, 0))
    return pl.pallas_call(
        body, name=name, grid=(H,), in_specs=[full, par, par, big, big],
        out_specs=(full, par, par),
        out_shape=(SDS((S, HEAD), MXU_DTYPE), SDS((H, 1, HEAD), f32), SDS((H, 1, HEAD), f32)),
        scratch_shapes=[pltpu.VMEM((S, HEAD), f32), pltpu.VMEM((S, HEAD), f32)], compiler_params=_params(1),
    )(pba, alog_b, dtb_b, dbeta, dg)


def _bdot(dims):
    back = {NN: ((NT, 'gb'), (TN, 'ag')), NT: ((NN, 'gb'), (TN, 'ga')), TN: ((NT, 'bg'), (NN, 'ag'))}[dims]
    d = lambda p, q, dm: lax.dot_general(_c(p), _c(q), (dm, ((), ())), preferred_element_type=f32)

    @jax.custom_vjp
    def f(a, b):
        return d(a, b, dims)

    def fwd(a, b):
        return d(a, b, dims), (a, b)

    def bwd(res, g):
        v = {'a': res[0], 'b': res[1], 'g': g}
        (da_dims, da_ops), (db_dims, db_ops) = back
        return d(v[da_ops[0]], v[da_ops[1]], da_dims), d(v[db_ops[0]], v[db_ops[1]], db_dims)

    f.defvjp(fwd, bwd)
    return f, lambda a, b: d(a, b, dims)


_BDOT = {dims: _bdot(dims) for dims in (NN, NT, TN)}


def _tri_inv_multi(Ls):
    n = Ls[0].shape[0]
    eye = jnp.where(lax.broadcasted_iota(jnp.int32, (n, n), 0) == lax.broadcasted_iota(jnp.int32, (n, n), 1), 1.0, 0.0)
    P = tuple(-L for L in Ls)
    T = tuple(eye + p for p in P)
    for _ in range(int(math.log2(n)) - 1):
        P = tuple(_dot3(p, p, NN) for p in P)
        T = tuple(t + _dot3(t, p, NN) for t, p in zip(T, P))
    return T


@jax.custom_vjp
def _tri_inv_multi_vjp(Ls):
    return _tri_inv_multi(Ls)


def _tri_inv_fwd(Ls):
    T = _tri_inv_multi(Ls)
    return T, T


def _tri_inv_bwd(T, dT):
    X = tuple(_dot3(d, t, NT) for d, t in zip(dT, T))
    return (tuple(-_dot3(t, x, TN) for t, x in zip(T, X)),)


_tri_inv_multi_vjp.defvjp(_tri_inv_fwd, _tri_inv_bwd)


def _pieces3(x):
    h1 = x.astype(MXU_DTYPE)
    r1 = x - h1.astype(f32)
    h2 = r1.astype(MXU_DTYPE)
    return h1, h2, (r1 - h2.astype(f32)).astype(MXU_DTYPE)


def _row_bcast_impl(sel_row, gc):
    s = _c(sel_row)
    d = lambda p: lax.dot_general(s, p, (NT, ((), ())), preferred_element_type=f32)
    h1, h2, h3 = _pieces3(gc)
    return (d(h1) + d(h2)) + d(h3)


def _row_bcast_bwd(sel_row, d):
    return jnp.zeros_like(sel_row), lax.dot_general(_c(d), _c(sel_row), (TN, ((), ())), preferred_element_type=f32)


_row_bcast = jax.custom_vjp(_row_bcast_impl)
_row_bcast.defvjp(lambda sel_row, gc: (_row_bcast_impl(sel_row, gc), sel_row), _row_bcast_bwd)


def _col_bcast_impl(gc):
    return gc[:, :DN_CHUNK]


def _col_bcast_bwd(_, d):
    return (jnp.broadcast_to(jnp.sum(d, axis=1, keepdims=True) * (1.0 / HEAD), (d.shape[0], HEAD)),)


_col_bcast = jax.custom_vjp(_col_bcast_impl)
_col_bcast.defvjp(lambda gc: (_col_bcast_impl(gc), None), _col_bcast_bwd)


def _last_row_bcast(n):
    def impl(gc):
        return jnp.broadcast_to(gc[DN_CHUNK - 1:DN_CHUNK, :], (n, HEAD))

    def bwd(_, d):
        row = lax.broadcasted_iota(jnp.int32, (DN_CHUNK, HEAD), 0)
        return (jnp.where(row == DN_CHUNK - 1, jnp.sum(d, axis=0, keepdims=True), 0.0),)

    f = jax.custom_vjp(impl)
    f.defvjp(lambda gc: (impl(gc), None), bwd)
    return impl, f


_LAST_C, _LAST_H = _last_row_bcast(DN_CHUNK), _last_row_bcast(HEAD)


def _chunk_consts():
    C = DN_CHUNK
    io = lambda shape, ax: lax.broadcasted_iota(jnp.int32, shape, ax)
    one = lambda m: jnp.where(m, 1.0, 0.0).astype(f32)
    r, c = io((C, C), 0), io((C, C), 1)
    return dict(causal=r >= c, strict=r > c, sel_row=one(io((C, HEAD), 1) == 0))


def _chunk_fn(kc, diff, q, k, v, gc, bB, S0):
    i = 0 if diff else 1
    mm, mm_nt, mm_tn = _BDOT[NN][i], _BDOT[NT][i], _BDOT[TN][i]
    tri = _tri_inv_multi_vjp if diff else _tri_inv_multi
    each = lambda f, *ls: tuple(f(*a) for a in zip(*ls))
    gcol = each(_col_bcast if diff else _col_bcast_impl, gc)
    grow = each(lambda g: (_row_bcast if diff else _row_bcast_impl)(kc['sel_row'], g), gc)
    glc = each(_LAST_C[i ^ 1], gc)
    glh = each(_LAST_H[i ^ 1], gc)
    decay = each(lambda a, b: jnp.where(kc['causal'], jnp.exp(jnp.where(kc['causal'], a - b, 0.0)), 0.0), gcol, grow)
    kb = each(lambda a, b: a * b, k, bB)
    vb = each(lambda a, b: a * b, v, bB)
    egc = each(jnp.exp, gc)
    kk = each(mm_nt, kb, k)
    T = tri(each(lambda a, d: jnp.where(kc['strict'], a * d, 0.0), kk, decay))
    u = each(mm, T, vb)
    w = each(mm, T, each(lambda a, b: a * b, kb, egc))
    qs = each(lambda a: a * (HEAD ** -0.5), q)
    qk = each(mm_nt, qs, k)
    attn = each(lambda a, d: jnp.where(kc['causal'], a * d, 0.0), qk, decay)
    wS = each(mm, w, S0)
    qS = each(mm, each(lambda a, b: a * b, qs, egc), S0)
    v_new = each(lambda a, b: a - b, u, wS)
    o = each(lambda a, b: a + b, qS, each(mm, attn, v_new))
    kdec = each(lambda a, gl, g: a * jnp.exp(gl - g), k, glc, gc)
    S1 = each(lambda s, gl, kv: s * jnp.exp(gl) + kv, S0, glh, each(mm_tn, kdec, v_new))
    return o, S1


def _heads_per_block(H):
    return 8 if H % 8 == 0 else (4 if H % 4 == 0 else 1)


def dn_chunk_fwd(qkv, gB, bB, name):
    _, S, W = qkv.shape
    H, C = W // HEAD, DN_CHUNK
    N, HB = S // C, _heads_per_block(H)

    def body(q_ref, k_ref, v_ref, g_ref, b_ref, o_ref, st_ref, s_scr):
        @pl.when(pl.program_id(1) == 0)
        def _():
            s_scr[...] = jnp.zeros_like(s_scr)

        kc = _chunk_consts()
        sls = [slice(hh * HEAD, (hh + 1) * HEAD) for hh in range(HB)]
        heads = lambda ref: tuple(ref[:, sl] for sl in sls)
        S0 = tuple(s_scr[hh] for hh in range(HB))
        for hh in range(HB):
            st_ref[hh] = S0[hh]
        o, S1 = _chunk_fn(kc, False, heads(q_ref), heads(k_ref), heads(v_ref), tuple(g_ref[hh] for hh in range(HB)),
                          tuple(b_ref[hh] for hh in range(HB)), S0)
        for hh in range(HB):
            o_ref[:, sls[hh]] = o[hh]
            s_scr[hh] = S1[hh]

    part = lambda p: pl.BlockSpec((None, C, HB * HEAD), lambda hb, n: (p, n, hb))
    gate = pl.BlockSpec((HB, C, HEAD), lambda hb, n: (hb, n, 0))
    return pl.pallas_call(
        body, name=name, grid=(H // HB, N), in_specs=[part(0), part(1), part(2), gate, gate],
        out_specs=(pl.BlockSpec((C, HB * HEAD), lambda hb, n: (n, hb)),
                   pl.BlockSpec((None, HB, HEAD, HEAD), lambda hb, n: (n, hb, 0, 0))),
        out_shape=(SDS((S, W), f32), SDS((N, H, HEAD, HEAD), f32)),
        scratch_shapes=[pltpu.VMEM((HB, HEAD, HEAD), f32)], compiler_params=_params(2),
    )(qkv, qkv, qkv, gB, bB)


def dn_chunk_bwd(qkv, gB, bB, states, do, name):
    _, S, W = qkv.shape
    H, C = W // HEAD, DN_CHUNK
    N, HB = S // C, _heads_per_block(H)

    def body(q_ref, k_ref, v_ref, g_ref, b_ref, st_ref, do_ref, dqkv_ref, dg_ref, db_ref, ds_scr):
        @pl.when(pl.program_id(1) == 0)
        def _():
            ds_scr[...] = jnp.zeros_like(ds_scr)

        kc = _chunk_consts()
        sls = [slice(hh * HEAD, (hh + 1) * HEAD) for hh in range(HB)]
        heads = lambda ref: tuple(ref[:, sl] for sl in sls)
        lead = lambda ref: tuple(ref[hh] for hh in range(HB))
        _, vjp = jax.vjp(functools.partial(_chunk_fn, kc, True), heads(q_ref), heads(k_ref), heads(v_ref),
                         lead(g_ref), lead(b_ref), lead(st_ref))
        dq, dk, dv, dg, db, dS0 = vjp((heads(do_ref), lead(ds_scr)))
        for hh in range(HB):
            dqkv_ref[0, :, sls[hh]] = dq[hh]
            dqkv_ref[1, :, sls[hh]] = dk[hh]
            dqkv_ref[2, :, sls[hh]] = dv[hh]
            dg_ref[hh] = dg[hh]
            db_ref[hh] = db[hh]
            ds_scr[hh] = dS0[hh]

    rev = lambda n: N - 1 - n
    part = lambda p: pl.BlockSpec((None, C, HB * HEAD), lambda hb, n: (p, rev(n), hb))
    gate = pl.BlockSpec((HB, C, HEAD), lambda hb, n: (hb, rev(n), 0))
    return pl.pallas_call(
        body, name=name, grid=(H // HB, N),
        in_specs=[part(0), part(1), part(2), gate, gate,
                  pl.BlockSpec((None, HB, HEAD, HEAD), lambda hb, n: (rev(n), hb, 0, 0)),
                  pl.BlockSpec((C, HB * HEAD), lambda hb, n: (rev(n), hb))],
        out_specs=(pl.BlockSpec((3, C, HB * HEAD), lambda hb, n: (0, rev(n), hb)), gate, gate),
        out_shape=(SDS((3, S, W), f32), SDS((H, S, HEAD), f32), SDS((H, S, HEAD), f32)),
        scratch_shapes=[pltpu.VMEM((HB, HEAD, HEAD), f32)], compiler_params=_params(2),
    )(qkv, qkv, qkv, gB, bB, states, do)


def _gate_norm(o, z, ng):
    return _rms(o, ng) * _silu(z)


def dn_out(o, proj, ng, wout, x1, g3, name):
    S, W = o.shape
    D = x1.shape[1]
    nh = W // HEAD
    tm = _row_tile(S, 256)

    def body(o_ref, z_ref, ng_ref, w_ref, x_ref, g_ref, xo_ref, m_ref, og_ref):
        for h in range(nh):
            sl = slice(h * HEAD, (h + 1) * HEAD)
            og_ref[:, sl] = _gate_norm(o_ref[:, sl], z_ref[:, sl], ng_ref[...]).astype(og_ref.dtype)
        m = _mm(og_ref[...], w_ref[...])
        m_ref[...] = m
        xo_ref[...] = x_ref[...] + _rms(m, g_ref[...])

    rw = pl.BlockSpec((tm, W), lambda i: (i, 0))
    rd = pl.BlockSpec((tm, D), lambda i: (i, 0))
    return pl.pallas_call(
        body, name=name, grid=(S // tm,),
        in_specs=[rw, pl.BlockSpec((tm, W), lambda i: (i, 3)), pl.BlockSpec((1, HEAD), lambda i: (0, 0)),
                  pl.BlockSpec((W, D), lambda i: (0, 0)), rd, pl.BlockSpec((1, D), lambda i: (0, 0))],
        out_specs=(rd, rd, rw),
        out_shape=(SDS((S, D), f32), SDS((S, D), f32), SDS((S, W), MXU_DTYPE)), compiler_params=_params(1),
    )(o, proj, ng, wout, x1, g3)


def dn_out_bwd(dxo, m, g3, o, proj, ng, wout, name):
    S, W = o.shape
    D = m.shape[1]
    nh = W // HEAD
    tm = _row_tile(S, 256)

    def body(dxo_ref, m_ref, g_ref, o_ref, z_ref, ng_ref, w_ref, dm_ref, do_ref, dz_ref, dng_ref, dg_ref):
        @pl.when(pl.program_id(0) == 0)
        def _():
            dng_ref[...] = jnp.zeros_like(dng_ref)
            dg_ref[...] = jnp.zeros_like(dg_ref)

        dm, dg = _rms_bwd(m_ref[...], g_ref[...], dxo_ref[...])
        dg_ref[...] += dg
        dmc = dm.astype(dm_ref.dtype)
        dm_ref[...] = dmc
        dog = _mm_nt(dmc, w_ref[...])
        for h in range(nh):
            sl = slice(h * HEAD, (h + 1) * HEAD)
            _, vjp = jax.vjp(_gate_norm, o_ref[:, sl], z_ref[:, sl], ng_ref[...])
            do, dz, dng = vjp(dog[:, sl])
            do_ref[:, sl] = do
            dz_ref[:, sl] = dz.astype(dz_ref.dtype)
            dng_ref[...] += dng

    rw = pl.BlockSpec((tm, W), lambda i: (i, 0))
    rd = pl.BlockSpec((tm, D), lambda i: (i, 0))
    vd = pl.BlockSpec((1, D), lambda i: (0, 0))
    vh = pl.BlockSpec((1, HEAD), lambda i: (0, 0))
    return pl.pallas_call(
        body, name=name, grid=(S // tm,),
        in_specs=[rd, rd, vd, rw, pl.BlockSpec((tm, W), lambda i: (i, 3)), vh, pl.BlockSpec((W, D), lambda i: (0, 0))],
        out_specs=(rd, rw, rw, vh, vd),
        out_shape=(SDS((S, D), MXU_DTYPE), SDS((S, W), f32), SDS((S, W), MXU_DTYPE), SDS((1, HEAD), f32),
                   SDS((1, D), f32)),
        compiler_params=_params(1),
    )(dxo, m, g3, o, proj, ng, wout)


def _sg_stage1(pu, pv, bu, bv, lg, lb):
    u = _gelu(pu + bu)
    t = _gelu(pv + bv)
    tc = t - jnp.mean(t, axis=-1, keepdims=True)
    v = tc * lax.rsqrt(jnp.mean(tc * tc, axis=-1, keepdims=True) + LN_EPS) * lg + lb
    return u, v


def _causal_mask(n):
    return lax.broadcasted_iota(jnp.int32, (n, n), 0) >= lax.broadcasted_iota(jnp.int32, (n, n), 1)


def sg_mid(pre, b_in, ln_g, ln_b, w_s, bsT, wout, x1, g3, name):
    S = pre.shape[0]
    E, D = ln_g.shape[1], x1.shape[1]
    G, CH = SG_GROUPS, SG_CHUNK
    Cg = E // G
    tm = _row_tile(S, 256)

    def body(pu_ref, pv_ref, bu_ref, bv_ref, lg_ref, lb_ref, ws_ref, bs_ref, w_ref, x_ref, g_ref,
             xo_ref, m_ref, gt_ref):
        u, v = _sg_stage1(pu_ref[...], pv_ref[...], bu_ref[...], bv_ref[...], lg_ref[...], lb_ref[...])
        mask = _causal_mask(CH)
        for g in range(G):
            wc = _c(jnp.where(mask, ws_ref[g], 0.0))
            bcol = bs_ref[:, g:g + 1]
            cs = slice(g * Cg, (g + 1) * Cg)
            for ch in range(tm // CH):
                rs = slice(ch * CH, (ch + 1) * CH)
                mixed = _mm(wc, _c(v[rs, cs])) + bcol
                gt_ref[rs, cs] = (u[rs, cs] * mixed).astype(gt_ref.dtype)
        m = _mm(gt_ref[...], w_ref[...])
        m_ref[...] = m
        xo_ref[...] = x_ref[...] + _rms(m, g_ref[...])

    half = lambda p: pl.BlockSpec((tm, E), lambda i: (i, p))
    vhalf = lambda p: pl.BlockSpec((1, E), lambda i: (0, p))
    ve = pl.BlockSpec((1, E), lambda i: (0, 0))
    rd = pl.BlockSpec((tm, D), lambda i: (i, 0))
    return pl.pallas_call(
        body, name=name, grid=(S // tm,),
        in_specs=[half(0), half(1), vhalf(0), vhalf(1), ve, ve, pl.BlockSpec((G, CH, CH), lambda i: (0, 0, 0)),
                  pl.BlockSpec((CH, G), lambda i: (0, 0)), pl.BlockSpec((E, D), lambda i: (0, 0)), rd,
                  pl.BlockSpec((1, D), lambda i: (0, 0))],
        out_specs=(rd, rd, pl.BlockSpec((tm, E), lambda i: (i, 0))),
        out_shape=(SDS((S, D), f32), SDS((S, D), f32), SDS((S, E), MXU_DTYPE)), compiler_params=_params(1),
    )(pre, pre, b_in, b_in, ln_g, ln_b, w_s, bsT, wout, x1, g3)


def sg_mid_bwd(dxo, m, g3, pre, b_in, ln_g, ln_b, w_s, bsT, wout, name):
    S = pre.shape[0]
    E, D = ln_g.shape[1], m.shape[1]
    G, CH = SG_GROUPS, SG_CHUNK
    Cg = E // G
    tm = _row_tile(S, 256)

    def body(dxo_ref, m_ref, g_ref, pu_ref, pv_ref, bu_ref, bv_ref, lg_ref, lb_ref, ws_ref, bs_ref, w_ref,
             dm_ref, dpre_ref, dbin_ref, dlg_ref, dlb_ref, dws_ref, dbs_ref, dg_ref, du_scr, dv_scr):
        @pl.when(pl.program_id(0) == 0)
        def _():
            for r in (dbin_ref, dlg_ref, dlb_ref, dws_ref, dbs_ref, dg_ref):
                r[...] = jnp.zeros_like(r)

        dm, dg = _rms_bwd(m_ref[...], g_ref[...], dxo_ref[...])
        dg_ref[...] += dg
        dmc = dm.astype(dm_ref.dtype)
        dm_ref[...] = dmc
        dgated = _mm_nt(dmc, w_ref[...])
        (u, v), vjp1 = jax.vjp(_sg_stage1, pu_ref[...], pv_ref[...], bu_ref[...], bv_ref[...], lg_ref[...],
                               lb_ref[...])
        mask = _causal_mask(CH)
        lane = lax.broadcasted_iota(jnp.int32, (CH, CH), 1)
        for g in range(G):
            wc = _c(jnp.where(mask, ws_ref[g], 0.0))
            bcol = bs_ref[:, g:g + 1]
            cs = slice(g * Cg, (g + 1) * Cg)
            dws = jnp.zeros((CH, CH), f32)
            dbs = jnp.zeros((CH, 1), f32)
            for ch in range(tm // CH):
                rs = slice(ch * CH, (ch + 1) * CH)
                vs = _c(v[rs, cs])
                mixed = _mm(wc, vs) + bcol
                dgt = dgated[rs, cs]
                du_scr[rs, cs] = dgt * mixed
                dmixed = dgt * u[rs, cs]
                dmc2 = _c(dmixed)
                dv_scr[rs, cs] = _mm_tn(wc, dmc2)
                dws = dws + _mm_nt(dmc2, vs)
                dbs = dbs + jnp.sum(dmixed, axis=1, keepdims=True)
            dws_ref[g] += jnp.where(mask, dws, 0.0)
            dbs_ref[...] += jnp.where(lane == g, jnp.broadcast_to(dbs, (CH, CH)), 0.0)
        dpu, dpv, dbu, dbv, dlg, dlb = vjp1((du_scr[...], dv_scr[...]))
        dpre_ref[:, :E] = dpu.astype(dpre_ref.dtype)
        dpre_ref[:, E:] = dpv.astype(dpre_ref.dtype)
        dbin_ref[:, :E] += dbu
        dbin_ref[:, E:] += dbv
        dlg_ref[...] += dlg
        dlb_ref[...] += dlb

    half = lambda p: pl.BlockSpec((tm, E), lambda i: (i, p))
    vhalf = lambda p: pl.BlockSpec((1, E), lambda i: (0, p))
    ve = pl.BlockSpec((1, E), lambda i: (0, 0))
    rd = pl.BlockSpec((tm, D), lambda i: (i, 0))
    vd = pl.BlockSpec((1, D), lambda i: (0, 0))
    wsb = pl.BlockSpec((G, CH, CH), lambda i: (0, 0, 0))
    return pl.pallas_call(
        body, name=name, grid=(S // tm,),
        in_specs=[rd, rd, vd, half(0), half(1), vhalf(0), vhalf(1), ve, ve, wsb,
                  pl.BlockSpec((CH, G), lambda i: (0, 0)), pl.BlockSpec((E, D), lambda i: (0, 0))],
        out_specs=(rd, pl.BlockSpec((tm, 2 * E), lambda i: (i, 0)), pl.BlockSpec((1, 2 * E), lambda i: (0, 0)), ve, ve,
                   wsb, pl.BlockSpec((CH, CH), lambda i: (0, 0)), vd),
        out_shape=(SDS((S, D), MXU_DTYPE), SDS((S, 2 * E), MXU_DTYPE), SDS((1, 2 * E), f32), SDS((1, E), f32),
                   SDS((1, E), f32), SDS((G, CH, CH), f32), SDS((CH, CH), f32), SDS((1, D), f32)),
        scratch_shapes=[pltpu.VMEM((tm, E), f32), pltpu.VMEM((tm, E), f32)], compiler_params=_params(1),
    )(dxo, m, g3, pre, pre, b_in, b_in, ln_g, ln_b, w_s, bsT, wout)


def loss_head(y, target, name):
    S, D = y.shape
    tm = _row_tile(S, 512)

    def body(y_ref, t_ref, l_ref, d_ref):
        @pl.when(pl.program_id(0) == 0)
        def _():
            l_ref[...] = jnp.zeros_like(l_ref)

        e = y_ref[...] - t_ref[...]
        d_ref[...] = e * (1.0 / D)
        l_ref[...] += jnp.sum(e * e) * (0.5 / D)

    row = pl.BlockSpec((tm, D), lambda i: (i, 0))
    return pl.pallas_call(
        body, name=name, grid=(S // tm,), in_specs=[row, row],
        out_specs=(pl.BlockSpec((1, HEAD), lambda i: (0, 0)), row),
        out_shape=(SDS((1, HEAD), f32), SDS((S, D), f32)), compiler_params=_params(1),
    )(y, target)


def sum_slots(r, name):
    _, R, C = r.shape
    tr = _row_tile(R, 648 if R % 648 == 0 else R)

    def body(r_ref, o_ref):
        acc = r_ref[0].astype(f32)
        for s in range(1, N_DEV):
            acc = acc + r_ref[s].astype(f32)
        o_ref[...] = acc

    return pl.pallas_call(
        body, name=name, grid=(R // tr,), in_specs=[pl.BlockSpec((N_DEV, tr, C), lambda i: (0, i, 0))],
        out_specs=pl.BlockSpec((tr, C), lambda i: (i, 0)), out_shape=SDS((R, C), f32), compiler_params=_params(1),
    )(r)


def _adam_math(w, g, m, v):
    m = ADAM_B1 * m + (1.0 - ADAM_B1) * g
    v = ADAM_B2 * v + (1.0 - ADAM_B2) * (g * g)
    m_hat = m / (1.0 - ADAM_B1 ** ADAM_STEP)
    v_hat = v / (1.0 - ADAM_B2 ** ADAM_STEP)
    delta = -ADAM_LR * (m_hat / (jnp.sqrt(v_hat) + ADAM_EPS) + ADAM_WD * w)
    return delta, m, v


def adam_slots(w, r, m, v, name, tr):
    R, C = w.shape
    tr = _row_tile(R, tr)

    def body(w_ref, r_ref, m_ref, v_ref, g_ref, d_ref, mo_ref, vo_ref):
        g = r_ref[0].astype(f32)
        for s in range(1, N_DEV):
            g = g + r_ref[s].astype(f32)
        g_ref[...] = g
        d_ref[...], mo_ref[...], vo_ref[...] = _adam_math(w_ref[...], g, m_ref[...], v_ref[...])

    row = pl.BlockSpec((tr, C), lambda i: (i, 0))
    return pl.pallas_call(
        body, name=name, grid=(R // tr,), in_specs=[row, pl.BlockSpec((N_DEV, tr, C), lambda i: (0, i, 0)), row, row],
        out_specs=(row, row, row, row), out_shape=tuple(SDS((R, C), f32) for _ in range(4)),
        compiler_params=_params(1),
    )(w, r, m, v)


def adam_small(w, g, m, v, name):
    def body(w_ref, g_ref, m_ref, v_ref, d_ref, mo_ref, vo_ref):
        d_ref[...], mo_ref[...], vo_ref[...] = _adam_math(w_ref[...], g_ref[...], m_ref[...], v_ref[...])

    return pl.pallas_call(body, name=name, out_shape=tuple(SDS(w.shape, f32) for _ in range(3)))(w, g, m, v)


def _pack_rows(parts):
    rows, offs, r = [], [], 0
    for p in parts:
        flat = p.reshape(-1)
        n = -(-flat.shape[0] // HEAD)
        flat = jnp.pad(flat, (0, n * HEAD - flat.shape[0]))
        rows.append(flat.reshape(n, HEAD))
        offs.append((r, n))
        r += n
    pad = (-r) % 8
    if pad:
        rows.append(jnp.zeros((pad, HEAD), f32))
    return jnp.concatenate(rows, axis=0), offs


def kernel(x, norm_g, ffn_w_gate, ffn_w_up, ffn_w_down, dn_w_in, dn_conv_w, dn_a_log, dn_dt_bias, dn_norm_g, dn_w_out, sg_w_in, sg_b_in, sg_ln_g, sg_ln_b, sg_w_s, sg_b_s, sg_w_out, loss_target, m_norm_g, m_ffn_w_gate, m_ffn_w_up, m_ffn_w_down, m_dn_w_in, m_dn_conv_w, m_dn_a_log, m_dn_dt_bias, m_dn_norm_g, m_dn_w_out, m_sg_w_in, m_sg_b_in, m_sg_ln_g, m_sg_ln_b, m_sg_w_s, m_sg_b_s, m_sg_w_out, v_norm_g, v_ffn_w_gate, v_ffn_w_up, v_ffn_w_down, v_dn_w_in, v_dn_conv_w, v_dn_a_log, v_dn_dt_bias, v_dn_norm_g, v_dn_w_out, v_sg_w_in, v_sg_b_in, v_sg_ln_g, v_sg_ln_b, v_sg_w_s, v_sg_b_s, v_sg_w_out):
    weights = dict(norm_g=norm_g, ffn_w_gate=ffn_w_gate, ffn_w_up=ffn_w_up, ffn_w_down=ffn_w_down, dn_w_in=dn_w_in,
                   dn_conv_w=dn_conv_w, dn_a_log=dn_a_log, dn_dt_bias=dn_dt_bias, dn_norm_g=dn_norm_g,
                   dn_w_out=dn_w_out, sg_w_in=sg_w_in, sg_b_in=sg_b_in, sg_ln_g=sg_ln_g, sg_ln_b=sg_ln_b,
                   sg_w_s=sg_w_s, sg_b_s=sg_b_s, sg_w_out=sg_w_out)
    mom_m = dict(norm_g=m_norm_g, ffn_w_gate=m_ffn_w_gate, ffn_w_up=m_ffn_w_up, ffn_w_down=m_ffn_w_down,
                 dn_w_in=m_dn_w_in, dn_conv_w=m_dn_conv_w, dn_a_log=m_dn_a_log, dn_dt_bias=m_dn_dt_bias,
                 dn_norm_g=m_dn_norm_g, dn_w_out=m_dn_w_out, sg_w_in=m_sg_w_in, sg_b_in=m_sg_b_in,
                 sg_ln_g=m_sg_ln_g, sg_ln_b=m_sg_ln_b, sg_w_s=m_sg_w_s, sg_b_s=m_sg_b_s, sg_w_out=m_sg_w_out)
    mom_v = dict(norm_g=v_norm_g, ffn_w_gate=v_ffn_w_gate, ffn_w_up=v_ffn_w_up, ffn_w_down=v_ffn_w_down,
                 dn_w_in=v_dn_w_in, dn_conv_w=v_dn_conv_w, dn_a_log=v_dn_a_log, dn_dt_bias=v_dn_dt_bias,
                 dn_norm_g=v_dn_norm_g, dn_w_out=v_dn_w_out, sg_w_in=v_sg_w_in, sg_b_in=v_sg_b_in,
                 sg_ln_g=v_sg_ln_g, sg_ln_b=v_sg_ln_b, sg_w_s=v_sg_w_s, sg_b_s=v_sg_b_s, sg_w_out=v_sg_w_out)
    order = list(weights)

    xs = x[0]
    S, D = xs.shape
    F8 = ffn_w_gate.shape[-1]
    depth = norm_g.shape[0]
    W = dn_w_out.shape[1] * N_DEV
    H = W // HEAD
    E = sg_ln_g.shape[1] * N_DEV
    G, CH = sg_w_s.shape[1], sg_w_s.shape[2]
    c8 = dn_w_in.shape[2]
    me = _slot(lax.axis_index("x"), lax.axis_index("y"), lax.axis_index("c"))

    small_in, small_offs = _pack_rows([norm_g, dn_conv_w, sg_b_in, sg_ln_g, sg_ln_b])
    gathered = all_gather_multi(
        [_c(ffn_w_gate), _c(ffn_w_up), _c(ffn_w_down), _c(dn_w_in[0]), _c(dn_w_out[0]), _c(sg_w_in[0]),
         _c(sg_w_out[0]), small_in], name="gather_weights")
    wg_all, wu_all, wd_all, dnin_all, dnout_all, sgin_all, sgout_all, small_all = gathered

    def small_piece(i, shard_shape):
        r0, n = small_offs[i]
        sz = math.prod(shard_shape)
        return small_all[:, r0:r0 + n, :].reshape(N_DEV, n * HEAD)[:, :sz].reshape((N_DEV,) + tuple(shard_shape))

    ng_full = jnp.moveaxis(small_piece(0, norm_g.shape), 0, 2).reshape(depth, 6, D)
    conv_full = jnp.moveaxis(small_piece(1, dn_conv_w.shape[1:]), 0, 1).reshape(CONV_K, 3 * W)
    bin_full = small_piece(2, sg_b_in.shape[1:]).reshape(1, 2 * E)
    lng_full = small_piece(3, sg_ln_g.shape[1:]).reshape(1, E)
    lnb_full = small_piece(4, sg_ln_b.shape[1:]).reshape(1, E)
    dn_win = jnp.moveaxis(dnin_all, 0, 1).reshape(D, N_DEV * c8)
    dn_wmain = dn_win[:, :4 * W]
    dn_wba = jnp.pad(dn_win[:, 4 * W:], ((0, 0), (0, HEAD - 2 * H)))
    dn_wout = dnout_all.reshape(W, D)
    sg_win = jnp.moveaxis(sgin_all, 0, 1).reshape(D, 2 * E)
    sg_wout = sgout_all.reshape(E, D)
    alog_b = jnp.broadcast_to(dn_a_log.reshape(H, 1, 1), (H, 1, HEAD))
    dtb_b = jnp.broadcast_to(dn_dt_bias.reshape(H, 1, 1), (H, 1, HEAD))
    bsT = sg_b_s[0].T
    gvec = lambda l, k: ng_full[l, k].reshape(1, D)

    saved = []
    cur = xs
    for l in range(depth):
        sv = {}
        sv['x0'] = cur
        cur, sv['hA'], sv['aA'], sv['bA'], sv['yA'] = ffn_fwd(cur, gvec(l, 0), gvec(l, 1), wg_all, wu_all, wd_all, l, 0,
                                                              name=f"ffn_fwd_{l}a")
        sv['x1'] = cur
        if l % 2 == 0:
            sv['hM'], sv['proj'], sv['pba'] = rms_mm(cur, gvec(l, 2), dn_wmain, dn_wba, name=f"dn_in_{l}")
            sv['qkv'] = dn_prep(sv['proj'], conv_full, name=f"dn_prep_{l}")
            sv['beta'], sv['g'] = dn_gates(sv['pba'], alog_b, dtb_b, name=f"dn_gates_{l}")
            sv['o'], sv['states'] = dn_chunk_fwd(sv['qkv'], sv['g'], sv['beta'], name=f"dn_chunk_{l}")
            cur, sv['m'], sv['og'] = dn_out(sv['o'], sv['proj'], dn_norm_g, dn_wout, cur, gvec(l, 3), name=f"dn_out_{l}")
        else:
            sv['hM'], sv['pre'] = rms_mm(cur, gvec(l, 2), sg_win, None, name=f"sg_in_{l}")
            cur, sv['m'], sv['gated'] = sg_mid(sv['pre'], bin_full, lng_full, lnb_full, sg_w_s[0], bsT, sg_wout, cur,
                                               gvec(l, 3), name=f"sg_mid_{l}")
        sv['x2'] = cur
        cur, sv['hB'], sv['aB'], sv['bB'], sv['yB'] = ffn_fwd(cur, gvec(l, 4), gvec(l, 5), wg_all, wu_all, wd_all, l, 1,
                                                              name=f"ffn_fwd_{l}b")
        saved.append(sv)

    loss_blk, dcur = loss_head(cur, loss_target[0], name="loss_head")
    loss = lax.psum(loss_blk[0, 0], ("x", "y", "c"))

    dng = [[None] * 6 for _ in range(depth)]
    dwg = [[None, None] for _ in range(depth)]
    dwu = [[None, None] for _ in range(depth)]
    dwd = [[None, None] for _ in range(depth)]
    grads = {}
    for l in reversed(range(depth)):
        sv = saved[l]
        dcur, da, db, dy, dng[l][4], dng[l][5] = ffn_bwd_dx(dcur, sv['x2'], sv['yB'], sv['aB'], sv['bB'], gvec(l, 4),
                                                            gvec(l, 5), wg_all, wu_all, wd_all, l, 1, name=f"ffn_bwd_{l}b")
        dwg[l][1], dwu[l][1], dwd[l][1] = ffn_bwd_dw(sv['hB'], dy, sv['aB'], sv['bB'], da, db, name=f"ffn_dw_{l}b")
        if l % 2 == 0:
            dm, do, dz, grads['dn_norm_g'], dng[l][3] = dn_out_bwd(dcur, sv['m'], gvec(l, 3), sv['o'], sv['proj'],
                                                                  dn_norm_g, dn_wout, name=f"dn_out_bwd_{l}")
            grads['dn_w_out'] = tn_mm(sv['og'], dm, name=f"dn_wout_dw_{l}").reshape(N_DEV, W // N_DEV, D)
            dqkv, dgB, dbB = dn_chunk_bwd(sv['qkv'], sv['g'], sv['beta'], sv['states'], do, name=f"dn_chunk_bwd_{l}")
            dpba, dal, ddt = dn_gates_bwd(sv['pba'], alog_b, dtb_b, dbB, dgB, name=f"dn_gates_bwd_{l}")
            grads['dn_a_log'] = dal[:, 0, 0].reshape(1, H)
            grads['dn_dt_bias'] = ddt[:, 0, 0].reshape(1, H)
            dproj, grads['dn_conv_w'] = dn_prep_bwd(sv['proj'], conv_full, dqkv, dz, name=f"dn_prep_bwd_{l}")
            dw_main = tn_mm(sv['hM'], dproj, name=f"dn_win_dw_{l}")
            dw_ba = tn_mm(sv['hM'], dpba, name=f"dn_wba_dw_{l}", tn=HEAD)
            dw_in = jnp.concatenate([dw_main, dw_ba[:, :2 * H]], axis=1)
            grads['dn_w_in'] = jnp.moveaxis(dw_in.reshape(D, N_DEV, c8), 1, 0)
            dcur, dng[l][2] = mm_bwd_dx(dcur, sv['x1'], gvec(l, 2), dproj, dn_wmain, dpba, dn_wba, name=f"dn_in_bwd_{l}")
        else:
            dm, dpre, grads['sg_b_in'], grads['sg_ln_g'], grads['sg_ln_b'], grads['sg_w_s'], dbs, dng[l][3] = sg_mid_bwd(
                dcur, sv['m'], gvec(l, 3), sv['pre'], bin_full, lng_full, lnb_full, sg_w_s[0], bsT, sg_wout,
                name=f"sg_mid_bwd_{l}")
            grads['sg_b_s'] = dbs[:, :G].T
            grads['sg_w_out'] = tn_mm(sv['gated'], dm, name=f"sg_wout_dw_{l}").reshape(N_DEV, E // N_DEV, D)
            grads['sg_w_in'] = tn_mm(sv['hM'], dpre, name=f"sg_win_dw_{l}", tn=2 * E // N_DEV, slot_major=True)
            dcur, dng[l][2] = mm_bwd_dx(dcur, sv['x1'], gvec(l, 2), dpre, sg_win, None, None, name=f"sg_in_bwd_{l}")
        dcur, da, db, dy, dng[l][0], dng[l][1] = ffn_bwd_dx(dcur, sv['x0'], sv['yA'], sv['aA'], sv['bA'], gvec(l, 0),
                                                            gvec(l, 1), wg_all, wu_all, wd_all, l, 0, name=f"ffn_bwd_{l}a")
        dwg[l][0], dwu[l][0], dwd[l][0] = ffn_bwd_dw(sv['hA'], dy, sv['aA'], sv['bA'], da, db, name=f"ffn_dw_{l}a")
    grad_x = dcur[None]

    stack4 = lambda t: jnp.stack([jnp.stack(r, axis=1) for r in t], axis=1)
    big = [stack4(dwg), stack4(dwu), stack4(dwd), grads['dn_w_in'], grads['dn_w_out'], grads['sg_w_in'],
           grads['sg_w_out']]
    big_names = ['ffn_w_gate', 'ffn_w_up', 'ffn_w_down', 'dn_w_in', 'dn_w_out', 'sg_w_in', 'sg_w_out']
    slots = exchange_slots(big, name="exchange_grads")

    dng_full = jnp.stack([jnp.concatenate(r, axis=0) for r in dng], axis=0)
    small_names = ['norm_g', 'dn_conv_w', 'sg_b_in', 'sg_ln_g', 'sg_ln_b', 'sg_w_s', 'sg_b_s', 'dn_a_log',
                   'dn_dt_bias', 'dn_norm_g']
    small_parts = [dng_full, grads['dn_conv_w'], grads['sg_b_in'], grads['sg_ln_g'], grads['sg_ln_b'],
                   grads['sg_w_s'], grads['sg_b_s'], grads['dn_a_log'], grads['dn_dt_bias'], grads['dn_norm_g']]
    small_pack, offs = _pack_rows(small_parts)
    (small_slots,) = all_gather_multi([small_pack], name="gather_small_grads")
    small_sum = sum_slots(small_slots, name="sum_small_grads")

    def small_grad(i):
        r0, n = offs[i]
        p = small_parts[i]
        return small_sum[r0:r0 + n].reshape(-1)[:p.size].reshape(p.shape)

    def my_shard(full, axis, like):
        n = full.shape[axis] // N_DEV
        return lax.dynamic_slice_in_dim(full, me * n, n, axis).reshape(like.shape)

    g_small = {
        'norm_g': my_shard(small_grad(0), 2, norm_g),
        'dn_conv_w': my_shard(small_grad(1), 1, dn_conv_w),
        'sg_b_in': my_shard(small_grad(2), 1, sg_b_in),
        'sg_ln_g': my_shard(small_grad(3), 1, sg_ln_g),
        'sg_ln_b': my_shard(small_grad(4), 1, sg_ln_b),
        'sg_w_s': small_grad(5).reshape(sg_w_s.shape),
        'sg_b_s': small_grad(6).reshape(sg_b_s.shape),
        'dn_a_log': small_grad(7).reshape(dn_a_log.shape),
        'dn_dt_bias': small_grad(8).reshape(dn_dt_bias.shape),
        'dn_norm_g': small_grad(9).reshape(dn_norm_g.shape),
    }

    out_g, out_d, out_m, out_v = {}, {}, {}, {}
    for nm, r in zip(big_names, slots):
        w = weights[nm]
        cols = w.shape[-1]
        rows = w.size // cols
        tr = {'ffn_w_gate': 512, 'ffn_w_up': 512, 'ffn_w_down': F8, 'dn_w_in': 256, 'sg_w_in': 256}.get(nm, rows)
        g, d, m2, v2 = adam_slots(w.reshape(rows, cols), r.reshape(N_DEV, rows, cols), mom_m[nm].reshape(rows, cols),
                                  mom_v[nm].reshape(rows, cols), name=f"adam_{nm}", tr=tr)
        out_g[nm], out_d[nm], out_m[nm], out_v[nm] = (t.reshape(w.shape) for t in (g, d, m2, v2))
    for nm in small_names:
        w = weights[nm]
        cols = w.shape[-1]
        rows = w.size // cols
        two = lambda t: t.reshape(rows, cols)
        d, m2, v2 = adam_small(two(w), two(g_small[nm]), two(mom_m[nm]), two(mom_v[nm]), name=f"adam_{nm}")
        out_g[nm] = g_small[nm]
        out_d[nm], out_m[nm], out_v[nm] = (t.reshape(w.shape) for t in (d, m2, v2))

    return (loss, grad_x, *[out_g[n] for n in order], *[out_d[n] for n in order], *[out_m[n] for n in order],
            *[out_v[n] for n in order])
```

```python
import functools
import math

import jax
import jax.numpy as jnp
from jax import lax
from jax.experimental import pallas as pl
from jax.experimental.pallas import tpu as pltpu

f32 = jnp.float32
MXU_DTYPE = jnp.bfloat16
N_DEV = 8
RMS_EPS = 1e-6
LN_EPS = 1e-5
L2_EPS = 1e-6
HEAD = 128
DN_CHUNK = 64
SG_CHUNK = 128
SG_GROUPS = 8
CONV_K = 4
ADAM_LR, ADAM_B1, ADAM_B2, ADAM_EPS, ADAM_WD, ADAM_STEP = 0.001, 0.9, 0.999, 1e-08, 0.01, 10
VMEM_LIMIT = 56 * 1024 * 1024
FFN_ROWS_FWD, FFN_ROWS_BWD, FFN_ROWS_DW = 512, 512, 1024
FFN_SLABS = 4
SDS = jax.ShapeDtypeStruct
HIGHEST = lax.Precision.HIGHEST
MESH = pl.DeviceIdType.MESH


def _params(n_grid):
    return pltpu.CompilerParams(dimension_semantics=("arbitrary",) * n_grid, vmem_limit_bytes=VMEM_LIMIT)


def _row_tile(s, want):
    t = min(s, want)
    assert s % t == 0, (s, t)
    return t


def _rms(x, g):
    return x * lax.rsqrt(jnp.mean(x * x, axis=-1, keepdims=True) + RMS_EPS) * g


def _rms_bwd(x, g, dy):
    _, vjp = jax.vjp(_rms, x, g)
    return vjp(dy)


def _silu(a):
    return a * jax.nn.sigmoid(a)


def _gelu(x):
    return 0.5 * x * (1.0 + lax.erf(x * 0.7071067811865476))


def _mm(a, b):
    return lax.dot_general(a, b, (((1,), (0,)), ((), ())), preferred_element_type=f32)


def _mm_nt(a, b):
    return lax.dot_general(a, b, (((1,), (1,)), ((), ())), preferred_element_type=f32)


def _mm_tn(a, b):
    return lax.dot_general(a, b, (((0,), (0,)), ((), ())), preferred_element_type=f32)


def _c(x):
    return x.astype(MXU_DTYPE)


def _split(a):
    hi = a.astype(MXU_DTYPE)
    lo = (a - hi.astype(f32)).astype(MXU_DTYPE)
    return hi, lo


def _dot3(a, b, dims):
    ah, al = _split(a)
    bh, bl = _split(b)
    d = lambda p, q: lax.dot_general(p, q, (dims, ((), ())), preferred_element_type=f32)
    return d(ah, bh) + (d(ah, bl) + d(al, bh))


NN, NT, TN = ((1,), (0,)), ((1,), (1,)), ((0,), (0,))


def _slot(px, py, pc):
    return 4 * px + 2 * py + pc


def all_gather_multi(arrs, name):
    n = len(arrs)

    def body(*refs):
        ins, outs = refs[:n], refs[n:2 * n]
        send_sems, recv_sems, local_sems = refs[2 * n:]
        x, y, c = lax.axis_index("x"), lax.axis_index("y"), lax.axis_index("c")
        me, sibling = (x, y, c), (x, y, 1 - c)
        chips = [(1 - x, y), (x, 1 - y), (1 - x, 1 - y)]

        def copy(a, k, block, to, src=None):
            dst = outs[a].at[_slot(*block)]
            return pltpu.make_async_remote_copy(
                src_ref=dst if src is None else src, dst_ref=dst, send_sem=send_sems.at[a, k],
                recv_sem=recv_sems.at[a, k], device_id=to, device_id_type=MESH)

        started = []
        for a in range(n):
            mine = pltpu.make_async_copy(ins[a], outs[a].at[_slot(*me)], local_sems.at[a])
            mine.start()
            started.append(mine)
        sends = []
        for a in range(n):
            first = [copy(a, 0, me, sibling, src=ins[a])]
            first += [copy(a, 1 + j, me, (*chip, c), src=ins[a]) for j, chip in enumerate(chips)]
            for cp in first:
                cp.start()
            sends += first
        for a in range(n):
            for j, chip in enumerate(chips):
                copy(a, 1 + j, (*chip, c), me).wait_recv()
                fwd = copy(a, 4 + j, (*chip, c), sibling)
                fwd.start()
                sends.append(fwd)
        for a in range(n):
            copy(a, 0, sibling, me).wait_recv()
            for j, chip in enumerate(chips):
                copy(a, 4 + j, (*chip, 1 - c), me).wait_recv()
        for cp in sends:
            cp.wait_send()
        for mine in started:
            mine.wait()

    hbm = pl.BlockSpec(memory_space=pltpu.HBM)
    return pl.pallas_call(
        body, name=name,
        out_shape=tuple(SDS((N_DEV,) + a.shape, a.dtype) for a in arrs),
        in_specs=[hbm] * n, out_specs=tuple([hbm] * n),
        scratch_shapes=[pltpu.SemaphoreType.DMA((n, 7)), pltpu.SemaphoreType.DMA((n, 7)),
                        pltpu.SemaphoreType.DMA((n,))],
    )(*arrs)


def exchange_slots(arrs, name):
    n = len(arrs)

    def body(*refs):
        ins, outs = refs[:n], refs[n:2 * n]
        send_sems, recv_sems, local_sems = refs[2 * n:]
        x, y, c = lax.axis_index("x"), lax.axis_index("y"), lax.axis_index("c")
        me = _slot(x, y, c)
        peers = [(x ^ (k >> 2), y ^ ((k >> 1) & 1), c ^ (k & 1)) for k in range(1, N_DEV)]

        def copy(a, k):
            peer = peers[k - 1]
            return pltpu.make_async_remote_copy(
                src_ref=ins[a].at[_slot(*peer)], dst_ref=outs[a].at[me], send_sem=send_sems.at[a, k - 1],
                recv_sem=recv_sems.at[a, k - 1], device_id=peer, device_id_type=MESH)

        def landed(a, k):
            peer = peers[k - 1]
            return pltpu.make_async_remote_copy(
                src_ref=ins[a].at[me], dst_ref=outs[a].at[_slot(*peer)], send_sem=send_sems.at[a, k - 1],
                recv_sem=recv_sems.at[a, k - 1], device_id=peer, device_id_type=MESH)

        local = []
        for a in range(n):
            cp = pltpu.make_async_copy(ins[a].at[me], outs[a].at[me], local_sems.at[a])
            cp.start()
            local.append(cp)
        order = [6, 7, 2, 3, 4, 5, 1]
        for a in range(n):
            for k in order:
                copy(a, k).start()
        for a in range(n):
            for k in order:
                copy(a, k).wait_send()
                landed(a, k).wait_recv()
        for cp in local:
            cp.wait()

    hbm = pl.BlockSpec(memory_space=pltpu.HBM)
    return pl.pallas_call(
        body, name=name,
        out_shape=tuple(SDS(a.shape, a.dtype) for a in arrs),
        in_specs=[hbm] * n, out_specs=tuple([hbm] * n),
        scratch_shapes=[pltpu.SemaphoreType.DMA((n, 7)), pltpu.SemaphoreType.DMA((n, 7)),
                        pltpu.SemaphoreType.DMA((n,))],
    )(*arrs)


def ffn_fwd(x, gpre, gpost, wg, wu, wd, li, ab, name):
    S, D = x.shape
    nj, F8 = wg.shape[0], wg.shape[-1]
    tm = _row_tile(S, FFN_ROWS_FWD)

    def body(x_ref, gpre_ref, gpost_ref, wg_ref, wu_ref, wd_ref, xo_ref, h_ref, a_ref, b_ref, y_ref):
        j = pl.program_id(1)

        @pl.when(j == 0)
        def _():
            h_ref[...] = _rms(x_ref[...], gpre_ref[...]).astype(h_ref.dtype)
            y_ref[...] = jnp.zeros_like(y_ref)

        h = h_ref[...]
        a = _mm(h, wg_ref[...]).astype(a_ref.dtype)
        b = _mm(h, wu_ref[...]).astype(b_ref.dtype)
        a_ref[...] = a
        b_ref[...] = b
        t = _silu(a.astype(f32)) * b.astype(f32)
        y_ref[...] += _mm(_c(t), wd_ref[...])

        @pl.when(j == nj - 1)
        def _():
            xo_ref[...] = x_ref[...] + 0.5 * _rms(y_ref[...], gpost_ref[...])

    row = pl.BlockSpec((tm, D), lambda i, j: (i, 0))
    vec = pl.BlockSpec((1, D), lambda i, j: (0, 0))
    wcol = pl.BlockSpec((None, None, None, D, F8), lambda i, j: (j, li, ab, 0, 0))
    wrow = pl.BlockSpec((None, None, None, F8, D), lambda i, j: (j, li, ab, 0, 0))
    hid = pl.BlockSpec((None, tm, F8), lambda i, j: (j, i, 0))
    return pl.pallas_call(
        body, name=name, grid=(S // tm, nj),
        in_specs=[row, vec, vec, wcol, wcol, wrow],
        out_specs=(row, row, hid, hid, row),
        out_shape=(SDS((S, D), f32), SDS((S, D), MXU_DTYPE), SDS((nj, S, F8), MXU_DTYPE),
                   SDS((nj, S, F8), MXU_DTYPE), SDS((S, D), f32)),
        compiler_params=_params(2),
    )(x, gpre, gpost, wg, wu, wd)


def ffn_bwd_dx(dxo, x, y, a, b, gpre, gpost, wg, wu, wd, li, ab, name):
    S, D = x.shape
    nj, F8 = wg.shape[0], wg.shape[-1]
    tm = _row_tile(S, FFN_ROWS_BWD)

    def body(dxo_ref, x_ref, y_ref, a_ref, b_ref, gpre_ref, gpost_ref, wg_ref, wu_ref, wd_ref,
             dx_ref, da_ref, db_ref, dy_ref, dgpre_ref, dgpost_ref, dh_ref):
        i, j = pl.program_id(0), pl.program_id(1)

        @pl.when(j == 0)
        def _():
            @pl.when(i == 0)
            def _():
                dgpre_ref[...] = jnp.zeros_like(dgpre_ref)
                dgpost_ref[...] = jnp.zeros_like(dgpost_ref)

            dy, dg = _rms_bwd(y_ref[...], gpost_ref[...], 0.5 * dxo_ref[...])
            dy_ref[...] = dy.astype(dy_ref.dtype)
            dgpost_ref[...] += dg
            dh_ref[...] = jnp.zeros_like(dh_ref)

        dt = _mm_nt(dy_ref[...], wd_ref[...])
        af, bf = a_ref[...].astype(f32), b_ref[...].astype(f32)
        s = jax.nn.sigmoid(af)
        da = (dt * bf * (s * (1.0 + af * (1.0 - s)))).astype(da_ref.dtype)
        db = (dt * (af * s)).astype(db_ref.dtype)
        da_ref[...] = da
        db_ref[...] = db
        dh_ref[...] += _mm_nt(da, wg_ref[...]) + _mm_nt(db, wu_ref[...])

        @pl.when(j == nj - 1)
        def _():
            dxx, dg = _rms_bwd(x_ref[...], gpre_ref[...], dh_ref[...])
            dx_ref[...] = dxo_ref[...] + dxx
            dgpre_ref[...] += dg

    row = pl.BlockSpec((tm, D), lambda i, j: (i, 0))
    vec = pl.BlockSpec((1, D), lambda i, j: (0, 0))
    wcol = pl.BlockSpec((None, None, None, D, F8), lambda i, j: (j, li, ab, 0, 0))
    wrow = pl.BlockSpec((None, None, None, F8, D), lambda i, j: (j, li, ab, 0, 0))
    hid = pl.BlockSpec((None, tm, F8), lambda i, j: (j, i, 0))
    return pl.pallas_call(
        body, name=name, grid=(S // tm, nj),
        in_specs=[row, row, row, hid, hid, vec, vec, wcol, wcol, wrow],
        out_specs=(row, hid, hid, row, vec, vec),
        out_shape=(SDS((S, D), f32), SDS((nj, S, F8), MXU_DTYPE), SDS((nj, S, F8), MXU_DTYPE),
                   SDS((S, D), MXU_DTYPE), SDS((1, D), f32), SDS((1, D), f32)),
        scratch_shapes=[pltpu.VMEM((tm, D), f32)],
        compiler_params=_params(2),
    )(dxo, x, y, a, b, gpre, gpost, wg, wu, wd)


def ffn_bwd_dw(h, dy, a, b, da, db, name):
    S, D = h.shape
    F8 = a.shape[-1]
    tm = _row_tile(S, FFN_ROWS_DW)
    ni = S // tm

    def body(h_ref, dy_ref, a_ref, b_ref, da_ref, db_ref, dwg_ref, dwu_ref, dwd_ref, accg, accu, accd):
        i = pl.program_id(1)

        @pl.when(i == 0)
        def _():
            accg[...] = jnp.zeros_like(accg)
            accu[...] = jnp.zeros_like(accu)
            accd[...] = jnp.zeros_like(accd)

        t = _c(_silu(a_ref[...].astype(f32)) * b_ref[...].astype(f32))
        hh = h_ref[...]
        accg[...] += _mm_tn(hh, da_ref[...])
        accu[...] += _mm_tn(hh, db_ref[...])
        accd[...] += _mm_tn(t, dy_ref[...])

        @pl.when(i == ni - 1)
        def _():
            dwg_ref[...] = accg[...].astype(dwg_ref.dtype)
            dwu_ref[...] = accu[...].astype(dwu_ref.dtype)
            dwd_ref[...] = accd[...].astype(dwd_ref.dtype)

    row = pl.BlockSpec((tm, D), lambda j, i: (i, 0))
    hid = pl.BlockSpec((None, tm, F8), lambda j, i: (j, i, 0))
    wcol = pl.BlockSpec((None, D, F8), lambda j, i: (j, 0, 0))
    wrow = pl.BlockSpec((None, F8, D), lambda j, i: (j, 0, 0))
    return pl.pallas_call(
        body, name=name, grid=(a.shape[0], ni),
        in_specs=[row, row, hid, hid, hid, hid],
        out_specs=(wcol, wcol, wrow),
        out_shape=(SDS((a.shape[0], D, F8), MXU_DTYPE), SDS((a.shape[0], D, F8), MXU_DTYPE),
                   SDS((a.shape[0], F8, D), MXU_DTYPE)),
        scratch_shapes=[pltpu.VMEM((D, F8), f32), pltpu.VMEM((D, F8), f32), pltpu.VMEM((F8, D), f32)],
        compiler_params=_params(2),
    )(h, dy, a, b, da, db)


def rms_mm(x, g, w, w2, name, tn=1024):
    S, D = x.shape
    N = w.shape[1]
    tm = _row_tile(S, 512)
    tn = _row_tile(N, tn)
    has2 = w2 is not None

    def body(*refs):
        if has2:
            x_ref, g_ref, w_ref, w2_ref, h_ref, o_ref, o2_ref = refs
        else:
            x_ref, g_ref, w_ref, h_ref, o_ref = refs
        j = pl.program_id(1)

        @pl.when(j == 0)
        def _():
            h = _rms(x_ref[...], g_ref[...]).astype(h_ref.dtype)
            h_ref[...] = h
            if has2:
                o2_ref[...] = _mm(h, w2_ref[...])

        o_ref[...] = _mm(h_ref[...], w_ref[...])

    row = pl.BlockSpec((tm, D), lambda i, j: (i, 0))
    in_specs = [row, pl.BlockSpec((1, D), lambda i, j: (0, 0)), pl.BlockSpec((D, tn), lambda i, j: (0, j))]
    out_specs = [row, pl.BlockSpec((tm, tn), lambda i, j: (i, j))]
    out_shape = [SDS((S, D), MXU_DTYPE), SDS((S, N), f32)]
    args = [x, g, w]
    if has2:
        in_specs.append(pl.BlockSpec((D, w2.shape[1]), lambda i, j: (0, 0)))
        out_specs.append(pl.BlockSpec((tm, w2.shape[1]), lambda i, j: (i, 0)))
        out_shape.append(SDS((S, w2.shape[1]), f32))
        args.append(w2)
    return pl.pallas_call(
        body, name=name, grid=(S // tm, N // tn), in_specs=in_specs, out_specs=tuple(out_specs),
        out_shape=tuple(out_shape), compiler_params=_params(2),
    )(*args)


def mm_bwd_dx(dres, x, g, dy, w, dy2, w2, name, tk=1024):
    S, D = x.shape
    K = dy.shape[1]
    tm = _row_tile(S, 512)
    tk = _row_tile(K, tk)
    nk = K // tk
    has2 = dy2 is not None

    def body(*refs):
        if has2:
            dres_ref, x_ref, g_ref, dy_ref, w_ref, dy2_ref, w2_ref, dx_ref, dg_ref, dh_ref = refs
        else:
            dres_ref, x_ref, g_ref, dy_ref, w_ref, dx_ref, dg_ref, dh_ref = refs
        i, k = pl.program_id(0), pl.program_id(1)

        @pl.when(k == 0)
        def _():
            @pl.when(i == 0)
            def _():
                dg_ref[...] = jnp.zeros_like(dg_ref)

            if has2:
                dh_ref[...] = _mm_nt(dy2_ref[...], w2_ref[...])
            else:
                dh_ref[...] = jnp.zeros_like(dh_ref)

        dh_ref[...] += _mm_nt(dy_ref[...], w_ref[...])

        @pl.when(k == nk - 1)
        def _():
            dxx, dg = _rms_bwd(x_ref[...], g_ref[...], dh_ref[...])
            dx_ref[...] = dres_ref[...] + dxx
            dg_ref[...] += dg

    row = pl.BlockSpec((tm, D), lambda i, k: (i, 0))
    vec = pl.BlockSpec((1, D), lambda i, k: (0, 0))
    in_specs = [row, row, vec, pl.BlockSpec((tm, tk), lambda i, k: (i, k)), pl.BlockSpec((D, tk), lambda i, k: (0, k))]
    args = [dres, x, g, dy, w]
    if has2:
        in_specs += [pl.BlockSpec((tm, dy2.shape[1]), lambda i, k: (i, 0)),
                     pl.BlockSpec((D, w2.shape[1]), lambda i, k: (0, 0))]
        args += [dy2, w2]
    return pl.pallas_call(
        body, name=name, grid=(S // tm, nk), in_specs=in_specs, out_specs=(row, vec),
        out_shape=(SDS((S, D), f32), SDS((1, D), f32)),
        scratch_shapes=[pltpu.VMEM((tm, D), f32)], compiler_params=_params(2),
    )(*args)


def tn_mm(a, b, name, tn=512, slot_major=False):
    S, K1 = a.shape
    N = b.shape[1]
    tm = _row_tile(S, 512)
    tn = _row_tile(N, tn)
    ni = S // tm

    def body(a_ref, b_ref, o_ref, acc):
        i = pl.program_id(1)

        @pl.when(i == 0)
        def _():
            acc[...] = jnp.zeros_like(acc)

        acc[...] += _mm_tn(a_ref[...], b_ref[...])

        @pl.when(i == ni - 1)
        def _():
            o_ref[...] = acc[...].astype(o_ref.dtype)

    if slot_major:
        out_spec, out_shape = pl.BlockSpec((None, K1, tn), lambda j, i: (j, 0, 0)), SDS((N // tn, K1, tn), MXU_DTYPE)
    else:
        out_spec, out_shape = pl.BlockSpec((K1, tn), lambda j, i: (0, j)), SDS((K1, N), MXU_DTYPE)
    return pl.pallas_call(
        body, name=name, grid=(N // tn, ni),
        in_specs=[pl.BlockSpec((tm, K1), lambda j, i: (i, 0)), pl.BlockSpec((tm, tn), lambda j, i: (i, j))],
        out_specs=out_spec, out_shape=out_shape,
        scratch_shapes=[pltpu.VMEM((K1, tn), f32)], compiler_params=_params(2),
    )(a, b)


CONV_ROWS = 512


def _shift_down(cur, prev8, s):
    r = pltpu.roll(cur, s, 0)
    row = lax.broadcasted_iota(jnp.int32, (8, cur.shape[1]), 0)
    top = jnp.where(row < s, pltpu.roll(prev8, s, 0), r[0:8])
    return jnp.concatenate([top, r[8:]], axis=0)


def _shift_up(cur, next8, s):
    n = cur.shape[0]
    r = pltpu.roll(cur, n - s, 0)
    row = lax.broadcasted_iota(jnp.int32, (8, cur.shape[1]), 0)
    bot = jnp.where(row >= 8 - s, pltpu.roll(next8, 8 - s, 0), r[n - 8:])
    return jnp.concatenate([r[:n - 8], bot], axis=0)


def _conv_taps(cur, prev8):
    return [_shift_down(cur, prev8, 3), _shift_down(cur, prev8, 2), _shift_down(cur, prev8, 1), cur]


def _act_qk(c):
    a = _silu(c)
    return a * lax.rsqrt(jnp.sum(a * a, axis=-1, keepdims=True) + L2_EPS)


def dn_prep(proj, conv_w, name):
    S = proj.shape[0]
    W = conv_w.shape[1] // 3
    nh = W // HEAD
    R = _row_tile(S, CONV_ROWS)

    def body(p_ref, w_ref, o_ref):
        j = pl.program_id(0)
        w = w_ref[...]

        def rows(r, prev8):
            cur = p_ref[pl.ds(r, R), :]
            taps = _conv_taps(cur, prev8)
            cv = taps[0] * w[0:1] + taps[1] * w[1:2] + taps[2] * w[2:3] + taps[3] * w[3:4]

            @pl.when(j < 2 * nh)
            def _():
                o_ref[pl.ds(r, R), :] = _act_qk(cv)

            @pl.when(j >= 2 * nh)
            def _():
                o_ref[pl.ds(r, R), :] = _silu(cv)

        rows(0, jnp.zeros((8, HEAD), f32))

        @pl.loop(1, S // R)
        def _(t):
            r = pl.multiple_of(t * R, R)
            rows(r, p_ref[pl.ds(r - 8, 8), :])

    return pl.pallas_call(
        body, name=name, grid=(3 * nh,),
        in_specs=[pl.BlockSpec((S, HEAD), lambda j: (0, j)), pl.BlockSpec((CONV_K, HEAD), lambda j: (0, j))],
        out_specs=pl.BlockSpec((None, S, HEAD), lambda j: (j // nh, 0, j % nh)),
        out_shape=SDS((3, S, W), f32), compiler_params=_params(1),
    )(proj, conv_w)


def dn_prep_bwd(proj, conv_w, dqkv, dz, name):
    S = proj.shape[0]
    W = conv_w.shape[1] // 3
    nh = W // HEAD
    nq = 3 * nh
    R = _row_tile(S, CONV_ROWS)
    nr = S // R

    def body(p_ref, w_ref, dq_ref, dz_ref, dp_ref, dw_ref, dc_ref):
        j = pl.program_id(0)

        @pl.when(j >= nq)
        def _():
            dp_ref[...] = dz_ref[...].astype(dp_ref.dtype)

        @pl.when(j < nq)
        def _():
            w = w_ref[...]
            dw_ref[...] = jnp.zeros_like(dw_ref)

            def rows(r, prev8):
                cur = p_ref[pl.ds(r, R), :]
                taps = _conv_taps(cur, prev8)
                cv = taps[0] * w[0:1] + taps[1] * w[1:2] + taps[2] * w[2:3] + taps[3] * w[3:4]
                dn = dq_ref[pl.ds(r, R), :]

                @pl.when(j < 2 * nh)
                def _():
                    dc_ref[pl.ds(r, R), :] = jax.vjp(_act_qk, cv)[1](dn)[0]

                @pl.when(j >= 2 * nh)
                def _():
                    dc_ref[pl.ds(r, R), :] = jax.vjp(_silu, cv)[1](dn)[0]

                dc = dc_ref[pl.ds(r, R), :]
                dw_ref[...] += jnp.concatenate(
                    [jnp.sum(dc * taps[q], axis=0, keepdims=True) for q in range(CONV_K)], axis=0)

            rows(0, jnp.zeros((8, HEAD), f32))

            @pl.loop(1, nr)
            def _(t):
                r = pl.multiple_of(t * R, R)
                rows(r, p_ref[pl.ds(r - 8, 8), :])

            def back(r, next8):
                dc = dc_ref[pl.ds(r, R), :]
                dx = dc * w[3:4]
                for s in (1, 2, 3):
                    dx = dx + _shift_up(dc, next8, s) * w[3 - s:4 - s]
                dp_ref[pl.ds(r, R), :] = dx.astype(dp_ref.dtype)

            @pl.loop(0, nr - 1)
            def _(t):
                r = pl.multiple_of(t * R, R)
                back(r, dc_ref[pl.ds(r + R, 8), :])

            back((nr - 1) * R, jnp.zeros((8, HEAD), f32))

    clamp = lambda j: jnp.minimum(j, nq - 1)
    return pl.pallas_call(
        body, name=name, grid=(4 * nh,),
        in_specs=[pl.BlockSpec((S, HEAD), lambda j: (0, clamp(j))),
                  pl.BlockSpec((CONV_K, HEAD), lambda j: (0, clamp(j))),
                  pl.BlockSpec((None, S, HEAD), lambda j: (clamp(j) // nh, 0, clamp(j) % nh)),
                  pl.BlockSpec((S, HEAD), lambda j: (0, jnp.maximum(j - nq, 0)))],
        out_specs=(pl.BlockSpec((S, HEAD), lambda j: (0, j)), pl.BlockSpec((CONV_K, HEAD), lambda j: (0, clamp(j)))),
        out_shape=(SDS((S, 4 * W), MXU_DTYPE), SDS((CONV_K, 3 * W), f32)),
        scratch_shapes=[pltpu.VMEM((S, HEAD), f32)], compiler_params=_params(1),
    )(proj, conv_w, dqkv, dz)


def _gate_fns(braw, araw, alog, dtb):
    beta = jax.nn.sigmoid(braw)
    g = -jnp.exp(alog) * jax.nn.softplus(araw + dtb)
    return beta, g


def _lane_pick(x, lane):
    sel = lax.broadcasted_iota(jnp.int32, x.shape, 1) == lane
    return jnp.broadcast_to(jnp.sum(jnp.where(sel, x, 0.0), axis=1, keepdims=True), x.shape)


CUM_ROWS = 256


def _sel_mm(m01, x):
    m = _c(m01)
    d = lambda p: lax.dot_general(m, p, (NN, ((), ())), preferred_element_type=f32)
    h1, h2, h3 = _pieces3(x)
    return (d(h1) + d(h2)) + d(h3)


def _chunk_cumsum_matrix(n, transpose):
    r, c = lax.broadcasted_iota(jnp.int32, (n, n), 0), lax.broadcasted_iota(jnp.int32, (n, n), 1)
    sh = int(math.log2(DN_CHUNK))
    same = (r >> sh) == (c >> sh)
    return jnp.where(same & ((r <= c) if transpose else (r >= c)), 1.0, 0.0).astype(f32)


def dn_gates(pba, alog_b, dtb_b, name):
    S = pba.shape[0]
    H = alog_b.shape[0]
    R = _row_tile(S, CUM_ROWS)

    def body(p_ref, al_ref, dt_ref, beta_ref, g_ref):
        h = pl.program_id(0)
        p = p_ref[...]
        beta, g = _gate_fns(_lane_pick(p, h), _lane_pick(p, H + h), al_ref[...], dt_ref[...])
        beta_ref[...] = beta
        g_ref[...] = g
        cum = _chunk_cumsum_matrix(R, False)

        @pl.loop(0, S // R)
        def _(t):
            r = pl.multiple_of(t * R, R)
            g_ref[pl.ds(r, R), :] = _sel_mm(cum, g_ref[pl.ds(r, R), :])

    par = pl.BlockSpec((None, 1, HEAD), lambda h: (h, 0, 0))
    out = pl.BlockSpec((None, S, HEAD), lambda h: (h, 0, 0))
    return pl.pallas_call(
        body, name=name, grid=(H,), in_specs=[pl.BlockSpec((S, HEAD), lambda h: (0, 0)), par, par],
        out_specs=(out, out), out_shape=(SDS((H, S, HEAD), f32), SDS((H, S, HEAD), f32)), compiler_params=_params(1),
    )(pba, alog_b, dtb_b)


def dn_gates_bwd(pba, alog_b, dtb_b, dbeta, dg, name):
    S = pba.shape[0]
    H = alog_b.shape[0]
    R = _row_tile(S, CUM_ROWS)

    def body(p_ref, al_ref, dt_ref, dbeta_ref, dg_ref, dp_ref, dal_ref, ddt_ref, acc, dgs):
        h = pl.program_id(0)

        @pl.when(h == 0)
        def _():
            acc[...] = jnp.zeros_like(acc)

        cum_t = _chunk_cumsum_matrix(R, True)

        @pl.loop(0, S // R)
        def _(t):
            r = pl.multiple_of(t * R, R)
            dgs[pl.ds(r, R), :] = _sel_mm(cum_t, dg_ref[pl.ds(r, R), :])

        p = p_ref[...]
        db = jnp.broadcast_to(jnp.sum(dbeta_ref[...], axis=1, keepdims=True), p.shape)
        dgg = jnp.broadcast_to(jnp.sum(dgs[...], axis=1, keepdims=True), p.shape)
        _, vjp = jax.vjp(_gate_fns, _lane_pick(p, h), _lane_pick(p, H + h), al_ref[...], dt_ref[...])
        dbraw, daraw, dal, ddt = vjp((db, dgg))
        lane = lax.broadcasted_iota(jnp.int32, p.shape, 1)
        acc[...] += jnp.where(lane == h, dbraw, 0.0) + jnp.where(lane == H + h, daraw, 0.0)
        dal_ref[...] = dal
        ddt_ref[...] = ddt

        @pl.when(h == H - 1)
        def _():
            dp_ref[...] = acc[...].astype(dp_ref.dtype)

    par = pl.BlockSpec((None, 1, HEAD), lambda h: (h, 0, 0))
    big = pl.BlockSpec((None, S, HEAD), lambda h: (h, 0, 0))
    full = pl.BlockSpec((S, HEAD), lambda h: (0, 0))
    return pl.pallas_call(
        body, name=name, grid=(H,), in_specs=[full, par, par, big, big],
        out_specs=(full, par, par),
        out_shape=(SDS((S, HEAD), MXU_DTYPE), SDS((H, 1, HEAD), f32), SDS((H, 1, HEAD), f32)),
        scratch_shapes=[pltpu.VMEM((S, HEAD), f32), pltpu.VMEM((S, HEAD), f32)], compiler_params=_params(1),
    )(pba, alog_b, dtb_b, dbeta, dg)


def _bdot(dims):
    back = {NN: ((NT, 'gb'), (TN, 'ag')), NT: ((NN, 'gb'), (TN, 'ga')), TN: ((NT, 'bg'), (NN, 'ag'))}[dims]
    d = lambda p, q, dm: lax.dot_general(_c(p), _c(q), (dm, ((), ())), preferred_element_type=f32)

    @jax.custom_vjp
    def f(a, b):
        return d(a, b, dims)

    def fwd(a, b):
        return d(a, b, dims), (a, b)

    def bwd(res, g):
        v = {'a': res[0], 'b': res[1], 'g': g}
        (da_dims, da_ops), (db_dims, db_ops) = back
        return d(v[da_ops[0]], v[da_ops[1]], da_dims), d(v[db_ops[0]], v[db_ops[1]], db_dims)

    f.defvjp(fwd, bwd)
    return f, lambda a, b: d(a, b, dims)


_BDOT = {dims: _bdot(dims) for dims in (NN, NT, TN)}


def _tri_inv_multi(Ls):
    n = Ls[0].shape[0]
    eye = jnp.where(lax.broadcasted_iota(jnp.int32, (n, n), 0) == lax.broadcasted_iota(jnp.int32, (n, n), 1), 1.0, 0.0)
    P = tuple(-L for L in Ls)
    T = tuple(eye + p for p in P)
    for _ in range(int(math.log2(n)) - 1):
        P = tuple(_dot3(p, p, NN) for p in P)
        T = tuple(t + _dot3(t, p, NN) for t, p in zip(T, P))
    return T


@jax.custom_vjp
def _tri_inv_multi_vjp(Ls):
    return _tri_inv_multi(Ls)


def _tri_inv_fwd(Ls):
    T = _tri_inv_multi(Ls)
    return T, T


def _tri_inv_bwd(T, dT):
    X = tuple(_dot3(d, t, NT) for d, t in zip(dT, T))
    return (tuple(-_dot3(t, x, TN) for t, x in zip(T, X)),)


_tri_inv_multi_vjp.defvjp(_tri_inv_fwd, _tri_inv_bwd)


def _pieces3(x):
    h1 = x.astype(MXU_DTYPE)
    r1 = x - h1.astype(f32)
    h2 = r1.astype(MXU_DTYPE)
    return h1, h2, (r1 - h2.astype(f32)).astype(MXU_DTYPE)


def _row_bcast_impl(sel_row, gc):
    s = _c(sel_row)
    d = lambda p: lax.dot_general(s, p, (NT, ((), ())), preferred_element_type=f32)
    h1, h2, h3 = _pieces3(gc)
    return (d(h1) + d(h2)) + d(h3)


def _row_bcast_bwd(sel_row, d):
    s = _c(sel_row)
    hi, lo = _split(d)
    t = lambda p: lax.dot_general(p, s, (TN, ((), ())), preferred_element_type=f32)
    return jnp.zeros_like(sel_row), t(hi) + t(lo)


_row_bcast = jax.custom_vjp(_row_bcast_impl)
_row_bcast.defvjp(lambda sel_row, gc: (_row_bcast_impl(sel_row, gc), sel_row), _row_bcast_bwd)


def _col_bcast_impl(gc):
    return gc[:, :DN_CHUNK]


def _col_bcast_bwd(_, d):
    return (jnp.broadcast_to(jnp.sum(d, axis=1, keepdims=True) * (1.0 / HEAD), (d.shape[0], HEAD)),)


_col_bcast = jax.custom_vjp(_col_bcast_impl)
_col_bcast.defvjp(lambda gc: (_col_bcast_impl(gc), None), _col_bcast_bwd)


def _last_row_bcast(n):
    def impl(gc):
        return jnp.broadcast_to(gc[DN_CHUNK - 1:DN_CHUNK, :], (n, HEAD))

    def bwd(_, d):
        row = lax.broadcasted_iota(jnp.int32, (DN_CHUNK, HEAD), 0)
        return (jnp.where(row == DN_CHUNK - 1, jnp.sum(d, axis=0, keepdims=True), 0.0),)

    f = jax.custom_vjp(impl)
    f.defvjp(lambda gc: (impl(gc), None), bwd)
    return impl, f


_LAST_C, _LAST_H = _last_row_bcast(DN_CHUNK), _last_row_bcast(HEAD)


def _chunk_consts():
    C = DN_CHUNK
    io = lambda shape, ax: lax.broadcasted_iota(jnp.int32, shape, ax)
    one = lambda m: jnp.where(m, 1.0, 0.0).astype(f32)
    r, c = io((C, C), 0), io((C, C), 1)
    return dict(causal=r >= c, strict=r > c, sel_row=one(io((C, HEAD), 1) == 0))


def _chunk_fn(kc, diff, q, k, v, gc, bB, S0):
    i = 0 if diff else 1
    mm, mm_nt, mm_tn = _BDOT[NN][i], _BDOT[NT][i], _BDOT[TN][i]
    tri = _tri_inv_multi_vjp if diff else _tri_inv_multi
    each = lambda f, *ls: tuple(f(*a) for a in zip(*ls))
    gcol = each(_col_bcast if diff else _col_bcast_impl, gc)
    grow = each(lambda g: (_row_bcast if diff else _row_bcast_impl)(kc['sel_row'], g), gc)
    glc = each(_LAST_C[i ^ 1], gc)
    glh = each(_LAST_H[i ^ 1], gc)
    decay = each(lambda a, b: jnp.where(kc['causal'], jnp.exp(jnp.where(kc['causal'], a - b, 0.0)), 0.0), gcol, grow)
    kb = each(lambda a, b: a * b, k, bB)
    vb = each(lambda a, b: a * b, v, bB)
    egc = each(jnp.exp, gc)
    kk = each(mm_nt, kb, k)
    T = tri(each(lambda a, d: jnp.where(kc['strict'], a * d, 0.0), kk, decay))
    u = each(mm, T, vb)
    w = each(mm, T, each(lambda a, b: a * b, kb, egc))
    qs = each(lambda a: a * (HEAD ** -0.5), q)
    qk = each(mm_nt, qs, k)
    attn = each(lambda a, d: jnp.where(kc['causal'], a * d, 0.0), qk, decay)
    wS = each(mm, w, S0)
    qS = each(mm, each(lambda a, b: a * b, qs, egc), S0)
    v_new = each(lambda a, b: a - b, u, wS)
    o = each(lambda a, b: a + b, qS, each(mm, attn, v_new))
    kdec = each(lambda a, gl, g: a * jnp.exp(gl - g), k, glc, gc)
    S1 = each(lambda s, gl, kv: s * jnp.exp(gl) + kv, S0, glh, each(mm_tn, kdec, v_new))
    return o, S1


def _heads_per_block(H):
    return 8 if H % 8 == 0 else (4 if H % 4 == 0 else 1)


def dn_chunk_fwd(qkv, gB, bB, name):
    _, S, W = qkv.shape
    H, C = W // HEAD, DN_CHUNK
    N, HB = S // C, _heads_per_block(H)

    def body(q_ref, k_ref, v_ref, g_ref, b_ref, o_ref, st_ref, s_scr):
        @pl.when(pl.program_id(1) == 0)
        def _():
            s_scr[...] = jnp.zeros_like(s_scr)

        kc = _chunk_consts()
        sls = [slice(hh * HEAD, (hh + 1) * HEAD) for hh in range(HB)]
        heads = lambda ref: tuple(ref[:, sl] for sl in sls)
        S0 = tuple(s_scr[hh] for hh in range(HB))
        for hh in range(HB):
            st_ref[hh] = S0[hh]
        o, S1 = _chunk_fn(kc, False, heads(q_ref), heads(k_ref), heads(v_ref), tuple(g_ref[hh] for hh in range(HB)),
                          tuple(b_ref[hh] for hh in range(HB)), S0)
        for hh in range(HB):
            o_ref[:, sls[hh]] = o[hh]
            s_scr[hh] = S1[hh]

    part = lambda p: pl.BlockSpec((None, C, HB * HEAD), lambda hb, n: (p, n, hb))
    gate = pl.BlockSpec((HB, C, HEAD), lambda hb, n: (hb, n, 0))
    return pl.pallas_call(
        body, name=name, grid=(H // HB, N), in_specs=[part(0), part(1), part(2), gate, gate],
        out_specs=(pl.BlockSpec((C, HB * HEAD), lambda hb, n: (n, hb)),
                   pl.BlockSpec((None, HB, HEAD, HEAD), lambda hb, n: (n, hb, 0, 0))),
        out_shape=(SDS((S, W), f32), SDS((N, H, HEAD, HEAD), f32)),
        scratch_shapes=[pltpu.VMEM((HB, HEAD, HEAD), f32)], compiler_params=_params(2),
    )(qkv, qkv, qkv, gB, bB)


def dn_chunk_bwd(qkv, gB, bB, states, do, name):
    _, S, W = qkv.shape
    H, C = W // HEAD, DN_CHUNK
    N, HB = S // C, _heads_per_block(H)

    def body(q_ref, k_ref, v_ref, g_ref, b_ref, st_ref, do_ref, dqkv_ref, dg_ref, db_ref, ds_scr):
        @pl.when(pl.program_id(1) == 0)
        def _():
            ds_scr[...] = jnp.zeros_like(ds_scr)

        kc = _chunk_consts()
        sls = [slice(hh * HEAD, (hh + 1) * HEAD) for hh in range(HB)]
        heads = lambda ref: tuple(ref[:, sl] for sl in sls)
        lead = lambda ref: tuple(ref[hh] for hh in range(HB))
        _, vjp = jax.vjp(functools.partial(_chunk_fn, kc, True), heads(q_ref), heads(k_ref), heads(v_ref),
                         lead(g_ref), lead(b_ref), lead(st_ref))
        dq, dk, dv, dg, db, dS0 = vjp((heads(do_ref), lead(ds_scr)))
        for hh in range(HB):
            dqkv_ref[0, :, sls[hh]] = dq[hh]
            dqkv_ref[1, :, sls[hh]] = dk[hh]
            dqkv_ref[2, :, sls[hh]] = dv[hh]
            dg_ref[hh] = dg[hh]
            db_ref[hh] = db[hh]
            ds_scr[hh] = dS0[hh]

    rev = lambda n: N - 1 - n
    part = lambda p: pl.BlockSpec((None, C, HB * HEAD), lambda hb, n: (p, rev(n), hb))
    gate = pl.BlockSpec((HB, C, HEAD), lambda hb, n: (hb, rev(n), 0))
    return pl.pallas_call(
        body, name=name, grid=(H // HB, N),
        in_specs=[part(0), part(1), part(2), gate, gate,
                  pl.BlockSpec((None, HB, HEAD, HEAD), lambda hb, n: (rev(n), hb, 0, 0)),
                  pl.BlockSpec((C, HB * HEAD), lambda hb, n: (rev(n), hb))],
        out_specs=(pl.BlockSpec((3, C, HB * HEAD), lambda hb, n: (0, rev(n), hb)), gate, gate),
        out_shape=(SDS((3, S, W), f32), SDS((H, S, HEAD), f32), SDS((H, S, HEAD), f32)),
        scratch_shapes=[pltpu.VMEM((HB, HEAD, HEAD), f32)], compiler_params=_params(2),
    )(qkv, qkv, qkv, gB, bB, states, do)


def _gate_norm(o, z, ng):
    return _rms(o, ng) * _silu(z)


def dn_out(o, proj, ng, wout, x1, g3, name):
    S, W = o.shape
    D = x1.shape[1]
    nh = W // HEAD
    tm = _row_tile(S, 256)

    def body(o_ref, z_ref, ng_ref, w_ref, x_ref, g_ref, xo_ref, m_ref, og_ref):
        for h in range(nh):
            sl = slice(h * HEAD, (h + 1) * HEAD)
            og_ref[:, sl] = _gate_norm(o_ref[:, sl], z_ref[:, sl], ng_ref[...]).astype(og_ref.dtype)
        m = _mm(og_ref[...], w_ref[...])
        m_ref[...] = m
        xo_ref[...] = x_ref[...] + _rms(m, g_ref[...])

    rw = pl.BlockSpec((tm, W), lambda i: (i, 0))
    rd = pl.BlockSpec((tm, D), lambda i: (i, 0))
    return pl.pallas_call(
        body, name=name, grid=(S // tm,),
        in_specs=[rw, pl.BlockSpec((tm, W), lambda i: (i, 3)), pl.BlockSpec((1, HEAD), lambda i: (0, 0)),
                  pl.BlockSpec((W, D), lambda i: (0, 0)), rd, pl.BlockSpec((1, D), lambda i: (0, 0))],
        out_specs=(rd, rd, rw),
        out_shape=(SDS((S, D), f32), SDS((S, D), f32), SDS((S, W), MXU_DTYPE)), compiler_params=_params(1),
    )(o, proj, ng, wout, x1, g3)


def dn_out_bwd(dxo, m, g3, o, proj, ng, wout, name):
    S, W = o.shape
    D = m.shape[1]
    nh = W // HEAD
    tm = _row_tile(S, 256)

    def body(dxo_ref, m_ref, g_ref, o_ref, z_ref, ng_ref, w_ref, dm_ref, do_ref, dz_ref, dng_ref, dg_ref):
        @pl.when(pl.program_id(0) == 0)
        def _():
            dng_ref[...] = jnp.zeros_like(dng_ref)
            dg_ref[...] = jnp.zeros_like(dg_ref)

        dm, dg = _rms_bwd(m_ref[...], g_ref[...], dxo_ref[...])
        dg_ref[...] += dg
        dmc = dm.astype(dm_ref.dtype)
        dm_ref[...] = dmc
        dog = _mm_nt(dmc, w_ref[...])
        for h in range(nh):
            sl = slice(h * HEAD, (h + 1) * HEAD)
            _, vjp = jax.vjp(_gate_norm, o_ref[:, sl], z_ref[:, sl], ng_ref[...])
            do, dz, dng = vjp(dog[:, sl])
            do_ref[:, sl] = do
            dz_ref[:, sl] = dz.astype(dz_ref.dtype)
            dng_ref[...] += dng

    rw = pl.BlockSpec((tm, W), lambda i: (i, 0))
    rd = pl.BlockSpec((tm, D), lambda i: (i, 0))
    vd = pl.BlockSpec((1, D), lambda i: (0, 0))
    vh = pl.BlockSpec((1, HEAD), lambda i: (0, 0))
    return pl.pallas_call(
        body, name=name, grid=(S // tm,),
        in_specs=[rd, rd, vd, rw, pl.BlockSpec((tm, W), lambda i: (i, 3)), vh, pl.BlockSpec((W, D), lambda i: (0, 0))],
        out_specs=(rd, rw, rw, vh, vd),
        out_shape=(SDS((S, D), MXU_DTYPE), SDS((S, W), f32), SDS((S, W), MXU_DTYPE), SDS((1, HEAD), f32),
                   SDS((1, D), f32)),
        compiler_params=_params(1),
    )(dxo, m, g3, o, proj, ng, wout)


def _sg_stage1(pu, pv, bu, bv, lg, lb):
    u = _gelu(pu + bu)
    t = _gelu(pv + bv)
    tc = t - jnp.mean(t, axis=-1, keepdims=True)
    v = tc * lax.rsqrt(jnp.mean(tc * tc, axis=-1, keepdims=True) + LN_EPS) * lg + lb
    return u, v


def _causal_mask(n):
    return lax.broadcasted_iota(jnp.int32, (n, n), 0) >= lax.broadcasted_iota(jnp.int32, (n, n), 1)


def sg_mid(pre, b_in, ln_g, ln_b, w_s, bsT, wout, x1, g3, name):
    S = pre.shape[0]
    E, D = ln_g.shape[1], x1.shape[1]
    G, CH = SG_GROUPS, SG_CHUNK
    Cg = E // G
    tm = _row_tile(S, 256)

    def body(pu_ref, pv_ref, bu_ref, bv_ref, lg_ref, lb_ref, ws_ref, bs_ref, w_ref, x_ref, g_ref,
             xo_ref, m_ref, gt_ref):
        u, v = _sg_stage1(pu_ref[...], pv_ref[...], bu_ref[...], bv_ref[...], lg_ref[...], lb_ref[...])
        mask = _causal_mask(CH)
        for g in range(G):
            wc = _c(jnp.where(mask, ws_ref[g], 0.0))
            bcol = bs_ref[:, g:g + 1]
            cs = slice(g * Cg, (g + 1) * Cg)
            for ch in range(tm // CH):
                rs = slice(ch * CH, (ch + 1) * CH)
                mixed = _mm(wc, _c(v[rs, cs])) + bcol
                gt_ref[rs, cs] = (u[rs, cs] * mixed).astype(gt_ref.dtype)
        m = _mm(gt_ref[...], w_ref[...])
        m_ref[...] = m
        xo_ref[...] = x_ref[...] + _rms(m, g_ref[...])

    half = lambda p: pl.BlockSpec((tm, E), lambda i: (i, p))
    vhalf = lambda p: pl.BlockSpec((1, E), lambda i: (0, p))
    ve = pl.BlockSpec((1, E), lambda i: (0, 0))
    rd = pl.BlockSpec((tm, D), lambda i: (i, 0))
    return pl.pallas_call(
        body, name=name, grid=(S // tm,),
        in_specs=[half(0), half(1), vhalf(0), vhalf(1), ve, ve, pl.BlockSpec((G, CH, CH), lambda i: (0, 0, 0)),
                  pl.BlockSpec((CH, G), lambda i: (0, 0)), pl.BlockSpec((E, D), lambda i: (0, 0)), rd,
                  pl.BlockSpec((1, D), lambda i: (0, 0))],
        out_specs=(rd, rd, pl.BlockSpec((tm, E), lambda i: (i, 0))),
        out_shape=(SDS((S, D), f32), SDS((S, D), f32), SDS((S, E), MXU_DTYPE)), compiler_params=_params(1),
    )(pre, pre, b_in, b_in, ln_g, ln_b, w_s, bsT, wout, x1, g3)


def sg_mid_bwd(dxo, m, g3, pre, b_in, ln_g, ln_b, w_s, bsT, wout, name):
    S = pre.shape[0]
    E, D = ln_g.shape[1], m.shape[1]
    G, CH = SG_GROUPS, SG_CHUNK
    Cg = E // G
    tm = _row_tile(S, 256)

    def body(dxo_ref, m_ref, g_ref, pu_ref, pv_ref, bu_ref, bv_ref, lg_ref, lb_ref, ws_ref, bs_ref, w_ref,
             dm_ref, dpre_ref, dbin_ref, dlg_ref, dlb_ref, dws_ref, dbs_ref, dg_ref, du_scr, dv_scr):
        @pl.when(pl.program_id(0) == 0)
        def _():
            for r in (dbin_ref, dlg_ref, dlb_ref, dws_ref, dbs_ref, dg_ref):
                r[...] = jnp.zeros_like(r)

        dm, dg = _rms_bwd(m_ref[...], g_ref[...], dxo_ref[...])
        dg_ref[...] += dg
        dmc = dm.astype(dm_ref.dtype)
        dm_ref[...] = dmc
        dgated = _mm_nt(dmc, w_ref[...])
        (u, v), vjp1 = jax.vjp(_sg_stage1, pu_ref[...], pv_ref[...], bu_ref[...], bv_ref[...], lg_ref[...],
                               lb_ref[...])
        mask = _causal_mask(CH)
        lane = lax.broadcasted_iota(jnp.int32, (CH, CH), 1)
        for g in range(G):
            wc = _c(jnp.where(mask, ws_ref[g], 0.0))
            bcol = bs_ref[:, g:g + 1]
            cs = slice(g * Cg, (g + 1) * Cg)
            dws = jnp.zeros((CH, CH), f32)
            dbs = jnp.zeros((CH, 1), f32)
            for ch in range(tm // CH):
                rs = slice(ch * CH, (ch + 1) * CH)
                vs = _c(v[rs, cs])
                mixed = _mm(wc, vs) + bcol
                dgt = dgated[rs, cs]
                du_scr[rs, cs] = dgt * mixed
                dmixed = dgt * u[rs, cs]
                dmc2 = _c(dmixed)
                dv_scr[rs, cs] = _mm_tn(wc, dmc2)
                dws = dws + _mm_nt(dmc2, vs)
                dbs = dbs + jnp.sum(dmixed, axis=1, keepdims=True)
            dws_ref[g] += jnp.where(mask, dws, 0.0)
            dbs_ref[...] += jnp.where(lane == g, jnp.broadcast_to(dbs, (CH, CH)), 0.0)
        dpu, dpv, dbu, dbv, dlg, dlb = vjp1((du_scr[...], dv_scr[...]))
        dpre_ref[:, :E] = dpu.astype(dpre_ref.dtype)
        dpre_ref[:, E:] = dpv.astype(dpre_ref.dtype)
        dbin_ref[:, :E] += dbu
        dbin_ref[:, E:] += dbv
        dlg_ref[...] += dlg
        dlb_ref[...] += dlb

    half = lambda p: pl.BlockSpec((tm, E), lambda i: (i, p))
    vhalf = lambda p: pl.BlockSpec((1, E), lambda i: (0, p))
    ve = pl.BlockSpec((1, E), lambda i: (0, 0))
    rd = pl.BlockSpec((tm, D), lambda i: (i, 0))
    vd = pl.BlockSpec((1, D), lambda i: (0, 0))
    wsb = pl.BlockSpec((G, CH, CH), lambda i: (0, 0, 0))
    return pl.pallas_call(
        body, name=name, grid=(S // tm,),
        in_specs=[rd, rd, vd, half(0), half(1), vhalf(0), vhalf(1), ve, ve, wsb,
                  pl.BlockSpec((CH, G), lambda i: (0, 0)), pl.BlockSpec((E, D), lambda i: (0, 0))],
        out_specs=(rd, pl.BlockSpec((tm, 2 * E), lambda i: (i, 0)), pl.BlockSpec((1, 2 * E), lambda i: (0, 0)), ve, ve,
                   wsb, pl.BlockSpec((CH, CH), lambda i: (0, 0)), vd),
        out_shape=(SDS((S, D), MXU_DTYPE), SDS((S, 2 * E), MXU_DTYPE), SDS((1, 2 * E), f32), SDS((1, E), f32),
                   SDS((1, E), f32), SDS((G, CH, CH), f32), SDS((CH, CH), f32), SDS((1, D), f32)),
        scratch_shapes=[pltpu.VMEM((tm, E), f32), pltpu.VMEM((tm, E), f32)], compiler_params=_params(1),
    )(dxo, m, g3, pre, pre, b_in, b_in, ln_g, ln_b, w_s, bsT, wout)


def loss_head(y, target, name):
    S, D = y.shape
    tm = _row_tile(S, 512)

    def body(y_ref, t_ref, l_ref, d_ref):
        @pl.when(pl.program_id(0) == 0)
        def _():
            l_ref[...] = jnp.zeros_like(l_ref)

        e = y_ref[...] - t_ref[...]
        d_ref[...] = e * (1.0 / D)
        l_ref[...] += jnp.sum(e * e) * (0.5 / D)

    row = pl.BlockSpec((tm, D), lambda i: (i, 0))
    return pl.pallas_call(
        body, name=name, grid=(S // tm,), in_specs=[row, row],
        out_specs=(pl.BlockSpec((1, HEAD), lambda i: (0, 0)), row),
        out_shape=(SDS((1, HEAD), f32), SDS((S, D), f32)), compiler_params=_params(1),
    )(y, target)


def sum_slots(r, name):
    _, R, C = r.shape
    tr = _row_tile(R, 648 if R % 648 == 0 else R)

    def body(r_ref, o_ref):
        acc = r_ref[0].astype(f32)
        for s in range(1, N_DEV):
            acc = acc + r_ref[s].astype(f32)
        o_ref[...] = acc

    return pl.pallas_call(
        body, name=name, grid=(R // tr,), in_specs=[pl.BlockSpec((N_DEV, tr, C), lambda i: (0, i, 0))],
        out_specs=pl.BlockSpec((tr, C), lambda i: (i, 0)), out_shape=SDS((R, C), f32), compiler_params=_params(1),
    )(r)


def _adam_math(w, g, m, v):
    m = ADAM_B1 * m + (1.0 - ADAM_B1) * g
    v = ADAM_B2 * v + (1.0 - ADAM_B2) * (g * g)
    m_hat = m / (1.0 - ADAM_B1 ** ADAM_STEP)
    v_hat = v / (1.0 - ADAM_B2 ** ADAM_STEP)
    delta = -ADAM_LR * (m_hat / (jnp.sqrt(v_hat) + ADAM_EPS) + ADAM_WD * w)
    return delta, m, v


def adam_slots(w, r, m, v, name, tr):
    R, C = w.shape
    tr = _row_tile(R, tr)

    def body(w_ref, r_ref, m_ref, v_ref, g_ref, d_ref, mo_ref, vo_ref):
        g = r_ref[0].astype(f32)
        for s in range(1, N_DEV):
            g = g + r_ref[s].astype(f32)
        g_ref[...] = g
        d_ref[...], mo_ref[...], vo_ref[...] = _adam_math(w_ref[...], g, m_ref[...], v_ref[...])

    row = pl.BlockSpec((tr, C), lambda i: (i, 0))
    return pl.pallas_call(
        body, name=name, grid=(R // tr,), in_specs=[row, pl.BlockSpec((N_DEV, tr, C), lambda i: (0, i, 0)), row, row],
        out_specs=(row, row, row, row), out_shape=tuple(SDS((R, C), f32) for _ in range(4)),
        compiler_params=_params(1),
    )(w, r, m, v)


def adam_small(w, g, m, v, name):
    def body(w_ref, g_ref, m_ref, v_ref, d_ref, mo_ref, vo_ref):
        d_ref[...], mo_ref[...], vo_ref[...] = _adam_math(w_ref[...], g_ref[...], m_ref[...], v_ref[...])

    return pl.pallas_call(body, name=name, out_shape=tuple(SDS(w.shape, f32) for _ in range(3)))(w, g, m, v)


def _pack_rows(parts):
    rows, offs, r = [], [], 0
    for p in parts:
        flat = p.reshape(-1)
        n = -(-flat.shape[0] // HEAD)
        flat = jnp.pad(flat, (0, n * HEAD - flat.shape[0]))
        rows.append(flat.reshape(n, HEAD))
        offs.append((r, n))
        r += n
    pad = (-r) % 8
    if pad:
        rows.append(jnp.zeros((pad, HEAD), f32))
    return jnp.concatenate(rows, axis=0), offs


def kernel(x, norm_g, ffn_w_gate, ffn_w_up, ffn_w_down, dn_w_in, dn_conv_w, dn_a_log, dn_dt_bias, dn_norm_g, dn_w_out, sg_w_in, sg_b_in, sg_ln_g, sg_ln_b, sg_w_s, sg_b_s, sg_w_out, loss_target, m_norm_g, m_ffn_w_gate, m_ffn_w_up, m_ffn_w_down, m_dn_w_in, m_dn_conv_w, m_dn_a_log, m_dn_dt_bias, m_dn_norm_g, m_dn_w_out, m_sg_w_in, m_sg_b_in, m_sg_ln_g, m_sg_ln_b, m_sg_w_s, m_sg_b_s, m_sg_w_out, v_norm_g, v_ffn_w_gate, v_ffn_w_up, v_ffn_w_down, v_dn_w_in, v_dn_conv_w, v_dn_a_log, v_dn_dt_bias, v_dn_norm_g, v_dn_w_out, v_sg_w_in, v_sg_b_in, v_sg_ln_g, v_sg_ln_b, v_sg_w_s, v_sg_b_s, v_sg_w_out):
    weights = dict(norm_g=norm_g, ffn_w_gate=ffn_w_gate, ffn_w_up=ffn_w_up, ffn_w_down=ffn_w_down, dn_w_in=dn_w_in,
                   dn_conv_w=dn_conv_w, dn_a_log=dn_a_log, dn_dt_bias=dn_dt_bias, dn_norm_g=dn_norm_g,
                   dn_w_out=dn_w_out, sg_w_in=sg_w_in, sg_b_in=sg_b_in, sg_ln_g=sg_ln_g, sg_ln_b=sg_ln_b,
                   sg_w_s=sg_w_s, sg_b_s=sg_b_s, sg_w_out=sg_w_out)
    mom_m = dict(norm_g=m_norm_g, ffn_w_gate=m_ffn_w_gate, ffn_w_up=m_ffn_w_up, ffn_w_down=m_ffn_w_down,
                 dn_w_in=m_dn_w_in, dn_conv_w=m_dn_conv_w, dn_a_log=m_dn_a_log, dn_dt_bias=m_dn_dt_bias,
                 dn_norm_g=m_dn_norm_g, dn_w_out=m_dn_w_out, sg_w_in=m_sg_w_in, sg_b_in=m_sg_b_in,
                 sg_ln_g=m_sg_ln_g, sg_ln_b=m_sg_ln_b, sg_w_s=m_sg_w_s, sg_b_s=m_sg_b_s, sg_w_out=m_sg_w_out)
    mom_v = dict(norm_g=v_norm_g, ffn_w_gate=v_ffn_w_gate, ffn_w_up=v_ffn_w_up, ffn_w_down=v_ffn_w_down,
                 dn_w_in=v_dn_w_in, dn_conv_w=v_dn_conv_w, dn_a_log=v_dn_a_log, dn_dt_bias=v_dn_dt_bias,
                 dn_norm_g=v_dn_norm_g, dn_w_out=v_dn_w_out, sg_w_in=v_sg_w_in, sg_b_in=v_sg_b_in,
                 sg_ln_g=v_sg_ln_g, sg_ln_b=v_sg_ln_b, sg_w_s=v_sg_w_s, sg_b_s=v_sg_b_s, sg_w_out=v_sg_w_out)
    order = list(weights)

    xs = x[0]
    S, D = xs.shape
    F8 = ffn_w_gate.shape[-1]
    depth = norm_g.shape[0]
    W = dn_w_out.shape[1] * N_DEV
    H = W // HEAD
    E = sg_ln_g.shape[1] * N_DEV
    G, CH = sg_w_s.shape[1], sg_w_s.shape[2]
    c8 = dn_w_in.shape[2]
    me = _slot(lax.axis_index("x"), lax.axis_index("y"), lax.axis_index("c"))

    small_in, small_offs = _pack_rows([norm_g, dn_conv_w, sg_b_in, sg_ln_g, sg_ln_b])
    gathered = all_gather_multi(
        [_c(ffn_w_gate), _c(ffn_w_up), _c(ffn_w_down), _c(dn_w_in[0]), _c(dn_w_out[0]), _c(sg_w_in[0]),
         _c(sg_w_out[0]), small_in], name="gather_weights")
    wg_all, wu_all, wd_all, dnin_all, dnout_all, sgin_all, sgout_all, small_all = gathered
    per = N_DEV // FFN_SLABS
    wide_cols = lambda w: jnp.transpose(w.reshape(FFN_SLABS, per, depth, 2, D, F8), (0, 2, 3, 4, 1, 5)).reshape(
        FFN_SLABS, depth, 2, D, per * F8)
    wg_all, wu_all = wide_cols(wg_all), wide_cols(wu_all)
    wd_all = jnp.transpose(wd_all.reshape(FFN_SLABS, per, depth, 2, F8, D), (0, 2, 3, 1, 4, 5)).reshape(
        FFN_SLABS, depth, 2, per * F8, D)
    shard_cols = lambda dw: jnp.transpose(dw.reshape(FFN_SLABS, D, per, F8), (0, 2, 1, 3)).reshape(N_DEV, D, F8)

    def small_piece(i, shard_shape):
        r0, n = small_offs[i]
        sz = math.prod(shard_shape)
        return small_all[:, r0:r0 + n, :].reshape(N_DEV, n * HEAD)[:, :sz].reshape((N_DEV,) + tuple(shard_shape))

    ng_full = jnp.moveaxis(small_piece(0, norm_g.shape), 0, 2).reshape(depth, 6, D)
    conv_full = jnp.moveaxis(small_piece(1, dn_conv_w.shape[1:]), 0, 1).reshape(CONV_K, 3 * W)
    bin_full = small_piece(2, sg_b_in.shape[1:]).reshape(1, 2 * E)
    lng_full = small_piece(3, sg_ln_g.shape[1:]).reshape(1, E)
    lnb_full = small_piece(4, sg_ln_b.shape[1:]).reshape(1, E)
    dn_win = jnp.moveaxis(dnin_all, 0, 1).reshape(D, N_DEV * c8)
    dn_wmain = dn_win[:, :4 * W]
    dn_wba = jnp.pad(dn_win[:, 4 * W:], ((0, 0), (0, HEAD - 2 * H)))
    dn_wout = dnout_all.reshape(W, D)
    sg_win = jnp.moveaxis(sgin_all, 0, 1).reshape(D, 2 * E)
    sg_wout = sgout_all.reshape(E, D)
    alog_b = jnp.broadcast_to(dn_a_log.reshape(H, 1, 1), (H, 1, HEAD))
    dtb_b = jnp.broadcast_to(dn_dt_bias.reshape(H, 1, 1), (H, 1, HEAD))
    bsT = sg_b_s[0].T
    gvec = lambda l, k: ng_full[l, k].reshape(1, D)

    saved = []
    cur = xs
    for l in range(depth):
        sv = {}
        sv['x0'] = cur
        cur, sv['hA'], sv['aA'], sv['bA'], sv['yA'] = ffn_fwd(cur, gvec(l, 0), gvec(l, 1), wg_all, wu_all, wd_all, l, 0,
                                                              name=f"ffn_fwd_{l}a")
        sv['x1'] = cur
        if l % 2 == 0:
            sv['hM'], sv['proj'], sv['pba'] = rms_mm(cur, gvec(l, 2), dn_wmain, dn_wba, name=f"dn_in_{l}")
            sv['qkv'] = dn_prep(sv['proj'], conv_full, name=f"dn_prep_{l}")
            sv['beta'], sv['g'] = dn_gates(sv['pba'], alog_b, dtb_b, name=f"dn_gates_{l}")
            sv['o'], sv['states'] = dn_chunk_fwd(sv['qkv'], sv['g'], sv['beta'], name=f"dn_chunk_{l}")
            cur, sv['m'], sv['og'] = dn_out(sv['o'], sv['proj'], dn_norm_g, dn_wout, cur, gvec(l, 3), name=f"dn_out_{l}")
        else:
            sv['hM'], sv['pre'] = rms_mm(cur, gvec(l, 2), sg_win, None, name=f"sg_in_{l}")
            cur, sv['m'], sv['gated'] = sg_mid(sv['pre'], bin_full, lng_full, lnb_full, sg_w_s[0], bsT, sg_wout, cur,
                                               gvec(l, 3), name=f"sg_mid_{l}")
        sv['x2'] = cur
        cur, sv['hB'], sv['aB'], sv['bB'], sv['yB'] = ffn_fwd(cur, gvec(l, 4), gvec(l, 5), wg_all, wu_all, wd_all, l, 1,
                                                              name=f"ffn_fwd_{l}b")
        saved.append(sv)

    loss_blk, dcur = loss_head(cur, loss_target[0], name="loss_head")
    loss = lax.psum(loss_blk[0, 0], ("x", "y", "c"))

    dng = [[None] * 6 for _ in range(depth)]
    dwg = [[None, None] for _ in range(depth)]
    dwu = [[None, None] for _ in range(depth)]
    dwd = [[None, None] for _ in range(depth)]
    grads = {}
    for l in reversed(range(depth)):
        sv = saved[l]
        dcur, da, db, dy, dng[l][4], dng[l][5] = ffn_bwd_dx(dcur, sv['x2'], sv['yB'], sv['aB'], sv['bB'], gvec(l, 4),
                                                            gvec(l, 5), wg_all, wu_all, wd_all, l, 1, name=f"ffn_bwd_{l}b")
        dwg[l][1], dwu[l][1], dwd[l][1] = ffn_bwd_dw(sv['hB'], dy, sv['aB'], sv['bB'], da, db, name=f"ffn_dw_{l}b")
        if l % 2 == 0:
            dm, do, dz, grads['dn_norm_g'], dng[l][3] = dn_out_bwd(dcur, sv['m'], gvec(l, 3), sv['o'], sv['proj'],
                                                                  dn_norm_g, dn_wout, name=f"dn_out_bwd_{l}")
            grads['dn_w_out'] = tn_mm(sv['og'], dm, name=f"dn_wout_dw_{l}").reshape(N_DEV, W // N_DEV, D)
            dqkv, dgB, dbB = dn_chunk_bwd(sv['qkv'], sv['g'], sv['beta'], sv['states'], do, name=f"dn_chunk_bwd_{l}")
            dpba, dal, ddt = dn_gates_bwd(sv['pba'], alog_b, dtb_b, dbB, dgB, name=f"dn_gates_bwd_{l}")
            grads['dn_a_log'] = dal[:, 0, 0].reshape(1, H)
            grads['dn_dt_bias'] = ddt[:, 0, 0].reshape(1, H)
            dproj, grads['dn_conv_w'] = dn_prep_bwd(sv['proj'], conv_full, dqkv, dz, name=f"dn_prep_bwd_{l}")
            dw_main = tn_mm(sv['hM'], dproj, name=f"dn_win_dw_{l}")
            dw_ba = tn_mm(sv['hM'], dpba, name=f"dn_wba_dw_{l}", tn=HEAD)
            dw_in = jnp.concatenate([dw_main, dw_ba[:, :2 * H]], axis=1)
            grads['dn_w_in'] = jnp.moveaxis(dw_in.reshape(D, N_DEV, c8), 1, 0)
            dcur, dng[l][2] = mm_bwd_dx(dcur, sv['x1'], gvec(l, 2), dproj, dn_wmain, dpba, dn_wba, name=f"dn_in_bwd_{l}")
        else:
            dm, dpre, grads['sg_b_in'], grads['sg_ln_g'], grads['sg_ln_b'], grads['sg_w_s'], dbs, dng[l][3] = sg_mid_bwd(
                dcur, sv['m'], gvec(l, 3), sv['pre'], bin_full, lng_full, lnb_full, sg_w_s[0], bsT, sg_wout,
                name=f"sg_mid_bwd_{l}")
            grads['sg_b_s'] = dbs[:, :G].T
            grads['sg_w_out'] = tn_mm(sv['gated'], dm, name=f"sg_wout_dw_{l}").reshape(N_DEV, E // N_DEV, D)
            grads['sg_w_in'] = tn_mm(sv['hM'], dpre, name=f"sg_win_dw_{l}", tn=2 * E // N_DEV, slot_major=True)
            dcur, dng[l][2] = mm_bwd_dx(dcur, sv['x1'], gvec(l, 2), dpre, sg_win, None, None, name=f"sg_in_bwd_{l}")
        dcur, da, db, dy, dng[l][0], dng[l][1] = ffn_bwd_dx(dcur, sv['x0'], sv['yA'], sv['aA'], sv['bA'], gvec(l, 0),
                                                            gvec(l, 1), wg_all, wu_all, wd_all, l, 0, name=f"ffn_bwd_{l}a")
        dwg[l][0], dwu[l][0], dwd[l][0] = ffn_bwd_dw(sv['hA'], dy, sv['aA'], sv['bA'], da, db, name=f"ffn_dw_{l}a")
    grad_x = dcur[None]

    stack4 = lambda t, f: jnp.stack([jnp.stack([f(e) for e in r], axis=1) for r in t], axis=1)
    big = [stack4(dwg, shard_cols), stack4(dwu, shard_cols), stack4(dwd, lambda dw: dw.reshape(N_DEV, F8, D)),
           grads['dn_w_in'], grads['dn_w_out'], grads['sg_w_in'],
           grads['sg_w_out']]
    big_names = ['ffn_w_gate', 'ffn_w_up', 'ffn_w_down', 'dn_w_in', 'dn_w_out', 'sg_w_in', 'sg_w_out']
    slots = exchange_slots(big, name="exchange_grads")

    dng_full = jnp.stack([jnp.concatenate(r, axis=0) for r in dng], axis=0)
    small_names = ['norm_g', 'dn_conv_w', 'sg_b_in', 'sg_ln_g', 'sg_ln_b', 'sg_w_s', 'sg_b_s', 'dn_a_log',
                   'dn_dt_bias', 'dn_norm_g']
    small_parts = [dng_full, grads['dn_conv_w'], grads['sg_b_in'], grads['sg_ln_g'], grads['sg_ln_b'],
                   grads['sg_w_s'], grads['sg_b_s'], grads['dn_a_log'], grads['dn_dt_bias'], grads['dn_norm_g']]
    small_pack, offs = _pack_rows(small_parts)
    (small_slots,) = all_gather_multi([small_pack], name="gather_small_grads")
    small_sum = sum_slots(small_slots, name="sum_small_grads")

    def small_grad(i):
        r0, n = offs[i]
        p = small_parts[i]
        return small_sum[r0:r0 + n].reshape(-1)[:p.size].reshape(p.shape)

    def my_shard(full, axis, like):
        n = full.shape[axis] // N_DEV
        return lax.dynamic_slice_in_dim(full, me * n, n, axis).reshape(like.shape)

    g_small = {
        'norm_g': my_shard(small_grad(0), 2, norm_g),
        'dn_conv_w': my_shard(small_grad(1), 1, dn_conv_w),
        'sg_b_in': my_shard(small_grad(2), 1, sg_b_in),
        'sg_ln_g': my_shard(small_grad(3), 1, sg_ln_g),
        'sg_ln_b': my_shard(small_grad(4), 1, sg_ln_b),
        'sg_w_s': small_grad(5).reshape(sg_w_s.shape),
        'sg_b_s': small_grad(6).reshape(sg_b_s.shape),
        'dn_a_log': small_grad(7).reshape(dn_a_log.shape),
        'dn_dt_bias': small_grad(8).reshape(dn_dt_bias.shape),
        'dn_norm_g': small_grad(9).reshape(dn_norm_g.shape),
    }

    out_g, out_d, out_m, out_v = {}, {}, {}, {}
    for nm, r in zip(big_names, slots):
        w = weights[nm]
        cols = w.shape[-1]
        rows = w.size // cols
        tr = {'ffn_w_gate': 512, 'ffn_w_up': 512, 'ffn_w_down': F8, 'dn_w_in': 256, 'sg_w_in': 256}.get(nm, rows)
        g, d, m2, v2 = adam_slots(w.reshape(rows, cols), r.reshape(N_DEV, rows, cols), mom_m[nm].reshape(rows, cols),
                                  mom_v[nm].reshape(rows, cols), name=f"adam_{nm}", tr=tr)
        out_g[nm], out_d[nm], out_m[nm], out_v[nm] = (t.reshape(w.shape) for t in (g, d, m2, v2))
    for nm in small_names:
        w = weights[nm]
        cols = w.shape[-1]
        rows = w.size // cols
        two = lambda t: t.reshape(rows, cols)
        d, m2, v2 = adam_small(two(w), two(g_small[nm]), two(mom_m[nm]), two(mom_v[nm]), name=f"adam_{nm}")
        out_g[nm] = g_small[nm]
        out_d[nm], out_m[nm], out_v[nm] = (t.reshape(w.shape) for t in (d, m2, v2))

    return (loss, grad_x, *[out_g[n] for n in order], *[out_d[n] for n in order], *[out_m[n] for n in order],
            *[out_v[n] for n in order])
```

```python
import functools
import math

import jax
import jax.numpy as jnp
from jax import lax
from jax.experimental import pallas as pl
from jax.experimental.pallas import tpu as pltpu

f32 = jnp.float32
MXU_DTYPE = jnp.bfloat16
N_DEV = 8
RMS_EPS = 1e-6
LN_EPS = 1e-5
L2_EPS = 1e-6
HEAD = 128
DN_CHUNK = 64
SG_CHUNK = 128
SG_GROUPS = 8
CONV_K = 4
ADAM_LR, ADAM_B1, ADAM_B2, ADAM_EPS, ADAM_WD, ADAM_STEP = 0.001, 0.9, 0.999, 1e-08, 0.01, 10
VMEM_LIMIT = 56 * 1024 * 1024
FFN_ROWS_FWD, FFN_ROWS_BWD, FFN_ROWS_DW = 512, 512, 1024
FFN_SLABS = 4
SDS = jax.ShapeDtypeStruct
HIGHEST = lax.Precision.HIGHEST
MESH = pl.DeviceIdType.MESH


def _params(n_grid):
    return pltpu.CompilerParams(dimension_semantics=("arbitrary",) * n_grid, vmem_limit_bytes=VMEM_LIMIT)


def _row_tile(s, want):
    t = min(s, want)
    assert s % t == 0, (s, t)
    return t


def _rms(x, g):
    return x * lax.rsqrt(jnp.mean(x * x, axis=-1, keepdims=True) + RMS_EPS) * g


def _rms_bwd(x, g, dy):
    _, vjp = jax.vjp(_rms, x, g)
    return vjp(dy)


def _silu(a):
    return a * jax.nn.sigmoid(a)


def _gelu(x):
    return 0.5 * x * (1.0 + lax.erf(x * 0.7071067811865476))


def _mm(a, b):
    return lax.dot_general(a, b, (((1,), (0,)), ((), ())), preferred_element_type=f32)


def _mm_nt(a, b):
    return lax.dot_general(a, b, (((1,), (1,)), ((), ())), preferred_element_type=f32)


def _mm_tn(a, b):
    return lax.dot_general(a, b, (((0,), (0,)), ((), ())), preferred_element_type=f32)


def _c(x):
    return x.astype(MXU_DTYPE)


def _split(a):
    hi = a.astype(MXU_DTYPE)
    lo = (a - hi.astype(f32)).astype(MXU_DTYPE)
    return hi, lo


def _dot3(a, b, dims):
    ah, al = _split(a)
    bh, bl = _split(b)
    d = lambda p, q: lax.dot_general(p, q, (dims, ((), ())), preferred_element_type=f32)
    return d(ah, bh) + (d(ah, bl) + d(al, bh))


NN, NT, TN = ((1,), (0,)), ((1,), (1,)), ((0,), (0,))


def _slot(px, py, pc):
    return 4 * px + 2 * py + pc


def all_gather_multi(arrs, name):
    return Comm("gather", arrs).alone(name)


def exchange_slots(arrs, name):
    return Comm("exchange", arrs).alone(name)


class Comm:
    def __init__(self, kind, arrs):
        self.kind, self.arrs, self.n = kind, list(arrs), len(arrs)
        hbm = pl.BlockSpec(memory_space=pltpu.HBM)
        self.in_specs = [hbm] * self.n
        self.out_specs = [hbm] * self.n
        lead = (N_DEV,) if kind == "gather" else ()
        self.out_shape = [SDS(lead + tuple(a.shape), a.dtype) for a in self.arrs]
        self.scratch = [pltpu.SemaphoreType.DMA((self.n, 7)), pltpu.SemaphoreType.DMA((self.n, 7)),
                        pltpu.SemaphoreType.DMA((self.n,))]

    def phase(self, p, ins, outs, sems):
        (self._gather if self.kind == "gather" else self._exchange)(p, ins, outs, sems)

    def _gather(self, p, ins, outs, sems):
        send_sems, recv_sems, local_sems = sems
        x, y, c = lax.axis_index("x"), lax.axis_index("y"), lax.axis_index("c")
        me, sibling = (x, y, c), (x, y, 1 - c)
        chips = [(1 - x, y), (x, 1 - y), (1 - x, 1 - y)]

        def copy(a, k, block, to, src=None):
            dst = outs[a].at[_slot(*block)]
            return pltpu.make_async_remote_copy(
                src_ref=dst if src is None else src, dst_ref=dst, send_sem=send_sems.at[a, k],
                recv_sem=recv_sems.at[a, k], device_id=to, device_id_type=MESH)

        mine = [pltpu.make_async_copy(ins[a], outs[a].at[_slot(*me)], local_sems.at[a]) for a in range(self.n)]
        first = [[copy(a, 0, me, sibling, src=ins[a])] +
                 [copy(a, 1 + j, me, (*chip, c), src=ins[a]) for j, chip in enumerate(chips)] for a in range(self.n)]
        passed = [[copy(a, 4 + j, (*chip, c), sibling) for j, chip in enumerate(chips)] for a in range(self.n)]
        if p == 0:
            for a in range(self.n):
                mine[a].start()
            for a in range(self.n):
                for cp in first[a]:
                    cp.start()
        elif p == 1:
            for a in range(self.n):
                for j, chip in enumerate(chips):
                    copy(a, 1 + j, (*chip, c), me).wait_recv()
                    passed[a][j].start()
        else:
            for a in range(self.n):
                copy(a, 0, sibling, me).wait_recv()
                for j, chip in enumerate(chips):
                    copy(a, 4 + j, (*chip, 1 - c), me).wait_recv()
            for a in range(self.n):
                for cp in first[a] + passed[a]:
                    cp.wait_send()
                mine[a].wait()

    def _exchange(self, p, ins, outs, sems):
        send_sems, recv_sems, local_sems = sems
        x, y, c = lax.axis_index("x"), lax.axis_index("y"), lax.axis_index("c")
        me = _slot(x, y, c)
        peers = [(x ^ (k >> 2), y ^ ((k >> 1) & 1), c ^ (k & 1)) for k in range(1, N_DEV)]

        def copy(a, k):
            peer = peers[k - 1]
            return pltpu.make_async_remote_copy(
                src_ref=ins[a].at[_slot(*peer)], dst_ref=outs[a].at[me], send_sem=send_sems.at[a, k - 1],
                recv_sem=recv_sems.at[a, k - 1], device_id=peer, device_id_type=MESH)

        def landed(a, k):
            peer = peers[k - 1]
            return pltpu.make_async_remote_copy(
                src_ref=ins[a].at[me], dst_ref=outs[a].at[_slot(*peer)], send_sem=send_sems.at[a, k - 1],
                recv_sem=recv_sems.at[a, k - 1], device_id=peer, device_id_type=MESH)

        local = [pltpu.make_async_copy(ins[a].at[me], outs[a].at[me], local_sems.at[a]) for a in range(self.n)]
        order = [6, 7, 2, 3, 4, 5, 1]
        if p == 0:
            for a in range(self.n):
                local[a].start()
            for a in range(self.n):
                for k in order:
                    copy(a, k).start()
        elif p == 2:
            for a in range(self.n):
                for k in order:
                    copy(a, k).wait_send()
                    landed(a, k).wait_recv()
                local[a].wait()

    def alone(self, name):
        n = self.n

        def body(*refs):
            for p in range(3):
                self.phase(p, refs[:n], refs[n:2 * n], refs[2 * n:])

        return pl.pallas_call(body, name=name, out_shape=tuple(self.out_shape), in_specs=self.in_specs,
                              out_specs=tuple(self.out_specs), scratch_shapes=self.scratch)(*self.arrs)


def hosted_call(body, comm, steps, *, name, grid, in_specs, out_specs, out_shape, scratch_shapes, args):
    if comm is None:
        outs = pl.pallas_call(body, name=name, grid=grid, in_specs=in_specs, out_specs=tuple(out_specs),
                              out_shape=tuple(out_shape), scratch_shapes=scratch_shapes,
                              compiler_params=_params(len(grid)))(*args)
        return outs, None
    ni, no, ns, cn = len(in_specs), len(out_specs), len(scratch_shapes), comm.n

    def both(*refs):
        h_in, c_in = refs[:ni], refs[ni:ni + cn]
        h_out, c_out = refs[ni + cn:ni + cn + no], refs[ni + cn + no:ni + 2 * cn + no]
        h_scr, c_scr = refs[ni + 2 * cn + no:ni + 2 * cn + no + ns], refs[ni + 2 * cn + no + ns:]
        when = steps()
        pl.when(when[0])(lambda: comm.phase(0, c_in, c_out, c_scr))
        body(*h_in, *h_out, *h_scr)
        pl.when(when[1])(lambda: comm.phase(1, c_in, c_out, c_scr))
        pl.when(when[2])(lambda: comm.phase(2, c_in, c_out, c_scr))

    outs = pl.pallas_call(
        both, name=name, grid=grid, in_specs=list(in_specs) + comm.in_specs,
        out_specs=tuple(out_specs) + tuple(comm.out_specs), out_shape=tuple(out_shape) + tuple(comm.out_shape),
        scratch_shapes=list(scratch_shapes) + comm.scratch, compiler_params=_params(len(grid)),
    )(*args, *comm.arrs)
    return outs[:no], outs[no:]


def _grid_steps(n_outer, n_inner=1):
    total = n_outer * n_inner

    def steps():
        t = pl.program_id(0) * n_inner + (pl.program_id(1) if n_inner > 1 else 0)
        return t == 0, t == (total * 5) // 8, t == total - 1
    return steps


def ffn_fwd(x, gpre, gpost, wg, wu, wd, name, comm=None):
    S, D = x.shape
    nj, F8 = wg.shape[0], wg.shape[-1]
    tm = _row_tile(S, FFN_ROWS_FWD)

    def body(x_ref, gpre_ref, gpost_ref, wg_ref, wu_ref, wd_ref, xo_ref, h_ref, a_ref, b_ref, y_ref):
        j = pl.program_id(1)

        @pl.when(j == 0)
        def _():
            h_ref[...] = _rms(x_ref[...], gpre_ref[...]).astype(h_ref.dtype)
            y_ref[...] = jnp.zeros_like(y_ref)

        h = h_ref[...]
        a = _mm(h, wg_ref[...]).astype(a_ref.dtype)
        b = _mm(h, wu_ref[...]).astype(b_ref.dtype)
        a_ref[...] = a
        b_ref[...] = b
        t = _silu(a.astype(f32)) * b.astype(f32)
        y_ref[...] += _mm(_c(t), wd_ref[...])

        @pl.when(j == nj - 1)
        def _():
            xo_ref[...] = x_ref[...] + 0.5 * _rms(y_ref[...], gpost_ref[...])

    row = pl.BlockSpec((tm, D), lambda i, j: (i, 0))
    vec = pl.BlockSpec((1, D), lambda i, j: (0, 0))
    wcol = pl.BlockSpec((None, D, F8), lambda i, j: (j, 0, 0))
    wrow = pl.BlockSpec((None, F8, D), lambda i, j: (j, 0, 0))
    hid = pl.BlockSpec((None, tm, F8), lambda i, j: (j, i, 0))
    return hosted_call(
        body, comm, _grid_steps(S // tm, nj), name=name, grid=(S // tm, nj),
        in_specs=[row, vec, vec, wcol, wcol, wrow],
        out_specs=(row, row, hid, hid, row),
        out_shape=(SDS((S, D), f32), SDS((S, D), MXU_DTYPE), SDS((nj, S, F8), MXU_DTYPE),
                   SDS((nj, S, F8), MXU_DTYPE), SDS((S, D), f32)),
        scratch_shapes=[], args=(x, gpre, gpost, wg, wu, wd))


def ffn_bwd_dx(dxo, x, y, a, b, gpre, gpost, wg, wu, wd, name, comm=None):
    S, D = x.shape
    nj, F8 = wg.shape[0], wg.shape[-1]
    tm = _row_tile(S, FFN_ROWS_BWD)

    def body(dxo_ref, x_ref, y_ref, a_ref, b_ref, gpre_ref, gpost_ref, wg_ref, wu_ref, wd_ref,
             dx_ref, da_ref, db_ref, dy_ref, dgpre_ref, dgpost_ref, dh_ref):
        i, j = pl.program_id(0), pl.program_id(1)

        @pl.when(j == 0)
        def _():
            @pl.when(i == 0)
            def _():
                dgpre_ref[...] = jnp.zeros_like(dgpre_ref)
                dgpost_ref[...] = jnp.zeros_like(dgpost_ref)

            dy, dg = _rms_bwd(y_ref[...], gpost_ref[...], 0.5 * dxo_ref[...])
            dy_ref[...] = dy.astype(dy_ref.dtype)
            dgpost_ref[...] += dg
            dh_ref[...] = jnp.zeros_like(dh_ref)

        dt = _mm_nt(dy_ref[...], wd_ref[...])
        af, bf = a_ref[...].astype(f32), b_ref[...].astype(f32)
        s = jax.nn.sigmoid(af)
        da = (dt * bf * (s * (1.0 + af * (1.0 - s)))).astype(da_ref.dtype)
        db = (dt * (af * s)).astype(db_ref.dtype)
        da_ref[...] = da
        db_ref[...] = db
        dh_ref[...] += _mm_nt(da, wg_ref[...]) + _mm_nt(db, wu_ref[...])

        @pl.when(j == nj - 1)
        def _():
            dxx, dg = _rms_bwd(x_ref[...], gpre_ref[...], dh_ref[...])
            dx_ref[...] = dxo_ref[...] + dxx
            dgpre_ref[...] += dg

    row = pl.BlockSpec((tm, D), lambda i, j: (i, 0))
    vec = pl.BlockSpec((1, D), lambda i, j: (0, 0))
    wcol = pl.BlockSpec((None, D, F8), lambda i, j: (j, 0, 0))
    wrow = pl.BlockSpec((None, F8, D), lambda i, j: (j, 0, 0))
    hid = pl.BlockSpec((None, tm, F8), lambda i, j: (j, i, 0))
    return hosted_call(
        body, comm, _grid_steps(S // tm, nj), name=name, grid=(S // tm, nj),
        in_specs=[row, row, row, hid, hid, vec, vec, wcol, wcol, wrow],
        out_specs=(row, hid, hid, row, vec, vec),
        out_shape=(SDS((S, D), f32), SDS((nj, S, F8), MXU_DTYPE), SDS((nj, S, F8), MXU_DTYPE),
                   SDS((S, D), MXU_DTYPE), SDS((1, D), f32), SDS((1, D), f32)),
        scratch_shapes=[pltpu.VMEM((tm, D), f32)], args=(dxo, x, y, a, b, gpre, gpost, wg, wu, wd))


def ffn_bwd_dw(h, dy, a, b, da, db, name):
    S, D = h.shape
    F8 = a.shape[-1]
    tm = _row_tile(S, FFN_ROWS_DW)
    ni = S // tm

    def body(h_ref, dy_ref, a_ref, b_ref, da_ref, db_ref, dwg_ref, dwu_ref, dwd_ref, accg, accu, accd):
        i = pl.program_id(1)

        @pl.when(i == 0)
        def _():
            accg[...] = jnp.zeros_like(accg)
            accu[...] = jnp.zeros_like(accu)
            accd[...] = jnp.zeros_like(accd)

        t = _c(_silu(a_ref[...].astype(f32)) * b_ref[...].astype(f32))
        hh = h_ref[...]
        accg[...] += _mm_tn(hh, da_ref[...])
        accu[...] += _mm_tn(hh, db_ref[...])
        accd[...] += _mm_tn(t, dy_ref[...])

        @pl.when(i == ni - 1)
        def _():
            dwg_ref[...] = accg[...].astype(dwg_ref.dtype)
            dwu_ref[...] = accu[...].astype(dwu_ref.dtype)
            dwd_ref[...] = accd[...].astype(dwd_ref.dtype)

    row = pl.BlockSpec((tm, D), lambda j, i: (i, 0))
    hid = pl.BlockSpec((None, tm, F8), lambda j, i: (j, i, 0))
    wcol = pl.BlockSpec((None, D, F8), lambda j, i: (j, 0, 0))
    wrow = pl.BlockSpec((None, F8, D), lambda j, i: (j, 0, 0))
    return pl.pallas_call(
        body, name=name, grid=(a.shape[0], ni),
        in_specs=[row, row, hid, hid, hid, hid],
        out_specs=(wcol, wcol, wrow),
        out_shape=(SDS((a.shape[0], D, F8), MXU_DTYPE), SDS((a.shape[0], D, F8), MXU_DTYPE),
                   SDS((a.shape[0], F8, D), MXU_DTYPE)),
        scratch_shapes=[pltpu.VMEM((D, F8), f32), pltpu.VMEM((D, F8), f32), pltpu.VMEM((F8, D), f32)],
        compiler_params=_params(2),
    )(h, dy, a, b, da, db)


def rms_mm(x, g, w, w2, name, tn=1024):
    S, D = x.shape
    N = w.shape[1]
    tm = _row_tile(S, 512)
    tn = _row_tile(N, tn)
    has2 = w2 is not None

    def body(*refs):
        if has2:
            x_ref, g_ref, w_ref, w2_ref, h_ref, o_ref, o2_ref = refs
        else:
            x_ref, g_ref, w_ref, h_ref, o_ref = refs
        j = pl.program_id(1)

        @pl.when(j == 0)
        def _():
            h = _rms(x_ref[...], g_ref[...]).astype(h_ref.dtype)
            h_ref[...] = h
            if has2:
                o2_ref[...] = _mm(h, w2_ref[...])

        o_ref[...] = _mm(h_ref[...], w_ref[...])

    row = pl.BlockSpec((tm, D), lambda i, j: (i, 0))
    in_specs = [row, pl.BlockSpec((1, D), lambda i, j: (0, 0)), pl.BlockSpec((D, tn), lambda i, j: (0, j))]
    out_specs = [row, pl.BlockSpec((tm, tn), lambda i, j: (i, j))]
    out_shape = [SDS((S, D), MXU_DTYPE), SDS((S, N), f32)]
    args = [x, g, w]
    if has2:
        in_specs.append(pl.BlockSpec((D, w2.shape[1]), lambda i, j: (0, 0)))
        out_specs.append(pl.BlockSpec((tm, w2.shape[1]), lambda i, j: (i, 0)))
        out_shape.append(SDS((S, w2.shape[1]), f32))
        args.append(w2)
    return pl.pallas_call(
        body, name=name, grid=(S // tm, N // tn), in_specs=in_specs, out_specs=tuple(out_specs),
        out_shape=tuple(out_shape), compiler_params=_params(2),
    )(*args)


def mm_bwd_dx(dres, x, g, dy, w, dy2, w2, name, tk=1024):
    S, D = x.shape
    K = dy.shape[1]
    tm = _row_tile(S, 512)
    tk = _row_tile(K, tk)
    nk = K // tk
    has2 = dy2 is not None

    def body(*refs):
        if has2:
            dres_ref, x_ref, g_ref, dy_ref, w_ref, dy2_ref, w2_ref, dx_ref, dg_ref, dh_ref = refs
        else:
            dres_ref, x_ref, g_ref, dy_ref, w_ref, dx_ref, dg_ref, dh_ref = refs
        i, k = pl.program_id(0), pl.program_id(1)

        @pl.when(k == 0)
        def _():
            @pl.when(i == 0)
            def _():
                dg_ref[...] = jnp.zeros_like(dg_ref)

            if has2:
                dh_ref[...] = _mm_nt(dy2_ref[...], w2_ref[...])
            else:
                dh_ref[...] = jnp.zeros_like(dh_ref)

        dh_ref[...] += _mm_nt(dy_ref[...], w_ref[...])

        @pl.when(k == nk - 1)
        def _():
            dxx, dg = _rms_bwd(x_ref[...], g_ref[...], dh_ref[...])
            dx_ref[...] = dres_ref[...] + dxx
            dg_ref[...] += dg

    row = pl.BlockSpec((tm, D), lambda i, k: (i, 0))
    vec = pl.BlockSpec((1, D), lambda i, k: (0, 0))
    in_specs = [row, row, vec, pl.BlockSpec((tm, tk), lambda i, k: (i, k)), pl.BlockSpec((D, tk), lambda i, k: (0, k))]
    args = [dres, x, g, dy, w]
    if has2:
        in_specs += [pl.BlockSpec((tm, dy2.shape[1]), lambda i, k: (i, 0)),
                     pl.BlockSpec((D, w2.shape[1]), lambda i, k: (0, 0))]
        args += [dy2, w2]
    return pl.pallas_call(
        body, name=name, grid=(S // tm, nk), in_specs=in_specs, out_specs=(row, vec),
        out_shape=(SDS((S, D), f32), SDS((1, D), f32)),
        scratch_shapes=[pltpu.VMEM((tm, D), f32)], compiler_params=_params(2),
    )(*args)


def tn_mm(a, b, name, tn=512, slot_major=False):
    S, K1 = a.shape
    N = b.shape[1]
    tm = _row_tile(S, 512)
    tn = _row_tile(N, tn)
    ni = S // tm

    def body(a_ref, b_ref, o_ref, acc):
        i = pl.program_id(1)

        @pl.when(i == 0)
        def _():
            acc[...] = jnp.zeros_like(acc)

        acc[...] += _mm_tn(a_ref[...], b_ref[...])

        @pl.when(i == ni - 1)
        def _():
            o_ref[...] = acc[...].astype(o_ref.dtype)

    if slot_major:
        out_spec, out_shape = pl.BlockSpec((None, K1, tn), lambda j, i: (j, 0, 0)), SDS((N // tn, K1, tn), MXU_DTYPE)
    else:
        out_spec, out_shape = pl.BlockSpec((K1, tn), lambda j, i: (0, j)), SDS((K1, N), MXU_DTYPE)
    return pl.pallas_call(
        body, name=name, grid=(N // tn, ni),
        in_specs=[pl.BlockSpec((tm, K1), lambda j, i: (i, 0)), pl.BlockSpec((tm, tn), lambda j, i: (i, j))],
        out_specs=out_spec, out_shape=out_shape,
        scratch_shapes=[pltpu.VMEM((K1, tn), f32)], compiler_params=_params(2),
    )(a, b)


CONV_ROWS = 512


def _shift_down(cur, prev8, s):
    r = pltpu.roll(cur, s, 0)
    row = lax.broadcasted_iota(jnp.int32, (8, cur.shape[1]), 0)
    top = jnp.where(row < s, pltpu.roll(prev8, s, 0), r[0:8])
    return jnp.concatenate([top, r[8:]], axis=0)


def _shift_up(cur, next8, s):
    n = cur.shape[0]
    r = pltpu.roll(cur, n - s, 0)
    row = lax.broadcasted_iota(jnp.int32, (8, cur.shape[1]), 0)
    bot = jnp.where(row >= 8 - s, pltpu.roll(next8, 8 - s, 0), r[n - 8:])
    return jnp.concatenate([r[:n - 8], bot], axis=0)


def _conv_taps(cur, prev8):
    return [_shift_down(cur, prev8, 3), _shift_down(cur, prev8, 2), _shift_down(cur, prev8, 1), cur]


def _act_qk(c):
    a = _silu(c)
    return a * lax.rsqrt(jnp.sum(a * a, axis=-1, keepdims=True) + L2_EPS)


def dn_prep(proj, conv_w, name):
    S = proj.shape[0]
    W = conv_w.shape[1] // 3
    nh = W // HEAD
    R = _row_tile(S, CONV_ROWS)

    def body(p_ref, w_ref, o_ref):
        j = pl.program_id(0)
        w = w_ref[...]

        def rows(r, prev8):
            cur = p_ref[pl.ds(r, R), :]
            taps = _conv_taps(cur, prev8)
            cv = taps[0] * w[0:1] + taps[1] * w[1:2] + taps[2] * w[2:3] + taps[3] * w[3:4]

            @pl.when(j < 2 * nh)
            def _():
                o_ref[pl.ds(r, R), :] = _act_qk(cv)

            @pl.when(j >= 2 * nh)
            def _():
                o_ref[pl.ds(r, R), :] = _silu(cv)

        rows(0, jnp.zeros((8, HEAD), f32))

        @pl.loop(1, S // R)
        def _(t):
            r = pl.multiple_of(t * R, R)
            rows(r, p_ref[pl.ds(r - 8, 8), :])

    return pl.pallas_call(
        body, name=name, grid=(3 * nh,),
        in_specs=[pl.BlockSpec((S, HEAD), lambda j: (0, j)), pl.BlockSpec((CONV_K, HEAD), lambda j: (0, j))],
        out_specs=pl.BlockSpec((None, S, HEAD), lambda j: (j // nh, 0, j % nh)),
        out_shape=SDS((3, S, W), f32), compiler_params=_params(1),
    )(proj, conv_w)


def dn_prep_bwd(proj, conv_w, dqkv, dz, name, comm=None):
    S = proj.shape[0]
    W = conv_w.shape[1] // 3
    nh = W // HEAD
    nq = 3 * nh
    R = _row_tile(S, CONV_ROWS)
    nr = S // R

    def body(p_ref, w_ref, dq_ref, dz_ref, dp_ref, dw_ref, dc_ref):
        j = pl.program_id(0)

        @pl.when(j >= nq)
        def _():
            dp_ref[...] = dz_ref[...].astype(dp_ref.dtype)

        @pl.when(j < nq)
        def _():
            w = w_ref[...]
            dw_ref[...] = jnp.zeros_like(dw_ref)

            def rows(r, prev8):
                cur = p_ref[pl.ds(r, R), :]
                taps = _conv_taps(cur, prev8)
                cv = taps[0] * w[0:1] + taps[1] * w[1:2] + taps[2] * w[2:3] + taps[3] * w[3:4]
                dn = dq_ref[pl.ds(r, R), :]

                @pl.when(j < 2 * nh)
                def _():
                    dc_ref[pl.ds(r, R), :] = jax.vjp(_act_qk, cv)[1](dn)[0]

                @pl.when(j >= 2 * nh)
                def _():
                    dc_ref[pl.ds(r, R), :] = jax.vjp(_silu, cv)[1](dn)[0]

                dc = dc_ref[pl.ds(r, R), :]
                dw_ref[...] += jnp.concatenate(
                    [jnp.sum(dc * taps[q], axis=0, keepdims=True) for q in range(CONV_K)], axis=0)

            rows(0, jnp.zeros((8, HEAD), f32))

            @pl.loop(1, nr)
            def _(t):
                r = pl.multiple_of(t * R, R)
                rows(r, p_ref[pl.ds(r - 8, 8), :])

            def back(r, next8):
                dc = dc_ref[pl.ds(r, R), :]
                dx = dc * w[3:4]
                for s in (1, 2, 3):
                    dx = dx + _shift_up(dc, next8, s) * w[3 - s:4 - s]
                dp_ref[pl.ds(r, R), :] = dx.astype(dp_ref.dtype)

            @pl.loop(0, nr - 1)
            def _(t):
                r = pl.multiple_of(t * R, R)
                back(r, dc_ref[pl.ds(r + R, 8), :])

            back((nr - 1) * R, jnp.zeros((8, HEAD), f32))

    clamp = lambda j: jnp.minimum(j, nq - 1)
    return hosted_call(
        body, comm, _grid_steps(4 * nh), name=name, grid=(4 * nh,),
        in_specs=[pl.BlockSpec((S, HEAD), lambda j: (0, clamp(j))),
                  pl.BlockSpec((CONV_K, HEAD), lambda j: (0, clamp(j))),
                  pl.BlockSpec((None, S, HEAD), lambda j: (clamp(j) // nh, 0, clamp(j) % nh)),
                  pl.BlockSpec((S, HEAD), lambda j: (0, jnp.maximum(j - nq, 0)))],
        out_specs=(pl.BlockSpec((S, HEAD), lambda j: (0, j)), pl.BlockSpec((CONV_K, HEAD), lambda j: (0, clamp(j)))),
        out_shape=(SDS((S, 4 * W), MXU_DTYPE), SDS((CONV_K, 3 * W), f32)),
        scratch_shapes=[pltpu.VMEM((S, HEAD), f32)], args=(proj, conv_w, dqkv, dz))


def _gate_fns(braw, araw, alog, dtb):
    beta = jax.nn.sigmoid(braw)
    g = -jnp.exp(alog) * jax.nn.softplus(araw + dtb)
    return beta, g


def _lane_pick(x, lane):
    sel = lax.broadcasted_iota(jnp.int32, x.shape, 1) == lane
    return jnp.broadcast_to(jnp.sum(jnp.where(sel, x, 0.0), axis=1, keepdims=True), x.shape)


CUM_ROWS = 256


def _sel_mm(m01, x):
    m = _c(m01)
    d = lambda p: lax.dot_general(m, p, (NN, ((), ())), preferred_element_type=f32)
    h1, h2, h3 = _pieces3(x)
    return (d(h1) + d(h2)) + d(h3)


def _chunk_cumsum_matrix(n, transpose):
    r, c = lax.broadcasted_iota(jnp.int32, (n, n), 0), lax.broadcasted_iota(jnp.int32, (n, n), 1)
    sh = int(math.log2(DN_CHUNK))
    same = (r >> sh) == (c >> sh)
    return jnp.where(same & ((r <= c) if transpose else (r >= c)), 1.0, 0.0).astype(f32)


def dn_gates(pba, alog_b, dtb_b, name):
    S = pba.shape[0]
    H = alog_b.shape[0]
    R = _row_tile(S, CUM_ROWS)

    def body(p_ref, al_ref, dt_ref, beta_ref, g_ref):
        h = pl.program_id(0)
        p = p_ref[...]
        beta, g = _gate_fns(_lane_pick(p, h), _lane_pick(p, H + h), al_ref[...], dt_ref[...])
        beta_ref[...] = beta
        g_ref[...] = g
        cum = _chunk_cumsum_matrix(R, False)

        @pl.loop(0, S // R)
        def _(t):
            r = pl.multiple_of(t * R, R)
            g_ref[pl.ds(r, R), :] = _sel_mm(cum, g_ref[pl.ds(r, R), :])

    par = pl.BlockSpec((None, 1, HEAD), lambda h: (h, 0, 0))
    out = pl.BlockSpec((None, S, HEAD), lambda h: (h, 0, 0))
    return pl.pallas_call(
        body, name=name, grid=(H,), in_specs=[pl.BlockSpec((S, HEAD), lambda h: (0, 0)), par, par],
        out_specs=(out, out), out_shape=(SDS((H, S, HEAD), f32), SDS((H, S, HEAD), f32)), compiler_params=_params(1),
    )(pba, alog_b, dtb_b)


def dn_gates_bwd(pba, alog_b, dtb_b, dbeta, dg, name):
    S = pba.shape[0]
    H = alog_b.shape[0]
    R = _row_tile(S, CUM_ROWS)

    def body(p_ref, al_ref, dt_ref, dbeta_ref, dg_ref, dp_ref, dal_ref, ddt_ref, acc, dgs):
        h = pl.program_id(0)

        @pl.when(h == 0)
        def _():
            acc[...] = jnp.zeros_like(acc)

        cum_t = _chunk_cumsum_matrix(R, True)

        @pl.loop(0, S // R)
        def _(t):
            r = pl.multiple_of(t * R, R)
            dgs[pl.ds(r, R), :] = _sel_mm(cum_t, dg_ref[pl.ds(r, R), :])

        p = p_ref[...]
        db = jnp.broadcast_to(jnp.sum(dbeta_ref[...], axis=1, keepdims=True), p.shape)
        dgg = jnp.broadcast_to(jnp.sum(dgs[...], axis=1, keepdims=True), p.shape)
        _, vjp = jax.vjp(_gate_fns, _lane_pick(p, h), _lane_pick(p, H + h), al_ref[...], dt_ref[...])
        dbraw, daraw, dal, ddt = vjp((db, dgg))
        lane = lax.broadcasted_iota(jnp.int32, p.shape, 1)
        acc[...] += jnp.where(lane == h, dbraw, 0.0) + jnp.where(lane == H + h, daraw, 0.0)
        dal_ref[...] = dal
        ddt_ref[...] = ddt

        @pl.when(h == H - 1)
        def _():
            dp_ref[...] = acc[...].astype(dp_ref.dtype)

    par = pl.BlockSpec((None, 1, HEAD), lambda h: (h, 0, 0))
    big = pl.BlockSpec((None, S, HEAD), lambda h: (h, 0, 0))
    full = pl.BlockSpec((S, HEAD), lambda h: (0, 0))
    return pl.pallas_call(
        body, name=name, grid=(H,), in_specs=[full, par, par, big, big],
        out_specs=(full, par, par),
        out_shape=(SDS((S, HEAD), MXU_DTYPE), SDS((H, 1, HEAD), f32), SDS((H, 1, HEAD), f32)),
        scratch_shapes=[pltpu.VMEM((S, HEAD), f32), pltpu.VMEM((S, HEAD), f32)], compiler_params=_params(1),
    )(pba, alog_b, dtb_b, dbeta, dg)


def _bdot(dims):
    back = {NN: ((NT, 'gb'), (TN, 'ag')), NT: ((NN, 'gb'), (TN, 'ga')), TN: ((NT, 'bg'), (NN, 'ag'))}[dims]
    d = lambda p, q, dm: lax.dot_general(_c(p), _c(q), (dm, ((), ())), preferred_element_type=f32)

    @jax.custom_vjp
    def f(a, b):
        return d(a, b, dims)

    def fwd(a, b):
        return d(a, b, dims), (a, b)

    def bwd(res, g):
        v = {'a': res[0], 'b': res[1], 'g': g}
        (da_dims, da_ops), (db_dims, db_ops) = back
        return d(v[da_ops[0]], v[da_ops[1]], da_dims), d(v[db_ops[0]], v[db_ops[1]], db_dims)

    f.defvjp(fwd, bwd)
    return f, lambda a, b: d(a, b, dims)


_BDOT = {dims: _bdot(dims) for dims in (NN, NT, TN)}


def _tri_inv_multi(Ls):
    n = Ls[0].shape[0]
    eye = jnp.where(lax.broadcasted_iota(jnp.int32, (n, n), 0) == lax.broadcasted_iota(jnp.int32, (n, n), 1), 1.0, 0.0)
    P = tuple(-L for L in Ls)
    T = tuple(eye + p for p in P)
    for _ in range(int(math.log2(n)) - 1):
        P = tuple(_dot3(p, p, NN) for p in P)
        T = tuple(t + _dot3(t, p, NN) for t, p in zip(T, P))
    return T


@jax.custom_vjp
def _tri_inv_multi_vjp(Ls):
    return _tri_inv_multi(Ls)


def _tri_inv_fwd(Ls):
    T = _tri_inv_multi(Ls)
    return T, T


def _tri_inv_bwd(T, dT):
    X = tuple(_dot3(d, t, NT) for d, t in zip(dT, T))
    return (tuple(-_dot3(t, x, TN) for t, x in zip(T, X)),)


_tri_inv_multi_vjp.defvjp(_tri_inv_fwd, _tri_inv_bwd)


def _pieces3(x):
    h1 = x.astype(MXU_DTYPE)
    r1 = x - h1.astype(f32)
    h2 = r1.astype(MXU_DTYPE)
    return h1, h2, (r1 - h2.astype(f32)).astype(MXU_DTYPE)


def _row_bcast_impl(sel_row, gc):
    s = _c(sel_row)
    d = lambda p: lax.dot_general(s, p, (NT, ((), ())), preferred_element_type=f32)
    h1, h2, h3 = _pieces3(gc)
    return (d(h1) + d(h2)) + d(h3)


def _row_bcast_bwd(sel_row, d):
    s = _c(sel_row)
    hi, lo = _split(d)
    t = lambda p: lax.dot_general(p, s, (TN, ((), ())), preferred_element_type=f32)
    return jnp.zeros_like(sel_row), t(hi) + t(lo)


_row_bcast = jax.custom_vjp(_row_bcast_impl)
_row_bcast.defvjp(lambda sel_row, gc: (_row_bcast_impl(sel_row, gc), sel_row), _row_bcast_bwd)


def _col_bcast_impl(gc):
    return gc[:, :DN_CHUNK]


def _col_bcast_bwd(_, d):
    return (jnp.broadcast_to(jnp.sum(d, axis=1, keepdims=True) * (1.0 / HEAD), (d.shape[0], HEAD)),)


_col_bcast = jax.custom_vjp(_col_bcast_impl)
_col_bcast.defvjp(lambda gc: (_col_bcast_impl(gc), None), _col_bcast_bwd)


def _last_row_bcast(n):
    def impl(gc):
        return jnp.broadcast_to(gc[DN_CHUNK - 1:DN_CHUNK, :], (n, HEAD))

    def bwd(_, d):
        row = lax.broadcasted_iota(jnp.int32, (DN_CHUNK, HEAD), 0)
        return (jnp.where(row == DN_CHUNK - 1, jnp.sum(d, axis=0, keepdims=True), 0.0),)

    f = jax.custom_vjp(impl)
    f.defvjp(lambda gc: (impl(gc), None), bwd)
    return impl, f


_LAST_C, _LAST_H = _last_row_bcast(DN_CHUNK), _last_row_bcast(HEAD)


def _chunk_consts():
    C = DN_CHUNK
    io = lambda shape, ax: lax.broadcasted_iota(jnp.int32, shape, ax)
    one = lambda m: jnp.where(m, 1.0, 0.0).astype(f32)
    r, c = io((C, C), 0), io((C, C), 1)
    return dict(causal=r >= c, strict=r > c, sel_row=one(io((C, HEAD), 1) == 0))


def _chunk_fn(kc, diff, q, k, v, gc, bB, S0):
    i = 0 if diff else 1
    mm, mm_nt, mm_tn = _BDOT[NN][i], _BDOT[NT][i], _BDOT[TN][i]
    tri = _tri_inv_multi_vjp if diff else _tri_inv_multi
    each = lambda f, *ls: tuple(f(*a) for a in zip(*ls))
    gcol = each(_col_bcast if diff else _col_bcast_impl, gc)
    grow = each(lambda g: (_row_bcast if diff else _row_bcast_impl)(kc['sel_row'], g), gc)
    glc = each(_LAST_C[i ^ 1], gc)
    glh = each(_LAST_H[i ^ 1], gc)
    decay = each(lambda a, b: jnp.where(kc['causal'], jnp.exp(jnp.where(kc['causal'], a - b, 0.0)), 0.0), gcol, grow)
    kb = each(lambda a, b: a * b, k, bB)
    vb = each(lambda a, b: a * b, v, bB)
    egc = each(jnp.exp, gc)
    kk = each(mm_nt, kb, k)
    T = tri(each(lambda a, d: jnp.where(kc['strict'], a * d, 0.0), kk, decay))
    u = each(mm, T, vb)
    w = each(mm, T, each(lambda a, b: a * b, kb, egc))
    qs = each(lambda a: a * (HEAD ** -0.5), q)
    qk = each(mm_nt, qs, k)
    attn = each(lambda a, d: jnp.where(kc['causal'], a * d, 0.0), qk, decay)
    wS = each(mm, w, S0)
    qS = each(mm, each(lambda a, b: a * b, qs, egc), S0)
    v_new = each(lambda a, b: a - b, u, wS)
    o = each(lambda a, b: a + b, qS, each(mm, attn, v_new))
    kdec = each(lambda a, gl, g: a * jnp.exp(gl - g), k, glc, gc)
    S1 = each(lambda s, gl, kv: s * jnp.exp(gl) + kv, S0, glh, each(mm_tn, kdec, v_new))
    return o, S1


def _heads_per_block(H):
    return 8 if H % 8 == 0 else (4 if H % 4 == 0 else 1)


def dn_chunk_fwd(qkv, gB, bB, name, comm=None):
    _, S, W = qkv.shape
    H, C = W // HEAD, DN_CHUNK
    N, HB = S // C, _heads_per_block(H)

    def body(q_ref, k_ref, v_ref, g_ref, b_ref, o_ref, st_ref, s_scr):
        @pl.when(pl.program_id(1) == 0)
        def _():
            s_scr[...] = jnp.zeros_like(s_scr)

        kc = _chunk_consts()
        sls = [slice(hh * HEAD, (hh + 1) * HEAD) for hh in range(HB)]
        heads = lambda ref: tuple(ref[:, sl] for sl in sls)
        S0 = tuple(s_scr[hh] for hh in range(HB))
        for hh in range(HB):
            st_ref[hh] = S0[hh]
        o, S1 = _chunk_fn(kc, False, heads(q_ref), heads(k_ref), heads(v_ref), tuple(g_ref[hh] for hh in range(HB)),
                          tuple(b_ref[hh] for hh in range(HB)), S0)
        for hh in range(HB):
            o_ref[:, sls[hh]] = o[hh]
            s_scr[hh] = S1[hh]

    part = lambda p: pl.BlockSpec((None, C, HB * HEAD), lambda hb, n: (p, n, hb))
    gate = pl.BlockSpec((HB, C, HEAD), lambda hb, n: (hb, n, 0))
    return hosted_call(
        body, comm, _grid_steps(H // HB, N), name=name, grid=(H // HB, N),
        in_specs=[part(0), part(1), part(2), gate, gate],
        out_specs=(pl.BlockSpec((C, HB * HEAD), lambda hb, n: (n, hb)),
                   pl.BlockSpec((None, HB, HEAD, HEAD), lambda hb, n: (n, hb, 0, 0))),
        out_shape=(SDS((S, W), f32), SDS((N, H, HEAD, HEAD), f32)),
        scratch_shapes=[pltpu.VMEM((HB, HEAD, HEAD), f32)], args=(qkv, qkv, qkv, gB, bB))


def dn_chunk_bwd(qkv, gB, bB, states, do, name, comm=None):
    _, S, W = qkv.shape
    H, C = W // HEAD, DN_CHUNK
    N, HB = S // C, _heads_per_block(H)

    def body(q_ref, k_ref, v_ref, g_ref, b_ref, st_ref, do_ref, dqkv_ref, dg_ref, db_ref, ds_scr):
        @pl.when(pl.program_id(1) == 0)
        def _():
            ds_scr[...] = jnp.zeros_like(ds_scr)

        kc = _chunk_consts()
        sls = [slice(hh * HEAD, (hh + 1) * HEAD) for hh in range(HB)]
        heads = lambda ref: tuple(ref[:, sl] for sl in sls)
        lead = lambda ref: tuple(ref[hh] for hh in range(HB))
        _, vjp = jax.vjp(functools.partial(_chunk_fn, kc, True), heads(q_ref), heads(k_ref), heads(v_ref),
                         lead(g_ref), lead(b_ref), lead(st_ref))
        dq, dk, dv, dg, db, dS0 = vjp((heads(do_ref), lead(ds_scr)))
        for hh in range(HB):
            dqkv_ref[0, :, sls[hh]] = dq[hh]
            dqkv_ref[1, :, sls[hh]] = dk[hh]
            dqkv_ref[2, :, sls[hh]] = dv[hh]
            dg_ref[hh] = dg[hh]
            db_ref[hh] = db[hh]
            ds_scr[hh] = dS0[hh]

    rev = lambda n: N - 1 - n
    part = lambda p: pl.BlockSpec((None, C, HB * HEAD), lambda hb, n: (p, rev(n), hb))
    gate = pl.BlockSpec((HB, C, HEAD), lambda hb, n: (hb, rev(n), 0))
    return hosted_call(
        body, comm, _grid_steps(H // HB, N), name=name, grid=(H // HB, N),
        in_specs=[part(0), part(1), part(2), gate, gate,
                  pl.BlockSpec((None, HB, HEAD, HEAD), lambda hb, n: (rev(n), hb, 0, 0)),
                  pl.BlockSpec((C, HB * HEAD), lambda hb, n: (rev(n), hb))],
        out_specs=(pl.BlockSpec((3, C, HB * HEAD), lambda hb, n: (0, rev(n), hb)), gate, gate),
        out_shape=(SDS((3, S, W), f32), SDS((H, S, HEAD), f32), SDS((H, S, HEAD), f32)),
        scratch_shapes=[pltpu.VMEM((HB, HEAD, HEAD), f32)], args=(qkv, qkv, qkv, gB, bB, states, do))


def _gate_norm(o, z, ng):
    return _rms(o, ng) * _silu(z)


def dn_out(o, proj, ng, wout, x1, g3, name):
    S, W = o.shape
    D = x1.shape[1]
    nh = W // HEAD
    tm = _row_tile(S, 256)

    def body(o_ref, z_ref, ng_ref, w_ref, x_ref, g_ref, xo_ref, m_ref, og_ref):
        for h in range(nh):
            sl = slice(h * HEAD, (h + 1) * HEAD)
            og_ref[:, sl] = _gate_norm(o_ref[:, sl], z_ref[:, sl], ng_ref[...]).astype(og_ref.dtype)
        m = _mm(og_ref[...], w_ref[...])
        m_ref[...] = m
        xo_ref[...] = x_ref[...] + _rms(m, g_ref[...])

    rw = pl.BlockSpec((tm, W), lambda i: (i, 0))
    rd = pl.BlockSpec((tm, D), lambda i: (i, 0))
    return pl.pallas_call(
        body, name=name, grid=(S // tm,),
        in_specs=[rw, pl.BlockSpec((tm, W), lambda i: (i, 3)), pl.BlockSpec((1, HEAD), lambda i: (0, 0)),
                  pl.BlockSpec((W, D), lambda i: (0, 0)), rd, pl.BlockSpec((1, D), lambda i: (0, 0))],
        out_specs=(rd, rd, rw),
        out_shape=(SDS((S, D), f32), SDS((S, D), f32), SDS((S, W), MXU_DTYPE)), compiler_params=_params(1),
    )(o, proj, ng, wout, x1, g3)


def dn_out_bwd(dxo, m, g3, o, proj, ng, wout, name):
    S, W = o.shape
    D = m.shape[1]
    nh = W // HEAD
    tm = _row_tile(S, 256)

    def body(dxo_ref, m_ref, g_ref, o_ref, z_ref, ng_ref, w_ref, dm_ref, do_ref, dz_ref, dng_ref, dg_ref):
        @pl.when(pl.program_id(0) == 0)
        def _():
            dng_ref[...] = jnp.zeros_like(dng_ref)
            dg_ref[...] = jnp.zeros_like(dg_ref)

        dm, dg = _rms_bwd(m_ref[...], g_ref[...], dxo_ref[...])
        dg_ref[...] += dg
        dmc = dm.astype(dm_ref.dtype)
        dm_ref[...] = dmc
        dog = _mm_nt(dmc, w_ref[...])
        for h in range(nh):
            sl = slice(h * HEAD, (h + 1) * HEAD)
            _, vjp = jax.vjp(_gate_norm, o_ref[:, sl], z_ref[:, sl], ng_ref[...])
            do, dz, dng = vjp(dog[:, sl])
            do_ref[:, sl] = do
            dz_ref[:, sl] = dz.astype(dz_ref.dtype)
            dng_ref[...] += dng

    rw = pl.BlockSpec((tm, W), lambda i: (i, 0))
    rd = pl.BlockSpec((tm, D), lambda i: (i, 0))
    vd = pl.BlockSpec((1, D), lambda i: (0, 0))
    vh = pl.BlockSpec((1, HEAD), lambda i: (0, 0))
    return pl.pallas_call(
        body, name=name, grid=(S // tm,),
        in_specs=[rd, rd, vd, rw, pl.BlockSpec((tm, W), lambda i: (i, 3)), vh, pl.BlockSpec((W, D), lambda i: (0, 0))],
        out_specs=(rd, rw, rw, vh, vd),
        out_shape=(SDS((S, D), MXU_DTYPE), SDS((S, W), f32), SDS((S, W), MXU_DTYPE), SDS((1, HEAD), f32),
                   SDS((1, D), f32)),
        compiler_params=_params(1),
    )(dxo, m, g3, o, proj, ng, wout)


def _sg_stage1(pu, pv, bu, bv, lg, lb):
    u = _gelu(pu + bu)
    t = _gelu(pv + bv)
    tc = t - jnp.mean(t, axis=-1, keepdims=True)
    v = tc * lax.rsqrt(jnp.mean(tc * tc, axis=-1, keepdims=True) + LN_EPS) * lg + lb
    return u, v


def _causal_mask(n):
    return lax.broadcasted_iota(jnp.int32, (n, n), 0) >= lax.broadcasted_iota(jnp.int32, (n, n), 1)


def sg_mid(pre, b_in, ln_g, ln_b, w_s, bsT, wout, x1, g3, name):
    S = pre.shape[0]
    E, D = ln_g.shape[1], x1.shape[1]
    G, CH = SG_GROUPS, SG_CHUNK
    Cg = E // G
    tm = _row_tile(S, 256)

    def body(pu_ref, pv_ref, bu_ref, bv_ref, lg_ref, lb_ref, ws_ref, bs_ref, w_ref, x_ref, g_ref,
             xo_ref, m_ref, gt_ref):
        u, v = _sg_stage1(pu_ref[...], pv_ref[...], bu_ref[...], bv_ref[...], lg_ref[...], lb_ref[...])
        mask = _causal_mask(CH)
        for g in range(G):
            wc = _c(jnp.where(mask, ws_ref[g], 0.0))
            bcol = bs_ref[:, g:g + 1]
            cs = slice(g * Cg, (g + 1) * Cg)
            for ch in range(tm // CH):
                rs = slice(ch * CH, (ch + 1) * CH)
                mixed = _mm(wc, _c(v[rs, cs])) + bcol
                gt_ref[rs, cs] = (u[rs, cs] * mixed).astype(gt_ref.dtype)
        m = _mm(gt_ref[...], w_ref[...])
        m_ref[...] = m
        xo_ref[...] = x_ref[...] + _rms(m, g_ref[...])

    half = lambda p: pl.BlockSpec((tm, E), lambda i: (i, p))
    vhalf = lambda p: pl.BlockSpec((1, E), lambda i: (0, p))
    ve = pl.BlockSpec((1, E), lambda i: (0, 0))
    rd = pl.BlockSpec((tm, D), lambda i: (i, 0))
    return pl.pallas_call(
        body, name=name, grid=(S // tm,),
        in_specs=[half(0), half(1), vhalf(0), vhalf(1), ve, ve, pl.BlockSpec((G, CH, CH), lambda i: (0, 0, 0)),
                  pl.BlockSpec((CH, G), lambda i: (0, 0)), pl.BlockSpec((E, D), lambda i: (0, 0)), rd,
                  pl.BlockSpec((1, D), lambda i: (0, 0))],
        out_specs=(rd, rd, pl.BlockSpec((tm, E), lambda i: (i, 0))),
        out_shape=(SDS((S, D), f32), SDS((S, D), f32), SDS((S, E), MXU_DTYPE)), compiler_params=_params(1),
    )(pre, pre, b_in, b_in, ln_g, ln_b, w_s, bsT, wout, x1, g3)


def sg_mid_bwd(dxo, m, g3, pre, b_in, ln_g, ln_b, w_s, bsT, wout, name):
    S = pre.shape[0]
    E, D = ln_g.shape[1], m.shape[1]
    G, CH = SG_GROUPS, SG_CHUNK
    Cg = E // G
    tm = _row_tile(S, 256)

    def body(dxo_ref, m_ref, g_ref, pu_ref, pv_ref, bu_ref, bv_ref, lg_ref, lb_ref, ws_ref, bs_ref, w_ref,
             dm_ref, dpre_ref, dbin_ref, dlg_ref, dlb_ref, dws_ref, dbs_ref, dg_ref, du_scr, dv_scr):
        @pl.when(pl.program_id(0) == 0)
        def _():
            for r in (dbin_ref, dlg_ref, dlb_ref, dws_ref, dbs_ref, dg_ref):
                r[...] = jnp.zeros_like(r)

        dm, dg = _rms_bwd(m_ref[...], g_ref[...], dxo_ref[...])
        dg_ref[...] += dg
        dmc = dm.astype(dm_ref.dtype)
        dm_ref[...] = dmc
        dgated = _mm_nt(dmc, w_ref[...])
        (u, v), vjp1 = jax.vjp(_sg_stage1, pu_ref[...], pv_ref[...], bu_ref[...], bv_ref[...], lg_ref[...],
                               lb_ref[...])
        mask = _causal_mask(CH)
        lane = lax.broadcasted_iota(jnp.int32, (CH, CH), 1)
        for g in range(G):
            wc = _c(jnp.where(mask, ws_ref[g], 0.0))
            bcol = bs_ref[:, g:g + 1]
            cs = slice(g * Cg, (g + 1) * Cg)
            dws = jnp.zeros((CH, CH), f32)
            dbs = jnp.zeros((CH, 1), f32)
            for ch in range(tm // CH):
                rs = slice(ch * CH, (ch + 1) * CH)
                vs = _c(v[rs, cs])
                mixed = _mm(wc, vs) + bcol
                dgt = dgated[rs, cs]
                du_scr[rs, cs] = dgt * mixed
                dmixed = dgt * u[rs, cs]
                dmc2 = _c(dmixed)
                dv_scr[rs, cs] = _mm_tn(wc, dmc2)
                dws = dws + _mm_nt(dmc2, vs)
                dbs = dbs + jnp.sum(dmixed, axis=1, keepdims=True)
            dws_ref[g] += jnp.where(mask, dws, 0.0)
            dbs_ref[...] += jnp.where(lane == g, jnp.broadcast_to(dbs, (CH, CH)), 0.0)
        dpu, dpv, dbu, dbv, dlg, dlb = vjp1((du_scr[...], dv_scr[...]))
        dpre_ref[:, :E] = dpu.astype(dpre_ref.dtype)
        dpre_ref[:, E:] = dpv.astype(dpre_ref.dtype)
        dbin_ref[:, :E] += dbu
        dbin_ref[:, E:] += dbv
        dlg_ref[...] += dlg
        dlb_ref[...] += dlb

    half = lambda p: pl.BlockSpec((tm, E), lambda i: (i, p))
    vhalf = lambda p: pl.BlockSpec((1, E), lambda i: (0, p))
    ve = pl.BlockSpec((1, E), lambda i: (0, 0))
    rd = pl.BlockSpec((tm, D), lambda i: (i, 0))
    vd = pl.BlockSpec((1, D), lambda i: (0, 0))
    wsb = pl.BlockSpec((G, CH, CH), lambda i: (0, 0, 0))
    return pl.pallas_call(
        body, name=name, grid=(S // tm,),
        in_specs=[rd, rd, vd, half(0), half(1), vhalf(0), vhalf(1), ve, ve, wsb,
                  pl.BlockSpec((CH, G), lambda i: (0, 0)), pl.BlockSpec((E, D), lambda i: (0, 0))],
        out_specs=(rd, pl.BlockSpec((tm, 2 * E), lambda i: (i, 0)), pl.BlockSpec((1, 2 * E), lambda i: (0, 0)), ve, ve,
                   wsb, pl.BlockSpec((CH, CH), lambda i: (0, 0)), vd),
        out_shape=(SDS((S, D), MXU_DTYPE), SDS((S, 2 * E), MXU_DTYPE), SDS((1, 2 * E), f32), SDS((1, E), f32),
                   SDS((1, E), f32), SDS((G, CH, CH), f32), SDS((CH, CH), f32), SDS((1, D), f32)),
        scratch_shapes=[pltpu.VMEM((tm, E), f32), pltpu.VMEM((tm, E), f32)], compiler_params=_params(1),
    )(dxo, m, g3, pre, pre, b_in, b_in, ln_g, ln_b, w_s, bsT, wout)


def loss_head(y, target, name):
    S, D = y.shape
    tm = _row_tile(S, 512)

    def body(y_ref, t_ref, l_ref, d_ref):
        @pl.when(pl.program_id(0) == 0)
        def _():
            l_ref[...] = jnp.zeros_like(l_ref)

        e = y_ref[...] - t_ref[...]
        d_ref[...] = e * (1.0 / D)
        l_ref[...] += jnp.sum(e * e) * (0.5 / D)

    row = pl.BlockSpec((tm, D), lambda i: (i, 0))
    return pl.pallas_call(
        body, name=name, grid=(S // tm,), in_specs=[row, row],
        out_specs=(pl.BlockSpec((1, HEAD), lambda i: (0, 0)), row),
        out_shape=(SDS((1, HEAD), f32), SDS((S, D), f32)), compiler_params=_params(1),
    )(y, target)


def sum_slots(r, name):
    _, R, C = r.shape
    tr = _row_tile(R, 648 if R % 648 == 0 else R)

    def body(r_ref, o_ref):
        acc = r_ref[0].astype(f32)
        for s in range(1, N_DEV):
            acc = acc + r_ref[s].astype(f32)
        o_ref[...] = acc

    return pl.pallas_call(
        body, name=name, grid=(R // tr,), in_specs=[pl.BlockSpec((N_DEV, tr, C), lambda i: (0, i, 0))],
        out_specs=pl.BlockSpec((tr, C), lambda i: (i, 0)), out_shape=SDS((R, C), f32), compiler_params=_params(1),
    )(r)


def _adam_math(w, g, m, v):
    m = ADAM_B1 * m + (1.0 - ADAM_B1) * g
    v = ADAM_B2 * v + (1.0 - ADAM_B2) * (g * g)
    m_hat = m / (1.0 - ADAM_B1 ** ADAM_STEP)
    v_hat = v / (1.0 - ADAM_B2 ** ADAM_STEP)
    delta = -ADAM_LR * (m_hat / (jnp.sqrt(v_hat) + ADAM_EPS) + ADAM_WD * w)
    return delta, m, v


def adam_slots(w, r, m, v, name, tr):
    R, C = w.shape
    tr = _row_tile(R, tr)

    def body(w_ref, r_ref, m_ref, v_ref, g_ref, d_ref, mo_ref, vo_ref):
        g = r_ref[0].astype(f32)
        for s in range(1, N_DEV):
            g = g + r_ref[s].astype(f32)
        g_ref[...] = g
        d_ref[...], mo_ref[...], vo_ref[...] = _adam_math(w_ref[...], g, m_ref[...], v_ref[...])

    row = pl.BlockSpec((tr, C), lambda i: (i, 0))
    return pl.pallas_call(
        body, name=name, grid=(R // tr,), in_specs=[row, pl.BlockSpec((N_DEV, tr, C), lambda i: (0, i, 0)), row, row],
        out_specs=(row, row, row, row), out_shape=tuple(SDS((R, C), f32) for _ in range(4)),
        compiler_params=_params(1),
    )(w, r, m, v)


def adam_small(w, g, m, v, name):
    def body(w_ref, g_ref, m_ref, v_ref, d_ref, mo_ref, vo_ref):
        d_ref[...], mo_ref[...], vo_ref[...] = _adam_math(w_ref[...], g_ref[...], m_ref[...], v_ref[...])

    return pl.pallas_call(body, name=name, out_shape=tuple(SDS(w.shape, f32) for _ in range(3)))(w, g, m, v)


def _pack_rows(parts):
    rows, offs, r = [], [], 0
    for p in parts:
        flat = p.reshape(-1)
        n = -(-flat.shape[0] // HEAD)
        flat = jnp.pad(flat, (0, n * HEAD - flat.shape[0]))
        rows.append(flat.reshape(n, HEAD))
        offs.append((r, n))
        r += n
    pad = (-r) % 8
    if pad:
        rows.append(jnp.zeros((pad, HEAD), f32))
    return jnp.concatenate(rows, axis=0), offs


def kernel(x, norm_g, ffn_w_gate, ffn_w_up, ffn_w_down, dn_w_in, dn_conv_w, dn_a_log, dn_dt_bias, dn_norm_g, dn_w_out, sg_w_in, sg_b_in, sg_ln_g, sg_ln_b, sg_w_s, sg_b_s, sg_w_out, loss_target, m_norm_g, m_ffn_w_gate, m_ffn_w_up, m_ffn_w_down, m_dn_w_in, m_dn_conv_w, m_dn_a_log, m_dn_dt_bias, m_dn_norm_g, m_dn_w_out, m_sg_w_in, m_sg_b_in, m_sg_ln_g, m_sg_ln_b, m_sg_w_s, m_sg_b_s, m_sg_w_out, v_norm_g, v_ffn_w_gate, v_ffn_w_up, v_ffn_w_down, v_dn_w_in, v_dn_conv_w, v_dn_a_log, v_dn_dt_bias, v_dn_norm_g, v_dn_w_out, v_sg_w_in, v_sg_b_in, v_sg_ln_g, v_sg_ln_b, v_sg_w_s, v_sg_b_s, v_sg_w_out):
    weights = dict(norm_g=norm_g, ffn_w_gate=ffn_w_gate, ffn_w_up=ffn_w_up, ffn_w_down=ffn_w_down, dn_w_in=dn_w_in,
                   dn_conv_w=dn_conv_w, dn_a_log=dn_a_log, dn_dt_bias=dn_dt_bias, dn_norm_g=dn_norm_g,
                   dn_w_out=dn_w_out, sg_w_in=sg_w_in, sg_b_in=sg_b_in, sg_ln_g=sg_ln_g, sg_ln_b=sg_ln_b,
                   sg_w_s=sg_w_s, sg_b_s=sg_b_s, sg_w_out=sg_w_out)
    mom_m = dict(norm_g=m_norm_g, ffn_w_gate=m_ffn_w_gate, ffn_w_up=m_ffn_w_up, ffn_w_down=m_ffn_w_down,
                 dn_w_in=m_dn_w_in, dn_conv_w=m_dn_conv_w, dn_a_log=m_dn_a_log, dn_dt_bias=m_dn_dt_bias,
                 dn_norm_g=m_dn_norm_g, dn_w_out=m_dn_w_out, sg_w_in=m_sg_w_in, sg_b_in=m_sg_b_in,
                 sg_ln_g=m_sg_ln_g, sg_ln_b=m_sg_ln_b, sg_w_s=m_sg_w_s, sg_b_s=m_sg_b_s, sg_w_out=m_sg_w_out)
    mom_v = dict(norm_g=v_norm_g, ffn_w_gate=v_ffn_w_gate, ffn_w_up=v_ffn_w_up, ffn_w_down=v_ffn_w_down,
                 dn_w_in=v_dn_w_in, dn_conv_w=v_dn_conv_w, dn_a_log=v_dn_a_log, dn_dt_bias=v_dn_dt_bias,
                 dn_norm_g=v_dn_norm_g, dn_w_out=v_dn_w_out, sg_w_in=v_sg_w_in, sg_b_in=v_sg_b_in,
                 sg_ln_g=v_sg_ln_g, sg_ln_b=v_sg_ln_b, sg_w_s=v_sg_w_s, sg_b_s=v_sg_b_s, sg_w_out=v_sg_w_out)
    order = list(weights)

    xs = x[0]
    S, D = xs.shape
    F8 = ffn_w_gate.shape[-1]
    depth = norm_g.shape[0]
    W = dn_w_out.shape[1] * N_DEV
    H = W // HEAD
    E = sg_ln_g.shape[1] * N_DEV
    G, CH = sg_w_s.shape[1], sg_w_s.shape[2]
    c8 = dn_w_in.shape[2]
    me = _slot(lax.axis_index("x"), lax.axis_index("y"), lax.axis_index("c"))

    assert depth == 2
    small_in, small_offs = _pack_rows([norm_g, dn_conv_w, sg_b_in, sg_ln_g, sg_ln_b])
    wg0a, wu0a, wd0a, small_all = all_gather_multi(
        [_c(ffn_w_gate[0, 0]), _c(ffn_w_up[0, 0]), _c(ffn_w_down[0, 0]), small_in], name="gather_first")
    gather_l0 = Comm("gather", [_c(dn_w_in[0]), _c(dn_w_out[0]), _c(ffn_w_gate[0, 1]), _c(ffn_w_up[0, 1]),
                                _c(ffn_w_down[0, 1])])
    gather_l1 = Comm("gather", [_c(ffn_w_gate[1]), _c(ffn_w_up[1]), _c(ffn_w_down[1]), _c(sg_w_in[0]),
                                _c(sg_w_out[0])])
    per = N_DEV // FFN_SLABS
    wide_cols = lambda w: jnp.transpose(w.reshape(FFN_SLABS, per, D, F8), (0, 2, 1, 3)).reshape(FFN_SLABS, D, per * F8)
    wide = lambda g, u, d: (wide_cols(g), wide_cols(u), d.reshape(FFN_SLABS, per * F8, D))
    shard_cols = lambda dw: jnp.transpose(dw.reshape(FFN_SLABS, D, per, F8), (0, 2, 1, 3)).reshape(N_DEV, D, F8)
    shard_rows = lambda dw: dw.reshape(N_DEV, F8, D)
    ffn_w = {(0, 0): wide(wg0a, wu0a, wd0a)}

    def small_piece(i, shard_shape):
        r0, n = small_offs[i]
        sz = math.prod(shard_shape)
        return small_all[:, r0:r0 + n, :].reshape(N_DEV, n * HEAD)[:, :sz].reshape((N_DEV,) + tuple(shard_shape))

    ng_full = jnp.moveaxis(small_piece(0, norm_g.shape), 0, 2).reshape(depth, 6, D)
    conv_full = jnp.moveaxis(small_piece(1, dn_conv_w.shape[1:]), 0, 1).reshape(CONV_K, 3 * W)
    bin_full = small_piece(2, sg_b_in.shape[1:]).reshape(1, 2 * E)
    lng_full = small_piece(3, sg_ln_g.shape[1:]).reshape(1, E)
    lnb_full = small_piece(4, sg_ln_b.shape[1:]).reshape(1, E)
    alog_b = jnp.broadcast_to(dn_a_log.reshape(H, 1, 1), (H, 1, HEAD))
    dtb_b = jnp.broadcast_to(dn_dt_bias.reshape(H, 1, 1), (H, 1, HEAD))
    bsT = sg_b_s[0].T
    gvec = lambda l, k: ng_full[l, k].reshape(1, D)

    saved = []
    cur = xs
    for l in range(depth):
        sv = {}
        sv['x0'] = cur
        (cur, sv['hA'], sv['aA'], sv['bA'], sv['yA']), got = ffn_fwd(
            cur, gvec(l, 0), gvec(l, 1), *ffn_w[l, 0], name=f"ffn_fwd_{l}a", comm=gather_l0 if l == 0 else None)
        sv['x1'] = cur
        if l == 0:
            dnin_all, dnout_all, wg0b, wu0b, wd0b = got
            ffn_w[0, 1] = wide(wg0b, wu0b, wd0b)
            dn_win = jnp.moveaxis(dnin_all, 0, 1).reshape(D, N_DEV * c8)
            dn_wmain = dn_win[:, :4 * W]
            dn_wba = jnp.pad(dn_win[:, 4 * W:], ((0, 0), (0, HEAD - 2 * H)))
            dn_wout = dnout_all.reshape(W, D)
            sv['hM'], sv['proj'], sv['pba'] = rms_mm(cur, gvec(l, 2), dn_wmain, dn_wba, name=f"dn_in_{l}")
            sv['qkv'] = dn_prep(sv['proj'], conv_full, name=f"dn_prep_{l}")
            sv['beta'], sv['g'] = dn_gates(sv['pba'], alog_b, dtb_b, name=f"dn_gates_{l}")
            (sv['o'], sv['states']), got = dn_chunk_fwd(sv['qkv'], sv['g'], sv['beta'], name=f"dn_chunk_{l}",
                                                        comm=gather_l1)
            wg1, wu1, wd1, sgin_all, sgout_all = got
            for ab in range(2):
                ffn_w[1, ab] = wide(wg1[:, ab], wu1[:, ab], wd1[:, ab])
            sg_win = jnp.moveaxis(sgin_all, 0, 1).reshape(D, 2 * E)
            sg_wout = sgout_all.reshape(E, D)
            cur, sv['m'], sv['og'] = dn_out(sv['o'], sv['proj'], dn_norm_g, dn_wout, cur, gvec(l, 3), name=f"dn_out_{l}")
        else:
            sv['hM'], sv['pre'] = rms_mm(cur, gvec(l, 2), sg_win, None, name=f"sg_in_{l}")
            cur, sv['m'], sv['gated'] = sg_mid(sv['pre'], bin_full, lng_full, lnb_full, sg_w_s[0], bsT, sg_wout, cur,
                                               gvec(l, 3), name=f"sg_mid_{l}")
        sv['x2'] = cur
        (cur, sv['hB'], sv['aB'], sv['bB'], sv['yB']), _ = ffn_fwd(cur, gvec(l, 4), gvec(l, 5), *ffn_w[l, 1],
                                                                   name=f"ffn_fwd_{l}b")
        saved.append(sv)

    loss_blk, dcur = loss_head(cur, loss_target[0], name="loss_head")
    loss = lax.psum(loss_blk[0, 0], ("x", "y", "c"))

    dng = [[None] * 6 for _ in range(depth)]
    ffn_dw = {}
    grads, slots = {}, {}

    def ffn_backward(l, ab, dcur, comm=None):
        sv, s = saved[l], 'AB'[ab]
        (dcur, da, db, dy, dng[l][4 * ab], dng[l][4 * ab + 1]), got = ffn_bwd_dx(
            dcur, sv['x2' if ab else 'x0'], sv['y' + s], sv['a' + s], sv['b' + s], gvec(l, 4 * ab), gvec(l, 4 * ab + 1),
            *ffn_w[l, ab], name=f"ffn_bwd_{l}{'ab'[ab]}", comm=comm)
        dg, du, dd = ffn_bwd_dw(sv['h' + s], dy, sv['a' + s], sv['b' + s], da, db, name=f"ffn_dw_{l}{'ab'[ab]}")
        ffn_dw[l, ab] = (shard_cols(dg), shard_cols(du), shard_rows(dd))
        return dcur, got

    sv = saved[1]
    dcur, _ = ffn_backward(1, 1, dcur)
    dm, dpre, grads['sg_b_in'], grads['sg_ln_g'], grads['sg_ln_b'], grads['sg_w_s'], dbs, dng[1][3] = sg_mid_bwd(
        dcur, sv['m'], gvec(1, 3), sv['pre'], bin_full, lng_full, lnb_full, sg_w_s[0], bsT, sg_wout, name="sg_mid_bwd_1")
    grads['sg_b_s'] = dbs[:, :G].T
    dsg_wout = tn_mm(sv['gated'], dm, name="sg_wout_dw_1").reshape(N_DEV, E // N_DEV, D)
    dsg_win = tn_mm(sv['hM'], dpre, name="sg_win_dw_1", tn=2 * E // N_DEV, slot_major=True)
    dcur, dng[1][2] = mm_bwd_dx(dcur, sv['x1'], gvec(1, 2), dpre, sg_win, None, None, name="sg_in_bwd_1")
    dcur, _ = ffn_backward(1, 0, dcur)
    sv = saved[0]
    dcur, _ = ffn_backward(0, 1, dcur)
    dm, do, dz, grads['dn_norm_g'], dng[0][3] = dn_out_bwd(dcur, sv['m'], gvec(0, 3), sv['o'], sv['proj'], dn_norm_g,
                                                          dn_wout, name="dn_out_bwd_0")
    ddn_wout = tn_mm(sv['og'], dm, name="dn_wout_dw_0").reshape(N_DEV, W // N_DEV, D)
    pair = lambda i: jnp.stack([ffn_dw[1, 0][i], ffn_dw[1, 1][i]], axis=1)
    (dqkv, dgB, dbB), got = dn_chunk_bwd(sv['qkv'], sv['g'], sv['beta'], sv['states'], do, name="dn_chunk_bwd_0",
                                         comm=Comm("exchange", [pair(0), pair(1), pair(2), dsg_win, dsg_wout]))
    l1_gate, l1_up, l1_down, slots['sg_w_in'], slots['sg_w_out'] = got
    dpba, dal, ddt = dn_gates_bwd(sv['pba'], alog_b, dtb_b, dbB, dgB, name="dn_gates_bwd_0")
    grads['dn_a_log'] = dal[:, 0, 0].reshape(1, H)
    grads['dn_dt_bias'] = ddt[:, 0, 0].reshape(1, H)
    (dproj, grads['dn_conv_w']), l0b = dn_prep_bwd(sv['proj'], conv_full, dqkv, dz, name="dn_prep_bwd_0",
                                                   comm=Comm("exchange", list(ffn_dw[0, 1])))
    dw_main = tn_mm(sv['hM'], dproj, name="dn_win_dw_0")
    dw_ba = tn_mm(sv['hM'], dpba, name="dn_wba_dw_0", tn=HEAD)
    dw_in = jnp.concatenate([dw_main, dw_ba[:, :2 * H]], axis=1)
    ddn_win = jnp.moveaxis(dw_in.reshape(D, N_DEV, c8), 1, 0)
    dcur, dng[0][2] = mm_bwd_dx(dcur, sv['x1'], gvec(0, 2), dproj, dn_wmain, dpba, dn_wba, name="dn_in_bwd_0")
    dcur, got = ffn_backward(0, 0, dcur, comm=Comm("exchange", [ddn_win, ddn_wout]))
    slots['dn_w_in'], slots['dn_w_out'] = got
    grad_x = dcur[None]
    l0a = exchange_slots(list(ffn_dw[0, 0]), name="exchange_last")
    for i, nm in enumerate(['ffn_w_gate', 'ffn_w_up', 'ffn_w_down']):
        slots[nm] = jnp.concatenate([l0a[i][:, None], l0b[i][:, None], (l1_gate, l1_up, l1_down)[i]], axis=1)
    big_names = ['ffn_w_gate', 'ffn_w_up', 'ffn_w_down', 'dn_w_in', 'dn_w_out', 'sg_w_in', 'sg_w_out']
    slots = [slots[nm] for nm in big_names]

    dng_full = jnp.stack([jnp.concatenate(r, axis=0) for r in dng], axis=0)
    small_names = ['norm_g', 'dn_conv_w', 'sg_b_in', 'sg_ln_g', 'sg_ln_b', 'sg_w_s', 'sg_b_s', 'dn_a_log',
                   'dn_dt_bias', 'dn_norm_g']
    small_parts = [dng_full, grads['dn_conv_w'], grads['sg_b_in'], grads['sg_ln_g'], grads['sg_ln_b'],
                   grads['sg_w_s'], grads['sg_b_s'], grads['dn_a_log'], grads['dn_dt_bias'], grads['dn_norm_g']]
    small_pack, offs = _pack_rows(small_parts)
    (small_slots,) = all_gather_multi([small_pack], name="gather_small_grads")
    small_sum = sum_slots(small_slots, name="sum_small_grads")

    def small_grad(i):
        r0, n = offs[i]
        p = small_parts[i]
        return small_sum[r0:r0 + n].reshape(-1)[:p.size].reshape(p.shape)

    def my_shard(full, axis, like):
        n = full.shape[axis] // N_DEV
        return lax.dynamic_slice_in_dim(full, me * n, n, axis).reshape(like.shape)

    g_small = {
        'norm_g': my_shard(small_grad(0), 2, norm_g),
        'dn_conv_w': my_shard(small_grad(1), 1, dn_conv_w),
        'sg_b_in': my_shard(small_grad(2), 1, sg_b_in),
        'sg_ln_g': my_shard(small_grad(3), 1, sg_ln_g),
        'sg_ln_b': my_shard(small_grad(4), 1, sg_ln_b),
        'sg_w_s': small_grad(5).reshape(sg_w_s.shape),
        'sg_b_s': small_grad(6).reshape(sg_b_s.shape),
        'dn_a_log': small_grad(7).reshape(dn_a_log.shape),
        'dn_dt_bias': small_grad(8).reshape(dn_dt_bias.shape),
        'dn_norm_g': small_grad(9).reshape(dn_norm_g.shape),
    }

    out_g, out_d, out_m, out_v = {}, {}, {}, {}
    for nm, r in zip(big_names, slots):
        w = weights[nm]
        cols = w.shape[-1]
        rows = w.size // cols
        tr = {'ffn_w_gate': 512, 'ffn_w_up': 512, 'ffn_w_down': F8, 'dn_w_in': 256, 'sg_w_in': 256}.get(nm, rows)
        g, d, m2, v2 = adam_slots(w.reshape(rows, cols), r.reshape(N_DEV, rows, cols), mom_m[nm].reshape(rows, cols),
                                  mom_v[nm].reshape(rows, cols), name=f"adam_{nm}", tr=tr)
        out_g[nm], out_d[nm], out_m[nm], out_v[nm] = (t.reshape(w.shape) for t in (g, d, m2, v2))
    for nm in small_names:
        w = weights[nm]
        cols = w.shape[-1]
        rows = w.size // cols
        two = lambda t: t.reshape(rows, cols)
        d, m2, v2 = adam_small(two(w), two(g_small[nm]), two(mom_m[nm]), two(mom_v[nm]), name=f"adam_{nm}")
        out_g[nm] = g_small[nm]
        out_d[nm], out_m[nm], out_v[nm] = (t.reshape(w.shape) for t in (d, m2, v2))

    return (loss, grad_x, *[out_g[n] for n in order], *[out_d[n] for n in order], *[out_m[n] for n in order],
            *[out_v[n] for n in order])
```

```python
import functools
import math

import jax
import jax.numpy as jnp
from jax import lax
from jax.experimental import pallas as pl
from jax.experimental.pallas import tpu as pltpu

f32 = jnp.float32
MXU_DTYPE = jnp.bfloat16
N_DEV = 8
RMS_EPS = 1e-6
LN_EPS = 1e-5
L2_EPS = 1e-6
HEAD = 128
DN_CHUNK = 64
SG_CHUNK = 128
SG_GROUPS = 8
CONV_K = 4
ADAM_LR, ADAM_B1, ADAM_B2, ADAM_EPS, ADAM_WD, ADAM_STEP = 0.001, 0.9, 0.999, 1e-08, 0.01, 10
VMEM_LIMIT = 56 * 1024 * 1024
FFN_ROWS_FWD, FFN_ROWS_BWD, FFN_ROWS_DW = 1024, 512, 1024
PROJ_ROWS, TN_ROWS = 1024, 2048
FFN_SLABS = 4
SDS = jax.ShapeDtypeStruct
HIGHEST = lax.Precision.HIGHEST
MESH = pl.DeviceIdType.MESH


def _params(n_grid):
    return pltpu.CompilerParams(dimension_semantics=("arbitrary",) * n_grid, vmem_limit_bytes=VMEM_LIMIT)


def _row_tile(s, want):
    t = min(s, want)
    assert s % t == 0, (s, t)
    return t


def _rms(x, g):
    return x * lax.rsqrt(jnp.mean(x * x, axis=-1, keepdims=True) + RMS_EPS) * g


def _rms_bwd(x, g, dy):
    _, vjp = jax.vjp(_rms, x, g)
    return vjp(dy)


def _silu(a):
    return a * jax.nn.sigmoid(a)


def _gelu(x):
    return 0.5 * x * (1.0 + lax.erf(x * 0.7071067811865476))


def _mm(a, b):
    return lax.dot_general(a, b, (((1,), (0,)), ((), ())), preferred_element_type=f32)


def _mm_nt(a, b):
    return lax.dot_general(a, b, (((1,), (1,)), ((), ())), preferred_element_type=f32)


def _mm_tn(a, b):
    return lax.dot_general(a, b, (((0,), (0,)), ((), ())), preferred_element_type=f32)


def _c(x):
    return x.astype(MXU_DTYPE)


def _split(a):
    hi = a.astype(MXU_DTYPE)
    lo = (a - hi.astype(f32)).astype(MXU_DTYPE)
    return hi, lo


def _dot3(a, b, dims):
    ah, al = _split(a)
    bh, bl = _split(b)
    d = lambda p, q: lax.dot_general(p, q, (dims, ((), ())), preferred_element_type=f32)
    return d(ah, bh) + (d(ah, bl) + d(al, bh))


NN, NT, TN = ((1,), (0,)), ((1,), (1,)), ((0,), (0,))


def _slot(px, py, pc):
    return 4 * px + 2 * py + pc


def all_gather_multi(arrs, name):
    return Comm("gather", arrs).alone(name)


def exchange_slots(arrs, name):
    return Comm("exchange", arrs).alone(name)


class Comm:
    def __init__(self, kind, arrs):
        self.kind, self.arrs, self.n = kind, list(arrs), len(arrs)
        hbm = pl.BlockSpec(memory_space=pltpu.HBM)
        self.in_specs = [hbm] * self.n
        self.out_specs = [hbm] * self.n
        lead = (N_DEV,) if kind == "gather" else ()
        self.out_shape = [SDS(lead + tuple(a.shape), a.dtype) for a in self.arrs]
        self.scratch = [pltpu.SemaphoreType.DMA((self.n, 7)), pltpu.SemaphoreType.DMA((self.n, 7)),
                        pltpu.SemaphoreType.DMA((self.n,))]

    def phase(self, p, ins, outs, sems):
        (self._gather if self.kind == "gather" else self._exchange)(p, ins, outs, sems)

    def _gather(self, p, ins, outs, sems):
        send_sems, recv_sems, local_sems = sems
        x, y, c = lax.axis_index("x"), lax.axis_index("y"), lax.axis_index("c")
        me, sibling = (x, y, c), (x, y, 1 - c)
        chips = [(1 - x, y), (x, 1 - y), (1 - x, 1 - y)]

        def copy(a, k, block, to, src=None):
            dst = outs[a].at[_slot(*block)]
            return pltpu.make_async_remote_copy(
                src_ref=dst if src is None else src, dst_ref=dst, send_sem=send_sems.at[a, k],
                recv_sem=recv_sems.at[a, k], device_id=to, device_id_type=MESH)

        mine = [pltpu.make_async_copy(ins[a], outs[a].at[_slot(*me)], local_sems.at[a]) for a in range(self.n)]
        first = [[copy(a, 0, me, sibling, src=ins[a])] +
                 [copy(a, 1 + j, me, (*chip, c), src=ins[a]) for j, chip in enumerate(chips)] for a in range(self.n)]
        passed = [[copy(a, 4 + j, (*chip, c), sibling) for j, chip in enumerate(chips)] for a in range(self.n)]
        if p == 0:
            for a in range(self.n):
                mine[a].start()
            for a in range(self.n):
                for cp in first[a]:
                    cp.start()
        elif p == 1:
            for a in range(self.n):
                for j, chip in enumerate(chips):
                    copy(a, 1 + j, (*chip, c), me).wait_recv()
                    passed[a][j].start()
        else:
            for a in range(self.n):
                copy(a, 0, sibling, me).wait_recv()
                for j, chip in enumerate(chips):
                    copy(a, 4 + j, (*chip, 1 - c), me).wait_recv()
            for a in range(self.n):
                for cp in first[a] + passed[a]:
                    cp.wait_send()
                mine[a].wait()

    def _exchange(self, p, ins, outs, sems):
        send_sems, recv_sems, local_sems = sems
        x, y, c = lax.axis_index("x"), lax.axis_index("y"), lax.axis_index("c")
        me = _slot(x, y, c)
        peers = [(x ^ (k >> 2), y ^ ((k >> 1) & 1), c ^ (k & 1)) for k in range(1, N_DEV)]

        def copy(a, k):
            peer = peers[k - 1]
            return pltpu.make_async_remote_copy(
                src_ref=ins[a].at[_slot(*peer)], dst_ref=outs[a].at[me], send_sem=send_sems.at[a, k - 1],
                recv_sem=recv_sems.at[a, k - 1], device_id=peer, device_id_type=MESH)

        def landed(a, k):
            peer = peers[k - 1]
            return pltpu.make_async_remote_copy(
                src_ref=ins[a].at[me], dst_ref=outs[a].at[_slot(*peer)], send_sem=send_sems.at[a, k - 1],
                recv_sem=recv_sems.at[a, k - 1], device_id=peer, device_id_type=MESH)

        local = [pltpu.make_async_copy(ins[a].at[me], outs[a].at[me], local_sems.at[a]) for a in range(self.n)]
        order = [6, 7, 2, 3, 4, 5, 1]
        if p == 0:
            for a in range(self.n):
                local[a].start()
            for a in range(self.n):
                for k in order:
                    copy(a, k).start()
        elif p == 2:
            for a in range(self.n):
                for k in order:
                    copy(a, k).wait_send()
                    landed(a, k).wait_recv()
                local[a].wait()

    def alone(self, name):
        n = self.n

        def body(*refs):
            for p in range(3):
                self.phase(p, refs[:n], refs[n:2 * n], refs[2 * n:])

        return pl.pallas_call(body, name=name, out_shape=tuple(self.out_shape), in_specs=self.in_specs,
                              out_specs=tuple(self.out_specs), scratch_shapes=self.scratch)(*self.arrs)


def hosted_call(body, comm, steps, *, name, grid, in_specs, out_specs, out_shape, scratch_shapes, args):
    if comm is None:
        outs = pl.pallas_call(body, name=name, grid=grid, in_specs=in_specs, out_specs=tuple(out_specs),
                              out_shape=tuple(out_shape), scratch_shapes=scratch_shapes,
                              compiler_params=_params(len(grid)))(*args)
        return outs, None
    ni, no, ns, cn = len(in_specs), len(out_specs), len(scratch_shapes), comm.n

    def both(*refs):
        h_in, c_in = refs[:ni], refs[ni:ni + cn]
        h_out, c_out = refs[ni + cn:ni + cn + no], refs[ni + cn + no:ni + 2 * cn + no]
        h_scr, c_scr = refs[ni + 2 * cn + no:ni + 2 * cn + no + ns], refs[ni + 2 * cn + no + ns:]
        when = steps()
        pl.when(when[0])(lambda: comm.phase(0, c_in, c_out, c_scr))
        body(*h_in, *h_out, *h_scr)
        pl.when(when[1])(lambda: comm.phase(1, c_in, c_out, c_scr))
        pl.when(when[2])(lambda: comm.phase(2, c_in, c_out, c_scr))

    outs = pl.pallas_call(
        both, name=name, grid=grid, in_specs=list(in_specs) + comm.in_specs,
        out_specs=tuple(out_specs) + tuple(comm.out_specs), out_shape=tuple(out_shape) + tuple(comm.out_shape),
        scratch_shapes=list(scratch_shapes) + comm.scratch, compiler_params=_params(len(grid)),
    )(*args, *comm.arrs)
    return outs[:no], outs[no:]


def _grid_steps(n_outer, n_inner=1):
    total = n_outer * n_inner

    def steps():
        t = pl.program_id(0) * n_inner + (pl.program_id(1) if n_inner > 1 else 0)
        return t == 0, t == (total * 5) // 8, t == total - 1
    return steps


def ffn_fwd(x, gpre, gpost, wg, wu, wd, name, comm=None):
    S, D = x.shape
    nj, F8 = wg.shape[0], wg.shape[-1]
    tm = _row_tile(S, FFN_ROWS_FWD)

    def body(x_ref, gpre_ref, gpost_ref, wg_ref, wu_ref, wd_ref, xo_ref, h_ref, a_ref, b_ref, y_ref):
        j = pl.program_id(1)

        @pl.when(j == 0)
        def _():
            h_ref[...] = _rms(x_ref[...], gpre_ref[...]).astype(h_ref.dtype)
            y_ref[...] = jnp.zeros_like(y_ref)

        h = h_ref[...]
        a = _mm(h, wg_ref[...]).astype(a_ref.dtype)
        b = _mm(h, wu_ref[...]).astype(b_ref.dtype)
        a_ref[...] = a
        b_ref[...] = b
        t = _silu(a.astype(f32)) * b.astype(f32)
        y_ref[...] += _mm(_c(t), wd_ref[...])

        @pl.when(j == nj - 1)
        def _():
            xo_ref[...] = x_ref[...] + 0.5 * _rms(y_ref[...], gpost_ref[...])

    row = pl.BlockSpec((tm, D), lambda i, j: (i, 0))
    vec = pl.BlockSpec((1, D), lambda i, j: (0, 0))
    wcol = pl.BlockSpec((None, D, F8), lambda i, j: (j, 0, 0))
    wrow = pl.BlockSpec((None, F8, D), lambda i, j: (j, 0, 0))
    hid = pl.BlockSpec((None, tm, F8), lambda i, j: (j, i, 0))
    return hosted_call(
        body, comm, _grid_steps(S // tm, nj), name=name, grid=(S // tm, nj),
        in_specs=[row, vec, vec, wcol, wcol, wrow],
        out_specs=(row, row, hid, hid, row),
        out_shape=(SDS((S, D), f32), SDS((S, D), MXU_DTYPE), SDS((nj, S, F8), MXU_DTYPE),
                   SDS((nj, S, F8), MXU_DTYPE), SDS((S, D), f32)),
        scratch_shapes=[], args=(x, gpre, gpost, wg, wu, wd))


def ffn_bwd_dx(dxo, x, y, a, b, gpre, gpost, wg, wu, wd, name, comm=None):
    S, D = x.shape
    nj, F8 = wg.shape[0], wg.shape[-1]
    tm = _row_tile(S, FFN_ROWS_BWD)

    def body(dxo_ref, x_ref, y_ref, a_ref, b_ref, gpre_ref, gpost_ref, wg_ref, wu_ref, wd_ref,
             dx_ref, da_ref, db_ref, dy_ref, dgpre_ref, dgpost_ref, dh_ref):
        i, j = pl.program_id(0), pl.program_id(1)

        @pl.when(j == 0)
        def _():
            @pl.when(i == 0)
            def _():
                dgpre_ref[...] = jnp.zeros_like(dgpre_ref)
                dgpost_ref[...] = jnp.zeros_like(dgpost_ref)

            dy, dg = _rms_bwd(y_ref[...], gpost_ref[...], 0.5 * dxo_ref[...])
            dy_ref[...] = dy.astype(dy_ref.dtype)
            dgpost_ref[...] += dg
            dh_ref[...] = jnp.zeros_like(dh_ref)

        half = tm // 2 if tm % 16 == 0 else tm
        for r0 in range(0, tm, half):
            rs = slice(r0, r0 + half)
            dt = _mm_nt(dy_ref[rs, :], wd_ref[...])
            af, bf = a_ref[rs, :].astype(f32), b_ref[rs, :].astype(f32)
            s = jax.nn.sigmoid(af)
            sa = af * s
            da = (dt * bf * (s + sa * (1.0 - s))).astype(da_ref.dtype)
            db = (dt * sa).astype(db_ref.dtype)
            da_ref[rs, :] = da
            db_ref[rs, :] = db
            dh_ref[rs, :] += _mm_nt(da, wg_ref[...]) + _mm_nt(db, wu_ref[...])

        @pl.when(j == nj - 1)
        def _():
            dxx, dg = _rms_bwd(x_ref[...], gpre_ref[...], dh_ref[...])
            dx_ref[...] = dxo_ref[...] + dxx
            dgpre_ref[...] += dg

    row = pl.BlockSpec((tm, D), lambda i, j: (i, 0))
    vec = pl.BlockSpec((1, D), lambda i, j: (0, 0))
    wcol = pl.BlockSpec((None, D, F8), lambda i, j: (j, 0, 0))
    wrow = pl.BlockSpec((None, F8, D), lambda i, j: (j, 0, 0))
    hid = pl.BlockSpec((None, tm, F8), lambda i, j: (j, i, 0))
    return hosted_call(
        body, comm, _grid_steps(S // tm, nj), name=name, grid=(S // tm, nj),
        in_specs=[row, row, row, hid, hid, vec, vec, wcol, wcol, wrow],
        out_specs=(row, hid, hid, row, vec, vec),
        out_shape=(SDS((S, D), f32), SDS((nj, S, F8), MXU_DTYPE), SDS((nj, S, F8), MXU_DTYPE),
                   SDS((S, D), MXU_DTYPE), SDS((1, D), f32), SDS((1, D), f32)),
        scratch_shapes=[pltpu.VMEM((tm, D), f32)], args=(dxo, x, y, a, b, gpre, gpost, wg, wu, wd))


def ffn_bwd_dw(h, dy, a, b, da, db, name):
    S, D = h.shape
    F8 = a.shape[-1]
    tm = _row_tile(S, FFN_ROWS_DW)
    ni = S // tm

    def body(h_ref, dy_ref, a_ref, b_ref, da_ref, db_ref, dwg_ref, dwu_ref, dwd_ref, accg, accu, accd):
        i = pl.program_id(1)

        @pl.when(i == 0)
        def _():
            accg[...] = jnp.zeros_like(accg)
            accu[...] = jnp.zeros_like(accu)
            accd[...] = jnp.zeros_like(accd)

        t = _c(_silu(a_ref[...].astype(f32)) * b_ref[...].astype(f32))
        hh = h_ref[...]
        accg[...] += _mm_tn(hh, da_ref[...])
        accu[...] += _mm_tn(hh, db_ref[...])
        accd[...] += _mm_tn(t, dy_ref[...])

        @pl.when(i == ni - 1)
        def _():
            dwg_ref[...] = accg[...].astype(dwg_ref.dtype)
            dwu_ref[...] = accu[...].astype(dwu_ref.dtype)
            dwd_ref[...] = accd[...].astype(dwd_ref.dtype)

    row = pl.BlockSpec((tm, D), lambda j, i: (i, 0))
    hid = pl.BlockSpec((None, tm, F8), lambda j, i: (j, i, 0))
    wcol = pl.BlockSpec((None, D, F8), lambda j, i: (j, 0, 0))
    wrow = pl.BlockSpec((None, F8, D), lambda j, i: (j, 0, 0))
    return pl.pallas_call(
        body, name=name, grid=(a.shape[0], ni),
        in_specs=[row, row, hid, hid, hid, hid],
        out_specs=(wcol, wcol, wrow),
        out_shape=(SDS((a.shape[0], D, F8), MXU_DTYPE), SDS((a.shape[0], D, F8), MXU_DTYPE),
                   SDS((a.shape[0], F8, D), MXU_DTYPE)),
        scratch_shapes=[pltpu.VMEM((D, F8), f32), pltpu.VMEM((D, F8), f32), pltpu.VMEM((F8, D), f32)],
        compiler_params=_params(2),
    )(h, dy, a, b, da, db)


def rms_mm(x, g, w, w2, name, tn=1024):
    S, D = x.shape
    N = w.shape[1]
    tm = _row_tile(S, PROJ_ROWS)
    tn = _row_tile(N, tn)
    has2 = w2 is not None

    def body(*refs):
        if has2:
            x_ref, g_ref, w_ref, w2_ref, h_ref, o_ref, o2_ref = refs
        else:
            x_ref, g_ref, w_ref, h_ref, o_ref = refs
        j = pl.program_id(1)

        @pl.when(j == 0)
        def _():
            h = _rms(x_ref[...], g_ref[...]).astype(h_ref.dtype)
            h_ref[...] = h
            if has2:
                o2_ref[...] = _mm(h, w2_ref[...])

        o_ref[...] = _mm(h_ref[...], w_ref[...])

    row = pl.BlockSpec((tm, D), lambda i, j: (i, 0))
    in_specs = [row, pl.BlockSpec((1, D), lambda i, j: (0, 0)), pl.BlockSpec((D, tn), lambda i, j: (0, j))]
    out_specs = [row, pl.BlockSpec((tm, tn), lambda i, j: (i, j))]
    out_shape = [SDS((S, D), MXU_DTYPE), SDS((S, N), f32)]
    args = [x, g, w]
    if has2:
        in_specs.append(pl.BlockSpec((D, w2.shape[1]), lambda i, j: (0, 0)))
        out_specs.append(pl.BlockSpec((tm, w2.shape[1]), lambda i, j: (i, 0)))
        out_shape.append(SDS((S, w2.shape[1]), f32))
        args.append(w2)
    return pl.pallas_call(
        body, name=name, grid=(S // tm, N // tn), in_specs=in_specs, out_specs=tuple(out_specs),
        out_shape=tuple(out_shape), compiler_params=_params(2),
    )(*args)


def mm_bwd_dx(dres, x, g, dy, w, dy2, w2, name, tk=1024):
    S, D = x.shape
    K = dy.shape[1]
    tm = _row_tile(S, PROJ_ROWS)
    tk = _row_tile(K, tk)
    nk = K // tk
    has2 = dy2 is not None

    def body(*refs):
        if has2:
            dres_ref, x_ref, g_ref, dy_ref, w_ref, dy2_ref, w2_ref, dx_ref, dg_ref, dh_ref = refs
        else:
            dres_ref, x_ref, g_ref, dy_ref, w_ref, dx_ref, dg_ref, dh_ref = refs
        i, k = pl.program_id(0), pl.program_id(1)

        @pl.when(k == 0)
        def _():
            @pl.when(i == 0)
            def _():
                dg_ref[...] = jnp.zeros_like(dg_ref)

            if has2:
                dh_ref[...] = _mm_nt(dy2_ref[...], w2_ref[...])
            else:
                dh_ref[...] = jnp.zeros_like(dh_ref)

        dh_ref[...] += _mm_nt(dy_ref[...], w_ref[...])

        @pl.when(k == nk - 1)
        def _():
            dxx, dg = _rms_bwd(x_ref[...], g_ref[...], dh_ref[...])
            dx_ref[...] = dres_ref[...] + dxx
            dg_ref[...] += dg

    row = pl.BlockSpec((tm, D), lambda i, k: (i, 0))
    vec = pl.BlockSpec((1, D), lambda i, k: (0, 0))
    in_specs = [row, row, vec, pl.BlockSpec((tm, tk), lambda i, k: (i, k)), pl.BlockSpec((D, tk), lambda i, k: (0, k))]
    args = [dres, x, g, dy, w]
    if has2:
        in_specs += [pl.BlockSpec((tm, dy2.shape[1]), lambda i, k: (i, 0)),
                     pl.BlockSpec((D, w2.shape[1]), lambda i, k: (0, 0))]
        args += [dy2, w2]
    return pl.pallas_call(
        body, name=name, grid=(S // tm, nk), in_specs=in_specs, out_specs=(row, vec),
        out_shape=(SDS((S, D), f32), SDS((1, D), f32)),
        scratch_shapes=[pltpu.VMEM((tm, D), f32)], compiler_params=_params(2),
    )(*args)


def tn_mm(a, b, name, tn=512, slot_major=False):
    S, K1 = a.shape
    N = b.shape[1]
    tm = _row_tile(S, TN_ROWS)
    tn = _row_tile(N, tn)
    ni = S // tm

    def body(a_ref, b_ref, o_ref, acc):
        i = pl.program_id(1)

        @pl.when(i == 0)
        def _():
            acc[...] = jnp.zeros_like(acc)

        acc[...] += _mm_tn(a_ref[...], b_ref[...])

        @pl.when(i == ni - 1)
        def _():
            o_ref[...] = acc[...].astype(o_ref.dtype)

    if slot_major:
        out_spec, out_shape = pl.BlockSpec((None, K1, tn), lambda j, i: (j, 0, 0)), SDS((N // tn, K1, tn), MXU_DTYPE)
    else:
        out_spec, out_shape = pl.BlockSpec((K1, tn), lambda j, i: (0, j)), SDS((K1, N), MXU_DTYPE)
    return pl.pallas_call(
        body, name=name, grid=(N // tn, ni),
        in_specs=[pl.BlockSpec((tm, K1), lambda j, i: (i, 0)), pl.BlockSpec((tm, tn), lambda j, i: (i, j))],
        out_specs=out_spec, out_shape=out_shape,
        scratch_shapes=[pltpu.VMEM((K1, tn), f32)], compiler_params=_params(2),
    )(a, b)


CONV_ROWS = 512


def _shift_down(cur, prev8, s):
    r = pltpu.roll(cur, s, 0)
    row = lax.broadcasted_iota(jnp.int32, (8, cur.shape[1]), 0)
    top = jnp.where(row < s, pltpu.roll(prev8, s, 0), r[0:8])
    return jnp.concatenate([top, r[8:]], axis=0)


def _shift_up(cur, next8, s):
    n = cur.shape[0]
    r = pltpu.roll(cur, n - s, 0)
    row = lax.broadcasted_iota(jnp.int32, (8, cur.shape[1]), 0)
    bot = jnp.where(row >= 8 - s, pltpu.roll(next8, 8 - s, 0), r[n - 8:])
    return jnp.concatenate([r[:n - 8], bot], axis=0)


def _conv_taps(cur, prev8):
    return [_shift_down(cur, prev8, 3), _shift_down(cur, prev8, 2), _shift_down(cur, prev8, 1), cur]


def _act_qk(c):
    a = _silu(c)
    return a * lax.rsqrt(jnp.sum(a * a, axis=-1, keepdims=True) + L2_EPS)


def dn_prep(proj, conv_w, name):
    S = proj.shape[0]
    W = conv_w.shape[1] // 3
    nh = W // HEAD
    R = _row_tile(S, CONV_ROWS)

    def body(p_ref, w_ref, o_ref):
        j = pl.program_id(0)
        w = w_ref[...]

        def rows(r, prev8):
            cur = p_ref[pl.ds(r, R), :]
            taps = _conv_taps(cur, prev8)
            cv = taps[0] * w[0:1] + taps[1] * w[1:2] + taps[2] * w[2:3] + taps[3] * w[3:4]

            @pl.when(j < 2 * nh)
            def _():
                o_ref[pl.ds(r, R), :] = _act_qk(cv)

            @pl.when(j >= 2 * nh)
            def _():
                o_ref[pl.ds(r, R), :] = _silu(cv)

        rows(0, jnp.zeros((8, HEAD), f32))

        @pl.loop(1, S // R)
        def _(t):
            r = pl.multiple_of(t * R, R)
            rows(r, p_ref[pl.ds(r - 8, 8), :])

    return pl.pallas_call(
        body, name=name, grid=(3 * nh,),
        in_specs=[pl.BlockSpec((S, HEAD), lambda j: (0, j)), pl.BlockSpec((CONV_K, HEAD), lambda j: (0, j))],
        out_specs=pl.BlockSpec((None, S, HEAD), lambda j: (j // nh, 0, j % nh)),
        out_shape=SDS((3, S, W), f32), compiler_params=_params(1),
    )(proj, conv_w)


def dn_prep_bwd(proj, conv_w, dqkv, dz, name, comm=None):
    S = proj.shape[0]
    W = conv_w.shape[1] // 3
    nh = W // HEAD
    nq = 3 * nh
    R = _row_tile(S, CONV_ROWS)
    nr = S // R

    def body(p_ref, w_ref, dq_ref, dz_ref, dp_ref, dw_ref, dc_ref):
        j = pl.program_id(0)

        @pl.when(j >= nq)
        def _():
            dp_ref[...] = dz_ref[...].astype(dp_ref.dtype)

        @pl.when(j < nq)
        def _():
            w = w_ref[...]
            dw_ref[...] = jnp.zeros_like(dw_ref)

            def rows(r, prev8):
                cur = p_ref[pl.ds(r, R), :]
                taps = _conv_taps(cur, prev8)
                cv = taps[0] * w[0:1] + taps[1] * w[1:2] + taps[2] * w[2:3] + taps[3] * w[3:4]
                dn = dq_ref[pl.ds(r, R), :]

                @pl.when(j < 2 * nh)
                def _():
                    dc_ref[pl.ds(r, R), :] = jax.vjp(_act_qk, cv)[1](dn)[0]

                @pl.when(j >= 2 * nh)
                def _():
                    dc_ref[pl.ds(r, R), :] = jax.vjp(_silu, cv)[1](dn)[0]

                dc = dc_ref[pl.ds(r, R), :]
                dw_ref[...] += jnp.concatenate(
                    [jnp.sum(dc * taps[q], axis=0, keepdims=True) for q in range(CONV_K)], axis=0)

            rows(0, jnp.zeros((8, HEAD), f32))

            @pl.loop(1, nr)
            def _(t):
                r = pl.multiple_of(t * R, R)
                rows(r, p_ref[pl.ds(r - 8, 8), :])

            def back(r, next8):
                dc = dc_ref[pl.ds(r, R), :]
                dx = dc * w[3:4]
                for s in (1, 2, 3):
                    dx = dx + _shift_up(dc, next8, s) * w[3 - s:4 - s]
                dp_ref[pl.ds(r, R), :] = dx.astype(dp_ref.dtype)

            @pl.loop(0, nr - 1)
            def _(t):
                r = pl.multiple_of(t * R, R)
                back(r, dc_ref[pl.ds(r + R, 8), :])

            back((nr - 1) * R, jnp.zeros((8, HEAD), f32))

    clamp = lambda j: jnp.minimum(j, nq - 1)
    return hosted_call(
        body, comm, _grid_steps(4 * nh), name=name, grid=(4 * nh,),
        in_specs=[pl.BlockSpec((S, HEAD), lambda j: (0, clamp(j))),
                  pl.BlockSpec((CONV_K, HEAD), lambda j: (0, clamp(j))),
                  pl.BlockSpec((None, S, HEAD), lambda j: (clamp(j) // nh, 0, clamp(j) % nh)),
                  pl.BlockSpec((S, HEAD), lambda j: (0, jnp.maximum(j - nq, 0)))],
        out_specs=(pl.BlockSpec((S, HEAD), lambda j: (0, j)), pl.BlockSpec((CONV_K, HEAD), lambda j: (0, clamp(j)))),
        out_shape=(SDS((S, 4 * W), MXU_DTYPE), SDS((CONV_K, 3 * W), f32)),
        scratch_shapes=[pltpu.VMEM((S, HEAD), f32)], args=(proj, conv_w, dqkv, dz))


def _gate_fns(braw, araw, alog, dtb):
    beta = jax.nn.sigmoid(braw)
    g = -jnp.exp(alog) * jax.nn.softplus(araw + dtb)
    return beta, g


def _lane_pick(x, lane):
    sel = lax.broadcasted_iota(jnp.int32, x.shape, 1) == lane
    return jnp.broadcast_to(jnp.sum(jnp.where(sel, x, 0.0), axis=1, keepdims=True), x.shape)


CUM_ROWS = 256


def _sel_mm(m01, x):
    m = _c(m01)
    d = lambda p: lax.dot_general(m, p, (NN, ((), ())), preferred_element_type=f32)
    h1, h2, h3 = _pieces3(x)
    return (d(h1) + d(h2)) + d(h3)


def _chunk_cumsum_matrix(n, transpose):
    r, c = lax.broadcasted_iota(jnp.int32, (n, n), 0), lax.broadcasted_iota(jnp.int32, (n, n), 1)
    sh = int(math.log2(DN_CHUNK))
    same = (r >> sh) == (c >> sh)
    return jnp.where(same & ((r <= c) if transpose else (r >= c)), 1.0, 0.0).astype(f32)


def dn_gates(pba, alog_b, dtb_b, name):
    S = pba.shape[0]
    H = alog_b.shape[0]
    R = _row_tile(S, CUM_ROWS)

    def body(p_ref, al_ref, dt_ref, beta_ref, g_ref):
        h = pl.program_id(0)
        p = p_ref[...]
        beta, g = _gate_fns(_lane_pick(p, h), _lane_pick(p, H + h), al_ref[...], dt_ref[...])
        beta_ref[...] = beta
        g_ref[...] = g
        cum = _chunk_cumsum_matrix(R, False)

        @pl.loop(0, S // R)
        def _(t):
            r = pl.multiple_of(t * R, R)
            g_ref[pl.ds(r, R), :] = _sel_mm(cum, g_ref[pl.ds(r, R), :])

    par = pl.BlockSpec((None, 1, HEAD), lambda h: (h, 0, 0))
    out = pl.BlockSpec((None, S, HEAD), lambda h: (h, 0, 0))
    return pl.pallas_call(
        body, name=name, grid=(H,), in_specs=[pl.BlockSpec((S, HEAD), lambda h: (0, 0)), par, par],
        out_specs=(out, out), out_shape=(SDS((H, S, HEAD), f32), SDS((H, S, HEAD), f32)), compiler_params=_params(1),
    )(pba, alog_b, dtb_b)


def dn_gates_bwd(pba, alog_b, dtb_b, dbeta, dg, name):
    S = pba.shape[0]
    H = alog_b.shape[0]
    R = _row_tile(S, CUM_ROWS)

    def body(p_ref, al_ref, dt_ref, dbeta_ref, dg_ref, dp_ref, dal_ref, ddt_ref, acc, dgs):
        h = pl.program_id(0)

        @pl.when(h == 0)
        def _():
            acc[...] = jnp.zeros_like(acc)

        cum_t = _chunk_cumsum_matrix(R, True)

        @pl.loop(0, S // R)
        def _(t):
            r = pl.multiple_of(t * R, R)
            dgs[pl.ds(r, R), :] = _sel_mm(cum_t, dg_ref[pl.ds(r, R), :])

        p = p_ref[...]
        db = jnp.broadcast_to(jnp.sum(dbeta_ref[...], axis=1, keepdims=True), p.shape)
        dgg = jnp.broadcast_to(jnp.sum(dgs[...], axis=1, keepdims=True), p.shape)
        _, vjp = jax.vjp(_gate_fns, _lane_pick(p, h), _lane_pick(p, H + h), al_ref[...], dt_ref[...])
        dbraw, daraw, dal, ddt = vjp((db, dgg))
        lane = lax.broadcasted_iota(jnp.int32, p.shape, 1)
        acc[...] += jnp.where(lane == h, dbraw, 0.0) + jnp.where(lane == H + h, daraw, 0.0)
        dal_ref[...] = dal
        ddt_ref[...] = ddt

        @pl.when(h == H - 1)
        def _():
            dp_ref[...] = acc[...].astype(dp_ref.dtype)

    par = pl.BlockSpec((None, 1, HEAD), lambda h: (h, 0, 0))
    big = pl.BlockSpec((None, S, HEAD), lambda h: (h, 0, 0))
    full = pl.BlockSpec((S, HEAD), lambda h: (0, 0))
    return pl.pallas_call(
        body, name=name, grid=(H,), in_specs=[full, par, par, big, big],
        out_specs=(full, par, par),
        out_shape=(SDS((S, HEAD), MXU_DTYPE), SDS((H, 1, HEAD), f32), SDS((H, 1, HEAD), f32)),
        scratch_shapes=[pltpu.VMEM((S, HEAD), f32), pltpu.VMEM((S, HEAD), f32)], compiler_params=_params(1),
    )(pba, alog_b, dtb_b, dbeta, dg)


def _bdot(dims):
    back = {NN: ((NT, 'gb'), (TN, 'ag')), NT: ((NN, 'gb'), (TN, 'ga')), TN: ((NT, 'bg'), (NN, 'ag'))}[dims]
    d = lambda p, q, dm: lax.dot_general(_c(p), _c(q), (dm, ((), ())), preferred_element_type=f32)

    @jax.custom_vjp
    def f(a, b):
        return d(a, b, dims)

    def fwd(a, b):
        return d(a, b, dims), (a, b)

    def bwd(res, g):
        v = {'a': res[0], 'b': res[1], 'g': g}
        (da_dims, da_ops), (db_dims, db_ops) = back
        return d(v[da_ops[0]], v[da_ops[1]], da_dims), d(v[db_ops[0]], v[db_ops[1]], db_dims)

    f.defvjp(fwd, bwd)
    return f, lambda a, b: d(a, b, dims)


_BDOT = {dims: _bdot(dims) for dims in (NN, NT, TN)}


def _tri_inv_multi(Ls):
    n = Ls[0].shape[0]
    eye = jnp.where(lax.broadcasted_iota(jnp.int32, (n, n), 0) == lax.broadcasted_iota(jnp.int32, (n, n), 1), 1.0, 0.0)
    P = tuple(-L for L in Ls)
    T = tuple(eye + p for p in P)
    for _ in range(int(math.log2(n)) - 1):
        P = tuple(_dot3(p, p, NN) for p in P)
        T = tuple(t + _dot3(t, p, NN) for t, p in zip(T, P))
    return T


@jax.custom_vjp
def _tri_inv_multi_vjp(Ls):
    return _tri_inv_multi(Ls)


def _tri_inv_fwd(Ls):
    T = _tri_inv_multi(Ls)
    return T, T


def _tri_inv_bwd(T, dT):
    X = tuple(_dot3(d, t, NT) for d, t in zip(dT, T))
    return (tuple(-_dot3(t, x, TN) for t, x in zip(T, X)),)


_tri_inv_multi_vjp.defvjp(_tri_inv_fwd, _tri_inv_bwd)


def _pieces3(x):
    h1 = x.astype(MXU_DTYPE)
    r1 = x - h1.astype(f32)
    h2 = r1.astype(MXU_DTYPE)
    return h1, h2, (r1 - h2.astype(f32)).astype(MXU_DTYPE)


def _row_bcast_impl(sel_row, gc):
    s = _c(sel_row)
    d = lambda p: lax.dot_general(s, p, (NT, ((), ())), preferred_element_type=f32)
    h1, h2, h3 = _pieces3(gc)
    return (d(h1) + d(h2)) + d(h3)


def _row_bcast_bwd(sel_row, d):
    s = _c(sel_row)
    hi, lo = _split(d)
    t = lambda p: lax.dot_general(p, s, (TN, ((), ())), preferred_element_type=f32)
    return jnp.zeros_like(sel_row), t(hi) + t(lo)


_row_bcast = jax.custom_vjp(_row_bcast_impl)
_row_bcast.defvjp(lambda sel_row, gc: (_row_bcast_impl(sel_row, gc), sel_row), _row_bcast_bwd)


def _col_bcast_impl(gc):
    return gc[:, :DN_CHUNK]


def _col_bcast_bwd(_, d):
    return (jnp.broadcast_to(jnp.sum(d, axis=1, keepdims=True) * (1.0 / HEAD), (d.shape[0], HEAD)),)


_col_bcast = jax.custom_vjp(_col_bcast_impl)
_col_bcast.defvjp(lambda gc: (_col_bcast_impl(gc), None), _col_bcast_bwd)


def _last_row_bcast(n):
    def impl(gc):
        return jnp.broadcast_to(gc[DN_CHUNK - 1:DN_CHUNK, :], (n, HEAD))

    def bwd(_, d):
        row = lax.broadcasted_iota(jnp.int32, (DN_CHUNK, HEAD), 0)
        return (jnp.where(row == DN_CHUNK - 1, jnp.sum(d, axis=0, keepdims=True), 0.0),)

    f = jax.custom_vjp(impl)
    f.defvjp(lambda gc: (impl(gc), None), bwd)
    return impl, f


_LAST_C, _LAST_H = _last_row_bcast(DN_CHUNK), _last_row_bcast(HEAD)


def _chunk_consts():
    C = DN_CHUNK
    io = lambda shape, ax: lax.broadcasted_iota(jnp.int32, shape, ax)
    one = lambda m: jnp.where(m, 1.0, 0.0).astype(f32)
    r, c = io((C, C), 0), io((C, C), 1)
    return dict(causal=r >= c, strict=r > c, sel_row=one(io((C, HEAD), 1) == 0))


def _chunk_fn(kc, diff, q, k, v, gc, bB, S0):
    i = 0 if diff else 1
    mm, mm_nt, mm_tn = _BDOT[NN][i], _BDOT[NT][i], _BDOT[TN][i]
    tri = _tri_inv_multi_vjp if diff else _tri_inv_multi
    each = lambda f, *ls: tuple(f(*a) for a in zip(*ls))
    gcol = each(_col_bcast if diff else _col_bcast_impl, gc)
    grow = each(lambda g: (_row_bcast if diff else _row_bcast_impl)(kc['sel_row'], g), gc)
    glc = each(_LAST_C[i ^ 1], gc)
    glh = each(_LAST_H[i ^ 1], gc)
    decay = each(lambda a, b: jnp.where(kc['causal'], jnp.exp(jnp.where(kc['causal'], a - b, 0.0)), 0.0), gcol, grow)
    kb = each(lambda a, b: a * b, k, bB)
    vb = each(lambda a, b: a * b, v, bB)
    egc = each(jnp.exp, gc)
    kk = each(mm_nt, kb, k)
    T = tri(each(lambda a, d: jnp.where(kc['strict'], a * d, 0.0), kk, decay))
    u = each(mm, T, vb)
    w = each(mm, T, each(lambda a, b: a * b, kb, egc))
    qs = each(lambda a: a * (HEAD ** -0.5), q)
    qk = each(mm_nt, qs, k)
    attn = each(lambda a, d: jnp.where(kc['causal'], a * d, 0.0), qk, decay)
    wS = each(mm, w, S0)
    qS = each(mm, each(lambda a, b: a * b, qs, egc), S0)
    v_new = each(lambda a, b: a - b, u, wS)
    o = each(lambda a, b: a + b, qS, each(mm, attn, v_new))
    kdec = each(lambda a, gl, g: a * jnp.exp(gl - g), k, glc, gc)
    S1 = each(lambda s, gl, kv: s * jnp.exp(gl) + kv, S0, glh, each(mm_tn, kdec, v_new))
    return o, S1


def _heads_per_block(H):
    return 8 if H % 8 == 0 else (4 if H % 4 == 0 else 1)


def dn_chunk_fwd(qkv, gB, bB, name, comm=None):
    _, S, W = qkv.shape
    H, C = W // HEAD, DN_CHUNK
    N, HB = S // C, _heads_per_block(H)

    def body(q_ref, k_ref, v_ref, g_ref, b_ref, o_ref, st_ref, s_scr):
        @pl.when(pl.program_id(1) == 0)
        def _():
            s_scr[...] = jnp.zeros_like(s_scr)

        kc = _chunk_consts()
        sls = [slice(hh * HEAD, (hh + 1) * HEAD) for hh in range(HB)]
        heads = lambda ref: tuple(ref[:, sl] for sl in sls)
        S0 = tuple(s_scr[hh] for hh in range(HB))
        for hh in range(HB):
            st_ref[hh] = S0[hh]
        o, S1 = _chunk_fn(kc, False, heads(q_ref), heads(k_ref), heads(v_ref), tuple(g_ref[hh] for hh in range(HB)),
                          tuple(b_ref[hh] for hh in range(HB)), S0)
        for hh in range(HB):
            o_ref[:, sls[hh]] = o[hh]
            s_scr[hh] = S1[hh]

    part = lambda p: pl.BlockSpec((None, C, HB * HEAD), lambda hb, n: (p, n, hb))
    gate = pl.BlockSpec((HB, C, HEAD), lambda hb, n: (hb, n, 0))
    return hosted_call(
        body, comm, _grid_steps(H // HB, N), name=name, grid=(H // HB, N),
        in_specs=[part(0), part(1), part(2), gate, gate],
        out_specs=(pl.BlockSpec((C, HB * HEAD), lambda hb, n: (n, hb)),
                   pl.BlockSpec((None, HB, HEAD, HEAD), lambda hb, n: (n, hb, 0, 0))),
        out_shape=(SDS((S, W), f32), SDS((N, H, HEAD, HEAD), f32)),
        scratch_shapes=[pltpu.VMEM((HB, HEAD, HEAD), f32)], args=(qkv, qkv, qkv, gB, bB))


def dn_chunk_bwd(qkv, gB, bB, states, do, name, comm=None):
    _, S, W = qkv.shape
    H, C = W // HEAD, DN_CHUNK
    N, HB = S // C, _heads_per_block(H)

    def body(q_ref, k_ref, v_ref, g_ref, b_ref, st_ref, do_ref, dqkv_ref, dg_ref, db_ref, ds_scr):
        @pl.when(pl.program_id(1) == 0)
        def _():
            ds_scr[...] = jnp.zeros_like(ds_scr)

        kc = _chunk_consts()
        sls = [slice(hh * HEAD, (hh + 1) * HEAD) for hh in range(HB)]
        heads = lambda ref: tuple(ref[:, sl] for sl in sls)
        lead = lambda ref: tuple(ref[hh] for hh in range(HB))
        _, vjp = jax.vjp(functools.partial(_chunk_fn, kc, True), heads(q_ref), heads(k_ref), heads(v_ref),
                         lead(g_ref), lead(b_ref), lead(st_ref))
        dq, dk, dv, dg, db, dS0 = vjp((heads(do_ref), lead(ds_scr)))
        for hh in range(HB):
            dqkv_ref[0, :, sls[hh]] = dq[hh]
            dqkv_ref[1, :, sls[hh]] = dk[hh]
            dqkv_ref[2, :, sls[hh]] = dv[hh]
            dg_ref[hh] = dg[hh]
            db_ref[hh] = db[hh]
            ds_scr[hh] = dS0[hh]

    rev = lambda n: N - 1 - n
    part = lambda p: pl.BlockSpec((None, C, HB * HEAD), lambda hb, n: (p, rev(n), hb))
    gate = pl.BlockSpec((HB, C, HEAD), lambda hb, n: (hb, rev(n), 0))
    return hosted_call(
        body, comm, _grid_steps(H // HB, N), name=name, grid=(H // HB, N),
        in_specs=[part(0), part(1), part(2), gate, gate,
                  pl.BlockSpec((None, HB, HEAD, HEAD), lambda hb, n: (rev(n), hb, 0, 0)),
                  pl.BlockSpec((C, HB * HEAD), lambda hb, n: (rev(n), hb))],
        out_specs=(pl.BlockSpec((3, C, HB * HEAD), lambda hb, n: (0, rev(n), hb)), gate, gate),
        out_shape=(SDS((3, S, W), f32), SDS((H, S, HEAD), f32), SDS((H, S, HEAD), f32)),
        scratch_shapes=[pltpu.VMEM((HB, HEAD, HEAD), f32)], args=(qkv, qkv, qkv, gB, bB, states, do))


def _gate_norm(o, z, ng):
    return _rms(o, ng) * _silu(z)


def dn_out(o, proj, ng, wout, x1, g3, name):
    S, W = o.shape
    D = x1.shape[1]
    nh = W // HEAD
    tm = _row_tile(S, 256)

    def body(o_ref, z_ref, ng_ref, w_ref, x_ref, g_ref, xo_ref, m_ref, og_ref):
        for h in range(nh):
            sl = slice(h * HEAD, (h + 1) * HEAD)
            og_ref[:, sl] = _gate_norm(o_ref[:, sl], z_ref[:, sl], ng_ref[...]).astype(og_ref.dtype)
        m = _mm(og_ref[...], w_ref[...])
        m_ref[...] = m
        xo_ref[...] = x_ref[...] + _rms(m, g_ref[...])

    rw = pl.BlockSpec((tm, W), lambda i: (i, 0))
    rd = pl.BlockSpec((tm, D), lambda i: (i, 0))
    return pl.pallas_call(
        body, name=name, grid=(S // tm,),
        in_specs=[rw, pl.BlockSpec((tm, W), lambda i: (i, 3)), pl.BlockSpec((1, HEAD), lambda i: (0, 0)),
                  pl.BlockSpec((W, D), lambda i: (0, 0)), rd, pl.BlockSpec((1, D), lambda i: (0, 0))],
        out_specs=(rd, rd, rw),
        out_shape=(SDS((S, D), f32), SDS((S, D), f32), SDS((S, W), MXU_DTYPE)), compiler_params=_params(1),
    )(o, proj, ng, wout, x1, g3)


def dn_out_bwd(dxo, m, g3, o, proj, ng, wout, name):
    S, W = o.shape
    D = m.shape[1]
    nh = W // HEAD
    tm = _row_tile(S, 256)

    def body(dxo_ref, m_ref, g_ref, o_ref, z_ref, ng_ref, w_ref, dm_ref, do_ref, dz_ref, dng_ref, dg_ref):
        @pl.when(pl.program_id(0) == 0)
        def _():
            dng_ref[...] = jnp.zeros_like(dng_ref)
            dg_ref[...] = jnp.zeros_like(dg_ref)

        dm, dg = _rms_bwd(m_ref[...], g_ref[...], dxo_ref[...])
        dg_ref[...] += dg
        dmc = dm.astype(dm_ref.dtype)
        dm_ref[...] = dmc
        dog = _mm_nt(dmc, w_ref[...])
        for h in range(nh):
            sl = slice(h * HEAD, (h + 1) * HEAD)
            _, vjp = jax.vjp(_gate_norm, o_ref[:, sl], z_ref[:, sl], ng_ref[...])
            do, dz, dng = vjp(dog[:, sl])
            do_ref[:, sl] = do
            dz_ref[:, sl] = dz.astype(dz_ref.dtype)
            dng_ref[...] += dng

    rw = pl.BlockSpec((tm, W), lambda i: (i, 0))
    rd = pl.BlockSpec((tm, D), lambda i: (i, 0))
    vd = pl.BlockSpec((1, D), lambda i: (0, 0))
    vh = pl.BlockSpec((1, HEAD), lambda i: (0, 0))
    return pl.pallas_call(
        body, name=name, grid=(S // tm,),
        in_specs=[rd, rd, vd, rw, pl.BlockSpec((tm, W), lambda i: (i, 3)), vh, pl.BlockSpec((W, D), lambda i: (0, 0))],
        out_specs=(rd, rw, rw, vh, vd),
        out_shape=(SDS((S, D), MXU_DTYPE), SDS((S, W), f32), SDS((S, W), MXU_DTYPE), SDS((1, HEAD), f32),
                   SDS((1, D), f32)),
        compiler_params=_params(1),
    )(dxo, m, g3, o, proj, ng, wout)


def _sg_stage1(pu, pv, bu, bv, lg, lb):
    u = _gelu(pu + bu)
    t = _gelu(pv + bv)
    tc = t - jnp.mean(t, axis=-1, keepdims=True)
    v = tc * lax.rsqrt(jnp.mean(tc * tc, axis=-1, keepdims=True) + LN_EPS) * lg + lb
    return u, v


def _causal_mask(n):
    return lax.broadcasted_iota(jnp.int32, (n, n), 0) >= lax.broadcasted_iota(jnp.int32, (n, n), 1)


def sg_mid(pre, b_in, ln_g, ln_b, w_s, bsT, wout, x1, g3, name):
    S = pre.shape[0]
    E, D = ln_g.shape[1], x1.shape[1]
    G, CH = SG_GROUPS, SG_CHUNK
    Cg = E // G
    tm = _row_tile(S, 256)

    def body(pu_ref, pv_ref, bu_ref, bv_ref, lg_ref, lb_ref, ws_ref, bs_ref, w_ref, x_ref, g_ref,
             xo_ref, m_ref, gt_ref):
        u, v = _sg_stage1(pu_ref[...], pv_ref[...], bu_ref[...], bv_ref[...], lg_ref[...], lb_ref[...])
        mask = _causal_mask(CH)
        for g in range(G):
            wc = _c(jnp.where(mask, ws_ref[g], 0.0))
            bcol = bs_ref[:, g:g + 1]
            cs = slice(g * Cg, (g + 1) * Cg)
            for ch in range(tm // CH):
                rs = slice(ch * CH, (ch + 1) * CH)
                mixed = _mm(wc, _c(v[rs, cs])) + bcol
                gt_ref[rs, cs] = (u[rs, cs] * mixed).astype(gt_ref.dtype)
        m = _mm(gt_ref[...], w_ref[...])
        m_ref[...] = m
        xo_ref[...] = x_ref[...] + _rms(m, g_ref[...])

    half = lambda p: pl.BlockSpec((tm, E), lambda i: (i, p))
    vhalf = lambda p: pl.BlockSpec((1, E), lambda i: (0, p))
    ve = pl.BlockSpec((1, E), lambda i: (0, 0))
    rd = pl.BlockSpec((tm, D), lambda i: (i, 0))
    return pl.pallas_call(
        body, name=name, grid=(S // tm,),
        in_specs=[half(0), half(1), vhalf(0), vhalf(1), ve, ve, pl.BlockSpec((G, CH, CH), lambda i: (0, 0, 0)),
                  pl.BlockSpec((CH, G), lambda i: (0, 0)), pl.BlockSpec((E, D), lambda i: (0, 0)), rd,
                  pl.BlockSpec((1, D), lambda i: (0, 0))],
        out_specs=(rd, rd, pl.BlockSpec((tm, E), lambda i: (i, 0))),
        out_shape=(SDS((S, D), f32), SDS((S, D), f32), SDS((S, E), MXU_DTYPE)), compiler_params=_params(1),
    )(pre, pre, b_in, b_in, ln_g, ln_b, w_s, bsT, wout, x1, g3)


def sg_mid_bwd(dxo, m, g3, pre, b_in, ln_g, ln_b, w_s, bsT, wout, name):
    S = pre.shape[0]
    E, D = ln_g.shape[1], m.shape[1]
    G, CH = SG_GROUPS, SG_CHUNK
    Cg = E // G
    tm = _row_tile(S, 256)

    def body(dxo_ref, m_ref, g_ref, pu_ref, pv_ref, bu_ref, bv_ref, lg_ref, lb_ref, ws_ref, bs_ref, w_ref,
             dm_ref, dpre_ref, dbin_ref, dlg_ref, dlb_ref, dws_ref, dbs_ref, dg_ref, du_scr, dv_scr):
        @pl.when(pl.program_id(0) == 0)
        def _():
            for r in (dbin_ref, dlg_ref, dlb_ref, dws_ref, dbs_ref, dg_ref):
                r[...] = jnp.zeros_like(r)

        dm, dg = _rms_bwd(m_ref[...], g_ref[...], dxo_ref[...])
        dg_ref[...] += dg
        dmc = dm.astype(dm_ref.dtype)
        dm_ref[...] = dmc
        dgated = _mm_nt(dmc, w_ref[...])
        (u, v), vjp1 = jax.vjp(_sg_stage1, pu_ref[...], pv_ref[...], bu_ref[...], bv_ref[...], lg_ref[...],
                               lb_ref[...])
        mask = _causal_mask(CH)
        lane = lax.broadcasted_iota(jnp.int32, (CH, CH), 1)
        for g in range(G):
            wc = _c(jnp.where(mask, ws_ref[g], 0.0))
            bcol = bs_ref[:, g:g + 1]
            cs = slice(g * Cg, (g + 1) * Cg)
            dws = jnp.zeros((CH, CH), f32)
            dbs = jnp.zeros((CH, 1), f32)
            for ch in range(tm // CH):
                rs = slice(ch * CH, (ch + 1) * CH)
                vs = _c(v[rs, cs])
                mixed = _mm(wc, vs) + bcol
                dgt = dgated[rs, cs]
                du_scr[rs, cs] = dgt * mixed
                dmixed = dgt * u[rs, cs]
                dmc2 = _c(dmixed)
                dv_scr[rs, cs] = _mm_tn(wc, dmc2)
                dws = dws + _mm_nt(dmc2, vs)
                dbs = dbs + jnp.sum(dmixed, axis=1, keepdims=True)
            dws_ref[g] += jnp.where(mask, dws, 0.0)
            dbs_ref[...] += jnp.where(lane == g, jnp.broadcast_to(dbs, (CH, CH)), 0.0)
        dpu, dpv, dbu, dbv, dlg, dlb = vjp1((du_scr[...], dv_scr[...]))
        dpre_ref[:, :E] = dpu.astype(dpre_ref.dtype)
        dpre_ref[:, E:] = dpv.astype(dpre_ref.dtype)
        dbin_ref[:, :E] += dbu
        dbin_ref[:, E:] += dbv
        dlg_ref[...] += dlg
        dlb_ref[...] += dlb

    half = lambda p: pl.BlockSpec((tm, E), lambda i: (i, p))
    vhalf = lambda p: pl.BlockSpec((1, E), lambda i: (0, p))
    ve = pl.BlockSpec((1, E), lambda i: (0, 0))
    rd = pl.BlockSpec((tm, D), lambda i: (i, 0))
    vd = pl.BlockSpec((1, D), lambda i: (0, 0))
    wsb = pl.BlockSpec((G, CH, CH), lambda i: (0, 0, 0))
    return pl.pallas_call(
        body, name=name, grid=(S // tm,),
        in_specs=[rd, rd, vd, half(0), half(1), vhalf(0), vhalf(1), ve, ve, wsb,
                  pl.BlockSpec((CH, G), lambda i: (0, 0)), pl.BlockSpec((E, D), lambda i: (0, 0))],
        out_specs=(rd, pl.BlockSpec((tm, 2 * E), lambda i: (i, 0)), pl.BlockSpec((1, 2 * E), lambda i: (0, 0)), ve, ve,
                   wsb, pl.BlockSpec((CH, CH), lambda i: (0, 0)), vd),
        out_shape=(SDS((S, D), MXU_DTYPE), SDS((S, 2 * E), MXU_DTYPE), SDS((1, 2 * E), f32), SDS((1, E), f32),
                   SDS((1, E), f32), SDS((G, CH, CH), f32), SDS((CH, CH), f32), SDS((1, D), f32)),
        scratch_shapes=[pltpu.VMEM((tm, E), f32), pltpu.VMEM((tm, E), f32)], compiler_params=_params(1),
    )(dxo, m, g3, pre, pre, b_in, b_in, ln_g, ln_b, w_s, bsT, wout)


def loss_head(y, target, name):
    S, D = y.shape
    tm = _row_tile(S, 512)

    def body(y_ref, t_ref, l_ref, d_ref):
        @pl.when(pl.program_id(0) == 0)
        def _():
            l_ref[...] = jnp.zeros_like(l_ref)

        e = y_ref[...] - t_ref[...]
        d_ref[...] = e * (1.0 / D)
        l_ref[...] += jnp.sum(e * e) * (0.5 / D)

    row = pl.BlockSpec((tm, D), lambda i: (i, 0))
    return pl.pallas_call(
        body, name=name, grid=(S // tm,), in_specs=[row, row],
        out_specs=(pl.BlockSpec((1, HEAD), lambda i: (0, 0)), row),
        out_shape=(SDS((1, HEAD), f32), SDS((S, D), f32)), compiler_params=_params(1),
    )(y, target)


def sum_slots(r, name):
    _, R, C = r.shape
    tr = _row_tile(R, 648 if R % 648 == 0 else R)

    def body(r_ref, o_ref):
        acc = r_ref[0].astype(f32)
        for s in range(1, N_DEV):
            acc = acc + r_ref[s].astype(f32)
        o_ref[...] = acc

    return pl.pallas_call(
        body, name=name, grid=(R // tr,), in_specs=[pl.BlockSpec((N_DEV, tr, C), lambda i: (0, i, 0))],
        out_specs=pl.BlockSpec((tr, C), lambda i: (i, 0)), out_shape=SDS((R, C), f32), compiler_params=_params(1),
    )(r)


def _adam_math(w, g, m, v):
    m = ADAM_B1 * m + (1.0 - ADAM_B1) * g
    v = ADAM_B2 * v + (1.0 - ADAM_B2) * (g * g)
    m_hat = m / (1.0 - ADAM_B1 ** ADAM_STEP)
    v_hat = v / (1.0 - ADAM_B2 ** ADAM_STEP)
    delta = -ADAM_LR * (m_hat / (jnp.sqrt(v_hat) + ADAM_EPS) + ADAM_WD * w)
    return delta, m, v


def adam_slots(w, rs, m, v, name, tr):
    R, C = w.shape
    tr = _row_tile(min(r.shape[1] for r in rs), tr)
    blocks = [r.shape[1] // tr for r in rs]
    starts = [sum(blocks[:k]) for k in range(len(rs))]
    assert sum(blocks) * tr == R

    def body(w_ref, *refs):
        r_refs, (m_ref, v_ref, g_ref, d_ref, mo_ref, vo_ref) = refs[:len(rs)], refs[len(rs):]
        i = pl.program_id(0)
        for k, r_ref in enumerate(r_refs):
            @pl.when((i >= starts[k]) & (i < starts[k] + blocks[k]))
            def _():
                g = r_ref[0].astype(f32)
                for s in range(1, N_DEV):
                    g = g + r_ref[s].astype(f32)
                g_ref[...] = g

        d_ref[...], mo_ref[...], vo_ref[...] = _adam_math(w_ref[...], g_ref[...], m_ref[...], v_ref[...])

    row = pl.BlockSpec((tr, C), lambda i: (i, 0))
    piece = lambda k: pl.BlockSpec((N_DEV, tr, C), lambda i: (0, jnp.clip(i - starts[k], 0, blocks[k] - 1), 0))
    return pl.pallas_call(
        body, name=name, grid=(R // tr,), in_specs=[row] + [piece(k) for k in range(len(rs))] + [row, row],
        out_specs=(row, row, row, row), out_shape=tuple(SDS((R, C), f32) for _ in range(4)),
        compiler_params=_params(1),
    )(w, *rs, m, v)


def adam_small(w, g, m, v, name):
    def body(w_ref, g_ref, m_ref, v_ref, d_ref, mo_ref, vo_ref):
        d_ref[...], mo_ref[...], vo_ref[...] = _adam_math(w_ref[...], g_ref[...], m_ref[...], v_ref[...])

    return pl.pallas_call(body, name=name, out_shape=tuple(SDS(w.shape, f32) for _ in range(3)))(w, g, m, v)


def _pack_rows(parts):
    rows, offs, r = [], [], 0
    for p in parts:
        flat = p.reshape(-1)
        n = -(-flat.shape[0] // HEAD)
        flat = jnp.pad(flat, (0, n * HEAD - flat.shape[0]))
        rows.append(flat.reshape(n, HEAD))
        offs.append((r, n))
        r += n
    pad = (-r) % 8
    if pad:
        rows.append(jnp.zeros((pad, HEAD), f32))
    return jnp.concatenate(rows, axis=0), offs


def kernel(x, norm_g, ffn_w_gate, ffn_w_up, ffn_w_down, dn_w_in, dn_conv_w, dn_a_log, dn_dt_bias, dn_norm_g, dn_w_out, sg_w_in, sg_b_in, sg_ln_g, sg_ln_b, sg_w_s, sg_b_s, sg_w_out, loss_target, m_norm_g, m_ffn_w_gate, m_ffn_w_up, m_ffn_w_down, m_dn_w_in, m_dn_conv_w, m_dn_a_log, m_dn_dt_bias, m_dn_norm_g, m_dn_w_out, m_sg_w_in, m_sg_b_in, m_sg_ln_g, m_sg_ln_b, m_sg_w_s, m_sg_b_s, m_sg_w_out, v_norm_g, v_ffn_w_gate, v_ffn_w_up, v_ffn_w_down, v_dn_w_in, v_dn_conv_w, v_dn_a_log, v_dn_dt_bias, v_dn_norm_g, v_dn_w_out, v_sg_w_in, v_sg_b_in, v_sg_ln_g, v_sg_ln_b, v_sg_w_s, v_sg_b_s, v_sg_w_out):
    weights = dict(norm_g=norm_g, ffn_w_gate=ffn_w_gate, ffn_w_up=ffn_w_up, ffn_w_down=ffn_w_down, dn_w_in=dn_w_in,
                   dn_conv_w=dn_conv_w, dn_a_log=dn_a_log, dn_dt_bias=dn_dt_bias, dn_norm_g=dn_norm_g,
                   dn_w_out=dn_w_out, sg_w_in=sg_w_in, sg_b_in=sg_b_in, sg_ln_g=sg_ln_g, sg_ln_b=sg_ln_b,
                   sg_w_s=sg_w_s, sg_b_s=sg_b_s, sg_w_out=sg_w_out)
    mom_m = dict(norm_g=m_norm_g, ffn_w_gate=m_ffn_w_gate, ffn_w_up=m_ffn_w_up, ffn_w_down=m_ffn_w_down,
                 dn_w_in=m_dn_w_in, dn_conv_w=m_dn_conv_w, dn_a_log=m_dn_a_log, dn_dt_bias=m_dn_dt_bias,
                 dn_norm_g=m_dn_norm_g, dn_w_out=m_dn_w_out, sg_w_in=m_sg_w_in, sg_b_in=m_sg_b_in,
                 sg_ln_g=m_sg_ln_g, sg_ln_b=m_sg_ln_b, sg_w_s=m_sg_w_s, sg_b_s=m_sg_b_s, sg_w_out=m_sg_w_out)
    mom_v = dict(norm_g=v_norm_g, ffn_w_gate=v_ffn_w_gate, ffn_w_up=v_ffn_w_up, ffn_w_down=v_ffn_w_down,
                 dn_w_in=v_dn_w_in, dn_conv_w=v_dn_conv_w, dn_a_log=v_dn_a_log, dn_dt_bias=v_dn_dt_bias,
                 dn_norm_g=v_dn_norm_g, dn_w_out=v_dn_w_out, sg_w_in=v_sg_w_in, sg_b_in=v_sg_b_in,
                 sg_ln_g=v_sg_ln_g, sg_ln_b=v_sg_ln_b, sg_w_s=v_sg_w_s, sg_b_s=v_sg_b_s, sg_w_out=v_sg_w_out)
    order = list(weights)

    xs = x[0]
    S, D = xs.shape
    F8 = ffn_w_gate.shape[-1]
    depth = norm_g.shape[0]
    W = dn_w_out.shape[1] * N_DEV
    H = W // HEAD
    E = sg_ln_g.shape[1] * N_DEV
    G, CH = sg_w_s.shape[1], sg_w_s.shape[2]
    c8 = dn_w_in.shape[2]
    me = _slot(lax.axis_index("x"), lax.axis_index("y"), lax.axis_index("c"))

    assert depth == 2
    small_in, small_offs = _pack_rows([norm_g, dn_conv_w, sg_b_in, sg_ln_g, sg_ln_b])
    wg0a, wu0a, wd0a, small_all = all_gather_multi(
        [_c(ffn_w_gate[0, 0]), _c(ffn_w_up[0, 0]), _c(ffn_w_down[0, 0]), small_in], name="gather_first")
    gather_l0 = Comm("gather", [_c(dn_w_in[0]), _c(dn_w_out[0]), _c(ffn_w_gate[0, 1]), _c(ffn_w_up[0, 1]),
                                _c(ffn_w_down[0, 1])])
    gather_l1 = Comm("gather", [_c(ffn_w_gate[1]), _c(ffn_w_up[1]), _c(ffn_w_down[1]), _c(sg_w_in[0]),
                                _c(sg_w_out[0])])
    per = N_DEV // FFN_SLABS
    wide_cols = lambda w: jnp.transpose(w.reshape(FFN_SLABS, per, D, F8), (0, 2, 1, 3)).reshape(FFN_SLABS, D, per * F8)
    wide = lambda g, u, d: (wide_cols(g), wide_cols(u), d.reshape(FFN_SLABS, per * F8, D))
    shard_cols = lambda dw: jnp.transpose(dw.reshape(FFN_SLABS, D, per, F8), (0, 2, 1, 3)).reshape(N_DEV, D, F8)
    shard_rows = lambda dw: dw.reshape(N_DEV, F8, D)
    ffn_w = {(0, 0): wide(wg0a, wu0a, wd0a)}

    def small_piece(i, shard_shape):
        r0, n = small_offs[i]
        sz = math.prod(shard_shape)
        return small_all[:, r0:r0 + n, :].reshape(N_DEV, n * HEAD)[:, :sz].reshape((N_DEV,) + tuple(shard_shape))

    ng_full = jnp.moveaxis(small_piece(0, norm_g.shape), 0, 2).reshape(depth, 6, D)
    conv_full = jnp.moveaxis(small_piece(1, dn_conv_w.shape[1:]), 0, 1).reshape(CONV_K, 3 * W)
    bin_full = small_piece(2, sg_b_in.shape[1:]).reshape(1, 2 * E)
    lng_full = small_piece(3, sg_ln_g.shape[1:]).reshape(1, E)
    lnb_full = small_piece(4, sg_ln_b.shape[1:]).reshape(1, E)
    alog_b = jnp.broadcast_to(dn_a_log.reshape(H, 1, 1), (H, 1, HEAD))
    dtb_b = jnp.broadcast_to(dn_dt_bias.reshape(H, 1, 1), (H, 1, HEAD))
    bsT = sg_b_s[0].T
    gvec = lambda l, k: ng_full[l, k].reshape(1, D)

    saved = []
    cur = xs
    for l in range(depth):
        sv = {}
        sv['x0'] = cur
        (cur, sv['hA'], sv['aA'], sv['bA'], sv['yA']), got = ffn_fwd(
            cur, gvec(l, 0), gvec(l, 1), *ffn_w[l, 0], name=f"ffn_fwd_{l}a", comm=gather_l0 if l == 0 else None)
        sv['x1'] = cur
        if l == 0:
            dnin_all, dnout_all, wg0b, wu0b, wd0b = got
            ffn_w[0, 1] = wide(wg0b, wu0b, wd0b)
            dn_win = jnp.moveaxis(dnin_all, 0, 1).reshape(D, N_DEV * c8)
            dn_wmain = dn_win[:, :4 * W]
            dn_wba = jnp.pad(dn_win[:, 4 * W:], ((0, 0), (0, HEAD - 2 * H)))
            dn_wout = dnout_all.reshape(W, D)
            sv['hM'], sv['proj'], sv['pba'] = rms_mm(cur, gvec(l, 2), dn_wmain, dn_wba, name=f"dn_in_{l}")
            sv['qkv'] = dn_prep(sv['proj'], conv_full, name=f"dn_prep_{l}")
            sv['beta'], sv['g'] = dn_gates(sv['pba'], alog_b, dtb_b, name=f"dn_gates_{l}")
            (sv['o'], sv['states']), got = dn_chunk_fwd(sv['qkv'], sv['g'], sv['beta'], name=f"dn_chunk_{l}",
                                                        comm=gather_l1)
            wg1, wu1, wd1, sgin_all, sgout_all = got
            for ab in range(2):
                ffn_w[1, ab] = wide(wg1[:, ab], wu1[:, ab], wd1[:, ab])
            sg_win = jnp.moveaxis(sgin_all, 0, 1).reshape(D, 2 * E)
            sg_wout = sgout_all.reshape(E, D)
            cur, sv['m'], sv['og'] = dn_out(sv['o'], sv['proj'], dn_norm_g, dn_wout, cur, gvec(l, 3), name=f"dn_out_{l}")
        else:
            sv['hM'], sv['pre'] = rms_mm(cur, gvec(l, 2), sg_win, None, name=f"sg_in_{l}")
            cur, sv['m'], sv['gated'] = sg_mid(sv['pre'], bin_full, lng_full, lnb_full, sg_w_s[0], bsT, sg_wout, cur,
                                               gvec(l, 3), name=f"sg_mid_{l}")
        sv['x2'] = cur
        (cur, sv['hB'], sv['aB'], sv['bB'], sv['yB']), _ = ffn_fwd(cur, gvec(l, 4), gvec(l, 5), *ffn_w[l, 1],
                                                                   name=f"ffn_fwd_{l}b")
        saved.append(sv)

    loss_blk, dcur = loss_head(cur, loss_target[0], name="loss_head")
    loss = lax.psum(loss_blk[0, 0], ("x", "y", "c"))

    dng = [[None] * 6 for _ in range(depth)]
    ffn_dw = {}
    grads, slots = {}, {}

    def ffn_backward(l, ab, dcur, comm=None):
        sv, s = saved[l], 'AB'[ab]
        (dcur, da, db, dy, dng[l][4 * ab], dng[l][4 * ab + 1]), got = ffn_bwd_dx(
            dcur, sv['x2' if ab else 'x0'], sv['y' + s], sv['a' + s], sv['b' + s], gvec(l, 4 * ab), gvec(l, 4 * ab + 1),
            *ffn_w[l, ab], name=f"ffn_bwd_{l}{'ab'[ab]}", comm=comm)
        dg, du, dd = ffn_bwd_dw(sv['h' + s], dy, sv['a' + s], sv['b' + s], da, db, name=f"ffn_dw_{l}{'ab'[ab]}")
        ffn_dw[l, ab] = (shard_cols(dg), shard_cols(du), shard_rows(dd))
        return dcur, got

    sv = saved[1]
    dcur, _ = ffn_backward(1, 1, dcur)
    dm, dpre, grads['sg_b_in'], grads['sg_ln_g'], grads['sg_ln_b'], grads['sg_w_s'], dbs, dng[1][3] = sg_mid_bwd(
        dcur, sv['m'], gvec(1, 3), sv['pre'], bin_full, lng_full, lnb_full, sg_w_s[0], bsT, sg_wout, name="sg_mid_bwd_1")
    grads['sg_b_s'] = dbs[:, :G].T
    dsg_wout = tn_mm(sv['gated'], dm, name="sg_wout_dw_1").reshape(N_DEV, E // N_DEV, D)
    dsg_win = tn_mm(sv['hM'], dpre, name="sg_win_dw_1", tn=2 * E // N_DEV, slot_major=True)
    dcur, dng[1][2] = mm_bwd_dx(dcur, sv['x1'], gvec(1, 2), dpre, sg_win, None, None, name="sg_in_bwd_1")
    dcur, _ = ffn_backward(1, 0, dcur)
    sv = saved[0]
    dcur, _ = ffn_backward(0, 1, dcur)
    dm, do, dz, grads['dn_norm_g'], dng[0][3] = dn_out_bwd(dcur, sv['m'], gvec(0, 3), sv['o'], sv['proj'], dn_norm_g,
                                                          dn_wout, name="dn_out_bwd_0")
    ddn_wout = tn_mm(sv['og'], dm, name="dn_wout_dw_0").reshape(N_DEV, W // N_DEV, D)
    (dqkv, dgB, dbB), got = dn_chunk_bwd(sv['qkv'], sv['g'], sv['beta'], sv['states'], do, name="dn_chunk_bwd_0",
                                         comm=Comm("exchange", [*ffn_dw[1, 0], *ffn_dw[1, 1], dsg_win, dsg_wout]))
    l1a, l1b, slots['sg_w_in'], slots['sg_w_out'] = got[0:3], got[3:6], [got[6]], [got[7]]
    dpba, dal, ddt = dn_gates_bwd(sv['pba'], alog_b, dtb_b, dbB, dgB, name="dn_gates_bwd_0")
    grads['dn_a_log'] = dal[:, 0, 0].reshape(1, H)
    grads['dn_dt_bias'] = ddt[:, 0, 0].reshape(1, H)
    (dproj, grads['dn_conv_w']), l0b = dn_prep_bwd(sv['proj'], conv_full, dqkv, dz, name="dn_prep_bwd_0",
                                                   comm=Comm("exchange", list(ffn_dw[0, 1])))
    dw_main = tn_mm(sv['hM'], dproj, name="dn_win_dw_0")
    dw_ba = tn_mm(sv['hM'], dpba, name="dn_wba_dw_0", tn=HEAD)
    dw_in = jnp.concatenate([dw_main, dw_ba[:, :2 * H]], axis=1)
    ddn_win = jnp.moveaxis(dw_in.reshape(D, N_DEV, c8), 1, 0)
    dcur, dng[0][2] = mm_bwd_dx(dcur, sv['x1'], gvec(0, 2), dproj, dn_wmain, dpba, dn_wba, name="dn_in_bwd_0")
    dcur, got = ffn_backward(0, 0, dcur, comm=Comm("exchange", [ddn_win, ddn_wout]))
    slots['dn_w_in'], slots['dn_w_out'] = [got[0]], [got[1]]
    grad_x = dcur[None]
    l0a = exchange_slots(list(ffn_dw[0, 0]), name="exchange_last")
    for i, nm in enumerate(['ffn_w_gate', 'ffn_w_up', 'ffn_w_down']):
        slots[nm] = [l0a[i], l0b[i], l1a[i], l1b[i]]
    big_names = ['ffn_w_gate', 'ffn_w_up', 'ffn_w_down', 'dn_w_in', 'dn_w_out', 'sg_w_in', 'sg_w_out']
    slots = [slots[nm] for nm in big_names]

    dng_full = jnp.stack([jnp.concatenate(r, axis=0) for r in dng], axis=0)
    small_names = ['norm_g', 'dn_conv_w', 'sg_b_in', 'sg_ln_g', 'sg_ln_b', 'sg_w_s', 'sg_b_s', 'dn_a_log',
                   'dn_dt_bias', 'dn_norm_g']
    small_parts = [dng_full, grads['dn_conv_w'], grads['sg_b_in'], grads['sg_ln_g'], grads['sg_ln_b'],
                   grads['sg_w_s'], grads['sg_b_s'], grads['dn_a_log'], grads['dn_dt_bias'], grads['dn_norm_g']]
    small_pack, offs = _pack_rows(small_parts)
    (small_slots,) = all_gather_multi([small_pack], name="gather_small_grads")
    small_sum = sum_slots(small_slots, name="sum_small_grads")

    def small_grad(i):
        r0, n = offs[i]
        p = small_parts[i]
        return small_sum[r0:r0 + n].reshape(-1)[:p.size].reshape(p.shape)

    def my_shard(full, axis, like):
        n = full.shape[axis] // N_DEV
        return lax.dynamic_slice_in_dim(full, me * n, n, axis).reshape(like.shape)

    g_small = {
        'norm_g': my_shard(small_grad(0), 2, norm_g),
        'dn_conv_w': my_shard(small_grad(1), 1, dn_conv_w),
        'sg_b_in': my_shard(small_grad(2), 1, sg_b_in),
        'sg_ln_g': my_shard(small_grad(3), 1, sg_ln_g),
        'sg_ln_b': my_shard(small_grad(4), 1, sg_ln_b),
        'sg_w_s': small_grad(5).reshape(sg_w_s.shape),
        'sg_b_s': small_grad(6).reshape(sg_b_s.shape),
        'dn_a_log': small_grad(7).reshape(dn_a_log.shape),
        'dn_dt_bias': small_grad(8).reshape(dn_dt_bias.shape),
        'dn_norm_g': small_grad(9).reshape(dn_norm_g.shape),
    }

    out_g, out_d, out_m, out_v = {}, {}, {}, {}
    for nm, r in zip(big_names, slots):
        w = weights[nm]
        cols = w.shape[-1]
        rows = w.size // cols
        tr = {'ffn_w_gate': 512, 'ffn_w_up': 512, 'ffn_w_down': F8 // 2, 'dn_w_in': 256, 'sg_w_in': 256}.get(nm, rows)
        pieces = [p.reshape(N_DEV, -1, cols) for p in r]
        g, d, m2, v2 = adam_slots(w.reshape(rows, cols), pieces, mom_m[nm].reshape(rows, cols),
                                  mom_v[nm].reshape(rows, cols), name=f"adam_{nm}", tr=tr)
        out_g[nm], out_d[nm], out_m[nm], out_v[nm] = (t.reshape(w.shape) for t in (g, d, m2, v2))
    for nm in small_names:
        w = weights[nm]
        cols = w.shape[-1]
        rows = w.size // cols
        two = lambda t: t.reshape(rows, cols)
        d, m2, v2 = adam_small(two(w), two(g_small[nm]), two(mom_m[nm]), two(mom_v[nm]), name=f"adam_{nm}")
        out_g[nm] = g_small[nm]
        out_d[nm], out_m[nm], out_v[nm] = (t.reshape(w.shape) for t in (d, m2, v2))

    return (loss, grad_x, *[out_g[n] for n in order], *[out_d[n] for n in order], *[out_m[n] for n in order],
            *[out_v[n] for n in order])
```

```python
import functools
import math

import jax
import jax.numpy as jnp
from jax import lax
from jax.experimental import pallas as pl
from jax.experimental.pallas import tpu as pltpu

f32 = jnp.float32
MXU_DTYPE = jnp.bfloat16
N_DEV = 8
RMS_EPS = 1e-6
LN_EPS = 1e-5
L2_EPS = 1e-6
HEAD = 128
DN_CHUNK = 64
SG_CHUNK = 128
SG_GROUPS = 8
CONV_K = 4
ADAM_LR, ADAM_B1, ADAM_B2, ADAM_EPS, ADAM_WD, ADAM_STEP = 0.001, 0.9, 0.999, 1e-08, 0.01, 10
VMEM_LIMIT = 56 * 1024 * 1024
FFN_ROWS_FWD, FFN_ROWS_BWD, FFN_ROWS_DW = 1024, 512, 1024
PROJ_ROWS, TN_ROWS = 1024, 2048
FFN_SLABS = 4
SDS = jax.ShapeDtypeStruct
HIGHEST = lax.Precision.HIGHEST
MESH = pl.DeviceIdType.MESH


def _params(n_grid):
    return pltpu.CompilerParams(dimension_semantics=("arbitrary",) * n_grid, vmem_limit_bytes=VMEM_LIMIT)


def _row_tile(s, want):
    t = min(s, want)
    assert s % t == 0, (s, t)
    return t


def _rms(x, g):
    return x * lax.rsqrt(jnp.mean(x * x, axis=-1, keepdims=True) + RMS_EPS) * g


def _rms_bwd(x, g, dy):
    _, vjp = jax.vjp(_rms, x, g)
    return vjp(dy)


def _silu(a):
    return a * jax.nn.sigmoid(a)


def _gelu(x):
    return 0.5 * x * (1.0 + lax.erf(x * 0.7071067811865476))


def _mm(a, b):
    return lax.dot_general(a, b, (((1,), (0,)), ((), ())), preferred_element_type=f32)


def _mm_nt(a, b):
    return lax.dot_general(a, b, (((1,), (1,)), ((), ())), preferred_element_type=f32)


def _mm_tn(a, b):
    return lax.dot_general(a, b, (((0,), (0,)), ((), ())), preferred_element_type=f32)


def _c(x):
    return x.astype(MXU_DTYPE)


def _split(a):
    hi = a.astype(MXU_DTYPE)
    lo = (a - hi.astype(f32)).astype(MXU_DTYPE)
    return hi, lo


def _dot3(a, b, dims):
    ah, al = _split(a)
    bh, bl = _split(b)
    d = lambda p, q: lax.dot_general(p, q, (dims, ((), ())), preferred_element_type=f32)
    return d(ah, bh) + (d(ah, bl) + d(al, bh))


NN, NT, TN = ((1,), (0,)), ((1,), (1,)), ((0,), (0,))


def _slot(px, py, pc):
    return 4 * px + 2 * py + pc


def all_gather_multi(arrs, name):
    return Comm("gather", arrs).alone(name)


def exchange_slots(arrs, name):
    return Comm("exchange", arrs).alone(name)


class Comm:
    def __init__(self, kind, arrs):
        self.kind, self.arrs, self.n = kind, list(arrs), len(arrs)
        hbm = pl.BlockSpec(memory_space=pltpu.HBM)
        self.in_specs = [hbm] * self.n
        self.out_specs = [hbm] * self.n
        lead = (N_DEV,) if kind == "gather" else ()
        self.out_shape = [SDS(lead + tuple(a.shape), a.dtype) for a in self.arrs]
        self.scratch = [pltpu.SemaphoreType.DMA((self.n, 7)), pltpu.SemaphoreType.DMA((self.n, 7)),
                        pltpu.SemaphoreType.DMA((self.n,))]

    def phase(self, p, ins, outs, sems):
        (self._gather if self.kind == "gather" else self._exchange)(p, ins, outs, sems)

    def _gather(self, p, ins, outs, sems):
        send_sems, recv_sems, local_sems = sems
        x, y, c = lax.axis_index("x"), lax.axis_index("y"), lax.axis_index("c")
        me, sibling = (x, y, c), (x, y, 1 - c)
        chips = [(1 - x, y), (x, 1 - y), (1 - x, 1 - y)]

        def copy(a, k, block, to, src=None):
            dst = outs[a].at[_slot(*block)]
            return pltpu.make_async_remote_copy(
                src_ref=dst if src is None else src, dst_ref=dst, send_sem=send_sems.at[a, k],
                recv_sem=recv_sems.at[a, k], device_id=to, device_id_type=MESH)

        mine = [pltpu.make_async_copy(ins[a], outs[a].at[_slot(*me)], local_sems.at[a]) for a in range(self.n)]
        first = [[copy(a, 0, me, sibling, src=ins[a])] +
                 [copy(a, 1 + j, me, (*chip, c), src=ins[a]) for j, chip in enumerate(chips)] for a in range(self.n)]
        passed = [[copy(a, 4 + j, (*chip, c), sibling) for j, chip in enumerate(chips)] for a in range(self.n)]
        if p == 0:
            for a in range(self.n):
                mine[a].start()
            for a in range(self.n):
                for cp in first[a]:
                    cp.start()
        elif p == 1:
            for a in range(self.n):
                for j, chip in enumerate(chips):
                    copy(a, 1 + j, (*chip, c), me).wait_recv()
                    passed[a][j].start()
        else:
            for a in range(self.n):
                copy(a, 0, sibling, me).wait_recv()
                for j, chip in enumerate(chips):
                    copy(a, 4 + j, (*chip, 1 - c), me).wait_recv()
            for a in range(self.n):
                for cp in first[a] + passed[a]:
                    cp.wait_send()
                mine[a].wait()

    def _exchange(self, p, ins, outs, sems):
        send_sems, recv_sems, local_sems = sems
        x, y, c = lax.axis_index("x"), lax.axis_index("y"), lax.axis_index("c")
        me = _slot(x, y, c)
        peers = [(x ^ (k >> 2), y ^ ((k >> 1) & 1), c ^ (k & 1)) for k in range(1, N_DEV)]

        def copy(a, k):
            peer = peers[k - 1]
            return pltpu.make_async_remote_copy(
                src_ref=ins[a].at[_slot(*peer)], dst_ref=outs[a].at[me], send_sem=send_sems.at[a, k - 1],
                recv_sem=recv_sems.at[a, k - 1], device_id=peer, device_id_type=MESH)

        def landed(a, k):
            peer = peers[k - 1]
            return pltpu.make_async_remote_copy(
                src_ref=ins[a].at[me], dst_ref=outs[a].at[_slot(*peer)], send_sem=send_sems.at[a, k - 1],
                recv_sem=recv_sems.at[a, k - 1], device_id=peer, device_id_type=MESH)

        local = [pltpu.make_async_copy(ins[a].at[me], outs[a].at[me], local_sems.at[a]) for a in range(self.n)]
        order = [6, 7, 2, 3, 4, 5, 1]
        if p == 0:
            for a in range(self.n):
                local[a].start()
            for a in range(self.n):
                for k in order:
                    copy(a, k).start()
        elif p == 2:
            for a in range(self.n):
                for k in order:
                    copy(a, k).wait_send()
                    landed(a, k).wait_recv()
                local[a].wait()

    def alone(self, name):
        n = self.n

        def body(*refs):
            for p in range(3):
                self.phase(p, refs[:n], refs[n:2 * n], refs[2 * n:])

        return pl.pallas_call(body, name=name, out_shape=tuple(self.out_shape), in_specs=self.in_specs,
                              out_specs=tuple(self.out_specs), scratch_shapes=self.scratch)(*self.arrs)


def hosted_call(body, comm, steps, *, name, grid, in_specs, out_specs, out_shape, scratch_shapes, args):
    if comm is None:
        outs = pl.pallas_call(body, name=name, grid=grid, in_specs=in_specs, out_specs=tuple(out_specs),
                              out_shape=tuple(out_shape), scratch_shapes=scratch_shapes,
                              compiler_params=_params(len(grid)))(*args)
        return outs, None
    ni, no, ns, cn = len(in_specs), len(out_specs), len(scratch_shapes), comm.n

    def both(*refs):
        h_in, c_in = refs[:ni], refs[ni:ni + cn]
        h_out, c_out = refs[ni + cn:ni + cn + no], refs[ni + cn + no:ni + 2 * cn + no]
        h_scr, c_scr = refs[ni + 2 * cn + no:ni + 2 * cn + no + ns], refs[ni + 2 * cn + no + ns:]
        when = steps()
        pl.when(when[0])(lambda: comm.phase(0, c_in, c_out, c_scr))
        body(*h_in, *h_out, *h_scr)
        pl.when(when[1])(lambda: comm.phase(1, c_in, c_out, c_scr))
        pl.when(when[2])(lambda: comm.phase(2, c_in, c_out, c_scr))

    outs = pl.pallas_call(
        both, name=name, grid=grid, in_specs=list(in_specs) + comm.in_specs,
        out_specs=tuple(out_specs) + tuple(comm.out_specs), out_shape=tuple(out_shape) + tuple(comm.out_shape),
        scratch_shapes=list(scratch_shapes) + comm.scratch, compiler_params=_params(len(grid)),
    )(*args, *comm.arrs)
    return outs[:no], outs[no:]


def _grid_steps(n_outer, n_inner=1):
    total = n_outer * n_inner

    def steps():
        t = pl.program_id(0) * n_inner + (pl.program_id(1) if n_inner > 1 else 0)
        return t == 0, t == (total * 5) // 8, t == total - 1
    return steps


def ffn_fwd(x, gpre, gpost, wg, wu, wd, name, comm=None):
    S, D = x.shape
    nj, F8 = wg.shape[0], wg.shape[-1]
    tm = _row_tile(S, FFN_ROWS_FWD)

    def body(x_ref, gpre_ref, gpost_ref, wg_ref, wu_ref, wd_ref, xo_ref, h_ref, p_ref, q_ref, t_ref, y_ref):
        j = pl.program_id(1)

        @pl.when(j == 0)
        def _():
            h_ref[...] = _rms(x_ref[...], gpre_ref[...]).astype(h_ref.dtype)
            y_ref[...] = jnp.zeros_like(y_ref)

        h = h_ref[...]
        a = _mm(h, wg_ref[...])
        b = _mm(h, wu_ref[...])
        s = jax.nn.sigmoid(a)
        q = a * s
        p_ref[...] = (b * (s + q * (1.0 - s))).astype(p_ref.dtype)
        q_ref[...] = q.astype(q_ref.dtype)
        t = (q * b).astype(t_ref.dtype)
        t_ref[...] = t
        y_ref[...] += _mm(t, wd_ref[...])

        @pl.when(j == nj - 1)
        def _():
            xo_ref[...] = x_ref[...] + 0.5 * _rms(y_ref[...], gpost_ref[...])

    row = pl.BlockSpec((tm, D), lambda i, j: (i, 0))
    vec = pl.BlockSpec((1, D), lambda i, j: (0, 0))
    wcol = pl.BlockSpec((None, D, F8), lambda i, j: (j, 0, 0))
    wrow = pl.BlockSpec((None, F8, D), lambda i, j: (j, 0, 0))
    hid = pl.BlockSpec((None, tm, F8), lambda i, j: (j, i, 0))
    return hosted_call(
        body, comm, _grid_steps(S // tm, nj), name=name, grid=(S // tm, nj),
        in_specs=[row, vec, vec, wcol, wcol, wrow],
        out_specs=(row, row, hid, hid, hid, row),
        out_shape=(SDS((S, D), f32), SDS((S, D), MXU_DTYPE), SDS((nj, S, F8), MXU_DTYPE),
                   SDS((nj, S, F8), MXU_DTYPE), SDS((nj, S, F8), MXU_DTYPE), SDS((S, D), f32)),
        scratch_shapes=[], args=(x, gpre, gpost, wg, wu, wd))


def ffn_bwd_dx(dxo, x, y, p, q, gpre, gpost, wg, wu, wd, name, comm=None):
    S, D = x.shape
    nj, F8 = wg.shape[0], wg.shape[-1]
    tm = _row_tile(S, FFN_ROWS_BWD)

    def body(dxo_ref, x_ref, y_ref, p_ref, q_ref, gpre_ref, gpost_ref, wg_ref, wu_ref, wd_ref,
             dx_ref, da_ref, db_ref, dy_ref, dgpre_ref, dgpost_ref, dh_ref):
        i, j = pl.program_id(0), pl.program_id(1)

        @pl.when(j == 0)
        def _():
            @pl.when(i == 0)
            def _():
                dgpre_ref[...] = jnp.zeros_like(dgpre_ref)
                dgpost_ref[...] = jnp.zeros_like(dgpost_ref)

            dy, dg = _rms_bwd(y_ref[...], gpost_ref[...], 0.5 * dxo_ref[...])
            dy_ref[...] = dy.astype(dy_ref.dtype)
            dgpost_ref[...] += dg
            dh_ref[...] = jnp.zeros_like(dh_ref)

        half = tm // 2 if tm % 16 == 0 else tm
        for r0 in range(0, tm, half):
            rs = slice(r0, r0 + half)
            dt = _mm_nt(dy_ref[rs, :], wd_ref[...])
            da = (dt * p_ref[rs, :].astype(f32)).astype(da_ref.dtype)
            db = (dt * q_ref[rs, :].astype(f32)).astype(db_ref.dtype)
            da_ref[rs, :] = da
            db_ref[rs, :] = db
            dh_ref[rs, :] += _mm_nt(da, wg_ref[...]) + _mm_nt(db, wu_ref[...])

        @pl.when(j == nj - 1)
        def _():
            dxx, dg = _rms_bwd(x_ref[...], gpre_ref[...], dh_ref[...])
            dx_ref[...] = dxo_ref[...] + dxx
            dgpre_ref[...] += dg

    row = pl.BlockSpec((tm, D), lambda i, j: (i, 0))
    vec = pl.BlockSpec((1, D), lambda i, j: (0, 0))
    wcol = pl.BlockSpec((None, D, F8), lambda i, j: (j, 0, 0))
    wrow = pl.BlockSpec((None, F8, D), lambda i, j: (j, 0, 0))
    hid = pl.BlockSpec((None, tm, F8), lambda i, j: (j, i, 0))
    return hosted_call(
        body, comm, _grid_steps(S // tm, nj), name=name, grid=(S // tm, nj),
        in_specs=[row, row, row, hid, hid, vec, vec, wcol, wcol, wrow],
        out_specs=(row, hid, hid, row, vec, vec),
        out_shape=(SDS((S, D), f32), SDS((nj, S, F8), MXU_DTYPE), SDS((nj, S, F8), MXU_DTYPE),
                   SDS((S, D), MXU_DTYPE), SDS((1, D), f32), SDS((1, D), f32)),
        scratch_shapes=[pltpu.VMEM((tm, D), f32)], args=(dxo, x, y, p, q, gpre, gpost, wg, wu, wd))


def ffn_bwd_dw(h, dy, t, da, db, name):
    S, D = h.shape
    F8 = t.shape[-1]
    tm = _row_tile(S, FFN_ROWS_DW)
    ni = S // tm

    def body(h_ref, dy_ref, t_ref, da_ref, db_ref, dwg_ref, dwu_ref, dwd_ref, accg, accu, accd):
        i = pl.program_id(1)

        @pl.when(i == 0)
        def _():
            accg[...] = jnp.zeros_like(accg)
            accu[...] = jnp.zeros_like(accu)
            accd[...] = jnp.zeros_like(accd)

        hh = h_ref[...]
        accg[...] += _mm_tn(hh, da_ref[...])
        accu[...] += _mm_tn(hh, db_ref[...])
        accd[...] += _mm_tn(t_ref[...], dy_ref[...])

        @pl.when(i == ni - 1)
        def _():
            dwg_ref[...] = accg[...].astype(dwg_ref.dtype)
            dwu_ref[...] = accu[...].astype(dwu_ref.dtype)
            dwd_ref[...] = accd[...].astype(dwd_ref.dtype)

    row = pl.BlockSpec((tm, D), lambda j, i: (i, 0))
    hid = pl.BlockSpec((None, tm, F8), lambda j, i: (j, i, 0))
    wcol = pl.BlockSpec((None, D, F8), lambda j, i: (j, 0, 0))
    wrow = pl.BlockSpec((None, F8, D), lambda j, i: (j, 0, 0))
    return pl.pallas_call(
        body, name=name, grid=(t.shape[0], ni),
        in_specs=[row, row, hid, hid, hid],
        out_specs=(wcol, wcol, wrow),
        out_shape=(SDS((t.shape[0], D, F8), MXU_DTYPE), SDS((t.shape[0], D, F8), MXU_DTYPE),
                   SDS((t.shape[0], F8, D), MXU_DTYPE)),
        scratch_shapes=[pltpu.VMEM((D, F8), f32), pltpu.VMEM((D, F8), f32), pltpu.VMEM((F8, D), f32)],
        compiler_params=_params(2),
    )(h, dy, t, da, db)


def rms_mm(x, g, w, w2, name, tn=1024):
    S, D = x.shape
    N = w.shape[1]
    tm = _row_tile(S, PROJ_ROWS)
    tn = _row_tile(N, tn)
    has2 = w2 is not None

    def body(*refs):
        if has2:
            x_ref, g_ref, w_ref, w2_ref, h_ref, o_ref, o2_ref = refs
        else:
            x_ref, g_ref, w_ref, h_ref, o_ref = refs
        j = pl.program_id(1)

        @pl.when(j == 0)
        def _():
            h = _rms(x_ref[...], g_ref[...]).astype(h_ref.dtype)
            h_ref[...] = h
            if has2:
                o2_ref[...] = _mm(h, w2_ref[...])

        o_ref[...] = _mm(h_ref[...], w_ref[...])

    row = pl.BlockSpec((tm, D), lambda i, j: (i, 0))
    in_specs = [row, pl.BlockSpec((1, D), lambda i, j: (0, 0)), pl.BlockSpec((D, tn), lambda i, j: (0, j))]
    out_specs = [row, pl.BlockSpec((tm, tn), lambda i, j: (i, j))]
    out_shape = [SDS((S, D), MXU_DTYPE), SDS((S, N), f32)]
    args = [x, g, w]
    if has2:
        in_specs.append(pl.BlockSpec((D, w2.shape[1]), lambda i, j: (0, 0)))
        out_specs.append(pl.BlockSpec((tm, w2.shape[1]), lambda i, j: (i, 0)))
        out_shape.append(SDS((S, w2.shape[1]), f32))
        args.append(w2)
    return pl.pallas_call(
        body, name=name, grid=(S // tm, N // tn), in_specs=in_specs, out_specs=tuple(out_specs),
        out_shape=tuple(out_shape), compiler_params=_params(2),
    )(*args)


def mm_bwd_dx(dres, x, g, dy, w, dy2, w2, name, tk=1024):
    S, D = x.shape
    K = dy.shape[1]
    tm = _row_tile(S, PROJ_ROWS)
    tk = _row_tile(K, tk)
    nk = K // tk
    has2 = dy2 is not None

    def body(*refs):
        if has2:
            dres_ref, x_ref, g_ref, dy_ref, w_ref, dy2_ref, w2_ref, dx_ref, dg_ref, dh_ref = refs
        else:
            dres_ref, x_ref, g_ref, dy_ref, w_ref, dx_ref, dg_ref, dh_ref = refs
        i, k = pl.program_id(0), pl.program_id(1)

        @pl.when(k == 0)
        def _():
            @pl.when(i == 0)
            def _():
                dg_ref[...] = jnp.zeros_like(dg_ref)

            if has2:
                dh_ref[...] = _mm_nt(dy2_ref[...], w2_ref[...])
            else:
                dh_ref[...] = jnp.zeros_like(dh_ref)

        dh_ref[...] += _mm_nt(dy_ref[...], w_ref[...])

        @pl.when(k == nk - 1)
        def _():
            dxx, dg = _rms_bwd(x_ref[...], g_ref[...], dh_ref[...])
            dx_ref[...] = dres_ref[...] + dxx
            dg_ref[...] += dg

    row = pl.BlockSpec((tm, D), lambda i, k: (i, 0))
    vec = pl.BlockSpec((1, D), lambda i, k: (0, 0))
    in_specs = [row, row, vec, pl.BlockSpec((tm, tk), lambda i, k: (i, k)), pl.BlockSpec((D, tk), lambda i, k: (0, k))]
    args = [dres, x, g, dy, w]
    if has2:
        in_specs += [pl.BlockSpec((tm, dy2.shape[1]), lambda i, k: (i, 0)),
                     pl.BlockSpec((D, w2.shape[1]), lambda i, k: (0, 0))]
        args += [dy2, w2]
    return pl.pallas_call(
        body, name=name, grid=(S // tm, nk), in_specs=in_specs, out_specs=(row, vec),
        out_shape=(SDS((S, D), f32), SDS((1, D), f32)),
        scratch_shapes=[pltpu.VMEM((tm, D), f32)], compiler_params=_params(2),
    )(*args)


def tn_mm(a, b, name, tn=512, slot_major=False):
    S, K1 = a.shape
    N = b.shape[1]
    tm = _row_tile(S, TN_ROWS)
    tn = _row_tile(N, tn)
    ni = S // tm

    def body(a_ref, b_ref, o_ref, acc):
        i = pl.program_id(1)

        @pl.when(i == 0)
        def _():
            acc[...] = jnp.zeros_like(acc)

        acc[...] += _mm_tn(a_ref[...], b_ref[...])

        @pl.when(i == ni - 1)
        def _():
            o_ref[...] = acc[...].astype(o_ref.dtype)

    if slot_major:
        out_spec, out_shape = pl.BlockSpec((None, K1, tn), lambda j, i: (j, 0, 0)), SDS((N // tn, K1, tn), MXU_DTYPE)
    else:
        out_spec, out_shape = pl.BlockSpec((K1, tn), lambda j, i: (0, j)), SDS((K1, N), MXU_DTYPE)
    return pl.pallas_call(
        body, name=name, grid=(N // tn, ni),
        in_specs=[pl.BlockSpec((tm, K1), lambda j, i: (i, 0)), pl.BlockSpec((tm, tn), lambda j, i: (i, j))],
        out_specs=out_spec, out_shape=out_shape,
        scratch_shapes=[pltpu.VMEM((K1, tn), f32)], compiler_params=_params(2),
    )(a, b)


CONV_ROWS = 512


def _shift_down(cur, prev8, s):
    r = pltpu.roll(cur, s, 0)
    row = lax.broadcasted_iota(jnp.int32, (8, cur.shape[1]), 0)
    top = jnp.where(row < s, pltpu.roll(prev8, s, 0), r[0:8])
    return jnp.concatenate([top, r[8:]], axis=0)


def _shift_up(cur, next8, s):
    n = cur.shape[0]
    r = pltpu.roll(cur, n - s, 0)
    row = lax.broadcasted_iota(jnp.int32, (8, cur.shape[1]), 0)
    bot = jnp.where(row >= 8 - s, pltpu.roll(next8, 8 - s, 0), r[n - 8:])
    return jnp.concatenate([r[:n - 8], bot], axis=0)


def _conv_taps(cur, prev8):
    return [_shift_down(cur, prev8, 3), _shift_down(cur, prev8, 2), _shift_down(cur, prev8, 1), cur]


def _act_qk(c):
    a = _silu(c)
    return a * lax.rsqrt(jnp.sum(a * a, axis=-1, keepdims=True) + L2_EPS)


def dn_prep(proj, conv_w, name):
    S = proj.shape[0]
    W = conv_w.shape[1] // 3
    nh = W // HEAD
    R = _row_tile(S, CONV_ROWS)

    def body(p_ref, w_ref, o_ref):
        j = pl.program_id(0)
        w = w_ref[...]

        def rows(r, prev8):
            cur = p_ref[pl.ds(r, R), :]
            taps = _conv_taps(cur, prev8)
            cv = taps[0] * w[0:1] + taps[1] * w[1:2] + taps[2] * w[2:3] + taps[3] * w[3:4]

            @pl.when(j < 2 * nh)
            def _():
                o_ref[pl.ds(r, R), :] = _act_qk(cv)

            @pl.when(j >= 2 * nh)
            def _():
                o_ref[pl.ds(r, R), :] = _silu(cv)

        rows(0, jnp.zeros((8, HEAD), f32))

        @pl.loop(1, S // R)
        def _(t):
            r = pl.multiple_of(t * R, R)
            rows(r, p_ref[pl.ds(r - 8, 8), :])

    return pl.pallas_call(
        body, name=name, grid=(3 * nh,),
        in_specs=[pl.BlockSpec((S, HEAD), lambda j: (0, j)), pl.BlockSpec((CONV_K, HEAD), lambda j: (0, j))],
        out_specs=pl.BlockSpec((None, S, HEAD), lambda j: (j // nh, 0, j % nh)),
        out_shape=SDS((3, S, W), f32), compiler_params=_params(1),
    )(proj, conv_w)


def dn_prep_bwd(proj, conv_w, dqkv, dz, name, comm=None):
    S = proj.shape[0]
    W = conv_w.shape[1] // 3
    nh = W // HEAD
    nq = 3 * nh
    R = _row_tile(S, CONV_ROWS)
    nr = S // R

    def body(p_ref, w_ref, dq_ref, dz_ref, dp_ref, dw_ref, dc_ref):
        j = pl.program_id(0)

        @pl.when(j >= nq)
        def _():
            dp_ref[...] = dz_ref[...].astype(dp_ref.dtype)

        @pl.when(j < nq)
        def _():
            w = w_ref[...]
            dw_ref[...] = jnp.zeros_like(dw_ref)

            def rows(r, prev8):
                cur = p_ref[pl.ds(r, R), :]
                taps = _conv_taps(cur, prev8)
                cv = taps[0] * w[0:1] + taps[1] * w[1:2] + taps[2] * w[2:3] + taps[3] * w[3:4]
                dn = dq_ref[pl.ds(r, R), :]

                @pl.when(j < 2 * nh)
                def _():
                    dc_ref[pl.ds(r, R), :] = jax.vjp(_act_qk, cv)[1](dn)[0]

                @pl.when(j >= 2 * nh)
                def _():
                    dc_ref[pl.ds(r, R), :] = jax.vjp(_silu, cv)[1](dn)[0]

                dc = dc_ref[pl.ds(r, R), :]
                dw_ref[...] += jnp.concatenate(
                    [jnp.sum(dc * taps[q], axis=0, keepdims=True) for q in range(CONV_K)], axis=0)

            rows(0, jnp.zeros((8, HEAD), f32))

            @pl.loop(1, nr)
            def _(t):
                r = pl.multiple_of(t * R, R)
                rows(r, p_ref[pl.ds(r - 8, 8), :])

            def back(r, next8):
                dc = dc_ref[pl.ds(r, R), :]
                dx = dc * w[3:4]
                for s in (1, 2, 3):
                    dx = dx + _shift_up(dc, next8, s) * w[3 - s:4 - s]
                dp_ref[pl.ds(r, R), :] = dx.astype(dp_ref.dtype)

            @pl.loop(0, nr - 1)
            def _(t):
                r = pl.multiple_of(t * R, R)
                back(r, dc_ref[pl.ds(r + R, 8), :])

            back((nr - 1) * R, jnp.zeros((8, HEAD), f32))

    clamp = lambda j: jnp.minimum(j, nq - 1)
    return hosted_call(
        body, comm, _grid_steps(4 * nh), name=name, grid=(4 * nh,),
        in_specs=[pl.BlockSpec((S, HEAD), lambda j: (0, clamp(j))),
                  pl.BlockSpec((CONV_K, HEAD), lambda j: (0, clamp(j))),
                  pl.BlockSpec((None, S, HEAD), lambda j: (clamp(j) // nh, 0, clamp(j) % nh)),
                  pl.BlockSpec((S, HEAD), lambda j: (0, jnp.maximum(j - nq, 0)))],
        out_specs=(pl.BlockSpec((S, HEAD), lambda j: (0, j)), pl.BlockSpec((CONV_K, HEAD), lambda j: (0, clamp(j)))),
        out_shape=(SDS((S, 4 * W), MXU_DTYPE), SDS((CONV_K, 3 * W), f32)),
        scratch_shapes=[pltpu.VMEM((S, HEAD), f32)], args=(proj, conv_w, dqkv, dz))


def _lane_pick(x, lane):
    sel = lax.broadcasted_iota(jnp.int32, x.shape, 1) == lane
    return jnp.broadcast_to(jnp.sum(jnp.where(sel, x, 0.0), axis=1, keepdims=True), x.shape)


CUM_ROWS = 256


def _sel_mm(m01, x):
    m = _c(m01)
    d = lambda p: lax.dot_general(m, p, (NN, ((), ())), preferred_element_type=f32)
    h1, h2, h3 = _pieces3(x)
    return (d(h1) + d(h2)) + d(h3)


def _chunk_cumsum_matrix(n, transpose):
    r, c = lax.broadcasted_iota(jnp.int32, (n, n), 0), lax.broadcasted_iota(jnp.int32, (n, n), 1)
    sh = int(math.log2(DN_CHUNK))
    same = (r >> sh) == (c >> sh)
    return jnp.where(same & ((r <= c) if transpose else (r >= c)), 1.0, 0.0).astype(f32)


def _gates_by_lane(H, p, al, dt):
    lane = lax.broadcasted_iota(jnp.int32, p.shape, 1)
    g = -jnp.exp(al) * jax.nn.softplus(p + dt)
    return jnp.where(lane < H, jax.nn.sigmoid(p), jnp.where(lane < 2 * H, g, 0.0))


def dn_gates(pba, al, dt, H, name):
    S = pba.shape[0]
    R = _row_tile(S, CUM_ROWS)

    def body(p_ref, al_ref, dt_ref, o_ref):
        raw = _gates_by_lane(H, p_ref[...], al_ref[...], dt_ref[...])
        lane = lax.broadcasted_iota(jnp.int32, raw.shape, 1)
        o_ref[...] = jnp.where(lane < H, raw, _sel_mm(_chunk_cumsum_matrix(R, False), raw))

    blk = pl.BlockSpec((R, HEAD), lambda i: (i, 0))
    par = pl.BlockSpec((1, HEAD), lambda i: (0, 0))
    return pl.pallas_call(body, name=name, grid=(S // R,), in_specs=[blk, par, par], out_specs=blk,
                          out_shape=SDS((S, HEAD), f32), compiler_params=_params(1))(pba, al, dt)


def dn_gates_bwd(pba, al, dt, dgates, H, name):
    S = pba.shape[0]
    R = _row_tile(S, CUM_ROWS)

    def body(p_ref, al_ref, dt_ref, dg_ref, dp_ref, dal_ref, ddt_ref):
        @pl.when(pl.program_id(0) == 0)
        def _():
            dal_ref[...] = jnp.zeros_like(dal_ref)
            ddt_ref[...] = jnp.zeros_like(ddt_ref)

        d = dg_ref[...]
        lane = lax.broadcasted_iota(jnp.int32, d.shape, 1)
        d = jnp.where(lane < H, d, _sel_mm(_chunk_cumsum_matrix(R, True), d))
        _, vjp = jax.vjp(functools.partial(_gates_by_lane, H), p_ref[...], al_ref[...], dt_ref[...])
        dp, dal, ddt = vjp(d)
        dp_ref[...] = dp.astype(dp_ref.dtype)
        dal_ref[...] += dal
        ddt_ref[...] += ddt

    blk = pl.BlockSpec((R, HEAD), lambda i: (i, 0))
    par = pl.BlockSpec((1, HEAD), lambda i: (0, 0))
    return pl.pallas_call(
        body, name=name, grid=(S // R,), in_specs=[blk, par, par, blk], out_specs=(blk, par, par),
        out_shape=(SDS((S, HEAD), MXU_DTYPE), SDS((1, HEAD), f32), SDS((1, HEAD), f32)), compiler_params=_params(1),
    )(pba, al, dt, dgates)


def _bdot(dims):
    back = {NN: ((NT, 'gb'), (TN, 'ag')), NT: ((NN, 'gb'), (TN, 'ga')), TN: ((NT, 'bg'), (NN, 'ag'))}[dims]
    d = lambda p, q, dm: lax.dot_general(_c(p), _c(q), (dm, ((), ())), preferred_element_type=f32)

    @jax.custom_vjp
    def f(a, b):
        return d(a, b, dims)

    def fwd(a, b):
        return d(a, b, dims), (a, b)

    def bwd(res, g):
        v = {'a': res[0], 'b': res[1], 'g': g}
        (da_dims, da_ops), (db_dims, db_ops) = back
        return d(v[da_ops[0]], v[da_ops[1]], da_dims), d(v[db_ops[0]], v[db_ops[1]], db_dims)

    f.defvjp(fwd, bwd)
    return f, lambda a, b: d(a, b, dims)


_BDOT = {dims: _bdot(dims) for dims in (NN, NT, TN)}


def _tri_inv_multi(Ls):
    n = Ls[0].shape[0]
    eye = jnp.where(lax.broadcasted_iota(jnp.int32, (n, n), 0) == lax.broadcasted_iota(jnp.int32, (n, n), 1), 1.0, 0.0)
    P = tuple(-L for L in Ls)
    T = tuple(eye + p for p in P)
    for _ in range(int(math.log2(n)) - 1):
        P = tuple(_dot3(p, p, NN) for p in P)
        T = tuple(t + _dot3(t, p, NN) for t, p in zip(T, P))
    return T


@jax.custom_vjp
def _tri_inv_multi_vjp(Ls):
    return _tri_inv_multi(Ls)


def _tri_inv_fwd(Ls):
    T = _tri_inv_multi(Ls)
    return T, T


def _tri_inv_bwd(T, dT):
    X = tuple(_dot3(d, t, NT) for d, t in zip(dT, T))
    return (tuple(-_dot3(t, x, TN) for t, x in zip(T, X)),)


_tri_inv_multi_vjp.defvjp(_tri_inv_fwd, _tri_inv_bwd)


def _pieces3(x):
    h1 = x.astype(MXU_DTYPE)
    r1 = x - h1.astype(f32)
    h2 = r1.astype(MXU_DTYPE)
    return h1, h2, (r1 - h2.astype(f32)).astype(MXU_DTYPE)


def _row_bcast_impl(sel_row, gc):
    s = _c(sel_row)
    d = lambda p: lax.dot_general(s, p, (NT, ((), ())), preferred_element_type=f32)
    h1, h2, h3 = _pieces3(gc)
    return (d(h1) + d(h2)) + d(h3)


def _row_bcast_bwd(sel_row, d):
    s = _c(sel_row)
    hi, lo = _split(d)
    t = lambda p: lax.dot_general(p, s, (TN, ((), ())), preferred_element_type=f32)
    return jnp.zeros_like(sel_row), t(hi) + t(lo)


_row_bcast = jax.custom_vjp(_row_bcast_impl)
_row_bcast.defvjp(lambda sel_row, gc: (_row_bcast_impl(sel_row, gc), sel_row), _row_bcast_bwd)


def _col_bcast_impl(gc):
    return gc[:, :DN_CHUNK]


def _col_bcast_bwd(_, d):
    return (jnp.broadcast_to(jnp.sum(d, axis=1, keepdims=True) * (1.0 / HEAD), (d.shape[0], HEAD)),)


_col_bcast = jax.custom_vjp(_col_bcast_impl)
_col_bcast.defvjp(lambda gc: (_col_bcast_impl(gc), None), _col_bcast_bwd)


def _last_row_bcast(n):
    def impl(gc):
        return jnp.broadcast_to(gc[DN_CHUNK - 1:DN_CHUNK, :], (n, HEAD))

    def bwd(_, d):
        row = lax.broadcasted_iota(jnp.int32, (DN_CHUNK, HEAD), 0)
        return (jnp.where(row == DN_CHUNK - 1, jnp.sum(d, axis=0, keepdims=True), 0.0),)

    f = jax.custom_vjp(impl)
    f.defvjp(lambda gc: (impl(gc), None), bwd)
    return impl, f


_LAST_C, _LAST_H = _last_row_bcast(DN_CHUNK), _last_row_bcast(HEAD)


def _chunk_consts():
    C = DN_CHUNK
    io = lambda shape, ax: lax.broadcasted_iota(jnp.int32, shape, ax)
    one = lambda m: jnp.where(m, 1.0, 0.0).astype(f32)
    r, c = io((C, C), 0), io((C, C), 1)
    return dict(causal=r >= c, strict=r > c, sel_row=one(io((C, HEAD), 1) == 0))


def _chunk_fn(kc, diff, q, k, v, gc, bB, S0):
    i = 0 if diff else 1
    mm, mm_nt, mm_tn = _BDOT[NN][i], _BDOT[NT][i], _BDOT[TN][i]
    tri = _tri_inv_multi_vjp if diff else _tri_inv_multi
    each = lambda f, *ls: tuple(f(*a) for a in zip(*ls))
    gcol = each(_col_bcast if diff else _col_bcast_impl, gc)
    grow = each(lambda g: (_row_bcast if diff else _row_bcast_impl)(kc['sel_row'], g), gc)
    glc = each(_LAST_C[i ^ 1], gc)
    glh = each(_LAST_H[i ^ 1], gc)
    decay = each(lambda a, b: jnp.where(kc['causal'], jnp.exp(jnp.where(kc['causal'], a - b, 0.0)), 0.0), gcol, grow)
    kb = each(lambda a, b: a * b, k, bB)
    vb = each(lambda a, b: a * b, v, bB)
    egc = each(jnp.exp, gc)
    kk = each(mm_nt, kb, k)
    T = tri(each(lambda a, d: jnp.where(kc['strict'], a * d, 0.0), kk, decay))
    u = each(mm, T, vb)
    w = each(mm, T, each(lambda a, b: a * b, kb, egc))
    qs = each(lambda a: a * (HEAD ** -0.5), q)
    qk = each(mm_nt, qs, k)
    attn = each(lambda a, d: jnp.where(kc['causal'], a * d, 0.0), qk, decay)
    wS = each(mm, w, S0)
    qS = each(mm, each(lambda a, b: a * b, qs, egc), S0)
    v_new = each(lambda a, b: a - b, u, wS)
    o = each(lambda a, b: a + b, qS, each(mm, attn, v_new))
    kdec = each(lambda a, gl, g: a * jnp.exp(gl - g), k, glc, gc)
    S1 = each(lambda s, gl, kv: s * jnp.exp(gl) + kv, S0, glh, each(mm_tn, kdec, v_new))
    return o, S1


def _chunks_per_step(N):
    return 2 if N % 2 == 0 else 1


def _heads_per_block(H):
    return 8 if H % 8 == 0 else (4 if H % 4 == 0 else 1)


def dn_chunk_fwd(qkv, gates, name, comm=None):
    _, S, W = qkv.shape
    H, C = W // HEAD, DN_CHUNK
    N, HB = S // C, _heads_per_block(H)
    assert HB == H
    CPS = _chunks_per_step(N)

    def body(q_ref, k_ref, v_ref, g_ref, o_ref, st_ref, s_scr):
        @pl.when(pl.program_id(1) == 0)
        def _():
            s_scr[...] = jnp.zeros_like(s_scr)

        kc = _chunk_consts()
        sls = [slice(hh * HEAD, (hh + 1) * HEAD) for hh in range(HB)]
        St = tuple(s_scr[hh] for hh in range(HB))
        for c in range(CPS):
            rows = slice(c * C, (c + 1) * C)
            heads = lambda ref: tuple(ref[rows, sl] for sl in sls)
            gr = g_ref[rows, :]
            for hh in range(HB):
                st_ref[c, hh] = St[hh]
            o, St = _chunk_fn(kc, False, heads(q_ref), heads(k_ref), heads(v_ref),
                              tuple(_lane_pick(gr, H + hh) for hh in range(HB)),
                              tuple(_lane_pick(gr, hh) for hh in range(HB)), St)
            for hh in range(HB):
                o_ref[rows, sls[hh]] = o[hh]
        for hh in range(HB):
            s_scr[hh] = St[hh]

    part = lambda p: pl.BlockSpec((None, CPS * C, HB * HEAD), lambda hb, n: (p, n, hb))
    return hosted_call(
        body, comm, _grid_steps(H // HB, N // CPS), name=name, grid=(H // HB, N // CPS),
        in_specs=[part(0), part(1), part(2), pl.BlockSpec((CPS * C, HEAD), lambda hb, n: (n, 0))],
        out_specs=(pl.BlockSpec((CPS * C, HB * HEAD), lambda hb, n: (n, hb)),
                   pl.BlockSpec((CPS, HB, HEAD, HEAD), lambda hb, n: (n, hb, 0, 0))),
        out_shape=(SDS((S, W), f32), SDS((N, H, HEAD, HEAD), f32)),
        scratch_shapes=[pltpu.VMEM((HB, HEAD, HEAD), f32)], args=(qkv, qkv, qkv, gates))


def dn_chunk_bwd(qkv, gates, states, do, name, comm=None):
    _, S, W = qkv.shape
    H, C = W // HEAD, DN_CHUNK
    N, HB = S // C, _heads_per_block(H)
    assert HB == H
    CPS = _chunks_per_step(N)
    NB = N // CPS

    def body(q_ref, k_ref, v_ref, g_ref, st_ref, do_ref, dqkv_ref, dg_ref, ds_scr):
        @pl.when(pl.program_id(1) == 0)
        def _():
            ds_scr[...] = jnp.zeros_like(ds_scr)

        kc = _chunk_consts()
        sls = [slice(hh * HEAD, (hh + 1) * HEAD) for hh in range(HB)]
        dSt = tuple(ds_scr[hh] for hh in range(HB))
        for c in reversed(range(CPS)):
            rows = slice(c * C, (c + 1) * C)
            heads = lambda ref: tuple(ref[rows, sl] for sl in sls)
            gr = g_ref[rows, :]
            _, vjp = jax.vjp(functools.partial(_chunk_fn, kc, True), heads(q_ref), heads(k_ref), heads(v_ref),
                             tuple(_lane_pick(gr, H + hh) for hh in range(HB)),
                             tuple(_lane_pick(gr, hh) for hh in range(HB)), tuple(st_ref[c, hh] for hh in range(HB)))
            dq, dk, dv, dg, db, dSt = vjp((heads(do_ref), dSt))
            lane = lax.broadcasted_iota(jnp.int32, (C, HEAD), 1)
            dgr = jnp.zeros((C, HEAD), f32)
            for hh in range(HB):
                dqkv_ref[0, rows, sls[hh]] = dq[hh]
                dqkv_ref[1, rows, sls[hh]] = dk[hh]
                dqkv_ref[2, rows, sls[hh]] = dv[hh]
                dgr = dgr + jnp.where(lane == hh, jnp.sum(db[hh], axis=1, keepdims=True), 0.0)
                dgr = dgr + jnp.where(lane == H + hh, jnp.sum(dg[hh], axis=1, keepdims=True), 0.0)
            dg_ref[rows, :] = dgr
        for hh in range(HB):
            ds_scr[hh] = dSt[hh]

    rev = lambda n: NB - 1 - n
    part = lambda p: pl.BlockSpec((None, CPS * C, HB * HEAD), lambda hb, n: (p, rev(n), hb))
    gate = pl.BlockSpec((CPS * C, HEAD), lambda hb, n: (rev(n), 0))
    return hosted_call(
        body, comm, _grid_steps(H // HB, NB), name=name, grid=(H // HB, NB),
        in_specs=[part(0), part(1), part(2), gate,
                  pl.BlockSpec((CPS, HB, HEAD, HEAD), lambda hb, n: (rev(n), hb, 0, 0)),
                  pl.BlockSpec((CPS * C, HB * HEAD), lambda hb, n: (rev(n), hb))],
        out_specs=(pl.BlockSpec((3, CPS * C, HB * HEAD), lambda hb, n: (0, rev(n), hb)), gate),
        out_shape=(SDS((3, S, W), f32), SDS((S, HEAD), f32)),
        scratch_shapes=[pltpu.VMEM((HB, HEAD, HEAD), f32)], args=(qkv, qkv, qkv, gates, states, do))


def _gate_norm(o, z, ng):
    return _rms(o, ng) * _silu(z)


def dn_out(o, proj, ng, wout, x1, g3, name):
    S, W = o.shape
    D = x1.shape[1]
    nh = W // HEAD
    tm = _row_tile(S, 256)

    def body(o_ref, z_ref, ng_ref, w_ref, x_ref, g_ref, xo_ref, m_ref, og_ref):
        for h in range(nh):
            sl = slice(h * HEAD, (h + 1) * HEAD)
            og_ref[:, sl] = _gate_norm(o_ref[:, sl], z_ref[:, sl], ng_ref[...]).astype(og_ref.dtype)
        m = _mm(og_ref[...], w_ref[...])
        m_ref[...] = m
        xo_ref[...] = x_ref[...] + _rms(m, g_ref[...])

    rw = pl.BlockSpec((tm, W), lambda i: (i, 0))
    rd = pl.BlockSpec((tm, D), lambda i: (i, 0))
    return pl.pallas_call(
        body, name=name, grid=(S // tm,),
        in_specs=[rw, pl.BlockSpec((tm, W), lambda i: (i, 3)), pl.BlockSpec((1, HEAD), lambda i: (0, 0)),
                  pl.BlockSpec((W, D), lambda i: (0, 0)), rd, pl.BlockSpec((1, D), lambda i: (0, 0))],
        out_specs=(rd, rd, rw),
        out_shape=(SDS((S, D), f32), SDS((S, D), f32), SDS((S, W), MXU_DTYPE)), compiler_params=_params(1),
    )(o, proj, ng, wout, x1, g3)


def dn_out_bwd(dxo, m, g3, o, proj, ng, wout, name):
    S, W = o.shape
    D = m.shape[1]
    nh = W // HEAD
    tm = _row_tile(S, 256)

    def body(dxo_ref, m_ref, g_ref, o_ref, z_ref, ng_ref, w_ref, dm_ref, do_ref, dz_ref, dng_ref, dg_ref):
        @pl.when(pl.program_id(0) == 0)
        def _():
            dng_ref[...] = jnp.zeros_like(dng_ref)
            dg_ref[...] = jnp.zeros_like(dg_ref)

        dm, dg = _rms_bwd(m_ref[...], g_ref[...], dxo_ref[...])
        dg_ref[...] += dg
        dmc = dm.astype(dm_ref.dtype)
        dm_ref[...] = dmc
        dog = _mm_nt(dmc, w_ref[...])
        for h in range(nh):
            sl = slice(h * HEAD, (h + 1) * HEAD)
            _, vjp = jax.vjp(_gate_norm, o_ref[:, sl], z_ref[:, sl], ng_ref[...])
            do, dz, dng = vjp(dog[:, sl])
            do_ref[:, sl] = do
            dz_ref[:, sl] = dz.astype(dz_ref.dtype)
            dng_ref[...] += dng

    rw = pl.BlockSpec((tm, W), lambda i: (i, 0))
    rd = pl.BlockSpec((tm, D), lambda i: (i, 0))
    vd = pl.BlockSpec((1, D), lambda i: (0, 0))
    vh = pl.BlockSpec((1, HEAD), lambda i: (0, 0))
    return pl.pallas_call(
        body, name=name, grid=(S // tm,),
        in_specs=[rd, rd, vd, rw, pl.BlockSpec((tm, W), lambda i: (i, 3)), vh, pl.BlockSpec((W, D), lambda i: (0, 0))],
        out_specs=(rd, rw, rw, vh, vd),
        out_shape=(SDS((S, D), MXU_DTYPE), SDS((S, W), f32), SDS((S, W), MXU_DTYPE), SDS((1, HEAD), f32),
                   SDS((1, D), f32)),
        compiler_params=_params(1),
    )(dxo, m, g3, o, proj, ng, wout)


def _sg_stage1(pu, pv, bu, bv, lg, lb):
    u = _gelu(pu + bu)
    t = _gelu(pv + bv)
    tc = t - jnp.mean(t, axis=-1, keepdims=True)
    v = tc * lax.rsqrt(jnp.mean(tc * tc, axis=-1, keepdims=True) + LN_EPS) * lg + lb
    return u, v


def _causal_mask(n):
    return lax.broadcasted_iota(jnp.int32, (n, n), 0) >= lax.broadcasted_iota(jnp.int32, (n, n), 1)


def sg_mid(pre, b_in, ln_g, ln_b, w_s, bsT, wout, x1, g3, name):
    S = pre.shape[0]
    E, D = ln_g.shape[1], x1.shape[1]
    G, CH = SG_GROUPS, SG_CHUNK
    Cg = E // G
    tm = _row_tile(S, 256)

    def body(pu_ref, pv_ref, bu_ref, bv_ref, lg_ref, lb_ref, ws_ref, bs_ref, w_ref, x_ref, g_ref,
             xo_ref, m_ref, gt_ref):
        u, v = _sg_stage1(pu_ref[...], pv_ref[...], bu_ref[...], bv_ref[...], lg_ref[...], lb_ref[...])
        mask = _causal_mask(CH)
        for g in range(G):
            wc = _c(jnp.where(mask, ws_ref[g], 0.0))
            bcol = bs_ref[:, g:g + 1]
            cs = slice(g * Cg, (g + 1) * Cg)
            for ch in range(tm // CH):
                rs = slice(ch * CH, (ch + 1) * CH)
                mixed = _mm(wc, _c(v[rs, cs])) + bcol
                gt_ref[rs, cs] = (u[rs, cs] * mixed).astype(gt_ref.dtype)
        m = _mm(gt_ref[...], w_ref[...])
        m_ref[...] = m
        xo_ref[...] = x_ref[...] + _rms(m, g_ref[...])

    half = lambda p: pl.BlockSpec((tm, E), lambda i: (i, p))
    vhalf = lambda p: pl.BlockSpec((1, E), lambda i: (0, p))
    ve = pl.BlockSpec((1, E), lambda i: (0, 0))
    rd = pl.BlockSpec((tm, D), lambda i: (i, 0))
    return pl.pallas_call(
        body, name=name, grid=(S // tm,),
        in_specs=[half(0), half(1), vhalf(0), vhalf(1), ve, ve, pl.BlockSpec((G, CH, CH), lambda i: (0, 0, 0)),
                  pl.BlockSpec((CH, G), lambda i: (0, 0)), pl.BlockSpec((E, D), lambda i: (0, 0)), rd,
                  pl.BlockSpec((1, D), lambda i: (0, 0))],
        out_specs=(rd, rd, pl.BlockSpec((tm, E), lambda i: (i, 0))),
        out_shape=(SDS((S, D), f32), SDS((S, D), f32), SDS((S, E), MXU_DTYPE)), compiler_params=_params(1),
    )(pre, pre, b_in, b_in, ln_g, ln_b, w_s, bsT, wout, x1, g3)


def sg_mid_bwd(dxo, m, g3, pre, b_in, ln_g, ln_b, w_s, bsT, wout, name):
    S = pre.shape[0]
    E, D = ln_g.shape[1], m.shape[1]
    G, CH = SG_GROUPS, SG_CHUNK
    Cg = E // G
    tm = _row_tile(S, 256)

    def body(dxo_ref, m_ref, g_ref, pu_ref, pv_ref, bu_ref, bv_ref, lg_ref, lb_ref, ws_ref, bs_ref, w_ref,
             dm_ref, dpre_ref, dbin_ref, dlg_ref, dlb_ref, dws_ref, dbs_ref, dg_ref, du_scr, dv_scr):
        @pl.when(pl.program_id(0) == 0)
        def _():
            for r in (dbin_ref, dlg_ref, dlb_ref, dws_ref, dbs_ref, dg_ref):
                r[...] = jnp.zeros_like(r)

        dm, dg = _rms_bwd(m_ref[...], g_ref[...], dxo_ref[...])
        dg_ref[...] += dg
        dmc = dm.astype(dm_ref.dtype)
        dm_ref[...] = dmc
        dgated = _mm_nt(dmc, w_ref[...])
        (u, v), vjp1 = jax.vjp(_sg_stage1, pu_ref[...], pv_ref[...], bu_ref[...], bv_ref[...], lg_ref[...],
                               lb_ref[...])
        mask = _causal_mask(CH)
        lane = lax.broadcasted_iota(jnp.int32, (CH, CH), 1)
        for g in range(G):
            wc = _c(jnp.where(mask, ws_ref[g], 0.0))
            bcol = bs_ref[:, g:g + 1]
            cs = slice(g * Cg, (g + 1) * Cg)
            dws = jnp.zeros((CH, CH), f32)
            dbs = jnp.zeros((CH, 1), f32)
            for ch in range(tm // CH):
                rs = slice(ch * CH, (ch + 1) * CH)
                vs = _c(v[rs, cs])
                mixed = _mm(wc, vs) + bcol
                dgt = dgated[rs, cs]
                du_scr[rs, cs] = dgt * mixed
                dmixed = dgt * u[rs, cs]
                dmc2 = _c(dmixed)
                dv_scr[rs, cs] = _mm_tn(wc, dmc2)
                dws = dws + _mm_nt(dmc2, vs)
                dbs = dbs + jnp.sum(dmixed, axis=1, keepdims=True)
            dws_ref[g] += jnp.where(mask, dws, 0.0)
            dbs_ref[...] += jnp.where(lane == g, jnp.broadcast_to(dbs, (CH, CH)), 0.0)
        dpu, dpv, dbu, dbv, dlg, dlb = vjp1((du_scr[...], dv_scr[...]))
        dpre_ref[:, :E] = dpu.astype(dpre_ref.dtype)
        dpre_ref[:, E:] = dpv.astype(dpre_ref.dtype)
        dbin_ref[:, :E] += dbu
        dbin_ref[:, E:] += dbv
        dlg_ref[...] += dlg
        dlb_ref[...] += dlb

    half = lambda p: pl.BlockSpec((tm, E), lambda i: (i, p))
    vhalf = lambda p: pl.BlockSpec((1, E), lambda i: (0, p))
    ve = pl.BlockSpec((1, E), lambda i: (0, 0))
    rd = pl.BlockSpec((tm, D), lambda i: (i, 0))
    vd = pl.BlockSpec((1, D), lambda i: (0, 0))
    wsb = pl.BlockSpec((G, CH, CH), lambda i: (0, 0, 0))
    return pl.pallas_call(
        body, name=name, grid=(S // tm,),
        in_specs=[rd, rd, vd, half(0), half(1), vhalf(0), vhalf(1), ve, ve, wsb,
                  pl.BlockSpec((CH, G), lambda i: (0, 0)), pl.BlockSpec((E, D), lambda i: (0, 0))],
        out_specs=(rd, pl.BlockSpec((tm, 2 * E), lambda i: (i, 0)), pl.BlockSpec((1, 2 * E), lambda i: (0, 0)), ve, ve,
                   wsb, pl.BlockSpec((CH, CH), lambda i: (0, 0)), vd),
        out_shape=(SDS((S, D), MXU_DTYPE), SDS((S, 2 * E), MXU_DTYPE), SDS((1, 2 * E), f32), SDS((1, E), f32),
                   SDS((1, E), f32), SDS((G, CH, CH), f32), SDS((CH, CH), f32), SDS((1, D), f32)),
        scratch_shapes=[pltpu.VMEM((tm, E), f32), pltpu.VMEM((tm, E), f32)], compiler_params=_params(1),
    )(dxo, m, g3, pre, pre, b_in, b_in, ln_g, ln_b, w_s, bsT, wout)


def loss_head(y, target, name):
    S, D = y.shape
    tm = _row_tile(S, 512)

    def body(y_ref, t_ref, l_ref, d_ref):
        @pl.when(pl.program_id(0) == 0)
        def _():
            l_ref[...] = jnp.zeros_like(l_ref)

        e = y_ref[...] - t_ref[...]
        d_ref[...] = e * (1.0 / D)
        l_ref[...] += jnp.sum(e * e) * (0.5 / D)

    row = pl.BlockSpec((tm, D), lambda i: (i, 0))
    return pl.pallas_call(
        body, name=name, grid=(S // tm,), in_specs=[row, row],
        out_specs=(pl.BlockSpec((1, HEAD), lambda i: (0, 0)), row),
        out_shape=(SDS((1, HEAD), f32), SDS((S, D), f32)), compiler_params=_params(1),
    )(y, target)


def sum_slots(r, name):
    _, R, C = r.shape
    tr = _row_tile(R, 648 if R % 648 == 0 else R)

    def body(r_ref, o_ref):
        acc = r_ref[0].astype(f32)
        for s in range(1, N_DEV):
            acc = acc + r_ref[s].astype(f32)
        o_ref[...] = acc

    return pl.pallas_call(
        body, name=name, grid=(R // tr,), in_specs=[pl.BlockSpec((N_DEV, tr, C), lambda i: (0, i, 0))],
        out_specs=pl.BlockSpec((tr, C), lambda i: (i, 0)), out_shape=SDS((R, C), f32), compiler_params=_params(1),
    )(r)


def _adam_math(w, g, m, v):
    m = ADAM_B1 * m + (1.0 - ADAM_B1) * g
    v = ADAM_B2 * v + (1.0 - ADAM_B2) * (g * g)
    m_hat = m / (1.0 - ADAM_B1 ** ADAM_STEP)
    v_hat = v / (1.0 - ADAM_B2 ** ADAM_STEP)
    delta = -ADAM_LR * (m_hat / (jnp.sqrt(v_hat) + ADAM_EPS) + ADAM_WD * w)
    return delta, m, v


def adam_slots(w, rs, m, v, name, tr):
    R, C = w.shape
    tr = _row_tile(min(r.shape[1] for r in rs), tr)
    blocks = [r.shape[1] // tr for r in rs]
    starts = [sum(blocks[:k]) for k in range(len(rs))]
    assert sum(blocks) * tr == R

    def body(w_ref, *refs):
        r_refs, (m_ref, v_ref, g_ref, d_ref, mo_ref, vo_ref) = refs[:len(rs)], refs[len(rs):]
        i = pl.program_id(0)
        for k, r_ref in enumerate(r_refs):
            @pl.when((i >= starts[k]) & (i < starts[k] + blocks[k]))
            def _():
                g = r_ref[0].astype(f32)
                for s in range(1, N_DEV):
                    g = g + r_ref[s].astype(f32)
                g_ref[...] = g

        d_ref[...], mo_ref[...], vo_ref[...] = _adam_math(w_ref[...], g_ref[...], m_ref[...], v_ref[...])

    row = pl.BlockSpec((tr, C), lambda i: (i, 0))
    piece = lambda k: pl.BlockSpec((N_DEV, tr, C), lambda i: (0, jnp.clip(i - starts[k], 0, blocks[k] - 1), 0))
    return pl.pallas_call(
        body, name=name, grid=(R // tr,), in_specs=[row] + [piece(k) for k in range(len(rs))] + [row, row],
        out_specs=(row, row, row, row), out_shape=tuple(SDS((R, C), f32) for _ in range(4)),
        compiler_params=_params(1),
    )(w, *rs, m, v)


def adam_small(w, g, m, v, name):
    def body(w_ref, g_ref, m_ref, v_ref, d_ref, mo_ref, vo_ref):
        d_ref[...], mo_ref[...], vo_ref[...] = _adam_math(w_ref[...], g_ref[...], m_ref[...], v_ref[...])

    return pl.pallas_call(body, name=name, out_shape=tuple(SDS(w.shape, f32) for _ in range(3)))(w, g, m, v)


def _pack_rows(parts):
    rows, offs, r = [], [], 0
    for p in parts:
        flat = p.reshape(-1)
        n = -(-flat.shape[0] // HEAD)
        flat = jnp.pad(flat, (0, n * HEAD - flat.shape[0]))
        rows.append(flat.reshape(n, HEAD))
        offs.append((r, n))
        r += n
    pad = (-r) % 8
    if pad:
        rows.append(jnp.zeros((pad, HEAD), f32))
    return jnp.concatenate(rows, axis=0), offs


def kernel(x, norm_g, ffn_w_gate, ffn_w_up, ffn_w_down, dn_w_in, dn_conv_w, dn_a_log, dn_dt_bias, dn_norm_g, dn_w_out, sg_w_in, sg_b_in, sg_ln_g, sg_ln_b, sg_w_s, sg_b_s, sg_w_out, loss_target, m_norm_g, m_ffn_w_gate, m_ffn_w_up, m_ffn_w_down, m_dn_w_in, m_dn_conv_w, m_dn_a_log, m_dn_dt_bias, m_dn_norm_g, m_dn_w_out, m_sg_w_in, m_sg_b_in, m_sg_ln_g, m_sg_ln_b, m_sg_w_s, m_sg_b_s, m_sg_w_out, v_norm_g, v_ffn_w_gate, v_ffn_w_up, v_ffn_w_down, v_dn_w_in, v_dn_conv_w, v_dn_a_log, v_dn_dt_bias, v_dn_norm_g, v_dn_w_out, v_sg_w_in, v_sg_b_in, v_sg_ln_g, v_sg_ln_b, v_sg_w_s, v_sg_b_s, v_sg_w_out):
    weights = dict(norm_g=norm_g, ffn_w_gate=ffn_w_gate, ffn_w_up=ffn_w_up, ffn_w_down=ffn_w_down, dn_w_in=dn_w_in,
                   dn_conv_w=dn_conv_w, dn_a_log=dn_a_log, dn_dt_bias=dn_dt_bias, dn_norm_g=dn_norm_g,
                   dn_w_out=dn_w_out, sg_w_in=sg_w_in, sg_b_in=sg_b_in, sg_ln_g=sg_ln_g, sg_ln_b=sg_ln_b,
                   sg_w_s=sg_w_s, sg_b_s=sg_b_s, sg_w_out=sg_w_out)
    mom_m = dict(norm_g=m_norm_g, ffn_w_gate=m_ffn_w_gate, ffn_w_up=m_ffn_w_up, ffn_w_down=m_ffn_w_down,
                 dn_w_in=m_dn_w_in, dn_conv_w=m_dn_conv_w, dn_a_log=m_dn_a_log, dn_dt_bias=m_dn_dt_bias,
                 dn_norm_g=m_dn_norm_g, dn_w_out=m_dn_w_out, sg_w_in=m_sg_w_in, sg_b_in=m_sg_b_in,
                 sg_ln_g=m_sg_ln_g, sg_ln_b=m_sg_ln_b, sg_w_s=m_sg_w_s, sg_b_s=m_sg_b_s, sg_w_out=m_sg_w_out)
    mom_v = dict(norm_g=v_norm_g, ffn_w_gate=v_ffn_w_gate, ffn_w_up=v_ffn_w_up, ffn_w_down=v_ffn_w_down,
                 dn_w_in=v_dn_w_in, dn_conv_w=v_dn_conv_w, dn_a_log=v_dn_a_log, dn_dt_bias=v_dn_dt_bias,
                 dn_norm_g=v_dn_norm_g, dn_w_out=v_dn_w_out, sg_w_in=v_sg_w_in, sg_b_in=v_sg_b_in,
                 sg_ln_g=v_sg_ln_g, sg_ln_b=v_sg_ln_b, sg_w_s=v_sg_w_s, sg_b_s=v_sg_b_s, sg_w_out=v_sg_w_out)
    order = list(weights)

    xs = x[0]
    S, D = xs.shape
    F8 = ffn_w_gate.shape[-1]
    depth = norm_g.shape[0]
    W = dn_w_out.shape[1] * N_DEV
    H = W // HEAD
    E = sg_ln_g.shape[1] * N_DEV
    G, CH = sg_w_s.shape[1], sg_w_s.shape[2]
    c8 = dn_w_in.shape[2]
    me = _slot(lax.axis_index("x"), lax.axis_index("y"), lax.axis_index("c"))

    assert depth == 2
    small_in, small_offs = _pack_rows([norm_g, dn_conv_w, sg_b_in, sg_ln_g, sg_ln_b])
    wg0a, wu0a, wd0a, small_all = all_gather_multi(
        [_c(ffn_w_gate[0, 0]), _c(ffn_w_up[0, 0]), _c(ffn_w_down[0, 0]), small_in], name="gather_first")
    gather_l0 = Comm("gather", [_c(dn_w_in[0]), _c(dn_w_out[0]), _c(ffn_w_gate[0, 1]), _c(ffn_w_up[0, 1]),
                                _c(ffn_w_down[0, 1])])
    gather_l1 = Comm("gather", [_c(ffn_w_gate[1]), _c(ffn_w_up[1]), _c(ffn_w_down[1]), _c(sg_w_in[0]),
                                _c(sg_w_out[0])])
    per = N_DEV // FFN_SLABS
    wide_cols = lambda w: jnp.transpose(w.reshape(FFN_SLABS, per, D, F8), (0, 2, 1, 3)).reshape(FFN_SLABS, D, per * F8)
    wide = lambda g, u, d: (wide_cols(g), wide_cols(u), d.reshape(FFN_SLABS, per * F8, D))
    shard_cols = lambda dw: jnp.transpose(dw.reshape(FFN_SLABS, D, per, F8), (0, 2, 1, 3)).reshape(N_DEV, D, F8)
    shard_rows = lambda dw: dw.reshape(N_DEV, F8, D)
    ffn_w = {(0, 0): wide(wg0a, wu0a, wd0a)}

    def small_piece(i, shard_shape):
        r0, n = small_offs[i]
        sz = math.prod(shard_shape)
        return small_all[:, r0:r0 + n, :].reshape(N_DEV, n * HEAD)[:, :sz].reshape((N_DEV,) + tuple(shard_shape))

    ng_full = jnp.moveaxis(small_piece(0, norm_g.shape), 0, 2).reshape(depth, 6, D)
    conv_full = jnp.moveaxis(small_piece(1, dn_conv_w.shape[1:]), 0, 1).reshape(CONV_K, 3 * W)
    bin_full = small_piece(2, sg_b_in.shape[1:]).reshape(1, 2 * E)
    lng_full = small_piece(3, sg_ln_g.shape[1:]).reshape(1, E)
    lnb_full = small_piece(4, sg_ln_b.shape[1:]).reshape(1, E)
    gate_lanes = lambda v: jnp.pad(v.reshape(1, H), ((0, 0), (H, HEAD - 2 * H)))
    al_row, dt_row = gate_lanes(dn_a_log), gate_lanes(dn_dt_bias)
    bsT = sg_b_s[0].T
    gvec = lambda l, k: ng_full[l, k].reshape(1, D)

    saved = []
    cur = xs
    for l in range(depth):
        sv = {}
        sv['x0'] = cur
        (cur, sv['hA'], sv['pA'], sv['qA'], sv['tA'], sv['yA']), got = ffn_fwd(
            cur, gvec(l, 0), gvec(l, 1), *ffn_w[l, 0], name=f"ffn_fwd_{l}a", comm=gather_l0 if l == 0 else None)
        sv['x1'] = cur
        if l == 0:
            dnin_all, dnout_all, wg0b, wu0b, wd0b = got
            ffn_w[0, 1] = wide(wg0b, wu0b, wd0b)
            dn_win = jnp.moveaxis(dnin_all, 0, 1).reshape(D, N_DEV * c8)
            dn_wmain = dn_win[:, :4 * W]
            dn_wba = jnp.pad(dn_win[:, 4 * W:], ((0, 0), (0, HEAD - 2 * H)))
            dn_wout = dnout_all.reshape(W, D)
            sv['hM'], sv['proj'], sv['pba'] = rms_mm(cur, gvec(l, 2), dn_wmain, dn_wba, name=f"dn_in_{l}")
            sv['qkv'] = dn_prep(sv['proj'], conv_full, name=f"dn_prep_{l}")
            sv['gates'] = dn_gates(sv['pba'], al_row, dt_row, H, name=f"dn_gates_{l}")
            (sv['o'], sv['states']), got = dn_chunk_fwd(sv['qkv'], sv['gates'], name=f"dn_chunk_{l}", comm=gather_l1)
            wg1, wu1, wd1, sgin_all, sgout_all = got
            for ab in range(2):
                ffn_w[1, ab] = wide(wg1[:, ab], wu1[:, ab], wd1[:, ab])
            sg_win = jnp.moveaxis(sgin_all, 0, 1).reshape(D, 2 * E)
            sg_wout = sgout_all.reshape(E, D)
            cur, sv['m'], sv['og'] = dn_out(sv['o'], sv['proj'], dn_norm_g, dn_wout, cur, gvec(l, 3), name=f"dn_out_{l}")
        else:
            sv['hM'], sv['pre'] = rms_mm(cur, gvec(l, 2), sg_win, None, name=f"sg_in_{l}")
            cur, sv['m'], sv['gated'] = sg_mid(sv['pre'], bin_full, lng_full, lnb_full, sg_w_s[0], bsT, sg_wout, cur,
                                               gvec(l, 3), name=f"sg_mid_{l}")
        sv['x2'] = cur
        (cur, sv['hB'], sv['pB'], sv['qB'], sv['tB'], sv['yB']), _ = ffn_fwd(cur, gvec(l, 4), gvec(l, 5), *ffn_w[l, 1],
                                                                   name=f"ffn_fwd_{l}b")
        saved.append(sv)

    loss_blk, dcur = loss_head(cur, loss_target[0], name="loss_head")
    loss = lax.psum(loss_blk[0, 0], ("x", "y", "c"))

    dng = [[None] * 6 for _ in range(depth)]
    ffn_dw = {}
    grads, slots = {}, {}

    def ffn_backward(l, ab, dcur, comm=None):
        sv, s = saved[l], 'AB'[ab]
        (dcur, da, db, dy, dng[l][4 * ab], dng[l][4 * ab + 1]), got = ffn_bwd_dx(
            dcur, sv['x2' if ab else 'x0'], sv['y' + s], sv['p' + s], sv['q' + s], gvec(l, 4 * ab), gvec(l, 4 * ab + 1),
            *ffn_w[l, ab], name=f"ffn_bwd_{l}{'ab'[ab]}", comm=comm)
        dg, du, dd = ffn_bwd_dw(sv['h' + s], dy, sv['t' + s], da, db, name=f"ffn_dw_{l}{'ab'[ab]}")
        ffn_dw[l, ab] = (shard_cols(dg), shard_cols(du), shard_rows(dd))
        return dcur, got

    sv = saved[1]
    dcur, _ = ffn_backward(1, 1, dcur)
    dm, dpre, grads['sg_b_in'], grads['sg_ln_g'], grads['sg_ln_b'], grads['sg_w_s'], dbs, dng[1][3] = sg_mid_bwd(
        dcur, sv['m'], gvec(1, 3), sv['pre'], bin_full, lng_full, lnb_full, sg_w_s[0], bsT, sg_wout, name="sg_mid_bwd_1")
    grads['sg_b_s'] = dbs[:, :G].T
    dsg_wout = tn_mm(sv['gated'], dm, name="sg_wout_dw_1").reshape(N_DEV, E // N_DEV, D)
    dsg_win = tn_mm(sv['hM'], dpre, name="sg_win_dw_1", tn=2 * E // N_DEV, slot_major=True)
    dcur, dng[1][2] = mm_bwd_dx(dcur, sv['x1'], gvec(1, 2), dpre, sg_win, None, None, name="sg_in_bwd_1")
    dcur, _ = ffn_backward(1, 0, dcur)
    sv = saved[0]
    dcur, _ = ffn_backward(0, 1, dcur)
    dm, do, dz, grads['dn_norm_g'], dng[0][3] = dn_out_bwd(dcur, sv['m'], gvec(0, 3), sv['o'], sv['proj'], dn_norm_g,
                                                          dn_wout, name="dn_out_bwd_0")
    ddn_wout = tn_mm(sv['og'], dm, name="dn_wout_dw_0").reshape(N_DEV, W // N_DEV, D)
    (dqkv, dgates), got = dn_chunk_bwd(sv['qkv'], sv['gates'], sv['states'], do, name="dn_chunk_bwd_0",
                                       comm=Comm("exchange", [*ffn_dw[1, 0], *ffn_dw[1, 1], dsg_win, dsg_wout]))
    l1a, l1b, slots['sg_w_in'], slots['sg_w_out'] = got[0:3], got[3:6], [got[6]], [got[7]]
    dpba, dal, ddt = dn_gates_bwd(sv['pba'], al_row, dt_row, dgates, H, name="dn_gates_bwd_0")
    grads['dn_a_log'] = dal[:, H:2 * H]
    grads['dn_dt_bias'] = ddt[:, H:2 * H]
    (dproj, grads['dn_conv_w']), l0b = dn_prep_bwd(sv['proj'], conv_full, dqkv, dz, name="dn_prep_bwd_0",
                                                   comm=Comm("exchange", list(ffn_dw[0, 1])))
    dw_main = tn_mm(sv['hM'], dproj, name="dn_win_dw_0")
    dw_ba = tn_mm(sv['hM'], dpba, name="dn_wba_dw_0", tn=HEAD)
    dw_in = jnp.concatenate([dw_main, dw_ba[:, :2 * H]], axis=1)
    ddn_win = jnp.moveaxis(dw_in.reshape(D, N_DEV, c8), 1, 0)
    dcur, dng[0][2] = mm_bwd_dx(dcur, sv['x1'], gvec(0, 2), dproj, dn_wmain, dpba, dn_wba, name="dn_in_bwd_0")
    dcur, got = ffn_backward(0, 0, dcur, comm=Comm("exchange", [ddn_win, ddn_wout]))
    slots['dn_w_in'], slots['dn_w_out'] = [got[0]], [got[1]]
    grad_x = dcur[None]
    l0a = exchange_slots(list(ffn_dw[0, 0]), name="exchange_last")
    for i, nm in enumerate(['ffn_w_gate', 'ffn_w_up', 'ffn_w_down']):
        slots[nm] = [l0a[i], l0b[i], l1a[i], l1b[i]]
    big_names = ['ffn_w_gate', 'ffn_w_up', 'ffn_w_down', 'dn_w_in', 'dn_w_out', 'sg_w_in', 'sg_w_out']
    slots = [slots[nm] for nm in big_names]

    dng_full = jnp.stack([jnp.concatenate(r, axis=0) for r in dng], axis=0)
    small_names = ['norm_g', 'dn_conv_w', 'sg_b_in', 'sg_ln_g', 'sg_ln_b', 'sg_w_s', 'sg_b_s', 'dn_a_log',
                   'dn_dt_bias', 'dn_norm_g']
    small_parts = [dng_full, grads['dn_conv_w'], grads['sg_b_in'], grads['sg_ln_g'], grads['sg_ln_b'],
                   grads['sg_w_s'], grads['sg_b_s'], grads['dn_a_log'], grads['dn_dt_bias'], grads['dn_norm_g']]
    small_pack, offs = _pack_rows(small_parts)
    (small_slots,) = all_gather_multi([small_pack], name="gather_small_grads")
    small_sum = sum_slots(small_slots, name="sum_small_grads")

    def small_grad(i):
        r0, n = offs[i]
        p = small_parts[i]
        return small_sum[r0:r0 + n].reshape(-1)[:p.size].reshape(p.shape)

    def my_shard(full, axis, like):
        n = full.shape[axis] // N_DEV
        return lax.dynamic_slice_in_dim(full, me * n, n, axis).reshape(like.shape)

    g_small = {
        'norm_g': my_shard(small_grad(0), 2, norm_g),
        'dn_conv_w': my_shard(small_grad(1), 1, dn_conv_w),
        'sg_b_in': my_shard(small_grad(2), 1, sg_b_in),
        'sg_ln_g': my_shard(small_grad(3), 1, sg_ln_g),
        'sg_ln_b': my_shard(small_grad(4), 1, sg_ln_b),
        'sg_w_s': small_grad(5).reshape(sg_w_s.shape),
        'sg_b_s': small_grad(6).reshape(sg_b_s.shape),
        'dn_a_log': small_grad(7).reshape(dn_a_log.shape),
        'dn_dt_bias': small_grad(8).reshape(dn_dt_bias.shape),
        'dn_norm_g': small_grad(9).reshape(dn_norm_g.shape),
    }

    out_g, out_d, out_m, out_v = {}, {}, {}, {}
    for nm, r in zip(big_names, slots):
        w = weights[nm]
        cols = w.shape[-1]
        rows = w.size // cols
        tr = {'ffn_w_gate': 512, 'ffn_w_up': 512, 'ffn_w_down': F8 // 2, 'dn_w_in': 256, 'sg_w_in': 256}.get(nm, rows)
        pieces = [p.reshape(N_DEV, -1, cols) for p in r]
        g, d, m2, v2 = adam_slots(w.reshape(rows, cols), pieces, mom_m[nm].reshape(rows, cols),
                                  mom_v[nm].reshape(rows, cols), name=f"adam_{nm}", tr=tr)
        out_g[nm], out_d[nm], out_m[nm], out_v[nm] = (t.reshape(w.shape) for t in (g, d, m2, v2))
    for nm in small_names:
        w = weights[nm]
        cols = w.shape[-1]
        rows = w.size // cols
        two = lambda t: t.reshape(rows, cols)
        d, m2, v2 = adam_small(two(w), two(g_small[nm]), two(mom_m[nm]), two(mom_v[nm]), name=f"adam_{nm}")
        out_g[nm] = g_small[nm]
        out_d[nm], out_m[nm], out_v[nm] = (t.reshape(w.shape) for t in (d, m2, v2))

    return (loss, grad_x, *[out_g[n] for n in order], *[out_d[n] for n in order], *[out_m[n] for n in order],
            *[out_v[n] for n in order])
```

```python
import functools
import math

import jax
import jax.numpy as jnp
from jax import lax
from jax.experimental import pallas as pl
from jax.experimental.pallas import tpu as pltpu

f32 = jnp.float32
MXU_DTYPE = jnp.bfloat16
N_DEV = 8
RMS_EPS = 1e-6
LN_EPS = 1e-5
L2_EPS = 1e-6
HEAD = 128
DN_CHUNK = 64
SG_CHUNK = 128
SG_GROUPS = 8
CONV_K = 4
ADAM_LR, ADAM_B1, ADAM_B2, ADAM_EPS, ADAM_WD, ADAM_STEP = 0.001, 0.9, 0.999, 1e-08, 0.01, 10
VMEM_LIMIT = 56 * 1024 * 1024
FFN_ROWS_FWD, FFN_ROWS_BWD, FFN_ROWS_DW = 1024, 512, 2048
PROJ_ROWS, TN_ROWS = 1024, 2048
FFN_SLABS = 4
SDS = jax.ShapeDtypeStruct
HIGHEST = lax.Precision.HIGHEST
MESH = pl.DeviceIdType.MESH


def _params(n_grid):
    return pltpu.CompilerParams(dimension_semantics=("arbitrary",) * n_grid, vmem_limit_bytes=VMEM_LIMIT)


def _row_tile(s, want):
    t = min(s, want)
    assert s % t == 0, (s, t)
    return t


def _rms(x, g):
    return x * lax.rsqrt(jnp.mean(x * x, axis=-1, keepdims=True) + RMS_EPS) * g


def _rms_bwd(x, g, dy):
    _, vjp = jax.vjp(_rms, x, g)
    return vjp(dy)


def _silu(a):
    return a * jax.nn.sigmoid(a)


def _gelu(x):
    return 0.5 * x * (1.0 + lax.erf(x * 0.7071067811865476))


def _mm(a, b):
    return lax.dot_general(a, b, (((1,), (0,)), ((), ())), preferred_element_type=f32)


def _mm_nt(a, b):
    return lax.dot_general(a, b, (((1,), (1,)), ((), ())), preferred_element_type=f32)


def _mm_tn(a, b):
    return lax.dot_general(a, b, (((0,), (0,)), ((), ())), preferred_element_type=f32)


def _c(x):
    return x.astype(MXU_DTYPE)


def _split(a):
    hi = a.astype(MXU_DTYPE)
    lo = (a - hi.astype(f32)).astype(MXU_DTYPE)
    return hi, lo


def _dot3(a, b, dims):
    ah, al = _split(a)
    bh, bl = _split(b)
    d = lambda p, q: lax.dot_general(p, q, (dims, ((), ())), preferred_element_type=f32)
    return d(ah, bh) + (d(ah, bl) + d(al, bh))


NN, NT, TN = ((1,), (0,)), ((1,), (1,)), ((0,), (0,))


def _slot(px, py, pc):
    return 4 * px + 2 * py + pc


def all_gather_multi(arrs, name):
    return Comm("gather", arrs).alone(name)


def exchange_slots(arrs, name):
    return Comm("exchange", arrs).alone(name)


class Comm:
    def __init__(self, kind, arrs):
        self.kind, self.arrs, self.n = kind, list(arrs), len(arrs)
        hbm = pl.BlockSpec(memory_space=pltpu.HBM)
        self.in_specs = [hbm] * self.n
        self.out_specs = [hbm] * self.n
        lead = (N_DEV,) if kind == "gather" else ()
        self.out_shape = [SDS(lead + tuple(a.shape), a.dtype) for a in self.arrs]
        self.scratch = [pltpu.SemaphoreType.DMA((self.n, 7)), pltpu.SemaphoreType.DMA((self.n, 7)),
                        pltpu.SemaphoreType.DMA((self.n,))]

    def phase(self, p, ins, outs, sems):
        (self._gather if self.kind == "gather" else self._exchange)(p, ins, outs, sems)

    def _gather(self, p, ins, outs, sems):
        send_sems, recv_sems, local_sems = sems
        x, y, c = lax.axis_index("x"), lax.axis_index("y"), lax.axis_index("c")
        me, sibling = (x, y, c), (x, y, 1 - c)
        chips = [(1 - x, y), (x, 1 - y), (1 - x, 1 - y)]

        def copy(a, k, block, to, src=None):
            dst = outs[a].at[_slot(*block)]
            return pltpu.make_async_remote_copy(
                src_ref=dst if src is None else src, dst_ref=dst, send_sem=send_sems.at[a, k],
                recv_sem=recv_sems.at[a, k], device_id=to, device_id_type=MESH)

        mine = [pltpu.make_async_copy(ins[a], outs[a].at[_slot(*me)], local_sems.at[a]) for a in range(self.n)]
        first = [[copy(a, 0, me, sibling, src=ins[a])] +
                 [copy(a, 1 + j, me, (*chip, c), src=ins[a]) for j, chip in enumerate(chips)] for a in range(self.n)]
        passed = [[copy(a, 4 + j, (*chip, c), sibling) for j, chip in enumerate(chips)] for a in range(self.n)]
        if p == 0:
            for a in range(self.n):
                mine[a].start()
            for a in range(self.n):
                for cp in first[a]:
                    cp.start()
        elif p == 1:
            for a in range(self.n):
                for j, chip in enumerate(chips):
                    copy(a, 1 + j, (*chip, c), me).wait_recv()
                    passed[a][j].start()
        else:
            for a in range(self.n):
                copy(a, 0, sibling, me).wait_recv()
                for j, chip in enumerate(chips):
                    copy(a, 4 + j, (*chip, 1 - c), me).wait_recv()
            for a in range(self.n):
                for cp in first[a] + passed[a]:
                    cp.wait_send()
                mine[a].wait()

    def _exchange(self, p, ins, outs, sems):
        send_sems, recv_sems, local_sems = sems
        x, y, c = lax.axis_index("x"), lax.axis_index("y"), lax.axis_index("c")
        me = _slot(x, y, c)
        peers = [(x ^ (k >> 2), y ^ ((k >> 1) & 1), c ^ (k & 1)) for k in range(1, N_DEV)]

        def copy(a, k):
            peer = peers[k - 1]
            return pltpu.make_async_remote_copy(
                src_ref=ins[a].at[_slot(*peer)], dst_ref=outs[a].at[me], send_sem=send_sems.at[a, k - 1],
                recv_sem=recv_sems.at[a, k - 1], device_id=peer, device_id_type=MESH)

        def landed(a, k):
            peer = peers[k - 1]
            return pltpu.make_async_remote_copy(
                src_ref=ins[a].at[me], dst_ref=outs[a].at[_slot(*peer)], send_sem=send_sems.at[a, k - 1],
                recv_sem=recv_sems.at[a, k - 1], device_id=peer, device_id_type=MESH)

        local = [pltpu.make_async_copy(ins[a].at[me], outs[a].at[me], local_sems.at[a]) for a in range(self.n)]
        order = [6, 7, 2, 3, 4, 5, 1]
        if p == 0:
            for a in range(self.n):
                local[a].start()
            for a in range(self.n):
                for k in order:
                    copy(a, k).start()
        elif p == 2:
            for a in range(self.n):
                for k in order:
                    copy(a, k).wait_send()
                    landed(a, k).wait_recv()
                local[a].wait()

    def alone(self, name):
        n = self.n

        def body(*refs):
            for p in range(3):
                self.phase(p, refs[:n], refs[n:2 * n], refs[2 * n:])

        return pl.pallas_call(body, name=name, out_shape=tuple(self.out_shape), in_specs=self.in_specs,
                              out_specs=tuple(self.out_specs), scratch_shapes=self.scratch)(*self.arrs)


def hosted_call(body, comm, steps, *, name, grid, in_specs, out_specs, out_shape, scratch_shapes, args):
    if comm is None:
        outs = pl.pallas_call(body, name=name, grid=grid, in_specs=in_specs, out_specs=tuple(out_specs),
                              out_shape=tuple(out_shape), scratch_shapes=scratch_shapes,
                              compiler_params=_params(len(grid)))(*args)
        return outs, None
    ni, no, ns, cn = len(in_specs), len(out_specs), len(scratch_shapes), comm.n

    def both(*refs):
        h_in, c_in = refs[:ni], refs[ni:ni + cn]
        h_out, c_out = refs[ni + cn:ni + cn + no], refs[ni + cn + no:ni + 2 * cn + no]
        h_scr, c_scr = refs[ni + 2 * cn + no:ni + 2 * cn + no + ns], refs[ni + 2 * cn + no + ns:]
        when = steps()
        pl.when(when[0])(lambda: comm.phase(0, c_in, c_out, c_scr))
        body(*h_in, *h_out, *h_scr)
        pl.when(when[1])(lambda: comm.phase(1, c_in, c_out, c_scr))
        pl.when(when[2])(lambda: comm.phase(2, c_in, c_out, c_scr))

    outs = pl.pallas_call(
        both, name=name, grid=grid, in_specs=list(in_specs) + comm.in_specs,
        out_specs=tuple(out_specs) + tuple(comm.out_specs), out_shape=tuple(out_shape) + tuple(comm.out_shape),
        scratch_shapes=list(scratch_shapes) + comm.scratch, compiler_params=_params(len(grid)),
    )(*args, *comm.arrs)
    return outs[:no], outs[no:]


def _grid_steps(n_outer, n_inner=1):
    total = n_outer * n_inner

    def steps():
        t = pl.program_id(0) * n_inner + (pl.program_id(1) if n_inner > 1 else 0)
        return t == 0, t == (total * 5) // 8, t == total - 1
    return steps


def ffn_fwd(x, gpre, gpost, wg, wu, wd, name, comm=None):
    S, D = x.shape
    nj, F8 = wg.shape[0], wg.shape[-1]
    tm = _row_tile(S, FFN_ROWS_FWD)

    def body(x_ref, gpre_ref, gpost_ref, wg_ref, wu_ref, wd_ref, xo_ref, h_ref, p_ref, q_ref, t_ref, y_ref):
        j = pl.program_id(1)

        @pl.when(j == 0)
        def _():
            h_ref[...] = _rms(x_ref[...], gpre_ref[...]).astype(h_ref.dtype)
            y_ref[...] = jnp.zeros_like(y_ref)

        h = h_ref[...]
        a = _mm(h, wg_ref[...])
        b = _mm(h, wu_ref[...])
        s = jax.nn.sigmoid(a)
        q = a * s
        p_ref[...] = (b * (s + q * (1.0 - s))).astype(p_ref.dtype)
        q_ref[...] = q.astype(q_ref.dtype)
        t = (q * b).astype(t_ref.dtype)
        t_ref[...] = t
        y_ref[...] += _mm(t, wd_ref[...])

        @pl.when(j == nj - 1)
        def _():
            xo_ref[...] = x_ref[...] + 0.5 * _rms(y_ref[...], gpost_ref[...])

    row = pl.BlockSpec((tm, D), lambda i, j: (i, 0))
    vec = pl.BlockSpec((1, D), lambda i, j: (0, 0))
    wcol = pl.BlockSpec((None, D, F8), lambda i, j: (j, 0, 0))
    wrow = pl.BlockSpec((None, F8, D), lambda i, j: (j, 0, 0))
    hid = pl.BlockSpec((None, tm, F8), lambda i, j: (j, i, 0))
    return hosted_call(
        body, comm, _grid_steps(S // tm, nj), name=name, grid=(S // tm, nj),
        in_specs=[row, vec, vec, wcol, wcol, wrow],
        out_specs=(row, row, hid, hid, hid, row),
        out_shape=(SDS((S, D), f32), SDS((S, D), MXU_DTYPE), SDS((nj, S, F8), MXU_DTYPE),
                   SDS((nj, S, F8), MXU_DTYPE), SDS((nj, S, F8), MXU_DTYPE), SDS((S, D), f32)),
        scratch_shapes=[], args=(x, gpre, gpost, wg, wu, wd))


def ffn_bwd_dx(dxo, x, y, p, q, gpre, gpost, wg, wu, wd, name, comm=None):
    S, D = x.shape
    nj, F8 = wg.shape[0], wg.shape[-1]
    tm = _row_tile(S, FFN_ROWS_BWD)

    def body(dxo_ref, x_ref, y_ref, p_ref, q_ref, gpre_ref, gpost_ref, wg_ref, wu_ref, wd_ref,
             dx_ref, da_ref, db_ref, dy_ref, dgpre_ref, dgpost_ref, dh_ref):
        i, j = pl.program_id(0), pl.program_id(1)

        @pl.when(j == 0)
        def _():
            @pl.when(i == 0)
            def _():
                dgpre_ref[...] = jnp.zeros_like(dgpre_ref)
                dgpost_ref[...] = jnp.zeros_like(dgpost_ref)

            dy, dg = _rms_bwd(y_ref[...], gpost_ref[...], 0.5 * dxo_ref[...])
            dy_ref[...] = dy.astype(dy_ref.dtype)
            dgpost_ref[...] += dg
            dh_ref[...] = jnp.zeros_like(dh_ref)

        half = tm // 2 if tm % 16 == 0 else tm
        for r0 in range(0, tm, half):
            rs = slice(r0, r0 + half)
            dt = _mm_nt(dy_ref[rs, :], wd_ref[...])
            da = (dt * p_ref[rs, :].astype(f32)).astype(da_ref.dtype)
            db = (dt * q_ref[rs, :].astype(f32)).astype(db_ref.dtype)
            da_ref[rs, :] = da
            db_ref[rs, :] = db
            dh_ref[rs, :] += _mm_nt(da, wg_ref[...]) + _mm_nt(db, wu_ref[...])

        @pl.when(j == nj - 1)
        def _():
            dxx, dg = _rms_bwd(x_ref[...], gpre_ref[...], dh_ref[...])
            dx_ref[...] = dxo_ref[...] + dxx
            dgpre_ref[...] += dg

    row = pl.BlockSpec((tm, D), lambda i, j: (i, 0))
    vec = pl.BlockSpec((1, D), lambda i, j: (0, 0))
    wcol = pl.BlockSpec((None, D, F8), lambda i, j: (j, 0, 0))
    wrow = pl.BlockSpec((None, F8, D), lambda i, j: (j, 0, 0))
    hid = pl.BlockSpec((None, tm, F8), lambda i, j: (j, i, 0))
    return hosted_call(
        body, comm, _grid_steps(S // tm, nj), name=name, grid=(S // tm, nj),
        in_specs=[row, row, row, hid, hid, vec, vec, wcol, wcol, wrow],
        out_specs=(row, hid, hid, row, vec, vec),
        out_shape=(SDS((S, D), f32), SDS((nj, S, F8), MXU_DTYPE), SDS((nj, S, F8), MXU_DTYPE),
                   SDS((S, D), MXU_DTYPE), SDS((1, D), f32), SDS((1, D), f32)),
        scratch_shapes=[pltpu.VMEM((tm, D), f32)], args=(dxo, x, y, p, q, gpre, gpost, wg, wu, wd))


def ffn_bwd_dw(h, dy, t, da, db, name):
    S, D = h.shape
    NS, F8 = t.shape[0], t.shape[-1]
    per = N_DEV // NS
    w8 = F8 // per
    tm = _row_tile(S, FFN_ROWS_DW)
    ni = S // tm

    def body(h_ref, dy_ref, t_ref, da_ref, db_ref, dwg_ref, dwu_ref, dwd_ref, accg, accu, accd):
        i = pl.program_id(1)

        @pl.when(i == 0)
        def _():
            accg[...] = jnp.zeros_like(accg)
            accu[...] = jnp.zeros_like(accu)
            accd[...] = jnp.zeros_like(accd)

        hh = h_ref[...]
        accg[...] += _mm_tn(hh, da_ref[...])
        accu[...] += _mm_tn(hh, db_ref[...])
        accd[...] += _mm_tn(t_ref[...], dy_ref[...])

        @pl.when(i == ni - 1)
        def _():
            for k in range(per):
                ks = slice(k * w8, (k + 1) * w8)
                dwg_ref[k] = accg[:, ks].astype(dwg_ref.dtype)
                dwu_ref[k] = accu[:, ks].astype(dwu_ref.dtype)
                dwd_ref[k] = accd[ks, :].astype(dwd_ref.dtype)

    row = pl.BlockSpec((tm, D), lambda j, i: (i, 0))
    hid = pl.BlockSpec((None, tm, F8), lambda j, i: (j, i, 0))
    wcol = pl.BlockSpec((per, D, w8), lambda j, i: (j, 0, 0))
    wrow = pl.BlockSpec((per, w8, D), lambda j, i: (j, 0, 0))
    return pl.pallas_call(
        body, name=name, grid=(NS, ni),
        in_specs=[row, row, hid, hid, hid],
        out_specs=(wcol, wcol, wrow),
        out_shape=(SDS((N_DEV, D, w8), MXU_DTYPE), SDS((N_DEV, D, w8), MXU_DTYPE), SDS((N_DEV, w8, D), MXU_DTYPE)),
        scratch_shapes=[pltpu.VMEM((D, F8), f32), pltpu.VMEM((D, F8), f32), pltpu.VMEM((F8, D), f32)],
        compiler_params=_params(2),
    )(h, dy, t, da, db)


def rms_mm(x, g, w, w2, name, tn=1024):
    S, D = x.shape
    N = w.shape[1]
    tm = _row_tile(S, PROJ_ROWS)
    tn = _row_tile(N, tn)
    has2 = w2 is not None

    def body(*refs):
        if has2:
            x_ref, g_ref, w_ref, w2_ref, h_ref, o_ref, o2_ref = refs
        else:
            x_ref, g_ref, w_ref, h_ref, o_ref = refs
        j = pl.program_id(1)

        @pl.when(j == 0)
        def _():
            h = _rms(x_ref[...], g_ref[...]).astype(h_ref.dtype)
            h_ref[...] = h
            if has2:
                o2_ref[...] = _mm(h, w2_ref[...])

        o_ref[...] = _mm(h_ref[...], w_ref[...])

    row = pl.BlockSpec((tm, D), lambda i, j: (i, 0))
    in_specs = [row, pl.BlockSpec((1, D), lambda i, j: (0, 0)), pl.BlockSpec((D, tn), lambda i, j: (0, j))]
    out_specs = [row, pl.BlockSpec((tm, tn), lambda i, j: (i, j))]
    out_shape = [SDS((S, D), MXU_DTYPE), SDS((S, N), f32)]
    args = [x, g, w]
    if has2:
        in_specs.append(pl.BlockSpec((D, w2.shape[1]), lambda i, j: (0, 0)))
        out_specs.append(pl.BlockSpec((tm, w2.shape[1]), lambda i, j: (i, 0)))
        out_shape.append(SDS((S, w2.shape[1]), f32))
        args.append(w2)
    return pl.pallas_call(
        body, name=name, grid=(S // tm, N // tn), in_specs=in_specs, out_specs=tuple(out_specs),
        out_shape=tuple(out_shape), compiler_params=_params(2),
    )(*args)


def mm_bwd_dx(dres, x, g, dy, w, dy2, w2, name, tk=1024, comm=None):
    S, D = x.shape
    K = dy.shape[1]
    tm = _row_tile(S, PROJ_ROWS)
    tk = _row_tile(K, tk)
    nk = K // tk
    has2 = dy2 is not None

    def body(*refs):
        if has2:
            dres_ref, x_ref, g_ref, dy_ref, w_ref, dy2_ref, w2_ref, dx_ref, dg_ref, dh_ref = refs
        else:
            dres_ref, x_ref, g_ref, dy_ref, w_ref, dx_ref, dg_ref, dh_ref = refs
        i, k = pl.program_id(0), pl.program_id(1)

        @pl.when(k == 0)
        def _():
            @pl.when(i == 0)
            def _():
                dg_ref[...] = jnp.zeros_like(dg_ref)

            if has2:
                dh_ref[...] = _mm_nt(dy2_ref[...], w2_ref[...])
            else:
                dh_ref[...] = jnp.zeros_like(dh_ref)

        dh_ref[...] += _mm_nt(dy_ref[...], w_ref[...])

        @pl.when(k == nk - 1)
        def _():
            dxx, dg = _rms_bwd(x_ref[...], g_ref[...], dh_ref[...])
            dx_ref[...] = dres_ref[...] + dxx
            dg_ref[...] += dg

    row = pl.BlockSpec((tm, D), lambda i, k: (i, 0))
    vec = pl.BlockSpec((1, D), lambda i, k: (0, 0))
    in_specs = [row, row, vec, pl.BlockSpec((tm, tk), lambda i, k: (i, k)), pl.BlockSpec((D, tk), lambda i, k: (0, k))]
    args = [dres, x, g, dy, w]
    if has2:
        in_specs += [pl.BlockSpec((tm, dy2.shape[1]), lambda i, k: (i, 0)),
                     pl.BlockSpec((D, w2.shape[1]), lambda i, k: (0, 0))]
        args += [dy2, w2]
    return hosted_call(
        body, comm, _grid_steps(S // tm, nk), name=name, grid=(S // tm, nk), in_specs=in_specs, out_specs=(row, vec),
        out_shape=(SDS((S, D), f32), SDS((1, D), f32)), scratch_shapes=[pltpu.VMEM((tm, D), f32)], args=args)


def tn_mm(a, b, name, tn=512, slot_major=False):
    S, K1 = a.shape
    N = b.shape[1]
    tm = _row_tile(S, TN_ROWS)
    tn = _row_tile(N, tn)
    ni = S // tm

    def body(a_ref, b_ref, o_ref, acc):
        i = pl.program_id(1)

        @pl.when(i == 0)
        def _():
            acc[...] = jnp.zeros_like(acc)

        acc[...] += _mm_tn(a_ref[...], b_ref[...])

        @pl.when(i == ni - 1)
        def _():
            o_ref[...] = acc[...].astype(o_ref.dtype)

    if slot_major:
        out_spec, out_shape = pl.BlockSpec((None, K1, tn), lambda j, i: (j, 0, 0)), SDS((N // tn, K1, tn), MXU_DTYPE)
    else:
        out_spec, out_shape = pl.BlockSpec((K1, tn), lambda j, i: (0, j)), SDS((K1, N), MXU_DTYPE)
    return pl.pallas_call(
        body, name=name, grid=(N // tn, ni),
        in_specs=[pl.BlockSpec((tm, K1), lambda j, i: (i, 0)), pl.BlockSpec((tm, tn), lambda j, i: (i, j))],
        out_specs=out_spec, out_shape=out_shape,
        scratch_shapes=[pltpu.VMEM((K1, tn), f32)], compiler_params=_params(2),
    )(a, b)


CONV_ROWS = 512


def _shift_down(cur, prev8, s):
    r = pltpu.roll(cur, s, 0)
    row = lax.broadcasted_iota(jnp.int32, (8, cur.shape[1]), 0)
    top = jnp.where(row < s, pltpu.roll(prev8, s, 0), r[0:8])
    return jnp.concatenate([top, r[8:]], axis=0)


def _shift_up(cur, next8, s):
    n = cur.shape[0]
    r = pltpu.roll(cur, n - s, 0)
    row = lax.broadcasted_iota(jnp.int32, (8, cur.shape[1]), 0)
    bot = jnp.where(row >= 8 - s, pltpu.roll(next8, 8 - s, 0), r[n - 8:])
    return jnp.concatenate([r[:n - 8], bot], axis=0)


def _conv_taps(cur, prev8):
    return [_shift_down(cur, prev8, 3), _shift_down(cur, prev8, 2), _shift_down(cur, prev8, 1), cur]


def _act_qk(c):
    a = _silu(c)
    return a * lax.rsqrt(jnp.sum(a * a, axis=-1, keepdims=True) + L2_EPS)


def dn_prep(proj, conv_w, name):
    S = proj.shape[0]
    W = conv_w.shape[1] // 3
    nh = W // HEAD
    R = _row_tile(S, CONV_ROWS)

    def body(p_ref, w_ref, o_ref):
        j = pl.program_id(0)
        w = w_ref[...]

        def rows(r, prev8):
            cur = p_ref[pl.ds(r, R), :]
            taps = _conv_taps(cur, prev8)
            cv = taps[0] * w[0:1] + taps[1] * w[1:2] + taps[2] * w[2:3] + taps[3] * w[3:4]

            @pl.when(j < 2 * nh)
            def _():
                o_ref[pl.ds(r, R), :] = _act_qk(cv)

            @pl.when(j >= 2 * nh)
            def _():
                o_ref[pl.ds(r, R), :] = _silu(cv)

        rows(0, jnp.zeros((8, HEAD), f32))

        @pl.loop(1, S // R)
        def _(t):
            r = pl.multiple_of(t * R, R)
            rows(r, p_ref[pl.ds(r - 8, 8), :])

    return pl.pallas_call(
        body, name=name, grid=(3 * nh,),
        in_specs=[pl.BlockSpec((S, HEAD), lambda j: (0, j)), pl.BlockSpec((CONV_K, HEAD), lambda j: (0, j))],
        out_specs=pl.BlockSpec((None, S, HEAD), lambda j: (j // nh, 0, j % nh)),
        out_shape=SDS((3, S, W), f32), compiler_params=_params(1),
    )(proj, conv_w)


def dn_prep_bwd(proj, conv_w, dqkv, dz, name, comm=None):
    S = proj.shape[0]
    W = conv_w.shape[1] // 3
    nh = W // HEAD
    nq = 3 * nh
    R = _row_tile(S, CONV_ROWS)
    nr = S // R

    def body(p_ref, w_ref, dq_ref, dz_ref, dp_ref, dw_ref, dc_ref):
        j = pl.program_id(0)

        @pl.when(j >= nq)
        def _():
            dp_ref[...] = dz_ref[...].astype(dp_ref.dtype)

        @pl.when(j < nq)
        def _():
            w = w_ref[...]
            dw_ref[...] = jnp.zeros_like(dw_ref)

            def rows(r, prev8):
                cur = p_ref[pl.ds(r, R), :]
                taps = _conv_taps(cur, prev8)
                cv = taps[0] * w[0:1] + taps[1] * w[1:2] + taps[2] * w[2:3] + taps[3] * w[3:4]
                dn = dq_ref[pl.ds(r, R), :]

                @pl.when(j < 2 * nh)
                def _():
                    dc_ref[pl.ds(r, R), :] = jax.vjp(_act_qk, cv)[1](dn)[0]

                @pl.when(j >= 2 * nh)
                def _():
                    dc_ref[pl.ds(r, R), :] = jax.vjp(_silu, cv)[1](dn)[0]

                dc = dc_ref[pl.ds(r, R), :]
                dw_ref[...] += jnp.concatenate(
                    [jnp.sum(dc * taps[q], axis=0, keepdims=True) for q in range(CONV_K)], axis=0)

            rows(0, jnp.zeros((8, HEAD), f32))

            @pl.loop(1, nr)
            def _(t):
                r = pl.multiple_of(t * R, R)
                rows(r, p_ref[pl.ds(r - 8, 8), :])

            def back(r, next8):
                dc = dc_ref[pl.ds(r, R), :]
                dx = dc * w[3:4]
                for s in (1, 2, 3):
                    dx = dx + _shift_up(dc, next8, s) * w[3 - s:4 - s]
                dp_ref[pl.ds(r, R), :] = dx.astype(dp_ref.dtype)

            @pl.loop(0, nr - 1)
            def _(t):
                r = pl.multiple_of(t * R, R)
                back(r, dc_ref[pl.ds(r + R, 8), :])

            back((nr - 1) * R, jnp.zeros((8, HEAD), f32))

    clamp = lambda j: jnp.minimum(j, nq - 1)
    return hosted_call(
        body, comm, _grid_steps(4 * nh), name=name, grid=(4 * nh,),
        in_specs=[pl.BlockSpec((S, HEAD), lambda j: (0, clamp(j))),
                  pl.BlockSpec((CONV_K, HEAD), lambda j: (0, clamp(j))),
                  pl.BlockSpec((None, S, HEAD), lambda j: (clamp(j) // nh, 0, clamp(j) % nh)),
                  pl.BlockSpec((S, HEAD), lambda j: (0, jnp.maximum(j - nq, 0)))],
        out_specs=(pl.BlockSpec((S, HEAD), lambda j: (0, j)), pl.BlockSpec((CONV_K, HEAD), lambda j: (0, clamp(j)))),
        out_shape=(SDS((S, 4 * W), MXU_DTYPE), SDS((CONV_K, 3 * W), f32)),
        scratch_shapes=[pltpu.VMEM((S, HEAD), f32)], args=(proj, conv_w, dqkv, dz))


def _lane_pick(x, lane):
    sel = lax.broadcasted_iota(jnp.int32, x.shape, 1) == lane
    return jnp.broadcast_to(jnp.sum(jnp.where(sel, x, 0.0), axis=1, keepdims=True), x.shape)


CUM_ROWS = 256


def _sel_mm(m01, x):
    m = _c(m01)
    d = lambda p: lax.dot_general(m, p, (NN, ((), ())), preferred_element_type=f32)
    h1, h2, h3 = _pieces3(x)
    return (d(h1) + d(h2)) + d(h3)


def _chunk_cumsum_matrix(n, transpose):
    r, c = lax.broadcasted_iota(jnp.int32, (n, n), 0), lax.broadcasted_iota(jnp.int32, (n, n), 1)
    sh = int(math.log2(DN_CHUNK))
    same = (r >> sh) == (c >> sh)
    return jnp.where(same & ((r <= c) if transpose else (r >= c)), 1.0, 0.0).astype(f32)


def _gates_by_lane(H, p, al, dt):
    lane = lax.broadcasted_iota(jnp.int32, p.shape, 1)
    g = -jnp.exp(al) * jax.nn.softplus(p + dt)
    return jnp.where(lane < H, jax.nn.sigmoid(p), jnp.where(lane < 2 * H, g, 0.0))


def dn_gates(pba, al, dt, H, name):
    S = pba.shape[0]
    R = _row_tile(S, CUM_ROWS)

    def body(p_ref, al_ref, dt_ref, o_ref):
        raw = _gates_by_lane(H, p_ref[...], al_ref[...], dt_ref[...])
        lane = lax.broadcasted_iota(jnp.int32, raw.shape, 1)
        o_ref[...] = jnp.where(lane < H, raw, _sel_mm(_chunk_cumsum_matrix(R, False), raw))

    blk = pl.BlockSpec((R, HEAD), lambda i: (i, 0))
    par = pl.BlockSpec((1, HEAD), lambda i: (0, 0))
    return pl.pallas_call(body, name=name, grid=(S // R,), in_specs=[blk, par, par], out_specs=blk,
                          out_shape=SDS((S, HEAD), f32), compiler_params=_params(1))(pba, al, dt)


def dn_gates_bwd(pba, al, dt, dgates, H, name):
    S = pba.shape[0]
    R = _row_tile(S, CUM_ROWS)

    def body(p_ref, al_ref, dt_ref, dg_ref, dp_ref, dal_ref, ddt_ref):
        @pl.when(pl.program_id(0) == 0)
        def _():
            dal_ref[...] = jnp.zeros_like(dal_ref)
            ddt_ref[...] = jnp.zeros_like(ddt_ref)

        d = dg_ref[...]
        lane = lax.broadcasted_iota(jnp.int32, d.shape, 1)
        d = jnp.where(lane < H, d, _sel_mm(_chunk_cumsum_matrix(R, True), d))
        _, vjp = jax.vjp(functools.partial(_gates_by_lane, H), p_ref[...], al_ref[...], dt_ref[...])
        dp, dal, ddt = vjp(d)
        dp_ref[...] = dp.astype(dp_ref.dtype)
        dal_ref[...] += dal
        ddt_ref[...] += ddt

    blk = pl.BlockSpec((R, HEAD), lambda i: (i, 0))
    par = pl.BlockSpec((1, HEAD), lambda i: (0, 0))
    return pl.pallas_call(
        body, name=name, grid=(S // R,), in_specs=[blk, par, par, blk], out_specs=(blk, par, par),
        out_shape=(SDS((S, HEAD), MXU_DTYPE), SDS((1, HEAD), f32), SDS((1, HEAD), f32)), compiler_params=_params(1),
    )(pba, al, dt, dgates)


def _bdot(dims):
    back = {NN: ((NT, 'gb'), (TN, 'ag')), NT: ((NN, 'gb'), (TN, 'ga')), TN: ((NT, 'bg'), (NN, 'ag'))}[dims]
    d = lambda p, q, dm: lax.dot_general(_c(p), _c(q), (dm, ((), ())), preferred_element_type=f32)

    @jax.custom_vjp
    def f(a, b):
        return d(a, b, dims)

    def fwd(a, b):
        return d(a, b, dims), (a, b)

    def bwd(res, g):
        v = {'a': res[0], 'b': res[1], 'g': g}
        (da_dims, da_ops), (db_dims, db_ops) = back
        return d(v[da_ops[0]], v[da_ops[1]], da_dims), d(v[db_ops[0]], v[db_ops[1]], db_dims)

    f.defvjp(fwd, bwd)
    return f, lambda a, b: d(a, b, dims)


_BDOT = {dims: _bdot(dims) for dims in (NN, NT, TN)}


def _tri_inv_multi(Ls):
    n = Ls[0].shape[0]
    eye = jnp.where(lax.broadcasted_iota(jnp.int32, (n, n), 0) == lax.broadcasted_iota(jnp.int32, (n, n), 1), 1.0, 0.0)
    P = tuple(-L for L in Ls)
    T = tuple(eye + p for p in P)
    for _ in range(int(math.log2(n)) - 1):
        P = tuple(_dot3(p, p, NN) for p in P)
        T = tuple(t + _dot3(t, p, NN) for t, p in zip(T, P))
    return T


@jax.custom_vjp
def _tri_inv_multi_vjp(Ls):
    return _tri_inv_multi(Ls)


def _tri_inv_fwd(Ls):
    T = _tri_inv_multi(Ls)
    return T, T


def _tri_inv_bwd(T, dT):
    X = tuple(_dot3(d, t, NT) for d, t in zip(dT, T))
    return (tuple(-_dot3(t, x, TN) for t, x in zip(T, X)),)


_tri_inv_multi_vjp.defvjp(_tri_inv_fwd, _tri_inv_bwd)


def _pieces3(x):
    h1 = x.astype(MXU_DTYPE)
    r1 = x - h1.astype(f32)
    h2 = r1.astype(MXU_DTYPE)
    return h1, h2, (r1 - h2.astype(f32)).astype(MXU_DTYPE)


def _row_bcast_impl(sel_row, gc):
    s = _c(sel_row)
    d = lambda p: lax.dot_general(s, p, (NT, ((), ())), preferred_element_type=f32)
    h1, h2, h3 = _pieces3(gc)
    return (d(h1) + d(h2)) + d(h3)


def _row_bcast_bwd(sel_row, d):
    s = _c(sel_row)
    hi, lo = _split(d)
    t = lambda p: lax.dot_general(p, s, (TN, ((), ())), preferred_element_type=f32)
    return jnp.zeros_like(sel_row), t(hi) + t(lo)


_row_bcast = jax.custom_vjp(_row_bcast_impl)
_row_bcast.defvjp(lambda sel_row, gc: (_row_bcast_impl(sel_row, gc), sel_row), _row_bcast_bwd)


def _col_bcast_impl(gc):
    return gc[:, :DN_CHUNK]


def _col_bcast_bwd(_, d):
    return (jnp.broadcast_to(jnp.sum(d, axis=1, keepdims=True) * (1.0 / HEAD), (d.shape[0], HEAD)),)


_col_bcast = jax.custom_vjp(_col_bcast_impl)
_col_bcast.defvjp(lambda gc: (_col_bcast_impl(gc), None), _col_bcast_bwd)


def _last_row_bcast(n):
    def impl(gc):
        return jnp.broadcast_to(gc[DN_CHUNK - 1:DN_CHUNK, :], (n, HEAD))

    def bwd(_, d):
        row = lax.broadcasted_iota(jnp.int32, (DN_CHUNK, HEAD), 0)
        return (jnp.where(row == DN_CHUNK - 1, jnp.sum(d, axis=0, keepdims=True), 0.0),)

    f = jax.custom_vjp(impl)
    f.defvjp(lambda gc: (impl(gc), None), bwd)
    return impl, f


_LAST_C, _LAST_H = _last_row_bcast(DN_CHUNK), _last_row_bcast(HEAD)


def _chunk_consts():
    C = DN_CHUNK
    io = lambda shape, ax: lax.broadcasted_iota(jnp.int32, shape, ax)
    one = lambda m: jnp.where(m, 1.0, 0.0).astype(f32)
    r, c = io((C, C), 0), io((C, C), 1)
    return dict(causal=r >= c, strict=r > c, sel_row=one(io((C, HEAD), 1) == 0))


def _chunk_fn(kc, diff, q, k, v, gc, bB, S0):
    i = 0 if diff else 1
    mm, mm_nt, mm_tn = _BDOT[NN][i], _BDOT[NT][i], _BDOT[TN][i]
    tri = _tri_inv_multi_vjp if diff else _tri_inv_multi
    each = lambda f, *ls: tuple(f(*a) for a in zip(*ls))
    gcol = each(_col_bcast if diff else _col_bcast_impl, gc)
    grow = each(lambda g: (_row_bcast if diff else _row_bcast_impl)(kc['sel_row'], g), gc)
    glc = each(_LAST_C[i ^ 1], gc)
    glh = each(_LAST_H[i ^ 1], gc)
    decay = each(lambda a, b: jnp.where(kc['causal'], jnp.exp(jnp.where(kc['causal'], a - b, 0.0)), 0.0), gcol, grow)
    kb = each(lambda a, b: a * b, k, bB)
    vb = each(lambda a, b: a * b, v, bB)
    egc = each(jnp.exp, gc)
    kk = each(mm_nt, kb, k)
    T = tri(each(lambda a, d: jnp.where(kc['strict'], a * d, 0.0), kk, decay))
    u = each(mm, T, vb)
    w = each(mm, T, each(lambda a, b: a * b, kb, egc))
    qs = each(lambda a: a * (HEAD ** -0.5), q)
    qk = each(mm_nt, qs, k)
    attn = each(lambda a, d: jnp.where(kc['causal'], a * d, 0.0), qk, decay)
    wS = each(mm, w, S0)
    qS = each(mm, each(lambda a, b: a * b, qs, egc), S0)
    v_new = each(lambda a, b: a - b, u, wS)
    o = each(lambda a, b: a + b, qS, each(mm, attn, v_new))
    kdec = each(lambda a, gl, g: a * jnp.exp(gl - g), k, glc, gc)
    S1 = each(lambda s, gl, kv: s * jnp.exp(gl) + kv, S0, glh, each(mm_tn, kdec, v_new))
    return o, S1


def _chunks_per_step(N):
    return 2 if N % 2 == 0 else 1


def _heads_per_block(H):
    return 8 if H % 8 == 0 else (4 if H % 4 == 0 else 1)


def dn_chunk_fwd(qkv, gates, name, comm=None):
    _, S, W = qkv.shape
    H, C = W // HEAD, DN_CHUNK
    N, HB = S // C, _heads_per_block(H)
    assert HB == H
    CPS = _chunks_per_step(N)

    def body(q_ref, k_ref, v_ref, g_ref, o_ref, st_ref, s_scr):
        @pl.when(pl.program_id(1) == 0)
        def _():
            s_scr[...] = jnp.zeros_like(s_scr)

        kc = _chunk_consts()
        sls = [slice(hh * HEAD, (hh + 1) * HEAD) for hh in range(HB)]
        St = tuple(s_scr[hh] for hh in range(HB))
        for c in range(CPS):
            rows = slice(c * C, (c + 1) * C)
            heads = lambda ref: tuple(ref[rows, sl] for sl in sls)
            gr = g_ref[rows, :]
            for hh in range(HB):
                st_ref[c, hh] = St[hh]
            o, St = _chunk_fn(kc, False, heads(q_ref), heads(k_ref), heads(v_ref),
                              tuple(_lane_pick(gr, H + hh) for hh in range(HB)),
                              tuple(_lane_pick(gr, hh) for hh in range(HB)), St)
            for hh in range(HB):
                o_ref[rows, sls[hh]] = o[hh]
        for hh in range(HB):
            s_scr[hh] = St[hh]

    part = lambda p: pl.BlockSpec((None, CPS * C, HB * HEAD), lambda hb, n: (p, n, hb))
    return hosted_call(
        body, comm, _grid_steps(H // HB, N // CPS), name=name, grid=(H // HB, N // CPS),
        in_specs=[part(0), part(1), part(2), pl.BlockSpec((CPS * C, HEAD), lambda hb, n: (n, 0))],
        out_specs=(pl.BlockSpec((CPS * C, HB * HEAD), lambda hb, n: (n, hb)),
                   pl.BlockSpec((CPS, HB, HEAD, HEAD), lambda hb, n: (n, hb, 0, 0))),
        out_shape=(SDS((S, W), f32), SDS((N, H, HEAD, HEAD), f32)),
        scratch_shapes=[pltpu.VMEM((HB, HEAD, HEAD), f32)], args=(qkv, qkv, qkv, gates))


def dn_chunk_bwd(qkv, gates, states, do, name, comm=None):
    _, S, W = qkv.shape
    H, C = W // HEAD, DN_CHUNK
    N, HB = S // C, _heads_per_block(H)
    assert HB == H
    CPS = _chunks_per_step(N)
    NB = N // CPS

    def body(q_ref, k_ref, v_ref, g_ref, st_ref, do_ref, dqkv_ref, dg_ref, ds_scr):
        @pl.when(pl.program_id(1) == 0)
        def _():
            ds_scr[...] = jnp.zeros_like(ds_scr)

        kc = _chunk_consts()
        sls = [slice(hh * HEAD, (hh + 1) * HEAD) for hh in range(HB)]
        dSt = tuple(ds_scr[hh] for hh in range(HB))
        for c in reversed(range(CPS)):
            rows = slice(c * C, (c + 1) * C)
            heads = lambda ref: tuple(ref[rows, sl] for sl in sls)
            gr = g_ref[rows, :]
            _, vjp = jax.vjp(functools.partial(_chunk_fn, kc, True), heads(q_ref), heads(k_ref), heads(v_ref),
                             tuple(_lane_pick(gr, H + hh) for hh in range(HB)),
                             tuple(_lane_pick(gr, hh) for hh in range(HB)), tuple(st_ref[c, hh] for hh in range(HB)))
            dq, dk, dv, dg, db, dSt = vjp((heads(do_ref), dSt))
            lane = lax.broadcasted_iota(jnp.int32, (C, HEAD), 1)
            dgr = jnp.zeros((C, HEAD), f32)
            for hh in range(HB):
                dqkv_ref[0, rows, sls[hh]] = dq[hh]
                dqkv_ref[1, rows, sls[hh]] = dk[hh]
                dqkv_ref[2, rows, sls[hh]] = dv[hh]
                dgr = dgr + jnp.where(lane == hh, jnp.sum(db[hh], axis=1, keepdims=True), 0.0)
                dgr = dgr + jnp.where(lane == H + hh, jnp.sum(dg[hh], axis=1, keepdims=True), 0.0)
            dg_ref[rows, :] = dgr
        for hh in range(HB):
            ds_scr[hh] = dSt[hh]

    rev = lambda n: NB - 1 - n
    part = lambda p: pl.BlockSpec((None, CPS * C, HB * HEAD), lambda hb, n: (p, rev(n), hb))
    gate = pl.BlockSpec((CPS * C, HEAD), lambda hb, n: (rev(n), 0))
    return hosted_call(
        body, comm, _grid_steps(H // HB, NB), name=name, grid=(H // HB, NB),
        in_specs=[part(0), part(1), part(2), gate,
                  pl.BlockSpec((CPS, HB, HEAD, HEAD), lambda hb, n: (rev(n), hb, 0, 0)),
                  pl.BlockSpec((CPS * C, HB * HEAD), lambda hb, n: (rev(n), hb))],
        out_specs=(pl.BlockSpec((3, CPS * C, HB * HEAD), lambda hb, n: (0, rev(n), hb)), gate),
        out_shape=(SDS((3, S, W), f32), SDS((S, HEAD), f32)),
        scratch_shapes=[pltpu.VMEM((HB, HEAD, HEAD), f32)], args=(qkv, qkv, qkv, gates, states, do))


def _gate_norm(o, z, ng):
    return _rms(o, ng) * _silu(z)


def dn_out(o, proj, ng, wout, x1, g3, name):
    S, W = o.shape
    D = x1.shape[1]
    nh = W // HEAD
    tm = _row_tile(S, 256)

    def body(o_ref, z_ref, ng_ref, w_ref, x_ref, g_ref, xo_ref, m_ref, og_ref):
        for h in range(nh):
            sl = slice(h * HEAD, (h + 1) * HEAD)
            og_ref[:, sl] = _gate_norm(o_ref[:, sl], z_ref[:, sl], ng_ref[...]).astype(og_ref.dtype)
        m = _mm(og_ref[...], w_ref[...])
        m_ref[...] = m
        xo_ref[...] = x_ref[...] + _rms(m, g_ref[...])

    rw = pl.BlockSpec((tm, W), lambda i: (i, 0))
    rd = pl.BlockSpec((tm, D), lambda i: (i, 0))
    return pl.pallas_call(
        body, name=name, grid=(S // tm,),
        in_specs=[rw, pl.BlockSpec((tm, W), lambda i: (i, 3)), pl.BlockSpec((1, HEAD), lambda i: (0, 0)),
                  pl.BlockSpec((W, D), lambda i: (0, 0)), rd, pl.BlockSpec((1, D), lambda i: (0, 0))],
        out_specs=(rd, rd, rw),
        out_shape=(SDS((S, D), f32), SDS((S, D), f32), SDS((S, W), MXU_DTYPE)), compiler_params=_params(1),
    )(o, proj, ng, wout, x1, g3)


def dn_out_bwd(dxo, m, g3, o, proj, ng, wout, name):
    S, W = o.shape
    D = m.shape[1]
    nh = W // HEAD
    tm = _row_tile(S, 256)

    def body(dxo_ref, m_ref, g_ref, o_ref, z_ref, ng_ref, w_ref, dm_ref, do_ref, dz_ref, dng_ref, dg_ref):
        @pl.when(pl.program_id(0) == 0)
        def _():
            dng_ref[...] = jnp.zeros_like(dng_ref)
            dg_ref[...] = jnp.zeros_like(dg_ref)

        dm, dg = _rms_bwd(m_ref[...], g_ref[...], dxo_ref[...])
        dg_ref[...] += dg
        dmc = dm.astype(dm_ref.dtype)
        dm_ref[...] = dmc
        dog = _mm_nt(dmc, w_ref[...])
        for h in range(nh):
            sl = slice(h * HEAD, (h + 1) * HEAD)
            _, vjp = jax.vjp(_gate_norm, o_ref[:, sl], z_ref[:, sl], ng_ref[...])
            do, dz, dng = vjp(dog[:, sl])
            do_ref[:, sl] = do
            dz_ref[:, sl] = dz.astype(dz_ref.dtype)
            dng_ref[...] += dng

    rw = pl.BlockSpec((tm, W), lambda i: (i, 0))
    rd = pl.BlockSpec((tm, D), lambda i: (i, 0))
    vd = pl.BlockSpec((1, D), lambda i: (0, 0))
    vh = pl.BlockSpec((1, HEAD), lambda i: (0, 0))
    return pl.pallas_call(
        body, name=name, grid=(S // tm,),
        in_specs=[rd, rd, vd, rw, pl.BlockSpec((tm, W), lambda i: (i, 3)), vh, pl.BlockSpec((W, D), lambda i: (0, 0))],
        out_specs=(rd, rw, rw, vh, vd),
        out_shape=(SDS((S, D), MXU_DTYPE), SDS((S, W), f32), SDS((S, W), MXU_DTYPE), SDS((1, HEAD), f32),
                   SDS((1, D), f32)),
        compiler_params=_params(1),
    )(dxo, m, g3, o, proj, ng, wout)


def _sg_stage1(pu, pv, bu, bv, lg, lb):
    u = _gelu(pu + bu)
    t = _gelu(pv + bv)
    tc = t - jnp.mean(t, axis=-1, keepdims=True)
    v = tc * lax.rsqrt(jnp.mean(tc * tc, axis=-1, keepdims=True) + LN_EPS) * lg + lb
    return u, v


def _causal_mask(n):
    return lax.broadcasted_iota(jnp.int32, (n, n), 0) >= lax.broadcasted_iota(jnp.int32, (n, n), 1)


def sg_mid(pre, b_in, ln_g, ln_b, w_s, bsT, wout, x1, g3, name):
    S = pre.shape[0]
    E, D = ln_g.shape[1], x1.shape[1]
    G, CH = SG_GROUPS, SG_CHUNK
    Cg = E // G
    tm = _row_tile(S, 256)

    def body(pu_ref, pv_ref, bu_ref, bv_ref, lg_ref, lb_ref, ws_ref, bs_ref, w_ref, x_ref, g_ref,
             xo_ref, m_ref, gt_ref):
        u, v = _sg_stage1(pu_ref[...], pv_ref[...], bu_ref[...], bv_ref[...], lg_ref[...], lb_ref[...])
        mask = _causal_mask(CH)
        for g in range(G):
            wc = _c(jnp.where(mask, ws_ref[g], 0.0))
            bcol = bs_ref[:, g:g + 1]
            cs = slice(g * Cg, (g + 1) * Cg)
            for ch in range(tm // CH):
                rs = slice(ch * CH, (ch + 1) * CH)
                mixed = _mm(wc, _c(v[rs, cs])) + bcol
                gt_ref[rs, cs] = (u[rs, cs] * mixed).astype(gt_ref.dtype)
        m = _mm(gt_ref[...], w_ref[...])
        m_ref[...] = m
        xo_ref[...] = x_ref[...] + _rms(m, g_ref[...])

    half = lambda p: pl.BlockSpec((tm, E), lambda i: (i, p))
    vhalf = lambda p: pl.BlockSpec((1, E), lambda i: (0, p))
    ve = pl.BlockSpec((1, E), lambda i: (0, 0))
    rd = pl.BlockSpec((tm, D), lambda i: (i, 0))
    return pl.pallas_call(
        body, name=name, grid=(S // tm,),
        in_specs=[half(0), half(1), vhalf(0), vhalf(1), ve, ve, pl.BlockSpec((G, CH, CH), lambda i: (0, 0, 0)),
                  pl.BlockSpec((CH, G), lambda i: (0, 0)), pl.BlockSpec((E, D), lambda i: (0, 0)), rd,
                  pl.BlockSpec((1, D), lambda i: (0, 0))],
        out_specs=(rd, rd, pl.BlockSpec((tm, E), lambda i: (i, 0))),
        out_shape=(SDS((S, D), f32), SDS((S, D), f32), SDS((S, E), MXU_DTYPE)), compiler_params=_params(1),
    )(pre, pre, b_in, b_in, ln_g, ln_b, w_s, bsT, wout, x1, g3)


def sg_mid_bwd(dxo, m, g3, pre, b_in, ln_g, ln_b, w_s, bsT, wout, name):
    S = pre.shape[0]
    E, D = ln_g.shape[1], m.shape[1]
    G, CH = SG_GROUPS, SG_CHUNK
    Cg = E // G
    tm = _row_tile(S, 256)

    def body(dxo_ref, m_ref, g_ref, pu_ref, pv_ref, bu_ref, bv_ref, lg_ref, lb_ref, ws_ref, bs_ref, w_ref,
             dm_ref, dpre_ref, dbin_ref, dlg_ref, dlb_ref, dws_ref, dbs_ref, dg_ref, du_scr, dv_scr):
        @pl.when(pl.program_id(0) == 0)
        def _():
            for r in (dbin_ref, dlg_ref, dlb_ref, dws_ref, dbs_ref, dg_ref):
                r[...] = jnp.zeros_like(r)

        dm, dg = _rms_bwd(m_ref[...], g_ref[...], dxo_ref[...])
        dg_ref[...] += dg
        dmc = dm.astype(dm_ref.dtype)
        dm_ref[...] = dmc
        dgated = _mm_nt(dmc, w_ref[...])
        (u, v), vjp1 = jax.vjp(_sg_stage1, pu_ref[...], pv_ref[...], bu_ref[...], bv_ref[...], lg_ref[...],
                               lb_ref[...])
        mask = _causal_mask(CH)
        lane = lax.broadcasted_iota(jnp.int32, (CH, CH), 1)
        for g in range(G):
            wc = _c(jnp.where(mask, ws_ref[g], 0.0))
            bcol = bs_ref[:, g:g + 1]
            cs = slice(g * Cg, (g + 1) * Cg)
            dws = jnp.zeros((CH, CH), f32)
            dbs = jnp.zeros((CH, 1), f32)
            for ch in range(tm // CH):
                rs = slice(ch * CH, (ch + 1) * CH)
                vs = _c(v[rs, cs])
                mixed = _mm(wc, vs) + bcol
                dgt = dgated[rs, cs]
                du_scr[rs, cs] = dgt * mixed
                dmixed = dgt * u[rs, cs]
                dmc2 = _c(dmixed)
                dv_scr[rs, cs] = _mm_tn(wc, dmc2)
                dws = dws + _mm_nt(dmc2, vs)
                dbs = dbs + jnp.sum(dmixed, axis=1, keepdims=True)
            dws_ref[g] += jnp.where(mask, dws, 0.0)
            dbs_ref[...] += jnp.where(lane == g, jnp.broadcast_to(dbs, (CH, CH)), 0.0)
        dpu, dpv, dbu, dbv, dlg, dlb = vjp1((du_scr[...], dv_scr[...]))
        dpre_ref[:, :E] = dpu.astype(dpre_ref.dtype)
        dpre_ref[:, E:] = dpv.astype(dpre_ref.dtype)
        dbin_ref[:, :E] += dbu
        dbin_ref[:, E:] += dbv
        dlg_ref[...] += dlg
        dlb_ref[...] += dlb

    half = lambda p: pl.BlockSpec((tm, E), lambda i: (i, p))
    vhalf = lambda p: pl.BlockSpec((1, E), lambda i: (0, p))
    ve = pl.BlockSpec((1, E), lambda i: (0, 0))
    rd = pl.BlockSpec((tm, D), lambda i: (i, 0))
    vd = pl.BlockSpec((1, D), lambda i: (0, 0))
    wsb = pl.BlockSpec((G, CH, CH), lambda i: (0, 0, 0))
    return pl.pallas_call(
        body, name=name, grid=(S // tm,),
        in_specs=[rd, rd, vd, half(0), half(1), vhalf(0), vhalf(1), ve, ve, wsb,
                  pl.BlockSpec((CH, G), lambda i: (0, 0)), pl.BlockSpec((E, D), lambda i: (0, 0))],
        out_specs=(rd, pl.BlockSpec((tm, 2 * E), lambda i: (i, 0)), pl.BlockSpec((1, 2 * E), lambda i: (0, 0)), ve, ve,
                   wsb, pl.BlockSpec((CH, CH), lambda i: (0, 0)), vd),
        out_shape=(SDS((S, D), MXU_DTYPE), SDS((S, 2 * E), MXU_DTYPE), SDS((1, 2 * E), f32), SDS((1, E), f32),
                   SDS((1, E), f32), SDS((G, CH, CH), f32), SDS((CH, CH), f32), SDS((1, D), f32)),
        scratch_shapes=[pltpu.VMEM((tm, E), f32), pltpu.VMEM((tm, E), f32)], compiler_params=_params(1),
    )(dxo, m, g3, pre, pre, b_in, b_in, ln_g, ln_b, w_s, bsT, wout)


def loss_head(y, target, name):
    S, D = y.shape
    tm = _row_tile(S, 512)

    def body(y_ref, t_ref, l_ref, d_ref):
        @pl.when(pl.program_id(0) == 0)
        def _():
            l_ref[...] = jnp.zeros_like(l_ref)

        e = y_ref[...] - t_ref[...]
        d_ref[...] = e * (1.0 / D)
        l_ref[...] += jnp.sum(e * e) * (0.5 / D)

    row = pl.BlockSpec((tm, D), lambda i: (i, 0))
    return pl.pallas_call(
        body, name=name, grid=(S // tm,), in_specs=[row, row],
        out_specs=(pl.BlockSpec((1, HEAD), lambda i: (0, 0)), row),
        out_shape=(SDS((1, HEAD), f32), SDS((S, D), f32)), compiler_params=_params(1),
    )(y, target)


def sum_slots(r, name):
    _, R, C = r.shape
    tr = _row_tile(R, 648 if R % 648 == 0 else R)

    def body(r_ref, o_ref):
        acc = r_ref[0].astype(f32)
        for s in range(1, N_DEV):
            acc = acc + r_ref[s].astype(f32)
        o_ref[...] = acc

    return pl.pallas_call(
        body, name=name, grid=(R // tr,), in_specs=[pl.BlockSpec((N_DEV, tr, C), lambda i: (0, i, 0))],
        out_specs=pl.BlockSpec((tr, C), lambda i: (i, 0)), out_shape=SDS((R, C), f32), compiler_params=_params(1),
    )(r)


def _adam_math(w, g, m, v):
    m = ADAM_B1 * m + (1.0 - ADAM_B1) * g
    v = ADAM_B2 * v + (1.0 - ADAM_B2) * (g * g)
    m_hat = m / (1.0 - ADAM_B1 ** ADAM_STEP)
    v_hat = v / (1.0 - ADAM_B2 ** ADAM_STEP)
    delta = -ADAM_LR * (m_hat / (jnp.sqrt(v_hat) + ADAM_EPS) + ADAM_WD * w)
    return delta, m, v


def adam_slots(w, rs, m, v, name, tr):
    R, C = w.shape
    tr = _row_tile(min(r.shape[1] for r in rs), tr)
    blocks = [r.shape[1] // tr for r in rs]
    starts = [sum(blocks[:k]) for k in range(len(rs))]
    assert sum(blocks) * tr == R

    def body(w_ref, *refs):
        r_refs, (m_ref, v_ref, g_ref, d_ref, mo_ref, vo_ref) = refs[:len(rs)], refs[len(rs):]
        i = pl.program_id(0)
        for k, r_ref in enumerate(r_refs):
            @pl.when((i >= starts[k]) & (i < starts[k] + blocks[k]))
            def _():
                g = r_ref[0].astype(f32)
                for s in range(1, N_DEV):
                    g = g + r_ref[s].astype(f32)
                g_ref[...] = g

        d_ref[...], mo_ref[...], vo_ref[...] = _adam_math(w_ref[...], g_ref[...], m_ref[...], v_ref[...])

    row = pl.BlockSpec((tr, C), lambda i: (i, 0))
    piece = lambda k: pl.BlockSpec((N_DEV, tr, C), lambda i: (0, jnp.clip(i - starts[k], 0, blocks[k] - 1), 0))
    return pl.pallas_call(
        body, name=name, grid=(R // tr,), in_specs=[row] + [piece(k) for k in range(len(rs))] + [row, row],
        out_specs=(row, row, row, row), out_shape=tuple(SDS((R, C), f32) for _ in range(4)),
        compiler_params=_params(1),
    )(w, *rs, m, v)


def adam_small(w, g, m, v, name):
    def body(w_ref, g_ref, m_ref, v_ref, d_ref, mo_ref, vo_ref):
        d_ref[...], mo_ref[...], vo_ref[...] = _adam_math(w_ref[...], g_ref[...], m_ref[...], v_ref[...])

    return pl.pallas_call(body, name=name, out_shape=tuple(SDS(w.shape, f32) for _ in range(3)))(w, g, m, v)


def _pack_rows(parts):
    rows, offs, r = [], [], 0
    for p in parts:
        flat = p.reshape(-1)
        n = -(-flat.shape[0] // HEAD)
        flat = jnp.pad(flat, (0, n * HEAD - flat.shape[0]))
        rows.append(flat.reshape(n, HEAD))
        offs.append((r, n))
        r += n
    pad = (-r) % 8
    if pad:
        rows.append(jnp.zeros((pad, HEAD), f32))
    return jnp.concatenate(rows, axis=0), offs


def kernel(x, norm_g, ffn_w_gate, ffn_w_up, ffn_w_down, dn_w_in, dn_conv_w, dn_a_log, dn_dt_bias, dn_norm_g, dn_w_out, sg_w_in, sg_b_in, sg_ln_g, sg_ln_b, sg_w_s, sg_b_s, sg_w_out, loss_target, m_norm_g, m_ffn_w_gate, m_ffn_w_up, m_ffn_w_down, m_dn_w_in, m_dn_conv_w, m_dn_a_log, m_dn_dt_bias, m_dn_norm_g, m_dn_w_out, m_sg_w_in, m_sg_b_in, m_sg_ln_g, m_sg_ln_b, m_sg_w_s, m_sg_b_s, m_sg_w_out, v_norm_g, v_ffn_w_gate, v_ffn_w_up, v_ffn_w_down, v_dn_w_in, v_dn_conv_w, v_dn_a_log, v_dn_dt_bias, v_dn_norm_g, v_dn_w_out, v_sg_w_in, v_sg_b_in, v_sg_ln_g, v_sg_ln_b, v_sg_w_s, v_sg_b_s, v_sg_w_out):
    weights = dict(norm_g=norm_g, ffn_w_gate=ffn_w_gate, ffn_w_up=ffn_w_up, ffn_w_down=ffn_w_down, dn_w_in=dn_w_in,
                   dn_conv_w=dn_conv_w, dn_a_log=dn_a_log, dn_dt_bias=dn_dt_bias, dn_norm_g=dn_norm_g,
                   dn_w_out=dn_w_out, sg_w_in=sg_w_in, sg_b_in=sg_b_in, sg_ln_g=sg_ln_g, sg_ln_b=sg_ln_b,
                   sg_w_s=sg_w_s, sg_b_s=sg_b_s, sg_w_out=sg_w_out)
    mom_m = dict(norm_g=m_norm_g, ffn_w_gate=m_ffn_w_gate, ffn_w_up=m_ffn_w_up, ffn_w_down=m_ffn_w_down,
                 dn_w_in=m_dn_w_in, dn_conv_w=m_dn_conv_w, dn_a_log=m_dn_a_log, dn_dt_bias=m_dn_dt_bias,
                 dn_norm_g=m_dn_norm_g, dn_w_out=m_dn_w_out, sg_w_in=m_sg_w_in, sg_b_in=m_sg_b_in,
                 sg_ln_g=m_sg_ln_g, sg_ln_b=m_sg_ln_b, sg_w_s=m_sg_w_s, sg_b_s=m_sg_b_s, sg_w_out=m_sg_w_out)
    mom_v = dict(norm_g=v_norm_g, ffn_w_gate=v_ffn_w_gate, ffn_w_up=v_ffn_w_up, ffn_w_down=v_ffn_w_down,
                 dn_w_in=v_dn_w_in, dn_conv_w=v_dn_conv_w, dn_a_log=v_dn_a_log, dn_dt_bias=v_dn_dt_bias,
                 dn_norm_g=v_dn_norm_g, dn_w_out=v_dn_w_out, sg_w_in=v_sg_w_in, sg_b_in=v_sg_b_in,
                 sg_ln_g=v_sg_ln_g, sg_ln_b=v_sg_ln_b, sg_w_s=v_sg_w_s, sg_b_s=v_sg_b_s, sg_w_out=v_sg_w_out)
    order = list(weights)

    xs = x[0]
    S, D = xs.shape
    F8 = ffn_w_gate.shape[-1]
    depth = norm_g.shape[0]
    W = dn_w_out.shape[1] * N_DEV
    H = W // HEAD
    E = sg_ln_g.shape[1] * N_DEV
    G, CH = sg_w_s.shape[1], sg_w_s.shape[2]
    c8 = dn_w_in.shape[2]
    me = _slot(lax.axis_index("x"), lax.axis_index("y"), lax.axis_index("c"))

    assert depth == 2
    small_in, small_offs = _pack_rows([norm_g, dn_conv_w, sg_b_in, sg_ln_g, sg_ln_b])
    wg0a, wu0a, wd0a, small_all = all_gather_multi(
        [_c(ffn_w_gate[0, 0]), _c(ffn_w_up[0, 0]), _c(ffn_w_down[0, 0]), small_in], name="gather_first")
    ffn_shards = lambda l, ab: [_c(ffn_w_gate[l, ab]), _c(ffn_w_up[l, ab]), _c(ffn_w_down[l, ab])]
    gather_dn = Comm("gather", [_c(dn_w_in[0]), _c(dn_w_out[0])])
    gather_mid = Comm("gather", ffn_shards(0, 1) + ffn_shards(1, 0))
    gather_end = Comm("gather", ffn_shards(1, 1) + [_c(sg_w_in[0]), _c(sg_w_out[0])])
    per = N_DEV // FFN_SLABS
    wide_cols = lambda w: jnp.transpose(w.reshape(FFN_SLABS, per, D, F8), (0, 2, 1, 3)).reshape(FFN_SLABS, D, per * F8)
    wide = lambda g, u, d: (wide_cols(g), wide_cols(u), d.reshape(FFN_SLABS, per * F8, D))
    ffn_w = {(0, 0): wide(wg0a, wu0a, wd0a)}

    def small_piece(i, shard_shape):
        r0, n = small_offs[i]
        sz = math.prod(shard_shape)
        return small_all[:, r0:r0 + n, :].reshape(N_DEV, n * HEAD)[:, :sz].reshape((N_DEV,) + tuple(shard_shape))

    ng_full = jnp.moveaxis(small_piece(0, norm_g.shape), 0, 2).reshape(depth, 6, D)
    conv_full = jnp.moveaxis(small_piece(1, dn_conv_w.shape[1:]), 0, 1).reshape(CONV_K, 3 * W)
    bin_full = small_piece(2, sg_b_in.shape[1:]).reshape(1, 2 * E)
    lng_full = small_piece(3, sg_ln_g.shape[1:]).reshape(1, E)
    lnb_full = small_piece(4, sg_ln_b.shape[1:]).reshape(1, E)
    gate_lanes = lambda v: jnp.pad(v.reshape(1, H), ((0, 0), (H, HEAD - 2 * H)))
    al_row, dt_row = gate_lanes(dn_a_log), gate_lanes(dn_dt_bias)
    bsT = sg_b_s[0].T
    gvec = lambda l, k: ng_full[l, k].reshape(1, D)

    saved = []
    cur = xs
    for l in range(depth):
        sv = {}
        sv['x0'] = cur
        (cur, sv['hA'], sv['pA'], sv['qA'], sv['tA'], sv['yA']), got = ffn_fwd(
            cur, gvec(l, 0), gvec(l, 1), *ffn_w[l, 0], name=f"ffn_fwd_{l}a", comm=gather_dn if l == 0 else None)
        sv['x1'] = cur
        if l == 0:
            dnin_all, dnout_all = got
            dn_win = jnp.moveaxis(dnin_all, 0, 1).reshape(D, N_DEV * c8)
            dn_wmain = dn_win[:, :4 * W]
            dn_wba = jnp.pad(dn_win[:, 4 * W:], ((0, 0), (0, HEAD - 2 * H)))
            dn_wout = dnout_all.reshape(W, D)
            sv['hM'], sv['proj'], sv['pba'] = rms_mm(cur, gvec(l, 2), dn_wmain, dn_wba, name=f"dn_in_{l}")
            sv['qkv'] = dn_prep(sv['proj'], conv_full, name=f"dn_prep_{l}")
            sv['gates'] = dn_gates(sv['pba'], al_row, dt_row, H, name=f"dn_gates_{l}")
            (sv['o'], sv['states']), got = dn_chunk_fwd(sv['qkv'], sv['gates'], name=f"dn_chunk_{l}", comm=gather_mid)
            ffn_w[0, 1], ffn_w[1, 0] = wide(*got[0:3]), wide(*got[3:6])
            cur, sv['m'], sv['og'] = dn_out(sv['o'], sv['proj'], dn_norm_g, dn_wout, cur, gvec(l, 3), name=f"dn_out_{l}")
        else:
            sv['hM'], sv['pre'] = rms_mm(cur, gvec(l, 2), sg_win, None, name=f"sg_in_{l}")
            cur, sv['m'], sv['gated'] = sg_mid(sv['pre'], bin_full, lng_full, lnb_full, sg_w_s[0], bsT, sg_wout, cur,
                                               gvec(l, 3), name=f"sg_mid_{l}")
        sv['x2'] = cur
        (cur, sv['hB'], sv['pB'], sv['qB'], sv['tB'], sv['yB']), got = ffn_fwd(
            cur, gvec(l, 4), gvec(l, 5), *ffn_w[l, 1], name=f"ffn_fwd_{l}b", comm=gather_end if l == 0 else None)
        if l == 0:
            ffn_w[1, 1] = wide(*got[0:3])
            sg_win = jnp.moveaxis(got[3], 0, 1).reshape(D, 2 * E)
            sg_wout = got[4].reshape(E, D)
        saved.append(sv)

    loss_blk, dcur = loss_head(cur, loss_target[0], name="loss_head")
    loss = lax.psum(loss_blk[0, 0], ("x", "y", "c"))

    dng = [[None] * 6 for _ in range(depth)]
    ffn_dw = {}
    grads, slots = {}, {}

    def ffn_backward(l, ab, dcur, comm=None):
        sv, s = saved[l], 'AB'[ab]
        (dcur, da, db, dy, dng[l][4 * ab], dng[l][4 * ab + 1]), got = ffn_bwd_dx(
            dcur, sv['x2' if ab else 'x0'], sv['y' + s], sv['p' + s], sv['q' + s], gvec(l, 4 * ab), gvec(l, 4 * ab + 1),
            *ffn_w[l, ab], name=f"ffn_bwd_{l}{'ab'[ab]}", comm=comm)
        dg, du, dd = ffn_bwd_dw(sv['h' + s], dy, sv['t' + s], da, db, name=f"ffn_dw_{l}{'ab'[ab]}")
        ffn_dw[l, ab] = (dg, du, dd)
        return dcur, got

    sv = saved[1]
    dcur, _ = ffn_backward(1, 1, dcur)
    dm, dpre, grads['sg_b_in'], grads['sg_ln_g'], grads['sg_ln_b'], grads['sg_w_s'], dbs, dng[1][3] = sg_mid_bwd(
        dcur, sv['m'], gvec(1, 3), sv['pre'], bin_full, lng_full, lnb_full, sg_w_s[0], bsT, sg_wout, name="sg_mid_bwd_1")
    grads['sg_b_s'] = dbs[:, :G].T
    dsg_wout = tn_mm(sv['gated'], dm, name="sg_wout_dw_1").reshape(N_DEV, E // N_DEV, D)
    dsg_win = tn_mm(sv['hM'], dpre, name="sg_win_dw_1", tn=2 * E // N_DEV, slot_major=True)
    (dcur, dng[1][2]), _ = mm_bwd_dx(dcur, sv['x1'], gvec(1, 2), dpre, sg_win, None, None, name="sg_in_bwd_1")
    dcur, _ = ffn_backward(1, 0, dcur)
    sv = saved[0]
    dcur, _ = ffn_backward(0, 1, dcur)
    dm, do, dz, grads['dn_norm_g'], dng[0][3] = dn_out_bwd(dcur, sv['m'], gvec(0, 3), sv['o'], sv['proj'], dn_norm_g,
                                                          dn_wout, name="dn_out_bwd_0")
    ddn_wout = tn_mm(sv['og'], dm, name="dn_wout_dw_0").reshape(N_DEV, W // N_DEV, D)
    (dqkv, dgates), got = dn_chunk_bwd(sv['qkv'], sv['gates'], sv['states'], do, name="dn_chunk_bwd_0",
                                       comm=Comm("exchange", [*ffn_dw[1, 0], *ffn_dw[1, 1], dsg_win, dsg_wout]))
    l1a, l1b, slots['sg_w_in'], slots['sg_w_out'] = got[0:3], got[3:6], [got[6]], [got[7]]
    dpba, dal, ddt = dn_gates_bwd(sv['pba'], al_row, dt_row, dgates, H, name="dn_gates_bwd_0")
    grads['dn_a_log'] = dal[:, H:2 * H]
    grads['dn_dt_bias'] = ddt[:, H:2 * H]
    (dproj, grads['dn_conv_w']), l0b = dn_prep_bwd(sv['proj'], conv_full, dqkv, dz, name="dn_prep_bwd_0",
                                                   comm=Comm("exchange", list(ffn_dw[0, 1])))
    dw_main = tn_mm(sv['hM'], dproj, name="dn_win_dw_0")
    dw_ba = tn_mm(sv['hM'], dpba, name="dn_wba_dw_0", tn=HEAD)
    dw_in = jnp.concatenate([dw_main, dw_ba[:, :2 * H]], axis=1)
    ddn_win = jnp.moveaxis(dw_in.reshape(D, N_DEV, c8), 1, 0)
    (dcur, dng[0][2]), got = mm_bwd_dx(dcur, sv['x1'], gvec(0, 2), dproj, dn_wmain, dpba, dn_wba, name="dn_in_bwd_0",
                                       comm=Comm("exchange", [ddn_win, ddn_wout]))
    slots['dn_w_in'], slots['dn_w_out'] = [got[0]], [got[1]]
    dcur, _ = ffn_backward(0, 0, dcur)
    grad_x = dcur[None]
    l0a = exchange_slots(list(ffn_dw[0, 0]), name="exchange_last")
    for i, nm in enumerate(['ffn_w_gate', 'ffn_w_up', 'ffn_w_down']):
        slots[nm] = [l0a[i], l0b[i], l1a[i], l1b[i]]
    big_names = ['ffn_w_gate', 'ffn_w_up', 'ffn_w_down', 'dn_w_in', 'dn_w_out', 'sg_w_in', 'sg_w_out']
    slots = [slots[nm] for nm in big_names]

    dng_full = jnp.stack([jnp.concatenate(r, axis=0) for r in dng], axis=0)
    small_names = ['norm_g', 'dn_conv_w', 'sg_b_in', 'sg_ln_g', 'sg_ln_b', 'sg_w_s', 'sg_b_s', 'dn_a_log',
                   'dn_dt_bias', 'dn_norm_g']
    small_parts = [dng_full, grads['dn_conv_w'], grads['sg_b_in'], grads['sg_ln_g'], grads['sg_ln_b'],
                   grads['sg_w_s'], grads['sg_b_s'], grads['dn_a_log'], grads['dn_dt_bias'], grads['dn_norm_g']]
    small_pack, offs = _pack_rows(small_parts)
    (small_slots,) = all_gather_multi([small_pack], name="gather_small_grads")
    small_sum = sum_slots(small_slots, name="sum_small_grads")

    def small_grad(i):
        r0, n = offs[i]
        p = small_parts[i]
        return small_sum[r0:r0 + n].reshape(-1)[:p.size].reshape(p.shape)

    def my_shard(full, axis, like):
        n = full.shape[axis] // N_DEV
        return lax.dynamic_slice_in_dim(full, me * n, n, axis).reshape(like.shape)

    g_small = {
        'norm_g': my_shard(small_grad(0), 2, norm_g),
        'dn_conv_w': my_shard(small_grad(1), 1, dn_conv_w),
        'sg_b_in': my_shard(small_grad(2), 1, sg_b_in),
        'sg_ln_g': my_shard(small_grad(3), 1, sg_ln_g),
        'sg_ln_b': my_shard(small_grad(4), 1, sg_ln_b),
        'sg_w_s': small_grad(5).reshape(sg_w_s.shape),
        'sg_b_s': small_grad(6).reshape(sg_b_s.shape),
        'dn_a_log': small_grad(7).reshape(dn_a_log.shape),
        'dn_dt_bias': small_grad(8).reshape(dn_dt_bias.shape),
        'dn_norm_g': small_grad(9).reshape(dn_norm_g.shape),
    }

    out_g, out_d, out_m, out_v = {}, {}, {}, {}
    for nm, r in zip(big_names, slots):
        w = weights[nm]
        cols = w.shape[-1]
        rows = w.size // cols
        tr = {'ffn_w_gate': 512, 'ffn_w_up': 512, 'ffn_w_down': F8 // 2, 'dn_w_in': 256, 'sg_w_in': 256}.get(nm, rows)
        pieces = [p.reshape(N_DEV, -1, cols) for p in r]
        g, d, m2, v2 = adam_slots(w.reshape(rows, cols), pieces, mom_m[nm].reshape(rows, cols),
                                  mom_v[nm].reshape(rows, cols), name=f"adam_{nm}", tr=tr)
        out_g[nm], out_d[nm], out_m[nm], out_v[nm] = (t.reshape(w.shape) for t in (g, d, m2, v2))
    for nm in small_names:
        w = weights[nm]
        cols = w.shape[-1]
        rows = w.size // cols
        two = lambda t: t.reshape(rows, cols)
        d, m2, v2 = adam_small(two(w), two(g_small[nm]), two(mom_m[nm]), two(mom_v[nm]), name=f"adam_{nm}")
        out_g[nm] = g_small[nm]
        out_d[nm], out_m[nm], out_v[nm] = (t.reshape(w.shape) for t in (d, m2, v2))

    return (loss, grad_x, *[out_g[n] for n in order], *[out_d[n] for n in order], *[out_m[n] for n in order],
            *[out_v[n] for n in order])
```

```python
import functools
import math

import jax
import jax.numpy as jnp
from jax import lax
from jax.experimental import pallas as pl
from jax.experimental.pallas import tpu as pltpu

f32 = jnp.float32
MXU_DTYPE = jnp.bfloat16
N_DEV = 8
RMS_EPS = 1e-6
LN_EPS = 1e-5
L2_EPS = 1e-6
HEAD = 128
DN_CHUNK = 64
SG_CHUNK = 128
SG_GROUPS = 8
CONV_K = 4
ADAM_LR, ADAM_B1, ADAM_B2, ADAM_EPS, ADAM_WD, ADAM_STEP = 0.001, 0.9, 0.999, 1e-08, 0.01, 10
VMEM_LIMIT = 56 * 1024 * 1024
FFN_ROWS_FWD, FFN_ROWS_BWD, FFN_ROWS_DW = 1024, 512, 2048
PROJ_ROWS, TN_ROWS = 1024, 2048
FFN_SLABS = 4
SDS = jax.ShapeDtypeStruct
HIGHEST = lax.Precision.HIGHEST
MESH = pl.DeviceIdType.MESH


def _params(n_grid):
    return pltpu.CompilerParams(dimension_semantics=("arbitrary",) * n_grid, vmem_limit_bytes=VMEM_LIMIT)


def _row_tile(s, want):
    t = min(s, want)
    assert s % t == 0, (s, t)
    return t


def _rms(x, g):
    return x * lax.rsqrt(jnp.mean(x * x, axis=-1, keepdims=True) + RMS_EPS) * g


def _rms_bwd(x, g, dy):
    _, vjp = jax.vjp(_rms, x, g)
    return vjp(dy)


def _silu(a):
    return a * jax.nn.sigmoid(a)


def _gelu(x):
    return 0.5 * x * (1.0 + lax.erf(x * 0.7071067811865476))


def _mm(a, b):
    return lax.dot_general(a, b, (((1,), (0,)), ((), ())), preferred_element_type=f32)


def _mm_nt(a, b):
    return lax.dot_general(a, b, (((1,), (1,)), ((), ())), preferred_element_type=f32)


def _mm_tn(a, b):
    return lax.dot_general(a, b, (((0,), (0,)), ((), ())), preferred_element_type=f32)


def _c(x):
    return x.astype(MXU_DTYPE)


def _split(a):
    hi = a.astype(MXU_DTYPE)
    lo = (a - hi.astype(f32)).astype(MXU_DTYPE)
    return hi, lo


def _dot3(a, b, dims):
    ah, al = _split(a)
    bh, bl = _split(b)
    d = lambda p, q: lax.dot_general(p, q, (dims, ((), ())), preferred_element_type=f32)
    return d(ah, bh) + (d(ah, bl) + d(al, bh))


NN, NT, TN = ((1,), (0,)), ((1,), (1,)), ((0,), (0,))


def _slot(px, py, pc):
    return 4 * px + 2 * py + pc


def all_gather_multi(arrs, name):
    return Comm("gather", arrs).alone(name)


def exchange_slots(arrs, name):
    return Comm("exchange", arrs).alone(name)


class Comm:
    def __init__(self, kind, arrs):
        self.kind, self.arrs, self.n = kind, list(arrs), len(arrs)
        hbm = pl.BlockSpec(memory_space=pltpu.HBM)
        self.in_specs = [hbm] * self.n
        self.out_specs = [hbm] * self.n
        lead = (N_DEV,) if kind == "gather" else ()
        self.out_shape = [SDS(lead + tuple(a.shape), a.dtype) for a in self.arrs]
        self.scratch = [pltpu.SemaphoreType.DMA((self.n, 7)), pltpu.SemaphoreType.DMA((self.n, 7)),
                        pltpu.SemaphoreType.DMA((self.n,))]

    def phase(self, p, ins, outs, sems):
        (self._gather if self.kind == "gather" else self._exchange)(p, ins, outs, sems)

    def _gather(self, p, ins, outs, sems):
        send_sems, recv_sems, local_sems = sems
        x, y, c = lax.axis_index("x"), lax.axis_index("y"), lax.axis_index("c")
        me, sibling = (x, y, c), (x, y, 1 - c)
        chips = [(1 - x, y), (x, 1 - y), (1 - x, 1 - y)]

        def copy(a, k, block, to, src=None):
            dst = outs[a].at[_slot(*block)]
            return pltpu.make_async_remote_copy(
                src_ref=dst if src is None else src, dst_ref=dst, send_sem=send_sems.at[a, k],
                recv_sem=recv_sems.at[a, k], device_id=to, device_id_type=MESH)

        mine = [pltpu.make_async_copy(ins[a], outs[a].at[_slot(*me)], local_sems.at[a]) for a in range(self.n)]
        first = [[copy(a, 0, me, sibling, src=ins[a])] +
                 [copy(a, 1 + j, me, (*chip, c), src=ins[a]) for j, chip in enumerate(chips)] for a in range(self.n)]
        passed = [[copy(a, 4 + j, (*chip, c), sibling) for j, chip in enumerate(chips)] for a in range(self.n)]
        if p == 0:
            for a in range(self.n):
                mine[a].start()
            for a in range(self.n):
                for cp in first[a]:
                    cp.start()
        elif p == 1:
            for a in range(self.n):
                for j, chip in enumerate(chips):
                    copy(a, 1 + j, (*chip, c), me).wait_recv()
                    passed[a][j].start()
        else:
            for a in range(self.n):
                copy(a, 0, sibling, me).wait_recv()
                for j, chip in enumerate(chips):
                    copy(a, 4 + j, (*chip, 1 - c), me).wait_recv()
            for a in range(self.n):
                for cp in first[a] + passed[a]:
                    cp.wait_send()
                mine[a].wait()

    def _exchange(self, p, ins, outs, sems):
        send_sems, recv_sems, local_sems = sems
        x, y, c = lax.axis_index("x"), lax.axis_index("y"), lax.axis_index("c")
        me = _slot(x, y, c)
        peers = [(x ^ (k >> 2), y ^ ((k >> 1) & 1), c ^ (k & 1)) for k in range(1, N_DEV)]

        def copy(a, k):
            peer = peers[k - 1]
            return pltpu.make_async_remote_copy(
                src_ref=ins[a].at[_slot(*peer)], dst_ref=outs[a].at[me], send_sem=send_sems.at[a, k - 1],
                recv_sem=recv_sems.at[a, k - 1], device_id=peer, device_id_type=MESH)

        def landed(a, k):
            peer = peers[k - 1]
            return pltpu.make_async_remote_copy(
                src_ref=ins[a].at[me], dst_ref=outs[a].at[_slot(*peer)], send_sem=send_sems.at[a, k - 1],
                recv_sem=recv_sems.at[a, k - 1], device_id=peer, device_id_type=MESH)

        local = [pltpu.make_async_copy(ins[a].at[me], outs[a].at[me], local_sems.at[a]) for a in range(self.n)]
        order = [6, 7, 2, 3, 4, 5, 1]
        if p == 0:
            for a in range(self.n):
                local[a].start()
            for a in range(self.n):
                for k in order:
                    copy(a, k).start()
        elif p == 2:
            for a in range(self.n):
                for k in order:
                    copy(a, k).wait_send()
                    landed(a, k).wait_recv()
                local[a].wait()

    def alone(self, name):
        n = self.n

        def body(*refs):
            for p in range(3):
                self.phase(p, refs[:n], refs[n:2 * n], refs[2 * n:])

        return pl.pallas_call(body, name=name, out_shape=tuple(self.out_shape), in_specs=self.in_specs,
                              out_specs=tuple(self.out_specs), scratch_shapes=self.scratch)(*self.arrs)


def hosted_call(body, comm, steps, *, name, grid, in_specs, out_specs, out_shape, scratch_shapes, args):
    if comm is None:
        outs = pl.pallas_call(body, name=name, grid=grid, in_specs=in_specs, out_specs=tuple(out_specs),
                              out_shape=tuple(out_shape), scratch_shapes=scratch_shapes,
                              compiler_params=_params(len(grid)))(*args)
        return outs, None
    ni, no, ns, cn = len(in_specs), len(out_specs), len(scratch_shapes), comm.n

    def both(*refs):
        h_in, c_in = refs[:ni], refs[ni:ni + cn]
        h_out, c_out = refs[ni + cn:ni + cn + no], refs[ni + cn + no:ni + 2 * cn + no]
        h_scr, c_scr = refs[ni + 2 * cn + no:ni + 2 * cn + no + ns], refs[ni + 2 * cn + no + ns:]
        when = steps()
        pl.when(when[0])(lambda: comm.phase(0, c_in, c_out, c_scr))
        body(*h_in, *h_out, *h_scr)
        pl.when(when[1])(lambda: comm.phase(1, c_in, c_out, c_scr))
        pl.when(when[2])(lambda: comm.phase(2, c_in, c_out, c_scr))

    outs = pl.pallas_call(
        both, name=name, grid=grid, in_specs=list(in_specs) + comm.in_specs,
        out_specs=tuple(out_specs) + tuple(comm.out_specs), out_shape=tuple(out_shape) + tuple(comm.out_shape),
        scratch_shapes=list(scratch_shapes) + comm.scratch, compiler_params=_params(len(grid)),
    )(*args, *comm.arrs)
    return outs[:no], outs[no:]


def _grid_steps(n_outer, n_inner=1):
    total = n_outer * n_inner

    def steps():
        t = pl.program_id(0) * n_inner + (pl.program_id(1) if n_inner > 1 else 0)
        return t == 0, t == (total * 5) // 8, t == total - 1
    return steps


def ffn_fwd(x, gpre, gpost, wg, wu, wd, name, comm=None):
    S, D = x.shape
    nj, F8 = wg.shape[0], wg.shape[-1]
    tm = _row_tile(S, FFN_ROWS_FWD)

    def body(x_ref, gpre_ref, gpost_ref, wg_ref, wu_ref, wd_ref, xo_ref, h_ref, p_ref, q_ref, t_ref, y_ref):
        j = pl.program_id(1)

        @pl.when(j == 0)
        def _():
            h_ref[...] = _rms(x_ref[...], gpre_ref[...]).astype(h_ref.dtype)
            y_ref[...] = jnp.zeros_like(y_ref)

        h = h_ref[...]
        a = _mm(h, wg_ref[...])
        b = _mm(h, wu_ref[...])
        s = jax.nn.sigmoid(a)
        q = a * s
        p_ref[...] = (b * (s + q * (1.0 - s))).astype(p_ref.dtype)
        q_ref[...] = q.astype(q_ref.dtype)
        t = (q * b).astype(t_ref.dtype)
        t_ref[...] = t
        y_ref[...] += _mm(t, wd_ref[...])

        @pl.when(j == nj - 1)
        def _():
            xo_ref[...] = x_ref[...] + 0.5 * _rms(y_ref[...], gpost_ref[...])

    row = pl.BlockSpec((tm, D), lambda i, j: (i, 0))
    vec = pl.BlockSpec((1, D), lambda i, j: (0, 0))
    wcol = pl.BlockSpec((None, D, F8), lambda i, j: (j, 0, 0))
    wrow = pl.BlockSpec((None, F8, D), lambda i, j: (j, 0, 0))
    hid = pl.BlockSpec((None, tm, F8), lambda i, j: (j, i, 0))
    return hosted_call(
        body, comm, _grid_steps(S // tm, nj), name=name, grid=(S // tm, nj),
        in_specs=[row, vec, vec, wcol, wcol, wrow],
        out_specs=(row, row, hid, hid, hid, row),
        out_shape=(SDS((S, D), f32), SDS((S, D), MXU_DTYPE), SDS((nj, S, F8), MXU_DTYPE),
                   SDS((nj, S, F8), MXU_DTYPE), SDS((nj, S, F8), MXU_DTYPE), SDS((S, D), f32)),
        scratch_shapes=[], args=(x, gpre, gpost, wg, wu, wd))


def ffn_bwd_dx(dxo, x, y, p, q, gpre, gpost, wg, wu, wd, name, comm=None):
    S, D = x.shape
    nj, F8 = wg.shape[0], wg.shape[-1]
    tm = _row_tile(S, FFN_ROWS_BWD)

    def body(dxo_ref, x_ref, y_ref, p_ref, q_ref, gpre_ref, gpost_ref, wg_ref, wu_ref, wd_ref,
             dx_ref, da_ref, db_ref, dy_ref, dgpre_ref, dgpost_ref, dh_ref):
        i, j = pl.program_id(0), pl.program_id(1)

        @pl.when(j == 0)
        def _():
            @pl.when(i == 0)
            def _():
                dgpre_ref[...] = jnp.zeros_like(dgpre_ref)
                dgpost_ref[...] = jnp.zeros_like(dgpost_ref)

            dy, dg = _rms_bwd(y_ref[...], gpost_ref[...], 0.5 * dxo_ref[...])
            dy_ref[...] = dy.astype(dy_ref.dtype)
            dgpost_ref[...] += dg
            dh_ref[...] = jnp.zeros_like(dh_ref)

        half = tm // 2 if tm % 16 == 0 else tm
        for r0 in range(0, tm, half):
            rs = slice(r0, r0 + half)
            dt = _mm_nt(dy_ref[rs, :], wd_ref[...])
            da = (dt * p_ref[rs, :].astype(f32)).astype(da_ref.dtype)
            db = (dt * q_ref[rs, :].astype(f32)).astype(db_ref.dtype)
            da_ref[rs, :] = da
            db_ref[rs, :] = db
            dh_ref[rs, :] += _mm_nt(da, wg_ref[...]) + _mm_nt(db, wu_ref[...])

        @pl.when(j == nj - 1)
        def _():
            dxx, dg = _rms_bwd(x_ref[...], gpre_ref[...], dh_ref[...])
            dx_ref[...] = dxo_ref[...] + dxx
            dgpre_ref[...] += dg

    row = pl.BlockSpec((tm, D), lambda i, j: (i, 0))
    vec = pl.BlockSpec((1, D), lambda i, j: (0, 0))
    wcol = pl.BlockSpec((None, D, F8), lambda i, j: (j, 0, 0))
    wrow = pl.BlockSpec((None, F8, D), lambda i, j: (j, 0, 0))
    hid = pl.BlockSpec((None, tm, F8), lambda i, j: (j, i, 0))
    return hosted_call(
        body, comm, _grid_steps(S // tm, nj), name=name, grid=(S // tm, nj),
        in_specs=[row, row, row, hid, hid, vec, vec, wcol, wcol, wrow],
        out_specs=(row, hid, hid, row, vec, vec),
        out_shape=(SDS((S, D), f32), SDS((nj, S, F8), MXU_DTYPE), SDS((nj, S, F8), MXU_DTYPE),
                   SDS((S, D), MXU_DTYPE), SDS((1, D), f32), SDS((1, D), f32)),
        scratch_shapes=[pltpu.VMEM((tm, D), f32)], args=(dxo, x, y, p, q, gpre, gpost, wg, wu, wd))


def ffn_bwd_dw(h, dy, t, da, db, name, comm=None):
    S, D = h.shape
    NS, F8 = t.shape[0], t.shape[-1]
    per = N_DEV // NS
    w8 = F8 // per
    tm = _row_tile(S, FFN_ROWS_DW)
    ni = S // tm

    def body(h_ref, dy_ref, t_ref, da_ref, db_ref, dwg_ref, dwu_ref, dwd_ref, accg, accu, accd):
        i = pl.program_id(1)

        @pl.when(i == 0)
        def _():
            accg[...] = jnp.zeros_like(accg)
            accu[...] = jnp.zeros_like(accu)
            accd[...] = jnp.zeros_like(accd)

        hh = h_ref[...]
        accg[...] += _mm_tn(hh, da_ref[...])
        accu[...] += _mm_tn(hh, db_ref[...])
        accd[...] += _mm_tn(t_ref[...], dy_ref[...])

        @pl.when(i == ni - 1)
        def _():
            for k in range(per):
                ks = slice(k * w8, (k + 1) * w8)
                dwg_ref[k] = accg[:, ks].astype(dwg_ref.dtype)
                dwu_ref[k] = accu[:, ks].astype(dwu_ref.dtype)
                dwd_ref[k] = accd[ks, :].astype(dwd_ref.dtype)

    row = pl.BlockSpec((tm, D), lambda j, i: (i, 0))
    hid = pl.BlockSpec((None, tm, F8), lambda j, i: (j, i, 0))
    wcol = pl.BlockSpec((per, D, w8), lambda j, i: (j, 0, 0))
    wrow = pl.BlockSpec((per, w8, D), lambda j, i: (j, 0, 0))
    return hosted_call(
        body, comm, _grid_steps(NS, ni), name=name, grid=(NS, ni),
        in_specs=[row, row, hid, hid, hid],
        out_specs=(wcol, wcol, wrow),
        out_shape=(SDS((N_DEV, D, w8), MXU_DTYPE), SDS((N_DEV, D, w8), MXU_DTYPE), SDS((N_DEV, w8, D), MXU_DTYPE)),
        scratch_shapes=[pltpu.VMEM((D, F8), f32), pltpu.VMEM((D, F8), f32), pltpu.VMEM((F8, D), f32)],
        args=(h, dy, t, da, db))


def ffn_bwd_dw_one(rows_op, slab_op, hidden_rows, name, comm=None):
    S, D = rows_op.shape
    NS, F8 = slab_op.shape[0], slab_op.shape[-1]
    per = N_DEV // NS
    w8 = F8 // per
    tm = _row_tile(S, FFN_ROWS_DW)
    ni = S // tm

    def body(r_ref, s_ref, o_ref, acc):
        i = pl.program_id(1)

        @pl.when(i == 0)
        def _():
            acc[...] = jnp.zeros_like(acc)

        acc[...] += _mm_tn(s_ref[...], r_ref[...]) if hidden_rows else _mm_tn(r_ref[...], s_ref[...])

        @pl.when(i == ni - 1)
        def _():
            for k in range(per):
                ks = slice(k * w8, (k + 1) * w8)
                o_ref[k] = (acc[ks, :] if hidden_rows else acc[:, ks]).astype(o_ref.dtype)

    blk = (per, w8, D) if hidden_rows else (per, D, w8)
    return hosted_call(
        body, comm, _grid_steps(NS, ni), name=name, grid=(NS, ni),
        in_specs=[pl.BlockSpec((tm, D), lambda j, i: (i, 0)), pl.BlockSpec((None, tm, F8), lambda j, i: (j, i, 0))],
        out_specs=(pl.BlockSpec(blk, lambda j, i: (j, 0, 0)),),
        out_shape=(SDS((N_DEV,) + blk[1:], MXU_DTYPE),),
        scratch_shapes=[pltpu.VMEM((F8, D) if hidden_rows else (D, F8), f32)], args=(rows_op, slab_op))


def rms_mm(x, g, w, w2, name, tn=1024):
    S, D = x.shape
    N = w.shape[1]
    tm = _row_tile(S, PROJ_ROWS)
    tn = _row_tile(N, tn)
    has2 = w2 is not None

    def body(*refs):
        if has2:
            x_ref, g_ref, w_ref, w2_ref, h_ref, o_ref, o2_ref = refs
        else:
            x_ref, g_ref, w_ref, h_ref, o_ref = refs
        j = pl.program_id(1)

        @pl.when(j == 0)
        def _():
            h = _rms(x_ref[...], g_ref[...]).astype(h_ref.dtype)
            h_ref[...] = h
            if has2:
                o2_ref[...] = _mm(h, w2_ref[...])

        o_ref[...] = _mm(h_ref[...], w_ref[...])

    row = pl.BlockSpec((tm, D), lambda i, j: (i, 0))
    in_specs = [row, pl.BlockSpec((1, D), lambda i, j: (0, 0)), pl.BlockSpec((D, tn), lambda i, j: (0, j))]
    out_specs = [row, pl.BlockSpec((tm, tn), lambda i, j: (i, j))]
    out_shape = [SDS((S, D), MXU_DTYPE), SDS((S, N), f32)]
    args = [x, g, w]
    if has2:
        in_specs.append(pl.BlockSpec((D, w2.shape[1]), lambda i, j: (0, 0)))
        out_specs.append(pl.BlockSpec((tm, w2.shape[1]), lambda i, j: (i, 0)))
        out_shape.append(SDS((S, w2.shape[1]), f32))
        args.append(w2)
    return pl.pallas_call(
        body, name=name, grid=(S // tm, N // tn), in_specs=in_specs, out_specs=tuple(out_specs),
        out_shape=tuple(out_shape), compiler_params=_params(2),
    )(*args)


def mm_bwd_dx(dres, x, g, dy, w, dy2, w2, name, tk=1024, comm=None):
    S, D = x.shape
    K = dy.shape[1]
    tm = _row_tile(S, PROJ_ROWS)
    tk = _row_tile(K, tk)
    nk = K // tk
    has2 = dy2 is not None

    def body(*refs):
        if has2:
            dres_ref, x_ref, g_ref, dy_ref, w_ref, dy2_ref, w2_ref, dx_ref, dg_ref, dh_ref = refs
        else:
            dres_ref, x_ref, g_ref, dy_ref, w_ref, dx_ref, dg_ref, dh_ref = refs
        i, k = pl.program_id(0), pl.program_id(1)

        @pl.when(k == 0)
        def _():
            @pl.when(i == 0)
            def _():
                dg_ref[...] = jnp.zeros_like(dg_ref)

            if has2:
                dh_ref[...] = _mm_nt(dy2_ref[...], w2_ref[...])
            else:
                dh_ref[...] = jnp.zeros_like(dh_ref)

        dh_ref[...] += _mm_nt(dy_ref[...], w_ref[...])

        @pl.when(k == nk - 1)
        def _():
            dxx, dg = _rms_bwd(x_ref[...], g_ref[...], dh_ref[...])
            dx_ref[...] = dres_ref[...] + dxx
            dg_ref[...] += dg

    row = pl.BlockSpec((tm, D), lambda i, k: (i, 0))
    vec = pl.BlockSpec((1, D), lambda i, k: (0, 0))
    in_specs = [row, row, vec, pl.BlockSpec((tm, tk), lambda i, k: (i, k)), pl.BlockSpec((D, tk), lambda i, k: (0, k))]
    args = [dres, x, g, dy, w]
    if has2:
        in_specs += [pl.BlockSpec((tm, dy2.shape[1]), lambda i, k: (i, 0)),
                     pl.BlockSpec((D, w2.shape[1]), lambda i, k: (0, 0))]
        args += [dy2, w2]
    return hosted_call(
        body, comm, _grid_steps(S // tm, nk), name=name, grid=(S // tm, nk), in_specs=in_specs, out_specs=(row, vec),
        out_shape=(SDS((S, D), f32), SDS((1, D), f32)), scratch_shapes=[pltpu.VMEM((tm, D), f32)], args=args)


def tn_mm(a, b, name, tn=512, slot_major=False):
    S, K1 = a.shape
    N = b.shape[1]
    tm = _row_tile(S, TN_ROWS)
    tn = _row_tile(N, tn)
    ni = S // tm

    def body(a_ref, b_ref, o_ref, acc):
        i = pl.program_id(1)

        @pl.when(i == 0)
        def _():
            acc[...] = jnp.zeros_like(acc)

        acc[...] += _mm_tn(a_ref[...], b_ref[...])

        @pl.when(i == ni - 1)
        def _():
            o_ref[...] = acc[...].astype(o_ref.dtype)

    if slot_major:
        out_spec, out_shape = pl.BlockSpec((None, K1, tn), lambda j, i: (j, 0, 0)), SDS((N // tn, K1, tn), MXU_DTYPE)
    else:
        out_spec, out_shape = pl.BlockSpec((K1, tn), lambda j, i: (0, j)), SDS((K1, N), MXU_DTYPE)
    return pl.pallas_call(
        body, name=name, grid=(N // tn, ni),
        in_specs=[pl.BlockSpec((tm, K1), lambda j, i: (i, 0)), pl.BlockSpec((tm, tn), lambda j, i: (i, j))],
        out_specs=out_spec, out_shape=out_shape,
        scratch_shapes=[pltpu.VMEM((K1, tn), f32)], compiler_params=_params(2),
    )(a, b)


CONV_ROWS = 512


def _shift_down(cur, prev8, s):
    r = pltpu.roll(cur, s, 0)
    row = lax.broadcasted_iota(jnp.int32, (8, cur.shape[1]), 0)
    top = jnp.where(row < s, pltpu.roll(prev8, s, 0), r[0:8])
    return jnp.concatenate([top, r[8:]], axis=0)


def _shift_up(cur, next8, s):
    n = cur.shape[0]
    r = pltpu.roll(cur, n - s, 0)
    row = lax.broadcasted_iota(jnp.int32, (8, cur.shape[1]), 0)
    bot = jnp.where(row >= 8 - s, pltpu.roll(next8, 8 - s, 0), r[n - 8:])
    return jnp.concatenate([r[:n - 8], bot], axis=0)


def _conv_taps(cur, prev8):
    return [_shift_down(cur, prev8, 3), _shift_down(cur, prev8, 2), _shift_down(cur, prev8, 1), cur]


def _act_qk(c):
    a = _silu(c)
    return a * lax.rsqrt(jnp.sum(a * a, axis=-1, keepdims=True) + L2_EPS)


def dn_prep(proj, conv_w, name):
    S = proj.shape[0]
    W = conv_w.shape[1] // 3
    nh = W // HEAD
    R = _row_tile(S, CONV_ROWS)

    def body(p_ref, w_ref, o_ref):
        j = pl.program_id(0)
        w = w_ref[...]

        def rows(r, prev8):
            cur = p_ref[pl.ds(r, R), :]
            taps = _conv_taps(cur, prev8)
            cv = taps[0] * w[0:1] + taps[1] * w[1:2] + taps[2] * w[2:3] + taps[3] * w[3:4]

            @pl.when(j < 2 * nh)
            def _():
                o_ref[pl.ds(r, R), :] = _act_qk(cv)

            @pl.when(j >= 2 * nh)
            def _():
                o_ref[pl.ds(r, R), :] = _silu(cv)

        rows(0, jnp.zeros((8, HEAD), f32))

        @pl.loop(1, S // R)
        def _(t):
            r = pl.multiple_of(t * R, R)
            rows(r, p_ref[pl.ds(r - 8, 8), :])

    return pl.pallas_call(
        body, name=name, grid=(3 * nh,),
        in_specs=[pl.BlockSpec((S, HEAD), lambda j: (0, j)), pl.BlockSpec((CONV_K, HEAD), lambda j: (0, j))],
        out_specs=pl.BlockSpec((None, S, HEAD), lambda j: (j // nh, 0, j % nh)),
        out_shape=SDS((3, S, W), f32), compiler_params=_params(1),
    )(proj, conv_w)


def dn_prep_bwd(proj, conv_w, dqkv, dz, name, comm=None):
    S = proj.shape[0]
    W = conv_w.shape[1] // 3
    nh = W // HEAD
    nq = 3 * nh
    R = _row_tile(S, CONV_ROWS)
    nr = S // R

    def body(p_ref, w_ref, dq_ref, dz_ref, dp_ref, dw_ref, dc_ref):
        j = pl.program_id(0)

        @pl.when(j >= nq)
        def _():
            dp_ref[...] = dz_ref[...].astype(dp_ref.dtype)

        @pl.when(j < nq)
        def _():
            w = w_ref[...]
            dw_ref[...] = jnp.zeros_like(dw_ref)

            def rows(r, prev8):
                cur = p_ref[pl.ds(r, R), :]
                taps = _conv_taps(cur, prev8)
                cv = taps[0] * w[0:1] + taps[1] * w[1:2] + taps[2] * w[2:3] + taps[3] * w[3:4]
                dn = dq_ref[pl.ds(r, R), :]

                @pl.when(j < 2 * nh)
                def _():
                    dc_ref[pl.ds(r, R), :] = jax.vjp(_act_qk, cv)[1](dn)[0]

                @pl.when(j >= 2 * nh)
                def _():
                    dc_ref[pl.ds(r, R), :] = jax.vjp(_silu, cv)[1](dn)[0]

                dc = dc_ref[pl.ds(r, R), :]
                dw_ref[...] += jnp.concatenate(
                    [jnp.sum(dc * taps[q], axis=0, keepdims=True) for q in range(CONV_K)], axis=0)

            rows(0, jnp.zeros((8, HEAD), f32))

            @pl.loop(1, nr)
            def _(t):
                r = pl.multiple_of(t * R, R)
                rows(r, p_ref[pl.ds(r - 8, 8), :])

            def back(r, next8):
                dc = dc_ref[pl.ds(r, R), :]
                dx = dc * w[3:4]
                for s in (1, 2, 3):
                    dx = dx + _shift_up(dc, next8, s) * w[3 - s:4 - s]
                dp_ref[pl.ds(r, R), :] = dx.astype(dp_ref.dtype)

            @pl.loop(0, nr - 1)
            def _(t):
                r = pl.multiple_of(t * R, R)
                back(r, dc_ref[pl.ds(r + R, 8), :])

            back((nr - 1) * R, jnp.zeros((8, HEAD), f32))

    clamp = lambda j: jnp.minimum(j, nq - 1)
    return hosted_call(
        body, comm, _grid_steps(4 * nh), name=name, grid=(4 * nh,),
        in_specs=[pl.BlockSpec((S, HEAD), lambda j: (0, clamp(j))),
                  pl.BlockSpec((CONV_K, HEAD), lambda j: (0, clamp(j))),
                  pl.BlockSpec((None, S, HEAD), lambda j: (clamp(j) // nh, 0, clamp(j) % nh)),
                  pl.BlockSpec((S, HEAD), lambda j: (0, jnp.maximum(j - nq, 0)))],
        out_specs=(pl.BlockSpec((S, HEAD), lambda j: (0, j)), pl.BlockSpec((CONV_K, HEAD), lambda j: (0, clamp(j)))),
        out_shape=(SDS((S, 4 * W), MXU_DTYPE), SDS((CONV_K, 3 * W), f32)),
        scratch_shapes=[pltpu.VMEM((S, HEAD), f32)], args=(proj, conv_w, dqkv, dz))


def _lane_pick(x, lane):
    sel = lax.broadcasted_iota(jnp.int32, x.shape, 1) == lane
    return jnp.broadcast_to(jnp.sum(jnp.where(sel, x, 0.0), axis=1, keepdims=True), x.shape)


CUM_ROWS = 256


def _sel_mm(m01, x):
    m = _c(m01)
    d = lambda p: lax.dot_general(m, p, (NN, ((), ())), preferred_element_type=f32)
    h1, h2, h3 = _pieces3(x)
    return (d(h1) + d(h2)) + d(h3)


def _chunk_cumsum_matrix(n, transpose):
    r, c = lax.broadcasted_iota(jnp.int32, (n, n), 0), lax.broadcasted_iota(jnp.int32, (n, n), 1)
    sh = int(math.log2(DN_CHUNK))
    same = (r >> sh) == (c >> sh)
    return jnp.where(same & ((r <= c) if transpose else (r >= c)), 1.0, 0.0).astype(f32)


def _gates_by_lane(H, p, al, dt):
    lane = lax.broadcasted_iota(jnp.int32, p.shape, 1)
    g = -jnp.exp(al) * jax.nn.softplus(p + dt)
    return jnp.where(lane < H, jax.nn.sigmoid(p), jnp.where(lane < 2 * H, g, 0.0))


def dn_gates(pba, al, dt, H, name):
    S = pba.shape[0]
    R = _row_tile(S, CUM_ROWS)

    def body(p_ref, al_ref, dt_ref, o_ref):
        raw = _gates_by_lane(H, p_ref[...], al_ref[...], dt_ref[...])
        lane = lax.broadcasted_iota(jnp.int32, raw.shape, 1)
        o_ref[...] = jnp.where(lane < H, raw, _sel_mm(_chunk_cumsum_matrix(R, False), raw))

    blk = pl.BlockSpec((R, HEAD), lambda i: (i, 0))
    par = pl.BlockSpec((1, HEAD), lambda i: (0, 0))
    return pl.pallas_call(body, name=name, grid=(S // R,), in_specs=[blk, par, par], out_specs=blk,
                          out_shape=SDS((S, HEAD), f32), compiler_params=_params(1))(pba, al, dt)


def dn_gates_bwd(pba, al, dt, dgates, H, name):
    S = pba.shape[0]
    R = _row_tile(S, CUM_ROWS)

    def body(p_ref, al_ref, dt_ref, dg_ref, dp_ref, dal_ref, ddt_ref):
        @pl.when(pl.program_id(0) == 0)
        def _():
            dal_ref[...] = jnp.zeros_like(dal_ref)
            ddt_ref[...] = jnp.zeros_like(ddt_ref)

        d = dg_ref[...]
        lane = lax.broadcasted_iota(jnp.int32, d.shape, 1)
        d = jnp.where(lane < H, d, _sel_mm(_chunk_cumsum_matrix(R, True), d))
        _, vjp = jax.vjp(functools.partial(_gates_by_lane, H), p_ref[...], al_ref[...], dt_ref[...])
        dp, dal, ddt = vjp(d)
        dp_ref[...] = dp.astype(dp_ref.dtype)
        dal_ref[...] += dal
        ddt_ref[...] += ddt

    blk = pl.BlockSpec((R, HEAD), lambda i: (i, 0))
    par = pl.BlockSpec((1, HEAD), lambda i: (0, 0))
    return pl.pallas_call(
        body, name=name, grid=(S // R,), in_specs=[blk, par, par, blk], out_specs=(blk, par, par),
        out_shape=(SDS((S, HEAD), MXU_DTYPE), SDS((1, HEAD), f32), SDS((1, HEAD), f32)), compiler_params=_params(1),
    )(pba, al, dt, dgates)


def _bdot(dims):
    back = {NN: ((NT, 'gb'), (TN, 'ag')), NT: ((NN, 'gb'), (TN, 'ga')), TN: ((NT, 'bg'), (NN, 'ag'))}[dims]
    d = lambda p, q, dm: lax.dot_general(_c(p), _c(q), (dm, ((), ())), preferred_element_type=f32)

    @jax.custom_vjp
    def f(a, b):
        return d(a, b, dims)

    def fwd(a, b):
        return d(a, b, dims), (a, b)

    def bwd(res, g):
        v = {'a': res[0], 'b': res[1], 'g': g}
        (da_dims, da_ops), (db_dims, db_ops) = back
        return d(v[da_ops[0]], v[da_ops[1]], da_dims), d(v[db_ops[0]], v[db_ops[1]], db_dims)

    f.defvjp(fwd, bwd)
    return f, lambda a, b: d(a, b, dims)


_BDOT = {dims: _bdot(dims) for dims in (NN, NT, TN)}


def _tri_inv_multi(Ls):
    n = Ls[0].shape[0]
    eye = jnp.where(lax.broadcasted_iota(jnp.int32, (n, n), 0) == lax.broadcasted_iota(jnp.int32, (n, n), 1), 1.0, 0.0)
    P = tuple(-L for L in Ls)
    T = tuple(eye + p for p in P)
    for _ in range(int(math.log2(n)) - 1):
        P = tuple(_dot3(p, p, NN) for p in P)
        T = tuple(t + _dot3(t, p, NN) for t, p in zip(T, P))
    return T


@jax.custom_vjp
def _tri_inv_multi_vjp(Ls):
    return _tri_inv_multi(Ls)


def _tri_inv_fwd(Ls):
    T = _tri_inv_multi(Ls)
    return T, T


def _tri_inv_bwd(T, dT):
    X = tuple(_dot3(d, t, NT) for d, t in zip(dT, T))
    return (tuple(-_dot3(t, x, TN) for t, x in zip(T, X)),)


_tri_inv_multi_vjp.defvjp(_tri_inv_fwd, _tri_inv_bwd)


def _pieces3(x):
    h1 = x.astype(MXU_DTYPE)
    r1 = x - h1.astype(f32)
    h2 = r1.astype(MXU_DTYPE)
    return h1, h2, (r1 - h2.astype(f32)).astype(MXU_DTYPE)


def _row_bcast_impl(sel_row, gc):
    s = _c(sel_row)
    d = lambda p: lax.dot_general(s, p, (NT, ((), ())), preferred_element_type=f32)
    h1, h2, h3 = _pieces3(gc)
    return (d(h1) + d(h2)) + d(h3)


def _row_bcast_bwd(sel_row, d):
    s = _c(sel_row)
    hi, lo = _split(d)
    t = lambda p: lax.dot_general(p, s, (TN, ((), ())), preferred_element_type=f32)
    return jnp.zeros_like(sel_row), t(hi) + t(lo)


_row_bcast = jax.custom_vjp(_row_bcast_impl)
_row_bcast.defvjp(lambda sel_row, gc: (_row_bcast_impl(sel_row, gc), sel_row), _row_bcast_bwd)


def _col_bcast_impl(gc):
    return gc[:, :DN_CHUNK]


def _col_bcast_bwd(_, d):
    return (jnp.broadcast_to(jnp.sum(d, axis=1, keepdims=True) * (1.0 / HEAD), (d.shape[0], HEAD)),)


_col_bcast = jax.custom_vjp(_col_bcast_impl)
_col_bcast.defvjp(lambda gc: (_col_bcast_impl(gc), None), _col_bcast_bwd)


def _last_row_bcast(n):
    def impl(gc):
        return jnp.broadcast_to(gc[DN_CHUNK - 1:DN_CHUNK, :], (n, HEAD))

    def bwd(_, d):
        row = lax.broadcasted_iota(jnp.int32, (DN_CHUNK, HEAD), 0)
        return (jnp.where(row == DN_CHUNK - 1, jnp.sum(d, axis=0, keepdims=True), 0.0),)

    f = jax.custom_vjp(impl)
    f.defvjp(lambda gc: (impl(gc), None), bwd)
    return impl, f


_LAST_C, _LAST_H = _last_row_bcast(DN_CHUNK), _last_row_bcast(HEAD)


def _chunk_consts():
    C = DN_CHUNK
    io = lambda shape, ax: lax.broadcasted_iota(jnp.int32, shape, ax)
    one = lambda m: jnp.where(m, 1.0, 0.0).astype(f32)
    r, c = io((C, C), 0), io((C, C), 1)
    return dict(causal=r >= c, strict=r > c, sel_row=one(io((C, HEAD), 1) == 0))


def _chunk_fn(kc, diff, q, k, v, gc, bB, S0):
    i = 0 if diff else 1
    mm, mm_nt, mm_tn = _BDOT[NN][i], _BDOT[NT][i], _BDOT[TN][i]
    tri = _tri_inv_multi_vjp if diff else _tri_inv_multi
    each = lambda f, *ls: tuple(f(*a) for a in zip(*ls))
    gcol = each(_col_bcast if diff else _col_bcast_impl, gc)
    grow = each(lambda g: (_row_bcast if diff else _row_bcast_impl)(kc['sel_row'], g), gc)
    glc = each(_LAST_C[i ^ 1], gc)
    glh = each(_LAST_H[i ^ 1], gc)
    decay = each(lambda a, b: jnp.where(kc['causal'], jnp.exp(jnp.where(kc['causal'], a - b, 0.0)), 0.0), gcol, grow)
    kb = each(lambda a, b: a * b, k, bB)
    vb = each(lambda a, b: a * b, v, bB)
    egc = each(jnp.exp, gc)
    kk = each(mm_nt, kb, k)
    T = tri(each(lambda a, d: jnp.where(kc['strict'], a * d, 0.0), kk, decay))
    u = each(mm, T, vb)
    w = each(mm, T, each(lambda a, b: a * b, kb, egc))
    qs = each(lambda a: a * (HEAD ** -0.5), q)
    qk = each(mm_nt, qs, k)
    attn = each(lambda a, d: jnp.where(kc['causal'], a * d, 0.0), qk, decay)
    wS = each(mm, w, S0)
    qS = each(mm, each(lambda a, b: a * b, qs, egc), S0)
    v_new = each(lambda a, b: a - b, u, wS)
    o = each(lambda a, b: a + b, qS, each(mm, attn, v_new))
    kdec = each(lambda a, gl, g: a * jnp.exp(gl - g), k, glc, gc)
    S1 = each(lambda s, gl, kv: s * jnp.exp(gl) + kv, S0, glh, each(mm_tn, kdec, v_new))
    return o, S1


def _chunks_per_step(N):
    return 2 if N % 2 == 0 else 1


def _heads_per_block(H):
    return 8 if H % 8 == 0 else (4 if H % 4 == 0 else 1)


def dn_chunk_fwd(qkv, gates, name, comm=None):
    _, S, W = qkv.shape
    H, C = W // HEAD, DN_CHUNK
    N, HB = S // C, _heads_per_block(H)
    assert HB == H
    CPS = _chunks_per_step(N)

    def body(q_ref, k_ref, v_ref, g_ref, o_ref, st_ref, s_scr):
        @pl.when(pl.program_id(1) == 0)
        def _():
            s_scr[...] = jnp.zeros_like(s_scr)

        kc = _chunk_consts()
        sls = [slice(hh * HEAD, (hh + 1) * HEAD) for hh in range(HB)]
        St = tuple(s_scr[hh] for hh in range(HB))
        for c in range(CPS):
            rows = slice(c * C, (c + 1) * C)
            heads = lambda ref: tuple(ref[rows, sl] for sl in sls)
            gr = g_ref[rows, :]
            for hh in range(HB):
                st_ref[c, hh] = St[hh]
            o, St = _chunk_fn(kc, False, heads(q_ref), heads(k_ref), heads(v_ref),
                              tuple(_lane_pick(gr, H + hh) for hh in range(HB)),
                              tuple(_lane_pick(gr, hh) for hh in range(HB)), St)
            for hh in range(HB):
                o_ref[rows, sls[hh]] = o[hh]
        for hh in range(HB):
            s_scr[hh] = St[hh]

    part = lambda p: pl.BlockSpec((None, CPS * C, HB * HEAD), lambda hb, n: (p, n, hb))
    return hosted_call(
        body, comm, _grid_steps(H // HB, N // CPS), name=name, grid=(H // HB, N // CPS),
        in_specs=[part(0), part(1), part(2), pl.BlockSpec((CPS * C, HEAD), lambda hb, n: (n, 0))],
        out_specs=(pl.BlockSpec((CPS * C, HB * HEAD), lambda hb, n: (n, hb)),
                   pl.BlockSpec((CPS, HB, HEAD, HEAD), lambda hb, n: (n, hb, 0, 0))),
        out_shape=(SDS((S, W), f32), SDS((N, H, HEAD, HEAD), f32)),
        scratch_shapes=[pltpu.VMEM((HB, HEAD, HEAD), f32)], args=(qkv, qkv, qkv, gates))


def dn_chunk_bwd(qkv, gates, states, do, name, comm=None):
    _, S, W = qkv.shape
    H, C = W // HEAD, DN_CHUNK
    N, HB = S // C, _heads_per_block(H)
    assert HB == H
    CPS = _chunks_per_step(N)
    NB = N // CPS

    def body(q_ref, k_ref, v_ref, g_ref, st_ref, do_ref, dqkv_ref, dg_ref, ds_scr):
        @pl.when(pl.program_id(1) == 0)
        def _():
            ds_scr[...] = jnp.zeros_like(ds_scr)

        kc = _chunk_consts()
        sls = [slice(hh * HEAD, (hh + 1) * HEAD) for hh in range(HB)]
        dSt = tuple(ds_scr[hh] for hh in range(HB))
        for c in reversed(range(CPS)):
            rows = slice(c * C, (c + 1) * C)
            heads = lambda ref: tuple(ref[rows, sl] for sl in sls)
            gr = g_ref[rows, :]
            _, vjp = jax.vjp(functools.partial(_chunk_fn, kc, True), heads(q_ref), heads(k_ref), heads(v_ref),
                             tuple(_lane_pick(gr, H + hh) for hh in range(HB)),
                             tuple(_lane_pick(gr, hh) for hh in range(HB)), tuple(st_ref[c, hh] for hh in range(HB)))
            dq, dk, dv, dg, db, dSt = vjp((heads(do_ref), dSt))
            lane = lax.broadcasted_iota(jnp.int32, (C, HEAD), 1)
            dgr = jnp.zeros((C, HEAD), f32)
            for hh in range(HB):
                dqkv_ref[0, rows, sls[hh]] = dq[hh]
                dqkv_ref[1, rows, sls[hh]] = dk[hh]
                dqkv_ref[2, rows, sls[hh]] = dv[hh]
                dgr = dgr + jnp.where(lane == hh, jnp.sum(db[hh], axis=1, keepdims=True), 0.0)
                dgr = dgr + jnp.where(lane == H + hh, jnp.sum(dg[hh], axis=1, keepdims=True), 0.0)
            dg_ref[rows, :] = dgr
        for hh in range(HB):
            ds_scr[hh] = dSt[hh]

    rev = lambda n: NB - 1 - n
    part = lambda p: pl.BlockSpec((None, CPS * C, HB * HEAD), lambda hb, n: (p, rev(n), hb))
    gate = pl.BlockSpec((CPS * C, HEAD), lambda hb, n: (rev(n), 0))
    return hosted_call(
        body, comm, _grid_steps(H // HB, NB), name=name, grid=(H // HB, NB),
        in_specs=[part(0), part(1), part(2), gate,
                  pl.BlockSpec((CPS, HB, HEAD, HEAD), lambda hb, n: (rev(n), hb, 0, 0)),
                  pl.BlockSpec((CPS * C, HB * HEAD), lambda hb, n: (rev(n), hb))],
        out_specs=(pl.BlockSpec((3, CPS * C, HB * HEAD), lambda hb, n: (0, rev(n), hb)), gate),
        out_shape=(SDS((3, S, W), f32), SDS((S, HEAD), f32)),
        scratch_shapes=[pltpu.VMEM((HB, HEAD, HEAD), f32)], args=(qkv, qkv, qkv, gates, states, do))


def _gate_norm(o, z, ng):
    return _rms(o, ng) * _silu(z)


def dn_out(o, proj, ng, wout, x1, g3, name):
    S, W = o.shape
    D = x1.shape[1]
    nh = W // HEAD
    tm = _row_tile(S, 256)

    def body(o_ref, z_ref, ng_ref, w_ref, x_ref, g_ref, xo_ref, m_ref, og_ref):
        for h in range(nh):
            sl = slice(h * HEAD, (h + 1) * HEAD)
            og_ref[:, sl] = _gate_norm(o_ref[:, sl], z_ref[:, sl], ng_ref[...]).astype(og_ref.dtype)
        m = _mm(og_ref[...], w_ref[...])
        m_ref[...] = m
        xo_ref[...] = x_ref[...] + _rms(m, g_ref[...])

    rw = pl.BlockSpec((tm, W), lambda i: (i, 0))
    rd = pl.BlockSpec((tm, D), lambda i: (i, 0))
    return pl.pallas_call(
        body, name=name, grid=(S // tm,),
        in_specs=[rw, pl.BlockSpec((tm, W), lambda i: (i, 3)), pl.BlockSpec((1, HEAD), lambda i: (0, 0)),
                  pl.BlockSpec((W, D), lambda i: (0, 0)), rd, pl.BlockSpec((1, D), lambda i: (0, 0))],
        out_specs=(rd, rd, rw),
        out_shape=(SDS((S, D), f32), SDS((S, D), f32), SDS((S, W), MXU_DTYPE)), compiler_params=_params(1),
    )(o, proj, ng, wout, x1, g3)


def dn_out_bwd(dxo, m, g3, o, proj, ng, wout, name):
    S, W = o.shape
    D = m.shape[1]
    nh = W // HEAD
    tm = _row_tile(S, 256)

    def body(dxo_ref, m_ref, g_ref, o_ref, z_ref, ng_ref, w_ref, dm_ref, do_ref, dz_ref, dng_ref, dg_ref):
        @pl.when(pl.program_id(0) == 0)
        def _():
            dng_ref[...] = jnp.zeros_like(dng_ref)
            dg_ref[...] = jnp.zeros_like(dg_ref)

        dm, dg = _rms_bwd(m_ref[...], g_ref[...], dxo_ref[...])
        dg_ref[...] += dg
        dmc = dm.astype(dm_ref.dtype)
        dm_ref[...] = dmc
        dog = _mm_nt(dmc, w_ref[...])
        for h in range(nh):
            sl = slice(h * HEAD, (h + 1) * HEAD)
            _, vjp = jax.vjp(_gate_norm, o_ref[:, sl], z_ref[:, sl], ng_ref[...])
            do, dz, dng = vjp(dog[:, sl])
            do_ref[:, sl] = do
            dz_ref[:, sl] = dz.astype(dz_ref.dtype)
            dng_ref[...] += dng

    rw = pl.BlockSpec((tm, W), lambda i: (i, 0))
    rd = pl.BlockSpec((tm, D), lambda i: (i, 0))
    vd = pl.BlockSpec((1, D), lambda i: (0, 0))
    vh = pl.BlockSpec((1, HEAD), lambda i: (0, 0))
    return pl.pallas_call(
        body, name=name, grid=(S // tm,),
        in_specs=[rd, rd, vd, rw, pl.BlockSpec((tm, W), lambda i: (i, 3)), vh, pl.BlockSpec((W, D), lambda i: (0, 0))],
        out_specs=(rd, rw, rw, vh, vd),
        out_shape=(SDS((S, D), MXU_DTYPE), SDS((S, W), f32), SDS((S, W), MXU_DTYPE), SDS((1, HEAD), f32),
                   SDS((1, D), f32)),
        compiler_params=_params(1),
    )(dxo, m, g3, o, proj, ng, wout)


def _sg_stage1(pu, pv, bu, bv, lg, lb):
    u = _gelu(pu + bu)
    t = _gelu(pv + bv)
    tc = t - jnp.mean(t, axis=-1, keepdims=True)
    v = tc * lax.rsqrt(jnp.mean(tc * tc, axis=-1, keepdims=True) + LN_EPS) * lg + lb
    return u, v


def _causal_mask(n):
    return lax.broadcasted_iota(jnp.int32, (n, n), 0) >= lax.broadcasted_iota(jnp.int32, (n, n), 1)


def sg_mid(pre, b_in, ln_g, ln_b, w_s, bsT, wout, x1, g3, name):
    S = pre.shape[0]
    E, D = ln_g.shape[1], x1.shape[1]
    G, CH = SG_GROUPS, SG_CHUNK
    Cg = E // G
    tm = _row_tile(S, 256)

    def body(pu_ref, pv_ref, bu_ref, bv_ref, lg_ref, lb_ref, ws_ref, bs_ref, w_ref, x_ref, g_ref,
             xo_ref, m_ref, gt_ref):
        u, v = _sg_stage1(pu_ref[...], pv_ref[...], bu_ref[...], bv_ref[...], lg_ref[...], lb_ref[...])
        mask = _causal_mask(CH)
        for g in range(G):
            wc = _c(jnp.where(mask, ws_ref[g], 0.0))
            bcol = bs_ref[:, g:g + 1]
            cs = slice(g * Cg, (g + 1) * Cg)
            for ch in range(tm // CH):
                rs = slice(ch * CH, (ch + 1) * CH)
                mixed = _mm(wc, _c(v[rs, cs])) + bcol
                gt_ref[rs, cs] = (u[rs, cs] * mixed).astype(gt_ref.dtype)
        m = _mm(gt_ref[...], w_ref[...])
        m_ref[...] = m
        xo_ref[...] = x_ref[...] + _rms(m, g_ref[...])

    half = lambda p: pl.BlockSpec((tm, E), lambda i: (i, p))
    vhalf = lambda p: pl.BlockSpec((1, E), lambda i: (0, p))
    ve = pl.BlockSpec((1, E), lambda i: (0, 0))
    rd = pl.BlockSpec((tm, D), lambda i: (i, 0))
    return pl.pallas_call(
        body, name=name, grid=(S // tm,),
        in_specs=[half(0), half(1), vhalf(0), vhalf(1), ve, ve, pl.BlockSpec((G, CH, CH), lambda i: (0, 0, 0)),
                  pl.BlockSpec((CH, G), lambda i: (0, 0)), pl.BlockSpec((E, D), lambda i: (0, 0)), rd,
                  pl.BlockSpec((1, D), lambda i: (0, 0))],
        out_specs=(rd, rd, pl.BlockSpec((tm, E), lambda i: (i, 0))),
        out_shape=(SDS((S, D), f32), SDS((S, D), f32), SDS((S, E), MXU_DTYPE)), compiler_params=_params(1),
    )(pre, pre, b_in, b_in, ln_g, ln_b, w_s, bsT, wout, x1, g3)


def sg_mid_bwd(dxo, m, g3, pre, b_in, ln_g, ln_b, w_s, bsT, wout, name):
    S = pre.shape[0]
    E, D = ln_g.shape[1], m.shape[1]
    G, CH = SG_GROUPS, SG_CHUNK
    Cg = E // G
    tm = _row_tile(S, 256)

    def body(dxo_ref, m_ref, g_ref, pu_ref, pv_ref, bu_ref, bv_ref, lg_ref, lb_ref, ws_ref, bs_ref, w_ref,
             dm_ref, dpre_ref, dbin_ref, dlg_ref, dlb_ref, dws_ref, dbs_ref, dg_ref, du_scr, dv_scr):
        @pl.when(pl.program_id(0) == 0)
        def _():
            for r in (dbin_ref, dlg_ref, dlb_ref, dws_ref, dbs_ref, dg_ref):
                r[...] = jnp.zeros_like(r)

        dm, dg = _rms_bwd(m_ref[...], g_ref[...], dxo_ref[...])
        dg_ref[...] += dg
        dmc = dm.astype(dm_ref.dtype)
        dm_ref[...] = dmc
        dgated = _mm_nt(dmc, w_ref[...])
        (u, v), vjp1 = jax.vjp(_sg_stage1, pu_ref[...], pv_ref[...], bu_ref[...], bv_ref[...], lg_ref[...],
                               lb_ref[...])
        mask = _causal_mask(CH)
        lane = lax.broadcasted_iota(jnp.int32, (CH, CH), 1)
        for g in range(G):
            wc = _c(jnp.where(mask, ws_ref[g], 0.0))
            bcol = bs_ref[:, g:g + 1]
            cs = slice(g * Cg, (g + 1) * Cg)
            dws = jnp.zeros((CH, CH), f32)
            dbs = jnp.zeros((CH, 1), f32)
            for ch in range(tm // CH):
                rs = slice(ch * CH, (ch + 1) * CH)
                vs = _c(v[rs, cs])
                mixed = _mm(wc, vs) + bcol
                dgt = dgated[rs, cs]
                du_scr[rs, cs] = dgt * mixed
                dmixed = dgt * u[rs, cs]
                dmc2 = _c(dmixed)
                dv_scr[rs, cs] = _mm_tn(wc, dmc2)
                dws = dws + _mm_nt(dmc2, vs)
                dbs = dbs + jnp.sum(dmixed, axis=1, keepdims=True)
            dws_ref[g] += jnp.where(mask, dws, 0.0)
            dbs_ref[...] += jnp.where(lane == g, jnp.broadcast_to(dbs, (CH, CH)), 0.0)
        dpu, dpv, dbu, dbv, dlg, dlb = vjp1((du_scr[...], dv_scr[...]))
        dpre_ref[:, :E] = dpu.astype(dpre_ref.dtype)
        dpre_ref[:, E:] = dpv.astype(dpre_ref.dtype)
        dbin_ref[:, :E] += dbu
        dbin_ref[:, E:] += dbv
        dlg_ref[...] += dlg
        dlb_ref[...] += dlb

    half = lambda p: pl.BlockSpec((tm, E), lambda i: (i, p))
    vhalf = lambda p: pl.BlockSpec((1, E), lambda i: (0, p))
    ve = pl.BlockSpec((1, E), lambda i: (0, 0))
    rd = pl.BlockSpec((tm, D), lambda i: (i, 0))
    vd = pl.BlockSpec((1, D), lambda i: (0, 0))
    wsb = pl.BlockSpec((G, CH, CH), lambda i: (0, 0, 0))
    return pl.pallas_call(
        body, name=name, grid=(S // tm,),
        in_specs=[rd, rd, vd, half(0), half(1), vhalf(0), vhalf(1), ve, ve, wsb,
                  pl.BlockSpec((CH, G), lambda i: (0, 0)), pl.BlockSpec((E, D), lambda i: (0, 0))],
        out_specs=(rd, pl.BlockSpec((tm, 2 * E), lambda i: (i, 0)), pl.BlockSpec((1, 2 * E), lambda i: (0, 0)), ve, ve,
                   wsb, pl.BlockSpec((CH, CH), lambda i: (0, 0)), vd),
        out_shape=(SDS((S, D), MXU_DTYPE), SDS((S, 2 * E), MXU_DTYPE), SDS((1, 2 * E), f32), SDS((1, E), f32),
                   SDS((1, E), f32), SDS((G, CH, CH), f32), SDS((CH, CH), f32), SDS((1, D), f32)),
        scratch_shapes=[pltpu.VMEM((tm, E), f32), pltpu.VMEM((tm, E), f32)], compiler_params=_params(1),
    )(dxo, m, g3, pre, pre, b_in, b_in, ln_g, ln_b, w_s, bsT, wout)


def loss_head(y, target, name):
    S, D = y.shape
    tm = _row_tile(S, 512)

    def body(y_ref, t_ref, l_ref, d_ref):
        @pl.when(pl.program_id(0) == 0)
        def _():
            l_ref[...] = jnp.zeros_like(l_ref)

        e = y_ref[...] - t_ref[...]
        d_ref[...] = e * (1.0 / D)
        l_ref[...] += jnp.sum(e * e) * (0.5 / D)

    row = pl.BlockSpec((tm, D), lambda i: (i, 0))
    return pl.pallas_call(
        body, name=name, grid=(S // tm,), in_specs=[row, row],
        out_specs=(pl.BlockSpec((1, HEAD), lambda i: (0, 0)), row),
        out_shape=(SDS((1, HEAD), f32), SDS((S, D), f32)), compiler_params=_params(1),
    )(y, target)


def sum_slots(r, name):
    _, R, C = r.shape
    tr = _row_tile(R, 648 if R % 648 == 0 else R)

    def body(r_ref, o_ref):
        acc = r_ref[0].astype(f32)
        for s in range(1, N_DEV):
            acc = acc + r_ref[s].astype(f32)
        o_ref[...] = acc

    return pl.pallas_call(
        body, name=name, grid=(R // tr,), in_specs=[pl.BlockSpec((N_DEV, tr, C), lambda i: (0, i, 0))],
        out_specs=pl.BlockSpec((tr, C), lambda i: (i, 0)), out_shape=SDS((R, C), f32), compiler_params=_params(1),
    )(r)


def _adam_math(w, g, m, v):
    m = ADAM_B1 * m + (1.0 - ADAM_B1) * g
    v = ADAM_B2 * v + (1.0 - ADAM_B2) * (g * g)
    m_hat = m / (1.0 - ADAM_B1 ** ADAM_STEP)
    v_hat = v / (1.0 - ADAM_B2 ** ADAM_STEP)
    delta = -ADAM_LR * (m_hat / (jnp.sqrt(v_hat) + ADAM_EPS) + ADAM_WD * w)
    return delta, m, v


def adam_slots(w, rs, m, v, name, tr):
    R, C = w.shape
    tr = _row_tile(min(r.shape[1] for r in rs), tr)
    blocks = [r.shape[1] // tr for r in rs]
    starts = [sum(blocks[:k]) for k in range(len(rs))]
    assert sum(blocks) * tr == R

    def body(w_ref, *refs):
        r_refs, (m_ref, v_ref, g_ref, d_ref, mo_ref, vo_ref) = refs[:len(rs)], refs[len(rs):]
        i = pl.program_id(0)
        for k, r_ref in enumerate(r_refs):
            @pl.when((i >= starts[k]) & (i < starts[k] + blocks[k]))
            def _():
                g = r_ref[0].astype(f32)
                for s in range(1, N_DEV):
                    g = g + r_ref[s].astype(f32)
                g_ref[...] = g

        d_ref[...], mo_ref[...], vo_ref[...] = _adam_math(w_ref[...], g_ref[...], m_ref[...], v_ref[...])

    row = pl.BlockSpec((tr, C), lambda i: (i, 0))
    piece = lambda k: pl.BlockSpec((N_DEV, tr, C), lambda i: (0, jnp.clip(i - starts[k], 0, blocks[k] - 1), 0))
    return pl.pallas_call(
        body, name=name, grid=(R // tr,), in_specs=[row] + [piece(k) for k in range(len(rs))] + [row, row],
        out_specs=(row, row, row, row), out_shape=tuple(SDS((R, C), f32) for _ in range(4)),
        compiler_params=_params(1),
    )(w, *rs, m, v)


def adam_small(w, g, m, v, name):
    def body(w_ref, g_ref, m_ref, v_ref, d_ref, mo_ref, vo_ref):
        d_ref[...], mo_ref[...], vo_ref[...] = _adam_math(w_ref[...], g_ref[...], m_ref[...], v_ref[...])

    return pl.pallas_call(body, name=name, out_shape=tuple(SDS(w.shape, f32) for _ in range(3)))(w, g, m, v)


def _pack_rows(parts):
    rows, offs, r = [], [], 0
    for p in parts:
        flat = p.reshape(-1)
        n = -(-flat.shape[0] // HEAD)
        flat = jnp.pad(flat, (0, n * HEAD - flat.shape[0]))
        rows.append(flat.reshape(n, HEAD))
        offs.append((r, n))
        r += n
    pad = (-r) % 8
    if pad:
        rows.append(jnp.zeros((pad, HEAD), f32))
    return jnp.concatenate(rows, axis=0), offs


def kernel(x, norm_g, ffn_w_gate, ffn_w_up, ffn_w_down, dn_w_in, dn_conv_w, dn_a_log, dn_dt_bias, dn_norm_g, dn_w_out, sg_w_in, sg_b_in, sg_ln_g, sg_ln_b, sg_w_s, sg_b_s, sg_w_out, loss_target, m_norm_g, m_ffn_w_gate, m_ffn_w_up, m_ffn_w_down, m_dn_w_in, m_dn_conv_w, m_dn_a_log, m_dn_dt_bias, m_dn_norm_g, m_dn_w_out, m_sg_w_in, m_sg_b_in, m_sg_ln_g, m_sg_ln_b, m_sg_w_s, m_sg_b_s, m_sg_w_out, v_norm_g, v_ffn_w_gate, v_ffn_w_up, v_ffn_w_down, v_dn_w_in, v_dn_conv_w, v_dn_a_log, v_dn_dt_bias, v_dn_norm_g, v_dn_w_out, v_sg_w_in, v_sg_b_in, v_sg_ln_g, v_sg_ln_b, v_sg_w_s, v_sg_b_s, v_sg_w_out):
    weights = dict(norm_g=norm_g, ffn_w_gate=ffn_w_gate, ffn_w_up=ffn_w_up, ffn_w_down=ffn_w_down, dn_w_in=dn_w_in,
                   dn_conv_w=dn_conv_w, dn_a_log=dn_a_log, dn_dt_bias=dn_dt_bias, dn_norm_g=dn_norm_g,
                   dn_w_out=dn_w_out, sg_w_in=sg_w_in, sg_b_in=sg_b_in, sg_ln_g=sg_ln_g, sg_ln_b=sg_ln_b,
                   sg_w_s=sg_w_s, sg_b_s=sg_b_s, sg_w_out=sg_w_out)
    mom_m = dict(norm_g=m_norm_g, ffn_w_gate=m_ffn_w_gate, ffn_w_up=m_ffn_w_up, ffn_w_down=m_ffn_w_down,
                 dn_w_in=m_dn_w_in, dn_conv_w=m_dn_conv_w, dn_a_log=m_dn_a_log, dn_dt_bias=m_dn_dt_bias,
                 dn_norm_g=m_dn_norm_g, dn_w_out=m_dn_w_out, sg_w_in=m_sg_w_in, sg_b_in=m_sg_b_in,
                 sg_ln_g=m_sg_ln_g, sg_ln_b=m_sg_ln_b, sg_w_s=m_sg_w_s, sg_b_s=m_sg_b_s, sg_w_out=m_sg_w_out)
    mom_v = dict(norm_g=v_norm_g, ffn_w_gate=v_ffn_w_gate, ffn_w_up=v_ffn_w_up, ffn_w_down=v_ffn_w_down,
                 dn_w_in=v_dn_w_in, dn_conv_w=v_dn_conv_w, dn_a_log=v_dn_a_log, dn_dt_bias=v_dn_dt_bias,
                 dn_norm_g=v_dn_norm_g, dn_w_out=v_dn_w_out, sg_w_in=v_sg_w_in, sg_b_in=v_sg_b_in,
                 sg_ln_g=v_sg_ln_g, sg_ln_b=v_sg_ln_b, sg_w_s=v_sg_w_s, sg_b_s=v_sg_b_s, sg_w_out=v_sg_w_out)
    order = list(weights)

    xs = x[0]
    S, D = xs.shape
    F8 = ffn_w_gate.shape[-1]
    depth = norm_g.shape[0]
    W = dn_w_out.shape[1] * N_DEV
    H = W // HEAD
    E = sg_ln_g.shape[1] * N_DEV
    G, CH = sg_w_s.shape[1], sg_w_s.shape[2]
    c8 = dn_w_in.shape[2]
    me = _slot(lax.axis_index("x"), lax.axis_index("y"), lax.axis_index("c"))

    assert depth == 2
    small_in, small_offs = _pack_rows([norm_g, dn_conv_w, sg_b_in, sg_ln_g, sg_ln_b])
    wg0a, wu0a, wd0a, small_all = all_gather_multi(
        [_c(ffn_w_gate[0, 0]), _c(ffn_w_up[0, 0]), _c(ffn_w_down[0, 0]), small_in], name="gather_first")
    ffn_shards = lambda l, ab: [_c(ffn_w_gate[l, ab]), _c(ffn_w_up[l, ab]), _c(ffn_w_down[l, ab])]
    gather_dn = Comm("gather", [_c(dn_w_in[0]), _c(dn_w_out[0])])
    gather_mid = Comm("gather", ffn_shards(0, 1) + ffn_shards(1, 0))
    gather_end = Comm("gather", ffn_shards(1, 1))
    gather_sg = Comm("gather", [_c(sg_w_in[0]), _c(sg_w_out[0])])
    per = N_DEV // FFN_SLABS
    wide_cols = lambda w: jnp.transpose(w.reshape(FFN_SLABS, per, D, F8), (0, 2, 1, 3)).reshape(FFN_SLABS, D, per * F8)
    wide = lambda g, u, d: (wide_cols(g), wide_cols(u), d.reshape(FFN_SLABS, per * F8, D))
    ffn_w = {(0, 0): wide(wg0a, wu0a, wd0a)}

    def small_piece(i, shard_shape):
        r0, n = small_offs[i]
        sz = math.prod(shard_shape)
        return small_all[:, r0:r0 + n, :].reshape(N_DEV, n * HEAD)[:, :sz].reshape((N_DEV,) + tuple(shard_shape))

    ng_full = jnp.moveaxis(small_piece(0, norm_g.shape), 0, 2).reshape(depth, 6, D)
    conv_full = jnp.moveaxis(small_piece(1, dn_conv_w.shape[1:]), 0, 1).reshape(CONV_K, 3 * W)
    bin_full = small_piece(2, sg_b_in.shape[1:]).reshape(1, 2 * E)
    lng_full = small_piece(3, sg_ln_g.shape[1:]).reshape(1, E)
    lnb_full = small_piece(4, sg_ln_b.shape[1:]).reshape(1, E)
    gate_lanes = lambda v: jnp.pad(v.reshape(1, H), ((0, 0), (H, HEAD - 2 * H)))
    al_row, dt_row = gate_lanes(dn_a_log), gate_lanes(dn_dt_bias)
    bsT = sg_b_s[0].T
    gvec = lambda l, k: ng_full[l, k].reshape(1, D)

    saved = []
    cur = xs
    for l in range(depth):
        sv = {}
        sv['x0'] = cur
        (cur, sv['hA'], sv['pA'], sv['qA'], sv['tA'], sv['yA']), got = ffn_fwd(
            cur, gvec(l, 0), gvec(l, 1), *ffn_w[l, 0], name=f"ffn_fwd_{l}a", comm=gather_dn if l == 0 else gather_sg)
        sv['x1'] = cur
        if l == 1:
            sg_win = jnp.moveaxis(got[0], 0, 1).reshape(D, 2 * E)
            sg_wout = got[1].reshape(E, D)
        if l == 0:
            dnin_all, dnout_all = got
            dn_win = jnp.moveaxis(dnin_all, 0, 1).reshape(D, N_DEV * c8)
            dn_wmain = dn_win[:, :4 * W]
            dn_wba = jnp.pad(dn_win[:, 4 * W:], ((0, 0), (0, HEAD - 2 * H)))
            dn_wout = dnout_all.reshape(W, D)
            sv['hM'], sv['proj'], sv['pba'] = rms_mm(cur, gvec(l, 2), dn_wmain, dn_wba, name=f"dn_in_{l}")
            sv['qkv'] = dn_prep(sv['proj'], conv_full, name=f"dn_prep_{l}")
            sv['gates'] = dn_gates(sv['pba'], al_row, dt_row, H, name=f"dn_gates_{l}")
            (sv['o'], sv['states']), got = dn_chunk_fwd(sv['qkv'], sv['gates'], name=f"dn_chunk_{l}", comm=gather_mid)
            ffn_w[0, 1], ffn_w[1, 0] = wide(*got[0:3]), wide(*got[3:6])
            cur, sv['m'], sv['og'] = dn_out(sv['o'], sv['proj'], dn_norm_g, dn_wout, cur, gvec(l, 3), name=f"dn_out_{l}")
        else:
            sv['hM'], sv['pre'] = rms_mm(cur, gvec(l, 2), sg_win, None, name=f"sg_in_{l}")
            cur, sv['m'], sv['gated'] = sg_mid(sv['pre'], bin_full, lng_full, lnb_full, sg_w_s[0], bsT, sg_wout, cur,
                                               gvec(l, 3), name=f"sg_mid_{l}")
        sv['x2'] = cur
        (cur, sv['hB'], sv['pB'], sv['qB'], sv['tB'], sv['yB']), got = ffn_fwd(
            cur, gvec(l, 4), gvec(l, 5), *ffn_w[l, 1], name=f"ffn_fwd_{l}b", comm=gather_end if l == 0 else None)
        if l == 0:
            ffn_w[1, 1] = wide(*got[0:3])
        saved.append(sv)

    loss_blk, dcur = loss_head(cur, loss_target[0], name="loss_head")
    loss = lax.psum(loss_blk[0, 0], ("x", "y", "c"))

    dng = [[None] * 6 for _ in range(depth)]
    ffn_dw = {}
    grads, slots = {}, {}

    def ffn_backward(l, ab, dcur):
        sv, s = saved[l], 'AB'[ab]
        (dcur, da, db, dy, dng[l][4 * ab], dng[l][4 * ab + 1]), _ = ffn_bwd_dx(
            dcur, sv['x2' if ab else 'x0'], sv['y' + s], sv['p' + s], sv['q' + s], gvec(l, 4 * ab), gvec(l, 4 * ab + 1),
            *ffn_w[l, ab], name=f"ffn_bwd_{l}{'ab'[ab]}")
        ffn_dw[l, ab], _ = ffn_bwd_dw(sv['h' + s], dy, sv['t' + s], da, db, name=f"ffn_dw_{l}{'ab'[ab]}")
        return dcur

    sv = saved[1]
    dcur = ffn_backward(1, 1, dcur)
    dm, dpre, grads['sg_b_in'], grads['sg_ln_g'], grads['sg_ln_b'], grads['sg_w_s'], dbs, dng[1][3] = sg_mid_bwd(
        dcur, sv['m'], gvec(1, 3), sv['pre'], bin_full, lng_full, lnb_full, sg_w_s[0], bsT, sg_wout, name="sg_mid_bwd_1")
    grads['sg_b_s'] = dbs[:, :G].T
    dsg_wout = tn_mm(sv['gated'], dm, name="sg_wout_dw_1").reshape(N_DEV, E // N_DEV, D)
    dsg_win = tn_mm(sv['hM'], dpre, name="sg_win_dw_1", tn=2 * E // N_DEV, slot_major=True)
    (dcur, dng[1][2]), _ = mm_bwd_dx(dcur, sv['x1'], gvec(1, 2), dpre, sg_win, None, None, name="sg_in_bwd_1")
    dcur = ffn_backward(1, 0, dcur)
    sv = saved[0]
    dcur = ffn_backward(0, 1, dcur)
    dm, do, dz, grads['dn_norm_g'], dng[0][3] = dn_out_bwd(dcur, sv['m'], gvec(0, 3), sv['o'], sv['proj'], dn_norm_g,
                                                          dn_wout, name="dn_out_bwd_0")
    ddn_wout = tn_mm(sv['og'], dm, name="dn_wout_dw_0").reshape(N_DEV, W // N_DEV, D)
    (dqkv, dgates), got = dn_chunk_bwd(sv['qkv'], sv['gates'], sv['states'], do, name="dn_chunk_bwd_0",
                                       comm=Comm("exchange", [*ffn_dw[1, 0], *ffn_dw[1, 1], dsg_win, dsg_wout]))
    l1a, l1b, slots['sg_w_in'], slots['sg_w_out'] = got[0:3], got[3:6], [got[6]], [got[7]]
    dpba, dal, ddt = dn_gates_bwd(sv['pba'], al_row, dt_row, dgates, H, name="dn_gates_bwd_0")
    grads['dn_a_log'] = dal[:, H:2 * H]
    grads['dn_dt_bias'] = ddt[:, H:2 * H]
    (dproj, grads['dn_conv_w']), got = dn_prep_bwd(sv['proj'], conv_full, dqkv, dz, name="dn_prep_bwd_0",
                                                   comm=Comm("exchange", [*ffn_dw[0, 1], ddn_wout]))
    l0b, slots['dn_w_out'] = got[0:3], [got[3]]
    dw_main = tn_mm(sv['hM'], dproj, name="dn_win_dw_0")
    dw_ba = tn_mm(sv['hM'], dpba, name="dn_wba_dw_0", tn=HEAD)
    dw_in = jnp.concatenate([dw_main, dw_ba[:, :2 * H]], axis=1)
    ddn_win = jnp.moveaxis(dw_in.reshape(D, N_DEV, c8), 1, 0)
    (dcur, dng[0][2]), got = mm_bwd_dx(dcur, sv['x1'], gvec(0, 2), dproj, dn_wmain, dpba, dn_wba, name="dn_in_bwd_0",
                                       comm=Comm("exchange", [ddn_win]))
    slots['dn_w_in'] = [got[0]]
    small_names = ['norm_g', 'dn_conv_w', 'sg_b_in', 'sg_ln_g', 'sg_ln_b', 'sg_w_s', 'sg_b_s', 'dn_a_log',
                   'dn_dt_bias', 'dn_norm_g']
    small = {}

    def gather_small():
        dng_full = jnp.stack([jnp.concatenate(r, axis=0) for r in dng], axis=0)
        small['parts'] = [dng_full, grads['dn_conv_w'], grads['sg_b_in'], grads['sg_ln_g'], grads['sg_ln_b'],
                          grads['sg_w_s'], grads['sg_b_s'], grads['dn_a_log'], grads['dn_dt_bias'], grads['dn_norm_g']]
        pack, small['offs'] = _pack_rows(small['parts'])
        return Comm("gather", [pack])

    (dcur, da, db, dy, dng[0][0], dng[0][1]), _ = ffn_bwd_dx(
        dcur, sv['x0'], sv['yA'], sv['pA'], sv['qA'], gvec(0, 0), gvec(0, 1), *ffn_w[0, 0], name="ffn_bwd_0a")
    grad_x = dcur[None]
    (dg,), (small_slots,) = ffn_bwd_dw_one(sv['hA'], da, False, name="ffn_dw_0a_gate", comm=gather_small())
    (du,), (xg,) = ffn_bwd_dw_one(sv['hA'], db, False, name="ffn_dw_0a_up", comm=Comm("exchange", [dg]))
    (dd,), (xu,) = ffn_bwd_dw_one(dy, sv['tA'], True, name="ffn_dw_0a_down", comm=Comm("exchange", [du]))
    small_parts, offs = small['parts'], small['offs']
    l0a = [xg, xu, exchange_slots([dd], name="exchange_last")[0]]
    for i, nm in enumerate(['ffn_w_gate', 'ffn_w_up', 'ffn_w_down']):
        slots[nm] = [l0a[i], l0b[i], l1a[i], l1b[i]]
    big_names = ['ffn_w_gate', 'ffn_w_up', 'ffn_w_down', 'dn_w_in', 'dn_w_out', 'sg_w_in', 'sg_w_out']
    slots = [slots[nm] for nm in big_names]
    small_sum = sum_slots(small_slots, name="sum_small_grads")

    def small_grad(i):
        r0, n = offs[i]
        p = small_parts[i]
        return small_sum[r0:r0 + n].reshape(-1)[:p.size].reshape(p.shape)

    def my_shard(full, axis, like):
        n = full.shape[axis] // N_DEV
        return lax.dynamic_slice_in_dim(full, me * n, n, axis).reshape(like.shape)

    g_small = {
        'norm_g': my_shard(small_grad(0), 2, norm_g),
        'dn_conv_w': my_shard(small_grad(1), 1, dn_conv_w),
        'sg_b_in': my_shard(small_grad(2), 1, sg_b_in),
        'sg_ln_g': my_shard(small_grad(3), 1, sg_ln_g),
        'sg_ln_b': my_shard(small_grad(4), 1, sg_ln_b),
        'sg_w_s': small_grad(5).reshape(sg_w_s.shape),
        'sg_b_s': small_grad(6).reshape(sg_b_s.shape),
        'dn_a_log': small_grad(7).reshape(dn_a_log.shape),
        'dn_dt_bias': small_grad(8).reshape(dn_dt_bias.shape),
        'dn_norm_g': small_grad(9).reshape(dn_norm_g.shape),
    }

    out_g, out_d, out_m, out_v = {}, {}, {}, {}
    for nm, r in zip(big_names, slots):
        w = weights[nm]
        cols = w.shape[-1]
        rows = w.size // cols
        tr = {'ffn_w_gate': 512, 'ffn_w_up': 512, 'ffn_w_down': F8 // 2, 'dn_w_in': 256, 'sg_w_in': 256}.get(nm, rows)
        pieces = [p.reshape(N_DEV, -1, cols) for p in r]
        g, d, m2, v2 = adam_slots(w.reshape(rows, cols), pieces, mom_m[nm].reshape(rows, cols),
                                  mom_v[nm].reshape(rows, cols), name=f"adam_{nm}", tr=tr)
        out_g[nm], out_d[nm], out_m[nm], out_v[nm] = (t.reshape(w.shape) for t in (g, d, m2, v2))
    for nm in small_names:
        w = weights[nm]
        cols = w.shape[-1]
        rows = w.size // cols
        two = lambda t: t.reshape(rows, cols)
        d, m2, v2 = adam_small(two(w), two(g_small[nm]), two(mom_m[nm]), two(mom_v[nm]), name=f"adam_{nm}")
        out_g[nm] = g_small[nm]
        out_d[nm], out_m[nm], out_v[nm] = (t.reshape(w.shape) for t in (d, m2, v2))

    return (loss, grad_x, *[out_g[n] for n in order], *[out_d[n] for n in order], *[out_m[n] for n in order],
            *[out_v[n] for n in order])
```

```python
import functools
import math

import jax
import jax.numpy as jnp
from jax import lax
from jax.experimental import pallas as pl
from jax.experimental.pallas import tpu as pltpu

f32 = jnp.float32
MXU_DTYPE = jnp.bfloat16
N_DEV = 8
RMS_EPS = 1e-6
LN_EPS = 1e-5
L2_EPS = 1e-6
HEAD = 128
DN_CHUNK = 64
SG_CHUNK = 128
SG_GROUPS = 8
CONV_K = 4
ADAM_LR, ADAM_B1, ADAM_B2, ADAM_EPS, ADAM_WD, ADAM_STEP = 0.001, 0.9, 0.999, 1e-08, 0.01, 10
VMEM_LIMIT = 56 * 1024 * 1024
FFN_ROWS_FWD, FFN_ROWS_BWD, FFN_ROWS_DW = 1024, 512, 2048
PROJ_ROWS, TN_ROWS = 1024, 2048
FFN_SLABS = 4
SDS = jax.ShapeDtypeStruct
HIGHEST = lax.Precision.HIGHEST
MESH = pl.DeviceIdType.MESH


def _params(n_grid):
    return pltpu.CompilerParams(dimension_semantics=("arbitrary",) * n_grid, vmem_limit_bytes=VMEM_LIMIT)


def _row_tile(s, want):
    t = min(s, want)
    assert s % t == 0, (s, t)
    return t


def _rms(x, g):
    return x * lax.rsqrt(jnp.mean(x * x, axis=-1, keepdims=True) + RMS_EPS) * g


def _rms_bwd(x, g, dy):
    _, vjp = jax.vjp(_rms, x, g)
    return vjp(dy)


def _silu(a):
    return a * jax.nn.sigmoid(a)


def _mm(a, b):
    return lax.dot_general(a, b, (((1,), (0,)), ((), ())), preferred_element_type=f32)


def _mm_nt(a, b):
    return lax.dot_general(a, b, (((1,), (1,)), ((), ())), preferred_element_type=f32)


def _mm_tn(a, b):
    return lax.dot_general(a, b, (((0,), (0,)), ((), ())), preferred_element_type=f32)


def _c(x):
    return x.astype(MXU_DTYPE)


def _split(a):
    hi = a.astype(MXU_DTYPE)
    lo = (a - hi.astype(f32)).astype(MXU_DTYPE)
    return hi, lo


def _dot3(a, b, dims):
    ah, al = _split(a)
    bh, bl = _split(b)
    d = lambda p, q: lax.dot_general(p, q, (dims, ((), ())), preferred_element_type=f32)
    return d(ah, bh) + (d(ah, bl) + d(al, bh))


NN, NT, TN = ((1,), (0,)), ((1,), (1,)), ((0,), (0,))


def _slot(px, py, pc):
    return 4 * px + 2 * py + pc


def all_gather_multi(arrs, name):
    return Comm("gather", arrs).alone(name)


def exchange_slots(arrs, name):
    return Comm("exchange", arrs).alone(name)


class Comm:
    def __init__(self, kind, arrs):
        self.kind, self.arrs, self.n = kind, list(arrs), len(arrs)
        hbm = pl.BlockSpec(memory_space=pltpu.HBM)
        self.in_specs = [hbm] * self.n
        self.out_specs = [hbm] * self.n
        lead = (N_DEV,) if kind == "gather" else ()
        self.out_shape = [SDS(lead + tuple(a.shape), a.dtype) for a in self.arrs]
        self.scratch = [pltpu.SemaphoreType.DMA((self.n, 7)), pltpu.SemaphoreType.DMA((self.n, 7)),
                        pltpu.SemaphoreType.DMA((self.n,))]

    def phase(self, p, ins, outs, sems):
        (self._gather if self.kind == "gather" else self._exchange)(p, ins, outs, sems)

    def _gather(self, p, ins, outs, sems):
        send_sems, recv_sems, local_sems = sems
        x, y, c = lax.axis_index("x"), lax.axis_index("y"), lax.axis_index("c")
        me, sibling = (x, y, c), (x, y, 1 - c)
        chips = [(1 - x, y), (x, 1 - y), (1 - x, 1 - y)]

        def copy(a, k, block, to, src=None):
            dst = outs[a].at[_slot(*block)]
            return pltpu.make_async_remote_copy(
                src_ref=dst if src is None else src, dst_ref=dst, send_sem=send_sems.at[a, k],
                recv_sem=recv_sems.at[a, k], device_id=to, device_id_type=MESH)

        mine = [pltpu.make_async_copy(ins[a], outs[a].at[_slot(*me)], local_sems.at[a]) for a in range(self.n)]
        first = [[copy(a, 0, me, sibling, src=ins[a])] +
                 [copy(a, 1 + j, me, (*chip, c), src=ins[a]) for j, chip in enumerate(chips)] for a in range(self.n)]
        passed = [[copy(a, 4 + j, (*chip, c), sibling) for j, chip in enumerate(chips)] for a in range(self.n)]
        if p == 0:
            for a in range(self.n):
                mine[a].start()
            for a in range(self.n):
                for cp in first[a]:
                    cp.start()
        elif p == 1:
            for a in range(self.n):
                for j, chip in enumerate(chips):
                    copy(a, 1 + j, (*chip, c), me).wait_recv()
                    passed[a][j].start()
        else:
            for a in range(self.n):
                copy(a, 0, sibling, me).wait_recv()
                for j, chip in enumerate(chips):
                    copy(a, 4 + j, (*chip, 1 - c), me).wait_recv()
            for a in range(self.n):
                for cp in first[a] + passed[a]:
                    cp.wait_send()
                mine[a].wait()

    def _exchange(self, p, ins, outs, sems):
        send_sems, recv_sems, local_sems = sems
        x, y, c = lax.axis_index("x"), lax.axis_index("y"), lax.axis_index("c")
        me = _slot(x, y, c)
        peers = [(x ^ (k >> 2), y ^ ((k >> 1) & 1), c ^ (k & 1)) for k in range(1, N_DEV)]

        def copy(a, k):
            peer = peers[k - 1]
            return pltpu.make_async_remote_copy(
                src_ref=ins[a].at[_slot(*peer)], dst_ref=outs[a].at[me], send_sem=send_sems.at[a, k - 1],
                recv_sem=recv_sems.at[a, k - 1], device_id=peer, device_id_type=MESH)

        def landed(a, k):
            peer = peers[k - 1]
            return pltpu.make_async_remote_copy(
                src_ref=ins[a].at[me], dst_ref=outs[a].at[_slot(*peer)], send_sem=send_sems.at[a, k - 1],
                recv_sem=recv_sems.at[a, k - 1], device_id=peer, device_id_type=MESH)

        local = [pltpu.make_async_copy(ins[a].at[me], outs[a].at[me], local_sems.at[a]) for a in range(self.n)]
        order = [6, 7, 2, 3, 4, 5, 1]
        if p == 0:
            for a in range(self.n):
                local[a].start()
            for a in range(self.n):
                for k in order:
                    copy(a, k).start()
        elif p == 2:
            for a in range(self.n):
                for k in order:
                    copy(a, k).wait_send()
                    landed(a, k).wait_recv()
                local[a].wait()

    def alone(self, name):
        n = self.n

        def body(*refs):
            for p in range(3):
                self.phase(p, refs[:n], refs[n:2 * n], refs[2 * n:])

        return pl.pallas_call(body, name=name, out_shape=tuple(self.out_shape), in_specs=self.in_specs,
                              out_specs=tuple(self.out_specs), scratch_shapes=self.scratch)(*self.arrs)


def hosted_call(body, comm, steps, *, name, grid, in_specs, out_specs, out_shape, scratch_shapes, args):
    if comm is None:
        outs = pl.pallas_call(body, name=name, grid=grid, in_specs=in_specs, out_specs=tuple(out_specs),
                              out_shape=tuple(out_shape), scratch_shapes=scratch_shapes,
                              compiler_params=_params(len(grid)))(*args)
        return outs, None
    ni, no, ns, cn = len(in_specs), len(out_specs), len(scratch_shapes), comm.n

    def both(*refs):
        h_in, c_in = refs[:ni], refs[ni:ni + cn]
        h_out, c_out = refs[ni + cn:ni + cn + no], refs[ni + cn + no:ni + 2 * cn + no]
        h_scr, c_scr = refs[ni + 2 * cn + no:ni + 2 * cn + no + ns], refs[ni + 2 * cn + no + ns:]
        when = steps()
        pl.when(when[0])(lambda: comm.phase(0, c_in, c_out, c_scr))
        body(*h_in, *h_out, *h_scr)
        pl.when(when[1])(lambda: comm.phase(1, c_in, c_out, c_scr))
        pl.when(when[2])(lambda: comm.phase(2, c_in, c_out, c_scr))

    outs = pl.pallas_call(
        both, name=name, grid=grid, in_specs=list(in_specs) + comm.in_specs,
        out_specs=tuple(out_specs) + tuple(comm.out_specs), out_shape=tuple(out_shape) + tuple(comm.out_shape),
        scratch_shapes=list(scratch_shapes) + comm.scratch, compiler_params=_params(len(grid)),
    )(*args, *comm.arrs)
    return outs[:no], outs[no:]


def _grid_steps(n_outer, n_inner=1):
    total = n_outer * n_inner

    def steps():
        t = pl.program_id(0) * n_inner + (pl.program_id(1) if n_inner > 1 else 0)
        return t == 0, t == (total * 5) // 8, t == total - 1
    return steps


def ffn_fwd(x, gpre, gpost, wg, wu, wd, name, comm=None):
    S, D = x.shape
    nj, F8 = wg.shape[0], wg.shape[-1]
    tm = _row_tile(S, FFN_ROWS_FWD)

    def body(x_ref, gpre_ref, gpost_ref, wg_ref, wu_ref, wd_ref, xo_ref, h_ref, p_ref, q_ref, t_ref, y_ref):
        j = pl.program_id(1)

        @pl.when(j == 0)
        def _():
            h_ref[...] = _rms(x_ref[...], gpre_ref[...]).astype(h_ref.dtype)
            y_ref[...] = jnp.zeros_like(y_ref)

        h = h_ref[...]
        a = _mm(h, wg_ref[...])
        b = _mm(h, wu_ref[...])
        s = jax.nn.sigmoid(a)
        q = a * s
        p_ref[...] = (b * (s + q * (1.0 - s))).astype(p_ref.dtype)
        q_ref[...] = q.astype(q_ref.dtype)
        t = (q * b).astype(t_ref.dtype)
        t_ref[...] = t
        y_ref[...] += _mm(t, wd_ref[...])

        @pl.when(j == nj - 1)
        def _():
            xo_ref[...] = x_ref[...] + 0.5 * _rms(y_ref[...], gpost_ref[...])

    row = pl.BlockSpec((tm, D), lambda i, j: (i, 0))
    vec = pl.BlockSpec((1, D), lambda i, j: (0, 0))
    wcol = pl.BlockSpec((None, D, F8), lambda i, j: (j, 0, 0))
    wrow = pl.BlockSpec((None, F8, D), lambda i, j: (j, 0, 0))
    hid = pl.BlockSpec((None, tm, F8), lambda i, j: (j, i, 0))
    return hosted_call(
        body, comm, _grid_steps(S // tm, nj), name=name, grid=(S // tm, nj),
        in_specs=[row, vec, vec, wcol, wcol, wrow],
        out_specs=(row, row, hid, hid, hid, row),
        out_shape=(SDS((S, D), f32), SDS((S, D), MXU_DTYPE), SDS((nj, S, F8), MXU_DTYPE),
                   SDS((nj, S, F8), MXU_DTYPE), SDS((nj, S, F8), MXU_DTYPE), SDS((S, D), f32)),
        scratch_shapes=[], args=(x, gpre, gpost, wg, wu, wd))


def ffn_bwd_dx(dxo, x, y, p, q, gpre, gpost, wg, wu, wd, name, comm=None):
    S, D = x.shape
    NS, F8 = wg.shape[0], wg.shape[-1]
    sps = 2 if NS % 2 == 0 else 1
    nj = NS // sps
    tm = _row_tile(S, FFN_ROWS_BWD)

    def body(dxo_ref, x_ref, y_ref, p_ref, q_ref, gpre_ref, gpost_ref, wg_ref, wu_ref, wd_ref,
             dx_ref, da_ref, db_ref, dy_ref, dgpre_ref, dgpost_ref, dh_ref):
        i, j = pl.program_id(0), pl.program_id(1)

        @pl.when(j == 0)
        def _():
            @pl.when(i == 0)
            def _():
                dgpre_ref[...] = jnp.zeros_like(dgpre_ref)
                dgpost_ref[...] = jnp.zeros_like(dgpost_ref)

            dy, dg = _rms_bwd(y_ref[...], gpost_ref[...], 0.5 * dxo_ref[...])
            dy_ref[...] = dy.astype(dy_ref.dtype)
            dgpost_ref[...] += dg
            dh_ref[...] = jnp.zeros_like(dh_ref)

        half = tm // 2 if tm % 16 == 0 else tm
        for r0 in range(0, tm, half):
            rs = slice(r0, r0 + half)
            upd = None
            for s in range(sps):
                dt = _mm_nt(dy_ref[rs, :], wd_ref[s])
                da = (dt * p_ref[s, rs, :].astype(f32)).astype(da_ref.dtype)
                db = (dt * q_ref[s, rs, :].astype(f32)).astype(db_ref.dtype)
                da_ref[s, rs, :] = da
                db_ref[s, rs, :] = db
                part = _mm_nt(da, wg_ref[s]) + _mm_nt(db, wu_ref[s])
                upd = part if upd is None else upd + part
            dh_ref[rs, :] += upd

        @pl.when(j == nj - 1)
        def _():
            dxx, dg = _rms_bwd(x_ref[...], gpre_ref[...], dh_ref[...])
            dx_ref[...] = dxo_ref[...] + dxx
            dgpre_ref[...] += dg

    row = pl.BlockSpec((tm, D), lambda i, j: (i, 0))
    vec = pl.BlockSpec((1, D), lambda i, j: (0, 0))
    wcol = pl.BlockSpec((sps, D, F8), lambda i, j: (j, 0, 0))
    wrow = pl.BlockSpec((sps, F8, D), lambda i, j: (j, 0, 0))
    hid = pl.BlockSpec((sps, tm, F8), lambda i, j: (j, i, 0))
    return hosted_call(
        body, comm, _grid_steps(S // tm, nj), name=name, grid=(S // tm, nj),
        in_specs=[row, row, row, hid, hid, vec, vec, wcol, wcol, wrow],
        out_specs=(row, hid, hid, row, vec, vec),
        out_shape=(SDS((S, D), f32), SDS((NS, S, F8), MXU_DTYPE), SDS((NS, S, F8), MXU_DTYPE),
                   SDS((S, D), MXU_DTYPE), SDS((1, D), f32), SDS((1, D), f32)),
        scratch_shapes=[pltpu.VMEM((tm, D), f32)], args=(dxo, x, y, p, q, gpre, gpost, wg, wu, wd))


def ffn_bwd_dw(h, dy, t, da, db, name, comm=None):
    S, D = h.shape
    NS, F8 = t.shape[0], t.shape[-1]
    per = N_DEV // NS
    w8 = F8 // per
    tm = _row_tile(S, FFN_ROWS_DW)
    ni = S // tm

    def body(h_ref, dy_ref, t_ref, da_ref, db_ref, dwg_ref, dwu_ref, dwd_ref, accg, accu, accd):
        i = pl.program_id(1)

        @pl.when(i == 0)
        def _():
            accg[...] = jnp.zeros_like(accg)
            accu[...] = jnp.zeros_like(accu)
            accd[...] = jnp.zeros_like(accd)

        hh = h_ref[...]
        accg[...] += _mm_tn(hh, da_ref[...])
        accu[...] += _mm_tn(hh, db_ref[...])
        accd[...] += _mm_tn(t_ref[...], dy_ref[...])

        @pl.when(i == ni - 1)
        def _():
            for k in range(per):
                ks = slice(k * w8, (k + 1) * w8)
                dwg_ref[k] = accg[:, ks].astype(dwg_ref.dtype)
                dwu_ref[k] = accu[:, ks].astype(dwu_ref.dtype)
                dwd_ref[k] = accd[ks, :].astype(dwd_ref.dtype)

    row = pl.BlockSpec((tm, D), lambda j, i: (i, 0))
    hid = pl.BlockSpec((None, tm, F8), lambda j, i: (j, i, 0))
    wcol = pl.BlockSpec((per, D, w8), lambda j, i: (j, 0, 0))
    wrow = pl.BlockSpec((per, w8, D), lambda j, i: (j, 0, 0))
    return hosted_call(
        body, comm, _grid_steps(NS, ni), name=name, grid=(NS, ni),
        in_specs=[row, row, hid, hid, hid],
        out_specs=(wcol, wcol, wrow),
        out_shape=(SDS((N_DEV, D, w8), MXU_DTYPE), SDS((N_DEV, D, w8), MXU_DTYPE), SDS((N_DEV, w8, D), MXU_DTYPE)),
        scratch_shapes=[pltpu.VMEM((D, F8), f32), pltpu.VMEM((D, F8), f32), pltpu.VMEM((F8, D), f32)],
        args=(h, dy, t, da, db))


def ffn_bwd_dw_one(rows_op, slab_op, hidden_rows, name, comm=None):
    S, D = rows_op.shape
    NS, F8 = slab_op.shape[0], slab_op.shape[-1]
    per = N_DEV // NS
    w8 = F8 // per
    tm = _row_tile(S, FFN_ROWS_DW)
    ni = S // tm

    def body(r_ref, s_ref, o_ref, acc):
        i = pl.program_id(1)

        @pl.when(i == 0)
        def _():
            acc[...] = jnp.zeros_like(acc)

        acc[...] += _mm_tn(s_ref[...], r_ref[...]) if hidden_rows else _mm_tn(r_ref[...], s_ref[...])

        @pl.when(i == ni - 1)
        def _():
            for k in range(per):
                ks = slice(k * w8, (k + 1) * w8)
                o_ref[k] = (acc[ks, :] if hidden_rows else acc[:, ks]).astype(o_ref.dtype)

    blk = (per, w8, D) if hidden_rows else (per, D, w8)
    return hosted_call(
        body, comm, _grid_steps(NS, ni), name=name, grid=(NS, ni),
        in_specs=[pl.BlockSpec((tm, D), lambda j, i: (i, 0)), pl.BlockSpec((None, tm, F8), lambda j, i: (j, i, 0))],
        out_specs=(pl.BlockSpec(blk, lambda j, i: (j, 0, 0)),),
        out_shape=(SDS((N_DEV,) + blk[1:], MXU_DTYPE),),
        scratch_shapes=[pltpu.VMEM((F8, D) if hidden_rows else (D, F8), f32)], args=(rows_op, slab_op))


def rms_mm(x, g, w, w2, name, tn=1024):
    S, D = x.shape
    N = w.shape[1]
    tm = _row_tile(S, PROJ_ROWS)
    tn = _row_tile(N, tn)
    has2 = w2 is not None

    def body(*refs):
        if has2:
            x_ref, g_ref, w_ref, w2_ref, h_ref, o_ref, o2_ref = refs
        else:
            x_ref, g_ref, w_ref, h_ref, o_ref = refs
        j = pl.program_id(1)

        @pl.when(j == 0)
        def _():
            h = _rms(x_ref[...], g_ref[...]).astype(h_ref.dtype)
            h_ref[...] = h
            if has2:
                o2_ref[...] = _mm(h, w2_ref[...])

        o_ref[...] = _mm(h_ref[...], w_ref[...])

    row = pl.BlockSpec((tm, D), lambda i, j: (i, 0))
    in_specs = [row, pl.BlockSpec((1, D), lambda i, j: (0, 0)), pl.BlockSpec((D, tn), lambda i, j: (0, j))]
    out_specs = [row, pl.BlockSpec((tm, tn), lambda i, j: (i, j))]
    out_shape = [SDS((S, D), MXU_DTYPE), SDS((S, N), f32)]
    args = [x, g, w]
    if has2:
        in_specs.append(pl.BlockSpec((D, w2.shape[1]), lambda i, j: (0, 0)))
        out_specs.append(pl.BlockSpec((tm, w2.shape[1]), lambda i, j: (i, 0)))
        out_shape.append(SDS((S, w2.shape[1]), f32))
        args.append(w2)
    return pl.pallas_call(
        body, name=name, grid=(S // tm, N // tn), in_specs=in_specs, out_specs=tuple(out_specs),
        out_shape=tuple(out_shape), compiler_params=_params(2),
    )(*args)


def mm_bwd_dx(dres, x, g, dy, w, dy2, w2, name, tk=1024, comm=None):
    S, D = x.shape
    K = dy.shape[1]
    tm = _row_tile(S, PROJ_ROWS)
    tk = _row_tile(K, tk)
    nk = K // tk
    has2 = dy2 is not None

    def body(*refs):
        if has2:
            dres_ref, x_ref, g_ref, dy_ref, w_ref, dy2_ref, w2_ref, dx_ref, dg_ref, dh_ref = refs
        else:
            dres_ref, x_ref, g_ref, dy_ref, w_ref, dx_ref, dg_ref, dh_ref = refs
        i, k = pl.program_id(0), pl.program_id(1)

        @pl.when(k == 0)
        def _():
            @pl.when(i == 0)
            def _():
                dg_ref[...] = jnp.zeros_like(dg_ref)

            if has2:
                dh_ref[...] = _mm_nt(dy2_ref[...], w2_ref[...])
            else:
                dh_ref[...] = jnp.zeros_like(dh_ref)

        dh_ref[...] += _mm_nt(dy_ref[...], w_ref[...])

        @pl.when(k == nk - 1)
        def _():
            dxx, dg = _rms_bwd(x_ref[...], g_ref[...], dh_ref[...])
            dx_ref[...] = dres_ref[...] + dxx
            dg_ref[...] += dg

    row = pl.BlockSpec((tm, D), lambda i, k: (i, 0))
    vec = pl.BlockSpec((1, D), lambda i, k: (0, 0))
    in_specs = [row, row, vec, pl.BlockSpec((tm, tk), lambda i, k: (i, k)), pl.BlockSpec((D, tk), lambda i, k: (0, k))]
    args = [dres, x, g, dy, w]
    if has2:
        in_specs += [pl.BlockSpec((tm, dy2.shape[1]), lambda i, k: (i, 0)),
                     pl.BlockSpec((D, w2.shape[1]), lambda i, k: (0, 0))]
        args += [dy2, w2]
    return hosted_call(
        body, comm, _grid_steps(S // tm, nk), name=name, grid=(S // tm, nk), in_specs=in_specs, out_specs=(row, vec),
        out_shape=(SDS((S, D), f32), SDS((1, D), f32)), scratch_shapes=[pltpu.VMEM((tm, D), f32)], args=args)


def tn_mm(a, b, name, tn=512, slot_major=False):
    S, K1 = a.shape
    N = b.shape[1]
    tm = _row_tile(S, TN_ROWS)
    tn = _row_tile(N, tn)
    ni = S // tm

    def body(a_ref, b_ref, o_ref, acc):
        i = pl.program_id(1)

        @pl.when(i == 0)
        def _():
            acc[...] = jnp.zeros_like(acc)

        acc[...] += _mm_tn(a_ref[...], b_ref[...])

        @pl.when(i == ni - 1)
        def _():
            o_ref[...] = acc[...].astype(o_ref.dtype)

    if slot_major:
        out_spec, out_shape = pl.BlockSpec((None, K1, tn), lambda j, i: (j, 0, 0)), SDS((N // tn, K1, tn), MXU_DTYPE)
    else:
        out_spec, out_shape = pl.BlockSpec((K1, tn), lambda j, i: (0, j)), SDS((K1, N), MXU_DTYPE)
    return pl.pallas_call(
        body, name=name, grid=(N // tn, ni),
        in_specs=[pl.BlockSpec((tm, K1), lambda j, i: (i, 0)), pl.BlockSpec((tm, tn), lambda j, i: (i, j))],
        out_specs=out_spec, out_shape=out_shape,
        scratch_shapes=[pltpu.VMEM((K1, tn), f32)], compiler_params=_params(2),
    )(a, b)


CONV_ROWS = 512


def _shift_down(cur, prev8, s):
    r = pltpu.roll(cur, s, 0)
    row = lax.broadcasted_iota(jnp.int32, (8, cur.shape[1]), 0)
    top = jnp.where(row < s, pltpu.roll(prev8, s, 0), r[0:8])
    return jnp.concatenate([top, r[8:]], axis=0)


def _shift_up(cur, next8, s):
    n = cur.shape[0]
    r = pltpu.roll(cur, n - s, 0)
    row = lax.broadcasted_iota(jnp.int32, (8, cur.shape[1]), 0)
    bot = jnp.where(row >= 8 - s, pltpu.roll(next8, 8 - s, 0), r[n - 8:])
    return jnp.concatenate([r[:n - 8], bot], axis=0)


def _conv_taps(cur, prev8):
    return [_shift_down(cur, prev8, 3), _shift_down(cur, prev8, 2), _shift_down(cur, prev8, 1), cur]


def _act_qk(c):
    a = _silu(c)
    return a * lax.rsqrt(jnp.sum(a * a, axis=-1, keepdims=True) + L2_EPS)


def dn_prep(proj, conv_w, name):
    S = proj.shape[0]
    W = conv_w.shape[1] // 3
    nh = W // HEAD
    R = _row_tile(S, CONV_ROWS)

    def body(p_ref, w_ref, o_ref):
        j = pl.program_id(0)
        w = w_ref[...]

        def rows(r, prev8):
            cur = p_ref[pl.ds(r, R), :]
            taps = _conv_taps(cur, prev8)
            cv = taps[0] * w[0:1] + taps[1] * w[1:2] + taps[2] * w[2:3] + taps[3] * w[3:4]

            @pl.when(j < 2 * nh)
            def _():
                o_ref[pl.ds(r, R), :] = _act_qk(cv)

            @pl.when(j >= 2 * nh)
            def _():
                o_ref[pl.ds(r, R), :] = _silu(cv)

        rows(0, jnp.zeros((8, HEAD), f32))

        @pl.loop(1, S // R)
        def _(t):
            r = pl.multiple_of(t * R, R)
            rows(r, p_ref[pl.ds(r - 8, 8), :])

    return pl.pallas_call(
        body, name=name, grid=(3 * nh,),
        in_specs=[pl.BlockSpec((S, HEAD), lambda j: (0, j)), pl.BlockSpec((CONV_K, HEAD), lambda j: (0, j))],
        out_specs=pl.BlockSpec((None, S, HEAD), lambda j: (j // nh, 0, j % nh)),
        out_shape=SDS((3, S, W), f32), compiler_params=_params(1),
    )(proj, conv_w)


def dn_prep_bwd(proj, conv_w, dqkv, dz, name, comm=None):
    S = proj.shape[0]
    W = conv_w.shape[1] // 3
    nh = W // HEAD
    nq = 3 * nh
    R = _row_tile(S, CONV_ROWS)
    nr = S // R

    def body(p_ref, w_ref, dq_ref, dz_ref, dp_ref, dw_ref, dc_ref):
        j = pl.program_id(0)

        @pl.when(j >= nq)
        def _():
            dp_ref[...] = dz_ref[...].astype(dp_ref.dtype)

        @pl.when(j < nq)
        def _():
            w = w_ref[...]
            dw_ref[...] = jnp.zeros_like(dw_ref)

            def rows(r, prev8):
                cur = p_ref[pl.ds(r, R), :]
                taps = _conv_taps(cur, prev8)
                cv = taps[0] * w[0:1] + taps[1] * w[1:2] + taps[2] * w[2:3] + taps[3] * w[3:4]
                dn = dq_ref[pl.ds(r, R), :]

                @pl.when(j < 2 * nh)
                def _():
                    dc_ref[pl.ds(r, R), :] = jax.vjp(_act_qk, cv)[1](dn)[0]

                @pl.when(j >= 2 * nh)
                def _():
                    dc_ref[pl.ds(r, R), :] = jax.vjp(_silu, cv)[1](dn)[0]

                dc = dc_ref[pl.ds(r, R), :]
                dw_ref[...] += jnp.concatenate(
                    [jnp.sum(dc * taps[q], axis=0, keepdims=True) for q in range(CONV_K)], axis=0)

            rows(0, jnp.zeros((8, HEAD), f32))

            @pl.loop(1, nr)
            def _(t):
                r = pl.multiple_of(t * R, R)
                rows(r, p_ref[pl.ds(r - 8, 8), :])

            def back(r, next8):
                dc = dc_ref[pl.ds(r, R), :]
                dx = dc * w[3:4]
                for s in (1, 2, 3):
                    dx = dx + _shift_up(dc, next8, s) * w[3 - s:4 - s]
                dp_ref[pl.ds(r, R), :] = dx.astype(dp_ref.dtype)

            @pl.loop(0, nr - 1)
            def _(t):
                r = pl.multiple_of(t * R, R)
                back(r, dc_ref[pl.ds(r + R, 8), :])

            back((nr - 1) * R, jnp.zeros((8, HEAD), f32))

    clamp = lambda j: jnp.minimum(j, nq - 1)
    return hosted_call(
        body, comm, _grid_steps(4 * nh), name=name, grid=(4 * nh,),
        in_specs=[pl.BlockSpec((S, HEAD), lambda j: (0, clamp(j))),
                  pl.BlockSpec((CONV_K, HEAD), lambda j: (0, clamp(j))),
                  pl.BlockSpec((None, S, HEAD), lambda j: (clamp(j) // nh, 0, clamp(j) % nh)),
                  pl.BlockSpec((S, HEAD), lambda j: (0, jnp.maximum(j - nq, 0)))],
        out_specs=(pl.BlockSpec((S, HEAD), lambda j: (0, j)), pl.BlockSpec((CONV_K, HEAD), lambda j: (0, clamp(j)))),
        out_shape=(SDS((S, 4 * W), MXU_DTYPE), SDS((CONV_K, 3 * W), f32)),
        scratch_shapes=[pltpu.VMEM((S, HEAD), f32)], args=(proj, conv_w, dqkv, dz))


def _lane_pick(x, lane):
    sel = lax.broadcasted_iota(jnp.int32, x.shape, 1) == lane
    return jnp.broadcast_to(jnp.sum(jnp.where(sel, x, 0.0), axis=1, keepdims=True), x.shape)


CUM_ROWS = 256


def _sel_mm(m01, x):
    m = _c(m01)
    d = lambda p: lax.dot_general(m, p, (NN, ((), ())), preferred_element_type=f32)
    h1, h2, h3 = _pieces3(x)
    return (d(h1) + d(h2)) + d(h3)


def _chunk_cumsum_matrix(n, transpose):
    r, c = lax.broadcasted_iota(jnp.int32, (n, n), 0), lax.broadcasted_iota(jnp.int32, (n, n), 1)
    sh = int(math.log2(DN_CHUNK))
    same = (r >> sh) == (c >> sh)
    return jnp.where(same & ((r <= c) if transpose else (r >= c)), 1.0, 0.0).astype(f32)


def _gates_by_lane(H, p, al, dt):
    lane = lax.broadcasted_iota(jnp.int32, p.shape, 1)
    g = -jnp.exp(al) * jax.nn.softplus(p + dt)
    return jnp.where(lane < H, jax.nn.sigmoid(p), jnp.where(lane < 2 * H, g, 0.0))


def dn_gates(pba, al, dt, H, name):
    S = pba.shape[0]
    R = _row_tile(S, CUM_ROWS)

    def body(p_ref, al_ref, dt_ref, o_ref):
        raw = _gates_by_lane(H, p_ref[...], al_ref[...], dt_ref[...])
        lane = lax.broadcasted_iota(jnp.int32, raw.shape, 1)
        o_ref[...] = jnp.where(lane < H, raw, _sel_mm(_chunk_cumsum_matrix(R, False), raw))

    blk = pl.BlockSpec((R, HEAD), lambda i: (i, 0))
    par = pl.BlockSpec((1, HEAD), lambda i: (0, 0))
    return pl.pallas_call(body, name=name, grid=(S // R,), in_specs=[blk, par, par], out_specs=blk,
                          out_shape=SDS((S, HEAD), f32), compiler_params=_params(1))(pba, al, dt)


def dn_gates_bwd(pba, al, dt, dgates, H, name):
    S = pba.shape[0]
    R = _row_tile(S, CUM_ROWS)

    def body(p_ref, al_ref, dt_ref, dg_ref, dp_ref, dal_ref, ddt_ref):
        @pl.when(pl.program_id(0) == 0)
        def _():
            dal_ref[...] = jnp.zeros_like(dal_ref)
            ddt_ref[...] = jnp.zeros_like(ddt_ref)

        d = dg_ref[...]
        lane = lax.broadcasted_iota(jnp.int32, d.shape, 1)
        d = jnp.where(lane < H, d, _sel_mm(_chunk_cumsum_matrix(R, True), d))
        _, vjp = jax.vjp(functools.partial(_gates_by_lane, H), p_ref[...], al_ref[...], dt_ref[...])
        dp, dal, ddt = vjp(d)
        dp_ref[...] = dp.astype(dp_ref.dtype)
        dal_ref[...] += dal
        ddt_ref[...] += ddt

    blk = pl.BlockSpec((R, HEAD), lambda i: (i, 0))
    par = pl.BlockSpec((1, HEAD), lambda i: (0, 0))
    return pl.pallas_call(
        body, name=name, grid=(S // R,), in_specs=[blk, par, par, blk], out_specs=(blk, par, par),
        out_shape=(SDS((S, HEAD), MXU_DTYPE), SDS((1, HEAD), f32), SDS((1, HEAD), f32)), compiler_params=_params(1),
    )(pba, al, dt, dgates)


def _bdot(dims):
    back = {NN: ((NT, 'gb'), (TN, 'ag')), NT: ((NN, 'gb'), (TN, 'ga')), TN: ((NT, 'bg'), (NN, 'ag'))}[dims]
    d = lambda p, q, dm: lax.dot_general(_c(p), _c(q), (dm, ((), ())), preferred_element_type=f32)

    @jax.custom_vjp
    def f(a, b):
        return d(a, b, dims)

    def fwd(a, b):
        return d(a, b, dims), (a, b)

    def bwd(res, g):
        v = {'a': res[0], 'b': res[1], 'g': g}
        (da_dims, da_ops), (db_dims, db_ops) = back
        return d(v[da_ops[0]], v[da_ops[1]], da_dims), d(v[db_ops[0]], v[db_ops[1]], db_dims)

    f.defvjp(fwd, bwd)
    return f, lambda a, b: d(a, b, dims)


_BDOT = {dims: _bdot(dims) for dims in (NN, NT, TN)}


def _tri_inv_multi(Ls):
    n = Ls[0].shape[0]
    eye = jnp.where(lax.broadcasted_iota(jnp.int32, (n, n), 0) == lax.broadcasted_iota(jnp.int32, (n, n), 1), 1.0, 0.0)
    P = tuple(-L for L in Ls)
    T = tuple(eye + p for p in P)
    for _ in range(int(math.log2(n)) - 1):
        P = tuple(_dot3(p, p, NN) for p in P)
        T = tuple(t + _dot3(t, p, NN) for t, p in zip(T, P))
    return T


@jax.custom_vjp
def _tri_inv_multi_vjp(Ls):
    return _tri_inv_multi(Ls)


def _tri_inv_fwd(Ls):
    T = _tri_inv_multi(Ls)
    return T, T


def _tri_inv_bwd(T, dT):
    X = tuple(_dot3(d, t, NT) for d, t in zip(dT, T))
    return (tuple(-_dot3(t, x, TN) for t, x in zip(T, X)),)


_tri_inv_multi_vjp.defvjp(_tri_inv_fwd, _tri_inv_bwd)


def _pieces3(x):
    h1 = x.astype(MXU_DTYPE)
    r1 = x - h1.astype(f32)
    h2 = r1.astype(MXU_DTYPE)
    return h1, h2, (r1 - h2.astype(f32)).astype(MXU_DTYPE)


def _row_bcast_impl(sel_row, gc):
    s = _c(sel_row)
    d = lambda p: lax.dot_general(s, p, (NT, ((), ())), preferred_element_type=f32)
    h1, h2, h3 = _pieces3(gc)
    return (d(h1) + d(h2)) + d(h3)


def _row_bcast_bwd(sel_row, d):
    s = _c(sel_row)
    hi, lo = _split(d)
    t = lambda p: lax.dot_general(p, s, (TN, ((), ())), preferred_element_type=f32)
    return jnp.zeros_like(sel_row), t(hi) + t(lo)


_row_bcast = jax.custom_vjp(_row_bcast_impl)
_row_bcast.defvjp(lambda sel_row, gc: (_row_bcast_impl(sel_row, gc), sel_row), _row_bcast_bwd)


def _col_bcast_impl(gc):
    return gc[:, :DN_CHUNK]


def _col_bcast_bwd(_, d):
    return (jnp.broadcast_to(jnp.sum(d, axis=1, keepdims=True) * (1.0 / HEAD), (d.shape[0], HEAD)),)


_col_bcast = jax.custom_vjp(_col_bcast_impl)
_col_bcast.defvjp(lambda gc: (_col_bcast_impl(gc), None), _col_bcast_bwd)


def _last_row_bcast(n):
    def impl(gc):
        return jnp.broadcast_to(gc[DN_CHUNK - 1:DN_CHUNK, :], (n, HEAD))

    def bwd(_, d):
        row = lax.broadcasted_iota(jnp.int32, (DN_CHUNK, HEAD), 0)
        return (jnp.where(row == DN_CHUNK - 1, jnp.sum(d, axis=0, keepdims=True), 0.0),)

    f = jax.custom_vjp(impl)
    f.defvjp(lambda gc: (impl(gc), None), bwd)
    return impl, f


_LAST_C, _LAST_H = _last_row_bcast(DN_CHUNK), _last_row_bcast(HEAD)


def _chunk_consts():
    C = DN_CHUNK
    io = lambda shape, ax: lax.broadcasted_iota(jnp.int32, shape, ax)
    one = lambda m: jnp.where(m, 1.0, 0.0).astype(f32)
    r, c = io((C, C), 0), io((C, C), 1)
    return dict(causal=r >= c, strict=r > c, sel_row=one(io((C, HEAD), 1) == 0))


def _chunk_fn(kc, diff, q, k, v, gc, bB, S0):
    i = 0 if diff else 1
    mm, mm_nt, mm_tn = _BDOT[NN][i], _BDOT[NT][i], _BDOT[TN][i]
    tri = _tri_inv_multi_vjp if diff else _tri_inv_multi
    each = lambda f, *ls: tuple(f(*a) for a in zip(*ls))
    gcol = each(_col_bcast if diff else _col_bcast_impl, gc)
    grow = each(lambda g: (_row_bcast if diff else _row_bcast_impl)(kc['sel_row'], g), gc)
    glc = each(_LAST_C[i ^ 1], gc)
    glh = each(_LAST_H[i ^ 1], gc)
    decay = each(lambda a, b: jnp.where(kc['causal'], jnp.exp(jnp.where(kc['causal'], a - b, 0.0)), 0.0), gcol, grow)
    kb = each(lambda a, b: a * b, k, bB)
    vb = each(lambda a, b: a * b, v, bB)
    egc = each(jnp.exp, gc)
    kk = each(mm_nt, kb, k)
    T = tri(each(lambda a, d: jnp.where(kc['strict'], a * d, 0.0), kk, decay))
    u = each(mm, T, vb)
    w = each(mm, T, each(lambda a, b: a * b, kb, egc))
    qs = each(lambda a: a * (HEAD ** -0.5), q)
    qk = each(mm_nt, qs, k)
    attn = each(lambda a, d: jnp.where(kc['causal'], a * d, 0.0), qk, decay)
    wS = each(mm, w, S0)
    qS = each(mm, each(lambda a, b: a * b, qs, egc), S0)
    v_new = each(lambda a, b: a - b, u, wS)
    o = each(lambda a, b: a + b, qS, each(mm, attn, v_new))
    kdec = each(lambda a, gl, g: a * jnp.exp(gl - g), k, glc, gc)
    S1 = each(lambda s, gl, kv: s * jnp.exp(gl) + kv, S0, glh, each(mm_tn, kdec, v_new))
    return o, S1


def _chunks_per_step(N):
    return 2 if N % 2 == 0 else 1


def _heads_per_block(H):
    return 8 if H % 8 == 0 else (4 if H % 4 == 0 else 1)


def dn_chunk_fwd(qkv, gates, name, comm=None):
    _, S, W = qkv.shape
    H, C = W // HEAD, DN_CHUNK
    N, HB = S // C, _heads_per_block(H)
    assert HB == H
    CPS = _chunks_per_step(N)

    def body(q_ref, k_ref, v_ref, g_ref, o_ref, st_ref, s_scr):
        @pl.when(pl.program_id(1) == 0)
        def _():
            s_scr[...] = jnp.zeros_like(s_scr)

        kc = _chunk_consts()
        sls = [slice(hh * HEAD, (hh + 1) * HEAD) for hh in range(HB)]
        St = tuple(s_scr[hh] for hh in range(HB))
        for c in range(CPS):
            rows = slice(c * C, (c + 1) * C)
            heads = lambda ref: tuple(ref[rows, sl] for sl in sls)
            gr = g_ref[rows, :]
            for hh in range(HB):
                st_ref[c, hh] = St[hh]
            o, St = _chunk_fn(kc, False, heads(q_ref), heads(k_ref), heads(v_ref),
                              tuple(_lane_pick(gr, H + hh) for hh in range(HB)),
                              tuple(_lane_pick(gr, hh) for hh in range(HB)), St)
            for hh in range(HB):
                o_ref[rows, sls[hh]] = o[hh]
        for hh in range(HB):
            s_scr[hh] = St[hh]

    part = lambda p: pl.BlockSpec((None, CPS * C, HB * HEAD), lambda hb, n: (p, n, hb))
    return hosted_call(
        body, comm, _grid_steps(H // HB, N // CPS), name=name, grid=(H // HB, N // CPS),
        in_specs=[part(0), part(1), part(2), pl.BlockSpec((CPS * C, HEAD), lambda hb, n: (n, 0))],
        out_specs=(pl.BlockSpec((CPS * C, HB * HEAD), lambda hb, n: (n, hb)),
                   pl.BlockSpec((CPS, HB, HEAD, HEAD), lambda hb, n: (n, hb, 0, 0))),
        out_shape=(SDS((S, W), f32), SDS((N, H, HEAD, HEAD), f32)),
        scratch_shapes=[pltpu.VMEM((HB, HEAD, HEAD), f32)], args=(qkv, qkv, qkv, gates))


def dn_chunk_bwd(qkv, gates, states, do, name, comm=None):
    _, S, W = qkv.shape
    H, C = W // HEAD, DN_CHUNK
    N, HB = S // C, _heads_per_block(H)
    assert HB == H
    CPS = _chunks_per_step(N)
    NB = N // CPS

    def body(q_ref, k_ref, v_ref, g_ref, st_ref, do_ref, dqkv_ref, dg_ref, ds_scr):
        @pl.when(pl.program_id(1) == 0)
        def _():
            ds_scr[...] = jnp.zeros_like(ds_scr)

        kc = _chunk_consts()
        sls = [slice(hh * HEAD, (hh + 1) * HEAD) for hh in range(HB)]
        dSt = tuple(ds_scr[hh] for hh in range(HB))
        for c in reversed(range(CPS)):
            rows = slice(c * C, (c + 1) * C)
            heads = lambda ref: tuple(ref[rows, sl] for sl in sls)
            gr = g_ref[rows, :]
            _, vjp = jax.vjp(functools.partial(_chunk_fn, kc, True), heads(q_ref), heads(k_ref), heads(v_ref),
                             tuple(_lane_pick(gr, H + hh) for hh in range(HB)),
                             tuple(_lane_pick(gr, hh) for hh in range(HB)), tuple(st_ref[c, hh] for hh in range(HB)))
            dq, dk, dv, dg, db, dSt = vjp((heads(do_ref), dSt))
            lane = lax.broadcasted_iota(jnp.int32, (C, HEAD), 1)
            dgr = jnp.zeros((C, HEAD), f32)
            for hh in range(HB):
                dqkv_ref[0, rows, sls[hh]] = dq[hh]
                dqkv_ref[1, rows, sls[hh]] = dk[hh]
                dqkv_ref[2, rows, sls[hh]] = dv[hh]
                dgr = dgr + jnp.where(lane == hh, jnp.sum(db[hh], axis=1, keepdims=True), 0.0)
                dgr = dgr + jnp.where(lane == H + hh, jnp.sum(dg[hh], axis=1, keepdims=True), 0.0)
            dg_ref[rows, :] = dgr
        for hh in range(HB):
            ds_scr[hh] = dSt[hh]

    rev = lambda n: NB - 1 - n
    part = lambda p: pl.BlockSpec((None, CPS * C, HB * HEAD), lambda hb, n: (p, rev(n), hb))
    gate = pl.BlockSpec((CPS * C, HEAD), lambda hb, n: (rev(n), 0))
    return hosted_call(
        body, comm, _grid_steps(H // HB, NB), name=name, grid=(H // HB, NB),
        in_specs=[part(0), part(1), part(2), gate,
                  pl.BlockSpec((CPS, HB, HEAD, HEAD), lambda hb, n: (rev(n), hb, 0, 0)),
                  pl.BlockSpec((CPS * C, HB * HEAD), lambda hb, n: (rev(n), hb))],
        out_specs=(pl.BlockSpec((3, CPS * C, HB * HEAD), lambda hb, n: (0, rev(n), hb)), gate),
        out_shape=(SDS((3, S, W), f32), SDS((S, HEAD), f32)),
        scratch_shapes=[pltpu.VMEM((HB, HEAD, HEAD), f32)], args=(qkv, qkv, qkv, gates, states, do))


def _gate_norm(o, z, ng):
    return _rms(o, ng) * _silu(z)


def dn_out(o, proj, ng, wout, x1, g3, name):
    S, W = o.shape
    D = x1.shape[1]
    nh = W // HEAD
    tm = _row_tile(S, 256)

    def body(o_ref, z_ref, ng_ref, w_ref, x_ref, g_ref, xo_ref, m_ref, og_ref):
        for h in range(nh):
            sl = slice(h * HEAD, (h + 1) * HEAD)
            og_ref[:, sl] = _gate_norm(o_ref[:, sl], z_ref[:, sl], ng_ref[...]).astype(og_ref.dtype)
        m = _mm(og_ref[...], w_ref[...])
        m_ref[...] = m
        xo_ref[...] = x_ref[...] + _rms(m, g_ref[...])

    rw = pl.BlockSpec((tm, W), lambda i: (i, 0))
    rd = pl.BlockSpec((tm, D), lambda i: (i, 0))
    return pl.pallas_call(
        body, name=name, grid=(S // tm,),
        in_specs=[rw, pl.BlockSpec((tm, W), lambda i: (i, 3)), pl.BlockSpec((1, HEAD), lambda i: (0, 0)),
                  pl.BlockSpec((W, D), lambda i: (0, 0)), rd, pl.BlockSpec((1, D), lambda i: (0, 0))],
        out_specs=(rd, rd, rw),
        out_shape=(SDS((S, D), f32), SDS((S, D), f32), SDS((S, W), MXU_DTYPE)), compiler_params=_params(1),
    )(o, proj, ng, wout, x1, g3)


def dn_out_bwd(dxo, m, g3, o, proj, ng, wout, name):
    S, W = o.shape
    D = m.shape[1]
    nh = W // HEAD
    tm = _row_tile(S, 256)

    def body(dxo_ref, m_ref, g_ref, o_ref, z_ref, ng_ref, w_ref, dm_ref, do_ref, dz_ref, dng_ref, dg_ref):
        @pl.when(pl.program_id(0) == 0)
        def _():
            dng_ref[...] = jnp.zeros_like(dng_ref)
            dg_ref[...] = jnp.zeros_like(dg_ref)

        dm, dg = _rms_bwd(m_ref[...], g_ref[...], dxo_ref[...])
        dg_ref[...] += dg
        dmc = dm.astype(dm_ref.dtype)
        dm_ref[...] = dmc
        dog = _mm_nt(dmc, w_ref[...])
        for h in range(nh):
            sl = slice(h * HEAD, (h + 1) * HEAD)
            _, vjp = jax.vjp(_gate_norm, o_ref[:, sl], z_ref[:, sl], ng_ref[...])
            do, dz, dng = vjp(dog[:, sl])
            do_ref[:, sl] = do
            dz_ref[:, sl] = dz.astype(dz_ref.dtype)
            dng_ref[...] += dng

    rw = pl.BlockSpec((tm, W), lambda i: (i, 0))
    rd = pl.BlockSpec((tm, D), lambda i: (i, 0))
    vd = pl.BlockSpec((1, D), lambda i: (0, 0))
    vh = pl.BlockSpec((1, HEAD), lambda i: (0, 0))
    return pl.pallas_call(
        body, name=name, grid=(S // tm,),
        in_specs=[rd, rd, vd, rw, pl.BlockSpec((tm, W), lambda i: (i, 3)), vh, pl.BlockSpec((W, D), lambda i: (0, 0))],
        out_specs=(rd, rw, rw, vh, vd),
        out_shape=(SDS((S, D), MXU_DTYPE), SDS((S, W), f32), SDS((S, W), MXU_DTYPE), SDS((1, HEAD), f32),
                   SDS((1, D), f32)),
        compiler_params=_params(1),
    )(dxo, m, g3, o, proj, ng, wout)


def _erf_arg(x):
    return lax.erf(x * 0.7071067811865476)


@jax.custom_vjp
def _gelu_with_erf(x, e):
    return 0.5 * x * (1.0 + e)


def _gelu_with_erf_bwd(res, g):
    x, e = res
    return g * (0.5 * (1.0 + e) + x * (jnp.exp(-0.5 * x * x) * 0.3989422804014327)), jnp.zeros_like(e)


_gelu_with_erf.defvjp(lambda x, e: (0.5 * x * (1.0 + e), (x, e)), _gelu_with_erf_bwd)


def _layernorm(t, lg, lb):
    tc = t - jnp.mean(t, axis=-1, keepdims=True)
    return tc * lax.rsqrt(jnp.mean(tc * tc, axis=-1, keepdims=True) + LN_EPS) * lg + lb


def _sg_stage1_kept(eu, ev, pu, pv, bu, bv, lg, lb):
    return _gelu_with_erf(pu + bu, eu), _layernorm(_gelu_with_erf(pv + bv, ev), lg, lb)


def _causal_mask(n):
    return lax.broadcasted_iota(jnp.int32, (n, n), 0) >= lax.broadcasted_iota(jnp.int32, (n, n), 1)


def sg_mid(pre, b_in, ln_g, ln_b, w_s, bsT, wout, x1, g3, name):
    S = pre.shape[0]
    E, D = ln_g.shape[1], x1.shape[1]
    G, CH = SG_GROUPS, SG_CHUNK
    Cg = E // G
    tm = _row_tile(S, 256)

    def body(pu_ref, pv_ref, bu_ref, bv_ref, lg_ref, lb_ref, ws_ref, bs_ref, w_ref, x_ref, g_ref,
             xo_ref, m_ref, gt_ref, e_ref):
        xu, xv = pu_ref[...] + bu_ref[...], pv_ref[...] + bv_ref[...]
        eu, ev = _erf_arg(xu), _erf_arg(xv)
        e_ref[:, :E] = eu.astype(e_ref.dtype)
        e_ref[:, E:] = ev.astype(e_ref.dtype)
        u = 0.5 * xu * (1.0 + eu)
        v = _layernorm(0.5 * xv * (1.0 + ev), lg_ref[...], lb_ref[...])
        mask = _causal_mask(CH)
        for g in range(G):
            wc = _c(jnp.where(mask, ws_ref[g], 0.0))
            bcol = bs_ref[:, g:g + 1]
            cs = slice(g * Cg, (g + 1) * Cg)
            for ch in range(tm // CH):
                rs = slice(ch * CH, (ch + 1) * CH)
                mixed = _mm(wc, _c(v[rs, cs])) + bcol
                gt_ref[rs, cs] = (u[rs, cs] * mixed).astype(gt_ref.dtype)
        m = _mm(gt_ref[...], w_ref[...])
        m_ref[...] = m
        xo_ref[...] = x_ref[...] + _rms(m, g_ref[...])

    half = lambda p: pl.BlockSpec((tm, E), lambda i: (i, p))
    vhalf = lambda p: pl.BlockSpec((1, E), lambda i: (0, p))
    ve = pl.BlockSpec((1, E), lambda i: (0, 0))
    rd = pl.BlockSpec((tm, D), lambda i: (i, 0))
    return pl.pallas_call(
        body, name=name, grid=(S // tm,),
        in_specs=[half(0), half(1), vhalf(0), vhalf(1), ve, ve, pl.BlockSpec((G, CH, CH), lambda i: (0, 0, 0)),
                  pl.BlockSpec((CH, G), lambda i: (0, 0)), pl.BlockSpec((E, D), lambda i: (0, 0)), rd,
                  pl.BlockSpec((1, D), lambda i: (0, 0))],
        out_specs=(rd, rd, pl.BlockSpec((tm, E), lambda i: (i, 0)), pl.BlockSpec((tm, 2 * E), lambda i: (i, 0))),
        out_shape=(SDS((S, D), f32), SDS((S, D), f32), SDS((S, E), MXU_DTYPE), SDS((S, 2 * E), MXU_DTYPE)),
        compiler_params=_params(1),
    )(pre, pre, b_in, b_in, ln_g, ln_b, w_s, bsT, wout, x1, g3)


def sg_mid_bwd(dxo, m, g3, pre, kept_erf, b_in, ln_g, ln_b, w_s, bsT, wout, name):
    S = pre.shape[0]
    E, D = ln_g.shape[1], m.shape[1]
    G, CH = SG_GROUPS, SG_CHUNK
    Cg = E // G
    tm = _row_tile(S, 256)

    def body(dxo_ref, m_ref, g_ref, pu_ref, pv_ref, eu_ref, ev_ref, bu_ref, bv_ref, lg_ref, lb_ref, ws_ref, bs_ref,
             w_ref, dm_ref, dpre_ref, dbin_ref, dlg_ref, dlb_ref, dws_ref, dbs_ref, dg_ref, du_scr, dv_scr):
        @pl.when(pl.program_id(0) == 0)
        def _():
            for r in (dbin_ref, dlg_ref, dlb_ref, dws_ref, dbs_ref, dg_ref):
                r[...] = jnp.zeros_like(r)

        dm, dg = _rms_bwd(m_ref[...], g_ref[...], dxo_ref[...])
        dg_ref[...] += dg
        dmc = dm.astype(dm_ref.dtype)
        dm_ref[...] = dmc
        dgated = _mm_nt(dmc, w_ref[...])
        stage1 = functools.partial(_sg_stage1_kept, eu_ref[...].astype(f32), ev_ref[...].astype(f32))
        (u, v), vjp1 = jax.vjp(stage1, pu_ref[...], pv_ref[...], bu_ref[...], bv_ref[...], lg_ref[...], lb_ref[...])
        mask = _causal_mask(CH)
        lane = lax.broadcasted_iota(jnp.int32, (CH, CH), 1)
        for g in range(G):
            wc = _c(jnp.where(mask, ws_ref[g], 0.0))
            bcol = bs_ref[:, g:g + 1]
            cs = slice(g * Cg, (g + 1) * Cg)
            dws = jnp.zeros((CH, CH), f32)
            dbs = jnp.zeros((CH, 1), f32)
            for ch in range(tm // CH):
                rs = slice(ch * CH, (ch + 1) * CH)
                vs = _c(v[rs, cs])
                mixed = _mm(wc, vs) + bcol
                dgt = dgated[rs, cs]
                du_scr[rs, cs] = dgt * mixed
                dmixed = dgt * u[rs, cs]
                dmc2 = _c(dmixed)
                dv_scr[rs, cs] = _mm_tn(wc, dmc2)
                dws = dws + _mm_nt(dmc2, vs)
                dbs = dbs + jnp.sum(dmixed, axis=1, keepdims=True)
            dws_ref[g] += jnp.where(mask, dws, 0.0)
            dbs_ref[...] += jnp.where(lane == g, jnp.broadcast_to(dbs, (CH, CH)), 0.0)
        dpu, dpv, dbu, dbv, dlg, dlb = vjp1((du_scr[...], dv_scr[...]))
        dpre_ref[:, :E] = dpu.astype(dpre_ref.dtype)
        dpre_ref[:, E:] = dpv.astype(dpre_ref.dtype)
        dbin_ref[:, :E] += dbu
        dbin_ref[:, E:] += dbv
        dlg_ref[...] += dlg
        dlb_ref[...] += dlb

    half = lambda p: pl.BlockSpec((tm, E), lambda i: (i, p))
    vhalf = lambda p: pl.BlockSpec((1, E), lambda i: (0, p))
    ve = pl.BlockSpec((1, E), lambda i: (0, 0))
    rd = pl.BlockSpec((tm, D), lambda i: (i, 0))
    vd = pl.BlockSpec((1, D), lambda i: (0, 0))
    wsb = pl.BlockSpec((G, CH, CH), lambda i: (0, 0, 0))
    return pl.pallas_call(
        body, name=name, grid=(S // tm,),
        in_specs=[rd, rd, vd, half(0), half(1), half(0), half(1), vhalf(0), vhalf(1), ve, ve, wsb,
                  pl.BlockSpec((CH, G), lambda i: (0, 0)), pl.BlockSpec((E, D), lambda i: (0, 0))],
        out_specs=(rd, pl.BlockSpec((tm, 2 * E), lambda i: (i, 0)), pl.BlockSpec((1, 2 * E), lambda i: (0, 0)), ve, ve,
                   wsb, pl.BlockSpec((CH, CH), lambda i: (0, 0)), vd),
        out_shape=(SDS((S, D), MXU_DTYPE), SDS((S, 2 * E), MXU_DTYPE), SDS((1, 2 * E), f32), SDS((1, E), f32),
                   SDS((1, E), f32), SDS((G, CH, CH), f32), SDS((CH, CH), f32), SDS((1, D), f32)),
        scratch_shapes=[pltpu.VMEM((tm, E), f32), pltpu.VMEM((tm, E), f32)], compiler_params=_params(1),
    )(dxo, m, g3, pre, pre, kept_erf, kept_erf, b_in, b_in, ln_g, ln_b, w_s, bsT, wout)


def loss_head(y, target, name):
    S, D = y.shape
    tm = _row_tile(S, 512)

    def body(y_ref, t_ref, l_ref, d_ref):
        @pl.when(pl.program_id(0) == 0)
        def _():
            l_ref[...] = jnp.zeros_like(l_ref)

        e = y_ref[...] - t_ref[...]
        d_ref[...] = e * (1.0 / D)
        l_ref[...] += jnp.sum(e * e) * (0.5 / D)

    row = pl.BlockSpec((tm, D), lambda i: (i, 0))
    return pl.pallas_call(
        body, name=name, grid=(S // tm,), in_specs=[row, row],
        out_specs=(pl.BlockSpec((1, HEAD), lambda i: (0, 0)), row),
        out_shape=(SDS((1, HEAD), f32), SDS((S, D), f32)), compiler_params=_params(1),
    )(y, target)


def sum_slots(r, name):
    _, R, C = r.shape
    tr = _row_tile(R, 648 if R % 648 == 0 else R)

    def body(r_ref, o_ref):
        acc = r_ref[0].astype(f32)
        for s in range(1, N_DEV):
            acc = acc + r_ref[s].astype(f32)
        o_ref[...] = acc

    return pl.pallas_call(
        body, name=name, grid=(R // tr,), in_specs=[pl.BlockSpec((N_DEV, tr, C), lambda i: (0, i, 0))],
        out_specs=pl.BlockSpec((tr, C), lambda i: (i, 0)), out_shape=SDS((R, C), f32), compiler_params=_params(1),
    )(r)


def _adam_math(w, g, m, v):
    m = ADAM_B1 * m + (1.0 - ADAM_B1) * g
    v = ADAM_B2 * v + (1.0 - ADAM_B2) * (g * g)
    m_hat = m / (1.0 - ADAM_B1 ** ADAM_STEP)
    v_hat = v / (1.0 - ADAM_B2 ** ADAM_STEP)
    delta = -ADAM_LR * (m_hat / (jnp.sqrt(v_hat) + ADAM_EPS) + ADAM_WD * w)
    return delta, m, v


def adam_slots(w, rs, m, v, name, tr):
    R, C = w.shape
    tr = _row_tile(min(r.shape[1] for r in rs), tr)
    blocks = [r.shape[1] // tr for r in rs]
    starts = [sum(blocks[:k]) for k in range(len(rs))]
    assert sum(blocks) * tr == R

    def body(w_ref, *refs):
        r_refs, (m_ref, v_ref, g_ref, d_ref, mo_ref, vo_ref) = refs[:len(rs)], refs[len(rs):]
        i = pl.program_id(0)
        for k, r_ref in enumerate(r_refs):
            @pl.when((i >= starts[k]) & (i < starts[k] + blocks[k]))
            def _():
                g = r_ref[0].astype(f32)
                for s in range(1, N_DEV):
                    g = g + r_ref[s].astype(f32)
                g_ref[...] = g

        d_ref[...], mo_ref[...], vo_ref[...] = _adam_math(w_ref[...], g_ref[...], m_ref[...], v_ref[...])

    row = pl.BlockSpec((tr, C), lambda i: (i, 0))
    piece = lambda k: pl.BlockSpec((N_DEV, tr, C), lambda i: (0, jnp.clip(i - starts[k], 0, blocks[k] - 1), 0))
    return pl.pallas_call(
        body, name=name, grid=(R // tr,), in_specs=[row] + [piece(k) for k in range(len(rs))] + [row, row],
        out_specs=(row, row, row, row), out_shape=tuple(SDS((R, C), f32) for _ in range(4)),
        compiler_params=_params(1),
    )(w, *rs, m, v)


def adam_small(w, g, m, v, name):
    def body(w_ref, g_ref, m_ref, v_ref, d_ref, mo_ref, vo_ref):
        d_ref[...], mo_ref[...], vo_ref[...] = _adam_math(w_ref[...], g_ref[...], m_ref[...], v_ref[...])

    return pl.pallas_call(body, name=name, out_shape=tuple(SDS(w.shape, f32) for _ in range(3)))(w, g, m, v)


def _pack_rows(parts):
    rows, offs, r = [], [], 0
    for p in parts:
        flat = p.reshape(-1)
        n = -(-flat.shape[0] // HEAD)
        flat = jnp.pad(flat, (0, n * HEAD - flat.shape[0]))
        rows.append(flat.reshape(n, HEAD))
        offs.append((r, n))
        r += n
    pad = (-r) % 8
    if pad:
        rows.append(jnp.zeros((pad, HEAD), f32))
    return jnp.concatenate(rows, axis=0), offs


def kernel(x, norm_g, ffn_w_gate, ffn_w_up, ffn_w_down, dn_w_in, dn_conv_w, dn_a_log, dn_dt_bias, dn_norm_g, dn_w_out, sg_w_in, sg_b_in, sg_ln_g, sg_ln_b, sg_w_s, sg_b_s, sg_w_out, loss_target, m_norm_g, m_ffn_w_gate, m_ffn_w_up, m_ffn_w_down, m_dn_w_in, m_dn_conv_w, m_dn_a_log, m_dn_dt_bias, m_dn_norm_g, m_dn_w_out, m_sg_w_in, m_sg_b_in, m_sg_ln_g, m_sg_ln_b, m_sg_w_s, m_sg_b_s, m_sg_w_out, v_norm_g, v_ffn_w_gate, v_ffn_w_up, v_ffn_w_down, v_dn_w_in, v_dn_conv_w, v_dn_a_log, v_dn_dt_bias, v_dn_norm_g, v_dn_w_out, v_sg_w_in, v_sg_b_in, v_sg_ln_g, v_sg_ln_b, v_sg_w_s, v_sg_b_s, v_sg_w_out):
    weights = dict(norm_g=norm_g, ffn_w_gate=ffn_w_gate, ffn_w_up=ffn_w_up, ffn_w_down=ffn_w_down, dn_w_in=dn_w_in,
                   dn_conv_w=dn_conv_w, dn_a_log=dn_a_log, dn_dt_bias=dn_dt_bias, dn_norm_g=dn_norm_g,
                   dn_w_out=dn_w_out, sg_w_in=sg_w_in, sg_b_in=sg_b_in, sg_ln_g=sg_ln_g, sg_ln_b=sg_ln_b,
                   sg_w_s=sg_w_s, sg_b_s=sg_b_s, sg_w_out=sg_w_out)
    mom_m = dict(norm_g=m_norm_g, ffn_w_gate=m_ffn_w_gate, ffn_w_up=m_ffn_w_up, ffn_w_down=m_ffn_w_down,
                 dn_w_in=m_dn_w_in, dn_conv_w=m_dn_conv_w, dn_a_log=m_dn_a_log, dn_dt_bias=m_dn_dt_bias,
                 dn_norm_g=m_dn_norm_g, dn_w_out=m_dn_w_out, sg_w_in=m_sg_w_in, sg_b_in=m_sg_b_in,
                 sg_ln_g=m_sg_ln_g, sg_ln_b=m_sg_ln_b, sg_w_s=m_sg_w_s, sg_b_s=m_sg_b_s, sg_w_out=m_sg_w_out)
    mom_v = dict(norm_g=v_norm_g, ffn_w_gate=v_ffn_w_gate, ffn_w_up=v_ffn_w_up, ffn_w_down=v_ffn_w_down,
                 dn_w_in=v_dn_w_in, dn_conv_w=v_dn_conv_w, dn_a_log=v_dn_a_log, dn_dt_bias=v_dn_dt_bias,
                 dn_norm_g=v_dn_norm_g, dn_w_out=v_dn_w_out, sg_w_in=v_sg_w_in, sg_b_in=v_sg_b_in,
                 sg_ln_g=v_sg_ln_g, sg_ln_b=v_sg_ln_b, sg_w_s=v_sg_w_s, sg_b_s=v_sg_b_s, sg_w_out=v_sg_w_out)
    order = list(weights)

    xs = x[0]
    S, D = xs.shape
    F8 = ffn_w_gate.shape[-1]
    depth = norm_g.shape[0]
    W = dn_w_out.shape[1] * N_DEV
    H = W // HEAD
    E = sg_ln_g.shape[1] * N_DEV
    G, CH = sg_w_s.shape[1], sg_w_s.shape[2]
    c8 = dn_w_in.shape[2]
    me = _slot(lax.axis_index("x"), lax.axis_index("y"), lax.axis_index("c"))

    assert depth == 2
    small_in, small_offs = _pack_rows([norm_g, dn_conv_w, sg_b_in, sg_ln_g, sg_ln_b])
    wg0a, wu0a, wd0a, small_all = all_gather_multi(
        [_c(ffn_w_gate[0, 0]), _c(ffn_w_up[0, 0]), _c(ffn_w_down[0, 0]), small_in], name="gather_first")
    ffn_shards = lambda l, ab: [_c(ffn_w_gate[l, ab]), _c(ffn_w_up[l, ab]), _c(ffn_w_down[l, ab])]
    gather_dn = Comm("gather", [_c(dn_w_in[0]), _c(dn_w_out[0])])
    gather_mid = Comm("gather", ffn_shards(0, 1) + ffn_shards(1, 0))
    gather_end = Comm("gather", ffn_shards(1, 1))
    gather_sg = Comm("gather", [_c(sg_w_in[0]), _c(sg_w_out[0])])
    per = N_DEV // FFN_SLABS
    wide_cols = lambda w: jnp.transpose(w.reshape(FFN_SLABS, per, D, F8), (0, 2, 1, 3)).reshape(FFN_SLABS, D, per * F8)
    wide = lambda g, u, d: (wide_cols(g), wide_cols(u), d.reshape(FFN_SLABS, per * F8, D))
    ffn_w = {(0, 0): wide(wg0a, wu0a, wd0a)}

    def small_piece(i, shard_shape):
        r0, n = small_offs[i]
        sz = math.prod(shard_shape)
        return small_all[:, r0:r0 + n, :].reshape(N_DEV, n * HEAD)[:, :sz].reshape((N_DEV,) + tuple(shard_shape))

    ng_full = jnp.moveaxis(small_piece(0, norm_g.shape), 0, 2).reshape(depth, 6, D)
    conv_full = jnp.moveaxis(small_piece(1, dn_conv_w.shape[1:]), 0, 1).reshape(CONV_K, 3 * W)
    bin_full = small_piece(2, sg_b_in.shape[1:]).reshape(1, 2 * E)
    lng_full = small_piece(3, sg_ln_g.shape[1:]).reshape(1, E)
    lnb_full = small_piece(4, sg_ln_b.shape[1:]).reshape(1, E)
    gate_lanes = lambda v: jnp.pad(v.reshape(1, H), ((0, 0), (H, HEAD - 2 * H)))
    al_row, dt_row = gate_lanes(dn_a_log), gate_lanes(dn_dt_bias)
    bsT = sg_b_s[0].T
    gvec = lambda l, k: ng_full[l, k].reshape(1, D)

    saved = []
    cur = xs
    for l in range(depth):
        sv = {}
        sv['x0'] = cur
        (cur, sv['hA'], sv['pA'], sv['qA'], sv['tA'], sv['yA']), got = ffn_fwd(
            cur, gvec(l, 0), gvec(l, 1), *ffn_w[l, 0], name=f"ffn_fwd_{l}a", comm=gather_dn if l == 0 else gather_sg)
        sv['x1'] = cur
        if l == 1:
            sg_win = jnp.moveaxis(got[0], 0, 1).reshape(D, 2 * E)
            sg_wout = got[1].reshape(E, D)
        if l == 0:
            dnin_all, dnout_all = got
            dn_win = jnp.moveaxis(dnin_all, 0, 1).reshape(D, N_DEV * c8)
            dn_wmain = dn_win[:, :4 * W]
            dn_wba = jnp.pad(dn_win[:, 4 * W:], ((0, 0), (0, HEAD - 2 * H)))
            dn_wout = dnout_all.reshape(W, D)
            sv['hM'], sv['proj'], sv['pba'] = rms_mm(cur, gvec(l, 2), dn_wmain, dn_wba, name=f"dn_in_{l}")
            sv['qkv'] = dn_prep(sv['proj'], conv_full, name=f"dn_prep_{l}")
            sv['gates'] = dn_gates(sv['pba'], al_row, dt_row, H, name=f"dn_gates_{l}")
            (sv['o'], sv['states']), got = dn_chunk_fwd(sv['qkv'], sv['gates'], name=f"dn_chunk_{l}", comm=gather_mid)
            ffn_w[0, 1], ffn_w[1, 0] = wide(*got[0:3]), wide(*got[3:6])
            cur, sv['m'], sv['og'] = dn_out(sv['o'], sv['proj'], dn_norm_g, dn_wout, cur, gvec(l, 3), name=f"dn_out_{l}")
        else:
            sv['hM'], sv['pre'] = rms_mm(cur, gvec(l, 2), sg_win, None, name=f"sg_in_{l}")
            cur, sv['m'], sv['gated'], sv['erf'] = sg_mid(sv['pre'], bin_full, lng_full, lnb_full, sg_w_s[0], bsT, sg_wout,
                                                          cur, gvec(l, 3), name=f"sg_mid_{l}")
        sv['x2'] = cur
        (cur, sv['hB'], sv['pB'], sv['qB'], sv['tB'], sv['yB']), got = ffn_fwd(
            cur, gvec(l, 4), gvec(l, 5), *ffn_w[l, 1], name=f"ffn_fwd_{l}b", comm=gather_end if l == 0 else None)
        if l == 0:
            ffn_w[1, 1] = wide(*got[0:3])
        saved.append(sv)

    loss_blk, dcur = loss_head(cur, loss_target[0], name="loss_head")
    loss = lax.psum(loss_blk[0, 0], ("x", "y", "c"))

    dng = [[None] * 6 for _ in range(depth)]
    ffn_dw = {}
    grads, slots = {}, {}

    def ffn_backward(l, ab, dcur):
        sv, s = saved[l], 'AB'[ab]
        (dcur, da, db, dy, dng[l][4 * ab], dng[l][4 * ab + 1]), _ = ffn_bwd_dx(
            dcur, sv['x2' if ab else 'x0'], sv['y' + s], sv['p' + s], sv['q' + s], gvec(l, 4 * ab), gvec(l, 4 * ab + 1),
            *ffn_w[l, ab], name=f"ffn_bwd_{l}{'ab'[ab]}")
        ffn_dw[l, ab], _ = ffn_bwd_dw(sv['h' + s], dy, sv['t' + s], da, db, name=f"ffn_dw_{l}{'ab'[ab]}")
        return dcur

    sv = saved[1]
    dcur = ffn_backward(1, 1, dcur)
    dm, dpre, grads['sg_b_in'], grads['sg_ln_g'], grads['sg_ln_b'], grads['sg_w_s'], dbs, dng[1][3] = sg_mid_bwd(
        dcur, sv['m'], gvec(1, 3), sv['pre'], sv['erf'], bin_full, lng_full, lnb_full, sg_w_s[0], bsT, sg_wout,
        name="sg_mid_bwd_1")
    grads['sg_b_s'] = dbs[:, :G].T
    dsg_wout = tn_mm(sv['gated'], dm, name="sg_wout_dw_1").reshape(N_DEV, E // N_DEV, D)
    dsg_win = tn_mm(sv['hM'], dpre, name="sg_win_dw_1", tn=2 * E // N_DEV, slot_major=True)
    (dcur, dng[1][2]), _ = mm_bwd_dx(dcur, sv['x1'], gvec(1, 2), dpre, sg_win, None, None, name="sg_in_bwd_1")
    dcur = ffn_backward(1, 0, dcur)
    sv = saved[0]
    dcur = ffn_backward(0, 1, dcur)
    dm, do, dz, grads['dn_norm_g'], dng[0][3] = dn_out_bwd(dcur, sv['m'], gvec(0, 3), sv['o'], sv['proj'], dn_norm_g,
                                                          dn_wout, name="dn_out_bwd_0")
    ddn_wout = tn_mm(sv['og'], dm, name="dn_wout_dw_0").reshape(N_DEV, W // N_DEV, D)
    (dqkv, dgates), got = dn_chunk_bwd(sv['qkv'], sv['gates'], sv['states'], do, name="dn_chunk_bwd_0",
                                       comm=Comm("exchange", [*ffn_dw[1, 0], *ffn_dw[1, 1], dsg_win, dsg_wout]))
    l1a, l1b, slots['sg_w_in'], slots['sg_w_out'] = got[0:3], got[3:6], [got[6]], [got[7]]
    dpba, dal, ddt = dn_gates_bwd(sv['pba'], al_row, dt_row, dgates, H, name="dn_gates_bwd_0")
    grads['dn_a_log'] = dal[:, H:2 * H]
    grads['dn_dt_bias'] = ddt[:, H:2 * H]
    (dproj, grads['dn_conv_w']), got = dn_prep_bwd(sv['proj'], conv_full, dqkv, dz, name="dn_prep_bwd_0",
                                                   comm=Comm("exchange", [*ffn_dw[0, 1], ddn_wout]))
    l0b, slots['dn_w_out'] = got[0:3], [got[3]]
    dw_main = tn_mm(sv['hM'], dproj, name="dn_win_dw_0")
    dw_ba = tn_mm(sv['hM'], dpba, name="dn_wba_dw_0", tn=HEAD)
    dw_in = jnp.concatenate([dw_main, dw_ba[:, :2 * H]], axis=1)
    ddn_win = jnp.moveaxis(dw_in.reshape(D, N_DEV, c8), 1, 0)
    (dcur, dng[0][2]), got = mm_bwd_dx(dcur, sv['x1'], gvec(0, 2), dproj, dn_wmain, dpba, dn_wba, name="dn_in_bwd_0",
                                       comm=Comm("exchange", [ddn_win]))
    slots['dn_w_in'] = [got[0]]
    small_names = ['norm_g', 'dn_conv_w', 'sg_b_in', 'sg_ln_g', 'sg_ln_b', 'sg_w_s', 'sg_b_s', 'dn_a_log',
                   'dn_dt_bias', 'dn_norm_g']
    small = {}

    def gather_small():
        dng_full = jnp.stack([jnp.concatenate(r, axis=0) for r in dng], axis=0)
        small['parts'] = [dng_full, grads['dn_conv_w'], grads['sg_b_in'], grads['sg_ln_g'], grads['sg_ln_b'],
                          grads['sg_w_s'], grads['sg_b_s'], grads['dn_a_log'], grads['dn_dt_bias'], grads['dn_norm_g']]
        pack, small['offs'] = _pack_rows(small['parts'])
        return Comm("gather", [pack])

    (dcur, da, db, dy, dng[0][0], dng[0][1]), _ = ffn_bwd_dx(
        dcur, sv['x0'], sv['yA'], sv['pA'], sv['qA'], gvec(0, 0), gvec(0, 1), *ffn_w[0, 0], name="ffn_bwd_0a")
    grad_x = dcur[None]
    (dg,), (small_slots,) = ffn_bwd_dw_one(sv['hA'], da, False, name="ffn_dw_0a_gate", comm=gather_small())
    (du,), (xg,) = ffn_bwd_dw_one(sv['hA'], db, False, name="ffn_dw_0a_up", comm=Comm("exchange", [dg]))
    (dd,), (xu,) = ffn_bwd_dw_one(dy, sv['tA'], True, name="ffn_dw_0a_down", comm=Comm("exchange", [du]))
    small_parts, offs = small['parts'], small['offs']
    l0a = [xg, xu, exchange_slots([dd], name="exchange_last")[0]]
    for i, nm in enumerate(['ffn_w_gate', 'ffn_w_up', 'ffn_w_down']):
        slots[nm] = [l0a[i], l0b[i], l1a[i], l1b[i]]
    big_names = ['ffn_w_gate', 'ffn_w_up', 'ffn_w_down', 'dn_w_in', 'dn_w_out', 'sg_w_in', 'sg_w_out']
    slots = [slots[nm] for nm in big_names]
    small_sum = sum_slots(small_slots, name="sum_small_grads")

    def small_grad(i):
        r0, n = offs[i]
        p = small_parts[i]
        return small_sum[r0:r0 + n].reshape(-1)[:p.size].reshape(p.shape)

    def my_shard(full, axis, like):
        n = full.shape[axis] // N_DEV
        return lax.dynamic_slice_in_dim(full, me * n, n, axis).reshape(like.shape)

    g_small = {
        'norm_g': my_shard(small_grad(0), 2, norm_g),
        'dn_conv_w': my_shard(small_grad(1), 1, dn_conv_w),
        'sg_b_in': my_shard(small_grad(2), 1, sg_b_in),
        'sg_ln_g': my_shard(small_grad(3), 1, sg_ln_g),
        'sg_ln_b': my_shard(small_grad(4), 1, sg_ln_b),
        'sg_w_s': small_grad(5).reshape(sg_w_s.shape),
        'sg_b_s': small_grad(6).reshape(sg_b_s.shape),
        'dn_a_log': small_grad(7).reshape(dn_a_log.shape),
        'dn_dt_bias': small_grad(8).reshape(dn_dt_bias.shape),
        'dn_norm_g': small_grad(9).reshape(dn_norm_g.shape),
    }

    out_g, out_d, out_m, out_v = {}, {}, {}, {}
    for nm, r in zip(big_names, slots):
        w = weights[nm]
        cols = w.shape[-1]
        rows = w.size // cols
        tr = {'ffn_w_gate': 512, 'ffn_w_up': 512, 'ffn_w_down': F8 // 2, 'dn_w_in': 256, 'sg_w_in': 256}.get(nm, rows)
        pieces = [p.reshape(N_DEV, -1, cols) for p in r]
        g, d, m2, v2 = adam_slots(w.reshape(rows, cols), pieces, mom_m[nm].reshape(rows, cols),
                                  mom_v[nm].reshape(rows, cols), name=f"adam_{nm}", tr=tr)
        out_g[nm], out_d[nm], out_m[nm], out_v[nm] = (t.reshape(w.shape) for t in (g, d, m2, v2))
    for nm in small_names:
        w = weights[nm]
        cols = w.shape[-1]
        rows = w.size // cols
        two = lambda t: t.reshape(rows, cols)
        d, m2, v2 = adam_small(two(w), two(g_small[nm]), two(mom_m[nm]), two(mom_v[nm]), name=f"adam_{nm}")
        out_g[nm] = g_small[nm]
        out_d[nm], out_m[nm], out_v[nm] = (t.reshape(w.shape) for t in (d, m2, v2))

    return (loss, grad_x, *[out_g[n] for n in order], *[out_d[n] for n in order], *[out_m[n] for n in order],
            *[out_v[n] for n in order])
```

```python
import functools
import math

import jax
import jax.numpy as jnp
from jax import lax
from jax.experimental import pallas as pl
from jax.experimental.pallas import tpu as pltpu

f32 = jnp.float32
MXU_DTYPE = jnp.bfloat16
N_DEV = 8
RMS_EPS = 1e-6
LN_EPS = 1e-5
L2_EPS = 1e-6
HEAD = 128
DN_CHUNK = 64
SG_CHUNK = 128
SG_GROUPS = 8
CONV_K = 4
ADAM_LR, ADAM_B1, ADAM_B2, ADAM_EPS, ADAM_WD, ADAM_STEP = 0.001, 0.9, 0.999, 1e-08, 0.01, 10
VMEM_LIMIT = 56 * 1024 * 1024
FFN_ROWS_FWD, FFN_ROWS_BWD, FFN_ROWS_DW = 1024, 512, 2048
PROJ_ROWS, TN_ROWS = 1024, 2048
FFN_SLABS = 4
SDS = jax.ShapeDtypeStruct
HIGHEST = lax.Precision.HIGHEST
MESH = pl.DeviceIdType.MESH


def _params(n_grid):
    return pltpu.CompilerParams(dimension_semantics=("arbitrary",) * n_grid, vmem_limit_bytes=VMEM_LIMIT)


def _row_tile(s, want):
    t = min(s, want)
    assert s % t == 0, (s, t)
    return t


def _rms(x, g):
    return x * lax.rsqrt(jnp.mean(x * x, axis=-1, keepdims=True) + RMS_EPS) * g


def _rms_bwd(x, g, dy):
    _, vjp = jax.vjp(_rms, x, g)
    return vjp(dy)


def _silu(a):
    return a * jax.nn.sigmoid(a)


def _mm(a, b):
    return lax.dot_general(a, b, (((1,), (0,)), ((), ())), preferred_element_type=f32)


def _mm_nt(a, b):
    return lax.dot_general(a, b, (((1,), (1,)), ((), ())), preferred_element_type=f32)


def _mm_tn(a, b):
    return lax.dot_general(a, b, (((0,), (0,)), ((), ())), preferred_element_type=f32)


def _c(x):
    return x.astype(MXU_DTYPE)


def _split(a):
    hi = a.astype(MXU_DTYPE)
    lo = (a - hi.astype(f32)).astype(MXU_DTYPE)
    return hi, lo


def _dot3(a, b, dims):
    ah, al = _split(a)
    bh, bl = _split(b)
    d = lambda p, q: lax.dot_general(p, q, (dims, ((), ())), preferred_element_type=f32)
    return d(ah, bh) + (d(ah, bl) + d(al, bh))


NN, NT, TN = ((1,), (0,)), ((1,), (1,)), ((0,), (0,))


def _slot(px, py, pc):
    return 4 * px + 2 * py + pc


def all_gather_multi(arrs, name):
    return Comm("gather", arrs).alone(name)


def exchange_slots(arrs, name):
    return Comm("exchange", arrs).alone(name)


class Comm:
    def __init__(self, kind, arrs):
        self.kind, self.arrs, self.n = kind, list(arrs), len(arrs)
        hbm = pl.BlockSpec(memory_space=pltpu.HBM)
        self.in_specs = [hbm] * self.n
        self.out_specs = [hbm] * self.n
        lead = (N_DEV,) if kind == "gather" else ()
        self.out_shape = [SDS(lead + tuple(a.shape), a.dtype) for a in self.arrs]
        self.scratch = [pltpu.SemaphoreType.DMA((self.n, 7)), pltpu.SemaphoreType.DMA((self.n, 7)),
                        pltpu.SemaphoreType.DMA((self.n,))]

    def phase(self, p, ins, outs, sems):
        (self._gather if self.kind == "gather" else self._exchange)(p, ins, outs, sems)

    def _gather(self, p, ins, outs, sems):
        send_sems, recv_sems, local_sems = sems
        x, y, c = lax.axis_index("x"), lax.axis_index("y"), lax.axis_index("c")
        me, sibling = (x, y, c), (x, y, 1 - c)
        chips = [(1 - x, y), (x, 1 - y), (1 - x, 1 - y)]

        def copy(a, k, block, to, src=None):
            dst = outs[a].at[_slot(*block)]
            return pltpu.make_async_remote_copy(
                src_ref=dst if src is None else src, dst_ref=dst, send_sem=send_sems.at[a, k],
                recv_sem=recv_sems.at[a, k], device_id=to, device_id_type=MESH)

        mine = [pltpu.make_async_copy(ins[a], outs[a].at[_slot(*me)], local_sems.at[a]) for a in range(self.n)]
        first = [[copy(a, 0, me, sibling, src=ins[a])] +
                 [copy(a, 1 + j, me, (*chip, c), src=ins[a]) for j, chip in enumerate(chips)] for a in range(self.n)]
        passed = [[copy(a, 4 + j, (*chip, c), sibling) for j, chip in enumerate(chips)] for a in range(self.n)]
        if p == 0:
            for a in range(self.n):
                mine[a].start()
            for a in range(self.n):
                for cp in first[a]:
                    cp.start()
        elif p == 1:
            for a in range(self.n):
                for j, chip in enumerate(chips):
                    copy(a, 1 + j, (*chip, c), me).wait_recv()
                    passed[a][j].start()
        else:
            for a in range(self.n):
                copy(a, 0, sibling, me).wait_recv()
                for j, chip in enumerate(chips):
                    copy(a, 4 + j, (*chip, 1 - c), me).wait_recv()
            for a in range(self.n):
                for cp in first[a] + passed[a]:
                    cp.wait_send()
                mine[a].wait()

    def _exchange(self, p, ins, outs, sems):
        send_sems, recv_sems, local_sems = sems
        x, y, c = lax.axis_index("x"), lax.axis_index("y"), lax.axis_index("c")
        me = _slot(x, y, c)
        peers = [(x ^ (k >> 2), y ^ ((k >> 1) & 1), c ^ (k & 1)) for k in range(1, N_DEV)]

        def copy(a, k):
            peer = peers[k - 1]
            return pltpu.make_async_remote_copy(
                src_ref=ins[a].at[_slot(*peer)], dst_ref=outs[a].at[me], send_sem=send_sems.at[a, k - 1],
                recv_sem=recv_sems.at[a, k - 1], device_id=peer, device_id_type=MESH)

        def landed(a, k):
            peer = peers[k - 1]
            return pltpu.make_async_remote_copy(
                src_ref=ins[a].at[me], dst_ref=outs[a].at[_slot(*peer)], send_sem=send_sems.at[a, k - 1],
                recv_sem=recv_sems.at[a, k - 1], device_id=peer, device_id_type=MESH)

        local = [pltpu.make_async_copy(ins[a].at[me], outs[a].at[me], local_sems.at[a]) for a in range(self.n)]
        order = [6, 7, 2, 3, 4, 5, 1]
        if p == 0:
            for a in range(self.n):
                local[a].start()
            for a in range(self.n):
                for k in order:
                    copy(a, k).start()
        elif p == 2:
            for a in range(self.n):
                for k in order:
                    copy(a, k).wait_send()
                    landed(a, k).wait_recv()
                local[a].wait()

    def alone(self, name):
        n = self.n

        def body(*refs):
            for p in range(3):
                self.phase(p, refs[:n], refs[n:2 * n], refs[2 * n:])

        return pl.pallas_call(body, name=name, out_shape=tuple(self.out_shape), in_specs=self.in_specs,
                              out_specs=tuple(self.out_specs), scratch_shapes=self.scratch)(*self.arrs)


def hosted_call(body, comm, steps, *, name, grid, in_specs, out_specs, out_shape, scratch_shapes, args):
    if comm is None:
        outs = pl.pallas_call(body, name=name, grid=grid, in_specs=in_specs, out_specs=tuple(out_specs),
                              out_shape=tuple(out_shape), scratch_shapes=scratch_shapes,
                              compiler_params=_params(len(grid)))(*args)
        return outs, None
    ni, no, ns, cn = len(in_specs), len(out_specs), len(scratch_shapes), comm.n

    def both(*refs):
        h_in, c_in = refs[:ni], refs[ni:ni + cn]
        h_out, c_out = refs[ni + cn:ni + cn + no], refs[ni + cn + no:ni + 2 * cn + no]
        h_scr, c_scr = refs[ni + 2 * cn + no:ni + 2 * cn + no + ns], refs[ni + 2 * cn + no + ns:]
        when = steps()
        pl.when(when[0])(lambda: comm.phase(0, c_in, c_out, c_scr))
        body(*h_in, *h_out, *h_scr)
        pl.when(when[1])(lambda: comm.phase(1, c_in, c_out, c_scr))
        pl.when(when[2])(lambda: comm.phase(2, c_in, c_out, c_scr))

    outs = pl.pallas_call(
        both, name=name, grid=grid, in_specs=list(in_specs) + comm.in_specs,
        out_specs=tuple(out_specs) + tuple(comm.out_specs), out_shape=tuple(out_shape) + tuple(comm.out_shape),
        scratch_shapes=list(scratch_shapes) + comm.scratch, compiler_params=_params(len(grid)),
    )(*args, *comm.arrs)
    return outs[:no], outs[no:]


def _grid_steps(n_outer, n_inner=1):
    total = n_outer * n_inner

    def steps():
        t = pl.program_id(0) * n_inner + (pl.program_id(1) if n_inner > 1 else 0)
        return t == 0, t == (total * 5) // 8, t == total - 1
    return steps


def widen_slabs(arrs, ns, name):
    per = N_DEV // ns
    _, R, C = arrs[0].shape
    n = len(arrs)

    def body(*refs):
        for a in range(n):
            for k in range(per):
                refs[n + a][:, k * C:(k + 1) * C] = refs[a][k]

    return pl.pallas_call(
        body, name=name, grid=(ns,), in_specs=[pl.BlockSpec((per, R, C), lambda s: (s, 0, 0))] * n,
        out_specs=tuple(pl.BlockSpec((None, R, per * C), lambda s: (s, 0, 0)) for _ in range(n)),
        out_shape=tuple(SDS((ns, R, per * C), a.dtype) for a in arrs), compiler_params=_params(1))(*arrs)


def ffn_fwd(x, gpre, gpost, wg, wu, wd, name, comm=None):
    S, D = x.shape
    nj, F8 = wg.shape[0], wg.shape[-1]
    tm = _row_tile(S, FFN_ROWS_FWD)

    def body(x_ref, gpre_ref, gpost_ref, wg_ref, wu_ref, wd_ref, xo_ref, h_ref, p_ref, q_ref, t_ref, y_ref):
        j = pl.program_id(1)

        @pl.when(j == 0)
        def _():
            h_ref[...] = _rms(x_ref[...], gpre_ref[...]).astype(h_ref.dtype)
            y_ref[...] = jnp.zeros_like(y_ref)

        h = h_ref[...]
        a = _mm(h, wg_ref[...])
        b = _mm(h, wu_ref[...])
        s = jax.nn.sigmoid(a)
        q = a * s
        p_ref[...] = (b * (s + q * (1.0 - s))).astype(p_ref.dtype)
        q_ref[...] = q.astype(q_ref.dtype)
        t = (q * b).astype(t_ref.dtype)
        t_ref[...] = t
        y_ref[...] += _mm(t, wd_ref[...])

        @pl.when(j == nj - 1)
        def _():
            xo_ref[...] = x_ref[...] + 0.5 * _rms(y_ref[...], gpost_ref[...])

    row = pl.BlockSpec((tm, D), lambda i, j: (i, 0))
    vec = pl.BlockSpec((1, D), lambda i, j: (0, 0))
    wcol = pl.BlockSpec((None, D, F8), lambda i, j: (j, 0, 0))
    wrow = pl.BlockSpec((None, F8, D), lambda i, j: (j, 0, 0))
    hid = pl.BlockSpec((None, tm, F8), lambda i, j: (j, i, 0))
    return hosted_call(
        body, comm, _grid_steps(S // tm, nj), name=name, grid=(S // tm, nj),
        in_specs=[row, vec, vec, wcol, wcol, wrow],
        out_specs=(row, row, hid, hid, hid, row),
        out_shape=(SDS((S, D), f32), SDS((S, D), MXU_DTYPE), SDS((nj, S, F8), MXU_DTYPE),
                   SDS((nj, S, F8), MXU_DTYPE), SDS((nj, S, F8), MXU_DTYPE), SDS((S, D), f32)),
        scratch_shapes=[], args=(x, gpre, gpost, wg, wu, wd))


def ffn_bwd_dx(dxo, x, y, p, q, gpre, gpost, wg, wu, wd, name, comm=None):
    S, D = x.shape
    NS, F8 = wg.shape[0], wg.shape[-1]
    sps = 2 if NS % 2 == 0 else 1
    nj = NS // sps
    tm = _row_tile(S, FFN_ROWS_BWD)

    def body(dxo_ref, x_ref, y_ref, p_ref, q_ref, gpre_ref, gpost_ref, wg_ref, wu_ref, wd_ref,
             dx_ref, da_ref, db_ref, dy_ref, dgpre_ref, dgpost_ref, dh_ref):
        i, j = pl.program_id(0), pl.program_id(1)

        @pl.when(j == 0)
        def _():
            @pl.when(i == 0)
            def _():
                dgpre_ref[...] = jnp.zeros_like(dgpre_ref)
                dgpost_ref[...] = jnp.zeros_like(dgpost_ref)

            dy, dg = _rms_bwd(y_ref[...], gpost_ref[...], 0.5 * dxo_ref[...])
            dy_ref[...] = dy.astype(dy_ref.dtype)
            dgpost_ref[...] += dg
            dh_ref[...] = jnp.zeros_like(dh_ref)

        half = tm // 2 if tm % 16 == 0 else tm
        for r0 in range(0, tm, half):
            rs = slice(r0, r0 + half)
            upd = None
            for s in range(sps):
                dt = _mm_nt(dy_ref[rs, :], wd_ref[s])
                da = (dt * p_ref[s, rs, :].astype(f32)).astype(da_ref.dtype)
                db = (dt * q_ref[s, rs, :].astype(f32)).astype(db_ref.dtype)
                da_ref[s, rs, :] = da
                db_ref[s, rs, :] = db
                part = _mm_nt(da, wg_ref[s]) + _mm_nt(db, wu_ref[s])
                upd = part if upd is None else upd + part
            dh_ref[rs, :] += upd

        @pl.when(j == nj - 1)
        def _():
            dxx, dg = _rms_bwd(x_ref[...], gpre_ref[...], dh_ref[...])
            dx_ref[...] = dxo_ref[...] + dxx
            dgpre_ref[...] += dg

    row = pl.BlockSpec((tm, D), lambda i, j: (i, 0))
    vec = pl.BlockSpec((1, D), lambda i, j: (0, 0))
    wcol = pl.BlockSpec((sps, D, F8), lambda i, j: (j, 0, 0))
    wrow = pl.BlockSpec((sps, F8, D), lambda i, j: (j, 0, 0))
    hid = pl.BlockSpec((sps, tm, F8), lambda i, j: (j, i, 0))
    return hosted_call(
        body, comm, _grid_steps(S // tm, nj), name=name, grid=(S // tm, nj),
        in_specs=[row, row, row, hid, hid, vec, vec, wcol, wcol, wrow],
        out_specs=(row, hid, hid, row, vec, vec),
        out_shape=(SDS((S, D), f32), SDS((NS, S, F8), MXU_DTYPE), SDS((NS, S, F8), MXU_DTYPE),
                   SDS((S, D), MXU_DTYPE), SDS((1, D), f32), SDS((1, D), f32)),
        scratch_shapes=[pltpu.VMEM((tm, D), f32)], args=(dxo, x, y, p, q, gpre, gpost, wg, wu, wd))


def ffn_bwd_dw(h, dy, t, da, db, name, comm=None):
    S, D = h.shape
    NS, F8 = t.shape[0], t.shape[-1]
    per = N_DEV // NS
    w8 = F8 // per
    tm = _row_tile(S, FFN_ROWS_DW)
    ni = S // tm

    def body(h_ref, dy_ref, t_ref, da_ref, db_ref, dwg_ref, dwu_ref, dwd_ref, accg, accu, accd):
        i = pl.program_id(1)

        @pl.when(i == 0)
        def _():
            accg[...] = jnp.zeros_like(accg)
            accu[...] = jnp.zeros_like(accu)
            accd[...] = jnp.zeros_like(accd)

        hh = h_ref[...]
        accg[...] += _mm_tn(hh, da_ref[...])
        accu[...] += _mm_tn(hh, db_ref[...])
        accd[...] += _mm_tn(t_ref[...], dy_ref[...])

        @pl.when(i == ni - 1)
        def _():
            for k in range(per):
                ks = slice(k * w8, (k + 1) * w8)
                dwg_ref[k] = accg[:, ks].astype(dwg_ref.dtype)
                dwu_ref[k] = accu[:, ks].astype(dwu_ref.dtype)
                dwd_ref[k] = accd[ks, :].astype(dwd_ref.dtype)

    row = pl.BlockSpec((tm, D), lambda j, i: (i, 0))
    hid = pl.BlockSpec((None, tm, F8), lambda j, i: (j, i, 0))
    wcol = pl.BlockSpec((per, D, w8), lambda j, i: (j, 0, 0))
    wrow = pl.BlockSpec((per, w8, D), lambda j, i: (j, 0, 0))
    return hosted_call(
        body, comm, _grid_steps(NS, ni), name=name, grid=(NS, ni),
        in_specs=[row, row, hid, hid, hid],
        out_specs=(wcol, wcol, wrow),
        out_shape=(SDS((N_DEV, D, w8), MXU_DTYPE), SDS((N_DEV, D, w8), MXU_DTYPE), SDS((N_DEV, w8, D), MXU_DTYPE)),
        scratch_shapes=[pltpu.VMEM((D, F8), f32), pltpu.VMEM((D, F8), f32), pltpu.VMEM((F8, D), f32)],
        args=(h, dy, t, da, db))


def ffn_bwd_dw_one(rows_op, slab_op, hidden_rows, name, comm=None):
    S, D = rows_op.shape
    NS, F8 = slab_op.shape[0], slab_op.shape[-1]
    per = N_DEV // NS
    w8 = F8 // per
    tm = _row_tile(S, FFN_ROWS_DW)
    ni = S // tm

    def body(r_ref, s_ref, o_ref, acc):
        i = pl.program_id(1)

        @pl.when(i == 0)
        def _():
            acc[...] = jnp.zeros_like(acc)

        acc[...] += _mm_tn(s_ref[...], r_ref[...]) if hidden_rows else _mm_tn(r_ref[...], s_ref[...])

        @pl.when(i == ni - 1)
        def _():
            for k in range(per):
                ks = slice(k * w8, (k + 1) * w8)
                o_ref[k] = (acc[ks, :] if hidden_rows else acc[:, ks]).astype(o_ref.dtype)

    blk = (per, w8, D) if hidden_rows else (per, D, w8)
    return hosted_call(
        body, comm, _grid_steps(NS, ni), name=name, grid=(NS, ni),
        in_specs=[pl.BlockSpec((tm, D), lambda j, i: (i, 0)), pl.BlockSpec((None, tm, F8), lambda j, i: (j, i, 0))],
        out_specs=(pl.BlockSpec(blk, lambda j, i: (j, 0, 0)),),
        out_shape=(SDS((N_DEV,) + blk[1:], MXU_DTYPE),),
        scratch_shapes=[pltpu.VMEM((F8, D) if hidden_rows else (D, F8), f32)], args=(rows_op, slab_op))


def rms_mm(x, g, w, w2, name, tn=1024):
    S, D = x.shape
    N = w.shape[1]
    tm = _row_tile(S, PROJ_ROWS)
    tn = _row_tile(N, tn)
    has2 = w2 is not None

    def body(*refs):
        if has2:
            x_ref, g_ref, w_ref, w2_ref, h_ref, o_ref, o2_ref = refs
        else:
            x_ref, g_ref, w_ref, h_ref, o_ref = refs
        j = pl.program_id(1)

        @pl.when(j == 0)
        def _():
            h = _rms(x_ref[...], g_ref[...]).astype(h_ref.dtype)
            h_ref[...] = h
            if has2:
                o2_ref[...] = _mm(h, w2_ref[...])

        o_ref[...] = _mm(h_ref[...], w_ref[...])

    row = pl.BlockSpec((tm, D), lambda i, j: (i, 0))
    in_specs = [row, pl.BlockSpec((1, D), lambda i, j: (0, 0)), pl.BlockSpec((D, tn), lambda i, j: (0, j))]
    out_specs = [row, pl.BlockSpec((tm, tn), lambda i, j: (i, j))]
    out_shape = [SDS((S, D), MXU_DTYPE), SDS((S, N), f32)]
    args = [x, g, w]
    if has2:
        in_specs.append(pl.BlockSpec((D, w2.shape[1]), lambda i, j: (0, 0)))
        out_specs.append(pl.BlockSpec((tm, w2.shape[1]), lambda i, j: (i, 0)))
        out_shape.append(SDS((S, w2.shape[1]), f32))
        args.append(w2)
    return pl.pallas_call(
        body, name=name, grid=(S // tm, N // tn), in_specs=in_specs, out_specs=tuple(out_specs),
        out_shape=tuple(out_shape), compiler_params=_params(2),
    )(*args)


def mm_bwd_dx(dres, x, g, dy, w, dy2, w2, name, tk=1024, comm=None):
    S, D = x.shape
    K = dy.shape[1]
    tm = _row_tile(S, PROJ_ROWS)
    tk = _row_tile(K, tk)
    nk = K // tk
    has2 = dy2 is not None

    def body(*refs):
        if has2:
            dres_ref, x_ref, g_ref, dy_ref, w_ref, dy2_ref, w2_ref, dx_ref, dg_ref, dh_ref = refs
        else:
            dres_ref, x_ref, g_ref, dy_ref, w_ref, dx_ref, dg_ref, dh_ref = refs
        i, k = pl.program_id(0), pl.program_id(1)

        @pl.when(k == 0)
        def _():
            @pl.when(i == 0)
            def _():
                dg_ref[...] = jnp.zeros_like(dg_ref)

            if has2:
                dh_ref[...] = _mm_nt(dy2_ref[...], w2_ref[...])
            else:
                dh_ref[...] = jnp.zeros_like(dh_ref)

        dh_ref[...] += _mm_nt(dy_ref[...], w_ref[...])

        @pl.when(k == nk - 1)
        def _():
            dxx, dg = _rms_bwd(x_ref[...], g_ref[...], dh_ref[...])
            dx_ref[...] = dres_ref[...] + dxx
            dg_ref[...] += dg

    row = pl.BlockSpec((tm, D), lambda i, k: (i, 0))
    vec = pl.BlockSpec((1, D), lambda i, k: (0, 0))
    in_specs = [row, row, vec, pl.BlockSpec((tm, tk), lambda i, k: (i, k)), pl.BlockSpec((D, tk), lambda i, k: (0, k))]
    args = [dres, x, g, dy, w]
    if has2:
        in_specs += [pl.BlockSpec((tm, dy2.shape[1]), lambda i, k: (i, 0)),
                     pl.BlockSpec((D, w2.shape[1]), lambda i, k: (0, 0))]
        args += [dy2, w2]
    return hosted_call(
        body, comm, _grid_steps(S // tm, nk), name=name, grid=(S // tm, nk), in_specs=in_specs, out_specs=(row, vec),
        out_shape=(SDS((S, D), f32), SDS((1, D), f32)), scratch_shapes=[pltpu.VMEM((tm, D), f32)], args=args)


def tn_mm(a, b, name, tn=512, slot_major=False):
    S, K1 = a.shape
    N = b.shape[1]
    tm = _row_tile(S, TN_ROWS)
    tn = _row_tile(N, tn)
    ni = S // tm

    def body(a_ref, b_ref, o_ref, acc):
        i = pl.program_id(1)

        @pl.when(i == 0)
        def _():
            acc[...] = jnp.zeros_like(acc)

        acc[...] += _mm_tn(a_ref[...], b_ref[...])

        @pl.when(i == ni - 1)
        def _():
            o_ref[...] = acc[...].astype(o_ref.dtype)

    if slot_major:
        out_spec, out_shape = pl.BlockSpec((None, K1, tn), lambda j, i: (j, 0, 0)), SDS((N // tn, K1, tn), MXU_DTYPE)
    else:
        out_spec, out_shape = pl.BlockSpec((K1, tn), lambda j, i: (0, j)), SDS((K1, N), MXU_DTYPE)
    return pl.pallas_call(
        body, name=name, grid=(N // tn, ni),
        in_specs=[pl.BlockSpec((tm, K1), lambda j, i: (i, 0)), pl.BlockSpec((tm, tn), lambda j, i: (i, j))],
        out_specs=out_spec, out_shape=out_shape,
        scratch_shapes=[pltpu.VMEM((K1, tn), f32)], compiler_params=_params(2),
    )(a, b)


CONV_ROWS = 512


def _shift_down(cur, prev8, s):
    r = pltpu.roll(cur, s, 0)
    row = lax.broadcasted_iota(jnp.int32, (8, cur.shape[1]), 0)
    top = jnp.where(row < s, pltpu.roll(prev8, s, 0), r[0:8])
    return jnp.concatenate([top, r[8:]], axis=0)


def _shift_up(cur, next8, s):
    n = cur.shape[0]
    r = pltpu.roll(cur, n - s, 0)
    row = lax.broadcasted_iota(jnp.int32, (8, cur.shape[1]), 0)
    bot = jnp.where(row >= 8 - s, pltpu.roll(next8, 8 - s, 0), r[n - 8:])
    return jnp.concatenate([r[:n - 8], bot], axis=0)


def _conv_taps(cur, prev8):
    return [_shift_down(cur, prev8, 3), _shift_down(cur, prev8, 2), _shift_down(cur, prev8, 1), cur]


def _act_qk(c):
    a = _silu(c)
    return a * lax.rsqrt(jnp.sum(a * a, axis=-1, keepdims=True) + L2_EPS)


def dn_prep(proj, conv_w, name):
    S = proj.shape[0]
    W = conv_w.shape[1] // 3
    nh = W // HEAD
    R = _row_tile(S, CONV_ROWS)

    def body(p_ref, w_ref, o_ref):
        j = pl.program_id(0)
        w = w_ref[...]

        def rows(r, prev8):
            cur = p_ref[pl.ds(r, R), :]
            taps = _conv_taps(cur, prev8)
            cv = taps[0] * w[0:1] + taps[1] * w[1:2] + taps[2] * w[2:3] + taps[3] * w[3:4]

            @pl.when(j < 2 * nh)
            def _():
                o_ref[pl.ds(r, R), :] = _act_qk(cv)

            @pl.when(j >= 2 * nh)
            def _():
                o_ref[pl.ds(r, R), :] = _silu(cv)

        rows(0, jnp.zeros((8, HEAD), f32))

        @pl.loop(1, S // R)
        def _(t):
            r = pl.multiple_of(t * R, R)
            rows(r, p_ref[pl.ds(r - 8, 8), :])

    return pl.pallas_call(
        body, name=name, grid=(3 * nh,),
        in_specs=[pl.BlockSpec((S, HEAD), lambda j: (0, j)), pl.BlockSpec((CONV_K, HEAD), lambda j: (0, j))],
        out_specs=pl.BlockSpec((None, S, HEAD), lambda j: (j // nh, 0, j % nh)),
        out_shape=SDS((3, S, W), f32), compiler_params=_params(1),
    )(proj, conv_w)


def dn_prep_bwd(proj, conv_w, dqkv, dz, name, comm=None):
    S = proj.shape[0]
    W = conv_w.shape[1] // 3
    nh = W // HEAD
    nq = 3 * nh
    R = _row_tile(S, CONV_ROWS)
    nr = S // R

    def body(p_ref, w_ref, dq_ref, dz_ref, dp_ref, dw_ref, dc_ref):
        j = pl.program_id(0)

        @pl.when(j >= nq)
        def _():
            dp_ref[...] = dz_ref[...].astype(dp_ref.dtype)

        @pl.when(j < nq)
        def _():
            w = w_ref[...]
            dw_ref[...] = jnp.zeros_like(dw_ref)

            def rows(r, prev8):
                cur = p_ref[pl.ds(r, R), :]
                taps = _conv_taps(cur, prev8)
                cv = taps[0] * w[0:1] + taps[1] * w[1:2] + taps[2] * w[2:3] + taps[3] * w[3:4]
                dn = dq_ref[pl.ds(r, R), :]

                @pl.when(j < 2 * nh)
                def _():
                    dc_ref[pl.ds(r, R), :] = jax.vjp(_act_qk, cv)[1](dn)[0]

                @pl.when(j >= 2 * nh)
                def _():
                    dc_ref[pl.ds(r, R), :] = jax.vjp(_silu, cv)[1](dn)[0]

                dc = dc_ref[pl.ds(r, R), :]
                dw_ref[...] += jnp.concatenate(
                    [jnp.sum(dc * taps[q], axis=0, keepdims=True) for q in range(CONV_K)], axis=0)

            rows(0, jnp.zeros((8, HEAD), f32))

            @pl.loop(1, nr)
            def _(t):
                r = pl.multiple_of(t * R, R)
                rows(r, p_ref[pl.ds(r - 8, 8), :])

            def back(r, next8):
                dc = dc_ref[pl.ds(r, R), :]
                dx = dc * w[3:4]
                for s in (1, 2, 3):
                    dx = dx + _shift_up(dc, next8, s) * w[3 - s:4 - s]
                dp_ref[pl.ds(r, R), :] = dx.astype(dp_ref.dtype)

            @pl.loop(0, nr - 1)
            def _(t):
                r = pl.multiple_of(t * R, R)
                back(r, dc_ref[pl.ds(r + R, 8), :])

            back((nr - 1) * R, jnp.zeros((8, HEAD), f32))

    clamp = lambda j: jnp.minimum(j, nq - 1)
    return hosted_call(
        body, comm, _grid_steps(4 * nh), name=name, grid=(4 * nh,),
        in_specs=[pl.BlockSpec((S, HEAD), lambda j: (0, clamp(j))),
                  pl.BlockSpec((CONV_K, HEAD), lambda j: (0, clamp(j))),
                  pl.BlockSpec((None, S, HEAD), lambda j: (clamp(j) // nh, 0, clamp(j) % nh)),
                  pl.BlockSpec((S, HEAD), lambda j: (0, jnp.maximum(j - nq, 0)))],
        out_specs=(pl.BlockSpec((S, HEAD), lambda j: (0, j)), pl.BlockSpec((CONV_K, HEAD), lambda j: (0, clamp(j)))),
        out_shape=(SDS((S, 4 * W), MXU_DTYPE), SDS((CONV_K, 3 * W), f32)),
        scratch_shapes=[pltpu.VMEM((S, HEAD), f32)], args=(proj, conv_w, dqkv, dz))


def _lane_pick(x, lane):
    sel = lax.broadcasted_iota(jnp.int32, x.shape, 1) == lane
    return jnp.broadcast_to(jnp.sum(jnp.where(sel, x, 0.0), axis=1, keepdims=True), x.shape)


CUM_ROWS = 256


def _sel_mm(m01, x):
    m = _c(m01)
    d = lambda p: lax.dot_general(m, p, (NN, ((), ())), preferred_element_type=f32)
    h1, h2, h3 = _pieces3(x)
    return (d(h1) + d(h2)) + d(h3)


def _chunk_cumsum_matrix(n, transpose):
    r, c = lax.broadcasted_iota(jnp.int32, (n, n), 0), lax.broadcasted_iota(jnp.int32, (n, n), 1)
    sh = int(math.log2(DN_CHUNK))
    same = (r >> sh) == (c >> sh)
    return jnp.where(same & ((r <= c) if transpose else (r >= c)), 1.0, 0.0).astype(f32)


def _gates_by_lane(H, p, al, dt):
    lane = lax.broadcasted_iota(jnp.int32, p.shape, 1)
    g = -jnp.exp(al) * jax.nn.softplus(p + dt)
    return jnp.where(lane < H, jax.nn.sigmoid(p), jnp.where(lane < 2 * H, g, 0.0))


def dn_gates(pba, al, dt, H, name):
    S = pba.shape[0]
    R = _row_tile(S, CUM_ROWS)

    def body(p_ref, al_ref, dt_ref, o_ref):
        raw = _gates_by_lane(H, p_ref[...], al_ref[...], dt_ref[...])
        lane = lax.broadcasted_iota(jnp.int32, raw.shape, 1)
        o_ref[...] = jnp.where(lane < H, raw, _sel_mm(_chunk_cumsum_matrix(R, False), raw))

    blk = pl.BlockSpec((R, HEAD), lambda i: (i, 0))
    par = pl.BlockSpec((1, HEAD), lambda i: (0, 0))
    return pl.pallas_call(body, name=name, grid=(S // R,), in_specs=[blk, par, par], out_specs=blk,
                          out_shape=SDS((S, HEAD), f32), compiler_params=_params(1))(pba, al, dt)


def dn_gates_bwd(pba, al, dt, dgates, H, name):
    S = pba.shape[0]
    R = _row_tile(S, CUM_ROWS)

    def body(p_ref, al_ref, dt_ref, dg_ref, dp_ref, dal_ref, ddt_ref):
        @pl.when(pl.program_id(0) == 0)
        def _():
            dal_ref[...] = jnp.zeros_like(dal_ref)
            ddt_ref[...] = jnp.zeros_like(ddt_ref)

        d = dg_ref[...]
        lane = lax.broadcasted_iota(jnp.int32, d.shape, 1)
        d = jnp.where(lane < H, d, _sel_mm(_chunk_cumsum_matrix(R, True), d))
        _, vjp = jax.vjp(functools.partial(_gates_by_lane, H), p_ref[...], al_ref[...], dt_ref[...])
        dp, dal, ddt = vjp(d)
        dp_ref[...] = dp.astype(dp_ref.dtype)
        dal_ref[...] += dal
        ddt_ref[...] += ddt

    blk = pl.BlockSpec((R, HEAD), lambda i: (i, 0))
    par = pl.BlockSpec((1, HEAD), lambda i: (0, 0))
    return pl.pallas_call(
        body, name=name, grid=(S // R,), in_specs=[blk, par, par, blk], out_specs=(blk, par, par),
        out_shape=(SDS((S, HEAD), MXU_DTYPE), SDS((1, HEAD), f32), SDS((1, HEAD), f32)), compiler_params=_params(1),
    )(pba, al, dt, dgates)


def _bdot(dims):
    back = {NN: ((NT, 'gb'), (TN, 'ag')), NT: ((NN, 'gb'), (TN, 'ga')), TN: ((NT, 'bg'), (NN, 'ag'))}[dims]
    d = lambda p, q, dm: lax.dot_general(_c(p), _c(q), (dm, ((), ())), preferred_element_type=f32)

    @jax.custom_vjp
    def f(a, b):
        return d(a, b, dims)

    def fwd(a, b):
        return d(a, b, dims), (a, b)

    def bwd(res, g):
        v = {'a': res[0], 'b': res[1], 'g': g}
        (da_dims, da_ops), (db_dims, db_ops) = back
        return d(v[da_ops[0]], v[da_ops[1]], da_dims), d(v[db_ops[0]], v[db_ops[1]], db_dims)

    f.defvjp(fwd, bwd)
    return f, lambda a, b: d(a, b, dims)


_BDOT = {dims: _bdot(dims) for dims in (NN, NT, TN)}


def _tri_inv_multi(Ls):
    n = Ls[0].shape[0]
    eye = jnp.where(lax.broadcasted_iota(jnp.int32, (n, n), 0) == lax.broadcasted_iota(jnp.int32, (n, n), 1), 1.0, 0.0)
    P = tuple(-L for L in Ls)
    T = tuple(eye + p for p in P)
    for _ in range(int(math.log2(n)) - 1):
        P = tuple(_dot3(p, p, NN) for p in P)
        T = tuple(t + _dot3(t, p, NN) for t, p in zip(T, P))
    return T


@jax.custom_vjp
def _tri_inv_multi_vjp(Ls):
    return _tri_inv_multi(Ls)


def _tri_inv_fwd(Ls):
    T = _tri_inv_multi(Ls)
    return T, T


def _tri_inv_bwd(T, dT):
    X = tuple(_dot3(d, t, NT) for d, t in zip(dT, T))
    return (tuple(-_dot3(t, x, TN) for t, x in zip(T, X)),)


_tri_inv_multi_vjp.defvjp(_tri_inv_fwd, _tri_inv_bwd)


def _pieces3(x):
    h1 = x.astype(MXU_DTYPE)
    r1 = x - h1.astype(f32)
    h2 = r1.astype(MXU_DTYPE)
    return h1, h2, (r1 - h2.astype(f32)).astype(MXU_DTYPE)


def _row_bcast_impl(sel_row, gc):
    s = _c(sel_row)
    d = lambda p: lax.dot_general(s, p, (NT, ((), ())), preferred_element_type=f32)
    h1, h2, h3 = _pieces3(gc)
    return (d(h1) + d(h2)) + d(h3)


def _row_bcast_bwd(sel_row, d):
    s = _c(sel_row)
    hi, lo = _split(d)
    t = lambda p: lax.dot_general(p, s, (TN, ((), ())), preferred_element_type=f32)
    return jnp.zeros_like(sel_row), t(hi) + t(lo)


_row_bcast = jax.custom_vjp(_row_bcast_impl)
_row_bcast.defvjp(lambda sel_row, gc: (_row_bcast_impl(sel_row, gc), sel_row), _row_bcast_bwd)


def _col_bcast_impl(gc):
    return gc[:, :DN_CHUNK]


def _col_bcast_bwd(_, d):
    return (jnp.broadcast_to(jnp.sum(d, axis=1, keepdims=True) * (1.0 / HEAD), (d.shape[0], HEAD)),)


_col_bcast = jax.custom_vjp(_col_bcast_impl)
_col_bcast.defvjp(lambda gc: (_col_bcast_impl(gc), None), _col_bcast_bwd)


def _last_row_bcast(n):
    def impl(gc):
        return jnp.broadcast_to(gc[DN_CHUNK - 1:DN_CHUNK, :], (n, HEAD))

    def bwd(_, d):
        row = lax.broadcasted_iota(jnp.int32, (DN_CHUNK, HEAD), 0)
        return (jnp.where(row == DN_CHUNK - 1, jnp.sum(d, axis=0, keepdims=True), 0.0),)

    f = jax.custom_vjp(impl)
    f.defvjp(lambda gc: (impl(gc), None), bwd)
    return impl, f


_LAST_C, _LAST_H = _last_row_bcast(DN_CHUNK), _last_row_bcast(HEAD)


def _halves(axis):
    def impl(x):
        n = x.shape[axis] // 2
        return lax.slice_in_dim(x, 0, n, axis=axis), lax.slice_in_dim(x, n, 2 * n, axis=axis)

    f = jax.custom_vjp(impl)
    f.defvjp(lambda x: (impl(x), None), lambda _, g: (jnp.concatenate(g, axis=axis),))
    return impl, f


_ROW_HALVES, _COL_HALVES = _halves(0), _halves(1)


def _chunk_consts():
    C = DN_CHUNK
    io = lambda shape, ax: lax.broadcasted_iota(jnp.int32, shape, ax)
    one = lambda m: jnp.where(m, 1.0, 0.0).astype(f32)
    r, c = io((C, C), 0), io((C, C), 1)
    return dict(causal=r >= c, strict=r > c, sel_row=one(io((C, HEAD), 1) == 0))


def _chunk_fn(kc, diff, q, k, v, gc, bB, S0):
    i = 0 if diff else 1
    mm, mm_nt, mm_tn = _BDOT[NN][i], _BDOT[NT][i], _BDOT[TN][i]
    tri = _tri_inv_multi_vjp if diff else _tri_inv_multi
    each = lambda f, *ls: tuple(f(*a) for a in zip(*ls))
    gcol = each(_col_bcast if diff else _col_bcast_impl, gc)
    grow = each(lambda g: (_row_bcast if diff else _row_bcast_impl)(kc['sel_row'], g), gc)
    glc = each(_LAST_C[i ^ 1], gc)
    glh = each(_LAST_H[i ^ 1], gc)
    decay = each(lambda a, b: jnp.where(kc['causal'], jnp.exp(jnp.where(kc['causal'], a - b, 0.0)), 0.0), gcol, grow)
    rows, cols = _ROW_HALVES[i ^ 1], _COL_HALVES[i ^ 1]
    first, second = (lambda ts: tuple(t[0] for t in ts)), (lambda ts: tuple(t[1] for t in ts))
    kb = each(lambda a, b: a * b, k, bB)
    vb = each(lambda a, b: a * b, v, bB)
    egc = each(jnp.exp, gc)
    qs = each(lambda a: a * (HEAD ** -0.5), q)
    kq = each(lambda a, b, kt: rows(mm_nt(jnp.concatenate([a, b], axis=0), kt)), kb, qs, k)
    kk, qk = first(kq), second(kq)
    T = tri(each(lambda a, d: jnp.where(kc['strict'], a * d, 0.0), kk, decay))
    uw = each(lambda t, a, b, e: cols(mm(t, jnp.concatenate([a, b * e], axis=1))), T, vb, kb, egc)
    u, w = first(uw), second(uw)
    attn = each(lambda a, d: jnp.where(kc['causal'], a * d, 0.0), qk, decay)
    wq = each(lambda a, b, e, s: rows(mm(jnp.concatenate([a, b * e], axis=0), s)), w, qs, egc, S0)
    wS, qS = first(wq), second(wq)
    v_new = each(lambda a, b: a - b, u, wS)
    o = each(lambda a, b: a + b, qS, each(mm, attn, v_new))
    kdec = each(lambda a, gl, g: a * jnp.exp(gl - g), k, glc, gc)
    S1 = each(lambda s, gl, kv: s * jnp.exp(gl) + kv, S0, glh, each(mm_tn, kdec, v_new))
    return o, S1


def _chunks_per_step(N):
    return 4 if N % 4 == 0 else (2 if N % 2 == 0 else 1)


def _heads_per_block(H):
    return 8 if H % 8 == 0 else (4 if H % 4 == 0 else 1)


def dn_chunk_fwd(qkv, gates, name, comm=None):
    _, S, W = qkv.shape
    H, C = W // HEAD, DN_CHUNK
    N, HB = S // C, _heads_per_block(H)
    assert HB == H
    CPS = _chunks_per_step(N)

    def body(q_ref, k_ref, v_ref, g_ref, o_ref, st_ref, s_scr):
        @pl.when(pl.program_id(1) == 0)
        def _():
            s_scr[...] = jnp.zeros_like(s_scr)

        kc = _chunk_consts()
        sls = [slice(hh * HEAD, (hh + 1) * HEAD) for hh in range(HB)]
        St = tuple(s_scr[hh] for hh in range(HB))
        for c in range(CPS):
            rows = slice(c * C, (c + 1) * C)
            heads = lambda ref: tuple(ref[rows, sl] for sl in sls)
            gr = g_ref[rows, :]
            for hh in range(HB):
                st_ref[c, hh] = St[hh]
            o, St = _chunk_fn(kc, False, heads(q_ref), heads(k_ref), heads(v_ref),
                              tuple(_lane_pick(gr, H + hh) for hh in range(HB)),
                              tuple(_lane_pick(gr, hh) for hh in range(HB)), St)
            for hh in range(HB):
                o_ref[rows, sls[hh]] = o[hh]
        for hh in range(HB):
            s_scr[hh] = St[hh]

    part = lambda p: pl.BlockSpec((None, CPS * C, HB * HEAD), lambda hb, n: (p, n, hb))
    return hosted_call(
        body, comm, _grid_steps(H // HB, N // CPS), name=name, grid=(H // HB, N // CPS),
        in_specs=[part(0), part(1), part(2), pl.BlockSpec((CPS * C, HEAD), lambda hb, n: (n, 0))],
        out_specs=(pl.BlockSpec((CPS * C, HB * HEAD), lambda hb, n: (n, hb)),
                   pl.BlockSpec((CPS, HB, HEAD, HEAD), lambda hb, n: (n, hb, 0, 0))),
        out_shape=(SDS((S, W), f32), SDS((N, H, HEAD, HEAD), f32)),
        scratch_shapes=[pltpu.VMEM((HB, HEAD, HEAD), f32)], args=(qkv, qkv, qkv, gates))


def dn_chunk_bwd(qkv, gates, states, do, name, comm=None):
    _, S, W = qkv.shape
    H, C = W // HEAD, DN_CHUNK
    N, HB = S // C, _heads_per_block(H)
    assert HB == H
    CPS = _chunks_per_step(N)
    NB = N // CPS

    def body(q_ref, k_ref, v_ref, g_ref, st_ref, do_ref, dqkv_ref, dg_ref, ds_scr):
        @pl.when(pl.program_id(1) == 0)
        def _():
            ds_scr[...] = jnp.zeros_like(ds_scr)

        kc = _chunk_consts()
        sls = [slice(hh * HEAD, (hh + 1) * HEAD) for hh in range(HB)]
        dSt = tuple(ds_scr[hh] for hh in range(HB))
        for c in reversed(range(CPS)):
            rows = slice(c * C, (c + 1) * C)
            heads = lambda ref: tuple(ref[rows, sl] for sl in sls)
            gr = g_ref[rows, :]
            _, vjp = jax.vjp(functools.partial(_chunk_fn, kc, True), heads(q_ref), heads(k_ref), heads(v_ref),
                             tuple(_lane_pick(gr, H + hh) for hh in range(HB)),
                             tuple(_lane_pick(gr, hh) for hh in range(HB)), tuple(st_ref[c, hh] for hh in range(HB)))
            dq, dk, dv, dg, db, dSt = vjp((heads(do_ref), dSt))
            lane = lax.broadcasted_iota(jnp.int32, (C, HEAD), 1)
            dgr = jnp.zeros((C, HEAD), f32)
            for hh in range(HB):
                dqkv_ref[0, rows, sls[hh]] = dq[hh]
                dqkv_ref[1, rows, sls[hh]] = dk[hh]
                dqkv_ref[2, rows, sls[hh]] = dv[hh]
                dgr = dgr + jnp.where(lane == hh, jnp.sum(db[hh], axis=1, keepdims=True), 0.0)
                dgr = dgr + jnp.where(lane == H + hh, jnp.sum(dg[hh], axis=1, keepdims=True), 0.0)
            dg_ref[rows, :] = dgr
        for hh in range(HB):
            ds_scr[hh] = dSt[hh]

    rev = lambda n: NB - 1 - n
    part = lambda p: pl.BlockSpec((None, CPS * C, HB * HEAD), lambda hb, n: (p, rev(n), hb))
    gate = pl.BlockSpec((CPS * C, HEAD), lambda hb, n: (rev(n), 0))
    return hosted_call(
        body, comm, _grid_steps(H // HB, NB), name=name, grid=(H // HB, NB),
        in_specs=[part(0), part(1), part(2), gate,
                  pl.BlockSpec((CPS, HB, HEAD, HEAD), lambda hb, n: (rev(n), hb, 0, 0)),
                  pl.BlockSpec((CPS * C, HB * HEAD), lambda hb, n: (rev(n), hb))],
        out_specs=(pl.BlockSpec((3, CPS * C, HB * HEAD), lambda hb, n: (0, rev(n), hb)), gate),
        out_shape=(SDS((3, S, W), f32), SDS((S, HEAD), f32)),
        scratch_shapes=[pltpu.VMEM((HB, HEAD, HEAD), f32)], args=(qkv, qkv, qkv, gates, states, do))


def _gate_norm(o, z, ng):
    return _rms(o, ng) * _silu(z)


def dn_out(o, proj, ng, wout, x1, g3, name):
    S, W = o.shape
    D = x1.shape[1]
    nh = W // HEAD
    tm = _row_tile(S, 256)

    def body(o_ref, z_ref, ng_ref, w_ref, x_ref, g_ref, xo_ref, m_ref, og_ref):
        for h in range(nh):
            sl = slice(h * HEAD, (h + 1) * HEAD)
            og_ref[:, sl] = _gate_norm(o_ref[:, sl], z_ref[:, sl], ng_ref[...]).astype(og_ref.dtype)
        m = _mm(og_ref[...], w_ref[...])
        m_ref[...] = m
        xo_ref[...] = x_ref[...] + _rms(m, g_ref[...])

    rw = pl.BlockSpec((tm, W), lambda i: (i, 0))
    rd = pl.BlockSpec((tm, D), lambda i: (i, 0))
    return pl.pallas_call(
        body, name=name, grid=(S // tm,),
        in_specs=[rw, pl.BlockSpec((tm, W), lambda i: (i, 3)), pl.BlockSpec((1, HEAD), lambda i: (0, 0)),
                  pl.BlockSpec((W, D), lambda i: (0, 0)), rd, pl.BlockSpec((1, D), lambda i: (0, 0))],
        out_specs=(rd, rd, rw),
        out_shape=(SDS((S, D), f32), SDS((S, D), f32), SDS((S, W), MXU_DTYPE)), compiler_params=_params(1),
    )(o, proj, ng, wout, x1, g3)


def dn_out_bwd(dxo, m, g3, o, proj, ng, wout, name):
    S, W = o.shape
    D = m.shape[1]
    nh = W // HEAD
    tm = _row_tile(S, 256)

    def body(dxo_ref, m_ref, g_ref, o_ref, z_ref, ng_ref, w_ref, dm_ref, do_ref, dz_ref, dng_ref, dg_ref):
        @pl.when(pl.program_id(0) == 0)
        def _():
            dng_ref[...] = jnp.zeros_like(dng_ref)
            dg_ref[...] = jnp.zeros_like(dg_ref)

        dm, dg = _rms_bwd(m_ref[...], g_ref[...], dxo_ref[...])
        dg_ref[...] += dg
        dmc = dm.astype(dm_ref.dtype)
        dm_ref[...] = dmc
        dog = _mm_nt(dmc, w_ref[...])
        for h in range(nh):
            sl = slice(h * HEAD, (h + 1) * HEAD)
            _, vjp = jax.vjp(_gate_norm, o_ref[:, sl], z_ref[:, sl], ng_ref[...])
            do, dz, dng = vjp(dog[:, sl])
            do_ref[:, sl] = do
            dz_ref[:, sl] = dz.astype(dz_ref.dtype)
            dng_ref[...] += dng

    rw = pl.BlockSpec((tm, W), lambda i: (i, 0))
    rd = pl.BlockSpec((tm, D), lambda i: (i, 0))
    vd = pl.BlockSpec((1, D), lambda i: (0, 0))
    vh = pl.BlockSpec((1, HEAD), lambda i: (0, 0))
    return pl.pallas_call(
        body, name=name, grid=(S // tm,),
        in_specs=[rd, rd, vd, rw, pl.BlockSpec((tm, W), lambda i: (i, 3)), vh, pl.BlockSpec((W, D), lambda i: (0, 0))],
        out_specs=(rd, rw, rw, vh, vd),
        out_shape=(SDS((S, D), MXU_DTYPE), SDS((S, W), f32), SDS((S, W), MXU_DTYPE), SDS((1, HEAD), f32),
                   SDS((1, D), f32)),
        compiler_params=_params(1),
    )(dxo, m, g3, o, proj, ng, wout)


def _erf_arg(x):
    return lax.erf(x * 0.7071067811865476)


@jax.custom_vjp
def _gelu_with_erf(x, e):
    return 0.5 * x * (1.0 + e)


def _gelu_with_erf_bwd(res, g):
    x, e = res
    return g * (0.5 * (1.0 + e) + x * (jnp.exp(-0.5 * x * x) * 0.3989422804014327)), jnp.zeros_like(e)


_gelu_with_erf.defvjp(lambda x, e: (0.5 * x * (1.0 + e), (x, e)), _gelu_with_erf_bwd)


def _layernorm(t, lg, lb):
    tc = t - jnp.mean(t, axis=-1, keepdims=True)
    return tc * lax.rsqrt(jnp.mean(tc * tc, axis=-1, keepdims=True) + LN_EPS) * lg + lb


def _sg_stage1_kept(eu, ev, pu, pv, bu, bv, lg, lb):
    return _gelu_with_erf(pu + bu, eu), _layernorm(_gelu_with_erf(pv + bv, ev), lg, lb)


def _causal_mask(n):
    return lax.broadcasted_iota(jnp.int32, (n, n), 0) >= lax.broadcasted_iota(jnp.int32, (n, n), 1)


def sg_mid(pre, b_in, ln_g, ln_b, w_s, bsT, wout, x1, g3, name):
    S = pre.shape[0]
    E, D = ln_g.shape[1], x1.shape[1]
    G, CH = SG_GROUPS, SG_CHUNK
    Cg = E // G
    tm = _row_tile(S, 256)

    def body(pu_ref, pv_ref, bu_ref, bv_ref, lg_ref, lb_ref, ws_ref, bs_ref, w_ref, x_ref, g_ref,
             xo_ref, m_ref, gt_ref, e_ref):
        xu, xv = pu_ref[...] + bu_ref[...], pv_ref[...] + bv_ref[...]
        eu, ev = _erf_arg(xu), _erf_arg(xv)
        e_ref[:, :E] = eu.astype(e_ref.dtype)
        e_ref[:, E:] = ev.astype(e_ref.dtype)
        u = 0.5 * xu * (1.0 + eu)
        v = _layernorm(0.5 * xv * (1.0 + ev), lg_ref[...], lb_ref[...])
        mask = _causal_mask(CH)
        for g in range(G):
            wc = _c(jnp.where(mask, ws_ref[g], 0.0))
            bcol = bs_ref[:, g:g + 1]
            cs = slice(g * Cg, (g + 1) * Cg)
            for ch in range(tm // CH):
                rs = slice(ch * CH, (ch + 1) * CH)
                mixed = _mm(wc, _c(v[rs, cs])) + bcol
                gt_ref[rs, cs] = (u[rs, cs] * mixed).astype(gt_ref.dtype)
        m = _mm(gt_ref[...], w_ref[...])
        m_ref[...] = m
        xo_ref[...] = x_ref[...] + _rms(m, g_ref[...])

    half = lambda p: pl.BlockSpec((tm, E), lambda i: (i, p))
    vhalf = lambda p: pl.BlockSpec((1, E), lambda i: (0, p))
    ve = pl.BlockSpec((1, E), lambda i: (0, 0))
    rd = pl.BlockSpec((tm, D), lambda i: (i, 0))
    return pl.pallas_call(
        body, name=name, grid=(S // tm,),
        in_specs=[half(0), half(1), vhalf(0), vhalf(1), ve, ve, pl.BlockSpec((G, CH, CH), lambda i: (0, 0, 0)),
                  pl.BlockSpec((CH, G), lambda i: (0, 0)), pl.BlockSpec((E, D), lambda i: (0, 0)), rd,
                  pl.BlockSpec((1, D), lambda i: (0, 0))],
        out_specs=(rd, rd, pl.BlockSpec((tm, E), lambda i: (i, 0)), pl.BlockSpec((tm, 2 * E), lambda i: (i, 0))),
        out_shape=(SDS((S, D), f32), SDS((S, D), f32), SDS((S, E), MXU_DTYPE), SDS((S, 2 * E), MXU_DTYPE)),
        compiler_params=_params(1),
    )(pre, pre, b_in, b_in, ln_g, ln_b, w_s, bsT, wout, x1, g3)


def sg_mid_bwd(dxo, m, g3, pre, kept_erf, b_in, ln_g, ln_b, w_s, bsT, wout, name):
    S = pre.shape[0]
    E, D = ln_g.shape[1], m.shape[1]
    G, CH = SG_GROUPS, SG_CHUNK
    Cg = E // G
    tm = _row_tile(S, 256)

    def body(dxo_ref, m_ref, g_ref, pu_ref, pv_ref, eu_ref, ev_ref, bu_ref, bv_ref, lg_ref, lb_ref, ws_ref, bs_ref,
             w_ref, dm_ref, dpre_ref, dbin_ref, dlg_ref, dlb_ref, dws_ref, dbs_ref, dg_ref, du_scr, dv_scr):
        @pl.when(pl.program_id(0) == 0)
        def _():
            for r in (dbin_ref, dlg_ref, dlb_ref, dws_ref, dbs_ref, dg_ref):
                r[...] = jnp.zeros_like(r)

        dm, dg = _rms_bwd(m_ref[...], g_ref[...], dxo_ref[...])
        dg_ref[...] += dg
        dmc = dm.astype(dm_ref.dtype)
        dm_ref[...] = dmc
        dgated = _mm_nt(dmc, w_ref[...])
        stage1 = functools.partial(_sg_stage1_kept, eu_ref[...].astype(f32), ev_ref[...].astype(f32))
        (u, v), vjp1 = jax.vjp(stage1, pu_ref[...], pv_ref[...], bu_ref[...], bv_ref[...], lg_ref[...], lb_ref[...])
        mask = _causal_mask(CH)
        lane = lax.broadcasted_iota(jnp.int32, (CH, CH), 1)
        for g in range(G):
            wc = _c(jnp.where(mask, ws_ref[g], 0.0))
            bcol = bs_ref[:, g:g + 1]
            cs = slice(g * Cg, (g + 1) * Cg)
            dws = jnp.zeros((CH, CH), f32)
            dbs = jnp.zeros((CH, 1), f32)
            for ch in range(tm // CH):
                rs = slice(ch * CH, (ch + 1) * CH)
                vs = _c(v[rs, cs])
                mixed = _mm(wc, vs) + bcol
                dgt = dgated[rs, cs]
                du_scr[rs, cs] = dgt * mixed
                dmixed = dgt * u[rs, cs]
                dmc2 = _c(dmixed)
                dv_scr[rs, cs] = _mm_tn(wc, dmc2)
                dws = dws + _mm_nt(dmc2, vs)
                dbs = dbs + jnp.sum(dmixed, axis=1, keepdims=True)
            dws_ref[g] += jnp.where(mask, dws, 0.0)
            dbs_ref[...] += jnp.where(lane == g, jnp.broadcast_to(dbs, (CH, CH)), 0.0)
        dpu, dpv, dbu, dbv, dlg, dlb = vjp1((du_scr[...], dv_scr[...]))
        dpre_ref[:, :E] = dpu.astype(dpre_ref.dtype)
        dpre_ref[:, E:] = dpv.astype(dpre_ref.dtype)
        dbin_ref[:, :E] += dbu
        dbin_ref[:, E:] += dbv
        dlg_ref[...] += dlg
        dlb_ref[...] += dlb

    half = lambda p: pl.BlockSpec((tm, E), lambda i: (i, p))
    vhalf = lambda p: pl.BlockSpec((1, E), lambda i: (0, p))
    ve = pl.BlockSpec((1, E), lambda i: (0, 0))
    rd = pl.BlockSpec((tm, D), lambda i: (i, 0))
    vd = pl.BlockSpec((1, D), lambda i: (0, 0))
    wsb = pl.BlockSpec((G, CH, CH), lambda i: (0, 0, 0))
    return pl.pallas_call(
        body, name=name, grid=(S // tm,),
        in_specs=[rd, rd, vd, half(0), half(1), half(0), half(1), vhalf(0), vhalf(1), ve, ve, wsb,
                  pl.BlockSpec((CH, G), lambda i: (0, 0)), pl.BlockSpec((E, D), lambda i: (0, 0))],
        out_specs=(rd, pl.BlockSpec((tm, 2 * E), lambda i: (i, 0)), pl.BlockSpec((1, 2 * E), lambda i: (0, 0)), ve, ve,
                   wsb, pl.BlockSpec((CH, CH), lambda i: (0, 0)), vd),
        out_shape=(SDS((S, D), MXU_DTYPE), SDS((S, 2 * E), MXU_DTYPE), SDS((1, 2 * E), f32), SDS((1, E), f32),
                   SDS((1, E), f32), SDS((G, CH, CH), f32), SDS((CH, CH), f32), SDS((1, D), f32)),
        scratch_shapes=[pltpu.VMEM((tm, E), f32), pltpu.VMEM((tm, E), f32)], compiler_params=_params(1),
    )(dxo, m, g3, pre, pre, kept_erf, kept_erf, b_in, b_in, ln_g, ln_b, w_s, bsT, wout)


def loss_head(y, target, name):
    S, D = y.shape
    tm = _row_tile(S, 512)

    def body(y_ref, t_ref, l_ref, d_ref):
        @pl.when(pl.program_id(0) == 0)
        def _():
            l_ref[...] = jnp.zeros_like(l_ref)

        e = y_ref[...] - t_ref[...]
        d_ref[...] = e * (1.0 / D)
        l_ref[...] += jnp.sum(e * e) * (0.5 / D)

    row = pl.BlockSpec((tm, D), lambda i: (i, 0))
    return pl.pallas_call(
        body, name=name, grid=(S // tm,), in_specs=[row, row],
        out_specs=(pl.BlockSpec((1, HEAD), lambda i: (0, 0)), row),
        out_shape=(SDS((1, HEAD), f32), SDS((S, D), f32)), compiler_params=_params(1),
    )(y, target)


def sum_slots(r, name):
    _, R, C = r.shape
    tr = _row_tile(R, 648 if R % 648 == 0 else R)

    def body(r_ref, o_ref):
        acc = r_ref[0].astype(f32)
        for s in range(1, N_DEV):
            acc = acc + r_ref[s].astype(f32)
        o_ref[...] = acc

    return pl.pallas_call(
        body, name=name, grid=(R // tr,), in_specs=[pl.BlockSpec((N_DEV, tr, C), lambda i: (0, i, 0))],
        out_specs=pl.BlockSpec((tr, C), lambda i: (i, 0)), out_shape=SDS((R, C), f32), compiler_params=_params(1),
    )(r)


def _adam_math(w, g, m, v):
    m = ADAM_B1 * m + (1.0 - ADAM_B1) * g
    v = ADAM_B2 * v + (1.0 - ADAM_B2) * (g * g)
    m_hat = m / (1.0 - ADAM_B1 ** ADAM_STEP)
    v_hat = v / (1.0 - ADAM_B2 ** ADAM_STEP)
    delta = -ADAM_LR * (m_hat / (jnp.sqrt(v_hat) + ADAM_EPS) + ADAM_WD * w)
    return delta, m, v


def adam_slots(w, rs, m, v, name, tr):
    R, C = w.shape
    tr = _row_tile(min(r.shape[1] for r in rs), tr)
    blocks = [r.shape[1] // tr for r in rs]
    starts = [sum(blocks[:k]) for k in range(len(rs))]
    assert sum(blocks) * tr == R

    def body(w_ref, *refs):
        r_refs, (m_ref, v_ref, g_ref, d_ref, mo_ref, vo_ref) = refs[:len(rs)], refs[len(rs):]
        i = pl.program_id(0)
        for k, r_ref in enumerate(r_refs):
            @pl.when((i >= starts[k]) & (i < starts[k] + blocks[k]))
            def _():
                g = r_ref[0].astype(f32)
                for s in range(1, N_DEV):
                    g = g + r_ref[s].astype(f32)
                g_ref[...] = g

        d_ref[...], mo_ref[...], vo_ref[...] = _adam_math(w_ref[...], g_ref[...], m_ref[...], v_ref[...])

    row = pl.BlockSpec((tr, C), lambda i: (i, 0))
    piece = lambda k: pl.BlockSpec((N_DEV, tr, C), lambda i: (0, jnp.clip(i - starts[k], 0, blocks[k] - 1), 0))
    return pl.pallas_call(
        body, name=name, grid=(R // tr,), in_specs=[row] + [piece(k) for k in range(len(rs))] + [row, row],
        out_specs=(row, row, row, row), out_shape=tuple(SDS((R, C), f32) for _ in range(4)),
        compiler_params=_params(1),
    )(w, *rs, m, v)


def adam_small(w, g, m, v, name):
    def body(w_ref, g_ref, m_ref, v_ref, d_ref, mo_ref, vo_ref):
        d_ref[...], mo_ref[...], vo_ref[...] = _adam_math(w_ref[...], g_ref[...], m_ref[...], v_ref[...])

    return pl.pallas_call(body, name=name, out_shape=tuple(SDS(w.shape, f32) for _ in range(3)))(w, g, m, v)


def _pack_rows(parts):
    rows, offs, r = [], [], 0
    for p in parts:
        flat = p.reshape(-1)
        n = -(-flat.shape[0] // HEAD)
        flat = jnp.pad(flat, (0, n * HEAD - flat.shape[0]))
        rows.append(flat.reshape(n, HEAD))
        offs.append((r, n))
        r += n
    pad = (-r) % 8
    if pad:
        rows.append(jnp.zeros((pad, HEAD), f32))
    return jnp.concatenate(rows, axis=0), offs


def kernel(x, norm_g, ffn_w_gate, ffn_w_up, ffn_w_down, dn_w_in, dn_conv_w, dn_a_log, dn_dt_bias, dn_norm_g, dn_w_out, sg_w_in, sg_b_in, sg_ln_g, sg_ln_b, sg_w_s, sg_b_s, sg_w_out, loss_target, m_norm_g, m_ffn_w_gate, m_ffn_w_up, m_ffn_w_down, m_dn_w_in, m_dn_conv_w, m_dn_a_log, m_dn_dt_bias, m_dn_norm_g, m_dn_w_out, m_sg_w_in, m_sg_b_in, m_sg_ln_g, m_sg_ln_b, m_sg_w_s, m_sg_b_s, m_sg_w_out, v_norm_g, v_ffn_w_gate, v_ffn_w_up, v_ffn_w_down, v_dn_w_in, v_dn_conv_w, v_dn_a_log, v_dn_dt_bias, v_dn_norm_g, v_dn_w_out, v_sg_w_in, v_sg_b_in, v_sg_ln_g, v_sg_ln_b, v_sg_w_s, v_sg_b_s, v_sg_w_out):
    weights = dict(norm_g=norm_g, ffn_w_gate=ffn_w_gate, ffn_w_up=ffn_w_up, ffn_w_down=ffn_w_down, dn_w_in=dn_w_in,
                   dn_conv_w=dn_conv_w, dn_a_log=dn_a_log, dn_dt_bias=dn_dt_bias, dn_norm_g=dn_norm_g,
                   dn_w_out=dn_w_out, sg_w_in=sg_w_in, sg_b_in=sg_b_in, sg_ln_g=sg_ln_g, sg_ln_b=sg_ln_b,
                   sg_w_s=sg_w_s, sg_b_s=sg_b_s, sg_w_out=sg_w_out)
    mom_m = dict(norm_g=m_norm_g, ffn_w_gate=m_ffn_w_gate, ffn_w_up=m_ffn_w_up, ffn_w_down=m_ffn_w_down,
                 dn_w_in=m_dn_w_in, dn_conv_w=m_dn_conv_w, dn_a_log=m_dn_a_log, dn_dt_bias=m_dn_dt_bias,
                 dn_norm_g=m_dn_norm_g, dn_w_out=m_dn_w_out, sg_w_in=m_sg_w_in, sg_b_in=m_sg_b_in,
                 sg_ln_g=m_sg_ln_g, sg_ln_b=m_sg_ln_b, sg_w_s=m_sg_w_s, sg_b_s=m_sg_b_s, sg_w_out=m_sg_w_out)
    mom_v = dict(norm_g=v_norm_g, ffn_w_gate=v_ffn_w_gate, ffn_w_up=v_ffn_w_up, ffn_w_down=v_ffn_w_down,
                 dn_w_in=v_dn_w_in, dn_conv_w=v_dn_conv_w, dn_a_log=v_dn_a_log, dn_dt_bias=v_dn_dt_bias,
                 dn_norm_g=v_dn_norm_g, dn_w_out=v_dn_w_out, sg_w_in=v_sg_w_in, sg_b_in=v_sg_b_in,
                 sg_ln_g=v_sg_ln_g, sg_ln_b=v_sg_ln_b, sg_w_s=v_sg_w_s, sg_b_s=v_sg_b_s, sg_w_out=v_sg_w_out)
    order = list(weights)

    xs = x[0]
    S, D = xs.shape
    F8 = ffn_w_gate.shape[-1]
    depth = norm_g.shape[0]
    W = dn_w_out.shape[1] * N_DEV
    H = W // HEAD
    E = sg_ln_g.shape[1] * N_DEV
    G, CH = sg_w_s.shape[1], sg_w_s.shape[2]
    c8 = dn_w_in.shape[2]
    me = _slot(lax.axis_index("x"), lax.axis_index("y"), lax.axis_index("c"))

    assert depth == 2
    small_in, small_offs = _pack_rows([norm_g, dn_conv_w, sg_b_in, sg_ln_g, sg_ln_b])
    wg0a, wu0a, wd0a, small_all = all_gather_multi(
        [_c(ffn_w_gate[0, 0]), _c(ffn_w_up[0, 0]), _c(ffn_w_down[0, 0]), small_in], name="gather_first")
    ffn_shards = lambda l, ab: [_c(ffn_w_gate[l, ab]), _c(ffn_w_up[l, ab]), _c(ffn_w_down[l, ab])]
    gather_dn = Comm("gather", [_c(dn_w_in[0]), _c(dn_w_out[0])])
    gather_mid = Comm("gather", ffn_shards(0, 1) + ffn_shards(1, 0))
    gather_end = Comm("gather", ffn_shards(1, 1))
    gather_sg = Comm("gather", [_c(sg_w_in[0]), _c(sg_w_out[0])])
    per = N_DEV // FFN_SLABS
    wide = lambda tag, g, u, d: (*widen_slabs([g, u], FFN_SLABS, name=f"widen_{tag}"),
                                 d.reshape(FFN_SLABS, per * F8, D))
    ffn_w = {(0, 0): wide("0a", wg0a, wu0a, wd0a)}

    def small_piece(i, shard_shape):
        r0, n = small_offs[i]
        sz = math.prod(shard_shape)
        return small_all[:, r0:r0 + n, :].reshape(N_DEV, n * HEAD)[:, :sz].reshape((N_DEV,) + tuple(shard_shape))

    ng_full = jnp.moveaxis(small_piece(0, norm_g.shape), 0, 2).reshape(depth, 6, D)
    conv_full = jnp.moveaxis(small_piece(1, dn_conv_w.shape[1:]), 0, 1).reshape(CONV_K, 3 * W)
    bin_full = small_piece(2, sg_b_in.shape[1:]).reshape(1, 2 * E)
    lng_full = small_piece(3, sg_ln_g.shape[1:]).reshape(1, E)
    lnb_full = small_piece(4, sg_ln_b.shape[1:]).reshape(1, E)
    gate_lanes = lambda v: jnp.pad(v.reshape(1, H), ((0, 0), (H, HEAD - 2 * H)))
    al_row, dt_row = gate_lanes(dn_a_log), gate_lanes(dn_dt_bias)
    bsT = sg_b_s[0].T
    gvec = lambda l, k: ng_full[l, k].reshape(1, D)

    saved = []
    cur = xs
    for l in range(depth):
        sv = {}
        sv['x0'] = cur
        (cur, sv['hA'], sv['pA'], sv['qA'], sv['tA'], sv['yA']), got = ffn_fwd(
            cur, gvec(l, 0), gvec(l, 1), *ffn_w[l, 0], name=f"ffn_fwd_{l}a", comm=gather_dn if l == 0 else gather_sg)
        sv['x1'] = cur
        if l == 1:
            sg_win = jnp.moveaxis(got[0], 0, 1).reshape(D, 2 * E)
            sg_wout = got[1].reshape(E, D)
        if l == 0:
            dnin_all, dnout_all = got
            dn_win = jnp.moveaxis(dnin_all, 0, 1).reshape(D, N_DEV * c8)
            dn_wmain = dn_win[:, :4 * W]
            dn_wba = jnp.pad(dn_win[:, 4 * W:], ((0, 0), (0, HEAD - 2 * H)))
            dn_wout = dnout_all.reshape(W, D)
            sv['hM'], sv['proj'], sv['pba'] = rms_mm(cur, gvec(l, 2), dn_wmain, dn_wba, name=f"dn_in_{l}")
            sv['qkv'] = dn_prep(sv['proj'], conv_full, name=f"dn_prep_{l}")
            sv['gates'] = dn_gates(sv['pba'], al_row, dt_row, H, name=f"dn_gates_{l}")
            (sv['o'], sv['states']), got = dn_chunk_fwd(sv['qkv'], sv['gates'], name=f"dn_chunk_{l}", comm=gather_mid)
            ffn_w[0, 1], ffn_w[1, 0] = wide("0b", *got[0:3]), wide("1a", *got[3:6])
            cur, sv['m'], sv['og'] = dn_out(sv['o'], sv['proj'], dn_norm_g, dn_wout, cur, gvec(l, 3), name=f"dn_out_{l}")
        else:
            sv['hM'], sv['pre'] = rms_mm(cur, gvec(l, 2), sg_win, None, name=f"sg_in_{l}")
            cur, sv['m'], sv['gated'], sv['erf'] = sg_mid(sv['pre'], bin_full, lng_full, lnb_full, sg_w_s[0], bsT, sg_wout,
                                                          cur, gvec(l, 3), name=f"sg_mid_{l}")
        sv['x2'] = cur
        (cur, sv['hB'], sv['pB'], sv['qB'], sv['tB'], sv['yB']), got = ffn_fwd(
            cur, gvec(l, 4), gvec(l, 5), *ffn_w[l, 1], name=f"ffn_fwd_{l}b", comm=gather_end if l == 0 else None)
        if l == 0:
            ffn_w[1, 1] = wide("1b", *got[0:3])
        saved.append(sv)

    loss_blk, dcur = loss_head(cur, loss_target[0], name="loss_head")
    loss = lax.psum(loss_blk[0, 0], ("x", "y", "c"))

    dng = [[None] * 6 for _ in range(depth)]
    ffn_dw = {}
    grads, slots = {}, {}

    def ffn_backward(l, ab, dcur):
        sv, s = saved[l], 'AB'[ab]
        (dcur, da, db, dy, dng[l][4 * ab], dng[l][4 * ab + 1]), _ = ffn_bwd_dx(
            dcur, sv['x2' if ab else 'x0'], sv['y' + s], sv['p' + s], sv['q' + s], gvec(l, 4 * ab), gvec(l, 4 * ab + 1),
            *ffn_w[l, ab], name=f"ffn_bwd_{l}{'ab'[ab]}")
        ffn_dw[l, ab], _ = ffn_bwd_dw(sv['h' + s], dy, sv['t' + s], da, db, name=f"ffn_dw_{l}{'ab'[ab]}")
        return dcur

    sv = saved[1]
    dcur = ffn_backward(1, 1, dcur)
    dm, dpre, grads['sg_b_in'], grads['sg_ln_g'], grads['sg_ln_b'], grads['sg_w_s'], dbs, dng[1][3] = sg_mid_bwd(
        dcur, sv['m'], gvec(1, 3), sv['pre'], sv['erf'], bin_full, lng_full, lnb_full, sg_w_s[0], bsT, sg_wout,
        name="sg_mid_bwd_1")
    grads['sg_b_s'] = dbs[:, :G].T
    dsg_wout = tn_mm(sv['gated'], dm, name="sg_wout_dw_1").reshape(N_DEV, E // N_DEV, D)
    dsg_win = tn_mm(sv['hM'], dpre, name="sg_win_dw_1", tn=2 * E // N_DEV, slot_major=True)
    (dcur, dng[1][2]), _ = mm_bwd_dx(dcur, sv['x1'], gvec(1, 2), dpre, sg_win, None, None, name="sg_in_bwd_1")
    dcur = ffn_backward(1, 0, dcur)
    sv = saved[0]
    dcur = ffn_backward(0, 1, dcur)
    dm, do, dz, grads['dn_norm_g'], dng[0][3] = dn_out_bwd(dcur, sv['m'], gvec(0, 3), sv['o'], sv['proj'], dn_norm_g,
                                                          dn_wout, name="dn_out_bwd_0")
    ddn_wout = tn_mm(sv['og'], dm, name="dn_wout_dw_0").reshape(N_DEV, W // N_DEV, D)
    (dqkv, dgates), got = dn_chunk_bwd(sv['qkv'], sv['gates'], sv['states'], do, name="dn_chunk_bwd_0",
                                       comm=Comm("exchange", [*ffn_dw[1, 0], *ffn_dw[1, 1], dsg_win, dsg_wout]))
    l1a, l1b, slots['sg_w_in'], slots['sg_w_out'] = got[0:3], got[3:6], [got[6]], [got[7]]
    dpba, dal, ddt = dn_gates_bwd(sv['pba'], al_row, dt_row, dgates, H, name="dn_gates_bwd_0")
    grads['dn_a_log'] = dal[:, H:2 * H]
    grads['dn_dt_bias'] = ddt[:, H:2 * H]
    (dproj, grads['dn_conv_w']), got = dn_prep_bwd(sv['proj'], conv_full, dqkv, dz, name="dn_prep_bwd_0",
                                                   comm=Comm("exchange", [*ffn_dw[0, 1], ddn_wout]))
    l0b, slots['dn_w_out'] = got[0:3], [got[3]]
    dw_main = tn_mm(sv['hM'], dproj, name="dn_win_dw_0")
    dw_ba = tn_mm(sv['hM'], dpba, name="dn_wba_dw_0", tn=HEAD)
    dw_in = jnp.concatenate([dw_main, dw_ba[:, :2 * H]], axis=1)
    ddn_win = jnp.moveaxis(dw_in.reshape(D, N_DEV, c8), 1, 0)
    (dcur, dng[0][2]), got = mm_bwd_dx(dcur, sv['x1'], gvec(0, 2), dproj, dn_wmain, dpba, dn_wba, name="dn_in_bwd_0",
                                       comm=Comm("exchange", [ddn_win]))
    slots['dn_w_in'] = [got[0]]
    small_names = ['norm_g', 'dn_conv_w', 'sg_b_in', 'sg_ln_g', 'sg_ln_b', 'sg_w_s', 'sg_b_s', 'dn_a_log',
                   'dn_dt_bias', 'dn_norm_g']
    small = {}

    def gather_small():
        dng_full = jnp.stack([jnp.concatenate(r, axis=0) for r in dng], axis=0)
        small['parts'] = [dng_full, grads['dn_conv_w'], grads['sg_b_in'], grads['sg_ln_g'], grads['sg_ln_b'],
                          grads['sg_w_s'], grads['sg_b_s'], grads['dn_a_log'], grads['dn_dt_bias'], grads['dn_norm_g']]
        pack, small['offs'] = _pack_rows(small['parts'])
        return Comm("gather", [pack])

    (dcur, da, db, dy, dng[0][0], dng[0][1]), _ = ffn_bwd_dx(
        dcur, sv['x0'], sv['yA'], sv['pA'], sv['qA'], gvec(0, 0), gvec(0, 1), *ffn_w[0, 0], name="ffn_bwd_0a")
    grad_x = dcur[None]
    (dg,), (small_slots,) = ffn_bwd_dw_one(sv['hA'], da, False, name="ffn_dw_0a_gate", comm=gather_small())
    (du,), (xg,) = ffn_bwd_dw_one(sv['hA'], db, False, name="ffn_dw_0a_up", comm=Comm("exchange", [dg]))
    (dd,), (xu,) = ffn_bwd_dw_one(dy, sv['tA'], True, name="ffn_dw_0a_down", comm=Comm("exchange", [du]))
    small_parts, offs = small['parts'], small['offs']
    l0a = [xg, xu, exchange_slots([dd], name="exchange_last")[0]]
    for i, nm in enumerate(['ffn_w_gate', 'ffn_w_up', 'ffn_w_down']):
        slots[nm] = [l0a[i], l0b[i], l1a[i], l1b[i]]
    big_names = ['ffn_w_gate', 'ffn_w_up', 'ffn_w_down', 'dn_w_in', 'dn_w_out', 'sg_w_in', 'sg_w_out']
    slots = [slots[nm] for nm in big_names]
    small_sum = sum_slots(small_slots, name="sum_small_grads")

    def small_grad(i):
        r0, n = offs[i]
        p = small_parts[i]
        return small_sum[r0:r0 + n].reshape(-1)[:p.size].reshape(p.shape)

    def my_shard(full, axis, like):
        n = full.shape[axis] // N_DEV
        return lax.dynamic_slice_in_dim(full, me * n, n, axis).reshape(like.shape)

    g_small = {
        'norm_g': my_shard(small_grad(0), 2, norm_g),
        'dn_conv_w': my_shard(small_grad(1), 1, dn_conv_w),
        'sg_b_in': my_shard(small_grad(2), 1, sg_b_in),
        'sg_ln_g': my_shard(small_grad(3), 1, sg_ln_g),
        'sg_ln_b': my_shard(small_grad(4), 1, sg_ln_b),
        'sg_w_s': small_grad(5).reshape(sg_w_s.shape),
        'sg_b_s': small_grad(6).reshape(sg_b_s.shape),
        'dn_a_log': small_grad(7).reshape(dn_a_log.shape),
        'dn_dt_bias': small_grad(8).reshape(dn_dt_bias.shape),
        'dn_norm_g': small_grad(9).reshape(dn_norm_g.shape),
    }

    out_g, out_d, out_m, out_v = {}, {}, {}, {}
    for nm, r in zip(big_names, slots):
        w = weights[nm]
        cols = w.shape[-1]
        rows = w.size // cols
        tr = {'ffn_w_gate': 512, 'ffn_w_up': 512, 'ffn_w_down': F8 // 2, 'dn_w_in': 256, 'sg_w_in': 256}.get(nm, rows)
        pieces = [p.reshape(N_DEV, -1, cols) for p in r]
        g, d, m2, v2 = adam_slots(w.reshape(rows, cols), pieces, mom_m[nm].reshape(rows, cols),
                                  mom_v[nm].reshape(rows, cols), name=f"adam_{nm}", tr=tr)
        out_g[nm], out_d[nm], out_m[nm], out_v[nm] = (t.reshape(w.shape) for t in (g, d, m2, v2))
    for nm in small_names:
        w = weights[nm]
        cols = w.shape[-1]
        rows = w.size // cols
        two = lambda t: t.reshape(rows, cols)
        d, m2, v2 = adam_small(two(w), two(g_small[nm]), two(mom_m[nm]), two(mom_v[nm]), name=f"adam_{nm}")
        out_g[nm] = g_small[nm]
        out_d[nm], out_m[nm], out_v[nm] = (t.reshape(w.shape) for t in (d, m2, v2))

    return (loss, grad_x, *[out_g[n] for n in order], *[out_d[n] for n in order], *[out_m[n] for n in order],
            *[out_v[n] for n in order])
```

```python
import functools
import math

import jax
import jax.numpy as jnp
from jax import lax
from jax.experimental import pallas as pl
from jax.experimental.pallas import tpu as pltpu

f32 = jnp.float32
MXU_DTYPE = jnp.bfloat16
N_DEV = 8
RMS_EPS = 1e-6
LN_EPS = 1e-5
L2_EPS = 1e-6
HEAD = 128
DN_CHUNK = 64
SG_CHUNK = 128
SG_GROUPS = 8
CONV_K = 4
ADAM_LR, ADAM_B1, ADAM_B2, ADAM_EPS, ADAM_WD, ADAM_STEP = 0.001, 0.9, 0.999, 1e-08, 0.01, 10
VMEM_LIMIT = 56 * 1024 * 1024
FFN_ROWS_FWD, FFN_ROWS_BWD, FFN_ROWS_DW = 1024, 512, 2048
PROJ_ROWS, TN_ROWS = 1024, 2048
FFN_SLABS = 4
SDS = jax.ShapeDtypeStruct
MESH = pl.DeviceIdType.MESH


def _params(n_grid):
    return pltpu.CompilerParams(dimension_semantics=("arbitrary",) * n_grid, vmem_limit_bytes=VMEM_LIMIT)


def _row_tile(s, want):
    t = min(s, want)
    assert s % t == 0, (s, t)
    return t


def _rms(x, g):
    return x * lax.rsqrt(jnp.mean(x * x, axis=-1, keepdims=True) + RMS_EPS) * g


def _rms_bwd(x, g, dy):
    _, vjp = jax.vjp(_rms, x, g)
    return vjp(dy)


def _silu(a):
    return a * jax.nn.sigmoid(a)


def _mm(a, b):
    return lax.dot_general(a, b, (((1,), (0,)), ((), ())), preferred_element_type=f32)


def _mm_nt(a, b):
    return lax.dot_general(a, b, (((1,), (1,)), ((), ())), preferred_element_type=f32)


def _mm_tn(a, b):
    return lax.dot_general(a, b, (((0,), (0,)), ((), ())), preferred_element_type=f32)


def _c(x):
    return x.astype(MXU_DTYPE)


def _split(a):
    hi = a.astype(MXU_DTYPE)
    lo = (a - hi.astype(f32)).astype(MXU_DTYPE)
    return hi, lo


def _dot3(a, b, dims):
    ah, al = _split(a)
    bh, bl = _split(b)
    d = lambda p, q: lax.dot_general(p, q, (dims, ((), ())), preferred_element_type=f32)
    return d(ah, bh) + (d(ah, bl) + d(al, bh))


NN, NT, TN = ((1,), (0,)), ((1,), (1,)), ((0,), (0,))


def _slot(px, py, pc):
    return 4 * px + 2 * py + pc


def all_gather_multi(arrs, name):
    return Comm("gather", arrs).alone(name)


def exchange_slots(arrs, name):
    return Comm("exchange", arrs).alone(name)


class Comm:
    def __init__(self, kind, arrs):
        self.kind, self.arrs, self.n = kind, list(arrs), len(arrs)
        hbm = pl.BlockSpec(memory_space=pltpu.HBM)
        self.in_specs = [hbm] * self.n
        self.out_specs = [hbm] * self.n
        lead = (N_DEV,) if kind == "gather" else ()
        self.out_shape = [SDS(lead + tuple(a.shape), a.dtype) for a in self.arrs]
        self.scratch = [pltpu.SemaphoreType.DMA((self.n, 7)), pltpu.SemaphoreType.DMA((self.n, 7)),
                        pltpu.SemaphoreType.DMA((self.n,))]

    def phase(self, p, ins, outs, sems):
        (self._gather if self.kind == "gather" else self._exchange)(p, ins, outs, sems)

    def _gather(self, p, ins, outs, sems):
        send_sems, recv_sems, local_sems = sems
        x, y, c = lax.axis_index("x"), lax.axis_index("y"), lax.axis_index("c")
        me, sibling = (x, y, c), (x, y, 1 - c)
        chips = [(1 - x, y), (x, 1 - y), (1 - x, 1 - y)]

        def copy(a, k, block, to, src=None):
            dst = outs[a].at[_slot(*block)]
            return pltpu.make_async_remote_copy(
                src_ref=dst if src is None else src, dst_ref=dst, send_sem=send_sems.at[a, k],
                recv_sem=recv_sems.at[a, k], device_id=to, device_id_type=MESH)

        mine = [pltpu.make_async_copy(ins[a], outs[a].at[_slot(*me)], local_sems.at[a]) for a in range(self.n)]
        first = [[copy(a, 0, me, sibling, src=ins[a])] +
                 [copy(a, 1 + j, me, (*chip, c), src=ins[a]) for j, chip in enumerate(chips)] for a in range(self.n)]
        passed = [[copy(a, 4 + j, (*chip, c), sibling) for j, chip in enumerate(chips)] for a in range(self.n)]
        if p == 0:
            for a in range(self.n):
                mine[a].start()
            for a in range(self.n):
                for cp in first[a]:
                    cp.start()
        elif p == 1:
            for a in range(self.n):
                for j, chip in enumerate(chips):
                    copy(a, 1 + j, (*chip, c), me).wait_recv()
                    passed[a][j].start()
        else:
            for a in range(self.n):
                copy(a, 0, sibling, me).wait_recv()
                for j, chip in enumerate(chips):
                    copy(a, 4 + j, (*chip, 1 - c), me).wait_recv()
            for a in range(self.n):
                for cp in first[a] + passed[a]:
                    cp.wait_send()
                mine[a].wait()

    def _exchange(self, p, ins, outs, sems):
        send_sems, recv_sems, local_sems = sems
        x, y, c = lax.axis_index("x"), lax.axis_index("y"), lax.axis_index("c")
        me = _slot(x, y, c)
        peers = [(x ^ (k >> 2), y ^ ((k >> 1) & 1), c ^ (k & 1)) for k in range(1, N_DEV)]

        def copy(a, k):
            peer = peers[k - 1]
            return pltpu.make_async_remote_copy(
                src_ref=ins[a].at[_slot(*peer)], dst_ref=outs[a].at[me], send_sem=send_sems.at[a, k - 1],
                recv_sem=recv_sems.at[a, k - 1], device_id=peer, device_id_type=MESH)

        def landed(a, k):
            peer = peers[k - 1]
            return pltpu.make_async_remote_copy(
                src_ref=ins[a].at[me], dst_ref=outs[a].at[_slot(*peer)], send_sem=send_sems.at[a, k - 1],
                recv_sem=recv_sems.at[a, k - 1], device_id=peer, device_id_type=MESH)

        local = [pltpu.make_async_copy(ins[a].at[me], outs[a].at[me], local_sems.at[a]) for a in range(self.n)]
        order = [6, 7, 2, 3, 4, 5, 1]
        if p == 0:
            for a in range(self.n):
                local[a].start()
            for a in range(self.n):
                for k in order:
                    copy(a, k).start()
        elif p == 2:
            for a in range(self.n):
                for k in order:
                    copy(a, k).wait_send()
                    landed(a, k).wait_recv()
                local[a].wait()

    def alone(self, name):
        n = self.n

        def body(*refs):
            for p in range(3):
                self.phase(p, refs[:n], refs[n:2 * n], refs[2 * n:])

        return pl.pallas_call(body, name=name, out_shape=tuple(self.out_shape), in_specs=self.in_specs,
                              out_specs=tuple(self.out_specs), scratch_shapes=self.scratch)(*self.arrs)


def hosted_call(body, comm, steps, *, name, grid, in_specs, out_specs, out_shape, scratch_shapes, args):
    if comm is None:
        outs = pl.pallas_call(body, name=name, grid=grid, in_specs=in_specs, out_specs=tuple(out_specs),
                              out_shape=tuple(out_shape), scratch_shapes=scratch_shapes,
                              compiler_params=_params(len(grid)))(*args)
        return outs, None
    ni, no, ns, cn = len(in_specs), len(out_specs), len(scratch_shapes), comm.n

    def both(*refs):
        h_in, c_in = refs[:ni], refs[ni:ni + cn]
        h_out, c_out = refs[ni + cn:ni + cn + no], refs[ni + cn + no:ni + 2 * cn + no]
        h_scr, c_scr = refs[ni + 2 * cn + no:ni + 2 * cn + no + ns], refs[ni + 2 * cn + no + ns:]
        when = steps()
        pl.when(when[0])(lambda: comm.phase(0, c_in, c_out, c_scr))
        body(*h_in, *h_out, *h_scr)
        pl.when(when[1])(lambda: comm.phase(1, c_in, c_out, c_scr))
        pl.when(when[2])(lambda: comm.phase(2, c_in, c_out, c_scr))

    outs = pl.pallas_call(
        both, name=name, grid=grid, in_specs=list(in_specs) + comm.in_specs,
        out_specs=tuple(out_specs) + tuple(comm.out_specs), out_shape=tuple(out_shape) + tuple(comm.out_shape),
        scratch_shapes=list(scratch_shapes) + comm.scratch, compiler_params=_params(len(grid)),
    )(*args, *comm.arrs)
    return outs[:no], outs[no:]


def _grid_steps(n_outer, n_inner=1):
    total = n_outer * n_inner

    def steps():
        t = pl.program_id(0) * n_inner + (pl.program_id(1) if n_inner > 1 else 0)
        return t == 0, t == (total * 5) // 8, t == total - 1
    return steps


def widen_slabs(arrs, ns, name):
    per = N_DEV // ns
    _, R, C = arrs[0].shape
    n = len(arrs)

    def body(*refs):
        for a in range(n):
            for k in range(per):
                refs[n + a][:, k * C:(k + 1) * C] = refs[a][k]

    return pl.pallas_call(
        body, name=name, grid=(ns,), in_specs=[pl.BlockSpec((per, R, C), lambda s: (s, 0, 0))] * n,
        out_specs=tuple(pl.BlockSpec((None, R, per * C), lambda s: (s, 0, 0)) for _ in range(n)),
        out_shape=tuple(SDS((ns, R, per * C), a.dtype) for a in arrs), compiler_params=_params(1))(*arrs)


def join_columns(blocks, n_main, name):
    nb, R, c8 = blocks.shape
    rest = nb * c8 - n_main
    tr = _row_tile(R, 256)

    def body(b_ref, main_ref, rest_ref, full):
        for k in range(nb):
            full[:, k * c8:(k + 1) * c8] = b_ref[k]
        main_ref[...] = full[:, :n_main]
        rest_ref[...] = jnp.zeros_like(rest_ref)
        rest_ref[:, :rest] = full[:, n_main:]

    return pl.pallas_call(
        body, name=name, grid=(R // tr,), in_specs=[pl.BlockSpec((nb, tr, c8), lambda i: (0, i, 0))],
        out_specs=(pl.BlockSpec((tr, n_main), lambda i: (i, 0)), pl.BlockSpec((tr, HEAD), lambda i: (i, 0))),
        out_shape=(SDS((R, n_main), blocks.dtype), SDS((R, HEAD), blocks.dtype)),
        scratch_shapes=[pltpu.VMEM((tr, nb * c8), blocks.dtype)], compiler_params=_params(1))(blocks)


def split_columns(main, rest, c8, name):
    R, n_main = main.shape
    nb = N_DEV
    n_rest = nb * c8 - n_main
    tr = _row_tile(R, 256)

    def body(main_ref, rest_ref, b_ref, full):
        full[:, :n_main] = main_ref[...]
        full[:, n_main:] = rest_ref[:, :n_rest]
        for k in range(nb):
            b_ref[k] = full[:, k * c8:(k + 1) * c8]

    return pl.pallas_call(
        body, name=name, grid=(R // tr,),
        in_specs=[pl.BlockSpec((tr, n_main), lambda i: (i, 0)), pl.BlockSpec((tr, HEAD), lambda i: (i, 0))],
        out_specs=pl.BlockSpec((nb, tr, c8), lambda i: (0, i, 0)), out_shape=SDS((nb, R, c8), main.dtype),
        scratch_shapes=[pltpu.VMEM((tr, nb * c8), main.dtype)], compiler_params=_params(1))(main, rest)


def ffn_fwd(x, gpre, gpost, wg, wu, wd, name, comm=None):
    S, D = x.shape
    nj, F8 = wg.shape[0], wg.shape[-1]
    tm = _row_tile(S, FFN_ROWS_FWD)

    def body(x_ref, gpre_ref, gpost_ref, wg_ref, wu_ref, wd_ref, xo_ref, h_ref, p_ref, q_ref, t_ref, y_ref):
        j = pl.program_id(1)

        @pl.when(j == 0)
        def _():
            h_ref[...] = _rms(x_ref[...], gpre_ref[...]).astype(h_ref.dtype)
            y_ref[...] = jnp.zeros_like(y_ref)

        h = h_ref[...]
        a = _mm(h, wg_ref[...])
        b = _mm(h, wu_ref[...])
        s = jax.nn.sigmoid(a)
        q = a * s
        p_ref[...] = (b * (s + q * (1.0 - s))).astype(p_ref.dtype)
        q_ref[...] = q.astype(q_ref.dtype)
        t = (q * b).astype(t_ref.dtype)
        t_ref[...] = t
        y_ref[...] += _mm(t, wd_ref[...])

        @pl.when(j == nj - 1)
        def _():
            xo_ref[...] = x_ref[...] + 0.5 * _rms(y_ref[...], gpost_ref[...])

    row = pl.BlockSpec((tm, D), lambda i, j: (i, 0))
    vec = pl.BlockSpec((1, D), lambda i, j: (0, 0))
    wcol = pl.BlockSpec((None, D, F8), lambda i, j: (j, 0, 0))
    wrow = pl.BlockSpec((None, F8, D), lambda i, j: (j, 0, 0))
    hid = pl.BlockSpec((None, tm, F8), lambda i, j: (j, i, 0))
    return hosted_call(
        body, comm, _grid_steps(S // tm, nj), name=name, grid=(S // tm, nj),
        in_specs=[row, vec, vec, wcol, wcol, wrow],
        out_specs=(row, row, hid, hid, hid, row),
        out_shape=(SDS((S, D), f32), SDS((S, D), MXU_DTYPE), SDS((nj, S, F8), MXU_DTYPE),
                   SDS((nj, S, F8), MXU_DTYPE), SDS((nj, S, F8), MXU_DTYPE), SDS((S, D), f32)),
        scratch_shapes=[], args=(x, gpre, gpost, wg, wu, wd))


def ffn_bwd_dx(dxo, x, y, p, q, gpre, gpost, wg, wu, wd, name, comm=None):
    S, D = x.shape
    NS, F8 = wg.shape[0], wg.shape[-1]
    sps = 2 if NS % 2 == 0 else 1
    nj = NS // sps
    tm = _row_tile(S, FFN_ROWS_BWD)

    def body(dxo_ref, x_ref, y_ref, p_ref, q_ref, gpre_ref, gpost_ref, wg_ref, wu_ref, wd_ref,
             dx_ref, da_ref, db_ref, dy_ref, dgpre_ref, dgpost_ref, dh_ref):
        i, j = pl.program_id(0), pl.program_id(1)

        @pl.when(j == 0)
        def _():
            @pl.when(i == 0)
            def _():
                dgpre_ref[...] = jnp.zeros_like(dgpre_ref)
                dgpost_ref[...] = jnp.zeros_like(dgpost_ref)

            dy, dg = _rms_bwd(y_ref[...], gpost_ref[...], 0.5 * dxo_ref[...])
            dy_ref[...] = dy.astype(dy_ref.dtype)
            dgpost_ref[...] += dg
            dh_ref[...] = jnp.zeros_like(dh_ref)

        half = tm // 2 if tm % 16 == 0 else tm
        for r0 in range(0, tm, half):
            rs = slice(r0, r0 + half)
            upd = None
            for s in range(sps):
                dt = _mm_nt(dy_ref[rs, :], wd_ref[s])
                da = (dt * p_ref[s, rs, :].astype(f32)).astype(da_ref.dtype)
                db = (dt * q_ref[s, rs, :].astype(f32)).astype(db_ref.dtype)
                da_ref[s, rs, :] = da
                db_ref[s, rs, :] = db
                part = _mm_nt(da, wg_ref[s]) + _mm_nt(db, wu_ref[s])
                upd = part if upd is None else upd + part
            dh_ref[rs, :] += upd

        @pl.when(j == nj - 1)
        def _():
            dxx, dg = _rms_bwd(x_ref[...], gpre_ref[...], dh_ref[...])
            dx_ref[...] = dxo_ref[...] + dxx
            dgpre_ref[...] += dg

    row = pl.BlockSpec((tm, D), lambda i, j: (i, 0))
    vec = pl.BlockSpec((1, D), lambda i, j: (0, 0))
    wcol = pl.BlockSpec((sps, D, F8), lambda i, j: (j, 0, 0))
    wrow = pl.BlockSpec((sps, F8, D), lambda i, j: (j, 0, 0))
    hid = pl.BlockSpec((sps, tm, F8), lambda i, j: (j, i, 0))
    return hosted_call(
        body, comm, _grid_steps(S // tm, nj), name=name, grid=(S // tm, nj),
        in_specs=[row, row, row, hid, hid, vec, vec, wcol, wcol, wrow],
        out_specs=(row, hid, hid, row, vec, vec),
        out_shape=(SDS((S, D), f32), SDS((NS, S, F8), MXU_DTYPE), SDS((NS, S, F8), MXU_DTYPE),
                   SDS((S, D), MXU_DTYPE), SDS((1, D), f32), SDS((1, D), f32)),
        scratch_shapes=[pltpu.VMEM((tm, D), f32)], args=(dxo, x, y, p, q, gpre, gpost, wg, wu, wd))


def ffn_bwd_dw(h, dy, t, da, db, name, comm=None):
    S, D = h.shape
    NS, F8 = t.shape[0], t.shape[-1]
    per = N_DEV // NS
    w8 = F8 // per
    tm = _row_tile(S, FFN_ROWS_DW)
    ni = S // tm

    def body(h_ref, dy_ref, t_ref, da_ref, db_ref, dwg_ref, dwu_ref, dwd_ref, accg, accu, accd):
        i = pl.program_id(1)

        @pl.when(i == 0)
        def _():
            accg[...] = jnp.zeros_like(accg)
            accu[...] = jnp.zeros_like(accu)
            accd[...] = jnp.zeros_like(accd)

        hh = h_ref[...]
        accg[...] += _mm_tn(hh, da_ref[...])
        accu[...] += _mm_tn(hh, db_ref[...])
        accd[...] += _mm_tn(t_ref[...], dy_ref[...])

        @pl.when(i == ni - 1)
        def _():
            for k in range(per):
                ks = slice(k * w8, (k + 1) * w8)
                dwg_ref[k] = accg[:, ks].astype(dwg_ref.dtype)
                dwu_ref[k] = accu[:, ks].astype(dwu_ref.dtype)
                dwd_ref[k] = accd[ks, :].astype(dwd_ref.dtype)

    row = pl.BlockSpec((tm, D), lambda j, i: (i, 0))
    hid = pl.BlockSpec((None, tm, F8), lambda j, i: (j, i, 0))
    wcol = pl.BlockSpec((per, D, w8), lambda j, i: (j, 0, 0))
    wrow = pl.BlockSpec((per, w8, D), lambda j, i: (j, 0, 0))
    return hosted_call(
        body, comm, _grid_steps(NS, ni), name=name, grid=(NS, ni),
        in_specs=[row, row, hid, hid, hid],
        out_specs=(wcol, wcol, wrow),
        out_shape=(SDS((N_DEV, D, w8), MXU_DTYPE), SDS((N_DEV, D, w8), MXU_DTYPE), SDS((N_DEV, w8, D), MXU_DTYPE)),
        scratch_shapes=[pltpu.VMEM((D, F8), f32), pltpu.VMEM((D, F8), f32), pltpu.VMEM((F8, D), f32)],
        args=(h, dy, t, da, db))


def ffn_bwd_dw_one(rows_op, slab_op, hidden_rows, name, comm=None):
    S, D = rows_op.shape
    NS, F8 = slab_op.shape[0], slab_op.shape[-1]
    per = N_DEV // NS
    w8 = F8 // per
    tm = _row_tile(S, FFN_ROWS_DW)
    ni = S // tm

    def body(r_ref, s_ref, o_ref, acc):
        i = pl.program_id(1)

        @pl.when(i == 0)
        def _():
            acc[...] = jnp.zeros_like(acc)

        acc[...] += _mm_tn(s_ref[...], r_ref[...]) if hidden_rows else _mm_tn(r_ref[...], s_ref[...])

        @pl.when(i == ni - 1)
        def _():
            for k in range(per):
                ks = slice(k * w8, (k + 1) * w8)
                o_ref[k] = (acc[ks, :] if hidden_rows else acc[:, ks]).astype(o_ref.dtype)

    blk = (per, w8, D) if hidden_rows else (per, D, w8)
    return hosted_call(
        body, comm, _grid_steps(NS, ni), name=name, grid=(NS, ni),
        in_specs=[pl.BlockSpec((tm, D), lambda j, i: (i, 0)), pl.BlockSpec((None, tm, F8), lambda j, i: (j, i, 0))],
        out_specs=(pl.BlockSpec(blk, lambda j, i: (j, 0, 0)),),
        out_shape=(SDS((N_DEV,) + blk[1:], MXU_DTYPE),),
        scratch_shapes=[pltpu.VMEM((F8, D) if hidden_rows else (D, F8), f32)], args=(rows_op, slab_op))


def rms_mm(x, g, w, w2, name, tn=1024):
    S, D = x.shape
    N = w.shape[1]
    tm = _row_tile(S, PROJ_ROWS)
    tn = _row_tile(N, tn)
    has2 = w2 is not None

    def body(*refs):
        if has2:
            x_ref, g_ref, w_ref, w2_ref, h_ref, o_ref, o2_ref = refs
        else:
            x_ref, g_ref, w_ref, h_ref, o_ref = refs
        j = pl.program_id(1)

        @pl.when(j == 0)
        def _():
            h = _rms(x_ref[...], g_ref[...]).astype(h_ref.dtype)
            h_ref[...] = h
            if has2:
                o2_ref[...] = _mm(h, w2_ref[...])

        o_ref[...] = _mm(h_ref[...], w_ref[...])

    row = pl.BlockSpec((tm, D), lambda i, j: (i, 0))
    in_specs = [row, pl.BlockSpec((1, D), lambda i, j: (0, 0)), pl.BlockSpec((D, tn), lambda i, j: (0, j))]
    out_specs = [row, pl.BlockSpec((tm, tn), lambda i, j: (i, j))]
    out_shape = [SDS((S, D), MXU_DTYPE), SDS((S, N), f32)]
    args = [x, g, w]
    if has2:
        in_specs.append(pl.BlockSpec((D, w2.shape[1]), lambda i, j: (0, 0)))
        out_specs.append(pl.BlockSpec((tm, w2.shape[1]), lambda i, j: (i, 0)))
        out_shape.append(SDS((S, w2.shape[1]), f32))
        args.append(w2)
    return pl.pallas_call(
        body, name=name, grid=(S // tm, N // tn), in_specs=in_specs, out_specs=tuple(out_specs),
        out_shape=tuple(out_shape), compiler_params=_params(2),
    )(*args)


def mm_bwd_dx(dres, x, g, dy, w, dy2, w2, name, tk=1024, comm=None):
    S, D = x.shape
    K = dy.shape[1]
    tm = _row_tile(S, PROJ_ROWS)
    tk = _row_tile(K, tk)
    nk = K // tk
    has2 = dy2 is not None

    def body(*refs):
        if has2:
            dres_ref, x_ref, g_ref, dy_ref, w_ref, dy2_ref, w2_ref, dx_ref, dg_ref, dh_ref = refs
        else:
            dres_ref, x_ref, g_ref, dy_ref, w_ref, dx_ref, dg_ref, dh_ref = refs
        i, k = pl.program_id(0), pl.program_id(1)

        @pl.when(k == 0)
        def _():
            @pl.when(i == 0)
            def _():
                dg_ref[...] = jnp.zeros_like(dg_ref)

            if has2:
                dh_ref[...] = _mm_nt(dy2_ref[...], w2_ref[...])
            else:
                dh_ref[...] = jnp.zeros_like(dh_ref)

        dh_ref[...] += _mm_nt(dy_ref[...], w_ref[...])

        @pl.when(k == nk - 1)
        def _():
            dxx, dg = _rms_bwd(x_ref[...], g_ref[...], dh_ref[...])
            dx_ref[...] = dres_ref[...] + dxx
            dg_ref[...] += dg

    row = pl.BlockSpec((tm, D), lambda i, k: (i, 0))
    vec = pl.BlockSpec((1, D), lambda i, k: (0, 0))
    in_specs = [row, row, vec, pl.BlockSpec((tm, tk), lambda i, k: (i, k)), pl.BlockSpec((D, tk), lambda i, k: (0, k))]
    args = [dres, x, g, dy, w]
    if has2:
        in_specs += [pl.BlockSpec((tm, dy2.shape[1]), lambda i, k: (i, 0)),
                     pl.BlockSpec((D, w2.shape[1]), lambda i, k: (0, 0))]
        args += [dy2, w2]
    return hosted_call(
        body, comm, _grid_steps(S // tm, nk), name=name, grid=(S // tm, nk), in_specs=in_specs, out_specs=(row, vec),
        out_shape=(SDS((S, D), f32), SDS((1, D), f32)), scratch_shapes=[pltpu.VMEM((tm, D), f32)], args=args)


def tn_mm(a, b, name, tn=512, slot_major=False):
    S, K1 = a.shape
    N = b.shape[1]
    tm = _row_tile(S, TN_ROWS)
    tn = _row_tile(N, tn)
    ni = S // tm

    def body(a_ref, b_ref, o_ref, acc):
        i = pl.program_id(1)

        @pl.when(i == 0)
        def _():
            acc[...] = jnp.zeros_like(acc)

        acc[...] += _mm_tn(a_ref[...], b_ref[...])

        @pl.when(i == ni - 1)
        def _():
            o_ref[...] = acc[...].astype(o_ref.dtype)

    if slot_major:
        out_spec, out_shape = pl.BlockSpec((None, K1, tn), lambda j, i: (j, 0, 0)), SDS((N // tn, K1, tn), MXU_DTYPE)
    else:
        out_spec, out_shape = pl.BlockSpec((K1, tn), lambda j, i: (0, j)), SDS((K1, N), MXU_DTYPE)
    return pl.pallas_call(
        body, name=name, grid=(N // tn, ni),
        in_specs=[pl.BlockSpec((tm, K1), lambda j, i: (i, 0)), pl.BlockSpec((tm, tn), lambda j, i: (i, j))],
        out_specs=out_spec, out_shape=out_shape,
        scratch_shapes=[pltpu.VMEM((K1, tn), f32)], compiler_params=_params(2),
    )(a, b)


CONV_ROWS = 512


def _shift_down(cur, prev8, s):
    r = pltpu.roll(cur, s, 0)
    row = lax.broadcasted_iota(jnp.int32, (8, cur.shape[1]), 0)
    top = jnp.where(row < s, pltpu.roll(prev8, s, 0), r[0:8])
    return jnp.concatenate([top, r[8:]], axis=0)


def _shift_up(cur, next8, s):
    n = cur.shape[0]
    r = pltpu.roll(cur, n - s, 0)
    row = lax.broadcasted_iota(jnp.int32, (8, cur.shape[1]), 0)
    bot = jnp.where(row >= 8 - s, pltpu.roll(next8, 8 - s, 0), r[n - 8:])
    return jnp.concatenate([r[:n - 8], bot], axis=0)


def _conv_taps(cur, prev8):
    return [_shift_down(cur, prev8, 3), _shift_down(cur, prev8, 2), _shift_down(cur, prev8, 1), cur]


def _act_qk(c):
    a = _silu(c)
    return a * lax.rsqrt(jnp.sum(a * a, axis=-1, keepdims=True) + L2_EPS)


def dn_prep(proj, conv_w, name):
    S = proj.shape[0]
    W = conv_w.shape[1] // 3
    nh = W // HEAD
    R = _row_tile(S, CONV_ROWS)

    def body(p_ref, w_ref, o_ref):
        j = pl.program_id(0)
        w = w_ref[...]

        def rows(r, prev8):
            cur = p_ref[pl.ds(r, R), :]
            taps = _conv_taps(cur, prev8)
            cv = taps[0] * w[0:1] + taps[1] * w[1:2] + taps[2] * w[2:3] + taps[3] * w[3:4]

            @pl.when(j < 2 * nh)
            def _():
                o_ref[pl.ds(r, R), :] = _act_qk(cv)

            @pl.when(j >= 2 * nh)
            def _():
                o_ref[pl.ds(r, R), :] = _silu(cv)

        rows(0, jnp.zeros((8, HEAD), f32))

        @pl.loop(1, S // R)
        def _(t):
            r = pl.multiple_of(t * R, R)
            rows(r, p_ref[pl.ds(r - 8, 8), :])

    return pl.pallas_call(
        body, name=name, grid=(3 * nh,),
        in_specs=[pl.BlockSpec((S, HEAD), lambda j: (0, j)), pl.BlockSpec((CONV_K, HEAD), lambda j: (0, j))],
        out_specs=pl.BlockSpec((None, S, HEAD), lambda j: (j // nh, 0, j % nh)),
        out_shape=SDS((3, S, W), f32), compiler_params=_params(1),
    )(proj, conv_w)


def dn_prep_bwd(proj, conv_w, dqkv, dz, name, comm=None):
    S = proj.shape[0]
    W = conv_w.shape[1] // 3
    nh = W // HEAD
    nq = 3 * nh
    R = _row_tile(S, CONV_ROWS)
    nr = S // R

    def body(p_ref, w_ref, dq_ref, dz_ref, dp_ref, dw_ref, dc_ref):
        j = pl.program_id(0)

        @pl.when(j >= nq)
        def _():
            dp_ref[...] = dz_ref[...].astype(dp_ref.dtype)

        @pl.when(j < nq)
        def _():
            w = w_ref[...]
            dw_ref[...] = jnp.zeros_like(dw_ref)

            def rows(r, prev8):
                cur = p_ref[pl.ds(r, R), :]
                taps = _conv_taps(cur, prev8)
                cv = taps[0] * w[0:1] + taps[1] * w[1:2] + taps[2] * w[2:3] + taps[3] * w[3:4]
                dn = dq_ref[pl.ds(r, R), :]

                @pl.when(j < 2 * nh)
                def _():
                    dc_ref[pl.ds(r, R), :] = jax.vjp(_act_qk, cv)[1](dn)[0]

                @pl.when(j >= 2 * nh)
                def _():
                    dc_ref[pl.ds(r, R), :] = jax.vjp(_silu, cv)[1](dn)[0]

                dc = dc_ref[pl.ds(r, R), :]
                dw_ref[...] += jnp.concatenate(
                    [jnp.sum(dc * taps[q], axis=0, keepdims=True) for q in range(CONV_K)], axis=0)

            rows(0, jnp.zeros((8, HEAD), f32))

            @pl.loop(1, nr)
            def _(t):
                r = pl.multiple_of(t * R, R)
                rows(r, p_ref[pl.ds(r - 8, 8), :])

            def back(r, next8):
                dc = dc_ref[pl.ds(r, R), :]
                dx = dc * w[3:4]
                for s in (1, 2, 3):
                    dx = dx + _shift_up(dc, next8, s) * w[3 - s:4 - s]
                dp_ref[pl.ds(r, R), :] = dx.astype(dp_ref.dtype)

            @pl.loop(0, nr - 1)
            def _(t):
                r = pl.multiple_of(t * R, R)
                back(r, dc_ref[pl.ds(r + R, 8), :])

            back((nr - 1) * R, jnp.zeros((8, HEAD), f32))

    clamp = lambda j: jnp.minimum(j, nq - 1)
    return hosted_call(
        body, comm, _grid_steps(4 * nh), name=name, grid=(4 * nh,),
        in_specs=[pl.BlockSpec((S, HEAD), lambda j: (0, clamp(j))),
                  pl.BlockSpec((CONV_K, HEAD), lambda j: (0, clamp(j))),
                  pl.BlockSpec((None, S, HEAD), lambda j: (clamp(j) // nh, 0, clamp(j) % nh)),
                  pl.BlockSpec((S, HEAD), lambda j: (0, jnp.maximum(j - nq, 0)))],
        out_specs=(pl.BlockSpec((S, HEAD), lambda j: (0, j)), pl.BlockSpec((CONV_K, HEAD), lambda j: (0, clamp(j)))),
        out_shape=(SDS((S, 4 * W), MXU_DTYPE), SDS((CONV_K, 3 * W), f32)),
        scratch_shapes=[pltpu.VMEM((S, HEAD), f32)], args=(proj, conv_w, dqkv, dz))


def _lane_pick(x, lane):
    sel = lax.broadcasted_iota(jnp.int32, x.shape, 1) == lane
    return jnp.broadcast_to(jnp.sum(jnp.where(sel, x, 0.0), axis=1, keepdims=True), x.shape)


CUM_ROWS = 256


def _sel_mm(m01, x):
    m = _c(m01)
    d = lambda p: lax.dot_general(m, p, (NN, ((), ())), preferred_element_type=f32)
    h1, h2, h3 = _pieces3(x)
    return (d(h1) + d(h2)) + d(h3)


def _chunk_cumsum_matrix(n, transpose):
    r, c = lax.broadcasted_iota(jnp.int32, (n, n), 0), lax.broadcasted_iota(jnp.int32, (n, n), 1)
    sh = int(math.log2(DN_CHUNK))
    same = (r >> sh) == (c >> sh)
    return jnp.where(same & ((r <= c) if transpose else (r >= c)), 1.0, 0.0).astype(f32)


def _gates_by_lane(H, p, al, dt):
    lane = lax.broadcasted_iota(jnp.int32, p.shape, 1)
    g = -jnp.exp(al) * jax.nn.softplus(p + dt)
    return jnp.where(lane < H, jax.nn.sigmoid(p), jnp.where(lane < 2 * H, g, 0.0))


def dn_gates(pba, al, dt, H, name):
    S = pba.shape[0]
    R = _row_tile(S, CUM_ROWS)

    def body(p_ref, al_ref, dt_ref, o_ref):
        raw = _gates_by_lane(H, p_ref[...], al_ref[...], dt_ref[...])
        lane = lax.broadcasted_iota(jnp.int32, raw.shape, 1)
        o_ref[...] = jnp.where(lane < H, raw, _sel_mm(_chunk_cumsum_matrix(R, False), raw))

    blk = pl.BlockSpec((R, HEAD), lambda i: (i, 0))
    par = pl.BlockSpec((1, HEAD), lambda i: (0, 0))
    return pl.pallas_call(body, name=name, grid=(S // R,), in_specs=[blk, par, par], out_specs=blk,
                          out_shape=SDS((S, HEAD), f32), compiler_params=_params(1))(pba, al, dt)


def dn_gates_bwd(pba, al, dt, dgates, H, name):
    S = pba.shape[0]
    R = _row_tile(S, CUM_ROWS)

    def body(p_ref, al_ref, dt_ref, dg_ref, dp_ref, dal_ref, ddt_ref):
        @pl.when(pl.program_id(0) == 0)
        def _():
            dal_ref[...] = jnp.zeros_like(dal_ref)
            ddt_ref[...] = jnp.zeros_like(ddt_ref)

        d = dg_ref[...]
        lane = lax.broadcasted_iota(jnp.int32, d.shape, 1)
        d = jnp.where(lane < H, d, _sel_mm(_chunk_cumsum_matrix(R, True), d))
        _, vjp = jax.vjp(functools.partial(_gates_by_lane, H), p_ref[...], al_ref[...], dt_ref[...])
        dp, dal, ddt = vjp(d)
        dp_ref[...] = dp.astype(dp_ref.dtype)
        dal_ref[...] += dal
        ddt_ref[...] += ddt

    blk = pl.BlockSpec((R, HEAD), lambda i: (i, 0))
    par = pl.BlockSpec((1, HEAD), lambda i: (0, 0))
    return pl.pallas_call(
        body, name=name, grid=(S // R,), in_specs=[blk, par, par, blk], out_specs=(blk, par, par),
        out_shape=(SDS((S, HEAD), MXU_DTYPE), SDS((1, HEAD), f32), SDS((1, HEAD), f32)), compiler_params=_params(1),
    )(pba, al, dt, dgates)


def _bdot(dims):
    back = {NN: ((NT, 'gb'), (TN, 'ag')), NT: ((NN, 'gb'), (TN, 'ga')), TN: ((NT, 'bg'), (NN, 'ag'))}[dims]
    d = lambda p, q, dm: lax.dot_general(_c(p), _c(q), (dm, ((), ())), preferred_element_type=f32)

    @jax.custom_vjp
    def f(a, b):
        return d(a, b, dims)

    def fwd(a, b):
        return d(a, b, dims), (a, b)

    def bwd(res, g):
        v = {'a': res[0], 'b': res[1], 'g': g}
        (da_dims, da_ops), (db_dims, db_ops) = back
        return d(v[da_ops[0]], v[da_ops[1]], da_dims), d(v[db_ops[0]], v[db_ops[1]], db_dims)

    f.defvjp(fwd, bwd)
    return f, lambda a, b: d(a, b, dims)


_BDOT = {dims: _bdot(dims) for dims in (NN, NT, TN)}


def _tri_inv_multi(Ls):
    n = Ls[0].shape[0]
    eye = jnp.where(lax.broadcasted_iota(jnp.int32, (n, n), 0) == lax.broadcasted_iota(jnp.int32, (n, n), 1), 1.0, 0.0)
    P = tuple(-L for L in Ls)
    T = tuple(eye + p for p in P)
    for _ in range(int(math.log2(n)) - 1):
        P = tuple(_dot3(p, p, NN) for p in P)
        T = tuple(t + _dot3(t, p, NN) for t, p in zip(T, P))
    return T


@jax.custom_vjp
def _tri_inv_multi_vjp(Ls):
    return _tri_inv_multi(Ls)


def _tri_inv_fwd(Ls):
    T = _tri_inv_multi(Ls)
    return T, T


def _tri_inv_bwd(T, dT):
    X = tuple(_dot3(d, t, NT) for d, t in zip(dT, T))
    return (tuple(-_dot3(t, x, TN) for t, x in zip(T, X)),)


_tri_inv_multi_vjp.defvjp(_tri_inv_fwd, _tri_inv_bwd)


def _pieces3(x):
    h1 = x.astype(MXU_DTYPE)
    r1 = x - h1.astype(f32)
    h2 = r1.astype(MXU_DTYPE)
    return h1, h2, (r1 - h2.astype(f32)).astype(MXU_DTYPE)


def _row_bcast_impl(sel_row, gc):
    s = _c(sel_row)
    d = lambda p: lax.dot_general(s, p, (NT, ((), ())), preferred_element_type=f32)
    h1, h2, h3 = _pieces3(gc)
    return (d(h1) + d(h2)) + d(h3)


def _row_bcast_bwd(sel_row, d):
    s = _c(sel_row)
    hi, lo = _split(d)
    t = lambda p: lax.dot_general(p, s, (TN, ((), ())), preferred_element_type=f32)
    return jnp.zeros_like(sel_row), t(hi) + t(lo)


_row_bcast = jax.custom_vjp(_row_bcast_impl)
_row_bcast.defvjp(lambda sel_row, gc: (_row_bcast_impl(sel_row, gc), sel_row), _row_bcast_bwd)


def _col_bcast_impl(gc):
    return gc[:, :DN_CHUNK]


def _col_bcast_bwd(_, d):
    return (jnp.broadcast_to(jnp.sum(d, axis=1, keepdims=True) * (1.0 / HEAD), (d.shape[0], HEAD)),)


_col_bcast = jax.custom_vjp(_col_bcast_impl)
_col_bcast.defvjp(lambda gc: (_col_bcast_impl(gc), None), _col_bcast_bwd)


def _last_row_bcast(n):
    def impl(gc):
        return jnp.broadcast_to(gc[DN_CHUNK - 1:DN_CHUNK, :], (n, HEAD))

    def bwd(_, d):
        row = lax.broadcasted_iota(jnp.int32, (DN_CHUNK, HEAD), 0)
        return (jnp.where(row == DN_CHUNK - 1, jnp.sum(d, axis=0, keepdims=True), 0.0),)

    f = jax.custom_vjp(impl)
    f.defvjp(lambda gc: (impl(gc), None), bwd)
    return impl, f


_LAST_C, _LAST_H = _last_row_bcast(DN_CHUNK), _last_row_bcast(HEAD)


def _halves(axis):
    def impl(x):
        n = x.shape[axis] // 2
        return lax.slice_in_dim(x, 0, n, axis=axis), lax.slice_in_dim(x, n, 2 * n, axis=axis)

    f = jax.custom_vjp(impl)
    f.defvjp(lambda x: (impl(x), None), lambda _, g: (jnp.concatenate(g, axis=axis),))
    return impl, f


_ROW_HALVES, _COL_HALVES = _halves(0), _halves(1)


def _chunk_consts():
    C = DN_CHUNK
    io = lambda shape, ax: lax.broadcasted_iota(jnp.int32, shape, ax)
    one = lambda m: jnp.where(m, 1.0, 0.0).astype(f32)
    r, c = io((C, C), 0), io((C, C), 1)
    return dict(causal=r >= c, strict=r > c, sel_row=one(io((C, HEAD), 1) == 0))


def _chunk_fn(kc, diff, q, k, v, gc, bB, S0):
    i = 0 if diff else 1
    mm, mm_nt, mm_tn = _BDOT[NN][i], _BDOT[NT][i], _BDOT[TN][i]
    tri = _tri_inv_multi_vjp if diff else _tri_inv_multi
    each = lambda f, *ls: tuple(f(*a) for a in zip(*ls))
    gcol = each(_col_bcast if diff else _col_bcast_impl, gc)
    grow = each(lambda g: (_row_bcast if diff else _row_bcast_impl)(kc['sel_row'], g), gc)
    glc = each(_LAST_C[i ^ 1], gc)
    glh = each(_LAST_H[i ^ 1], gc)
    decay = each(lambda a, b: jnp.where(kc['causal'], jnp.exp(jnp.where(kc['causal'], a - b, 0.0)), 0.0), gcol, grow)
    rows, cols = _ROW_HALVES[i ^ 1], _COL_HALVES[i ^ 1]
    first, second = (lambda ts: tuple(t[0] for t in ts)), (lambda ts: tuple(t[1] for t in ts))
    kb = each(lambda a, b: a * b, k, bB)
    vb = each(lambda a, b: a * b, v, bB)
    egc = each(jnp.exp, gc)
    qs = each(lambda a: a * (HEAD ** -0.5), q)
    kq = each(lambda a, b, kt: rows(mm_nt(jnp.concatenate([a, b], axis=0), kt)), kb, qs, k)
    kk, qk = first(kq), second(kq)
    T = tri(each(lambda a, d: jnp.where(kc['strict'], a * d, 0.0), kk, decay))
    uw = each(lambda t, a, b, e: cols(mm(t, jnp.concatenate([a, b * e], axis=1))), T, vb, kb, egc)
    u, w = first(uw), second(uw)
    attn = each(lambda a, d: jnp.where(kc['causal'], a * d, 0.0), qk, decay)
    wq = each(lambda a, b, e, s: rows(mm(jnp.concatenate([a, b * e], axis=0), s)), w, qs, egc, S0)
    wS, qS = first(wq), second(wq)
    v_new = each(lambda a, b: a - b, u, wS)
    o = each(lambda a, b: a + b, qS, each(mm, attn, v_new))
    kdec = each(lambda a, gl, g: a * jnp.exp(gl - g), k, glc, gc)
    S1 = each(lambda s, gl, kv: s * jnp.exp(gl) + kv, S0, glh, each(mm_tn, kdec, v_new))
    return o, S1


def _chunks_per_step(N):
    return 4 if N % 4 == 0 else (2 if N % 2 == 0 else 1)


def _heads_per_block(H):
    return 8 if H % 8 == 0 else (4 if H % 4 == 0 else 1)


def dn_chunk_fwd(qkv, gates, name, comm=None):
    _, S, W = qkv.shape
    H, C = W // HEAD, DN_CHUNK
    N, HB = S // C, _heads_per_block(H)
    assert HB == H
    CPS = _chunks_per_step(N)

    def body(q_ref, k_ref, v_ref, g_ref, o_ref, st_ref, s_scr):
        @pl.when(pl.program_id(1) == 0)
        def _():
            s_scr[...] = jnp.zeros_like(s_scr)

        kc = _chunk_consts()
        sls = [slice(hh * HEAD, (hh + 1) * HEAD) for hh in range(HB)]
        St = tuple(s_scr[hh] for hh in range(HB))
        for c in range(CPS):
            rows = slice(c * C, (c + 1) * C)
            heads = lambda ref: tuple(ref[rows, sl] for sl in sls)
            gr = g_ref[rows, :]
            for hh in range(HB):
                st_ref[c, hh] = St[hh]
            o, St = _chunk_fn(kc, False, heads(q_ref), heads(k_ref), heads(v_ref),
                              tuple(_lane_pick(gr, H + hh) for hh in range(HB)),
                              tuple(_lane_pick(gr, hh) for hh in range(HB)), St)
            for hh in range(HB):
                o_ref[rows, sls[hh]] = o[hh]
        for hh in range(HB):
            s_scr[hh] = St[hh]

    part = lambda p: pl.BlockSpec((None, CPS * C, HB * HEAD), lambda hb, n: (p, n, hb))
    return hosted_call(
        body, comm, _grid_steps(H // HB, N // CPS), name=name, grid=(H // HB, N // CPS),
        in_specs=[part(0), part(1), part(2), pl.BlockSpec((CPS * C, HEAD), lambda hb, n: (n, 0))],
        out_specs=(pl.BlockSpec((CPS * C, HB * HEAD), lambda hb, n: (n, hb)),
                   pl.BlockSpec((CPS, HB, HEAD, HEAD), lambda hb, n: (n, hb, 0, 0))),
        out_shape=(SDS((S, W), f32), SDS((N, H, HEAD, HEAD), f32)),
        scratch_shapes=[pltpu.VMEM((HB, HEAD, HEAD), f32)], args=(qkv, qkv, qkv, gates))


def dn_chunk_bwd(qkv, gates, states, do, name, comm=None):
    _, S, W = qkv.shape
    H, C = W // HEAD, DN_CHUNK
    N, HB = S // C, _heads_per_block(H)
    assert HB == H
    CPS = _chunks_per_step(N)
    NB = N // CPS

    def body(q_ref, k_ref, v_ref, g_ref, st_ref, do_ref, dqkv_ref, dg_ref, ds_scr):
        @pl.when(pl.program_id(1) == 0)
        def _():
            ds_scr[...] = jnp.zeros_like(ds_scr)

        kc = _chunk_consts()
        sls = [slice(hh * HEAD, (hh + 1) * HEAD) for hh in range(HB)]
        dSt = tuple(ds_scr[hh] for hh in range(HB))
        for c in reversed(range(CPS)):
            rows = slice(c * C, (c + 1) * C)
            heads = lambda ref: tuple(ref[rows, sl] for sl in sls)
            gr = g_ref[rows, :]
            _, vjp = jax.vjp(functools.partial(_chunk_fn, kc, True), heads(q_ref), heads(k_ref), heads(v_ref),
                             tuple(_lane_pick(gr, H + hh) for hh in range(HB)),
                             tuple(_lane_pick(gr, hh) for hh in range(HB)), tuple(st_ref[c, hh] for hh in range(HB)))
            dq, dk, dv, dg, db, dSt = vjp((heads(do_ref), dSt))
            lane = lax.broadcasted_iota(jnp.int32, (C, HEAD), 1)
            dgr = jnp.zeros((C, HEAD), f32)
            for hh in range(HB):
                dqkv_ref[0, rows, sls[hh]] = dq[hh]
                dqkv_ref[1, rows, sls[hh]] = dk[hh]
                dqkv_ref[2, rows, sls[hh]] = dv[hh]
                dgr = dgr + jnp.where(lane == hh, jnp.sum(db[hh], axis=1, keepdims=True), 0.0)
                dgr = dgr + jnp.where(lane == H + hh, jnp.sum(dg[hh], axis=1, keepdims=True), 0.0)
            dg_ref[rows, :] = dgr
        for hh in range(HB):
            ds_scr[hh] = dSt[hh]

    rev = lambda n: NB - 1 - n
    part = lambda p: pl.BlockSpec((None, CPS * C, HB * HEAD), lambda hb, n: (p, rev(n), hb))
    gate = pl.BlockSpec((CPS * C, HEAD), lambda hb, n: (rev(n), 0))
    return hosted_call(
        body, comm, _grid_steps(H // HB, NB), name=name, grid=(H // HB, NB),
        in_specs=[part(0), part(1), part(2), gate,
                  pl.BlockSpec((CPS, HB, HEAD, HEAD), lambda hb, n: (rev(n), hb, 0, 0)),
                  pl.BlockSpec((CPS * C, HB * HEAD), lambda hb, n: (rev(n), hb))],
        out_specs=(pl.BlockSpec((3, CPS * C, HB * HEAD), lambda hb, n: (0, rev(n), hb)), gate),
        out_shape=(SDS((3, S, W), f32), SDS((S, HEAD), f32)),
        scratch_shapes=[pltpu.VMEM((HB, HEAD, HEAD), f32)], args=(qkv, qkv, qkv, gates, states, do))


def _gate_norm(o, z, ng):
    return _rms(o, ng) * _silu(z)


def dn_out(o, proj, ng, wout, x1, g3, name):
    S, W = o.shape
    D = x1.shape[1]
    nh = W // HEAD
    tm = _row_tile(S, 256)

    def body(o_ref, z_ref, ng_ref, w_ref, x_ref, g_ref, xo_ref, m_ref, og_ref):
        for h in range(nh):
            sl = slice(h * HEAD, (h + 1) * HEAD)
            og_ref[:, sl] = _gate_norm(o_ref[:, sl], z_ref[:, sl], ng_ref[...]).astype(og_ref.dtype)
        m = _mm(og_ref[...], w_ref[...])
        m_ref[...] = m
        xo_ref[...] = x_ref[...] + _rms(m, g_ref[...])

    rw = pl.BlockSpec((tm, W), lambda i: (i, 0))
    rd = pl.BlockSpec((tm, D), lambda i: (i, 0))
    return pl.pallas_call(
        body, name=name, grid=(S // tm,),
        in_specs=[rw, pl.BlockSpec((tm, W), lambda i: (i, 3)), pl.BlockSpec((1, HEAD), lambda i: (0, 0)),
                  pl.BlockSpec((W, D), lambda i: (0, 0)), rd, pl.BlockSpec((1, D), lambda i: (0, 0))],
        out_specs=(rd, rd, rw),
        out_shape=(SDS((S, D), f32), SDS((S, D), f32), SDS((S, W), MXU_DTYPE)), compiler_params=_params(1),
    )(o, proj, ng, wout, x1, g3)


def dn_out_bwd(dxo, m, g3, o, proj, ng, wout, name):
    S, W = o.shape
    D = m.shape[1]
    nh = W // HEAD
    tm = _row_tile(S, 256)

    def body(dxo_ref, m_ref, g_ref, o_ref, z_ref, ng_ref, w_ref, dm_ref, do_ref, dz_ref, dng_ref, dg_ref):
        @pl.when(pl.program_id(0) == 0)
        def _():
            dng_ref[...] = jnp.zeros_like(dng_ref)
            dg_ref[...] = jnp.zeros_like(dg_ref)

        dm, dg = _rms_bwd(m_ref[...], g_ref[...], dxo_ref[...])
        dg_ref[...] += dg
        dmc = dm.astype(dm_ref.dtype)
        dm_ref[...] = dmc
        dog = _mm_nt(dmc, w_ref[...])
        for h in range(nh):
            sl = slice(h * HEAD, (h + 1) * HEAD)
            _, vjp = jax.vjp(_gate_norm, o_ref[:, sl], z_ref[:, sl], ng_ref[...])
            do, dz, dng = vjp(dog[:, sl])
            do_ref[:, sl] = do
            dz_ref[:, sl] = dz.astype(dz_ref.dtype)
            dng_ref[...] += dng

    rw = pl.BlockSpec((tm, W), lambda i: (i, 0))
    rd = pl.BlockSpec((tm, D), lambda i: (i, 0))
    vd = pl.BlockSpec((1, D), lambda i: (0, 0))
    vh = pl.BlockSpec((1, HEAD), lambda i: (0, 0))
    return pl.pallas_call(
        body, name=name, grid=(S // tm,),
        in_specs=[rd, rd, vd, rw, pl.BlockSpec((tm, W), lambda i: (i, 3)), vh, pl.BlockSpec((W, D), lambda i: (0, 0))],
        out_specs=(rd, rw, rw, vh, vd),
        out_shape=(SDS((S, D), MXU_DTYPE), SDS((S, W), f32), SDS((S, W), MXU_DTYPE), SDS((1, HEAD), f32),
                   SDS((1, D), f32)),
        compiler_params=_params(1),
    )(dxo, m, g3, o, proj, ng, wout)


def _erf_arg(x):
    return lax.erf(x * 0.7071067811865476)


@jax.custom_vjp
def _gelu_with_erf(x, e):
    return 0.5 * x * (1.0 + e)


def _gelu_with_erf_bwd(res, g):
    x, e = res
    return g * (0.5 * (1.0 + e) + x * (jnp.exp(-0.5 * x * x) * 0.3989422804014327)), jnp.zeros_like(e)


_gelu_with_erf.defvjp(lambda x, e: (0.5 * x * (1.0 + e), (x, e)), _gelu_with_erf_bwd)


def _layernorm(t, lg, lb):
    tc = t - jnp.mean(t, axis=-1, keepdims=True)
    return tc * lax.rsqrt(jnp.mean(tc * tc, axis=-1, keepdims=True) + LN_EPS) * lg + lb


def _sg_stage1_kept(eu, ev, pu, pv, bu, bv, lg, lb):
    return _gelu_with_erf(pu + bu, eu), _layernorm(_gelu_with_erf(pv + bv, ev), lg, lb)


def _causal_mask(n):
    return lax.broadcasted_iota(jnp.int32, (n, n), 0) >= lax.broadcasted_iota(jnp.int32, (n, n), 1)


def sg_mid(pre, b_in, ln_g, ln_b, w_s, bsT, wout, x1, g3, name):
    S = pre.shape[0]
    E, D = ln_g.shape[1], x1.shape[1]
    G, CH = SG_GROUPS, SG_CHUNK
    Cg = E // G
    tm = _row_tile(S, 256)

    def body(pu_ref, pv_ref, bu_ref, bv_ref, lg_ref, lb_ref, ws_ref, bs_ref, w_ref, x_ref, g_ref,
             xo_ref, m_ref, gt_ref, e_ref):
        xu, xv = pu_ref[...] + bu_ref[...], pv_ref[...] + bv_ref[...]
        eu, ev = _erf_arg(xu), _erf_arg(xv)
        e_ref[:, :E] = eu.astype(e_ref.dtype)
        e_ref[:, E:] = ev.astype(e_ref.dtype)
        u = 0.5 * xu * (1.0 + eu)
        v = _layernorm(0.5 * xv * (1.0 + ev), lg_ref[...], lb_ref[...])
        mask = _causal_mask(CH)
        for g in range(G):
            wc = _c(jnp.where(mask, ws_ref[g], 0.0))
            bcol = bs_ref[:, g:g + 1]
            cs = slice(g * Cg, (g + 1) * Cg)
            for ch in range(tm // CH):
                rs = slice(ch * CH, (ch + 1) * CH)
                mixed = _mm(wc, _c(v[rs, cs])) + bcol
                gt_ref[rs, cs] = (u[rs, cs] * mixed).astype(gt_ref.dtype)
        m = _mm(gt_ref[...], w_ref[...])
        m_ref[...] = m
        xo_ref[...] = x_ref[...] + _rms(m, g_ref[...])

    half = lambda p: pl.BlockSpec((tm, E), lambda i: (i, p))
    vhalf = lambda p: pl.BlockSpec((1, E), lambda i: (0, p))
    ve = pl.BlockSpec((1, E), lambda i: (0, 0))
    rd = pl.BlockSpec((tm, D), lambda i: (i, 0))
    return pl.pallas_call(
        body, name=name, grid=(S // tm,),
        in_specs=[half(0), half(1), vhalf(0), vhalf(1), ve, ve, pl.BlockSpec((G, CH, CH), lambda i: (0, 0, 0)),
                  pl.BlockSpec((CH, G), lambda i: (0, 0)), pl.BlockSpec((E, D), lambda i: (0, 0)), rd,
                  pl.BlockSpec((1, D), lambda i: (0, 0))],
        out_specs=(rd, rd, pl.BlockSpec((tm, E), lambda i: (i, 0)), pl.BlockSpec((tm, 2 * E), lambda i: (i, 0))),
        out_shape=(SDS((S, D), f32), SDS((S, D), f32), SDS((S, E), MXU_DTYPE), SDS((S, 2 * E), MXU_DTYPE)),
        compiler_params=_params(1),
    )(pre, pre, b_in, b_in, ln_g, ln_b, w_s, bsT, wout, x1, g3)


def sg_mid_bwd(dxo, m, g3, pre, kept_erf, b_in, ln_g, ln_b, w_s, bsT, wout, name):
    S = pre.shape[0]
    E, D = ln_g.shape[1], m.shape[1]
    G, CH = SG_GROUPS, SG_CHUNK
    Cg = E // G
    tm = _row_tile(S, 256)

    def body(dxo_ref, m_ref, g_ref, pu_ref, pv_ref, eu_ref, ev_ref, bu_ref, bv_ref, lg_ref, lb_ref, ws_ref, bs_ref,
             w_ref, dm_ref, dpre_ref, dbin_ref, dlg_ref, dlb_ref, dws_ref, dbs_ref, dg_ref, du_scr, dv_scr):
        @pl.when(pl.program_id(0) == 0)
        def _():
            for r in (dbin_ref, dlg_ref, dlb_ref, dws_ref, dbs_ref, dg_ref):
                r[...] = jnp.zeros_like(r)

        dm, dg = _rms_bwd(m_ref[...], g_ref[...], dxo_ref[...])
        dg_ref[...] += dg
        dmc = dm.astype(dm_ref.dtype)
        dm_ref[...] = dmc
        dgated = _mm_nt(dmc, w_ref[...])
        stage1 = functools.partial(_sg_stage1_kept, eu_ref[...].astype(f32), ev_ref[...].astype(f32))
        (u, v), vjp1 = jax.vjp(stage1, pu_ref[...], pv_ref[...], bu_ref[...], bv_ref[...], lg_ref[...], lb_ref[...])
        mask = _causal_mask(CH)
        lane = lax.broadcasted_iota(jnp.int32, (CH, CH), 1)
        for g in range(G):
            wc = _c(jnp.where(mask, ws_ref[g], 0.0))
            bcol = bs_ref[:, g:g + 1]
            cs = slice(g * Cg, (g + 1) * Cg)
            dws = jnp.zeros((CH, CH), f32)
            dbs = jnp.zeros((CH, 1), f32)
            for ch in range(tm // CH):
                rs = slice(ch * CH, (ch + 1) * CH)
                vs = _c(v[rs, cs])
                mixed = _mm(wc, vs) + bcol
                dgt = dgated[rs, cs]
                du_scr[rs, cs] = dgt * mixed
                dmixed = dgt * u[rs, cs]
                dmc2 = _c(dmixed)
                dv_scr[rs, cs] = _mm_tn(wc, dmc2)
                dws = dws + _mm_nt(dmc2, vs)
                dbs = dbs + jnp.sum(dmixed, axis=1, keepdims=True)
            dws_ref[g] += jnp.where(mask, dws, 0.0)
            dbs_ref[...] += jnp.where(lane == g, jnp.broadcast_to(dbs, (CH, CH)), 0.0)
        dpu, dpv, dbu, dbv, dlg, dlb = vjp1((du_scr[...], dv_scr[...]))
        dpre_ref[:, :E] = dpu.astype(dpre_ref.dtype)
        dpre_ref[:, E:] = dpv.astype(dpre_ref.dtype)
        dbin_ref[:, :E] += dbu
        dbin_ref[:, E:] += dbv
        dlg_ref[...] += dlg
        dlb_ref[...] += dlb

    half = lambda p: pl.BlockSpec((tm, E), lambda i: (i, p))
    vhalf = lambda p: pl.BlockSpec((1, E), lambda i: (0, p))
    ve = pl.BlockSpec((1, E), lambda i: (0, 0))
    rd = pl.BlockSpec((tm, D), lambda i: (i, 0))
    vd = pl.BlockSpec((1, D), lambda i: (0, 0))
    wsb = pl.BlockSpec((G, CH, CH), lambda i: (0, 0, 0))
    return pl.pallas_call(
        body, name=name, grid=(S // tm,),
        in_specs=[rd, rd, vd, half(0), half(1), half(0), half(1), vhalf(0), vhalf(1), ve, ve, wsb,
                  pl.BlockSpec((CH, G), lambda i: (0, 0)), pl.BlockSpec((E, D), lambda i: (0, 0))],
        out_specs=(rd, pl.BlockSpec((tm, 2 * E), lambda i: (i, 0)), pl.BlockSpec((1, 2 * E), lambda i: (0, 0)), ve, ve,
                   wsb, pl.BlockSpec((CH, CH), lambda i: (0, 0)), vd),
        out_shape=(SDS((S, D), MXU_DTYPE), SDS((S, 2 * E), MXU_DTYPE), SDS((1, 2 * E), f32), SDS((1, E), f32),
                   SDS((1, E), f32), SDS((G, CH, CH), f32), SDS((CH, CH), f32), SDS((1, D), f32)),
        scratch_shapes=[pltpu.VMEM((tm, E), f32), pltpu.VMEM((tm, E), f32)], compiler_params=_params(1),
    )(dxo, m, g3, pre, pre, kept_erf, kept_erf, b_in, b_in, ln_g, ln_b, w_s, bsT, wout)


def loss_head(y, target, name):
    S, D = y.shape
    tm = _row_tile(S, 512)

    def body(y_ref, t_ref, l_ref, d_ref):
        @pl.when(pl.program_id(0) == 0)
        def _():
            l_ref[...] = jnp.zeros_like(l_ref)

        e = y_ref[...] - t_ref[...]
        d_ref[...] = e * (1.0 / D)
        l_ref[...] += jnp.sum(e * e) * (0.5 / D)

    row = pl.BlockSpec((tm, D), lambda i: (i, 0))
    return pl.pallas_call(
        body, name=name, grid=(S // tm,), in_specs=[row, row],
        out_specs=(pl.BlockSpec((1, HEAD), lambda i: (0, 0)), row),
        out_shape=(SDS((1, HEAD), f32), SDS((S, D), f32)), compiler_params=_params(1),
    )(y, target)


def sum_slots(r, name):
    _, R, C = r.shape
    tr = R // 2 if R % 16 == 0 else R

    def body(r_ref, o_ref):
        acc = r_ref[0].astype(f32)
        for s in range(1, N_DEV):
            acc = acc + r_ref[s].astype(f32)
        o_ref[...] = acc

    return pl.pallas_call(
        body, name=name, grid=(R // tr,), in_specs=[pl.BlockSpec((N_DEV, tr, C), lambda i: (0, i, 0))],
        out_specs=pl.BlockSpec((tr, C), lambda i: (i, 0)), out_shape=SDS((R, C), f32), compiler_params=_params(1),
    )(r)


def _adam_math(w, g, m, v):
    m = ADAM_B1 * m + (1.0 - ADAM_B1) * g
    v = ADAM_B2 * v + (1.0 - ADAM_B2) * (g * g)
    m_hat = m / (1.0 - ADAM_B1 ** ADAM_STEP)
    v_hat = v / (1.0 - ADAM_B2 ** ADAM_STEP)
    delta = -ADAM_LR * (m_hat / (jnp.sqrt(v_hat) + ADAM_EPS) + ADAM_WD * w)
    return delta, m, v


def adam_slots(w, rs, m, v, name, tr):
    R, C = w.shape
    tr = _row_tile(min(r.shape[1] for r in rs), tr)
    blocks = [r.shape[1] // tr for r in rs]
    starts = [sum(blocks[:k]) for k in range(len(rs))]
    assert sum(blocks) * tr == R

    def body(w_ref, *refs):
        r_refs, (m_ref, v_ref, g_ref, d_ref, mo_ref, vo_ref) = refs[:len(rs)], refs[len(rs):]
        i = pl.program_id(0)
        for k, r_ref in enumerate(r_refs):
            @pl.when((i >= starts[k]) & (i < starts[k] + blocks[k]))
            def _():
                g = r_ref[0].astype(f32)
                for s in range(1, N_DEV):
                    g = g + r_ref[s].astype(f32)
                g_ref[...] = g

        d_ref[...], mo_ref[...], vo_ref[...] = _adam_math(w_ref[...], g_ref[...], m_ref[...], v_ref[...])

    row = pl.BlockSpec((tr, C), lambda i: (i, 0))
    piece = lambda k: pl.BlockSpec((N_DEV, tr, C), lambda i: (0, jnp.clip(i - starts[k], 0, blocks[k] - 1), 0))
    return pl.pallas_call(
        body, name=name, grid=(R // tr,), in_specs=[row] + [piece(k) for k in range(len(rs))] + [row, row],
        out_specs=(row, row, row, row), out_shape=tuple(SDS((R, C), f32) for _ in range(4)),
        compiler_params=_params(1),
    )(w, *rs, m, v)


def adam_small(w, g, m, v, name):
    def body(w_ref, g_ref, m_ref, v_ref, d_ref, mo_ref, vo_ref):
        d_ref[...], mo_ref[...], vo_ref[...] = _adam_math(w_ref[...], g_ref[...], m_ref[...], v_ref[...])

    return pl.pallas_call(body, name=name, out_shape=tuple(SDS(w.shape, f32) for _ in range(3)))(w, g, m, v)


def _pack_rows(parts):
    rows, offs, r = [], [], 0
    for p in parts:
        flat = p.reshape(-1)
        n = -(-flat.shape[0] // HEAD)
        flat = jnp.pad(flat, (0, n * HEAD - flat.shape[0]))
        rows.append(flat.reshape(n, HEAD))
        offs.append((r, n))
        r += n
    pad = (-r) % 8
    if pad:
        rows.append(jnp.zeros((pad, HEAD), f32))
    return jnp.concatenate(rows, axis=0), offs


def kernel(x, norm_g, ffn_w_gate, ffn_w_up, ffn_w_down, dn_w_in, dn_conv_w, dn_a_log, dn_dt_bias, dn_norm_g, dn_w_out, sg_w_in, sg_b_in, sg_ln_g, sg_ln_b, sg_w_s, sg_b_s, sg_w_out, loss_target, m_norm_g, m_ffn_w_gate, m_ffn_w_up, m_ffn_w_down, m_dn_w_in, m_dn_conv_w, m_dn_a_log, m_dn_dt_bias, m_dn_norm_g, m_dn_w_out, m_sg_w_in, m_sg_b_in, m_sg_ln_g, m_sg_ln_b, m_sg_w_s, m_sg_b_s, m_sg_w_out, v_norm_g, v_ffn_w_gate, v_ffn_w_up, v_ffn_w_down, v_dn_w_in, v_dn_conv_w, v_dn_a_log, v_dn_dt_bias, v_dn_norm_g, v_dn_w_out, v_sg_w_in, v_sg_b_in, v_sg_ln_g, v_sg_ln_b, v_sg_w_s, v_sg_b_s, v_sg_w_out):
    weights = dict(norm_g=norm_g, ffn_w_gate=ffn_w_gate, ffn_w_up=ffn_w_up, ffn_w_down=ffn_w_down, dn_w_in=dn_w_in,
                   dn_conv_w=dn_conv_w, dn_a_log=dn_a_log, dn_dt_bias=dn_dt_bias, dn_norm_g=dn_norm_g,
                   dn_w_out=dn_w_out, sg_w_in=sg_w_in, sg_b_in=sg_b_in, sg_ln_g=sg_ln_g, sg_ln_b=sg_ln_b,
                   sg_w_s=sg_w_s, sg_b_s=sg_b_s, sg_w_out=sg_w_out)
    mom_m = dict(norm_g=m_norm_g, ffn_w_gate=m_ffn_w_gate, ffn_w_up=m_ffn_w_up, ffn_w_down=m_ffn_w_down,
                 dn_w_in=m_dn_w_in, dn_conv_w=m_dn_conv_w, dn_a_log=m_dn_a_log, dn_dt_bias=m_dn_dt_bias,
                 dn_norm_g=m_dn_norm_g, dn_w_out=m_dn_w_out, sg_w_in=m_sg_w_in, sg_b_in=m_sg_b_in,
                 sg_ln_g=m_sg_ln_g, sg_ln_b=m_sg_ln_b, sg_w_s=m_sg_w_s, sg_b_s=m_sg_b_s, sg_w_out=m_sg_w_out)
    mom_v = dict(norm_g=v_norm_g, ffn_w_gate=v_ffn_w_gate, ffn_w_up=v_ffn_w_up, ffn_w_down=v_ffn_w_down,
                 dn_w_in=v_dn_w_in, dn_conv_w=v_dn_conv_w, dn_a_log=v_dn_a_log, dn_dt_bias=v_dn_dt_bias,
                 dn_norm_g=v_dn_norm_g, dn_w_out=v_dn_w_out, sg_w_in=v_sg_w_in, sg_b_in=v_sg_b_in,
                 sg_ln_g=v_sg_ln_g, sg_ln_b=v_sg_ln_b, sg_w_s=v_sg_w_s, sg_b_s=v_sg_b_s, sg_w_out=v_sg_w_out)
    order = list(weights)

    xs = x[0]
    S, D = xs.shape
    F8 = ffn_w_gate.shape[-1]
    depth = norm_g.shape[0]
    W = dn_w_out.shape[1] * N_DEV
    H = W // HEAD
    E = sg_ln_g.shape[1] * N_DEV
    G, CH = sg_w_s.shape[1], sg_w_s.shape[2]
    c8 = dn_w_in.shape[2]
    me = _slot(lax.axis_index("x"), lax.axis_index("y"), lax.axis_index("c"))

    assert depth == 2
    small_in, small_offs = _pack_rows([norm_g, dn_conv_w, sg_b_in, sg_ln_g, sg_ln_b])
    wg0a, wu0a, wd0a, small_all = all_gather_multi(
        [_c(ffn_w_gate[0, 0]), _c(ffn_w_up[0, 0]), _c(ffn_w_down[0, 0]), small_in], name="gather_first")
    ffn_shards = lambda l, ab: [_c(ffn_w_gate[l, ab]), _c(ffn_w_up[l, ab]), _c(ffn_w_down[l, ab])]
    gather_dn = Comm("gather", [_c(dn_w_in[0]), _c(dn_w_out[0])])
    gather_mid = Comm("gather", ffn_shards(0, 1) + ffn_shards(1, 0))
    gather_end = Comm("gather", ffn_shards(1, 1))
    gather_sg = Comm("gather", [_c(sg_w_in[0]), _c(sg_w_out[0])])
    per = N_DEV // FFN_SLABS
    wide = lambda tag, g, u, d: (*widen_slabs([g, u], FFN_SLABS, name=f"widen_{tag}"),
                                 d.reshape(FFN_SLABS, per * F8, D))
    ffn_w = {(0, 0): wide("0a", wg0a, wu0a, wd0a)}

    def small_piece(i, shard_shape):
        r0, n = small_offs[i]
        sz = math.prod(shard_shape)
        return small_all[:, r0:r0 + n, :].reshape(N_DEV, n * HEAD)[:, :sz].reshape((N_DEV,) + tuple(shard_shape))

    ng_full = jnp.moveaxis(small_piece(0, norm_g.shape), 0, 2).reshape(depth, 6, D)
    conv_full = jnp.moveaxis(small_piece(1, dn_conv_w.shape[1:]), 0, 1).reshape(CONV_K, 3 * W)
    bin_full = small_piece(2, sg_b_in.shape[1:]).reshape(1, 2 * E)
    lng_full = small_piece(3, sg_ln_g.shape[1:]).reshape(1, E)
    lnb_full = small_piece(4, sg_ln_b.shape[1:]).reshape(1, E)
    gate_lanes = lambda v: jnp.pad(v.reshape(1, H), ((0, 0), (H, HEAD - 2 * H)))
    al_row, dt_row = gate_lanes(dn_a_log), gate_lanes(dn_dt_bias)
    bsT = sg_b_s[0].T
    gvec = lambda l, k: ng_full[l, k].reshape(1, D)

    saved = []
    cur = xs
    for l in range(depth):
        sv = {}
        sv['x0'] = cur
        (cur, sv['hA'], sv['pA'], sv['qA'], sv['tA'], sv['yA']), got = ffn_fwd(
            cur, gvec(l, 0), gvec(l, 1), *ffn_w[l, 0], name=f"ffn_fwd_{l}a", comm=gather_dn if l == 0 else gather_sg)
        sv['x1'] = cur
        if l == 1:
            sg_win = jnp.moveaxis(got[0], 0, 1).reshape(D, 2 * E)
            sg_wout = got[1].reshape(E, D)
        if l == 0:
            dnin_all, dnout_all = got
            dn_wmain, dn_wba = join_columns(dnin_all, 4 * W, name="dn_w_in_join")
            dn_wout = dnout_all.reshape(W, D)
            sv['hM'], sv['proj'], sv['pba'] = rms_mm(cur, gvec(l, 2), dn_wmain, dn_wba, name=f"dn_in_{l}")
            sv['qkv'] = dn_prep(sv['proj'], conv_full, name=f"dn_prep_{l}")
            sv['gates'] = dn_gates(sv['pba'], al_row, dt_row, H, name=f"dn_gates_{l}")
            (sv['o'], sv['states']), got = dn_chunk_fwd(sv['qkv'], sv['gates'], name=f"dn_chunk_{l}", comm=gather_mid)
            ffn_w[0, 1], ffn_w[1, 0] = wide("0b", *got[0:3]), wide("1a", *got[3:6])
            cur, sv['m'], sv['og'] = dn_out(sv['o'], sv['proj'], dn_norm_g, dn_wout, cur, gvec(l, 3), name=f"dn_out_{l}")
        else:
            sv['hM'], sv['pre'] = rms_mm(cur, gvec(l, 2), sg_win, None, name=f"sg_in_{l}")
            cur, sv['m'], sv['gated'], sv['erf'] = sg_mid(sv['pre'], bin_full, lng_full, lnb_full, sg_w_s[0], bsT, sg_wout,
                                                          cur, gvec(l, 3), name=f"sg_mid_{l}")
        sv['x2'] = cur
        (cur, sv['hB'], sv['pB'], sv['qB'], sv['tB'], sv['yB']), got = ffn_fwd(
            cur, gvec(l, 4), gvec(l, 5), *ffn_w[l, 1], name=f"ffn_fwd_{l}b", comm=gather_end if l == 0 else None)
        if l == 0:
            ffn_w[1, 1] = wide("1b", *got[0:3])
        saved.append(sv)

    loss_blk, dcur = loss_head(cur, loss_target[0], name="loss_head")
    loss = lax.psum(loss_blk[0, 0], ("x", "y", "c"))

    dng = [[None] * 6 for _ in range(depth)]
    ffn_dw = {}
    grads, slots = {}, {}

    def ffn_backward(l, ab, dcur):
        sv, s = saved[l], 'AB'[ab]
        (dcur, da, db, dy, dng[l][4 * ab], dng[l][4 * ab + 1]), _ = ffn_bwd_dx(
            dcur, sv['x2' if ab else 'x0'], sv['y' + s], sv['p' + s], sv['q' + s], gvec(l, 4 * ab), gvec(l, 4 * ab + 1),
            *ffn_w[l, ab], name=f"ffn_bwd_{l}{'ab'[ab]}")
        ffn_dw[l, ab], _ = ffn_bwd_dw(sv['h' + s], dy, sv['t' + s], da, db, name=f"ffn_dw_{l}{'ab'[ab]}")
        return dcur

    sv = saved[1]
    dcur = ffn_backward(1, 1, dcur)
    dm, dpre, grads['sg_b_in'], grads['sg_ln_g'], grads['sg_ln_b'], grads['sg_w_s'], dbs, dng[1][3] = sg_mid_bwd(
        dcur, sv['m'], gvec(1, 3), sv['pre'], sv['erf'], bin_full, lng_full, lnb_full, sg_w_s[0], bsT, sg_wout,
        name="sg_mid_bwd_1")
    grads['sg_b_s'] = dbs[:, :G].T
    dsg_wout = tn_mm(sv['gated'], dm, name="sg_wout_dw_1").reshape(N_DEV, E // N_DEV, D)
    dsg_win = tn_mm(sv['hM'], dpre, name="sg_win_dw_1", tn=2 * E // N_DEV, slot_major=True)
    (dcur, dng[1][2]), _ = mm_bwd_dx(dcur, sv['x1'], gvec(1, 2), dpre, sg_win, None, None, name="sg_in_bwd_1")
    dcur = ffn_backward(1, 0, dcur)
    sv = saved[0]
    dcur = ffn_backward(0, 1, dcur)
    dm, do, dz, grads['dn_norm_g'], dng[0][3] = dn_out_bwd(dcur, sv['m'], gvec(0, 3), sv['o'], sv['proj'], dn_norm_g,
                                                          dn_wout, name="dn_out_bwd_0")
    ddn_wout = tn_mm(sv['og'], dm, name="dn_wout_dw_0").reshape(N_DEV, W // N_DEV, D)
    (dqkv, dgates), got = dn_chunk_bwd(sv['qkv'], sv['gates'], sv['states'], do, name="dn_chunk_bwd_0",
                                       comm=Comm("exchange", [*ffn_dw[1, 0], *ffn_dw[1, 1], dsg_win, dsg_wout]))
    l1a, l1b, slots['sg_w_in'], slots['sg_w_out'] = got[0:3], got[3:6], [got[6]], [got[7]]
    dpba, dal, ddt = dn_gates_bwd(sv['pba'], al_row, dt_row, dgates, H, name="dn_gates_bwd_0")
    grads['dn_a_log'] = dal[:, H:2 * H]
    grads['dn_dt_bias'] = ddt[:, H:2 * H]
    (dproj, grads['dn_conv_w']), got = dn_prep_bwd(sv['proj'], conv_full, dqkv, dz, name="dn_prep_bwd_0",
                                                   comm=Comm("exchange", [*ffn_dw[0, 1], ddn_wout]))
    l0b, slots['dn_w_out'] = got[0:3], [got[3]]
    dw_main = tn_mm(sv['hM'], dproj, name="dn_win_dw_0")
    dw_ba = tn_mm(sv['hM'], dpba, name="dn_wba_dw_0", tn=HEAD)
    ddn_win = split_columns(dw_main, dw_ba, c8, name="dn_w_in_split")
    (dcur, dng[0][2]), got = mm_bwd_dx(dcur, sv['x1'], gvec(0, 2), dproj, dn_wmain, dpba, dn_wba, name="dn_in_bwd_0",
                                       comm=Comm("exchange", [ddn_win]))
    slots['dn_w_in'] = [got[0]]
    small_names = ['norm_g', 'dn_conv_w', 'sg_b_in', 'sg_ln_g', 'sg_ln_b', 'sg_w_s', 'sg_b_s', 'dn_a_log',
                   'dn_dt_bias', 'dn_norm_g']
    small = {}

    def gather_small():
        dng_full = jnp.stack([jnp.concatenate(r, axis=0) for r in dng], axis=0)
        small['parts'] = [dng_full, grads['dn_conv_w'], grads['sg_b_in'], grads['sg_ln_g'], grads['sg_ln_b'],
                          grads['sg_w_s'], grads['sg_b_s'], grads['dn_a_log'], grads['dn_dt_bias'], grads['dn_norm_g']]
        pack, small['offs'] = _pack_rows(small['parts'])
        return Comm("gather", [pack])

    (dcur, da, db, dy, dng[0][0], dng[0][1]), _ = ffn_bwd_dx(
        dcur, sv['x0'], sv['yA'], sv['pA'], sv['qA'], gvec(0, 0), gvec(0, 1), *ffn_w[0, 0], name="ffn_bwd_0a")
    grad_x = dcur[None]
    (dg,), (small_slots,) = ffn_bwd_dw_one(sv['hA'], da, False, name="ffn_dw_0a_gate", comm=gather_small())
    (du,), (xg,) = ffn_bwd_dw_one(sv['hA'], db, False, name="ffn_dw_0a_up", comm=Comm("exchange", [dg]))
    (dd,), (xu,) = ffn_bwd_dw_one(dy, sv['tA'], True, name="ffn_dw_0a_down", comm=Comm("exchange", [du]))
    small_parts, offs = small['parts'], small['offs']
    l0a = [xg, xu, exchange_slots([dd], name="exchange_last")[0]]
    for i, nm in enumerate(['ffn_w_gate', 'ffn_w_up', 'ffn_w_down']):
        slots[nm] = [l0a[i], l0b[i], l1a[i], l1b[i]]
    big_names = ['ffn_w_gate', 'ffn_w_up', 'ffn_w_down', 'dn_w_in', 'dn_w_out', 'sg_w_in', 'sg_w_out']
    slots = [slots[nm] for nm in big_names]
    small_sum = sum_slots(small_slots, name="sum_small_grads")

    def small_grad(i):
        r0, n = offs[i]
        p = small_parts[i]
        return small_sum[r0:r0 + n].reshape(-1)[:p.size].reshape(p.shape)

    def my_shard(full, axis, like):
        n = full.shape[axis] // N_DEV
        return lax.dynamic_slice_in_dim(full, me * n, n, axis).reshape(like.shape)

    g_small = {
        'norm_g': my_shard(small_grad(0), 2, norm_g),
        'dn_conv_w': my_shard(small_grad(1), 1, dn_conv_w),
        'sg_b_in': my_shard(small_grad(2), 1, sg_b_in),
        'sg_ln_g': my_shard(small_grad(3), 1, sg_ln_g),
        'sg_ln_b': my_shard(small_grad(4), 1, sg_ln_b),
        'sg_w_s': small_grad(5).reshape(sg_w_s.shape),
        'sg_b_s': small_grad(6).reshape(sg_b_s.shape),
        'dn_a_log': small_grad(7).reshape(dn_a_log.shape),
        'dn_dt_bias': small_grad(8).reshape(dn_dt_bias.shape),
        'dn_norm_g': small_grad(9).reshape(dn_norm_g.shape),
    }

    out_g, out_d, out_m, out_v = {}, {}, {}, {}
    for nm, r in zip(big_names, slots):
        w = weights[nm]
        cols = w.shape[-1]
        rows = w.size // cols
        tr = {'ffn_w_gate': 512, 'ffn_w_up': 512, 'ffn_w_down': F8 // 2, 'dn_w_in': 256, 'sg_w_in': 256}.get(nm, rows)
        pieces = [p.reshape(N_DEV, -1, cols) for p in r]
        g, d, m2, v2 = adam_slots(w.reshape(rows, cols), pieces, mom_m[nm].reshape(rows, cols),
                                  mom_v[nm].reshape(rows, cols), name=f"adam_{nm}", tr=tr)
        out_g[nm], out_d[nm], out_m[nm], out_v[nm] = (t.reshape(w.shape) for t in (g, d, m2, v2))
    for nm in small_names:
        w = weights[nm]
        cols = w.shape[-1]
        rows = w.size // cols
        two = lambda t: t.reshape(rows, cols)
        d, m2, v2 = adam_small(two(w), two(g_small[nm]), two(mom_m[nm]), two(mom_v[nm]), name=f"adam_{nm}")
        out_g[nm] = g_small[nm]
        out_d[nm], out_m[nm], out_v[nm] = (t.reshape(w.shape) for t in (d, m2, v2))

    return (loss, grad_x, *[out_g[n] for n in order], *[out_d[n] for n in order], *[out_m[n] for n in order],
            *[out_v[n] for n in order])
```

```python
import functools
import math

import jax
import jax.numpy as jnp
from jax import lax
from jax.experimental import pallas as pl
from jax.experimental.pallas import tpu as pltpu

f32 = jnp.float32
MXU_DTYPE = jnp.bfloat16
N_DEV = 8
RMS_EPS = 1e-6
LN_EPS = 1e-5
L2_EPS = 1e-6
HEAD = 128
DN_CHUNK = 64
SG_CHUNK = 128
SG_GROUPS = 8
CONV_K = 4
ADAM_LR, ADAM_B1, ADAM_B2, ADAM_EPS, ADAM_WD, ADAM_STEP = 0.001, 0.9, 0.999, 1e-08, 0.01, 10
VMEM_LIMIT = 56 * 1024 * 1024
FFN_ROWS_FWD, FFN_ROWS_BWD, FFN_ROWS_DW = 1024, 512, 2048
PROJ_ROWS, TN_ROWS = 1024, 2048
FFN_SLABS = 4
SDS = jax.ShapeDtypeStruct
MESH = pl.DeviceIdType.MESH


def _params(n_grid):
    return pltpu.CompilerParams(dimension_semantics=("arbitrary",) * n_grid, vmem_limit_bytes=VMEM_LIMIT)


def _row_tile(s, want):
    t = min(s, want)
    assert s % t == 0, (s, t)
    return t


def _rms(x, g):
    return x * lax.rsqrt(jnp.mean(x * x, axis=-1, keepdims=True) + RMS_EPS) * g


def _rms_bwd(x, g, dy):
    _, vjp = jax.vjp(_rms, x, g)
    return vjp(dy)


def _silu(a):
    return a * jax.nn.sigmoid(a)


def _mm(a, b):
    return lax.dot_general(a, b, (((1,), (0,)), ((), ())), preferred_element_type=f32)


def _mm_nt(a, b):
    return lax.dot_general(a, b, (((1,), (1,)), ((), ())), preferred_element_type=f32)


def _mm_tn(a, b):
    return lax.dot_general(a, b, (((0,), (0,)), ((), ())), preferred_element_type=f32)


def _c(x):
    return x.astype(MXU_DTYPE)


def _split(a):
    hi = a.astype(MXU_DTYPE)
    lo = (a - hi.astype(f32)).astype(MXU_DTYPE)
    return hi, lo


def _dot3(a, b, dims):
    ah, al = _split(a)
    bh, bl = _split(b)
    d = lambda p, q: lax.dot_general(p, q, (dims, ((), ())), preferred_element_type=f32)
    return d(ah, bh) + (d(ah, bl) + d(al, bh))


NN, NT, TN = ((1,), (0,)), ((1,), (1,)), ((0,), (0,))


def _slot(px, py, pc):
    return 4 * px + 2 * py + pc


def all_gather_multi(arrs, name):
    return Comm("gather", arrs).alone(name)


def exchange_slots(arrs, name):
    return Comm("exchange", arrs).alone(name)


class Comm:
    def __init__(self, kind, arrs):
        self.kind, self.arrs, self.n = kind, list(arrs), len(arrs)
        hbm = pl.BlockSpec(memory_space=pltpu.HBM)
        self.in_specs = [hbm] * self.n
        self.out_specs = [hbm] * self.n
        lead = (N_DEV,) if kind == "gather" else ()
        self.out_shape = [SDS(lead + tuple(a.shape), a.dtype) for a in self.arrs]
        self.scratch = [pltpu.SemaphoreType.DMA((self.n, 7)), pltpu.SemaphoreType.DMA((self.n, 7)),
                        pltpu.SemaphoreType.DMA((self.n,))]

    def phase(self, p, ins, outs, sems):
        (self._gather if self.kind == "gather" else self._exchange)(p, ins, outs, sems)

    def _gather(self, p, ins, outs, sems):
        send_sems, recv_sems, local_sems = sems
        x, y, c = lax.axis_index("x"), lax.axis_index("y"), lax.axis_index("c")
        me, sibling = (x, y, c), (x, y, 1 - c)
        chips = [(1 - x, y), (x, 1 - y), (1 - x, 1 - y)]

        def copy(a, k, block, to, src=None):
            dst = outs[a].at[_slot(*block)]
            return pltpu.make_async_remote_copy(
                src_ref=dst if src is None else src, dst_ref=dst, send_sem=send_sems.at[a, k],
                recv_sem=recv_sems.at[a, k], device_id=to, device_id_type=MESH)

        mine = [pltpu.make_async_copy(ins[a], outs[a].at[_slot(*me)], local_sems.at[a]) for a in range(self.n)]
        first = [[copy(a, 0, me, sibling, src=ins[a])] +
                 [copy(a, 1 + j, me, (*chip, c), src=ins[a]) for j, chip in enumerate(chips)] for a in range(self.n)]
        passed = [[copy(a, 4 + j, (*chip, c), sibling) for j, chip in enumerate(chips)] for a in range(self.n)]
        if p == 0:
            for a in range(self.n):
                mine[a].start()
            for a in range(self.n):
                for cp in first[a]:
                    cp.start()
        elif p == 1:
            for a in range(self.n):
                for j, chip in enumerate(chips):
                    copy(a, 1 + j, (*chip, c), me).wait_recv()
                    passed[a][j].start()
        else:
            for a in range(self.n):
                copy(a, 0, sibling, me).wait_recv()
                for j, chip in enumerate(chips):
                    copy(a, 4 + j, (*chip, 1 - c), me).wait_recv()
            for a in range(self.n):
                for cp in first[a] + passed[a]:
                    cp.wait_send()
                mine[a].wait()

    def _exchange(self, p, ins, outs, sems):
        send_sems, recv_sems, local_sems = sems
        x, y, c = lax.axis_index("x"), lax.axis_index("y"), lax.axis_index("c")
        me = _slot(x, y, c)
        peers = [(x ^ (k >> 2), y ^ ((k >> 1) & 1), c ^ (k & 1)) for k in range(1, N_DEV)]

        def copy(a, k):
            peer = peers[k - 1]
            return pltpu.make_async_remote_copy(
                src_ref=ins[a].at[_slot(*peer)], dst_ref=outs[a].at[me], send_sem=send_sems.at[a, k - 1],
                recv_sem=recv_sems.at[a, k - 1], device_id=peer, device_id_type=MESH)

        def landed(a, k):
            peer = peers[k - 1]
            return pltpu.make_async_remote_copy(
                src_ref=ins[a].at[me], dst_ref=outs[a].at[_slot(*peer)], send_sem=send_sems.at[a, k - 1],
                recv_sem=recv_sems.at[a, k - 1], device_id=peer, device_id_type=MESH)

        local = [pltpu.make_async_copy(ins[a].at[me], outs[a].at[me], local_sems.at[a]) for a in range(self.n)]
        order = [6, 7, 2, 3, 4, 5, 1]
        if p == 0:
            for a in range(self.n):
                local[a].start()
            for a in range(self.n):
                for k in order:
                    copy(a, k).start()
        elif p == 2:
            for a in range(self.n):
                for k in order:
                    copy(a, k).wait_send()
                    landed(a, k).wait_recv()
                local[a].wait()

    def alone(self, name):
        n = self.n

        def body(*refs):
            for p in range(3):
                self.phase(p, refs[:n], refs[n:2 * n], refs[2 * n:])

        return pl.pallas_call(body, name=name, out_shape=tuple(self.out_shape), in_specs=self.in_specs,
                              out_specs=tuple(self.out_specs), scratch_shapes=self.scratch)(*self.arrs)


def hosted_call(body, comm, steps, *, name, grid, in_specs, out_specs, out_shape, scratch_shapes, args):
    if comm is None:
        outs = pl.pallas_call(body, name=name, grid=grid, in_specs=in_specs, out_specs=tuple(out_specs),
                              out_shape=tuple(out_shape), scratch_shapes=scratch_shapes,
                              compiler_params=_params(len(grid)))(*args)
        return outs, None
    ni, no, ns, cn = len(in_specs), len(out_specs), len(scratch_shapes), comm.n

    def both(*refs):
        h_in, c_in = refs[:ni], refs[ni:ni + cn]
        h_out, c_out = refs[ni + cn:ni + cn + no], refs[ni + cn + no:ni + 2 * cn + no]
        h_scr, c_scr = refs[ni + 2 * cn + no:ni + 2 * cn + no + ns], refs[ni + 2 * cn + no + ns:]
        when = steps()
        pl.when(when[0])(lambda: comm.phase(0, c_in, c_out, c_scr))
        body(*h_in, *h_out, *h_scr)
        pl.when(when[1])(lambda: comm.phase(1, c_in, c_out, c_scr))
        pl.when(when[2])(lambda: comm.phase(2, c_in, c_out, c_scr))

    outs = pl.pallas_call(
        both, name=name, grid=grid, in_specs=list(in_specs) + comm.in_specs,
        out_specs=tuple(out_specs) + tuple(comm.out_specs), out_shape=tuple(out_shape) + tuple(comm.out_shape),
        scratch_shapes=list(scratch_shapes) + comm.scratch, compiler_params=_params(len(grid)),
    )(*args, *comm.arrs)
    return outs[:no], outs[no:]


def _grid_steps(n_outer, n_inner=1):
    total = n_outer * n_inner

    def steps():
        t = pl.program_id(0) * n_inner + (pl.program_id(1) if n_inner > 1 else 0)
        return t == 0, t == (total * 5) // 8, t == total - 1
    return steps


def widen_slabs(arrs, ns, name):
    per = N_DEV // ns
    _, R, C = arrs[0].shape
    n = len(arrs)

    def body(*refs):
        for a in range(n):
            for k in range(per):
                refs[n + a][:, k * C:(k + 1) * C] = refs[a][k]

    return pl.pallas_call(
        body, name=name, grid=(ns,), in_specs=[pl.BlockSpec((per, R, C), lambda s: (s, 0, 0))] * n,
        out_specs=tuple(pl.BlockSpec((None, R, per * C), lambda s: (s, 0, 0)) for _ in range(n)),
        out_shape=tuple(SDS((ns, R, per * C), a.dtype) for a in arrs), compiler_params=_params(1))(*arrs)


def join_columns(blocks, n_main, name):
    nb, R, c8 = blocks.shape
    rest = nb * c8 - n_main
    tr = _row_tile(R, 256)

    def body(b_ref, main_ref, rest_ref, full):
        for k in range(nb):
            full[:, k * c8:(k + 1) * c8] = b_ref[k]
        main_ref[...] = full[:, :n_main]
        rest_ref[...] = jnp.zeros_like(rest_ref)
        rest_ref[:, :rest] = full[:, n_main:]

    return pl.pallas_call(
        body, name=name, grid=(R // tr,), in_specs=[pl.BlockSpec((nb, tr, c8), lambda i: (0, i, 0))],
        out_specs=(pl.BlockSpec((tr, n_main), lambda i: (i, 0)), pl.BlockSpec((tr, HEAD), lambda i: (i, 0))),
        out_shape=(SDS((R, n_main), blocks.dtype), SDS((R, HEAD), blocks.dtype)),
        scratch_shapes=[pltpu.VMEM((tr, nb * c8), blocks.dtype)], compiler_params=_params(1))(blocks)


def split_columns(main, rest, c8, name):
    R, n_main = main.shape
    nb = N_DEV
    n_rest = nb * c8 - n_main
    tr = _row_tile(R, 256)

    def body(main_ref, rest_ref, b_ref, full):
        full[:, :n_main] = main_ref[...]
        full[:, n_main:] = rest_ref[:, :n_rest]
        for k in range(nb):
            b_ref[k] = full[:, k * c8:(k + 1) * c8]

    return pl.pallas_call(
        body, name=name, grid=(R // tr,),
        in_specs=[pl.BlockSpec((tr, n_main), lambda i: (i, 0)), pl.BlockSpec((tr, HEAD), lambda i: (i, 0))],
        out_specs=pl.BlockSpec((nb, tr, c8), lambda i: (0, i, 0)), out_shape=SDS((nb, R, c8), main.dtype),
        scratch_shapes=[pltpu.VMEM((tr, nb * c8), main.dtype)], compiler_params=_params(1))(main, rest)


def ffn_fwd(x, gpre, gpost, wg, wu, wd, name, comm=None):
    S, D = x.shape
    nj, F8 = wg.shape[0], wg.shape[-1]
    tm = _row_tile(S, FFN_ROWS_FWD)

    def body(x_ref, gpre_ref, gpost_ref, wg_ref, wu_ref, wd_ref, xo_ref, h_ref, p_ref, q_ref, t_ref, y_ref):
        j = pl.program_id(1)

        @pl.when(j == 0)
        def _():
            h_ref[...] = _rms(x_ref[...], gpre_ref[...]).astype(h_ref.dtype)
            y_ref[...] = jnp.zeros_like(y_ref)

        h = h_ref[...]
        a = _mm(h, wg_ref[...])
        b = _mm(h, wu_ref[...])
        s = jax.nn.sigmoid(a)
        q = a * s
        p_ref[...] = (b * (s + q * (1.0 - s))).astype(p_ref.dtype)
        q_ref[...] = q.astype(q_ref.dtype)
        t = (q * b).astype(t_ref.dtype)
        t_ref[...] = t
        y_ref[...] += _mm(t, wd_ref[...])

        @pl.when(j == nj - 1)
        def _():
            xo_ref[...] = x_ref[...] + 0.5 * _rms(y_ref[...], gpost_ref[...])

    row = pl.BlockSpec((tm, D), lambda i, j: (i, 0))
    vec = pl.BlockSpec((1, D), lambda i, j: (0, 0))
    wcol = pl.BlockSpec((None, D, F8), lambda i, j: (j, 0, 0))
    wrow = pl.BlockSpec((None, F8, D), lambda i, j: (j, 0, 0))
    hid = pl.BlockSpec((None, tm, F8), lambda i, j: (j, i, 0))
    return hosted_call(
        body, comm, _grid_steps(S // tm, nj), name=name, grid=(S // tm, nj),
        in_specs=[row, vec, vec, wcol, wcol, wrow],
        out_specs=(row, row, hid, hid, hid, row),
        out_shape=(SDS((S, D), f32), SDS((S, D), MXU_DTYPE), SDS((nj, S, F8), MXU_DTYPE),
                   SDS((nj, S, F8), MXU_DTYPE), SDS((nj, S, F8), MXU_DTYPE), SDS((S, D), f32)),
        scratch_shapes=[], args=(x, gpre, gpost, wg, wu, wd))


def ffn_bwd_dx(dxo, x, y, p, q, gpre, gpost, wg, wu, wd, name, comm=None):
    S, D = x.shape
    NS, F8 = wg.shape[0], wg.shape[-1]
    sps = 2 if NS % 2 == 0 else 1
    nj = NS // sps
    tm = _row_tile(S, FFN_ROWS_BWD)

    def body(dxo_ref, x_ref, y_ref, p_ref, q_ref, gpre_ref, gpost_ref, wg_ref, wu_ref, wd_ref,
             dx_ref, da_ref, db_ref, dy_ref, dgpre_ref, dgpost_ref, dh_ref):
        i, j = pl.program_id(0), pl.program_id(1)

        @pl.when(j == 0)
        def _():
            @pl.when(i == 0)
            def _():
                dgpre_ref[...] = jnp.zeros_like(dgpre_ref)
                dgpost_ref[...] = jnp.zeros_like(dgpost_ref)

            dy, dg = _rms_bwd(y_ref[...], gpost_ref[...], 0.5 * dxo_ref[...])
            dy_ref[...] = dy.astype(dy_ref.dtype)
            dgpost_ref[...] += dg
            dh_ref[...] = jnp.zeros_like(dh_ref)

        half = tm // 2 if tm % 16 == 0 else tm
        for r0 in range(0, tm, half):
            rs = slice(r0, r0 + half)
            upd = None
            for s in range(sps):
                dt = _mm_nt(dy_ref[rs, :], wd_ref[s])
                da = (dt * p_ref[s, rs, :].astype(f32)).astype(da_ref.dtype)
                db = (dt * q_ref[s, rs, :].astype(f32)).astype(db_ref.dtype)
                da_ref[s, rs, :] = da
                db_ref[s, rs, :] = db
                part = _mm_nt(da, wg_ref[s]) + _mm_nt(db, wu_ref[s])
                upd = part if upd is None else upd + part
            dh_ref[rs, :] += upd

        @pl.when(j == nj - 1)
        def _():
            dxx, dg = _rms_bwd(x_ref[...], gpre_ref[...], dh_ref[...])
            dx_ref[...] = dxo_ref[...] + dxx
            dgpre_ref[...] += dg

    row = pl.BlockSpec((tm, D), lambda i, j: (i, 0))
    vec = pl.BlockSpec((1, D), lambda i, j: (0, 0))
    wcol = pl.BlockSpec((sps, D, F8), lambda i, j: (j, 0, 0))
    wrow = pl.BlockSpec((sps, F8, D), lambda i, j: (j, 0, 0))
    hid = pl.BlockSpec((sps, tm, F8), lambda i, j: (j, i, 0))
    return hosted_call(
        body, comm, _grid_steps(S // tm, nj), name=name, grid=(S // tm, nj),
        in_specs=[row, row, row, hid, hid, vec, vec, wcol, wcol, wrow],
        out_specs=(row, hid, hid, row, vec, vec),
        out_shape=(SDS((S, D), f32), SDS((NS, S, F8), MXU_DTYPE), SDS((NS, S, F8), MXU_DTYPE),
                   SDS((S, D), MXU_DTYPE), SDS((1, D), f32), SDS((1, D), f32)),
        scratch_shapes=[pltpu.VMEM((tm, D), f32)], args=(dxo, x, y, p, q, gpre, gpost, wg, wu, wd))


def ffn_bwd_dw(h, dy, t, da, db, name, comm=None):
    S, D = h.shape
    NS, F8 = t.shape[0], t.shape[-1]
    per = N_DEV // NS
    w8 = F8 // per
    tm = _row_tile(S, FFN_ROWS_DW)
    ni = S // tm

    def body(h_ref, dy_ref, t_ref, da_ref, db_ref, dwg_ref, dwu_ref, dwd_ref, accg, accu, accd):
        i = pl.program_id(1)

        @pl.when(i == 0)
        def _():
            accg[...] = jnp.zeros_like(accg)
            accu[...] = jnp.zeros_like(accu)
            accd[...] = jnp.zeros_like(accd)

        hh = h_ref[...]
        accg[...] += _mm_tn(hh, da_ref[...])
        accu[...] += _mm_tn(hh, db_ref[...])
        accd[...] += _mm_tn(t_ref[...], dy_ref[...])

        @pl.when(i == ni - 1)
        def _():
            for k in range(per):
                ks = slice(k * w8, (k + 1) * w8)
                dwg_ref[k] = accg[:, ks].astype(dwg_ref.dtype)
                dwu_ref[k] = accu[:, ks].astype(dwu_ref.dtype)
                dwd_ref[k] = accd[ks, :].astype(dwd_ref.dtype)

    row = pl.BlockSpec((tm, D), lambda j, i: (i, 0))
    hid = pl.BlockSpec((None, tm, F8), lambda j, i: (j, i, 0))
    wcol = pl.BlockSpec((per, D, w8), lambda j, i: (j, 0, 0))
    wrow = pl.BlockSpec((per, w8, D), lambda j, i: (j, 0, 0))
    return hosted_call(
        body, comm, _grid_steps(NS, ni), name=name, grid=(NS, ni),
        in_specs=[row, row, hid, hid, hid],
        out_specs=(wcol, wcol, wrow),
        out_shape=(SDS((N_DEV, D, w8), MXU_DTYPE), SDS((N_DEV, D, w8), MXU_DTYPE), SDS((N_DEV, w8, D), MXU_DTYPE)),
        scratch_shapes=[pltpu.VMEM((D, F8), f32), pltpu.VMEM((D, F8), f32), pltpu.VMEM((F8, D), f32)],
        args=(h, dy, t, da, db))


def ffn_bwd_dw_one(rows_op, slab_op, hidden_rows, name, comm=None):
    S, D = rows_op.shape
    NS, F8 = slab_op.shape[0], slab_op.shape[-1]
    per = N_DEV // NS
    w8 = F8 // per
    tm = _row_tile(S, FFN_ROWS_DW)
    ni = S // tm

    def body(r_ref, s_ref, o_ref, acc):
        i = pl.program_id(1)

        @pl.when(i == 0)
        def _():
            acc[...] = jnp.zeros_like(acc)

        acc[...] += _mm_tn(s_ref[...], r_ref[...]) if hidden_rows else _mm_tn(r_ref[...], s_ref[...])

        @pl.when(i == ni - 1)
        def _():
            for k in range(per):
                ks = slice(k * w8, (k + 1) * w8)
                o_ref[k] = (acc[ks, :] if hidden_rows else acc[:, ks]).astype(o_ref.dtype)

    blk = (per, w8, D) if hidden_rows else (per, D, w8)
    return hosted_call(
        body, comm, _grid_steps(NS, ni), name=name, grid=(NS, ni),
        in_specs=[pl.BlockSpec((tm, D), lambda j, i: (i, 0)), pl.BlockSpec((None, tm, F8), lambda j, i: (j, i, 0))],
        out_specs=(pl.BlockSpec(blk, lambda j, i: (j, 0, 0)),),
        out_shape=(SDS((N_DEV,) + blk[1:], MXU_DTYPE),),
        scratch_shapes=[pltpu.VMEM((F8, D) if hidden_rows else (D, F8), f32)], args=(rows_op, slab_op))


def rms_mm(x, g, w, w2, name, tn=1024):
    S, D = x.shape
    N = w.shape[1]
    tm = _row_tile(S, PROJ_ROWS)
    tn = _row_tile(N, tn)
    has2 = w2 is not None

    def body(*refs):
        if has2:
            x_ref, g_ref, w_ref, w2_ref, h_ref, o_ref, o2_ref = refs
        else:
            x_ref, g_ref, w_ref, h_ref, o_ref = refs
        j = pl.program_id(1)

        @pl.when(j == 0)
        def _():
            h = _rms(x_ref[...], g_ref[...]).astype(h_ref.dtype)
            h_ref[...] = h
            if has2:
                o2_ref[...] = _mm(h, w2_ref[...])

        o_ref[...] = _mm(h_ref[...], w_ref[...])

    row = pl.BlockSpec((tm, D), lambda i, j: (i, 0))
    in_specs = [row, pl.BlockSpec((1, D), lambda i, j: (0, 0)), pl.BlockSpec((D, tn), lambda i, j: (0, j))]
    out_specs = [row, pl.BlockSpec((tm, tn), lambda i, j: (i, j))]
    out_shape = [SDS((S, D), MXU_DTYPE), SDS((S, N), f32)]
    args = [x, g, w]
    if has2:
        in_specs.append(pl.BlockSpec((D, w2.shape[1]), lambda i, j: (0, 0)))
        out_specs.append(pl.BlockSpec((tm, w2.shape[1]), lambda i, j: (i, 0)))
        out_shape.append(SDS((S, w2.shape[1]), f32))
        args.append(w2)
    return pl.pallas_call(
        body, name=name, grid=(S // tm, N // tn), in_specs=in_specs, out_specs=tuple(out_specs),
        out_shape=tuple(out_shape), compiler_params=_params(2),
    )(*args)


def mm_bwd_dx(dres, x, g, dy, w, dy2, w2, name, tk=1024, comm=None):
    S, D = x.shape
    K = dy.shape[1]
    tm = _row_tile(S, PROJ_ROWS)
    tk = _row_tile(K, tk)
    nk = K // tk
    has2 = dy2 is not None

    def body(*refs):
        if has2:
            dres_ref, x_ref, g_ref, dy_ref, w_ref, dy2_ref, w2_ref, dx_ref, dg_ref, dh_ref = refs
        else:
            dres_ref, x_ref, g_ref, dy_ref, w_ref, dx_ref, dg_ref, dh_ref = refs
        i, k = pl.program_id(0), pl.program_id(1)

        @pl.when(k == 0)
        def _():
            @pl.when(i == 0)
            def _():
                dg_ref[...] = jnp.zeros_like(dg_ref)

            if has2:
                dh_ref[...] = _mm_nt(dy2_ref[...], w2_ref[...])
            else:
                dh_ref[...] = jnp.zeros_like(dh_ref)

        dh_ref[...] += _mm_nt(dy_ref[...], w_ref[...])

        @pl.when(k == nk - 1)
        def _():
            dxx, dg = _rms_bwd(x_ref[...], g_ref[...], dh_ref[...])
            dx_ref[...] = dres_ref[...] + dxx
            dg_ref[...] += dg

    row = pl.BlockSpec((tm, D), lambda i, k: (i, 0))
    vec = pl.BlockSpec((1, D), lambda i, k: (0, 0))
    in_specs = [row, row, vec, pl.BlockSpec((tm, tk), lambda i, k: (i, k)), pl.BlockSpec((D, tk), lambda i, k: (0, k))]
    args = [dres, x, g, dy, w]
    if has2:
        in_specs += [pl.BlockSpec((tm, dy2.shape[1]), lambda i, k: (i, 0)),
                     pl.BlockSpec((D, w2.shape[1]), lambda i, k: (0, 0))]
        args += [dy2, w2]
    return hosted_call(
        body, comm, _grid_steps(S // tm, nk), name=name, grid=(S // tm, nk), in_specs=in_specs, out_specs=(row, vec),
        out_shape=(SDS((S, D), f32), SDS((1, D), f32)), scratch_shapes=[pltpu.VMEM((tm, D), f32)], args=args)


def tn_mm(a, b, name, tn=512, slot_major=False):
    S, K1 = a.shape
    N = b.shape[1]
    tm = _row_tile(S, TN_ROWS)
    tn = _row_tile(N, tn)
    ni = S // tm

    def body(a_ref, b_ref, o_ref, acc):
        i = pl.program_id(1)

        @pl.when(i == 0)
        def _():
            acc[...] = jnp.zeros_like(acc)

        acc[...] += _mm_tn(a_ref[...], b_ref[...])

        @pl.when(i == ni - 1)
        def _():
            o_ref[...] = acc[...].astype(o_ref.dtype)

    if slot_major:
        out_spec, out_shape = pl.BlockSpec((None, K1, tn), lambda j, i: (j, 0, 0)), SDS((N // tn, K1, tn), MXU_DTYPE)
    else:
        out_spec, out_shape = pl.BlockSpec((K1, tn), lambda j, i: (0, j)), SDS((K1, N), MXU_DTYPE)
    return pl.pallas_call(
        body, name=name, grid=(N // tn, ni),
        in_specs=[pl.BlockSpec((tm, K1), lambda j, i: (i, 0)), pl.BlockSpec((tm, tn), lambda j, i: (i, j))],
        out_specs=out_spec, out_shape=out_shape,
        scratch_shapes=[pltpu.VMEM((K1, tn), f32)], compiler_params=_params(2),
    )(a, b)


CONV_ROWS = 512


def _shift_down(cur, prev8, s):
    r = pltpu.roll(cur, s, 0)
    row = lax.broadcasted_iota(jnp.int32, (8, cur.shape[1]), 0)
    top = jnp.where(row < s, pltpu.roll(prev8, s, 0), r[0:8])
    return jnp.concatenate([top, r[8:]], axis=0)


def _shift_up(cur, next8, s):
    n = cur.shape[0]
    r = pltpu.roll(cur, n - s, 0)
    row = lax.broadcasted_iota(jnp.int32, (8, cur.shape[1]), 0)
    bot = jnp.where(row >= 8 - s, pltpu.roll(next8, 8 - s, 0), r[n - 8:])
    return jnp.concatenate([r[:n - 8], bot], axis=0)


def _conv_taps(cur, prev8):
    return [_shift_down(cur, prev8, 3), _shift_down(cur, prev8, 2), _shift_down(cur, prev8, 1), cur]


def _act_qk(c):
    a = _silu(c)
    return a * lax.rsqrt(jnp.sum(a * a, axis=-1, keepdims=True) + L2_EPS)


def dn_prep(proj, conv_w, name):
    S = proj.shape[0]
    W = conv_w.shape[1] // 3
    nh = W // HEAD
    R = _row_tile(S, CONV_ROWS)

    def body(p_ref, w_ref, o_ref):
        j = pl.program_id(0)
        w = w_ref[...]

        def rows(r, prev8):
            cur = p_ref[pl.ds(r, R), :]
            taps = _conv_taps(cur, prev8)
            cv = taps[0] * w[0:1] + taps[1] * w[1:2] + taps[2] * w[2:3] + taps[3] * w[3:4]

            @pl.when(j < 2 * nh)
            def _():
                o_ref[pl.ds(r, R), :] = _act_qk(cv)

            @pl.when(j >= 2 * nh)
            def _():
                o_ref[pl.ds(r, R), :] = _silu(cv)

        rows(0, jnp.zeros((8, HEAD), f32))

        @pl.loop(1, S // R)
        def _(t):
            r = pl.multiple_of(t * R, R)
            rows(r, p_ref[pl.ds(r - 8, 8), :])

    return pl.pallas_call(
        body, name=name, grid=(3 * nh,),
        in_specs=[pl.BlockSpec((S, HEAD), lambda j: (0, j)), pl.BlockSpec((CONV_K, HEAD), lambda j: (0, j))],
        out_specs=pl.BlockSpec((None, S, HEAD), lambda j: (j // nh, 0, j % nh)),
        out_shape=SDS((3, S, W), f32), compiler_params=_params(1),
    )(proj, conv_w)


def dn_prep_bwd(proj, conv_w, dqkv, dz, name, comm=None):
    S = proj.shape[0]
    W = conv_w.shape[1] // 3
    nh = W // HEAD
    nq = 3 * nh
    R = _row_tile(S, CONV_ROWS)
    nr = S // R

    def body(p_ref, w_ref, dq_ref, dz_ref, dp_ref, dw_ref, dc_ref):
        j = pl.program_id(0)

        @pl.when(j >= nq)
        def _():
            dp_ref[...] = dz_ref[...].astype(dp_ref.dtype)

        @pl.when(j < nq)
        def _():
            w = w_ref[...]
            dw_ref[...] = jnp.zeros_like(dw_ref)

            def rows(r, prev8):
                cur = p_ref[pl.ds(r, R), :]
                taps = _conv_taps(cur, prev8)
                cv = taps[0] * w[0:1] + taps[1] * w[1:2] + taps[2] * w[2:3] + taps[3] * w[3:4]
                dn = dq_ref[pl.ds(r, R), :]

                @pl.when(j < 2 * nh)
                def _():
                    dc_ref[pl.ds(r, R), :] = jax.vjp(_act_qk, cv)[1](dn)[0]

                @pl.when(j >= 2 * nh)
                def _():
                    dc_ref[pl.ds(r, R), :] = jax.vjp(_silu, cv)[1](dn)[0]

                dc = dc_ref[pl.ds(r, R), :]
                dw_ref[...] += jnp.concatenate(
                    [jnp.sum(dc * taps[q], axis=0, keepdims=True) for q in range(CONV_K)], axis=0)

            rows(0, jnp.zeros((8, HEAD), f32))

            @pl.loop(1, nr)
            def _(t):
                r = pl.multiple_of(t * R, R)
                rows(r, p_ref[pl.ds(r - 8, 8), :])

            def back(r, next8):
                dc = dc_ref[pl.ds(r, R), :]
                dx = dc * w[3:4]
                for s in (1, 2, 3):
                    dx = dx + _shift_up(dc, next8, s) * w[3 - s:4 - s]
                dp_ref[pl.ds(r, R), :] = dx.astype(dp_ref.dtype)

            @pl.loop(0, nr - 1)
            def _(t):
                r = pl.multiple_of(t * R, R)
                back(r, dc_ref[pl.ds(r + R, 8), :])

            back((nr - 1) * R, jnp.zeros((8, HEAD), f32))

    clamp = lambda j: jnp.minimum(j, nq - 1)
    return hosted_call(
        body, comm, _grid_steps(4 * nh), name=name, grid=(4 * nh,),
        in_specs=[pl.BlockSpec((S, HEAD), lambda j: (0, clamp(j))),
                  pl.BlockSpec((CONV_K, HEAD), lambda j: (0, clamp(j))),
                  pl.BlockSpec((None, S, HEAD), lambda j: (clamp(j) // nh, 0, clamp(j) % nh)),
                  pl.BlockSpec((S, HEAD), lambda j: (0, jnp.maximum(j - nq, 0)))],
        out_specs=(pl.BlockSpec((S, HEAD), lambda j: (0, j)), pl.BlockSpec((CONV_K, HEAD), lambda j: (0, clamp(j)))),
        out_shape=(SDS((S, 4 * W), MXU_DTYPE), SDS((CONV_K, 3 * W), f32)),
        scratch_shapes=[pltpu.VMEM((S, HEAD), f32)], args=(proj, conv_w, dqkv, dz))


def _lane_pick(x, lane):
    sel = lax.broadcasted_iota(jnp.int32, x.shape, 1) == lane
    return jnp.broadcast_to(jnp.sum(jnp.where(sel, x, 0.0), axis=1, keepdims=True), x.shape)


CUM_ROWS = 256


def _sel_mm(m01, x):
    m = _c(m01)
    d = lambda p: lax.dot_general(m, p, (NN, ((), ())), preferred_element_type=f32)
    h1, h2, h3 = _pieces3(x)
    return (d(h1) + d(h2)) + d(h3)


def _chunk_cumsum_matrix(n, transpose):
    r, c = lax.broadcasted_iota(jnp.int32, (n, n), 0), lax.broadcasted_iota(jnp.int32, (n, n), 1)
    sh = int(math.log2(DN_CHUNK))
    same = (r >> sh) == (c >> sh)
    return jnp.where(same & ((r <= c) if transpose else (r >= c)), 1.0, 0.0).astype(f32)


def _gates_by_lane(H, p, al, dt):
    lane = lax.broadcasted_iota(jnp.int32, p.shape, 1)
    g = -jnp.exp(al) * jax.nn.softplus(p + dt)
    return jnp.where(lane < H, jax.nn.sigmoid(p), jnp.where(lane < 2 * H, g, 0.0))


def dn_gates(pba, al, dt, H, name):
    S = pba.shape[0]
    R = _row_tile(S, CUM_ROWS)

    def body(p_ref, al_ref, dt_ref, o_ref):
        raw = _gates_by_lane(H, p_ref[...], al_ref[...], dt_ref[...])
        lane = lax.broadcasted_iota(jnp.int32, raw.shape, 1)
        o_ref[...] = jnp.where(lane < H, raw, _sel_mm(_chunk_cumsum_matrix(R, False), raw))

    blk = pl.BlockSpec((R, HEAD), lambda i: (i, 0))
    par = pl.BlockSpec((1, HEAD), lambda i: (0, 0))
    return pl.pallas_call(body, name=name, grid=(S // R,), in_specs=[blk, par, par], out_specs=blk,
                          out_shape=SDS((S, HEAD), f32), compiler_params=_params(1))(pba, al, dt)


def dn_gates_bwd(pba, al, dt, dgates, H, name):
    S = pba.shape[0]
    R = _row_tile(S, CUM_ROWS)

    def body(p_ref, al_ref, dt_ref, dg_ref, dp_ref, dal_ref, ddt_ref):
        @pl.when(pl.program_id(0) == 0)
        def _():
            dal_ref[...] = jnp.zeros_like(dal_ref)
            ddt_ref[...] = jnp.zeros_like(ddt_ref)

        d = dg_ref[...]
        lane = lax.broadcasted_iota(jnp.int32, d.shape, 1)
        d = jnp.where(lane < H, d, _sel_mm(_chunk_cumsum_matrix(R, True), d))
        _, vjp = jax.vjp(functools.partial(_gates_by_lane, H), p_ref[...], al_ref[...], dt_ref[...])
        dp, dal, ddt = vjp(d)
        dp_ref[...] = dp.astype(dp_ref.dtype)
        dal_ref[...] += dal
        ddt_ref[...] += ddt

    blk = pl.BlockSpec((R, HEAD), lambda i: (i, 0))
    par = pl.BlockSpec((1, HEAD), lambda i: (0, 0))
    return pl.pallas_call(
        body, name=name, grid=(S // R,), in_specs=[blk, par, par, blk], out_specs=(blk, par, par),
        out_shape=(SDS((S, HEAD), MXU_DTYPE), SDS((1, HEAD), f32), SDS((1, HEAD), f32)), compiler_params=_params(1),
    )(pba, al, dt, dgates)


def _bdot(dims):
    back = {NN: ((NT, 'gb'), (TN, 'ag')), NT: ((NN, 'gb'), (TN, 'ga')), TN: ((NT, 'bg'), (NN, 'ag'))}[dims]
    d = lambda p, q, dm: lax.dot_general(_c(p), _c(q), (dm, ((), ())), preferred_element_type=f32)

    @jax.custom_vjp
    def f(a, b):
        return d(a, b, dims)

    def fwd(a, b):
        return d(a, b, dims), (a, b)

    def bwd(res, g):
        v = {'a': res[0], 'b': res[1], 'g': g}
        (da_dims, da_ops), (db_dims, db_ops) = back
        return d(v[da_ops[0]], v[da_ops[1]], da_dims), d(v[db_ops[0]], v[db_ops[1]], db_dims)

    f.defvjp(fwd, bwd)
    return f, lambda a, b: d(a, b, dims)


_BDOT = {dims: _bdot(dims) for dims in (NN, NT, TN)}


def _tri_inv_multi(Ls):
    n = Ls[0].shape[0]
    eye = jnp.where(lax.broadcasted_iota(jnp.int32, (n, n), 0) == lax.broadcasted_iota(jnp.int32, (n, n), 1), 1.0, 0.0)
    P = tuple(-L for L in Ls)
    T = tuple(eye + p for p in P)
    for _ in range(int(math.log2(n)) - 1):
        P = tuple(_dot3(p, p, NN) for p in P)
        T = tuple(t + _dot3(t, p, NN) for t, p in zip(T, P))
    return T


@jax.custom_vjp
def _tri_inv_kept(Ls, Ts):
    return Ts


def _tri_inv_kept_bwd(T, dT):
    X = tuple(_dot3(d, t, NT) for d, t in zip(dT, T))
    return tuple(-_dot3(t, x, TN) for t, x in zip(T, X)), tuple(jnp.zeros_like(t) for t in T)


_tri_inv_kept.defvjp(lambda Ls, Ts: (Ts, Ts), _tri_inv_kept_bwd)


def _pieces3(x):
    h1 = x.astype(MXU_DTYPE)
    r1 = x - h1.astype(f32)
    h2 = r1.astype(MXU_DTYPE)
    return h1, h2, (r1 - h2.astype(f32)).astype(MXU_DTYPE)


def _row_bcast_impl(sel_row, gc):
    s = _c(sel_row)
    d = lambda p: lax.dot_general(s, p, (NT, ((), ())), preferred_element_type=f32)
    h1, h2, h3 = _pieces3(gc)
    return (d(h1) + d(h2)) + d(h3)


def _row_bcast_bwd(sel_row, d):
    s = _c(sel_row)
    hi, lo = _split(d)
    t = lambda p: lax.dot_general(p, s, (TN, ((), ())), preferred_element_type=f32)
    return jnp.zeros_like(sel_row), t(hi) + t(lo)


_row_bcast = jax.custom_vjp(_row_bcast_impl)
_row_bcast.defvjp(lambda sel_row, gc: (_row_bcast_impl(sel_row, gc), sel_row), _row_bcast_bwd)


def _col_bcast_impl(gc):
    return gc[:, :DN_CHUNK]


def _col_bcast_bwd(_, d):
    return (jnp.broadcast_to(jnp.sum(d, axis=1, keepdims=True) * (1.0 / HEAD), (d.shape[0], HEAD)),)


_col_bcast = jax.custom_vjp(_col_bcast_impl)
_col_bcast.defvjp(lambda gc: (_col_bcast_impl(gc), None), _col_bcast_bwd)


def _last_row_bcast(n):
    def impl(gc):
        return jnp.broadcast_to(gc[DN_CHUNK - 1:DN_CHUNK, :], (n, HEAD))

    def bwd(_, d):
        row = lax.broadcasted_iota(jnp.int32, (DN_CHUNK, HEAD), 0)
        return (jnp.where(row == DN_CHUNK - 1, jnp.sum(d, axis=0, keepdims=True), 0.0),)

    f = jax.custom_vjp(impl)
    f.defvjp(lambda gc: (impl(gc), None), bwd)
    return impl, f


_LAST_C, _LAST_H = _last_row_bcast(DN_CHUNK), _last_row_bcast(HEAD)


def _halves(axis):
    def impl(x):
        n = x.shape[axis] // 2
        return lax.slice_in_dim(x, 0, n, axis=axis), lax.slice_in_dim(x, n, 2 * n, axis=axis)

    f = jax.custom_vjp(impl)
    f.defvjp(lambda x: (impl(x), None), lambda _, g: (jnp.concatenate(g, axis=axis),))
    return impl, f


_ROW_HALVES, _COL_HALVES = _halves(0), _halves(1)


def _chunk_consts():
    C = DN_CHUNK
    io = lambda shape, ax: lax.broadcasted_iota(jnp.int32, shape, ax)
    one = lambda m: jnp.where(m, 1.0, 0.0).astype(f32)
    r, c = io((C, C), 0), io((C, C), 1)
    return dict(causal=r >= c, strict=r > c, sel_row=one(io((C, HEAD), 1) == 0))


def _chunk_fn(kc, kept_T, q, k, v, gc, bB, S0):
    diff = kept_T is not None
    i = 0 if diff else 1
    mm, mm_nt, mm_tn = _BDOT[NN][i], _BDOT[NT][i], _BDOT[TN][i]
    tri = (lambda Ls: _tri_inv_kept(Ls, kept_T)) if diff else _tri_inv_multi
    each = lambda f, *ls: tuple(f(*a) for a in zip(*ls))
    gcol = each(_col_bcast if diff else _col_bcast_impl, gc)
    grow = each(lambda g: (_row_bcast if diff else _row_bcast_impl)(kc['sel_row'], g), gc)
    glc = each(_LAST_C[i ^ 1], gc)
    glh = each(_LAST_H[i ^ 1], gc)
    decay = each(lambda a, b: jnp.where(kc['causal'], jnp.exp(jnp.where(kc['causal'], a - b, 0.0)), 0.0), gcol, grow)
    rows, cols = _ROW_HALVES[i ^ 1], _COL_HALVES[i ^ 1]
    first, second = (lambda ts: tuple(t[0] for t in ts)), (lambda ts: tuple(t[1] for t in ts))
    kb = each(lambda a, b: a * b, k, bB)
    vb = each(lambda a, b: a * b, v, bB)
    egc = each(jnp.exp, gc)
    qs = each(lambda a: a * (HEAD ** -0.5), q)
    kq = each(lambda a, b, kt: rows(mm_nt(jnp.concatenate([a, b], axis=0), kt)), kb, qs, k)
    kk, qk = first(kq), second(kq)
    T = tri(each(lambda a, d: jnp.where(kc['strict'], a * d, 0.0), kk, decay))
    uw = each(lambda t, a, b, e: cols(mm(t, jnp.concatenate([a, b * e], axis=1))), T, vb, kb, egc)
    u, w = first(uw), second(uw)
    attn = each(lambda a, d: jnp.where(kc['causal'], a * d, 0.0), qk, decay)
    wq = each(lambda a, b, e, s: rows(mm(jnp.concatenate([a, b * e], axis=0), s)), w, qs, egc, S0)
    wS, qS = first(wq), second(wq)
    v_new = each(lambda a, b: a - b, u, wS)
    o = each(lambda a, b: a + b, qS, each(mm, attn, v_new))
    kdec = each(lambda a, gl, g: a * jnp.exp(gl - g), k, glc, gc)
    S1 = each(lambda s, gl, kv: s * jnp.exp(gl) + kv, S0, glh, each(mm_tn, kdec, v_new))
    return (o, S1) if diff else (o, S1, T)


def _chunks_per_step(N):
    return 4 if N % 4 == 0 else (2 if N % 2 == 0 else 1)


def _heads_per_block(H):
    return 8 if H % 8 == 0 else (4 if H % 4 == 0 else 1)


def dn_chunk_fwd(qkv, gates, name, comm=None):
    _, S, W = qkv.shape
    H, C = W // HEAD, DN_CHUNK
    N, HB = S // C, _heads_per_block(H)
    assert HB == H
    CPS = _chunks_per_step(N)

    def body(q_ref, k_ref, v_ref, g_ref, o_ref, st_ref, t_ref, s_scr):
        @pl.when(pl.program_id(1) == 0)
        def _():
            s_scr[...] = jnp.zeros_like(s_scr)

        kc = _chunk_consts()
        sls = [slice(hh * HEAD, (hh + 1) * HEAD) for hh in range(HB)]
        St = tuple(s_scr[hh] for hh in range(HB))
        for c in range(CPS):
            rows = slice(c * C, (c + 1) * C)
            heads = lambda ref: tuple(ref[rows, sl] for sl in sls)
            gr = g_ref[rows, :]
            for hh in range(HB):
                st_ref[c, hh] = St[hh]
            o, St, T = _chunk_fn(kc, None, heads(q_ref), heads(k_ref), heads(v_ref),
                                 tuple(_lane_pick(gr, H + hh) for hh in range(HB)),
                                 tuple(_lane_pick(gr, hh) for hh in range(HB)), St)
            for hh in range(HB):
                o_ref[rows, sls[hh]] = o[hh]
                t_ref[c, hh] = T[hh]
        for hh in range(HB):
            s_scr[hh] = St[hh]

    part = lambda p: pl.BlockSpec((None, CPS * C, HB * HEAD), lambda hb, n: (p, n, hb))
    return hosted_call(
        body, comm, _grid_steps(H // HB, N // CPS), name=name, grid=(H // HB, N // CPS),
        in_specs=[part(0), part(1), part(2), pl.BlockSpec((CPS * C, HEAD), lambda hb, n: (n, 0))],
        out_specs=(pl.BlockSpec((CPS * C, HB * HEAD), lambda hb, n: (n, hb)),
                   pl.BlockSpec((CPS, HB, HEAD, HEAD), lambda hb, n: (n, hb, 0, 0)),
                   pl.BlockSpec((CPS, HB, C, C), lambda hb, n: (n, hb, 0, 0))),
        out_shape=(SDS((S, W), f32), SDS((N, H, HEAD, HEAD), f32), SDS((N, H, C, C), f32)),
        scratch_shapes=[pltpu.VMEM((HB, HEAD, HEAD), f32)], args=(qkv, qkv, qkv, gates))


def dn_chunk_bwd(qkv, gates, states, kept_T, do, name, comm=None):
    _, S, W = qkv.shape
    H, C = W // HEAD, DN_CHUNK
    N, HB = S // C, _heads_per_block(H)
    assert HB == H
    CPS = _chunks_per_step(N)
    NB = N // CPS

    def body(q_ref, k_ref, v_ref, g_ref, st_ref, t_ref, do_ref, dqkv_ref, dg_ref, ds_scr):
        @pl.when(pl.program_id(1) == 0)
        def _():
            ds_scr[...] = jnp.zeros_like(ds_scr)

        kc = _chunk_consts()
        sls = [slice(hh * HEAD, (hh + 1) * HEAD) for hh in range(HB)]
        dSt = tuple(ds_scr[hh] for hh in range(HB))
        for c in reversed(range(CPS)):
            rows = slice(c * C, (c + 1) * C)
            heads = lambda ref: tuple(ref[rows, sl] for sl in sls)
            gr = g_ref[rows, :]
            kept = tuple(t_ref[c, hh] for hh in range(HB))
            _, vjp = jax.vjp(functools.partial(_chunk_fn, kc, kept), heads(q_ref), heads(k_ref), heads(v_ref),
                             tuple(_lane_pick(gr, H + hh) for hh in range(HB)),
                             tuple(_lane_pick(gr, hh) for hh in range(HB)), tuple(st_ref[c, hh] for hh in range(HB)))
            dq, dk, dv, dg, db, dSt = vjp((heads(do_ref), dSt))
            lane = lax.broadcasted_iota(jnp.int32, (C, HEAD), 1)
            dgr = jnp.zeros((C, HEAD), f32)
            for hh in range(HB):
                dqkv_ref[0, rows, sls[hh]] = dq[hh]
                dqkv_ref[1, rows, sls[hh]] = dk[hh]
                dqkv_ref[2, rows, sls[hh]] = dv[hh]
                dgr = dgr + jnp.where(lane == hh, jnp.sum(db[hh], axis=1, keepdims=True), 0.0)
                dgr = dgr + jnp.where(lane == H + hh, jnp.sum(dg[hh], axis=1, keepdims=True), 0.0)
            dg_ref[rows, :] = dgr
        for hh in range(HB):
            ds_scr[hh] = dSt[hh]

    rev = lambda n: NB - 1 - n
    part = lambda p: pl.BlockSpec((None, CPS * C, HB * HEAD), lambda hb, n: (p, rev(n), hb))
    gate = pl.BlockSpec((CPS * C, HEAD), lambda hb, n: (rev(n), 0))
    return hosted_call(
        body, comm, _grid_steps(H // HB, NB), name=name, grid=(H // HB, NB),
        in_specs=[part(0), part(1), part(2), gate,
                  pl.BlockSpec((CPS, HB, HEAD, HEAD), lambda hb, n: (rev(n), hb, 0, 0)),
                  pl.BlockSpec((CPS, HB, C, C), lambda hb, n: (rev(n), hb, 0, 0)),
                  pl.BlockSpec((CPS * C, HB * HEAD), lambda hb, n: (rev(n), hb))],
        out_specs=(pl.BlockSpec((3, CPS * C, HB * HEAD), lambda hb, n: (0, rev(n), hb)), gate),
        out_shape=(SDS((3, S, W), f32), SDS((S, HEAD), f32)),
        scratch_shapes=[pltpu.VMEM((HB, HEAD, HEAD), f32)], args=(qkv, qkv, qkv, gates, states, kept_T, do))


def _gate_norm(o, z, ng):
    return _rms(o, ng) * _silu(z)


def dn_out(o, proj, ng, wout, x1, g3, name):
    S, W = o.shape
    D = x1.shape[1]
    nh = W // HEAD
    tm = _row_tile(S, 256)

    def body(o_ref, z_ref, ng_ref, w_ref, x_ref, g_ref, xo_ref, m_ref, og_ref):
        for h in range(nh):
            sl = slice(h * HEAD, (h + 1) * HEAD)
            og_ref[:, sl] = _gate_norm(o_ref[:, sl], z_ref[:, sl], ng_ref[...]).astype(og_ref.dtype)
        m = _mm(og_ref[...], w_ref[...])
        m_ref[...] = m
        xo_ref[...] = x_ref[...] + _rms(m, g_ref[...])

    rw = pl.BlockSpec((tm, W), lambda i: (i, 0))
    rd = pl.BlockSpec((tm, D), lambda i: (i, 0))
    return pl.pallas_call(
        body, name=name, grid=(S // tm,),
        in_specs=[rw, pl.BlockSpec((tm, W), lambda i: (i, 3)), pl.BlockSpec((1, HEAD), lambda i: (0, 0)),
                  pl.BlockSpec((W, D), lambda i: (0, 0)), rd, pl.BlockSpec((1, D), lambda i: (0, 0))],
        out_specs=(rd, rd, rw),
        out_shape=(SDS((S, D), f32), SDS((S, D), f32), SDS((S, W), MXU_DTYPE)), compiler_params=_params(1),
    )(o, proj, ng, wout, x1, g3)


def dn_out_bwd(dxo, m, g3, o, proj, ng, wout, name):
    S, W = o.shape
    D = m.shape[1]
    nh = W // HEAD
    tm = _row_tile(S, 256)

    def body(dxo_ref, m_ref, g_ref, o_ref, z_ref, ng_ref, w_ref, dm_ref, do_ref, dz_ref, dng_ref, dg_ref):
        @pl.when(pl.program_id(0) == 0)
        def _():
            dng_ref[...] = jnp.zeros_like(dng_ref)
            dg_ref[...] = jnp.zeros_like(dg_ref)

        dm, dg = _rms_bwd(m_ref[...], g_ref[...], dxo_ref[...])
        dg_ref[...] += dg
        dmc = dm.astype(dm_ref.dtype)
        dm_ref[...] = dmc
        dog = _mm_nt(dmc, w_ref[...])
        for h in range(nh):
            sl = slice(h * HEAD, (h + 1) * HEAD)
            _, vjp = jax.vjp(_gate_norm, o_ref[:, sl], z_ref[:, sl], ng_ref[...])
            do, dz, dng = vjp(dog[:, sl])
            do_ref[:, sl] = do
            dz_ref[:, sl] = dz.astype(dz_ref.dtype)
            dng_ref[...] += dng

    rw = pl.BlockSpec((tm, W), lambda i: (i, 0))
    rd = pl.BlockSpec((tm, D), lambda i: (i, 0))
    vd = pl.BlockSpec((1, D), lambda i: (0, 0))
    vh = pl.BlockSpec((1, HEAD), lambda i: (0, 0))
    return pl.pallas_call(
        body, name=name, grid=(S // tm,),
        in_specs=[rd, rd, vd, rw, pl.BlockSpec((tm, W), lambda i: (i, 3)), vh, pl.BlockSpec((W, D), lambda i: (0, 0))],
        out_specs=(rd, rw, rw, vh, vd),
        out_shape=(SDS((S, D), MXU_DTYPE), SDS((S, W), f32), SDS((S, W), MXU_DTYPE), SDS((1, HEAD), f32),
                   SDS((1, D), f32)),
        compiler_params=_params(1),
    )(dxo, m, g3, o, proj, ng, wout)


def _erf_arg(x):
    return lax.erf(x * 0.7071067811865476)


@jax.custom_vjp
def _gelu_with_erf(x, e):
    return 0.5 * x * (1.0 + e)


def _gelu_with_erf_bwd(res, g):
    x, e = res
    return g * (0.5 * (1.0 + e) + x * (jnp.exp(-0.5 * x * x) * 0.3989422804014327)), jnp.zeros_like(e)


_gelu_with_erf.defvjp(lambda x, e: (0.5 * x * (1.0 + e), (x, e)), _gelu_with_erf_bwd)


def _layernorm(t, lg, lb):
    tc = t - jnp.mean(t, axis=-1, keepdims=True)
    return tc * lax.rsqrt(jnp.mean(tc * tc, axis=-1, keepdims=True) + LN_EPS) * lg + lb


def _sg_stage1_kept(eu, ev, pu, pv, bu, bv, lg, lb):
    return _gelu_with_erf(pu + bu, eu), _layernorm(_gelu_with_erf(pv + bv, ev), lg, lb)


def _causal_mask(n):
    return lax.broadcasted_iota(jnp.int32, (n, n), 0) >= lax.broadcasted_iota(jnp.int32, (n, n), 1)


def sg_mid(pre, b_in, ln_g, ln_b, w_s, bsT, wout, x1, g3, name):
    S = pre.shape[0]
    E, D = ln_g.shape[1], x1.shape[1]
    G, CH = SG_GROUPS, SG_CHUNK
    Cg = E // G
    tm = _row_tile(S, 256)

    def body(pu_ref, pv_ref, bu_ref, bv_ref, lg_ref, lb_ref, ws_ref, bs_ref, w_ref, x_ref, g_ref,
             xo_ref, m_ref, gt_ref, e_ref):
        xu, xv = pu_ref[...] + bu_ref[...], pv_ref[...] + bv_ref[...]
        eu, ev = _erf_arg(xu), _erf_arg(xv)
        e_ref[:, :E] = eu.astype(e_ref.dtype)
        e_ref[:, E:] = ev.astype(e_ref.dtype)
        u = 0.5 * xu * (1.0 + eu)
        v = _layernorm(0.5 * xv * (1.0 + ev), lg_ref[...], lb_ref[...])
        mask = _causal_mask(CH)
        for g in range(G):
            wc = _c(jnp.where(mask, ws_ref[g], 0.0))
            bcol = bs_ref[:, g:g + 1]
            cs = slice(g * Cg, (g + 1) * Cg)
            for ch in range(tm // CH):
                rs = slice(ch * CH, (ch + 1) * CH)
                mixed = _mm(wc, _c(v[rs, cs])) + bcol
                gt_ref[rs, cs] = (u[rs, cs] * mixed).astype(gt_ref.dtype)
        m = _mm(gt_ref[...], w_ref[...])
        m_ref[...] = m
        xo_ref[...] = x_ref[...] + _rms(m, g_ref[...])

    half = lambda p: pl.BlockSpec((tm, E), lambda i: (i, p))
    vhalf = lambda p: pl.BlockSpec((1, E), lambda i: (0, p))
    ve = pl.BlockSpec((1, E), lambda i: (0, 0))
    rd = pl.BlockSpec((tm, D), lambda i: (i, 0))
    return pl.pallas_call(
        body, name=name, grid=(S // tm,),
        in_specs=[half(0), half(1), vhalf(0), vhalf(1), ve, ve, pl.BlockSpec((G, CH, CH), lambda i: (0, 0, 0)),
                  pl.BlockSpec((CH, G), lambda i: (0, 0)), pl.BlockSpec((E, D), lambda i: (0, 0)), rd,
                  pl.BlockSpec((1, D), lambda i: (0, 0))],
        out_specs=(rd, rd, pl.BlockSpec((tm, E), lambda i: (i, 0)), pl.BlockSpec((tm, 2 * E), lambda i: (i, 0))),
        out_shape=(SDS((S, D), f32), SDS((S, D), f32), SDS((S, E), MXU_DTYPE), SDS((S, 2 * E), MXU_DTYPE)),
        compiler_params=_params(1),
    )(pre, pre, b_in, b_in, ln_g, ln_b, w_s, bsT, wout, x1, g3)


def sg_mid_bwd(dxo, m, g3, pre, kept_erf, b_in, ln_g, ln_b, w_s, bsT, wout, name):
    S = pre.shape[0]
    E, D = ln_g.shape[1], m.shape[1]
    G, CH = SG_GROUPS, SG_CHUNK
    Cg = E // G
    tm = _row_tile(S, 256)

    def body(dxo_ref, m_ref, g_ref, pu_ref, pv_ref, eu_ref, ev_ref, bu_ref, bv_ref, lg_ref, lb_ref, ws_ref, bs_ref,
             w_ref, dm_ref, dpre_ref, dbin_ref, dlg_ref, dlb_ref, dws_ref, dbs_ref, dg_ref, du_scr, dv_scr):
        @pl.when(pl.program_id(0) == 0)
        def _():
            for r in (dbin_ref, dlg_ref, dlb_ref, dws_ref, dbs_ref, dg_ref):
                r[...] = jnp.zeros_like(r)

        dm, dg = _rms_bwd(m_ref[...], g_ref[...], dxo_ref[...])
        dg_ref[...] += dg
        dmc = dm.astype(dm_ref.dtype)
        dm_ref[...] = dmc
        dgated = _mm_nt(dmc, w_ref[...])
        stage1 = functools.partial(_sg_stage1_kept, eu_ref[...].astype(f32), ev_ref[...].astype(f32))
        (u, v), vjp1 = jax.vjp(stage1, pu_ref[...], pv_ref[...], bu_ref[...], bv_ref[...], lg_ref[...], lb_ref[...])
        mask = _causal_mask(CH)
        lane = lax.broadcasted_iota(jnp.int32, (CH, CH), 1)
        for g in range(G):
            wc = _c(jnp.where(mask, ws_ref[g], 0.0))
            bcol = bs_ref[:, g:g + 1]
            cs = slice(g * Cg, (g + 1) * Cg)
            dws = jnp.zeros((CH, CH), f32)
            dbs = jnp.zeros((CH, 1), f32)
            for ch in range(tm // CH):
                rs = slice(ch * CH, (ch + 1) * CH)
                vs = _c(v[rs, cs])
                mixed = _mm(wc, vs) + bcol
                dgt = dgated[rs, cs]
                du_scr[rs, cs] = dgt * mixed
                dmixed = dgt * u[rs, cs]
                dmc2 = _c(dmixed)
                dv_scr[rs, cs] = _mm_tn(wc, dmc2)
                dws = dws + _mm_nt(dmc2, vs)
                dbs = dbs + jnp.sum(dmixed, axis=1, keepdims=True)
            dws_ref[g] += jnp.where(mask, dws, 0.0)
            dbs_ref[...] += jnp.where(lane == g, jnp.broadcast_to(dbs, (CH, CH)), 0.0)
        dpu, dpv, dbu, dbv, dlg, dlb = vjp1((du_scr[...], dv_scr[...]))
        dpre_ref[:, :E] = dpu.astype(dpre_ref.dtype)
        dpre_ref[:, E:] = dpv.astype(dpre_ref.dtype)
        dbin_ref[:, :E] += dbu
        dbin_ref[:, E:] += dbv
        dlg_ref[...] += dlg
        dlb_ref[...] += dlb

    half = lambda p: pl.BlockSpec((tm, E), lambda i: (i, p))
    vhalf = lambda p: pl.BlockSpec((1, E), lambda i: (0, p))
    ve = pl.BlockSpec((1, E), lambda i: (0, 0))
    rd = pl.BlockSpec((tm, D), lambda i: (i, 0))
    vd = pl.BlockSpec((1, D), lambda i: (0, 0))
    wsb = pl.BlockSpec((G, CH, CH), lambda i: (0, 0, 0))
    return pl.pallas_call(
        body, name=name, grid=(S // tm,),
        in_specs=[rd, rd, vd, half(0), half(1), half(0), half(1), vhalf(0), vhalf(1), ve, ve, wsb,
                  pl.BlockSpec((CH, G), lambda i: (0, 0)), pl.BlockSpec((E, D), lambda i: (0, 0))],
        out_specs=(rd, pl.BlockSpec((tm, 2 * E), lambda i: (i, 0)), pl.BlockSpec((1, 2 * E), lambda i: (0, 0)), ve, ve,
                   wsb, pl.BlockSpec((CH, CH), lambda i: (0, 0)), vd),
        out_shape=(SDS((S, D), MXU_DTYPE), SDS((S, 2 * E), MXU_DTYPE), SDS((1, 2 * E), f32), SDS((1, E), f32),
                   SDS((1, E), f32), SDS((G, CH, CH), f32), SDS((CH, CH), f32), SDS((1, D), f32)),
        scratch_shapes=[pltpu.VMEM((tm, E), f32), pltpu.VMEM((tm, E), f32)], compiler_params=_params(1),
    )(dxo, m, g3, pre, pre, kept_erf, kept_erf, b_in, b_in, ln_g, ln_b, w_s, bsT, wout)


def loss_head(y, target, name):
    S, D = y.shape
    tm = _row_tile(S, 512)

    def body(y_ref, t_ref, l_ref, d_ref):
        @pl.when(pl.program_id(0) == 0)
        def _():
            l_ref[...] = jnp.zeros_like(l_ref)

        e = y_ref[...] - t_ref[...]
        d_ref[...] = e * (1.0 / D)
        l_ref[...] += jnp.sum(e * e) * (0.5 / D)

    row = pl.BlockSpec((tm, D), lambda i: (i, 0))
    return pl.pallas_call(
        body, name=name, grid=(S // tm,), in_specs=[row, row],
        out_specs=(pl.BlockSpec((1, HEAD), lambda i: (0, 0)), row),
        out_shape=(SDS((1, HEAD), f32), SDS((S, D), f32)), compiler_params=_params(1),
    )(y, target)


def sum_slots(r, name):
    _, R, C = r.shape
    tr = R // 2 if R % 16 == 0 else R

    def body(r_ref, o_ref):
        acc = r_ref[0].astype(f32)
        for s in range(1, N_DEV):
            acc = acc + r_ref[s].astype(f32)
        o_ref[...] = acc

    return pl.pallas_call(
        body, name=name, grid=(R // tr,), in_specs=[pl.BlockSpec((N_DEV, tr, C), lambda i: (0, i, 0))],
        out_specs=pl.BlockSpec((tr, C), lambda i: (i, 0)), out_shape=SDS((R, C), f32), compiler_params=_params(1),
    )(r)


def _adam_math(w, g, m, v):
    m = ADAM_B1 * m + (1.0 - ADAM_B1) * g
    v = ADAM_B2 * v + (1.0 - ADAM_B2) * (g * g)
    m_hat = m / (1.0 - ADAM_B1 ** ADAM_STEP)
    v_hat = v / (1.0 - ADAM_B2 ** ADAM_STEP)
    delta = -ADAM_LR * (m_hat / (jnp.sqrt(v_hat) + ADAM_EPS) + ADAM_WD * w)
    return delta, m, v


def adam_slots(w, rs, m, v, name, tr):
    R, C = w.shape
    tr = _row_tile(min(r.shape[1] for r in rs), tr)
    blocks = [r.shape[1] // tr for r in rs]
    starts = [sum(blocks[:k]) for k in range(len(rs))]
    assert sum(blocks) * tr == R

    def body(w_ref, *refs):
        r_refs, (m_ref, v_ref, g_ref, d_ref, mo_ref, vo_ref) = refs[:len(rs)], refs[len(rs):]
        i = pl.program_id(0)
        for k, r_ref in enumerate(r_refs):
            @pl.when((i >= starts[k]) & (i < starts[k] + blocks[k]))
            def _():
                g = r_ref[0].astype(f32)
                for s in range(1, N_DEV):
                    g = g + r_ref[s].astype(f32)
                g_ref[...] = g

        d_ref[...], mo_ref[...], vo_ref[...] = _adam_math(w_ref[...], g_ref[...], m_ref[...], v_ref[...])

    row = pl.BlockSpec((tr, C), lambda i: (i, 0))
    piece = lambda k: pl.BlockSpec((N_DEV, tr, C), lambda i: (0, jnp.clip(i - starts[k], 0, blocks[k] - 1), 0))
    return pl.pallas_call(
        body, name=name, grid=(R // tr,), in_specs=[row] + [piece(k) for k in range(len(rs))] + [row, row],
        out_specs=(row, row, row, row), out_shape=tuple(SDS((R, C), f32) for _ in range(4)),
        compiler_params=_params(1),
    )(w, *rs, m, v)


def adam_small(w, g, m, v, name):
    def body(w_ref, g_ref, m_ref, v_ref, d_ref, mo_ref, vo_ref):
        d_ref[...], mo_ref[...], vo_ref[...] = _adam_math(w_ref[...], g_ref[...], m_ref[...], v_ref[...])

    return pl.pallas_call(body, name=name, out_shape=tuple(SDS(w.shape, f32) for _ in range(3)))(w, g, m, v)


def _pack_rows(parts):
    rows, offs, r = [], [], 0
    for p in parts:
        flat = p.reshape(-1)
        n = -(-flat.shape[0] // HEAD)
        flat = jnp.pad(flat, (0, n * HEAD - flat.shape[0]))
        rows.append(flat.reshape(n, HEAD))
        offs.append((r, n))
        r += n
    pad = (-r) % 8
    if pad:
        rows.append(jnp.zeros((pad, HEAD), f32))
    return jnp.concatenate(rows, axis=0), offs


def kernel(x, norm_g, ffn_w_gate, ffn_w_up, ffn_w_down, dn_w_in, dn_conv_w, dn_a_log, dn_dt_bias, dn_norm_g, dn_w_out, sg_w_in, sg_b_in, sg_ln_g, sg_ln_b, sg_w_s, sg_b_s, sg_w_out, loss_target, m_norm_g, m_ffn_w_gate, m_ffn_w_up, m_ffn_w_down, m_dn_w_in, m_dn_conv_w, m_dn_a_log, m_dn_dt_bias, m_dn_norm_g, m_dn_w_out, m_sg_w_in, m_sg_b_in, m_sg_ln_g, m_sg_ln_b, m_sg_w_s, m_sg_b_s, m_sg_w_out, v_norm_g, v_ffn_w_gate, v_ffn_w_up, v_ffn_w_down, v_dn_w_in, v_dn_conv_w, v_dn_a_log, v_dn_dt_bias, v_dn_norm_g, v_dn_w_out, v_sg_w_in, v_sg_b_in, v_sg_ln_g, v_sg_ln_b, v_sg_w_s, v_sg_b_s, v_sg_w_out):
    weights = dict(norm_g=norm_g, ffn_w_gate=ffn_w_gate, ffn_w_up=ffn_w_up, ffn_w_down=ffn_w_down, dn_w_in=dn_w_in,
                   dn_conv_w=dn_conv_w, dn_a_log=dn_a_log, dn_dt_bias=dn_dt_bias, dn_norm_g=dn_norm_g,
                   dn_w_out=dn_w_out, sg_w_in=sg_w_in, sg_b_in=sg_b_in, sg_ln_g=sg_ln_g, sg_ln_b=sg_ln_b,
                   sg_w_s=sg_w_s, sg_b_s=sg_b_s, sg_w_out=sg_w_out)
    mom_m = dict(norm_g=m_norm_g, ffn_w_gate=m_ffn_w_gate, ffn_w_up=m_ffn_w_up, ffn_w_down=m_ffn_w_down,
                 dn_w_in=m_dn_w_in, dn_conv_w=m_dn_conv_w, dn_a_log=m_dn_a_log, dn_dt_bias=m_dn_dt_bias,
                 dn_norm_g=m_dn_norm_g, dn_w_out=m_dn_w_out, sg_w_in=m_sg_w_in, sg_b_in=m_sg_b_in,
                 sg_ln_g=m_sg_ln_g, sg_ln_b=m_sg_ln_b, sg_w_s=m_sg_w_s, sg_b_s=m_sg_b_s, sg_w_out=m_sg_w_out)
    mom_v = dict(norm_g=v_norm_g, ffn_w_gate=v_ffn_w_gate, ffn_w_up=v_ffn_w_up, ffn_w_down=v_ffn_w_down,
                 dn_w_in=v_dn_w_in, dn_conv_w=v_dn_conv_w, dn_a_log=v_dn_a_log, dn_dt_bias=v_dn_dt_bias,
                 dn_norm_g=v_dn_norm_g, dn_w_out=v_dn_w_out, sg_w_in=v_sg_w_in, sg_b_in=v_sg_b_in,
                 sg_ln_g=v_sg_ln_g, sg_ln_b=v_sg_ln_b, sg_w_s=v_sg_w_s, sg_b_s=v_sg_b_s, sg_w_out=v_sg_w_out)
    order = list(weights)

    xs = x[0]
    S, D = xs.shape
    F8 = ffn_w_gate.shape[-1]
    depth = norm_g.shape[0]
    W = dn_w_out.shape[1] * N_DEV
    H = W // HEAD
    E = sg_ln_g.shape[1] * N_DEV
    G, CH = sg_w_s.shape[1], sg_w_s.shape[2]
    c8 = dn_w_in.shape[2]
    me = _slot(lax.axis_index("x"), lax.axis_index("y"), lax.axis_index("c"))

    assert depth == 2
    small_in, small_offs = _pack_rows([norm_g, dn_conv_w, sg_b_in, sg_ln_g, sg_ln_b])
    wg0a, wu0a, wd0a, small_all = all_gather_multi(
        [_c(ffn_w_gate[0, 0]), _c(ffn_w_up[0, 0]), _c(ffn_w_down[0, 0]), small_in], name="gather_first")
    ffn_shards = lambda l, ab: [_c(ffn_w_gate[l, ab]), _c(ffn_w_up[l, ab]), _c(ffn_w_down[l, ab])]
    gather_dn = Comm("gather", [_c(dn_w_in[0]), _c(dn_w_out[0])])
    gather_mid = Comm("gather", ffn_shards(0, 1) + ffn_shards(1, 0))
    gather_end = Comm("gather", ffn_shards(1, 1))
    gather_sg = Comm("gather", [_c(sg_w_in[0]), _c(sg_w_out[0])])
    per = N_DEV // FFN_SLABS
    wide = lambda tag, g, u, d: (*widen_slabs([g, u], FFN_SLABS, name=f"widen_{tag}"),
                                 d.reshape(FFN_SLABS, per * F8, D))
    ffn_w = {(0, 0): wide("0a", wg0a, wu0a, wd0a)}

    def small_piece(i, shard_shape):
        r0, n = small_offs[i]
        sz = math.prod(shard_shape)
        return small_all[:, r0:r0 + n, :].reshape(N_DEV, n * HEAD)[:, :sz].reshape((N_DEV,) + tuple(shard_shape))

    ng_full = jnp.moveaxis(small_piece(0, norm_g.shape), 0, 2).reshape(depth, 6, D)
    conv_full = jnp.moveaxis(small_piece(1, dn_conv_w.shape[1:]), 0, 1).reshape(CONV_K, 3 * W)
    bin_full = small_piece(2, sg_b_in.shape[1:]).reshape(1, 2 * E)
    lng_full = small_piece(3, sg_ln_g.shape[1:]).reshape(1, E)
    lnb_full = small_piece(4, sg_ln_b.shape[1:]).reshape(1, E)
    gate_lanes = lambda v: jnp.pad(v.reshape(1, H), ((0, 0), (H, HEAD - 2 * H)))
    al_row, dt_row = gate_lanes(dn_a_log), gate_lanes(dn_dt_bias)
    bsT = sg_b_s[0].T
    gvec = lambda l, k: ng_full[l, k].reshape(1, D)

    saved = []
    cur = xs
    for l in range(depth):
        sv = {}
        sv['x0'] = cur
        (cur, sv['hA'], sv['pA'], sv['qA'], sv['tA'], sv['yA']), got = ffn_fwd(
            cur, gvec(l, 0), gvec(l, 1), *ffn_w[l, 0], name=f"ffn_fwd_{l}a", comm=gather_dn if l == 0 else gather_sg)
        sv['x1'] = cur
        if l == 1:
            sg_win = jnp.moveaxis(got[0], 0, 1).reshape(D, 2 * E)
            sg_wout = got[1].reshape(E, D)
        if l == 0:
            dnin_all, dnout_all = got
            dn_wmain, dn_wba = join_columns(dnin_all, 4 * W, name="dn_w_in_join")
            dn_wout = dnout_all.reshape(W, D)
            sv['hM'], sv['proj'], sv['pba'] = rms_mm(cur, gvec(l, 2), dn_wmain, dn_wba, name=f"dn_in_{l}")
            sv['qkv'] = dn_prep(sv['proj'], conv_full, name=f"dn_prep_{l}")
            sv['gates'] = dn_gates(sv['pba'], al_row, dt_row, H, name=f"dn_gates_{l}")
            (sv['o'], sv['states'], sv['T']), got = dn_chunk_fwd(sv['qkv'], sv['gates'], name=f"dn_chunk_{l}",
                                                                 comm=gather_mid)
            ffn_w[0, 1], ffn_w[1, 0] = wide("0b", *got[0:3]), wide("1a", *got[3:6])
            cur, sv['m'], sv['og'] = dn_out(sv['o'], sv['proj'], dn_norm_g, dn_wout, cur, gvec(l, 3), name=f"dn_out_{l}")
        else:
            sv['hM'], sv['pre'] = rms_mm(cur, gvec(l, 2), sg_win, None, name=f"sg_in_{l}")
            cur, sv['m'], sv['gated'], sv['erf'] = sg_mid(sv['pre'], bin_full, lng_full, lnb_full, sg_w_s[0], bsT, sg_wout,
                                                          cur, gvec(l, 3), name=f"sg_mid_{l}")
        sv['x2'] = cur
        (cur, sv['hB'], sv['pB'], sv['qB'], sv['tB'], sv['yB']), got = ffn_fwd(
            cur, gvec(l, 4), gvec(l, 5), *ffn_w[l, 1], name=f"ffn_fwd_{l}b", comm=gather_end if l == 0 else None)
        if l == 0:
            ffn_w[1, 1] = wide("1b", *got[0:3])
        saved.append(sv)

    loss_blk, dcur = loss_head(cur, loss_target[0], name="loss_head")
    loss = lax.psum(loss_blk[0, 0], ("x", "y", "c"))

    dng = [[None] * 6 for _ in range(depth)]
    ffn_dw = {}
    grads, slots = {}, {}

    def ffn_backward(l, ab, dcur):
        sv, s = saved[l], 'AB'[ab]
        (dcur, da, db, dy, dng[l][4 * ab], dng[l][4 * ab + 1]), _ = ffn_bwd_dx(
            dcur, sv['x2' if ab else 'x0'], sv['y' + s], sv['p' + s], sv['q' + s], gvec(l, 4 * ab), gvec(l, 4 * ab + 1),
            *ffn_w[l, ab], name=f"ffn_bwd_{l}{'ab'[ab]}")
        ffn_dw[l, ab], _ = ffn_bwd_dw(sv['h' + s], dy, sv['t' + s], da, db, name=f"ffn_dw_{l}{'ab'[ab]}")
        return dcur

    sv = saved[1]
    dcur = ffn_backward(1, 1, dcur)
    dm, dpre, grads['sg_b_in'], grads['sg_ln_g'], grads['sg_ln_b'], grads['sg_w_s'], dbs, dng[1][3] = sg_mid_bwd(
        dcur, sv['m'], gvec(1, 3), sv['pre'], sv['erf'], bin_full, lng_full, lnb_full, sg_w_s[0], bsT, sg_wout,
        name="sg_mid_bwd_1")
    grads['sg_b_s'] = dbs[:, :G].T
    dsg_wout = tn_mm(sv['gated'], dm, name="sg_wout_dw_1").reshape(N_DEV, E // N_DEV, D)
    dsg_win = tn_mm(sv['hM'], dpre, name="sg_win_dw_1", tn=2 * E // N_DEV, slot_major=True)
    (dcur, dng[1][2]), _ = mm_bwd_dx(dcur, sv['x1'], gvec(1, 2), dpre, sg_win, None, None, name="sg_in_bwd_1")
    dcur = ffn_backward(1, 0, dcur)
    sv = saved[0]
    dcur = ffn_backward(0, 1, dcur)
    dm, do, dz, grads['dn_norm_g'], dng[0][3] = dn_out_bwd(dcur, sv['m'], gvec(0, 3), sv['o'], sv['proj'], dn_norm_g,
                                                          dn_wout, name="dn_out_bwd_0")
    ddn_wout = tn_mm(sv['og'], dm, name="dn_wout_dw_0").reshape(N_DEV, W // N_DEV, D)
    (dqkv, dgates), got = dn_chunk_bwd(sv['qkv'], sv['gates'], sv['states'], sv['T'], do, name="dn_chunk_bwd_0",
                                       comm=Comm("exchange", [*ffn_dw[1, 0], *ffn_dw[1, 1], dsg_win, dsg_wout]))
    l1a, l1b, slots['sg_w_in'], slots['sg_w_out'] = got[0:3], got[3:6], [got[6]], [got[7]]
    dpba, dal, ddt = dn_gates_bwd(sv['pba'], al_row, dt_row, dgates, H, name="dn_gates_bwd_0")
    grads['dn_a_log'] = dal[:, H:2 * H]
    grads['dn_dt_bias'] = ddt[:, H:2 * H]
    (dproj, grads['dn_conv_w']), got = dn_prep_bwd(sv['proj'], conv_full, dqkv, dz, name="dn_prep_bwd_0",
                                                   comm=Comm("exchange", [*ffn_dw[0, 1], ddn_wout]))
    l0b, slots['dn_w_out'] = got[0:3], [got[3]]
    dw_main = tn_mm(sv['hM'], dproj, name="dn_win_dw_0")
    dw_ba = tn_mm(sv['hM'], dpba, name="dn_wba_dw_0", tn=HEAD)
    ddn_win = split_columns(dw_main, dw_ba, c8, name="dn_w_in_split")
    (dcur, dng[0][2]), got = mm_bwd_dx(dcur, sv['x1'], gvec(0, 2), dproj, dn_wmain, dpba, dn_wba, name="dn_in_bwd_0",
                                       comm=Comm("exchange", [ddn_win]))
    slots['dn_w_in'] = [got[0]]
    small_names = ['norm_g', 'dn_conv_w', 'sg_b_in', 'sg_ln_g', 'sg_ln_b', 'sg_w_s', 'sg_b_s', 'dn_a_log',
                   'dn_dt_bias', 'dn_norm_g']
    small = {}

    def gather_small():
        dng_full = jnp.stack([jnp.concatenate(r, axis=0) for r in dng], axis=0)
        small['parts'] = [dng_full, grads['dn_conv_w'], grads['sg_b_in'], grads['sg_ln_g'], grads['sg_ln_b'],
                          grads['sg_w_s'], grads['sg_b_s'], grads['dn_a_log'], grads['dn_dt_bias'], grads['dn_norm_g']]
        pack, small['offs'] = _pack_rows(small['parts'])
        return Comm("gather", [pack])

    (dcur, da, db, dy, dng[0][0], dng[0][1]), _ = ffn_bwd_dx(
        dcur, sv['x0'], sv['yA'], sv['pA'], sv['qA'], gvec(0, 0), gvec(0, 1), *ffn_w[0, 0], name="ffn_bwd_0a")
    grad_x = dcur[None]
    (dg,), (small_slots,) = ffn_bwd_dw_one(sv['hA'], da, False, name="ffn_dw_0a_gate", comm=gather_small())
    (du,), (xg,) = ffn_bwd_dw_one(sv['hA'], db, False, name="ffn_dw_0a_up", comm=Comm("exchange", [dg]))
    (dd,), (xu,) = ffn_bwd_dw_one(dy, sv['tA'], True, name="ffn_dw_0a_down", comm=Comm("exchange", [du]))
    small_parts, offs = small['parts'], small['offs']
    l0a = [xg, xu, exchange_slots([dd], name="exchange_last")[0]]
    for i, nm in enumerate(['ffn_w_gate', 'ffn_w_up', 'ffn_w_down']):
        slots[nm] = [l0a[i], l0b[i], l1a[i], l1b[i]]
    big_names = ['ffn_w_gate', 'ffn_w_up', 'ffn_w_down', 'dn_w_in', 'dn_w_out', 'sg_w_in', 'sg_w_out']
    slots = [slots[nm] for nm in big_names]
    small_sum = sum_slots(small_slots, name="sum_small_grads")

    def small_grad(i):
        r0, n = offs[i]
        p = small_parts[i]
        return small_sum[r0:r0 + n].reshape(-1)[:p.size].reshape(p.shape)

    def my_shard(full, axis, like):
        n = full.shape[axis] // N_DEV
        return lax.dynamic_slice_in_dim(full, me * n, n, axis).reshape(like.shape)

    g_small = {
        'norm_g': my_shard(small_grad(0), 2, norm_g),
        'dn_conv_w': my_shard(small_grad(1), 1, dn_conv_w),
        'sg_b_in': my_shard(small_grad(2), 1, sg_b_in),
        'sg_ln_g': my_shard(small_grad(3), 1, sg_ln_g),
        'sg_ln_b': my_shard(small_grad(4), 1, sg_ln_b),
        'sg_w_s': small_grad(5).reshape(sg_w_s.shape),
        'sg_b_s': small_grad(6).reshape(sg_b_s.shape),
        'dn_a_log': small_grad(7).reshape(dn_a_log.shape),
        'dn_dt_bias': small_grad(8).reshape(dn_dt_bias.shape),
        'dn_norm_g': small_grad(9).reshape(dn_norm_g.shape),
    }

    out_g, out_d, out_m, out_v = {}, {}, {}, {}
    for nm, r in zip(big_names, slots):
        w = weights[nm]
        cols = w.shape[-1]
        rows = w.size // cols
        tr = {'ffn_w_gate': 512, 'ffn_w_up': 512, 'ffn_w_down': F8 // 2, 'dn_w_in': 256, 'sg_w_in': 256}.get(nm, rows)
        pieces = [p.reshape(N_DEV, -1, cols) for p in r]
        g, d, m2, v2 = adam_slots(w.reshape(rows, cols), pieces, mom_m[nm].reshape(rows, cols),
                                  mom_v[nm].reshape(rows, cols), name=f"adam_{nm}", tr=tr)
        out_g[nm], out_d[nm], out_m[nm], out_v[nm] = (t.reshape(w.shape) for t in (g, d, m2, v2))
    for nm in small_names:
        w = weights[nm]
        cols = w.shape[-1]
        rows = w.size // cols
        two = lambda t: t.reshape(rows, cols)
        d, m2, v2 = adam_small(two(w), two(g_small[nm]), two(mom_m[nm]), two(mom_v[nm]), name=f"adam_{nm}")
        out_g[nm] = g_small[nm]
        out_d[nm], out_m[nm], out_v[nm] = (t.reshape(w.shape) for t in (d, m2, v2))

    return (loss, grad_x, *[out_g[n] for n in order], *[out_d[n] for n in order], *[out_m[n] for n in order],
            *[out_v[n] for n in order])
```

```python
import functools
import math

import jax
import jax.numpy as jnp
from jax import lax
from jax.experimental import pallas as pl
from jax.experimental.pallas import tpu as pltpu

f32 = jnp.float32
MXU_DTYPE = jnp.bfloat16
N_DEV = 8
RMS_EPS = 1e-6
LN_EPS = 1e-5
L2_EPS = 1e-6
HEAD = 128
DN_CHUNK = 64
SG_CHUNK = 128
SG_GROUPS = 8
CONV_K = 4
ADAM_LR, ADAM_B1, ADAM_B2, ADAM_EPS, ADAM_WD, ADAM_STEP = 0.001, 0.9, 0.999, 1e-08, 0.01, 10
VMEM_LIMIT = 56 * 1024 * 1024
FFN_ROWS_FWD, FFN_ROWS_BWD, FFN_ROWS_DW = 1024, 512, 2048
PROJ_ROWS, TN_ROWS = 1024, 2048
FFN_SLABS = 4
SDS = jax.ShapeDtypeStruct
MESH = pl.DeviceIdType.MESH


def _params(n_grid):
    return pltpu.CompilerParams(dimension_semantics=("arbitrary",) * n_grid, vmem_limit_bytes=VMEM_LIMIT)


def _row_tile(s, want):
    t = min(s, want)
    assert s % t == 0, (s, t)
    return t


def _rms(x, g):
    return x * lax.rsqrt(jnp.mean(x * x, axis=-1, keepdims=True) + RMS_EPS) * g


def _rms_bwd(x, g, dy):
    _, vjp = jax.vjp(_rms, x, g)
    return vjp(dy)


def _silu(a):
    return a * jax.nn.sigmoid(a)


def _mm(a, b):
    return lax.dot_general(a, b, (((1,), (0,)), ((), ())), preferred_element_type=f32)


def _mm_nt(a, b):
    return lax.dot_general(a, b, (((1,), (1,)), ((), ())), preferred_element_type=f32)


def _mm_tn(a, b):
    return lax.dot_general(a, b, (((0,), (0,)), ((), ())), preferred_element_type=f32)


def _c(x):
    return x.astype(MXU_DTYPE)


def _split(a):
    hi = a.astype(MXU_DTYPE)
    lo = (a - hi.astype(f32)).astype(MXU_DTYPE)
    return hi, lo


def _dot3(a, b, dims):
    ah, al = _split(a)
    bh, bl = _split(b)
    d = lambda p, q: lax.dot_general(p, q, (dims, ((), ())), preferred_element_type=f32)
    return d(ah, bh) + (d(ah, bl) + d(al, bh))


NN, NT, TN = ((1,), (0,)), ((1,), (1,)), ((0,), (0,))


def _slot(px, py, pc):
    return 4 * px + 2 * py + pc


def all_gather_multi(arrs, name):
    return Comm("gather", arrs).alone(name)


def exchange_slots(arrs, name):
    return Comm("exchange", arrs).alone(name)


class Comm:
    def __init__(self, kind, arrs):
        self.kind, self.arrs, self.n = kind, list(arrs), len(arrs)
        hbm = pl.BlockSpec(memory_space=pltpu.HBM)
        self.in_specs = [hbm] * self.n
        self.out_specs = [hbm] * self.n
        lead = (N_DEV,) if kind == "gather" else ()
        self.out_shape = [SDS(lead + tuple(a.shape), a.dtype) for a in self.arrs]
        self.scratch = [pltpu.SemaphoreType.DMA((self.n, 7)), pltpu.SemaphoreType.DMA((self.n, 7)),
                        pltpu.SemaphoreType.DMA((self.n,))]

    def phase(self, p, ins, outs, sems):
        (self._gather if self.kind == "gather" else self._exchange)(p, ins, outs, sems)

    def _gather(self, p, ins, outs, sems):
        send_sems, recv_sems, local_sems = sems
        x, y, c = lax.axis_index("x"), lax.axis_index("y"), lax.axis_index("c")
        me, sibling = (x, y, c), (x, y, 1 - c)
        chips = [(1 - x, y), (x, 1 - y), (1 - x, 1 - y)]

        def copy(a, k, block, to, src=None):
            dst = outs[a].at[_slot(*block)]
            return pltpu.make_async_remote_copy(
                src_ref=dst if src is None else src, dst_ref=dst, send_sem=send_sems.at[a, k],
                recv_sem=recv_sems.at[a, k], device_id=to, device_id_type=MESH)

        mine = [pltpu.make_async_copy(ins[a], outs[a].at[_slot(*me)], local_sems.at[a]) for a in range(self.n)]
        first = [[copy(a, 0, me, sibling, src=ins[a])] +
                 [copy(a, 1 + j, me, (*chip, c), src=ins[a]) for j, chip in enumerate(chips)] for a in range(self.n)]
        passed = [[copy(a, 4 + j, (*chip, c), sibling) for j, chip in enumerate(chips)] for a in range(self.n)]
        if p == 0:
            for a in range(self.n):
                mine[a].start()
            for a in range(self.n):
                for cp in first[a]:
                    cp.start()
        elif p == 1:
            for a in range(self.n):
                for j, chip in enumerate(chips):
                    copy(a, 1 + j, (*chip, c), me).wait_recv()
                    passed[a][j].start()
        else:
            for a in range(self.n):
                copy(a, 0, sibling, me).wait_recv()
                for j, chip in enumerate(chips):
                    copy(a, 4 + j, (*chip, 1 - c), me).wait_recv()
            for a in range(self.n):
                for cp in first[a] + passed[a]:
                    cp.wait_send()
                mine[a].wait()

    def _exchange(self, p, ins, outs, sems):
        send_sems, recv_sems, local_sems = sems
        x, y, c = lax.axis_index("x"), lax.axis_index("y"), lax.axis_index("c")
        me = _slot(x, y, c)
        peers = [(x ^ (k >> 2), y ^ ((k >> 1) & 1), c ^ (k & 1)) for k in range(1, N_DEV)]

        def copy(a, k):
            peer = peers[k - 1]
            return pltpu.make_async_remote_copy(
                src_ref=ins[a].at[_slot(*peer)], dst_ref=outs[a].at[me], send_sem=send_sems.at[a, k - 1],
                recv_sem=recv_sems.at[a, k - 1], device_id=peer, device_id_type=MESH)

        def landed(a, k):
            peer = peers[k - 1]
            return pltpu.make_async_remote_copy(
                src_ref=ins[a].at[me], dst_ref=outs[a].at[_slot(*peer)], send_sem=send_sems.at[a, k - 1],
                recv_sem=recv_sems.at[a, k - 1], device_id=peer, device_id_type=MESH)

        local = [pltpu.make_async_copy(ins[a].at[me], outs[a].at[me], local_sems.at[a]) for a in range(self.n)]
        order = [6, 7, 2, 3, 4, 5, 1]
        if p == 0:
            for a in range(self.n):
                local[a].start()
            for a in range(self.n):
                for k in order:
                    copy(a, k).start()
        elif p == 2:
            for a in range(self.n):
                for k in order:
                    copy(a, k).wait_send()
                    landed(a, k).wait_recv()
                local[a].wait()

    def alone(self, name):
        n = self.n

        def body(*refs):
            for p in range(3):
                self.phase(p, refs[:n], refs[n:2 * n], refs[2 * n:])

        return pl.pallas_call(body, name=name, out_shape=tuple(self.out_shape), in_specs=self.in_specs,
                              out_specs=tuple(self.out_specs), scratch_shapes=self.scratch)(*self.arrs)


def hosted_call(body, comm, steps, *, name, grid, in_specs, out_specs, out_shape, scratch_shapes, args):
    if comm is None:
        outs = pl.pallas_call(body, name=name, grid=grid, in_specs=in_specs, out_specs=tuple(out_specs),
                              out_shape=tuple(out_shape), scratch_shapes=scratch_shapes,
                              compiler_params=_params(len(grid)))(*args)
        return outs, None
    ni, no, ns, cn = len(in_specs), len(out_specs), len(scratch_shapes), comm.n

    def both(*refs):
        h_in, c_in = refs[:ni], refs[ni:ni + cn]
        h_out, c_out = refs[ni + cn:ni + cn + no], refs[ni + cn + no:ni + 2 * cn + no]
        h_scr, c_scr = refs[ni + 2 * cn + no:ni + 2 * cn + no + ns], refs[ni + 2 * cn + no + ns:]
        when = steps()
        pl.when(when[0])(lambda: comm.phase(0, c_in, c_out, c_scr))
        body(*h_in, *h_out, *h_scr)
        pl.when(when[1])(lambda: comm.phase(1, c_in, c_out, c_scr))
        pl.when(when[2])(lambda: comm.phase(2, c_in, c_out, c_scr))

    outs = pl.pallas_call(
        both, name=name, grid=grid, in_specs=list(in_specs) + comm.in_specs,
        out_specs=tuple(out_specs) + tuple(comm.out_specs), out_shape=tuple(out_shape) + tuple(comm.out_shape),
        scratch_shapes=list(scratch_shapes) + comm.scratch, compiler_params=_params(len(grid)),
    )(*args, *comm.arrs)
    return outs[:no], outs[no:]


def _grid_steps(n_outer, n_inner=1):
    total = n_outer * n_inner

    def steps():
        t = pl.program_id(0) * n_inner + (pl.program_id(1) if n_inner > 1 else 0)
        return t == 0, t == (total * 5) // 8, t == total - 1
    return steps


def widen_slabs(arrs, ns, name):
    per = N_DEV // ns
    _, R, C = arrs[0].shape
    n = len(arrs)

    def body(*refs):
        for a in range(n):
            for k in range(per):
                refs[n + a][:, k * C:(k + 1) * C] = refs[a][k]

    return pl.pallas_call(
        body, name=name, grid=(ns,), in_specs=[pl.BlockSpec((per, R, C), lambda s: (s, 0, 0))] * n,
        out_specs=tuple(pl.BlockSpec((None, R, per * C), lambda s: (s, 0, 0)) for _ in range(n)),
        out_shape=tuple(SDS((ns, R, per * C), a.dtype) for a in arrs), compiler_params=_params(1))(*arrs)


def join_columns(blocks, n_main, name):
    nb, R, c8 = blocks.shape
    rest = nb * c8 - n_main
    tr = _row_tile(R, 256)

    def body(b_ref, main_ref, rest_ref, full):
        for k in range(nb):
            full[:, k * c8:(k + 1) * c8] = b_ref[k]
        main_ref[...] = full[:, :n_main]
        rest_ref[...] = jnp.zeros_like(rest_ref)
        rest_ref[:, :rest] = full[:, n_main:]

    return pl.pallas_call(
        body, name=name, grid=(R // tr,), in_specs=[pl.BlockSpec((nb, tr, c8), lambda i: (0, i, 0))],
        out_specs=(pl.BlockSpec((tr, n_main), lambda i: (i, 0)), pl.BlockSpec((tr, HEAD), lambda i: (i, 0))),
        out_shape=(SDS((R, n_main), blocks.dtype), SDS((R, HEAD), blocks.dtype)),
        scratch_shapes=[pltpu.VMEM((tr, nb * c8), blocks.dtype)], compiler_params=_params(1))(blocks)


def split_columns(main, rest, c8, name):
    R, n_main = main.shape
    nb = N_DEV
    n_rest = nb * c8 - n_main
    tr = _row_tile(R, 256)

    def body(main_ref, rest_ref, b_ref, full):
        full[:, :n_main] = main_ref[...]
        full[:, n_main:] = rest_ref[:, :n_rest]
        for k in range(nb):
            b_ref[k] = full[:, k * c8:(k + 1) * c8]

    return pl.pallas_call(
        body, name=name, grid=(R // tr,),
        in_specs=[pl.BlockSpec((tr, n_main), lambda i: (i, 0)), pl.BlockSpec((tr, HEAD), lambda i: (i, 0))],
        out_specs=pl.BlockSpec((nb, tr, c8), lambda i: (0, i, 0)), out_shape=SDS((nb, R, c8), main.dtype),
        scratch_shapes=[pltpu.VMEM((tr, nb * c8), main.dtype)], compiler_params=_params(1))(main, rest)


def ffn_fwd(x, gpre, gpost, wg, wu, wd, name, comm=None):
    S, D = x.shape
    nj, F8 = wg.shape[0], wg.shape[-1]
    tm = _row_tile(S, FFN_ROWS_FWD)

    def body(x_ref, gpre_ref, gpost_ref, wg_ref, wu_ref, wd_ref, xo_ref, h_ref, p_ref, q_ref, t_ref, y_ref):
        j = pl.program_id(1)

        @pl.when(j == 0)
        def _():
            h_ref[...] = _rms(x_ref[...], gpre_ref[...]).astype(h_ref.dtype)
            y_ref[...] = jnp.zeros_like(y_ref)

        h = h_ref[...]
        a = _mm(h, wg_ref[...])
        b = _mm(h, wu_ref[...])
        s = jax.nn.sigmoid(a)
        q = a * s
        p_ref[...] = (b * (s + q * (1.0 - s))).astype(p_ref.dtype)
        q_ref[...] = q.astype(q_ref.dtype)
        t = (q * b).astype(t_ref.dtype)
        t_ref[...] = t
        y_ref[...] += _mm(t, wd_ref[...])

        @pl.when(j == nj - 1)
        def _():
            xo_ref[...] = x_ref[...] + 0.5 * _rms(y_ref[...], gpost_ref[...])

    row = pl.BlockSpec((tm, D), lambda i, j: (i, 0))
    vec = pl.BlockSpec((1, D), lambda i, j: (0, 0))
    wcol = pl.BlockSpec((None, D, F8), lambda i, j: (j, 0, 0))
    wrow = pl.BlockSpec((None, F8, D), lambda i, j: (j, 0, 0))
    hid = pl.BlockSpec((None, tm, F8), lambda i, j: (j, i, 0))
    return hosted_call(
        body, comm, _grid_steps(S // tm, nj), name=name, grid=(S // tm, nj),
        in_specs=[row, vec, vec, wcol, wcol, wrow],
        out_specs=(row, row, hid, hid, hid, row),
        out_shape=(SDS((S, D), f32), SDS((S, D), MXU_DTYPE), SDS((nj, S, F8), MXU_DTYPE),
                   SDS((nj, S, F8), MXU_DTYPE), SDS((nj, S, F8), MXU_DTYPE), SDS((S, D), f32)),
        scratch_shapes=[], args=(x, gpre, gpost, wg, wu, wd))


def ffn_bwd_dx(dxo, x, y, p, q, gpre, gpost, wg, wu, wd, name, comm=None):
    S, D = x.shape
    NS, F8 = wg.shape[0], wg.shape[-1]
    sps = 2 if NS % 2 == 0 else 1
    nj = NS // sps
    tm = _row_tile(S, FFN_ROWS_BWD)

    def body(dxo_ref, x_ref, y_ref, p_ref, q_ref, gpre_ref, gpost_ref, wg_ref, wu_ref, wd_ref,
             dx_ref, da_ref, db_ref, dy_ref, dgpre_ref, dgpost_ref, dh_ref):
        i, j = pl.program_id(0), pl.program_id(1)

        @pl.when(j == 0)
        def _():
            @pl.when(i == 0)
            def _():
                dgpre_ref[...] = jnp.zeros_like(dgpre_ref)
                dgpost_ref[...] = jnp.zeros_like(dgpost_ref)

            dy, dg = _rms_bwd(y_ref[...], gpost_ref[...], 0.5 * dxo_ref[...])
            dy_ref[...] = dy.astype(dy_ref.dtype)
            dgpost_ref[...] += dg
            dh_ref[...] = jnp.zeros_like(dh_ref)

        half = tm // 2 if tm % 16 == 0 else tm
        for r0 in range(0, tm, half):
            rs = slice(r0, r0 + half)
            upd = None
            for s in range(sps):
                dt = _mm_nt(dy_ref[rs, :], wd_ref[s])
                da = (dt * p_ref[s, rs, :].astype(f32)).astype(da_ref.dtype)
                db = (dt * q_ref[s, rs, :].astype(f32)).astype(db_ref.dtype)
                da_ref[s, rs, :] = da
                db_ref[s, rs, :] = db
                part = _mm_nt(da, wg_ref[s]) + _mm_nt(db, wu_ref[s])
                upd = part if upd is None else upd + part
            dh_ref[rs, :] += upd

        @pl.when(j == nj - 1)
        def _():
            dxx, dg = _rms_bwd(x_ref[...], gpre_ref[...], dh_ref[...])
            dx_ref[...] = dxo_ref[...] + dxx
            dgpre_ref[...] += dg

    row = pl.BlockSpec((tm, D), lambda i, j: (i, 0))
    vec = pl.BlockSpec((1, D), lambda i, j: (0, 0))
    wcol = pl.BlockSpec((sps, D, F8), lambda i, j: (j, 0, 0))
    wrow = pl.BlockSpec((sps, F8, D), lambda i, j: (j, 0, 0))
    hid = pl.BlockSpec((sps, tm, F8), lambda i, j: (j, i, 0))
    return hosted_call(
        body, comm, _grid_steps(S // tm, nj), name=name, grid=(S // tm, nj),
        in_specs=[row, row, row, hid, hid, vec, vec, wcol, wcol, wrow],
        out_specs=(row, hid, hid, row, vec, vec),
        out_shape=(SDS((S, D), f32), SDS((NS, S, F8), MXU_DTYPE), SDS((NS, S, F8), MXU_DTYPE),
                   SDS((S, D), MXU_DTYPE), SDS((1, D), f32), SDS((1, D), f32)),
        scratch_shapes=[pltpu.VMEM((tm, D), f32)], args=(dxo, x, y, p, q, gpre, gpost, wg, wu, wd))


def ffn_bwd_dw(h, dy, t, da, db, name, comm=None):
    S, D = h.shape
    NS, F8 = t.shape[0], t.shape[-1]
    per = N_DEV // NS
    w8 = F8 // per
    tm = _row_tile(S, FFN_ROWS_DW)
    ni = S // tm

    def body(h_ref, dy_ref, t_ref, da_ref, db_ref, dwg_ref, dwu_ref, dwd_ref, accg, accu, accd):
        i = pl.program_id(1)

        @pl.when(i == 0)
        def _():
            accg[...] = jnp.zeros_like(accg)
            accu[...] = jnp.zeros_like(accu)
            accd[...] = jnp.zeros_like(accd)

        hh = h_ref[...]
        accg[...] += _mm_tn(hh, da_ref[...])
        accu[...] += _mm_tn(hh, db_ref[...])
        accd[...] += _mm_tn(t_ref[...], dy_ref[...])

        @pl.when(i == ni - 1)
        def _():
            for k in range(per):
                ks = slice(k * w8, (k + 1) * w8)
                dwg_ref[k] = accg[:, ks].astype(dwg_ref.dtype)
                dwu_ref[k] = accu[:, ks].astype(dwu_ref.dtype)
                dwd_ref[k] = accd[ks, :].astype(dwd_ref.dtype)

    row = pl.BlockSpec((tm, D), lambda j, i: (i, 0))
    hid = pl.BlockSpec((None, tm, F8), lambda j, i: (j, i, 0))
    wcol = pl.BlockSpec((per, D, w8), lambda j, i: (j, 0, 0))
    wrow = pl.BlockSpec((per, w8, D), lambda j, i: (j, 0, 0))
    return hosted_call(
        body, comm, _grid_steps(NS, ni), name=name, grid=(NS, ni),
        in_specs=[row, row, hid, hid, hid],
        out_specs=(wcol, wcol, wrow),
        out_shape=(SDS((N_DEV, D, w8), MXU_DTYPE), SDS((N_DEV, D, w8), MXU_DTYPE), SDS((N_DEV, w8, D), MXU_DTYPE)),
        scratch_shapes=[pltpu.VMEM((D, F8), f32), pltpu.VMEM((D, F8), f32), pltpu.VMEM((F8, D), f32)],
        args=(h, dy, t, da, db))


def ffn_bwd_dw_one(rows_op, slab_op, hidden_rows, name, comm=None):
    S, D = rows_op.shape
    NS, F8 = slab_op.shape[0], slab_op.shape[-1]
    per = N_DEV // NS
    w8 = F8 // per
    tm = _row_tile(S, FFN_ROWS_DW)
    ni = S // tm

    def body(r_ref, s_ref, o_ref, acc):
        i = pl.program_id(1)

        @pl.when(i == 0)
        def _():
            acc[...] = jnp.zeros_like(acc)

        acc[...] += _mm_tn(s_ref[...], r_ref[...]) if hidden_rows else _mm_tn(r_ref[...], s_ref[...])

        @pl.when(i == ni - 1)
        def _():
            for k in range(per):
                ks = slice(k * w8, (k + 1) * w8)
                o_ref[k] = (acc[ks, :] if hidden_rows else acc[:, ks]).astype(o_ref.dtype)

    blk = (per, w8, D) if hidden_rows else (per, D, w8)
    return hosted_call(
        body, comm, _grid_steps(NS, ni), name=name, grid=(NS, ni),
        in_specs=[pl.BlockSpec((tm, D), lambda j, i: (i, 0)), pl.BlockSpec((None, tm, F8), lambda j, i: (j, i, 0))],
        out_specs=(pl.BlockSpec(blk, lambda j, i: (j, 0, 0)),),
        out_shape=(SDS((N_DEV,) + blk[1:], MXU_DTYPE),),
        scratch_shapes=[pltpu.VMEM((F8, D) if hidden_rows else (D, F8), f32)], args=(rows_op, slab_op))


def rms_mm(x, g, w, w2, name, tn=1024):
    S, D = x.shape
    N = w.shape[1]
    tm = _row_tile(S, PROJ_ROWS)
    tn = _row_tile(N, tn)
    has2 = w2 is not None

    def body(*refs):
        if has2:
            x_ref, g_ref, w_ref, w2_ref, h_ref, o_ref, o2_ref = refs
        else:
            x_ref, g_ref, w_ref, h_ref, o_ref = refs
        j = pl.program_id(1)

        @pl.when(j == 0)
        def _():
            h = _rms(x_ref[...], g_ref[...]).astype(h_ref.dtype)
            h_ref[...] = h
            if has2:
                o2_ref[...] = _mm(h, w2_ref[...])

        o_ref[...] = _mm(h_ref[...], w_ref[...])

    row = pl.BlockSpec((tm, D), lambda i, j: (i, 0))
    in_specs = [row, pl.BlockSpec((1, D), lambda i, j: (0, 0)), pl.BlockSpec((D, tn), lambda i, j: (0, j))]
    out_specs = [row, pl.BlockSpec((tm, tn), lambda i, j: (i, j))]
    out_shape = [SDS((S, D), MXU_DTYPE), SDS((S, N), f32)]
    args = [x, g, w]
    if has2:
        in_specs.append(pl.BlockSpec((D, w2.shape[1]), lambda i, j: (0, 0)))
        out_specs.append(pl.BlockSpec((tm, w2.shape[1]), lambda i, j: (i, 0)))
        out_shape.append(SDS((S, w2.shape[1]), f32))
        args.append(w2)
    return pl.pallas_call(
        body, name=name, grid=(S // tm, N // tn), in_specs=in_specs, out_specs=tuple(out_specs),
        out_shape=tuple(out_shape), compiler_params=_params(2),
    )(*args)


def mm_bwd_dx(dres, x, g, dy, w, dy2, w2, name, tk=1024, comm=None):
    S, D = x.shape
    K = dy.shape[1]
    tm = _row_tile(S, PROJ_ROWS)
    tk = _row_tile(K, tk)
    nk = K // tk
    has2 = dy2 is not None

    def body(*refs):
        if has2:
            dres_ref, x_ref, g_ref, dy_ref, w_ref, dy2_ref, w2_ref, dx_ref, dg_ref, dh_ref = refs
        else:
            dres_ref, x_ref, g_ref, dy_ref, w_ref, dx_ref, dg_ref, dh_ref = refs
        i, k = pl.program_id(0), pl.program_id(1)

        @pl.when(k == 0)
        def _():
            @pl.when(i == 0)
            def _():
                dg_ref[...] = jnp.zeros_like(dg_ref)

            if has2:
                dh_ref[...] = _mm_nt(dy2_ref[...], w2_ref[...])
            else:
                dh_ref[...] = jnp.zeros_like(dh_ref)

        dh_ref[...] += _mm_nt(dy_ref[...], w_ref[...])

        @pl.when(k == nk - 1)
        def _():
            dxx, dg = _rms_bwd(x_ref[...], g_ref[...], dh_ref[...])
            dx_ref[...] = dres_ref[...] + dxx
            dg_ref[...] += dg

    row = pl.BlockSpec((tm, D), lambda i, k: (i, 0))
    vec = pl.BlockSpec((1, D), lambda i, k: (0, 0))
    in_specs = [row, row, vec, pl.BlockSpec((tm, tk), lambda i, k: (i, k)), pl.BlockSpec((D, tk), lambda i, k: (0, k))]
    args = [dres, x, g, dy, w]
    if has2:
        in_specs += [pl.BlockSpec((tm, dy2.shape[1]), lambda i, k: (i, 0)),
                     pl.BlockSpec((D, w2.shape[1]), lambda i, k: (0, 0))]
        args += [dy2, w2]
    return hosted_call(
        body, comm, _grid_steps(S // tm, nk), name=name, grid=(S // tm, nk), in_specs=in_specs, out_specs=(row, vec),
        out_shape=(SDS((S, D), f32), SDS((1, D), f32)), scratch_shapes=[pltpu.VMEM((tm, D), f32)], args=args)


def tn_mm(a, b, name, tn=512, slot_major=False):
    S, K1 = a.shape
    N = b.shape[1]
    tm = _row_tile(S, TN_ROWS)
    tn = _row_tile(N, tn)
    ni = S // tm

    def body(a_ref, b_ref, o_ref, acc):
        i = pl.program_id(1)

        @pl.when(i == 0)
        def _():
            acc[...] = jnp.zeros_like(acc)

        acc[...] += _mm_tn(a_ref[...], b_ref[...])

        @pl.when(i == ni - 1)
        def _():
            o_ref[...] = acc[...].astype(o_ref.dtype)

    if slot_major:
        out_spec, out_shape = pl.BlockSpec((None, K1, tn), lambda j, i: (j, 0, 0)), SDS((N // tn, K1, tn), MXU_DTYPE)
    else:
        out_spec, out_shape = pl.BlockSpec((K1, tn), lambda j, i: (0, j)), SDS((K1, N), MXU_DTYPE)
    return pl.pallas_call(
        body, name=name, grid=(N // tn, ni),
        in_specs=[pl.BlockSpec((tm, K1), lambda j, i: (i, 0)), pl.BlockSpec((tm, tn), lambda j, i: (i, j))],
        out_specs=out_spec, out_shape=out_shape,
        scratch_shapes=[pltpu.VMEM((K1, tn), f32)], compiler_params=_params(2),
    )(a, b)


CONV_ROWS = 512


def _shift_down(cur, prev8, s):
    r = pltpu.roll(cur, s, 0)
    row = lax.broadcasted_iota(jnp.int32, (8, cur.shape[1]), 0)
    top = jnp.where(row < s, pltpu.roll(prev8, s, 0), r[0:8])
    return jnp.concatenate([top, r[8:]], axis=0)


def _shift_up(cur, next8, s):
    n = cur.shape[0]
    r = pltpu.roll(cur, n - s, 0)
    row = lax.broadcasted_iota(jnp.int32, (8, cur.shape[1]), 0)
    bot = jnp.where(row >= 8 - s, pltpu.roll(next8, 8 - s, 0), r[n - 8:])
    return jnp.concatenate([r[:n - 8], bot], axis=0)


def _conv_taps(cur, prev8):
    return [_shift_down(cur, prev8, 3), _shift_down(cur, prev8, 2), _shift_down(cur, prev8, 1), cur]


def _act_qk(c):
    a = _silu(c)
    return a * lax.rsqrt(jnp.sum(a * a, axis=-1, keepdims=True) + L2_EPS)


def dn_prep(proj, conv_w, name):
    S = proj.shape[0]
    W = conv_w.shape[1] // 3
    nh = W // HEAD
    R = _row_tile(S, CONV_ROWS)

    def body(p_ref, w_ref, o_ref):
        j = pl.program_id(0)
        w = w_ref[...]

        def rows(r, prev8):
            cur = p_ref[pl.ds(r, R), :]
            taps = _conv_taps(cur, prev8)
            cv = taps[0] * w[0:1] + taps[1] * w[1:2] + taps[2] * w[2:3] + taps[3] * w[3:4]

            @pl.when(j < 2 * nh)
            def _():
                o_ref[pl.ds(r, R), :] = _act_qk(cv)

            @pl.when(j >= 2 * nh)
            def _():
                o_ref[pl.ds(r, R), :] = _silu(cv)

        rows(0, jnp.zeros((8, HEAD), f32))

        @pl.loop(1, S // R)
        def _(t):
            r = pl.multiple_of(t * R, R)
            rows(r, p_ref[pl.ds(r - 8, 8), :])

    return pl.pallas_call(
        body, name=name, grid=(3 * nh,),
        in_specs=[pl.BlockSpec((S, HEAD), lambda j: (0, j)), pl.BlockSpec((CONV_K, HEAD), lambda j: (0, j))],
        out_specs=pl.BlockSpec((None, S, HEAD), lambda j: (j // nh, 0, j % nh)),
        out_shape=SDS((3, S, W), f32), compiler_params=_params(1),
    )(proj, conv_w)


def dn_prep_bwd(proj, conv_w, dqkv, dz, name, comm=None):
    S = proj.shape[0]
    W = conv_w.shape[1] // 3
    nh = W // HEAD
    nq = 3 * nh
    R = _row_tile(S, CONV_ROWS)
    nr = S // R

    def body(p_ref, w_ref, dq_ref, dz_ref, dp_ref, dw_ref, dc_ref):
        j = pl.program_id(0)

        @pl.when(j >= nq)
        def _():
            dp_ref[...] = dz_ref[...].astype(dp_ref.dtype)

        @pl.when(j < nq)
        def _():
            w = w_ref[...]
            dw_ref[...] = jnp.zeros_like(dw_ref)

            def rows(r, prev8):
                cur = p_ref[pl.ds(r, R), :]
                taps = _conv_taps(cur, prev8)
                cv = taps[0] * w[0:1] + taps[1] * w[1:2] + taps[2] * w[2:3] + taps[3] * w[3:4]
                dn = dq_ref[pl.ds(r, R), :]

                @pl.when(j < 2 * nh)
                def _():
                    dc_ref[pl.ds(r, R), :] = jax.vjp(_act_qk, cv)[1](dn)[0]

                @pl.when(j >= 2 * nh)
                def _():
                    dc_ref[pl.ds(r, R), :] = jax.vjp(_silu, cv)[1](dn)[0]

                dc = dc_ref[pl.ds(r, R), :]
                dw_ref[...] += jnp.concatenate(
                    [jnp.sum(dc * taps[q], axis=0, keepdims=True) for q in range(CONV_K)], axis=0)

            rows(0, jnp.zeros((8, HEAD), f32))

            @pl.loop(1, nr)
            def _(t):
                r = pl.multiple_of(t * R, R)
                rows(r, p_ref[pl.ds(r - 8, 8), :])

            def back(r, next8):
                dc = dc_ref[pl.ds(r, R), :]
                dx = dc * w[3:4]
                for s in (1, 2, 3):
                    dx = dx + _shift_up(dc, next8, s) * w[3 - s:4 - s]
                dp_ref[pl.ds(r, R), :] = dx.astype(dp_ref.dtype)

            @pl.loop(0, nr - 1)
            def _(t):
                r = pl.multiple_of(t * R, R)
                back(r, dc_ref[pl.ds(r + R, 8), :])

            back((nr - 1) * R, jnp.zeros((8, HEAD), f32))

    clamp = lambda j: jnp.minimum(j, nq - 1)
    return hosted_call(
        body, comm, _grid_steps(4 * nh), name=name, grid=(4 * nh,),
        in_specs=[pl.BlockSpec((S, HEAD), lambda j: (0, clamp(j))),
                  pl.BlockSpec((CONV_K, HEAD), lambda j: (0, clamp(j))),
                  pl.BlockSpec((None, S, HEAD), lambda j: (clamp(j) // nh, 0, clamp(j) % nh)),
                  pl.BlockSpec((S, HEAD), lambda j: (0, jnp.maximum(j - nq, 0)))],
        out_specs=(pl.BlockSpec((S, HEAD), lambda j: (0, j)), pl.BlockSpec((CONV_K, HEAD), lambda j: (0, clamp(j)))),
        out_shape=(SDS((S, 4 * W), MXU_DTYPE), SDS((CONV_K, 3 * W), f32)),
        scratch_shapes=[pltpu.VMEM((S, HEAD), f32)], args=(proj, conv_w, dqkv, dz))


def _lane_pick(x, lane):
    sel = lax.broadcasted_iota(jnp.int32, x.shape, 1) == lane
    return jnp.broadcast_to(jnp.sum(jnp.where(sel, x, 0.0), axis=1, keepdims=True), x.shape)


CUM_ROWS = 256


def _sel_mm(m01, x):
    m = _c(m01)
    d = lambda p: lax.dot_general(m, p, (NN, ((), ())), preferred_element_type=f32)
    h1, h2, h3 = _pieces3(x)
    return (d(h1) + d(h2)) + d(h3)


def _chunk_cumsum_matrix(n, transpose):
    r, c = lax.broadcasted_iota(jnp.int32, (n, n), 0), lax.broadcasted_iota(jnp.int32, (n, n), 1)
    sh = int(math.log2(DN_CHUNK))
    same = (r >> sh) == (c >> sh)
    return jnp.where(same & ((r <= c) if transpose else (r >= c)), 1.0, 0.0).astype(f32)


def _gates_by_lane(H, p, al, dt):
    lane = lax.broadcasted_iota(jnp.int32, p.shape, 1)
    g = -jnp.exp(al) * jax.nn.softplus(p + dt)
    return jnp.where(lane < H, jax.nn.sigmoid(p), jnp.where(lane < 2 * H, g, 0.0))


def dn_gates(pba, al, dt, H, name):
    S = pba.shape[0]
    R = _row_tile(S, CUM_ROWS)

    def body(p_ref, al_ref, dt_ref, o_ref):
        raw = _gates_by_lane(H, p_ref[...], al_ref[...], dt_ref[...])
        lane = lax.broadcasted_iota(jnp.int32, raw.shape, 1)
        o_ref[...] = jnp.where(lane < H, raw, _sel_mm(_chunk_cumsum_matrix(R, False), raw))

    blk = pl.BlockSpec((R, HEAD), lambda i: (i, 0))
    par = pl.BlockSpec((1, HEAD), lambda i: (0, 0))
    return pl.pallas_call(body, name=name, grid=(S // R,), in_specs=[blk, par, par], out_specs=blk,
                          out_shape=SDS((S, HEAD), f32), compiler_params=_params(1))(pba, al, dt)


def dn_gates_bwd(pba, al, dt, dgates, H, name):
    S = pba.shape[0]
    R = _row_tile(S, CUM_ROWS)

    def body(p_ref, al_ref, dt_ref, dg_ref, dp_ref, dal_ref, ddt_ref):
        @pl.when(pl.program_id(0) == 0)
        def _():
            dal_ref[...] = jnp.zeros_like(dal_ref)
            ddt_ref[...] = jnp.zeros_like(ddt_ref)

        d = dg_ref[...]
        lane = lax.broadcasted_iota(jnp.int32, d.shape, 1)
        d = jnp.where(lane < H, d, _sel_mm(_chunk_cumsum_matrix(R, True), d))
        _, vjp = jax.vjp(functools.partial(_gates_by_lane, H), p_ref[...], al_ref[...], dt_ref[...])
        dp, dal, ddt = vjp(d)
        dp_ref[...] = dp.astype(dp_ref.dtype)
        dal_ref[...] += dal
        ddt_ref[...] += ddt

    blk = pl.BlockSpec((R, HEAD), lambda i: (i, 0))
    par = pl.BlockSpec((1, HEAD), lambda i: (0, 0))
    return pl.pallas_call(
        body, name=name, grid=(S // R,), in_specs=[blk, par, par, blk], out_specs=(blk, par, par),
        out_shape=(SDS((S, HEAD), MXU_DTYPE), SDS((1, HEAD), f32), SDS((1, HEAD), f32)), compiler_params=_params(1),
    )(pba, al, dt, dgates)


def _bdot(dims):
    back = {NN: ((NT, 'gb'), (TN, 'ag')), NT: ((NN, 'gb'), (TN, 'ga')), TN: ((NT, 'bg'), (NN, 'ag'))}[dims]
    d = lambda p, q, dm: lax.dot_general(_c(p), _c(q), (dm, ((), ())), preferred_element_type=f32)

    @jax.custom_vjp
    def f(a, b):
        return d(a, b, dims)

    def fwd(a, b):
        return d(a, b, dims), (a, b)

    def bwd(res, g):
        v = {'a': res[0], 'b': res[1], 'g': g}
        (da_dims, da_ops), (db_dims, db_ops) = back
        return d(v[da_ops[0]], v[da_ops[1]], da_dims), d(v[db_ops[0]], v[db_ops[1]], db_dims)

    f.defvjp(fwd, bwd)
    return f, lambda a, b: d(a, b, dims)


_BDOT = {dims: _bdot(dims) for dims in (NN, NT, TN)}


def _tri_inv_multi(Ls):
    n = Ls[0].shape[0]
    eye = jnp.where(lax.broadcasted_iota(jnp.int32, (n, n), 0) == lax.broadcasted_iota(jnp.int32, (n, n), 1), 1.0, 0.0)
    P = tuple(-L for L in Ls)
    T = tuple(eye + p for p in P)
    for _ in range(int(math.log2(n)) - 1):
        P = tuple(_dot3(p, p, NN) for p in P)
        T = tuple(t + _dot3(t, p, NN) for t, p in zip(T, P))
    return T


@jax.custom_vjp
def _tri_inv_kept(Ls, Ts):
    return Ts


def _tri_inv_kept_bwd(T, dT):
    X = tuple(_dot3(d, t, NT) for d, t in zip(dT, T))
    return tuple(-_dot3(t, x, TN) for t, x in zip(T, X)), tuple(jnp.zeros_like(t) for t in T)


_tri_inv_kept.defvjp(lambda Ls, Ts: (Ts, Ts), _tri_inv_kept_bwd)


def _pieces3(x):
    h1 = x.astype(MXU_DTYPE)
    r1 = x - h1.astype(f32)
    h2 = r1.astype(MXU_DTYPE)
    return h1, h2, (r1 - h2.astype(f32)).astype(MXU_DTYPE)


def _row_bcast_impl(sel_row, gc):
    s = _c(sel_row)
    d = lambda p: lax.dot_general(s, p, (NT, ((), ())), preferred_element_type=f32)
    h1, h2, h3 = _pieces3(gc)
    return (d(h1) + d(h2)) + d(h3)


def _row_bcast_bwd(sel_row, d):
    s = _c(sel_row)
    hi, lo = _split(d)
    t = lambda p: lax.dot_general(p, s, (TN, ((), ())), preferred_element_type=f32)
    return jnp.zeros_like(sel_row), t(hi) + t(lo)


_row_bcast = jax.custom_vjp(_row_bcast_impl)
_row_bcast.defvjp(lambda sel_row, gc: (_row_bcast_impl(sel_row, gc), sel_row), _row_bcast_bwd)


def _col_bcast_impl(gc):
    return gc[:, :DN_CHUNK]


def _col_bcast_bwd(_, d):
    return (jnp.broadcast_to(jnp.sum(d, axis=1, keepdims=True) * (1.0 / HEAD), (d.shape[0], HEAD)),)


_col_bcast = jax.custom_vjp(_col_bcast_impl)
_col_bcast.defvjp(lambda gc: (_col_bcast_impl(gc), None), _col_bcast_bwd)


def _last_row_bcast(n):
    def impl(gc):
        return jnp.broadcast_to(gc[DN_CHUNK - 1:DN_CHUNK, :], (n, HEAD))

    def bwd(_, d):
        row = lax.broadcasted_iota(jnp.int32, (DN_CHUNK, HEAD), 0)
        return (jnp.where(row == DN_CHUNK - 1, jnp.sum(d, axis=0, keepdims=True), 0.0),)

    f = jax.custom_vjp(impl)
    f.defvjp(lambda gc: (impl(gc), None), bwd)
    return impl, f


_LAST_C, _LAST_H = _last_row_bcast(DN_CHUNK), _last_row_bcast(HEAD)


def _halves(axis):
    def impl(x):
        n = x.shape[axis] // 2
        return lax.slice_in_dim(x, 0, n, axis=axis), lax.slice_in_dim(x, n, 2 * n, axis=axis)

    f = jax.custom_vjp(impl)
    f.defvjp(lambda x: (impl(x), None), lambda _, g: (jnp.concatenate(g, axis=axis),))
    return impl, f


_ROW_HALVES, _COL_HALVES = _halves(0), _halves(1)


def _chunk_consts():
    C = DN_CHUNK
    io = lambda shape, ax: lax.broadcasted_iota(jnp.int32, shape, ax)
    one = lambda m: jnp.where(m, 1.0, 0.0).astype(f32)
    r, c = io((C, C), 0), io((C, C), 1)
    return dict(causal=r >= c, strict=r > c, sel_row=one(io((C, HEAD), 1) == 0))


def _chunk_fn(kc, kept_T, q, k, v, gc, bB, S0):
    diff = kept_T is not None
    i = 0 if diff else 1
    mm, mm_nt, mm_tn = _BDOT[NN][i], _BDOT[NT][i], _BDOT[TN][i]
    tri = (lambda Ls: _tri_inv_kept(Ls, kept_T)) if diff else _tri_inv_multi
    each = lambda f, *ls: tuple(f(*a) for a in zip(*ls))
    gcol = each(_col_bcast if diff else _col_bcast_impl, gc)
    grow = each(lambda g: (_row_bcast if diff else _row_bcast_impl)(kc['sel_row'], g), gc)
    glc = each(_LAST_C[i ^ 1], gc)
    glh = each(_LAST_H[i ^ 1], gc)
    decay = each(lambda a, b: jnp.where(kc['causal'], jnp.exp(jnp.where(kc['causal'], a - b, 0.0)), 0.0), gcol, grow)
    rows, cols = _ROW_HALVES[i ^ 1], _COL_HALVES[i ^ 1]
    first, second = (lambda ts: tuple(t[0] for t in ts)), (lambda ts: tuple(t[1] for t in ts))
    kb = each(lambda a, b: a * b, k, bB)
    vb = each(lambda a, b: a * b, v, bB)
    egc = each(jnp.exp, gc)
    qs = each(lambda a: a * (HEAD ** -0.5), q)
    kq = each(lambda a, b, kt: rows(mm_nt(jnp.concatenate([a, b], axis=0), kt)), kb, qs, k)
    kk, qk = first(kq), second(kq)
    T = tri(each(lambda a, d: jnp.where(kc['strict'], a * d, 0.0), kk, decay))
    uw = each(lambda t, a, b, e: cols(mm(t, jnp.concatenate([a, b * e], axis=1))), T, vb, kb, egc)
    u, w = first(uw), second(uw)
    attn = each(lambda a, d: jnp.where(kc['causal'], a * d, 0.0), qk, decay)
    wq = each(lambda a, b, e, s: rows(mm(jnp.concatenate([a, b * e], axis=0), s)), w, qs, egc, S0)
    wS, qS = first(wq), second(wq)
    v_new = each(lambda a, b: a - b, u, wS)
    o = each(lambda a, b: a + b, qS, each(mm, attn, v_new))
    kdec = each(lambda a, gl, g: a * jnp.exp(gl - g), k, glc, gc)
    S1 = each(lambda s, gl, kv: s * jnp.exp(gl) + kv, S0, glh, each(mm_tn, kdec, v_new))
    return (o, S1) if diff else (o, S1, T)


def _chunks_per_step(N):
    return 4 if N % 4 == 0 else (2 if N % 2 == 0 else 1)


def _heads_per_block(H):
    return 8 if H % 8 == 0 else (4 if H % 4 == 0 else 1)


def dn_chunk_fwd(qkv, gates, name, comm=None):
    _, S, W = qkv.shape
    H, C = W // HEAD, DN_CHUNK
    N, HB = S // C, _heads_per_block(H)
    assert HB == H
    CPS = _chunks_per_step(N)

    def body(q_ref, k_ref, v_ref, g_ref, o_ref, st_ref, t_ref, s_scr):
        @pl.when(pl.program_id(1) == 0)
        def _():
            s_scr[...] = jnp.zeros_like(s_scr)

        kc = _chunk_consts()
        sls = [slice(hh * HEAD, (hh + 1) * HEAD) for hh in range(HB)]
        St = tuple(s_scr[hh] for hh in range(HB))
        for c in range(CPS):
            rows = slice(c * C, (c + 1) * C)
            heads = lambda ref: tuple(ref[rows, sl] for sl in sls)
            gr = g_ref[rows, :]
            for hh in range(HB):
                st_ref[c, hh] = St[hh]
            o, St, T = _chunk_fn(kc, None, heads(q_ref), heads(k_ref), heads(v_ref),
                                 tuple(_lane_pick(gr, H + hh) for hh in range(HB)),
                                 tuple(_lane_pick(gr, hh) for hh in range(HB)), St)
            for hh in range(HB):
                o_ref[rows, sls[hh]] = o[hh]
                t_ref[c, hh] = T[hh]
        for hh in range(HB):
            s_scr[hh] = St[hh]

    part = lambda p: pl.BlockSpec((None, CPS * C, HB * HEAD), lambda hb, n: (p, n, hb))
    return hosted_call(
        body, comm, _grid_steps(H // HB, N // CPS), name=name, grid=(H // HB, N // CPS),
        in_specs=[part(0), part(1), part(2), pl.BlockSpec((CPS * C, HEAD), lambda hb, n: (n, 0))],
        out_specs=(pl.BlockSpec((CPS * C, HB * HEAD), lambda hb, n: (n, hb)),
                   pl.BlockSpec((CPS, HB, HEAD, HEAD), lambda hb, n: (n, hb, 0, 0)),
                   pl.BlockSpec((CPS, HB, C, C), lambda hb, n: (n, hb, 0, 0))),
        out_shape=(SDS((S, W), f32), SDS((N, H, HEAD, HEAD), f32), SDS((N, H, C, C), f32)),
        scratch_shapes=[pltpu.VMEM((HB, HEAD, HEAD), f32)], args=(qkv, qkv, qkv, gates))


def dn_chunk_bwd(qkv, gates, states, kept_T, do, name, comm=None):
    _, S, W = qkv.shape
    H, C = W // HEAD, DN_CHUNK
    N, HB = S // C, _heads_per_block(H)
    assert HB == H
    CPS = _chunks_per_step(N)
    NB = N // CPS

    def body(q_ref, k_ref, v_ref, g_ref, st_ref, t_ref, do_ref, dqkv_ref, dg_ref, ds_scr):
        @pl.when(pl.program_id(1) == 0)
        def _():
            ds_scr[...] = jnp.zeros_like(ds_scr)

        kc = _chunk_consts()
        sls = [slice(hh * HEAD, (hh + 1) * HEAD) for hh in range(HB)]
        dSt = tuple(ds_scr[hh] for hh in range(HB))
        for c in reversed(range(CPS)):
            rows = slice(c * C, (c + 1) * C)
            heads = lambda ref: tuple(ref[rows, sl] for sl in sls)
            gr = g_ref[rows, :]
            kept = tuple(t_ref[c, hh] for hh in range(HB))
            _, vjp = jax.vjp(functools.partial(_chunk_fn, kc, kept), heads(q_ref), heads(k_ref), heads(v_ref),
                             tuple(_lane_pick(gr, H + hh) for hh in range(HB)),
                             tuple(_lane_pick(gr, hh) for hh in range(HB)), tuple(st_ref[c, hh] for hh in range(HB)))
            dq, dk, dv, dg, db, dSt = vjp((heads(do_ref), dSt))
            lane = lax.broadcasted_iota(jnp.int32, (C, HEAD), 1)
            dgr = jnp.zeros((C, HEAD), f32)
            for hh in range(HB):
                dqkv_ref[0, rows, sls[hh]] = dq[hh]
                dqkv_ref[1, rows, sls[hh]] = dk[hh]
                dqkv_ref[2, rows, sls[hh]] = dv[hh]
                dgr = dgr + jnp.where(lane == hh, jnp.sum(db[hh], axis=1, keepdims=True), 0.0)
                dgr = dgr + jnp.where(lane == H + hh, jnp.sum(dg[hh], axis=1, keepdims=True), 0.0)
            dg_ref[rows, :] = dgr
        for hh in range(HB):
            ds_scr[hh] = dSt[hh]

    rev = lambda n: NB - 1 - n
    part = lambda p: pl.BlockSpec((None, CPS * C, HB * HEAD), lambda hb, n: (p, rev(n), hb))
    gate = pl.BlockSpec((CPS * C, HEAD), lambda hb, n: (rev(n), 0))
    return hosted_call(
        body, comm, _grid_steps(H // HB, NB), name=name, grid=(H // HB, NB),
        in_specs=[part(0), part(1), part(2), gate,
                  pl.BlockSpec((CPS, HB, HEAD, HEAD), lambda hb, n: (rev(n), hb, 0, 0)),
                  pl.BlockSpec((CPS, HB, C, C), lambda hb, n: (rev(n), hb, 0, 0)),
                  pl.BlockSpec((CPS * C, HB * HEAD), lambda hb, n: (rev(n), hb))],
        out_specs=(pl.BlockSpec((3, CPS * C, HB * HEAD), lambda hb, n: (0, rev(n), hb)), gate),
        out_shape=(SDS((3, S, W), f32), SDS((S, HEAD), f32)),
        scratch_shapes=[pltpu.VMEM((HB, HEAD, HEAD), f32)], args=(qkv, qkv, qkv, gates, states, kept_T, do))


def _gate_norm(o, z, ng):
    return _rms(o, ng) * _silu(z)


def dn_out(o, proj, ng, wout, x1, g3, name):
    S, W = o.shape
    D = x1.shape[1]
    nh = W // HEAD
    tm = _row_tile(S, 256)

    def body(o_ref, z_ref, ng_ref, w_ref, x_ref, g_ref, xo_ref, m_ref, og_ref):
        for h in range(nh):
            sl = slice(h * HEAD, (h + 1) * HEAD)
            og_ref[:, sl] = _gate_norm(o_ref[:, sl], z_ref[:, sl], ng_ref[...]).astype(og_ref.dtype)
        m = _mm(og_ref[...], w_ref[...])
        m_ref[...] = m
        xo_ref[...] = x_ref[...] + _rms(m, g_ref[...])

    rw = pl.BlockSpec((tm, W), lambda i: (i, 0))
    rd = pl.BlockSpec((tm, D), lambda i: (i, 0))
    return pl.pallas_call(
        body, name=name, grid=(S // tm,),
        in_specs=[rw, pl.BlockSpec((tm, W), lambda i: (i, 3)), pl.BlockSpec((1, HEAD), lambda i: (0, 0)),
                  pl.BlockSpec((W, D), lambda i: (0, 0)), rd, pl.BlockSpec((1, D), lambda i: (0, 0))],
        out_specs=(rd, rd, rw),
        out_shape=(SDS((S, D), f32), SDS((S, D), f32), SDS((S, W), MXU_DTYPE)), compiler_params=_params(1),
    )(o, proj, ng, wout, x1, g3)


def dn_out_bwd(dxo, m, g3, o, proj, ng, wout, name):
    S, W = o.shape
    D = m.shape[1]
    nh = W // HEAD
    tm = _row_tile(S, 256)

    def body(dxo_ref, m_ref, g_ref, o_ref, z_ref, ng_ref, w_ref, dm_ref, do_ref, dz_ref, dng_ref, dg_ref):
        @pl.when(pl.program_id(0) == 0)
        def _():
            dng_ref[...] = jnp.zeros_like(dng_ref)
            dg_ref[...] = jnp.zeros_like(dg_ref)

        dm, dg = _rms_bwd(m_ref[...], g_ref[...], dxo_ref[...])
        dg_ref[...] += dg
        dmc = dm.astype(dm_ref.dtype)
        dm_ref[...] = dmc
        dog = _mm_nt(dmc, w_ref[...])
        for h in range(nh):
            sl = slice(h * HEAD, (h + 1) * HEAD)
            _, vjp = jax.vjp(_gate_norm, o_ref[:, sl], z_ref[:, sl], ng_ref[...])
            do, dz, dng = vjp(dog[:, sl])
            do_ref[:, sl] = do
            dz_ref[:, sl] = dz.astype(dz_ref.dtype)
            dng_ref[...] += dng

    rw = pl.BlockSpec((tm, W), lambda i: (i, 0))
    rd = pl.BlockSpec((tm, D), lambda i: (i, 0))
    vd = pl.BlockSpec((1, D), lambda i: (0, 0))
    vh = pl.BlockSpec((1, HEAD), lambda i: (0, 0))
    return pl.pallas_call(
        body, name=name, grid=(S // tm,),
        in_specs=[rd, rd, vd, rw, pl.BlockSpec((tm, W), lambda i: (i, 3)), vh, pl.BlockSpec((W, D), lambda i: (0, 0))],
        out_specs=(rd, rw, rw, vh, vd),
        out_shape=(SDS((S, D), MXU_DTYPE), SDS((S, W), f32), SDS((S, W), MXU_DTYPE), SDS((1, HEAD), f32),
                   SDS((1, D), f32)),
        compiler_params=_params(1),
    )(dxo, m, g3, o, proj, ng, wout)


def _erf_arg(x):
    return lax.erf(x * 0.7071067811865476)


@jax.custom_vjp
def _gelu_with_erf(x, e):
    return 0.5 * x * (1.0 + e)


def _gelu_with_erf_bwd(res, g):
    x, e = res
    return g * (0.5 * (1.0 + e) + x * (jnp.exp(-0.5 * x * x) * 0.3989422804014327)), jnp.zeros_like(e)


_gelu_with_erf.defvjp(lambda x, e: (0.5 * x * (1.0 + e), (x, e)), _gelu_with_erf_bwd)


def _layernorm(t, lg, lb):
    tc = t - jnp.mean(t, axis=-1, keepdims=True)
    return tc * lax.rsqrt(jnp.mean(tc * tc, axis=-1, keepdims=True) + LN_EPS) * lg + lb


def _sg_stage1_kept(eu, ev, pu, pv, bu, bv, lg, lb):
    return _gelu_with_erf(pu + bu, eu), _layernorm(_gelu_with_erf(pv + bv, ev), lg, lb)


def _causal_mask(n):
    return lax.broadcasted_iota(jnp.int32, (n, n), 0) >= lax.broadcasted_iota(jnp.int32, (n, n), 1)


def sg_mid(pre, b_in, ln_g, ln_b, w_s, bsT, wout, x1, g3, name):
    S = pre.shape[0]
    E, D = ln_g.shape[1], x1.shape[1]
    G, CH = SG_GROUPS, SG_CHUNK
    Cg = E // G
    tm = _row_tile(S, 256)

    def body(pu_ref, pv_ref, bu_ref, bv_ref, lg_ref, lb_ref, ws_ref, bs_ref, w_ref, x_ref, g_ref,
             xo_ref, m_ref, gt_ref, e_ref):
        xu, xv = pu_ref[...] + bu_ref[...], pv_ref[...] + bv_ref[...]
        eu, ev = _erf_arg(xu), _erf_arg(xv)
        e_ref[:, :E] = eu.astype(e_ref.dtype)
        e_ref[:, E:] = ev.astype(e_ref.dtype)
        u = 0.5 * xu * (1.0 + eu)
        v = _layernorm(0.5 * xv * (1.0 + ev), lg_ref[...], lb_ref[...])
        mask = _causal_mask(CH)
        for g in range(G):
            wc = _c(jnp.where(mask, ws_ref[g], 0.0))
            bcol = bs_ref[:, g:g + 1]
            cs = slice(g * Cg, (g + 1) * Cg)
            for ch in range(tm // CH):
                rs = slice(ch * CH, (ch + 1) * CH)
                mixed = _mm(wc, _c(v[rs, cs])) + bcol
                gt_ref[rs, cs] = (u[rs, cs] * mixed).astype(gt_ref.dtype)
        m = _mm(gt_ref[...], w_ref[...])
        m_ref[...] = m
        xo_ref[...] = x_ref[...] + _rms(m, g_ref[...])

    half = lambda p: pl.BlockSpec((tm, E), lambda i: (i, p))
    vhalf = lambda p: pl.BlockSpec((1, E), lambda i: (0, p))
    ve = pl.BlockSpec((1, E), lambda i: (0, 0))
    rd = pl.BlockSpec((tm, D), lambda i: (i, 0))
    return pl.pallas_call(
        body, name=name, grid=(S // tm,),
        in_specs=[half(0), half(1), vhalf(0), vhalf(1), ve, ve, pl.BlockSpec((G, CH, CH), lambda i: (0, 0, 0)),
                  pl.BlockSpec((CH, G), lambda i: (0, 0)), pl.BlockSpec((E, D), lambda i: (0, 0)), rd,
                  pl.BlockSpec((1, D), lambda i: (0, 0))],
        out_specs=(rd, rd, pl.BlockSpec((tm, E), lambda i: (i, 0)), pl.BlockSpec((tm, 2 * E), lambda i: (i, 0))),
        out_shape=(SDS((S, D), f32), SDS((S, D), f32), SDS((S, E), MXU_DTYPE), SDS((S, 2 * E), MXU_DTYPE)),
        compiler_params=_params(1),
    )(pre, pre, b_in, b_in, ln_g, ln_b, w_s, bsT, wout, x1, g3)


def sg_mid_bwd(dxo, m, g3, pre, kept_erf, b_in, ln_g, ln_b, w_s, bsT, wout, name):
    S = pre.shape[0]
    E, D = ln_g.shape[1], m.shape[1]
    G, CH = SG_GROUPS, SG_CHUNK
    Cg = E // G
    tm = _row_tile(S, 256)

    def body(dxo_ref, m_ref, g_ref, pu_ref, pv_ref, eu_ref, ev_ref, bu_ref, bv_ref, lg_ref, lb_ref, ws_ref, bs_ref,
             w_ref, dm_ref, dpre_ref, dbin_ref, dlg_ref, dlb_ref, dws_ref, dbs_ref, dg_ref, du_scr, dv_scr):
        @pl.when(pl.program_id(0) == 0)
        def _():
            for r in (dbin_ref, dlg_ref, dlb_ref, dws_ref, dbs_ref, dg_ref):
                r[...] = jnp.zeros_like(r)

        dm, dg = _rms_bwd(m_ref[...], g_ref[...], dxo_ref[...])
        dg_ref[...] += dg
        dmc = dm.astype(dm_ref.dtype)
        dm_ref[...] = dmc
        dgated = _mm_nt(dmc, w_ref[...])
        stage1 = functools.partial(_sg_stage1_kept, eu_ref[...].astype(f32), ev_ref[...].astype(f32))
        (u, v), vjp1 = jax.vjp(stage1, pu_ref[...], pv_ref[...], bu_ref[...], bv_ref[...], lg_ref[...], lb_ref[...])
        mask = _causal_mask(CH)
        lane = lax.broadcasted_iota(jnp.int32, (CH, CH), 1)
        for g in range(G):
            wc = _c(jnp.where(mask, ws_ref[g], 0.0))
            bcol = bs_ref[:, g:g + 1]
            cs = slice(g * Cg, (g + 1) * Cg)
            dws = jnp.zeros((CH, CH), f32)
            dbs = jnp.zeros((CH, 1), f32)
            for ch in range(tm // CH):
                rs = slice(ch * CH, (ch + 1) * CH)
                vs = _c(v[rs, cs])
                mixed = _mm(wc, vs) + bcol
                dgt = dgated[rs, cs]
                du_scr[rs, cs] = dgt * mixed
                dmixed = dgt * u[rs, cs]
                dmc2 = _c(dmixed)
                dv_scr[rs, cs] = _mm_tn(wc, dmc2)
                dws = dws + _mm_nt(dmc2, vs)
                dbs = dbs + jnp.sum(dmixed, axis=1, keepdims=True)
            dws_ref[g] += jnp.where(mask, dws, 0.0)
            dbs_ref[...] += jnp.where(lane == g, jnp.broadcast_to(dbs, (CH, CH)), 0.0)
        dpu, dpv, dbu, dbv, dlg, dlb = vjp1((du_scr[...], dv_scr[...]))
        dpre_ref[:, :E] = dpu.astype(dpre_ref.dtype)
        dpre_ref[:, E:] = dpv.astype(dpre_ref.dtype)
        dbin_ref[:, :E] += dbu
        dbin_ref[:, E:] += dbv
        dlg_ref[...] += dlg
        dlb_ref[...] += dlb

    half = lambda p: pl.BlockSpec((tm, E), lambda i: (i, p))
    vhalf = lambda p: pl.BlockSpec((1, E), lambda i: (0, p))
    ve = pl.BlockSpec((1, E), lambda i: (0, 0))
    rd = pl.BlockSpec((tm, D), lambda i: (i, 0))
    vd = pl.BlockSpec((1, D), lambda i: (0, 0))
    wsb = pl.BlockSpec((G, CH, CH), lambda i: (0, 0, 0))
    return pl.pallas_call(
        body, name=name, grid=(S // tm,),
        in_specs=[rd, rd, vd, half(0), half(1), half(0), half(1), vhalf(0), vhalf(1), ve, ve, wsb,
                  pl.BlockSpec((CH, G), lambda i: (0, 0)), pl.BlockSpec((E, D), lambda i: (0, 0))],
        out_specs=(rd, pl.BlockSpec((tm, 2 * E), lambda i: (i, 0)), pl.BlockSpec((1, 2 * E), lambda i: (0, 0)), ve, ve,
                   wsb, pl.BlockSpec((CH, CH), lambda i: (0, 0)), vd),
        out_shape=(SDS((S, D), MXU_DTYPE), SDS((S, 2 * E), MXU_DTYPE), SDS((1, 2 * E), f32), SDS((1, E), f32),
                   SDS((1, E), f32), SDS((G, CH, CH), f32), SDS((CH, CH), f32), SDS((1, D), f32)),
        scratch_shapes=[pltpu.VMEM((tm, E), f32), pltpu.VMEM((tm, E), f32)], compiler_params=_params(1),
    )(dxo, m, g3, pre, pre, kept_erf, kept_erf, b_in, b_in, ln_g, ln_b, w_s, bsT, wout)


def loss_head(y, target, name):
    S, D = y.shape
    tm = _row_tile(S, 512)

    def body(y_ref, t_ref, l_ref, d_ref):
        @pl.when(pl.program_id(0) == 0)
        def _():
            l_ref[...] = jnp.zeros_like(l_ref)

        e = y_ref[...] - t_ref[...]
        d_ref[...] = e * (1.0 / D)
        l_ref[...] += jnp.sum(e * e) * (0.5 / D)

    row = pl.BlockSpec((tm, D), lambda i: (i, 0))
    return pl.pallas_call(
        body, name=name, grid=(S // tm,), in_specs=[row, row],
        out_specs=(pl.BlockSpec((1, HEAD), lambda i: (0, 0)), row),
        out_shape=(SDS((1, HEAD), f32), SDS((S, D), f32)), compiler_params=_params(1),
    )(y, target)


def sum_slots(r, name):
    _, R, C = r.shape
    tr = R // 2 if R % 16 == 0 else R

    def body(r_ref, o_ref):
        acc = r_ref[0].astype(f32)
        for s in range(1, N_DEV):
            acc = acc + r_ref[s].astype(f32)
        o_ref[...] = acc

    return pl.pallas_call(
        body, name=name, grid=(R // tr,), in_specs=[pl.BlockSpec((N_DEV, tr, C), lambda i: (0, i, 0))],
        out_specs=pl.BlockSpec((tr, C), lambda i: (i, 0)), out_shape=SDS((R, C), f32), compiler_params=_params(1),
    )(r)


def _adam_math(w, g, m, v):
    m = ADAM_B1 * m + (1.0 - ADAM_B1) * g
    v = ADAM_B2 * v + (1.0 - ADAM_B2) * (g * g)
    m_hat = m / (1.0 - ADAM_B1 ** ADAM_STEP)
    v_hat = v / (1.0 - ADAM_B2 ** ADAM_STEP)
    delta = -ADAM_LR * (m_hat / (jnp.sqrt(v_hat) + ADAM_EPS) + ADAM_WD * w)
    return delta, m, v


def adam_slots(w, rs, m, v, name, tr):
    R, C = w.shape
    tr = _row_tile(min(r.shape[1] for r in rs), tr)
    blocks = [r.shape[1] // tr for r in rs]
    starts = [sum(blocks[:k]) for k in range(len(rs))]
    assert sum(blocks) * tr == R

    def body(w_ref, *refs):
        r_refs, (m_ref, v_ref, g_ref, d_ref, mo_ref, vo_ref) = refs[:len(rs)], refs[len(rs):]
        i = pl.program_id(0)
        for k, r_ref in enumerate(r_refs):
            @pl.when((i >= starts[k]) & (i < starts[k] + blocks[k]))
            def _():
                g = r_ref[0].astype(f32)
                for s in range(1, N_DEV):
                    g = g + r_ref[s].astype(f32)
                g_ref[...] = g

        d_ref[...], mo_ref[...], vo_ref[...] = _adam_math(w_ref[...], g_ref[...], m_ref[...], v_ref[...])

    row = pl.BlockSpec((tr, C), lambda i: (i, 0))
    piece = lambda k: pl.BlockSpec((N_DEV, tr, C), lambda i: (0, jnp.clip(i - starts[k], 0, blocks[k] - 1), 0))
    return pl.pallas_call(
        body, name=name, grid=(R // tr,), in_specs=[row] + [piece(k) for k in range(len(rs))] + [row, row],
        out_specs=(row, row, row, row), out_shape=tuple(SDS((R, C), f32) for _ in range(4)),
        compiler_params=_params(1),
    )(w, *rs, m, v)


def adam_small(w, g, m, v, name):
    def body(w_ref, g_ref, m_ref, v_ref, d_ref, mo_ref, vo_ref):
        d_ref[...], mo_ref[...], vo_ref[...] = _adam_math(w_ref[...], g_ref[...], m_ref[...], v_ref[...])

    return pl.pallas_call(body, name=name, out_shape=tuple(SDS(w.shape, f32) for _ in range(3)))(w, g, m, v)


def _pack_rows(parts):
    rows, offs, r = [], [], 0
    for p in parts:
        flat = p.reshape(-1)
        n = -(-flat.shape[0] // HEAD)
        flat = jnp.pad(flat, (0, n * HEAD - flat.shape[0]))
        rows.append(flat.reshape(n, HEAD))
        offs.append((r, n))
        r += n
    pad = (-r) % 8
    if pad:
        rows.append(jnp.zeros((pad, HEAD), f32))
    return jnp.concatenate(rows, axis=0), offs


def kernel(x, norm_g, ffn_w_gate, ffn_w_up, ffn_w_down, dn_w_in, dn_conv_w, dn_a_log, dn_dt_bias, dn_norm_g, dn_w_out, sg_w_in, sg_b_in, sg_ln_g, sg_ln_b, sg_w_s, sg_b_s, sg_w_out, loss_target, m_norm_g, m_ffn_w_gate, m_ffn_w_up, m_ffn_w_down, m_dn_w_in, m_dn_conv_w, m_dn_a_log, m_dn_dt_bias, m_dn_norm_g, m_dn_w_out, m_sg_w_in, m_sg_b_in, m_sg_ln_g, m_sg_ln_b, m_sg_w_s, m_sg_b_s, m_sg_w_out, v_norm_g, v_ffn_w_gate, v_ffn_w_up, v_ffn_w_down, v_dn_w_in, v_dn_conv_w, v_dn_a_log, v_dn_dt_bias, v_dn_norm_g, v_dn_w_out, v_sg_w_in, v_sg_b_in, v_sg_ln_g, v_sg_ln_b, v_sg_w_s, v_sg_b_s, v_sg_w_out):
    weights = dict(norm_g=norm_g, ffn_w_gate=ffn_w_gate, ffn_w_up=ffn_w_up, ffn_w_down=ffn_w_down, dn_w_in=dn_w_in,
                   dn_conv_w=dn_conv_w, dn_a_log=dn_a_log, dn_dt_bias=dn_dt_bias, dn_norm_g=dn_norm_g,
                   dn_w_out=dn_w_out, sg_w_in=sg_w_in, sg_b_in=sg_b_in, sg_ln_g=sg_ln_g, sg_ln_b=sg_ln_b,
                   sg_w_s=sg_w_s, sg_b_s=sg_b_s, sg_w_out=sg_w_out)
    mom_m = dict(norm_g=m_norm_g, ffn_w_gate=m_ffn_w_gate, ffn_w_up=m_ffn_w_up, ffn_w_down=m_ffn_w_down,
                 dn_w_in=m_dn_w_in, dn_conv_w=m_dn_conv_w, dn_a_log=m_dn_a_log, dn_dt_bias=m_dn_dt_bias,
                 dn_norm_g=m_dn_norm_g, dn_w_out=m_dn_w_out, sg_w_in=m_sg_w_in, sg_b_in=m_sg_b_in,
                 sg_ln_g=m_sg_ln_g, sg_ln_b=m_sg_ln_b, sg_w_s=m_sg_w_s, sg_b_s=m_sg_b_s, sg_w_out=m_sg_w_out)
    mom_v = dict(norm_g=v_norm_g, ffn_w_gate=v_ffn_w_gate, ffn_w_up=v_ffn_w_up, ffn_w_down=v_ffn_w_down,
                 dn_w_in=v_dn_w_in, dn_conv_w=v_dn_conv_w, dn_a_log=v_dn_a_log, dn_dt_bias=v_dn_dt_bias,
                 dn_norm_g=v_dn_norm_g, dn_w_out=v_dn_w_out, sg_w_in=v_sg_w_in, sg_b_in=v_sg_b_in,
                 sg_ln_g=v_sg_ln_g, sg_ln_b=v_sg_ln_b, sg_w_s=v_sg_w_s, sg_b_s=v_sg_b_s, sg_w_out=v_sg_w_out)
    order = list(weights)

    xs = x[0]
    S, D = xs.shape
    F8 = ffn_w_gate.shape[-1]
    depth = norm_g.shape[0]
    W = dn_w_out.shape[1] * N_DEV
    H = W // HEAD
    E = sg_ln_g.shape[1] * N_DEV
    G, CH = sg_w_s.shape[1], sg_w_s.shape[2]
    c8 = dn_w_in.shape[2]
    me = _slot(lax.axis_index("x"), lax.axis_index("y"), lax.axis_index("c"))

    assert depth == 2
    small_in, small_offs = _pack_rows([norm_g, dn_conv_w, sg_b_in, sg_ln_g, sg_ln_b])
    wg0a, wu0a, wd0a, small_all = all_gather_multi(
        [_c(ffn_w_gate[0, 0]), _c(ffn_w_up[0, 0]), _c(ffn_w_down[0, 0]), small_in], name="gather_first")
    ffn_shards = lambda l, ab: [_c(ffn_w_gate[l, ab]), _c(ffn_w_up[l, ab]), _c(ffn_w_down[l, ab])]
    gather_dn = Comm("gather", [_c(dn_w_in[0]), _c(dn_w_out[0])])
    gather_mid = Comm("gather", ffn_shards(0, 1) + ffn_shards(1, 0))
    gather_end = Comm("gather", ffn_shards(1, 1))
    gather_sg = Comm("gather", [_c(sg_w_in[0]), _c(sg_w_out[0])])
    per = N_DEV // FFN_SLABS
    wide = lambda tag, g, u, d: (*widen_slabs([g, u], FFN_SLABS, name=f"widen_{tag}"),
                                 d.reshape(FFN_SLABS, per * F8, D))
    ffn_w = {(0, 0): wide("0a", wg0a, wu0a, wd0a)}

    def small_piece(i, shard_shape):
        r0, n = small_offs[i]
        sz = math.prod(shard_shape)
        return small_all[:, r0:r0 + n, :].reshape(N_DEV, n * HEAD)[:, :sz].reshape((N_DEV,) + tuple(shard_shape))

    ng_full = jnp.moveaxis(small_piece(0, norm_g.shape), 0, 2).reshape(depth, 6, D)
    conv_full = jnp.moveaxis(small_piece(1, dn_conv_w.shape[1:]), 0, 1).reshape(CONV_K, 3 * W)
    bin_full = small_piece(2, sg_b_in.shape[1:]).reshape(1, 2 * E)
    lng_full = small_piece(3, sg_ln_g.shape[1:]).reshape(1, E)
    lnb_full = small_piece(4, sg_ln_b.shape[1:]).reshape(1, E)
    gate_lanes = lambda v: jnp.pad(v.reshape(1, H), ((0, 0), (H, HEAD - 2 * H)))
    al_row, dt_row = gate_lanes(dn_a_log), gate_lanes(dn_dt_bias)
    bsT = sg_b_s[0].T
    gvec = lambda l, k: ng_full[l, k].reshape(1, D)

    saved = []
    cur = xs
    for l in range(depth):
        sv = {}
        sv['x0'] = cur
        (cur, sv['hA'], sv['pA'], sv['qA'], sv['tA'], sv['yA']), got = ffn_fwd(
            cur, gvec(l, 0), gvec(l, 1), *ffn_w[l, 0], name=f"ffn_fwd_{l}a", comm=gather_dn if l == 0 else gather_sg)
        sv['x1'] = cur
        if l == 1:
            sg_win = jnp.moveaxis(got[0], 0, 1).reshape(D, 2 * E)
            sg_wout = got[1].reshape(E, D)
        if l == 0:
            dnin_all, dnout_all = got
            dn_wmain, dn_wba = join_columns(dnin_all, 4 * W, name="dn_w_in_join")
            dn_wout = dnout_all.reshape(W, D)
            sv['hM'], sv['proj'], sv['pba'] = rms_mm(cur, gvec(l, 2), dn_wmain, dn_wba, name=f"dn_in_{l}")
            sv['qkv'] = dn_prep(sv['proj'], conv_full, name=f"dn_prep_{l}")
            sv['gates'] = dn_gates(sv['pba'], al_row, dt_row, H, name=f"dn_gates_{l}")
            (sv['o'], sv['states'], sv['T']), got = dn_chunk_fwd(sv['qkv'], sv['gates'], name=f"dn_chunk_{l}",
                                                                 comm=gather_mid)
            ffn_w[0, 1], ffn_w[1, 0] = wide("0b", *got[0:3]), wide("1a", *got[3:6])
            cur, sv['m'], sv['og'] = dn_out(sv['o'], sv['proj'], dn_norm_g, dn_wout, cur, gvec(l, 3), name=f"dn_out_{l}")
        else:
            sv['hM'], sv['pre'] = rms_mm(cur, gvec(l, 2), sg_win, None, name=f"sg_in_{l}")
            cur, sv['m'], sv['gated'], sv['erf'] = sg_mid(sv['pre'], bin_full, lng_full, lnb_full, sg_w_s[0], bsT, sg_wout,
                                                          cur, gvec(l, 3), name=f"sg_mid_{l}")
        sv['x2'] = cur
        (cur, sv['hB'], sv['pB'], sv['qB'], sv['tB'], sv['yB']), got = ffn_fwd(
            cur, gvec(l, 4), gvec(l, 5), *ffn_w[l, 1], name=f"ffn_fwd_{l}b", comm=gather_end if l == 0 else None)
        if l == 0:
            ffn_w[1, 1] = wide("1b", *got[0:3])
        saved.append(sv)

    loss_blk, dcur = loss_head(cur, loss_target[0], name="loss_head")
    loss = lax.psum(loss_blk[0, 0], ("x", "y", "c"))

    dng = [[None] * 6 for _ in range(depth)]
    ffn_dw = {}
    grads, slots = {}, {}

    def ffn_backward(l, ab, dcur, exchange=None):
        sv, s = saved[l], 'AB'[ab]
        (dcur, da, db, dy, dng[l][4 * ab], dng[l][4 * ab + 1]), got = ffn_bwd_dx(
            dcur, sv['x2' if ab else 'x0'], sv['y' + s], sv['p' + s], sv['q' + s], gvec(l, 4 * ab), gvec(l, 4 * ab + 1),
            *ffn_w[l, ab], name=f"ffn_bwd_{l}{'ab'[ab]}", comm=Comm("exchange", exchange) if exchange else None)
        ffn_dw[l, ab], _ = ffn_bwd_dw(sv['h' + s], dy, sv['t' + s], da, db, name=f"ffn_dw_{l}{'ab'[ab]}")
        return dcur, got

    sv = saved[1]
    dcur, _ = ffn_backward(1, 1, dcur)
    dm, dpre, grads['sg_b_in'], grads['sg_ln_g'], grads['sg_ln_b'], grads['sg_w_s'], dbs, dng[1][3] = sg_mid_bwd(
        dcur, sv['m'], gvec(1, 3), sv['pre'], sv['erf'], bin_full, lng_full, lnb_full, sg_w_s[0], bsT, sg_wout,
        name="sg_mid_bwd_1")
    grads['sg_b_s'] = dbs[:, :G].T
    dsg_wout = tn_mm(sv['gated'], dm, name="sg_wout_dw_1").reshape(N_DEV, E // N_DEV, D)
    dsg_win = tn_mm(sv['hM'], dpre, name="sg_win_dw_1", tn=2 * E // N_DEV, slot_major=True)
    (dcur, dng[1][2]), _ = mm_bwd_dx(dcur, sv['x1'], gvec(1, 2), dpre, sg_win, None, None, name="sg_in_bwd_1")
    dcur, l1b = ffn_backward(1, 0, dcur, exchange=list(ffn_dw[1, 1]))
    sv = saved[0]
    dcur, got = ffn_backward(0, 1, dcur, exchange=[dsg_win, dsg_wout])
    slots['sg_w_in'], slots['sg_w_out'] = [got[0]], [got[1]]
    dm, do, dz, grads['dn_norm_g'], dng[0][3] = dn_out_bwd(dcur, sv['m'], gvec(0, 3), sv['o'], sv['proj'], dn_norm_g,
                                                          dn_wout, name="dn_out_bwd_0")
    ddn_wout = tn_mm(sv['og'], dm, name="dn_wout_dw_0").reshape(N_DEV, W // N_DEV, D)
    (dqkv, dgates), got = dn_chunk_bwd(sv['qkv'], sv['gates'], sv['states'], sv['T'], do, name="dn_chunk_bwd_0",
                                       comm=Comm("exchange", list(ffn_dw[1, 0])))
    l1a = got
    dpba, dal, ddt = dn_gates_bwd(sv['pba'], al_row, dt_row, dgates, H, name="dn_gates_bwd_0")
    grads['dn_a_log'] = dal[:, H:2 * H]
    grads['dn_dt_bias'] = ddt[:, H:2 * H]
    (dproj, grads['dn_conv_w']), got = dn_prep_bwd(sv['proj'], conv_full, dqkv, dz, name="dn_prep_bwd_0",
                                                   comm=Comm("exchange", [*ffn_dw[0, 1], ddn_wout]))
    l0b, slots['dn_w_out'] = got[0:3], [got[3]]
    dw_main = tn_mm(sv['hM'], dproj, name="dn_win_dw_0")
    dw_ba = tn_mm(sv['hM'], dpba, name="dn_wba_dw_0", tn=HEAD)
    ddn_win = split_columns(dw_main, dw_ba, c8, name="dn_w_in_split")
    (dcur, dng[0][2]), got = mm_bwd_dx(dcur, sv['x1'], gvec(0, 2), dproj, dn_wmain, dpba, dn_wba, name="dn_in_bwd_0",
                                       comm=Comm("exchange", [ddn_win]))
    slots['dn_w_in'] = [got[0]]
    small_names = ['norm_g', 'dn_conv_w', 'sg_b_in', 'sg_ln_g', 'sg_ln_b', 'sg_w_s', 'sg_b_s', 'dn_a_log',
                   'dn_dt_bias', 'dn_norm_g']
    small = {}

    def gather_small():
        dng_full = jnp.stack([jnp.concatenate(r, axis=0) for r in dng], axis=0)
        small['parts'] = [dng_full, grads['dn_conv_w'], grads['sg_b_in'], grads['sg_ln_g'], grads['sg_ln_b'],
                          grads['sg_w_s'], grads['sg_b_s'], grads['dn_a_log'], grads['dn_dt_bias'], grads['dn_norm_g']]
        pack, small['offs'] = _pack_rows(small['parts'])
        return Comm("gather", [pack])

    (dcur, da, db, dy, dng[0][0], dng[0][1]), _ = ffn_bwd_dx(
        dcur, sv['x0'], sv['yA'], sv['pA'], sv['qA'], gvec(0, 0), gvec(0, 1), *ffn_w[0, 0], name="ffn_bwd_0a")
    grad_x = dcur[None]
    (dg,), (small_slots,) = ffn_bwd_dw_one(sv['hA'], da, False, name="ffn_dw_0a_gate", comm=gather_small())
    (du,), (xg,) = ffn_bwd_dw_one(sv['hA'], db, False, name="ffn_dw_0a_up", comm=Comm("exchange", [dg]))
    (dd,), (xu,) = ffn_bwd_dw_one(dy, sv['tA'], True, name="ffn_dw_0a_down", comm=Comm("exchange", [du]))
    small_parts, offs = small['parts'], small['offs']
    l0a = [xg, xu, exchange_slots([dd], name="exchange_last")[0]]
    for i, nm in enumerate(['ffn_w_gate', 'ffn_w_up', 'ffn_w_down']):
        slots[nm] = [l0a[i], l0b[i], l1a[i], l1b[i]]
    big_names = ['ffn_w_gate', 'ffn_w_up', 'ffn_w_down', 'dn_w_in', 'dn_w_out', 'sg_w_in', 'sg_w_out']
    slots = [slots[nm] for nm in big_names]
    small_sum = sum_slots(small_slots, name="sum_small_grads")

    def small_grad(i):
        r0, n = offs[i]
        p = small_parts[i]
        return small_sum[r0:r0 + n].reshape(-1)[:p.size].reshape(p.shape)

    def my_shard(full, axis, like):
        n = full.shape[axis] // N_DEV
        return lax.dynamic_slice_in_dim(full, me * n, n, axis).reshape(like.shape)

    g_small = {
        'norm_g': my_shard(small_grad(0), 2, norm_g),
        'dn_conv_w': my_shard(small_grad(1), 1, dn_conv_w),
        'sg_b_in': my_shard(small_grad(2), 1, sg_b_in),
        'sg_ln_g': my_shard(small_grad(3), 1, sg_ln_g),
        'sg_ln_b': my_shard(small_grad(4), 1, sg_ln_b),
        'sg_w_s': small_grad(5).reshape(sg_w_s.shape),
        'sg_b_s': small_grad(6).reshape(sg_b_s.shape),
        'dn_a_log': small_grad(7).reshape(dn_a_log.shape),
        'dn_dt_bias': small_grad(8).reshape(dn_dt_bias.shape),
        'dn_norm_g': small_grad(9).reshape(dn_norm_g.shape),
    }

    out_g, out_d, out_m, out_v = {}, {}, {}, {}
    for nm, r in zip(big_names, slots):
        w = weights[nm]
        cols = w.shape[-1]
        rows = w.size // cols
        tr = {'ffn_w_gate': 512, 'ffn_w_up': 512, 'ffn_w_down': F8 // 2, 'dn_w_in': 256, 'sg_w_in': 256}.get(nm, rows)
        pieces = [p.reshape(N_DEV, -1, cols) for p in r]
        g, d, m2, v2 = adam_slots(w.reshape(rows, cols), pieces, mom_m[nm].reshape(rows, cols),
                                  mom_v[nm].reshape(rows, cols), name=f"adam_{nm}", tr=tr)
        out_g[nm], out_d[nm], out_m[nm], out_v[nm] = (t.reshape(w.shape) for t in (g, d, m2, v2))
    for nm in small_names:
        w = weights[nm]
        cols = w.shape[-1]
        rows = w.size // cols
        two = lambda t: t.reshape(rows, cols)
        d, m2, v2 = adam_small(two(w), two(g_small[nm]), two(mom_m[nm]), two(mom_v[nm]), name=f"adam_{nm}")
        out_g[nm] = g_small[nm]
        out_d[nm], out_m[nm], out_v[nm] = (t.reshape(w.shape) for t in (d, m2, v2))

    return (loss, grad_x, *[out_g[n] for n in order], *[out_d[n] for n in order], *[out_m[n] for n in order],
            *[out_v[n] for n in order])
```

```python
import functools
import math

import jax
import jax.numpy as jnp
from jax import lax
from jax.experimental import pallas as pl
from jax.experimental.pallas import tpu as pltpu

f32 = jnp.float32
MXU_DTYPE = jnp.bfloat16
N_DEV = 8
RMS_EPS = 1e-6
LN_EPS = 1e-5
L2_EPS = 1e-6
HEAD = 128
DN_CHUNK = 64
SG_CHUNK = 128
SG_GROUPS = 8
CONV_K = 4
ADAM_LR, ADAM_B1, ADAM_B2, ADAM_EPS, ADAM_WD, ADAM_STEP = 0.001, 0.9, 0.999, 1e-08, 0.01, 10
VMEM_LIMIT = 56 * 1024 * 1024
FFN_ROWS_FWD, FFN_ROWS_BWD, FFN_ROWS_DW = 1024, 512, 2048
PROJ_ROWS, TN_ROWS = 1024, 2048
FFN_SLABS = 4
SDS = jax.ShapeDtypeStruct
MESH = pl.DeviceIdType.MESH


def _params(n_grid):
    return pltpu.CompilerParams(dimension_semantics=("arbitrary",) * n_grid, vmem_limit_bytes=VMEM_LIMIT)


def _row_tile(s, want):
    t = min(s, want)
    assert s % t == 0, (s, t)
    return t


def _rms(x, g):
    return x * lax.rsqrt(jnp.mean(x * x, axis=-1, keepdims=True) + RMS_EPS) * g


def _rms_bwd(x, g, dy):
    _, vjp = jax.vjp(_rms, x, g)
    return vjp(dy)


def _silu(a):
    return a * jax.nn.sigmoid(a)


def _mm(a, b):
    return lax.dot_general(a, b, (((1,), (0,)), ((), ())), preferred_element_type=f32)


def _mm_nt(a, b):
    return lax.dot_general(a, b, (((1,), (1,)), ((), ())), preferred_element_type=f32)


def _mm_tn(a, b):
    return lax.dot_general(a, b, (((0,), (0,)), ((), ())), preferred_element_type=f32)


def _c(x):
    return x.astype(MXU_DTYPE)


def _split(a):
    hi = a.astype(MXU_DTYPE)
    lo = (a - hi.astype(f32)).astype(MXU_DTYPE)
    return hi, lo


def _dot3(a, b, dims):
    ah, al = _split(a)
    bh, bl = _split(b)
    d = lambda p, q: lax.dot_general(p, q, (dims, ((), ())), preferred_element_type=f32)
    return d(ah, bh) + (d(ah, bl) + d(al, bh))


NN, NT, TN = ((1,), (0,)), ((1,), (1,)), ((0,), (0,))


def _slot(px, py, pc):
    return 4 * px + 2 * py + pc


def all_gather_multi(arrs, name):
    return Comm("gather", arrs).alone(name)


def exchange_slots(arrs, name):
    return Comm("exchange", arrs).alone(name)


class Comm:
    def __init__(self, kind, arrs):
        self.kind, self.arrs, self.n = kind, list(arrs), len(arrs)
        hbm = pl.BlockSpec(memory_space=pltpu.HBM)
        self.in_specs = [hbm] * self.n
        self.out_specs = [hbm] * self.n
        lead = (N_DEV,) if kind == "gather" else ()
        self.out_shape = [SDS(lead + tuple(a.shape), a.dtype) for a in self.arrs]
        self.scratch = [pltpu.SemaphoreType.DMA((self.n, 7)), pltpu.SemaphoreType.DMA((self.n, 7)),
                        pltpu.SemaphoreType.DMA((self.n,))]

    def phase(self, p, ins, outs, sems):
        (self._gather if self.kind == "gather" else self._exchange)(p, ins, outs, sems)

    def _gather(self, p, ins, outs, sems):
        send_sems, recv_sems, local_sems = sems
        x, y, c = lax.axis_index("x"), lax.axis_index("y"), lax.axis_index("c")
        me, sibling = (x, y, c), (x, y, 1 - c)
        chips = [(1 - x, y), (x, 1 - y), (1 - x, 1 - y)]

        def copy(a, k, block, to, src=None):
            dst = outs[a].at[_slot(*block)]
            return pltpu.make_async_remote_copy(
                src_ref=dst if src is None else src, dst_ref=dst, send_sem=send_sems.at[a, k],
                recv_sem=recv_sems.at[a, k], device_id=to, device_id_type=MESH)

        mine = [pltpu.make_async_copy(ins[a], outs[a].at[_slot(*me)], local_sems.at[a]) for a in range(self.n)]
        first = [[copy(a, 0, me, sibling, src=ins[a])] +
                 [copy(a, 1 + j, me, (*chip, c), src=ins[a]) for j, chip in enumerate(chips)] for a in range(self.n)]
        passed = [[copy(a, 4 + j, (*chip, c), sibling) for j, chip in enumerate(chips)] for a in range(self.n)]
        if p == 0:
            for a in range(self.n):
                mine[a].start()
            for a in range(self.n):
                for cp in first[a]:
                    cp.start()
        elif p == 1:
            for a in range(self.n):
                for j, chip in enumerate(chips):
                    copy(a, 1 + j, (*chip, c), me).wait_recv()
                    passed[a][j].start()
        else:
            for a in range(self.n):
                copy(a, 0, sibling, me).wait_recv()
                for j, chip in enumerate(chips):
                    copy(a, 4 + j, (*chip, 1 - c), me).wait_recv()
            for a in range(self.n):
                for cp in first[a] + passed[a]:
                    cp.wait_send()
                mine[a].wait()

    def _exchange(self, p, ins, outs, sems):
        send_sems, recv_sems, local_sems = sems
        x, y, c = lax.axis_index("x"), lax.axis_index("y"), lax.axis_index("c")
        me = _slot(x, y, c)
        peers = [(x ^ (k >> 2), y ^ ((k >> 1) & 1), c ^ (k & 1)) for k in range(1, N_DEV)]

        def copy(a, k):
            peer = peers[k - 1]
            return pltpu.make_async_remote_copy(
                src_ref=ins[a].at[_slot(*peer)], dst_ref=outs[a].at[me], send_sem=send_sems.at[a, k - 1],
                recv_sem=recv_sems.at[a, k - 1], device_id=peer, device_id_type=MESH)

        def landed(a, k):
            peer = peers[k - 1]
            return pltpu.make_async_remote_copy(
                src_ref=ins[a].at[me], dst_ref=outs[a].at[_slot(*peer)], send_sem=send_sems.at[a, k - 1],
                recv_sem=recv_sems.at[a, k - 1], device_id=peer, device_id_type=MESH)

        local = [pltpu.make_async_copy(ins[a].at[me], outs[a].at[me], local_sems.at[a]) for a in range(self.n)]
        order = [6, 7, 2, 3, 4, 5, 1]
        if p == 0:
            for a in range(self.n):
                local[a].start()
            for a in range(self.n):
                for k in order:
                    copy(a, k).start()
        elif p == 2:
            for a in range(self.n):
                for k in order:
                    copy(a, k).wait_send()
                    landed(a, k).wait_recv()
                local[a].wait()

    def alone(self, name):
        n = self.n

        def body(*refs):
            for p in range(3):
                self.phase(p, refs[:n], refs[n:2 * n], refs[2 * n:])

        return pl.pallas_call(body, name=name, out_shape=tuple(self.out_shape), in_specs=self.in_specs,
                              out_specs=tuple(self.out_specs), scratch_shapes=self.scratch)(*self.arrs)


def hosted_call(body, comm, steps, *, name, grid, in_specs, out_specs, out_shape, scratch_shapes, args):
    if comm is None:
        outs = pl.pallas_call(body, name=name, grid=grid, in_specs=in_specs, out_specs=tuple(out_specs),
                              out_shape=tuple(out_shape), scratch_shapes=scratch_shapes,
                              compiler_params=_params(len(grid)))(*args)
        return outs, None
    ni, no, ns, cn = len(in_specs), len(out_specs), len(scratch_shapes), comm.n

    def both(*refs):
        h_in, c_in = refs[:ni], refs[ni:ni + cn]
        h_out, c_out = refs[ni + cn:ni + cn + no], refs[ni + cn + no:ni + 2 * cn + no]
        h_scr, c_scr = refs[ni + 2 * cn + no:ni + 2 * cn + no + ns], refs[ni + 2 * cn + no + ns:]
        when = steps()
        pl.when(when[0])(lambda: comm.phase(0, c_in, c_out, c_scr))
        body(*h_in, *h_out, *h_scr)
        pl.when(when[1])(lambda: comm.phase(1, c_in, c_out, c_scr))
        pl.when(when[2])(lambda: comm.phase(2, c_in, c_out, c_scr))

    outs = pl.pallas_call(
        both, name=name, grid=grid, in_specs=list(in_specs) + comm.in_specs,
        out_specs=tuple(out_specs) + tuple(comm.out_specs), out_shape=tuple(out_shape) + tuple(comm.out_shape),
        scratch_shapes=list(scratch_shapes) + comm.scratch, compiler_params=_params(len(grid)),
    )(*args, *comm.arrs)
    return outs[:no], outs[no:]


def _grid_steps(n_outer, n_inner=1):
    total = n_outer * n_inner

    def steps():
        t = pl.program_id(0) * n_inner + (pl.program_id(1) if n_inner > 1 else 0)
        return t == 0, t == (total * 5) // 8, t == total - 1
    return steps


def widen_slabs(arrs, ns, name):
    per = N_DEV // ns
    _, R, C = arrs[0].shape
    n = len(arrs)

    def body(*refs):
        for a in range(n):
            for k in range(per):
                refs[n + a][:, k * C:(k + 1) * C] = refs[a][k]

    return pl.pallas_call(
        body, name=name, grid=(ns,), in_specs=[pl.BlockSpec((per, R, C), lambda s: (s, 0, 0))] * n,
        out_specs=tuple(pl.BlockSpec((None, R, per * C), lambda s: (s, 0, 0)) for _ in range(n)),
        out_shape=tuple(SDS((ns, R, per * C), a.dtype) for a in arrs), compiler_params=_params(1))(*arrs)


def join_columns(blocks, n_main, name):
    nb, R, c8 = blocks.shape
    rest = nb * c8 - n_main
    tr = _row_tile(R, 256)

    def body(b_ref, main_ref, rest_ref, full):
        for k in range(nb):
            full[:, k * c8:(k + 1) * c8] = b_ref[k]
        main_ref[...] = full[:, :n_main]
        rest_ref[...] = jnp.zeros_like(rest_ref)
        rest_ref[:, :rest] = full[:, n_main:]

    return pl.pallas_call(
        body, name=name, grid=(R // tr,), in_specs=[pl.BlockSpec((nb, tr, c8), lambda i: (0, i, 0))],
        out_specs=(pl.BlockSpec((tr, n_main), lambda i: (i, 0)), pl.BlockSpec((tr, HEAD), lambda i: (i, 0))),
        out_shape=(SDS((R, n_main), blocks.dtype), SDS((R, HEAD), blocks.dtype)),
        scratch_shapes=[pltpu.VMEM((tr, nb * c8), blocks.dtype)], compiler_params=_params(1))(blocks)


def split_columns(main, rest, c8, name):
    R, n_main = main.shape
    nb = N_DEV
    n_rest = nb * c8 - n_main
    tr = _row_tile(R, 256)

    def body(main_ref, rest_ref, b_ref, full):
        full[:, :n_main] = main_ref[...]
        full[:, n_main:] = rest_ref[:, :n_rest]
        for k in range(nb):
            b_ref[k] = full[:, k * c8:(k + 1) * c8]

    return pl.pallas_call(
        body, name=name, grid=(R // tr,),
        in_specs=[pl.BlockSpec((tr, n_main), lambda i: (i, 0)), pl.BlockSpec((tr, HEAD), lambda i: (i, 0))],
        out_specs=pl.BlockSpec((nb, tr, c8), lambda i: (0, i, 0)), out_shape=SDS((nb, R, c8), main.dtype),
        scratch_shapes=[pltpu.VMEM((tr, nb * c8), main.dtype)], compiler_params=_params(1))(main, rest)


def ffn_fwd(x, gpre, gpost, wg, wu, wd, name, comm=None):
    S, D = x.shape
    nj, F8 = wg.shape[0], wg.shape[-1]
    tm = _row_tile(S, FFN_ROWS_FWD)

    def body(x_ref, gpre_ref, gpost_ref, wg_ref, wu_ref, wd_ref, xo_ref, h_ref, p_ref, q_ref, t_ref, y_ref):
        j = pl.program_id(1)

        @pl.when(j == 0)
        def _():
            h_ref[...] = _rms(x_ref[...], gpre_ref[...]).astype(h_ref.dtype)
            y_ref[...] = jnp.zeros_like(y_ref)

        h = h_ref[...]
        a = _mm(h, wg_ref[...])
        b = _mm(h, wu_ref[...])
        s = jax.nn.sigmoid(a)
        q = a * s
        p_ref[...] = (b * (s + q * (1.0 - s))).astype(p_ref.dtype)
        q_ref[...] = q.astype(q_ref.dtype)
        t = (q * b).astype(t_ref.dtype)
        t_ref[...] = t
        y_ref[...] += _mm(t, wd_ref[...])

        @pl.when(j == nj - 1)
        def _():
            xo_ref[...] = x_ref[...] + 0.5 * _rms(y_ref[...], gpost_ref[...])

    row = pl.BlockSpec((tm, D), lambda i, j: (i, 0))
    vec = pl.BlockSpec((1, D), lambda i, j: (0, 0))
    wcol = pl.BlockSpec((None, D, F8), lambda i, j: (j, 0, 0))
    wrow = pl.BlockSpec((None, F8, D), lambda i, j: (j, 0, 0))
    hid = pl.BlockSpec((None, tm, F8), lambda i, j: (j, i, 0))
    return hosted_call(
        body, comm, _grid_steps(S // tm, nj), name=name, grid=(S // tm, nj),
        in_specs=[row, vec, vec, wcol, wcol, wrow],
        out_specs=(row, row, hid, hid, hid, row),
        out_shape=(SDS((S, D), f32), SDS((S, D), MXU_DTYPE), SDS((nj, S, F8), MXU_DTYPE),
                   SDS((nj, S, F8), MXU_DTYPE), SDS((nj, S, F8), MXU_DTYPE), SDS((S, D), f32)),
        scratch_shapes=[], args=(x, gpre, gpost, wg, wu, wd))


def ffn_bwd_dx(dxo, x, y, p, q, gpre, gpost, wg, wu, wd, name, comm=None):
    S, D = x.shape
    NS, F8 = wg.shape[0], wg.shape[-1]
    sps = 2 if NS % 2 == 0 else 1
    nj = NS // sps
    tm = _row_tile(S, FFN_ROWS_BWD)

    def body(dxo_ref, x_ref, y_ref, p_ref, q_ref, gpre_ref, gpost_ref, wg_ref, wu_ref, wd_ref,
             dx_ref, da_ref, db_ref, dy_ref, dgpre_ref, dgpost_ref, dh_ref):
        i, j = pl.program_id(0), pl.program_id(1)

        @pl.when(j == 0)
        def _():
            @pl.when(i == 0)
            def _():
                dgpre_ref[...] = jnp.zeros_like(dgpre_ref)
                dgpost_ref[...] = jnp.zeros_like(dgpost_ref)

            dy, dg = _rms_bwd(y_ref[...], gpost_ref[...], 0.5 * dxo_ref[...])
            dy_ref[...] = dy.astype(dy_ref.dtype)
            dgpost_ref[...] += dg
            dh_ref[...] = jnp.zeros_like(dh_ref)

        dy = dy_ref[...]
        das, dbs = [], []
        for s in range(sps):
            dt = _mm_nt(dy, wd_ref[s])
            das.append((dt * p_ref[s].astype(f32)).astype(da_ref.dtype))
            dbs.append((dt * q_ref[s].astype(f32)).astype(db_ref.dtype))
            da_ref[s] = das[s]
            db_ref[s] = dbs[s]
        upd = None
        for s in range(sps):
            part = _mm_nt(das[s], wg_ref[s]) + _mm_nt(dbs[s], wu_ref[s])
            upd = part if upd is None else upd + part
        dh_ref[...] += upd

        @pl.when(j == nj - 1)
        def _():
            dxx, dg = _rms_bwd(x_ref[...], gpre_ref[...], dh_ref[...])
            dx_ref[...] = dxo_ref[...] + dxx
            dgpre_ref[...] += dg

    row = pl.BlockSpec((tm, D), lambda i, j: (i, 0))
    vec = pl.BlockSpec((1, D), lambda i, j: (0, 0))
    wcol = pl.BlockSpec((sps, D, F8), lambda i, j: (j, 0, 0))
    wrow = pl.BlockSpec((sps, F8, D), lambda i, j: (j, 0, 0))
    hid = pl.BlockSpec((sps, tm, F8), lambda i, j: (j, i, 0))
    return hosted_call(
        body, comm, _grid_steps(S // tm, nj), name=name, grid=(S // tm, nj),
        in_specs=[row, row, row, hid, hid, vec, vec, wcol, wcol, wrow],
        out_specs=(row, hid, hid, row, vec, vec),
        out_shape=(SDS((S, D), f32), SDS((NS, S, F8), MXU_DTYPE), SDS((NS, S, F8), MXU_DTYPE),
                   SDS((S, D), MXU_DTYPE), SDS((1, D), f32), SDS((1, D), f32)),
        scratch_shapes=[pltpu.VMEM((tm, D), f32)], args=(dxo, x, y, p, q, gpre, gpost, wg, wu, wd))


def ffn_bwd_dw(h, dy, t, da, db, name, comm=None):
    S, D = h.shape
    NS, F8 = t.shape[0], t.shape[-1]
    per = N_DEV // NS
    w8 = F8 // per
    tm = _row_tile(S, FFN_ROWS_DW)
    ni = S // tm

    def body(h_ref, dy_ref, t_ref, da_ref, db_ref, dwg_ref, dwu_ref, dwd_ref, accg, accu, accd):
        i = pl.program_id(1)

        @pl.when(i == 0)
        def _():
            accg[...] = jnp.zeros_like(accg)
            accu[...] = jnp.zeros_like(accu)
            accd[...] = jnp.zeros_like(accd)

        hh = h_ref[...]
        accg[...] += _mm_tn(hh, da_ref[...])
        accu[...] += _mm_tn(hh, db_ref[...])
        accd[...] += _mm_tn(t_ref[...], dy_ref[...])

        @pl.when(i == ni - 1)
        def _():
            for k in range(per):
                ks = slice(k * w8, (k + 1) * w8)
                dwg_ref[k] = accg[:, ks].astype(dwg_ref.dtype)
                dwu_ref[k] = accu[:, ks].astype(dwu_ref.dtype)
                dwd_ref[k] = accd[ks, :].astype(dwd_ref.dtype)

    row = pl.BlockSpec((tm, D), lambda j, i: (i, 0))
    hid = pl.BlockSpec((None, tm, F8), lambda j, i: (j, i, 0))
    wcol = pl.BlockSpec((per, D, w8), lambda j, i: (j, 0, 0))
    wrow = pl.BlockSpec((per, w8, D), lambda j, i: (j, 0, 0))
    return hosted_call(
        body, comm, _grid_steps(NS, ni), name=name, grid=(NS, ni),
        in_specs=[row, row, hid, hid, hid],
        out_specs=(wcol, wcol, wrow),
        out_shape=(SDS((N_DEV, D, w8), MXU_DTYPE), SDS((N_DEV, D, w8), MXU_DTYPE), SDS((N_DEV, w8, D), MXU_DTYPE)),
        scratch_shapes=[pltpu.VMEM((D, F8), f32), pltpu.VMEM((D, F8), f32), pltpu.VMEM((F8, D), f32)],
        args=(h, dy, t, da, db))


def ffn_bwd_dw_one(rows_op, slab_op, hidden_rows, name, comm=None):
    S, D = rows_op.shape
    NS, F8 = slab_op.shape[0], slab_op.shape[-1]
    per = N_DEV // NS
    w8 = F8 // per
    tm = _row_tile(S, FFN_ROWS_DW)
    ni = S // tm

    def body(r_ref, s_ref, o_ref, acc):
        i = pl.program_id(1)

        @pl.when(i == 0)
        def _():
            acc[...] = jnp.zeros_like(acc)

        acc[...] += _mm_tn(s_ref[...], r_ref[...]) if hidden_rows else _mm_tn(r_ref[...], s_ref[...])

        @pl.when(i == ni - 1)
        def _():
            for k in range(per):
                ks = slice(k * w8, (k + 1) * w8)
                o_ref[k] = (acc[ks, :] if hidden_rows else acc[:, ks]).astype(o_ref.dtype)

    blk = (per, w8, D) if hidden_rows else (per, D, w8)
    return hosted_call(
        body, comm, _grid_steps(NS, ni), name=name, grid=(NS, ni),
        in_specs=[pl.BlockSpec((tm, D), lambda j, i: (i, 0)), pl.BlockSpec((None, tm, F8), lambda j, i: (j, i, 0))],
        out_specs=(pl.BlockSpec(blk, lambda j, i: (j, 0, 0)),),
        out_shape=(SDS((N_DEV,) + blk[1:], MXU_DTYPE),),
        scratch_shapes=[pltpu.VMEM((F8, D) if hidden_rows else (D, F8), f32)], args=(rows_op, slab_op))


def rms_mm(x, g, w, w2, name, tn=1024):
    S, D = x.shape
    N = w.shape[1]
    tm = _row_tile(S, PROJ_ROWS)
    tn = _row_tile(N, tn)
    has2 = w2 is not None

    def body(*refs):
        if has2:
            x_ref, g_ref, w_ref, w2_ref, h_ref, o_ref, o2_ref = refs
        else:
            x_ref, g_ref, w_ref, h_ref, o_ref = refs
        j = pl.program_id(1)

        @pl.when(j == 0)
        def _():
            h = _rms(x_ref[...], g_ref[...]).astype(h_ref.dtype)
            h_ref[...] = h
            if has2:
                o2_ref[...] = _mm(h, w2_ref[...])

        o_ref[...] = _mm(h_ref[...], w_ref[...])

    row = pl.BlockSpec((tm, D), lambda i, j: (i, 0))
    in_specs = [row, pl.BlockSpec((1, D), lambda i, j: (0, 0)), pl.BlockSpec((D, tn), lambda i, j: (0, j))]
    out_specs = [row, pl.BlockSpec((tm, tn), lambda i, j: (i, j))]
    out_shape = [SDS((S, D), MXU_DTYPE), SDS((S, N), f32)]
    args = [x, g, w]
    if has2:
        in_specs.append(pl.BlockSpec((D, w2.shape[1]), lambda i, j: (0, 0)))
        out_specs.append(pl.BlockSpec((tm, w2.shape[1]), lambda i, j: (i, 0)))
        out_shape.append(SDS((S, w2.shape[1]), f32))
        args.append(w2)
    return pl.pallas_call(
        body, name=name, grid=(S // tm, N // tn), in_specs=in_specs, out_specs=tuple(out_specs),
        out_shape=tuple(out_shape), compiler_params=_params(2),
    )(*args)


def mm_bwd_dx(dres, x, g, dy, w, dy2, w2, name, tk=1024, comm=None):
    S, D = x.shape
    K = dy.shape[1]
    tm = _row_tile(S, PROJ_ROWS)
    tk = _row_tile(K, tk)
    nk = K // tk
    has2 = dy2 is not None

    def body(*refs):
        if has2:
            dres_ref, x_ref, g_ref, dy_ref, w_ref, dy2_ref, w2_ref, dx_ref, dg_ref, dh_ref = refs
        else:
            dres_ref, x_ref, g_ref, dy_ref, w_ref, dx_ref, dg_ref, dh_ref = refs
        i, k = pl.program_id(0), pl.program_id(1)

        @pl.when(k == 0)
        def _():
            @pl.when(i == 0)
            def _():
                dg_ref[...] = jnp.zeros_like(dg_ref)

            if has2:
                dh_ref[...] = _mm_nt(dy2_ref[...], w2_ref[...])
            else:
                dh_ref[...] = jnp.zeros_like(dh_ref)

        dh_ref[...] += _mm_nt(dy_ref[...], w_ref[...])

        @pl.when(k == nk - 1)
        def _():
            dxx, dg = _rms_bwd(x_ref[...], g_ref[...], dh_ref[...])
            dx_ref[...] = dres_ref[...] + dxx
            dg_ref[...] += dg

    row = pl.BlockSpec((tm, D), lambda i, k: (i, 0))
    vec = pl.BlockSpec((1, D), lambda i, k: (0, 0))
    in_specs = [row, row, vec, pl.BlockSpec((tm, tk), lambda i, k: (i, k)), pl.BlockSpec((D, tk), lambda i, k: (0, k))]
    args = [dres, x, g, dy, w]
    if has2:
        in_specs += [pl.BlockSpec((tm, dy2.shape[1]), lambda i, k: (i, 0)),
                     pl.BlockSpec((D, w2.shape[1]), lambda i, k: (0, 0))]
        args += [dy2, w2]
    return hosted_call(
        body, comm, _grid_steps(S // tm, nk), name=name, grid=(S // tm, nk), in_specs=in_specs, out_specs=(row, vec),
        out_shape=(SDS((S, D), f32), SDS((1, D), f32)), scratch_shapes=[pltpu.VMEM((tm, D), f32)], args=args)


def tn_mm(a, b, name, tn=512, slot_major=False):
    S, K1 = a.shape
    N = b.shape[1]
    tm = _row_tile(S, TN_ROWS)
    tn = _row_tile(N, tn)
    ni = S // tm

    def body(a_ref, b_ref, o_ref, acc):
        i = pl.program_id(1)

        @pl.when(i == 0)
        def _():
            acc[...] = jnp.zeros_like(acc)

        acc[...] += _mm_tn(a_ref[...], b_ref[...])

        @pl.when(i == ni - 1)
        def _():
            o_ref[...] = acc[...].astype(o_ref.dtype)

    if slot_major:
        out_spec, out_shape = pl.BlockSpec((None, K1, tn), lambda j, i: (j, 0, 0)), SDS((N // tn, K1, tn), MXU_DTYPE)
    else:
        out_spec, out_shape = pl.BlockSpec((K1, tn), lambda j, i: (0, j)), SDS((K1, N), MXU_DTYPE)
    return pl.pallas_call(
        body, name=name, grid=(N // tn, ni),
        in_specs=[pl.BlockSpec((tm, K1), lambda j, i: (i, 0)), pl.BlockSpec((tm, tn), lambda j, i: (i, j))],
        out_specs=out_spec, out_shape=out_shape,
        scratch_shapes=[pltpu.VMEM((K1, tn), f32)], compiler_params=_params(2),
    )(a, b)


CONV_ROWS = 512


def _shift_down(cur, prev8, s):
    r = pltpu.roll(cur, s, 0)
    row = lax.broadcasted_iota(jnp.int32, (8, cur.shape[1]), 0)
    top = jnp.where(row < s, pltpu.roll(prev8, s, 0), r[0:8])
    return jnp.concatenate([top, r[8:]], axis=0)


def _shift_up(cur, next8, s):
    n = cur.shape[0]
    r = pltpu.roll(cur, n - s, 0)
    row = lax.broadcasted_iota(jnp.int32, (8, cur.shape[1]), 0)
    bot = jnp.where(row >= 8 - s, pltpu.roll(next8, 8 - s, 0), r[n - 8:])
    return jnp.concatenate([r[:n - 8], bot], axis=0)


def _conv_taps(cur, prev8):
    return [_shift_down(cur, prev8, 3), _shift_down(cur, prev8, 2), _shift_down(cur, prev8, 1), cur]


def _act_qk(c):
    a = _silu(c)
    return a * lax.rsqrt(jnp.sum(a * a, axis=-1, keepdims=True) + L2_EPS)


def dn_prep(proj, conv_w, name):
    S = proj.shape[0]
    W = conv_w.shape[1] // 3
    nh = W // HEAD
    R = _row_tile(S, CONV_ROWS)

    def body(p_ref, w_ref, o_ref):
        j = pl.program_id(0)
        w = w_ref[...]

        def rows(r, prev8):
            cur = p_ref[pl.ds(r, R), :]
            taps = _conv_taps(cur, prev8)
            cv = taps[0] * w[0:1] + taps[1] * w[1:2] + taps[2] * w[2:3] + taps[3] * w[3:4]

            @pl.when(j < 2 * nh)
            def _():
                o_ref[pl.ds(r, R), :] = _act_qk(cv)

            @pl.when(j >= 2 * nh)
            def _():
                o_ref[pl.ds(r, R), :] = _silu(cv)

        rows(0, jnp.zeros((8, HEAD), f32))

        @pl.loop(1, S // R)
        def _(t):
            r = pl.multiple_of(t * R, R)
            rows(r, p_ref[pl.ds(r - 8, 8), :])

    return pl.pallas_call(
        body, name=name, grid=(3 * nh,),
        in_specs=[pl.BlockSpec((S, HEAD), lambda j: (0, j)), pl.BlockSpec((CONV_K, HEAD), lambda j: (0, j))],
        out_specs=pl.BlockSpec((None, S, HEAD), lambda j: (j // nh, 0, j % nh)),
        out_shape=SDS((3, S, W), f32), compiler_params=_params(1),
    )(proj, conv_w)


def dn_prep_bwd(proj, conv_w, dqkv, dz, name, comm=None):
    S = proj.shape[0]
    W = conv_w.shape[1] // 3
    nh = W // HEAD
    nq = 3 * nh
    R = _row_tile(S, CONV_ROWS)
    nr = S // R

    def body(p_ref, w_ref, dq_ref, dz_ref, dp_ref, dw_ref, dc_ref):
        j = pl.program_id(0)

        @pl.when(j >= nq)
        def _():
            dp_ref[...] = dz_ref[...].astype(dp_ref.dtype)

        @pl.when(j < nq)
        def _():
            w = w_ref[...]
            dw_ref[...] = jnp.zeros_like(dw_ref)

            def rows(r, prev8):
                cur = p_ref[pl.ds(r, R), :]
                taps = _conv_taps(cur, prev8)
                cv = taps[0] * w[0:1] + taps[1] * w[1:2] + taps[2] * w[2:3] + taps[3] * w[3:4]
                dn = dq_ref[pl.ds(r, R), :]

                @pl.when(j < 2 * nh)
                def _():
                    dc_ref[pl.ds(r, R), :] = jax.vjp(_act_qk, cv)[1](dn)[0]

                @pl.when(j >= 2 * nh)
                def _():
                    dc_ref[pl.ds(r, R), :] = jax.vjp(_silu, cv)[1](dn)[0]

                dc = dc_ref[pl.ds(r, R), :]
                dw_ref[...] += jnp.concatenate(
                    [jnp.sum(dc * taps[q], axis=0, keepdims=True) for q in range(CONV_K)], axis=0)

            rows(0, jnp.zeros((8, HEAD), f32))

            @pl.loop(1, nr)
            def _(t):
                r = pl.multiple_of(t * R, R)
                rows(r, p_ref[pl.ds(r - 8, 8), :])

            def back(r, next8):
                dc = dc_ref[pl.ds(r, R), :]
                dx = dc * w[3:4]
                for s in (1, 2, 3):
                    dx = dx + _shift_up(dc, next8, s) * w[3 - s:4 - s]
                dp_ref[pl.ds(r, R), :] = dx.astype(dp_ref.dtype)

            @pl.loop(0, nr - 1)
            def _(t):
                r = pl.multiple_of(t * R, R)
                back(r, dc_ref[pl.ds(r + R, 8), :])

            back((nr - 1) * R, jnp.zeros((8, HEAD), f32))

    clamp = lambda j: jnp.minimum(j, nq - 1)
    return hosted_call(
        body, comm, _grid_steps(4 * nh), name=name, grid=(4 * nh,),
        in_specs=[pl.BlockSpec((S, HEAD), lambda j: (0, clamp(j))),
                  pl.BlockSpec((CONV_K, HEAD), lambda j: (0, clamp(j))),
                  pl.BlockSpec((None, S, HEAD), lambda j: (clamp(j) // nh, 0, clamp(j) % nh)),
                  pl.BlockSpec((S, HEAD), lambda j: (0, jnp.maximum(j - nq, 0)))],
        out_specs=(pl.BlockSpec((S, HEAD), lambda j: (0, j)), pl.BlockSpec((CONV_K, HEAD), lambda j: (0, clamp(j)))),
        out_shape=(SDS((S, 4 * W), MXU_DTYPE), SDS((CONV_K, 3 * W), f32)),
        scratch_shapes=[pltpu.VMEM((S, HEAD), f32)], args=(proj, conv_w, dqkv, dz))


def _lane_pick(x, lane):
    sel = lax.broadcasted_iota(jnp.int32, x.shape, 1) == lane
    return jnp.broadcast_to(jnp.sum(jnp.where(sel, x, 0.0), axis=1, keepdims=True), x.shape)


CUM_ROWS = 256


def _sel_mm(m01, x):
    m = _c(m01)
    d = lambda p: lax.dot_general(m, p, (NN, ((), ())), preferred_element_type=f32)
    h1, h2, h3 = _pieces3(x)
    return (d(h1) + d(h2)) + d(h3)


def _chunk_cumsum_matrix(n, transpose):
    r, c = lax.broadcasted_iota(jnp.int32, (n, n), 0), lax.broadcasted_iota(jnp.int32, (n, n), 1)
    sh = int(math.log2(DN_CHUNK))
    same = (r >> sh) == (c >> sh)
    return jnp.where(same & ((r <= c) if transpose else (r >= c)), 1.0, 0.0).astype(f32)


def _gates_by_lane(H, p, al, dt):
    lane = lax.broadcasted_iota(jnp.int32, p.shape, 1)
    g = -jnp.exp(al) * jax.nn.softplus(p + dt)
    return jnp.where(lane < H, jax.nn.sigmoid(p), jnp.where(lane < 2 * H, g, 0.0))


def dn_gates(pba, al, dt, H, name):
    S = pba.shape[0]
    R = _row_tile(S, CUM_ROWS)

    def body(p_ref, al_ref, dt_ref, o_ref):
        raw = _gates_by_lane(H, p_ref[...], al_ref[...], dt_ref[...])
        lane = lax.broadcasted_iota(jnp.int32, raw.shape, 1)
        o_ref[...] = jnp.where(lane < H, raw, _sel_mm(_chunk_cumsum_matrix(R, False), raw))

    blk = pl.BlockSpec((R, HEAD), lambda i: (i, 0))
    par = pl.BlockSpec((1, HEAD), lambda i: (0, 0))
    return pl.pallas_call(body, name=name, grid=(S // R,), in_specs=[blk, par, par], out_specs=blk,
                          out_shape=SDS((S, HEAD), f32), compiler_params=_params(1))(pba, al, dt)


def dn_gates_bwd(pba, al, dt, dgates, H, name):
    S = pba.shape[0]
    R = _row_tile(S, CUM_ROWS)

    def body(p_ref, al_ref, dt_ref, dg_ref, dp_ref, dal_ref, ddt_ref):
        @pl.when(pl.program_id(0) == 0)
        def _():
            dal_ref[...] = jnp.zeros_like(dal_ref)
            ddt_ref[...] = jnp.zeros_like(ddt_ref)

        d = dg_ref[...]
        lane = lax.broadcasted_iota(jnp.int32, d.shape, 1)
        d = jnp.where(lane < H, d, _sel_mm(_chunk_cumsum_matrix(R, True), d))
        _, vjp = jax.vjp(functools.partial(_gates_by_lane, H), p_ref[...], al_ref[...], dt_ref[...])
        dp, dal, ddt = vjp(d)
        dp_ref[...] = dp.astype(dp_ref.dtype)
        dal_ref[...] += dal
        ddt_ref[...] += ddt

    blk = pl.BlockSpec((R, HEAD), lambda i: (i, 0))
    par = pl.BlockSpec((1, HEAD), lambda i: (0, 0))
    return pl.pallas_call(
        body, name=name, grid=(S // R,), in_specs=[blk, par, par, blk], out_specs=(blk, par, par),
        out_shape=(SDS((S, HEAD), MXU_DTYPE), SDS((1, HEAD), f32), SDS((1, HEAD), f32)), compiler_params=_params(1),
    )(pba, al, dt, dgates)


def _bdot(dims):
    back = {NN: ((NT, 'gb'), (TN, 'ag')), NT: ((NN, 'gb'), (TN, 'ga')), TN: ((NT, 'bg'), (NN, 'ag'))}[dims]
    d = lambda p, q, dm: lax.dot_general(_c(p), _c(q), (dm, ((), ())), preferred_element_type=f32)

    @jax.custom_vjp
    def f(a, b):
        return d(a, b, dims)

    def fwd(a, b):
        return d(a, b, dims), (a, b)

    def bwd(res, g):
        v = {'a': res[0], 'b': res[1], 'g': g}
        (da_dims, da_ops), (db_dims, db_ops) = back
        return d(v[da_ops[0]], v[da_ops[1]], da_dims), d(v[db_ops[0]], v[db_ops[1]], db_dims)

    f.defvjp(fwd, bwd)
    return f, lambda a, b: d(a, b, dims)


_BDOT = {dims: _bdot(dims) for dims in (NN, NT, TN)}


def _tri_inv_multi(Ls):
    n = Ls[0].shape[0]
    eye = jnp.where(lax.broadcasted_iota(jnp.int32, (n, n), 0) == lax.broadcasted_iota(jnp.int32, (n, n), 1), 1.0, 0.0)
    P = tuple(-L for L in Ls)
    T = tuple(eye + p for p in P)
    for _ in range(int(math.log2(n)) - 1):
        P = tuple(_dot3(p, p, NN) for p in P)
        T = tuple(t + _dot3(t, p, NN) for t, p in zip(T, P))
    return T


@jax.custom_vjp
def _tri_inv_kept(Ls, Ts):
    return Ts


def _tri_inv_kept_bwd(T, dT):
    X = tuple(_dot3(d, t, NT) for d, t in zip(dT, T))
    return tuple(-_dot3(t, x, TN) for t, x in zip(T, X)), tuple(jnp.zeros_like(t) for t in T)


_tri_inv_kept.defvjp(lambda Ls, Ts: (Ts, Ts), _tri_inv_kept_bwd)


def _pieces3(x):
    h1 = x.astype(MXU_DTYPE)
    r1 = x - h1.astype(f32)
    h2 = r1.astype(MXU_DTYPE)
    return h1, h2, (r1 - h2.astype(f32)).astype(MXU_DTYPE)


def _row_bcast_impl(sel_row, gc):
    s = _c(sel_row)
    d = lambda p: lax.dot_general(s, p, (NT, ((), ())), preferred_element_type=f32)
    h1, h2, h3 = _pieces3(gc)
    return (d(h1) + d(h2)) + d(h3)


def _row_bcast_bwd(sel_row, d):
    s = _c(sel_row)
    hi, lo = _split(d)
    t = lambda p: lax.dot_general(p, s, (TN, ((), ())), preferred_element_type=f32)
    return jnp.zeros_like(sel_row), t(hi) + t(lo)


_row_bcast = jax.custom_vjp(_row_bcast_impl)
_row_bcast.defvjp(lambda sel_row, gc: (_row_bcast_impl(sel_row, gc), sel_row), _row_bcast_bwd)


def _col_bcast_impl(gc):
    return gc[:, :DN_CHUNK]


def _col_bcast_bwd(_, d):
    return (jnp.broadcast_to(jnp.sum(d, axis=1, keepdims=True) * (1.0 / HEAD), (d.shape[0], HEAD)),)


_col_bcast = jax.custom_vjp(_col_bcast_impl)
_col_bcast.defvjp(lambda gc: (_col_bcast_impl(gc), None), _col_bcast_bwd)


def _last_row_bcast(n):
    def impl(gc):
        return jnp.broadcast_to(gc[DN_CHUNK - 1:DN_CHUNK, :], (n, HEAD))

    def bwd(_, d):
        row = lax.broadcasted_iota(jnp.int32, (DN_CHUNK, HEAD), 0)
        return (jnp.where(row == DN_CHUNK - 1, jnp.sum(d, axis=0, keepdims=True), 0.0),)

    f = jax.custom_vjp(impl)
    f.defvjp(lambda gc: (impl(gc), None), bwd)
    return impl, f


_LAST_C, _LAST_H = _last_row_bcast(DN_CHUNK), _last_row_bcast(HEAD)


def _halves(axis):
    def impl(x):
        n = x.shape[axis] // 2
        return lax.slice_in_dim(x, 0, n, axis=axis), lax.slice_in_dim(x, n, 2 * n, axis=axis)

    f = jax.custom_vjp(impl)
    f.defvjp(lambda x: (impl(x), None), lambda _, g: (jnp.concatenate(g, axis=axis),))
    return impl, f


_ROW_HALVES, _COL_HALVES = _halves(0), _halves(1)


def _chunk_consts():
    C = DN_CHUNK
    io = lambda shape, ax: lax.broadcasted_iota(jnp.int32, shape, ax)
    one = lambda m: jnp.where(m, 1.0, 0.0).astype(f32)
    r, c = io((C, C), 0), io((C, C), 1)
    return dict(causal=r >= c, strict=r > c, sel_row=one(io((C, HEAD), 1) == 0))


def _chunk_fn(kc, kept_T, q, k, v, gc, bB, S0):
    diff = kept_T is not None
    i = 0 if diff else 1
    mm, mm_nt, mm_tn = _BDOT[NN][i], _BDOT[NT][i], _BDOT[TN][i]
    tri = (lambda Ls: _tri_inv_kept(Ls, kept_T)) if diff else _tri_inv_multi
    each = lambda f, *ls: tuple(f(*a) for a in zip(*ls))
    gcol = each(_col_bcast if diff else _col_bcast_impl, gc)
    grow = each(lambda g: (_row_bcast if diff else _row_bcast_impl)(kc['sel_row'], g), gc)
    glc = each(_LAST_C[i ^ 1], gc)
    glh = each(_LAST_H[i ^ 1], gc)
    decay = each(lambda a, b: jnp.where(kc['causal'], jnp.exp(jnp.where(kc['causal'], a - b, 0.0)), 0.0), gcol, grow)
    rows, cols = _ROW_HALVES[i ^ 1], _COL_HALVES[i ^ 1]
    first, second = (lambda ts: tuple(t[0] for t in ts)), (lambda ts: tuple(t[1] for t in ts))
    kb = each(lambda a, b: a * b, k, bB)
    vb = each(lambda a, b: a * b, v, bB)
    egc = each(jnp.exp, gc)
    qs = each(lambda a: a * (HEAD ** -0.5), q)
    kq = each(lambda a, b, kt: rows(mm_nt(jnp.concatenate([a, b], axis=0), kt)), kb, qs, k)
    kk, qk = first(kq), second(kq)
    T = tri(each(lambda a, d: jnp.where(kc['strict'], a * d, 0.0), kk, decay))
    uw = each(lambda t, a, b, e: cols(mm(t, jnp.concatenate([a, b * e], axis=1))), T, vb, kb, egc)
    u, w = first(uw), second(uw)
    attn = each(lambda a, d: jnp.where(kc['causal'], a * d, 0.0), qk, decay)
    wq = each(lambda a, b, e, s: rows(mm(jnp.concatenate([a, b * e], axis=0), s)), w, qs, egc, S0)
    wS, qS = first(wq), second(wq)
    v_new = each(lambda a, b: a - b, u, wS)
    o = each(lambda a, b: a + b, qS, each(mm, attn, v_new))
    kdec = each(lambda a, gl, g: a * jnp.exp(gl - g), k, glc, gc)
    S1 = each(lambda s, gl, kv: s * jnp.exp(gl) + kv, S0, glh, each(mm_tn, kdec, v_new))
    return (o, S1) if diff else (o, S1, T)


def _chunks_per_step(N):
    return 4 if N % 4 == 0 else (2 if N % 2 == 0 else 1)


def _heads_per_block(H):
    return 8 if H % 8 == 0 else (4 if H % 4 == 0 else 1)


def dn_chunk_fwd(qkv, gates, name, comm=None):
    _, S, W = qkv.shape
    H, C = W // HEAD, DN_CHUNK
    N, HB = S // C, _heads_per_block(H)
    assert HB == H
    CPS = _chunks_per_step(N)

    def body(q_ref, k_ref, v_ref, g_ref, o_ref, st_ref, t_ref, s_scr):
        @pl.when(pl.program_id(1) == 0)
        def _():
            s_scr[...] = jnp.zeros_like(s_scr)

        kc = _chunk_consts()
        sls = [slice(hh * HEAD, (hh + 1) * HEAD) for hh in range(HB)]
        St = tuple(s_scr[hh] for hh in range(HB))
        for c in range(CPS):
            rows = slice(c * C, (c + 1) * C)
            heads = lambda ref: tuple(ref[rows, sl] for sl in sls)
            gr = g_ref[rows, :]
            for hh in range(HB):
                st_ref[c, hh] = St[hh]
            o, St, T = _chunk_fn(kc, None, heads(q_ref), heads(k_ref), heads(v_ref),
                                 tuple(_lane_pick(gr, H + hh) for hh in range(HB)),
                                 tuple(_lane_pick(gr, hh) for hh in range(HB)), St)
            for hh in range(HB):
                o_ref[rows, sls[hh]] = o[hh]
                t_ref[c, hh] = T[hh]
        for hh in range(HB):
            s_scr[hh] = St[hh]

    part = lambda p: pl.BlockSpec((None, CPS * C, HB * HEAD), lambda hb, n: (p, n, hb))
    return hosted_call(
        body, comm, _grid_steps(H // HB, N // CPS), name=name, grid=(H // HB, N // CPS),
        in_specs=[part(0), part(1), part(2), pl.BlockSpec((CPS * C, HEAD), lambda hb, n: (n, 0))],
        out_specs=(pl.BlockSpec((CPS * C, HB * HEAD), lambda hb, n: (n, hb)),
                   pl.BlockSpec((CPS, HB, HEAD, HEAD), lambda hb, n: (n, hb, 0, 0)),
                   pl.BlockSpec((CPS, HB, C, C), lambda hb, n: (n, hb, 0, 0))),
        out_shape=(SDS((S, W), f32), SDS((N, H, HEAD, HEAD), f32), SDS((N, H, C, C), f32)),
        scratch_shapes=[pltpu.VMEM((HB, HEAD, HEAD), f32)], args=(qkv, qkv, qkv, gates))


def dn_chunk_bwd(qkv, gates, states, kept_T, do, name, comm=None):
    _, S, W = qkv.shape
    H, C = W // HEAD, DN_CHUNK
    N, HB = S // C, _heads_per_block(H)
    assert HB == H
    CPS = _chunks_per_step(N)
    NB = N // CPS

    def body(q_ref, k_ref, v_ref, g_ref, st_ref, t_ref, do_ref, dqkv_ref, dg_ref, ds_scr):
        @pl.when(pl.program_id(1) == 0)
        def _():
            ds_scr[...] = jnp.zeros_like(ds_scr)

        kc = _chunk_consts()
        sls = [slice(hh * HEAD, (hh + 1) * HEAD) for hh in range(HB)]
        dSt = tuple(ds_scr[hh] for hh in range(HB))
        for c in reversed(range(CPS)):
            rows = slice(c * C, (c + 1) * C)
            heads = lambda ref: tuple(ref[rows, sl] for sl in sls)
            gr = g_ref[rows, :]
            kept = tuple(t_ref[c, hh] for hh in range(HB))
            _, vjp = jax.vjp(functools.partial(_chunk_fn, kc, kept), heads(q_ref), heads(k_ref), heads(v_ref),
                             tuple(_lane_pick(gr, H + hh) for hh in range(HB)),
                             tuple(_lane_pick(gr, hh) for hh in range(HB)), tuple(st_ref[c, hh] for hh in range(HB)))
            dq, dk, dv, dg, db, dSt = vjp((heads(do_ref), dSt))
            lane = lax.broadcasted_iota(jnp.int32, (C, HEAD), 1)
            dgr = jnp.zeros((C, HEAD), f32)
            for hh in range(HB):
                dqkv_ref[0, rows, sls[hh]] = dq[hh]
                dqkv_ref[1, rows, sls[hh]] = dk[hh]
                dqkv_ref[2, rows, sls[hh]] = dv[hh]
                dgr = dgr + jnp.where(lane == hh, jnp.sum(db[hh], axis=1, keepdims=True), 0.0)
                dgr = dgr + jnp.where(lane == H + hh, jnp.sum(dg[hh], axis=1, keepdims=True), 0.0)
            dg_ref[rows, :] = dgr
        for hh in range(HB):
            ds_scr[hh] = dSt[hh]

    rev = lambda n: NB - 1 - n
    part = lambda p: pl.BlockSpec((None, CPS * C, HB * HEAD), lambda hb, n: (p, rev(n), hb))
    gate = pl.BlockSpec((CPS * C, HEAD), lambda hb, n: (rev(n), 0))
    return hosted_call(
        body, comm, _grid_steps(H // HB, NB), name=name, grid=(H // HB, NB),
        in_specs=[part(0), part(1), part(2), gate,
                  pl.BlockSpec((CPS, HB, HEAD, HEAD), lambda hb, n: (rev(n), hb, 0, 0)),
                  pl.BlockSpec((CPS, HB, C, C), lambda hb, n: (rev(n), hb, 0, 0)),
                  pl.BlockSpec((CPS * C, HB * HEAD), lambda hb, n: (rev(n), hb))],
        out_specs=(pl.BlockSpec((3, CPS * C, HB * HEAD), lambda hb, n: (0, rev(n), hb)), gate),
        out_shape=(SDS((3, S, W), f32), SDS((S, HEAD), f32)),
        scratch_shapes=[pltpu.VMEM((HB, HEAD, HEAD), f32)], args=(qkv, qkv, qkv, gates, states, kept_T, do))


def _gate_norm(o, z, ng):
    return _rms(o, ng) * _silu(z)


def dn_out(o, proj, ng, wout, x1, g3, name):
    S, W = o.shape
    D = x1.shape[1]
    nh = W // HEAD
    tm = _row_tile(S, 256)

    def body(o_ref, z_ref, ng_ref, w_ref, x_ref, g_ref, xo_ref, m_ref, og_ref):
        for h in range(nh):
            sl = slice(h * HEAD, (h + 1) * HEAD)
            og_ref[:, sl] = _gate_norm(o_ref[:, sl], z_ref[:, sl], ng_ref[...]).astype(og_ref.dtype)
        m = _mm(og_ref[...], w_ref[...])
        m_ref[...] = m
        xo_ref[...] = x_ref[...] + _rms(m, g_ref[...])

    rw = pl.BlockSpec((tm, W), lambda i: (i, 0))
    rd = pl.BlockSpec((tm, D), lambda i: (i, 0))
    return pl.pallas_call(
        body, name=name, grid=(S // tm,),
        in_specs=[rw, pl.BlockSpec((tm, W), lambda i: (i, 3)), pl.BlockSpec((1, HEAD), lambda i: (0, 0)),
                  pl.BlockSpec((W, D), lambda i: (0, 0)), rd, pl.BlockSpec((1, D), lambda i: (0, 0))],
        out_specs=(rd, rd, rw),
        out_shape=(SDS((S, D), f32), SDS((S, D), f32), SDS((S, W), MXU_DTYPE)), compiler_params=_params(1),
    )(o, proj, ng, wout, x1, g3)


def dn_out_bwd(dxo, m, g3, o, proj, ng, wout, name):
    S, W = o.shape
    D = m.shape[1]
    nh = W // HEAD
    tm = _row_tile(S, 256)

    def body(dxo_ref, m_ref, g_ref, o_ref, z_ref, ng_ref, w_ref, dm_ref, do_ref, dz_ref, dng_ref, dg_ref):
        @pl.when(pl.program_id(0) == 0)
        def _():
            dng_ref[...] = jnp.zeros_like(dng_ref)
            dg_ref[...] = jnp.zeros_like(dg_ref)

        dm, dg = _rms_bwd(m_ref[...], g_ref[...], dxo_ref[...])
        dg_ref[...] += dg
        dmc = dm.astype(dm_ref.dtype)
        dm_ref[...] = dmc
        dog = _mm_nt(dmc, w_ref[...])
        for h in range(nh):
            sl = slice(h * HEAD, (h + 1) * HEAD)
            _, vjp = jax.vjp(_gate_norm, o_ref[:, sl], z_ref[:, sl], ng_ref[...])
            do, dz, dng = vjp(dog[:, sl])
            do_ref[:, sl] = do
            dz_ref[:, sl] = dz.astype(dz_ref.dtype)
            dng_ref[...] += dng

    rw = pl.BlockSpec((tm, W), lambda i: (i, 0))
    rd = pl.BlockSpec((tm, D), lambda i: (i, 0))
    vd = pl.BlockSpec((1, D), lambda i: (0, 0))
    vh = pl.BlockSpec((1, HEAD), lambda i: (0, 0))
    return pl.pallas_call(
        body, name=name, grid=(S // tm,),
        in_specs=[rd, rd, vd, rw, pl.BlockSpec((tm, W), lambda i: (i, 3)), vh, pl.BlockSpec((W, D), lambda i: (0, 0))],
        out_specs=(rd, rw, rw, vh, vd),
        out_shape=(SDS((S, D), MXU_DTYPE), SDS((S, W), f32), SDS((S, W), MXU_DTYPE), SDS((1, HEAD), f32),
                   SDS((1, D), f32)),
        compiler_params=_params(1),
    )(dxo, m, g3, o, proj, ng, wout)


def _erf_arg(x):
    return lax.erf(x * 0.7071067811865476)


@jax.custom_vjp
def _gelu_with_erf(x, e):
    return 0.5 * x * (1.0 + e)


def _gelu_with_erf_bwd(res, g):
    x, e = res
    return g * (0.5 * (1.0 + e) + x * (jnp.exp(-0.5 * x * x) * 0.3989422804014327)), jnp.zeros_like(e)


_gelu_with_erf.defvjp(lambda x, e: (0.5 * x * (1.0 + e), (x, e)), _gelu_with_erf_bwd)


def _layernorm(t, lg, lb):
    tc = t - jnp.mean(t, axis=-1, keepdims=True)
    return tc * lax.rsqrt(jnp.mean(tc * tc, axis=-1, keepdims=True) + LN_EPS) * lg + lb


def _sg_stage1_kept(eu, ev, pu, pv, bu, bv, lg, lb):
    return _gelu_with_erf(pu + bu, eu), _layernorm(_gelu_with_erf(pv + bv, ev), lg, lb)


def _causal_mask(n):
    return lax.broadcasted_iota(jnp.int32, (n, n), 0) >= lax.broadcasted_iota(jnp.int32, (n, n), 1)


def sg_mid(pre, b_in, ln_g, ln_b, w_s, bsT, wout, x1, g3, name):
    S = pre.shape[0]
    E, D = ln_g.shape[1], x1.shape[1]
    G, CH = SG_GROUPS, SG_CHUNK
    Cg = E // G
    tm = _row_tile(S, 256)

    def body(pu_ref, pv_ref, bu_ref, bv_ref, lg_ref, lb_ref, ws_ref, bs_ref, w_ref, x_ref, g_ref,
             xo_ref, m_ref, gt_ref, e_ref):
        xu, xv = pu_ref[...] + bu_ref[...], pv_ref[...] + bv_ref[...]
        eu, ev = _erf_arg(xu), _erf_arg(xv)
        e_ref[:, :E] = eu.astype(e_ref.dtype)
        e_ref[:, E:] = ev.astype(e_ref.dtype)
        u = 0.5 * xu * (1.0 + eu)
        v = _layernorm(0.5 * xv * (1.0 + ev), lg_ref[...], lb_ref[...])
        mask = _causal_mask(CH)
        for g in range(G):
            wc = _c(jnp.where(mask, ws_ref[g], 0.0))
            bcol = bs_ref[:, g:g + 1]
            cs = slice(g * Cg, (g + 1) * Cg)
            for ch in range(tm // CH):
                rs = slice(ch * CH, (ch + 1) * CH)
                mixed = _mm(wc, _c(v[rs, cs])) + bcol
                gt_ref[rs, cs] = (u[rs, cs] * mixed).astype(gt_ref.dtype)
        m = _mm(gt_ref[...], w_ref[...])
        m_ref[...] = m
        xo_ref[...] = x_ref[...] + _rms(m, g_ref[...])

    half = lambda p: pl.BlockSpec((tm, E), lambda i: (i, p))
    vhalf = lambda p: pl.BlockSpec((1, E), lambda i: (0, p))
    ve = pl.BlockSpec((1, E), lambda i: (0, 0))
    rd = pl.BlockSpec((tm, D), lambda i: (i, 0))
    return pl.pallas_call(
        body, name=name, grid=(S // tm,),
        in_specs=[half(0), half(1), vhalf(0), vhalf(1), ve, ve, pl.BlockSpec((G, CH, CH), lambda i: (0, 0, 0)),
                  pl.BlockSpec((CH, G), lambda i: (0, 0)), pl.BlockSpec((E, D), lambda i: (0, 0)), rd,
                  pl.BlockSpec((1, D), lambda i: (0, 0))],
        out_specs=(rd, rd, pl.BlockSpec((tm, E), lambda i: (i, 0)), pl.BlockSpec((tm, 2 * E), lambda i: (i, 0))),
        out_shape=(SDS((S, D), f32), SDS((S, D), f32), SDS((S, E), MXU_DTYPE), SDS((S, 2 * E), MXU_DTYPE)),
        compiler_params=_params(1),
    )(pre, pre, b_in, b_in, ln_g, ln_b, w_s, bsT, wout, x1, g3)


def sg_mid_bwd(dxo, m, g3, pre, kept_erf, b_in, ln_g, ln_b, w_s, bsT, wout, name):
    S = pre.shape[0]
    E, D = ln_g.shape[1], m.shape[1]
    G, CH = SG_GROUPS, SG_CHUNK
    Cg = E // G
    tm = _row_tile(S, 256)

    def body(dxo_ref, m_ref, g_ref, pu_ref, pv_ref, eu_ref, ev_ref, bu_ref, bv_ref, lg_ref, lb_ref, ws_ref, bs_ref,
             w_ref, dm_ref, dpre_ref, dbin_ref, dlg_ref, dlb_ref, dws_ref, dbs_ref, dg_ref, du_scr, dv_scr):
        @pl.when(pl.program_id(0) == 0)
        def _():
            for r in (dbin_ref, dlg_ref, dlb_ref, dws_ref, dbs_ref, dg_ref):
                r[...] = jnp.zeros_like(r)

        dm, dg = _rms_bwd(m_ref[...], g_ref[...], dxo_ref[...])
        dg_ref[...] += dg
        dmc = dm.astype(dm_ref.dtype)
        dm_ref[...] = dmc
        dgated = _mm_nt(dmc, w_ref[...])
        stage1 = functools.partial(_sg_stage1_kept, eu_ref[...].astype(f32), ev_ref[...].astype(f32))
        (u, v), vjp1 = jax.vjp(stage1, pu_ref[...], pv_ref[...], bu_ref[...], bv_ref[...], lg_ref[...], lb_ref[...])
        mask = _causal_mask(CH)
        lane = lax.broadcasted_iota(jnp.int32, (CH, CH), 1)
        for g in range(G):
            wc = _c(jnp.where(mask, ws_ref[g], 0.0))
            bcol = bs_ref[:, g:g + 1]
            cs = slice(g * Cg, (g + 1) * Cg)
            dws = jnp.zeros((CH, CH), f32)
            dbs = jnp.zeros((CH, 1), f32)
            for ch in range(tm // CH):
                rs = slice(ch * CH, (ch + 1) * CH)
                vs = _c(v[rs, cs])
                mixed = _mm(wc, vs) + bcol
                dgt = dgated[rs, cs]
                du_scr[rs, cs] = dgt * mixed
                dmixed = dgt * u[rs, cs]
                dmc2 = _c(dmixed)
                dv_scr[rs, cs] = _mm_tn(wc, dmc2)
                dws = dws + _mm_nt(dmc2, vs)
                dbs = dbs + jnp.sum(dmixed, axis=1, keepdims=True)
            dws_ref[g] += jnp.where(mask, dws, 0.0)
            dbs_ref[...] += jnp.where(lane == g, jnp.broadcast_to(dbs, (CH, CH)), 0.0)
        dpu, dpv, dbu, dbv, dlg, dlb = vjp1((du_scr[...], dv_scr[...]))
        dpre_ref[:, :E] = dpu.astype(dpre_ref.dtype)
        dpre_ref[:, E:] = dpv.astype(dpre_ref.dtype)
        dbin_ref[:, :E] += dbu
        dbin_ref[:, E:] += dbv
        dlg_ref[...] += dlg
        dlb_ref[...] += dlb

    half = lambda p: pl.BlockSpec((tm, E), lambda i: (i, p))
    vhalf = lambda p: pl.BlockSpec((1, E), lambda i: (0, p))
    ve = pl.BlockSpec((1, E), lambda i: (0, 0))
    rd = pl.BlockSpec((tm, D), lambda i: (i, 0))
    vd = pl.BlockSpec((1, D), lambda i: (0, 0))
    wsb = pl.BlockSpec((G, CH, CH), lambda i: (0, 0, 0))
    return pl.pallas_call(
        body, name=name, grid=(S // tm,),
        in_specs=[rd, rd, vd, half(0), half(1), half(0), half(1), vhalf(0), vhalf(1), ve, ve, wsb,
                  pl.BlockSpec((CH, G), lambda i: (0, 0)), pl.BlockSpec((E, D), lambda i: (0, 0))],
        out_specs=(rd, pl.BlockSpec((tm, 2 * E), lambda i: (i, 0)), pl.BlockSpec((1, 2 * E), lambda i: (0, 0)), ve, ve,
                   wsb, pl.BlockSpec((CH, CH), lambda i: (0, 0)), vd),
        out_shape=(SDS((S, D), MXU_DTYPE), SDS((S, 2 * E), MXU_DTYPE), SDS((1, 2 * E), f32), SDS((1, E), f32),
                   SDS((1, E), f32), SDS((G, CH, CH), f32), SDS((CH, CH), f32), SDS((1, D), f32)),
        scratch_shapes=[pltpu.VMEM((tm, E), f32), pltpu.VMEM((tm, E), f32)], compiler_params=_params(1),
    )(dxo, m, g3, pre, pre, kept_erf, kept_erf, b_in, b_in, ln_g, ln_b, w_s, bsT, wout)


def loss_head(y, target, name):
    S, D = y.shape
    tm = _row_tile(S, 512)

    def body(y_ref, t_ref, l_ref, d_ref):
        @pl.when(pl.program_id(0) == 0)
        def _():
            l_ref[...] = jnp.zeros_like(l_ref)

        e = y_ref[...] - t_ref[...]
        d_ref[...] = e * (1.0 / D)
        l_ref[...] += jnp.sum(e * e) * (0.5 / D)

    row = pl.BlockSpec((tm, D), lambda i: (i, 0))
    return pl.pallas_call(
        body, name=name, grid=(S // tm,), in_specs=[row, row],
        out_specs=(pl.BlockSpec((1, HEAD), lambda i: (0, 0)), row),
        out_shape=(SDS((1, HEAD), f32), SDS((S, D), f32)), compiler_params=_params(1),
    )(y, target)


def sum_slots(r, name):
    _, R, C = r.shape
    tr = R // 2 if R % 16 == 0 else R

    def body(r_ref, o_ref):
        acc = r_ref[0].astype(f32)
        for s in range(1, N_DEV):
            acc = acc + r_ref[s].astype(f32)
        o_ref[...] = acc

    return pl.pallas_call(
        body, name=name, grid=(R // tr,), in_specs=[pl.BlockSpec((N_DEV, tr, C), lambda i: (0, i, 0))],
        out_specs=pl.BlockSpec((tr, C), lambda i: (i, 0)), out_shape=SDS((R, C), f32), compiler_params=_params(1),
    )(r)


def _adam_math(w, g, m, v):
    m = ADAM_B1 * m + (1.0 - ADAM_B1) * g
    v = ADAM_B2 * v + (1.0 - ADAM_B2) * (g * g)
    m_hat = m / (1.0 - ADAM_B1 ** ADAM_STEP)
    v_hat = v / (1.0 - ADAM_B2 ** ADAM_STEP)
    delta = -ADAM_LR * (m_hat / (jnp.sqrt(v_hat) + ADAM_EPS) + ADAM_WD * w)
    return delta, m, v


def adam_slots(w, rs, m, v, name, tr):
    R, C = w.shape
    tr = _row_tile(min(r.shape[1] for r in rs), tr)
    blocks = [r.shape[1] // tr for r in rs]
    starts = [sum(blocks[:k]) for k in range(len(rs))]
    assert sum(blocks) * tr == R

    def body(w_ref, *refs):
        r_refs, (m_ref, v_ref, g_ref, d_ref, mo_ref, vo_ref) = refs[:len(rs)], refs[len(rs):]
        i = pl.program_id(0)
        for k, r_ref in enumerate(r_refs):
            @pl.when((i >= starts[k]) & (i < starts[k] + blocks[k]))
            def _():
                g = r_ref[0].astype(f32)
                for s in range(1, N_DEV):
                    g = g + r_ref[s].astype(f32)
                g_ref[...] = g

        d_ref[...], mo_ref[...], vo_ref[...] = _adam_math(w_ref[...], g_ref[...], m_ref[...], v_ref[...])

    row = pl.BlockSpec((tr, C), lambda i: (i, 0))
    piece = lambda k: pl.BlockSpec((N_DEV, tr, C), lambda i: (0, jnp.clip(i - starts[k], 0, blocks[k] - 1), 0))
    return pl.pallas_call(
        body, name=name, grid=(R // tr,), in_specs=[row] + [piece(k) for k in range(len(rs))] + [row, row],
        out_specs=(row, row, row, row), out_shape=tuple(SDS((R, C), f32) for _ in range(4)),
        compiler_params=_params(1),
    )(w, *rs, m, v)


def adam_small(w, g, m, v, name):
    def body(w_ref, g_ref, m_ref, v_ref, d_ref, mo_ref, vo_ref):
        d_ref[...], mo_ref[...], vo_ref[...] = _adam_math(w_ref[...], g_ref[...], m_ref[...], v_ref[...])

    return pl.pallas_call(body, name=name, out_shape=tuple(SDS(w.shape, f32) for _ in range(3)))(w, g, m, v)


def _pack_rows(parts):
    rows, offs, r = [], [], 0
    for p in parts:
        flat = p.reshape(-1)
        n = -(-flat.shape[0] // HEAD)
        flat = jnp.pad(flat, (0, n * HEAD - flat.shape[0]))
        rows.append(flat.reshape(n, HEAD))
        offs.append((r, n))
        r += n
    pad = (-r) % 8
    if pad:
        rows.append(jnp.zeros((pad, HEAD), f32))
    return jnp.concatenate(rows, axis=0), offs


def kernel(x, norm_g, ffn_w_gate, ffn_w_up, ffn_w_down, dn_w_in, dn_conv_w, dn_a_log, dn_dt_bias, dn_norm_g, dn_w_out, sg_w_in, sg_b_in, sg_ln_g, sg_ln_b, sg_w_s, sg_b_s, sg_w_out, loss_target, m_norm_g, m_ffn_w_gate, m_ffn_w_up, m_ffn_w_down, m_dn_w_in, m_dn_conv_w, m_dn_a_log, m_dn_dt_bias, m_dn_norm_g, m_dn_w_out, m_sg_w_in, m_sg_b_in, m_sg_ln_g, m_sg_ln_b, m_sg_w_s, m_sg_b_s, m_sg_w_out, v_norm_g, v_ffn_w_gate, v_ffn_w_up, v_ffn_w_down, v_dn_w_in, v_dn_conv_w, v_dn_a_log, v_dn_dt_bias, v_dn_norm_g, v_dn_w_out, v_sg_w_in, v_sg_b_in, v_sg_ln_g, v_sg_ln_b, v_sg_w_s, v_sg_b_s, v_sg_w_out):
    weights = dict(norm_g=norm_g, ffn_w_gate=ffn_w_gate, ffn_w_up=ffn_w_up, ffn_w_down=ffn_w_down, dn_w_in=dn_w_in,
                   dn_conv_w=dn_conv_w, dn_a_log=dn_a_log, dn_dt_bias=dn_dt_bias, dn_norm_g=dn_norm_g,
                   dn_w_out=dn_w_out, sg_w_in=sg_w_in, sg_b_in=sg_b_in, sg_ln_g=sg_ln_g, sg_ln_b=sg_ln_b,
                   sg_w_s=sg_w_s, sg_b_s=sg_b_s, sg_w_out=sg_w_out)
    mom_m = dict(norm_g=m_norm_g, ffn_w_gate=m_ffn_w_gate, ffn_w_up=m_ffn_w_up, ffn_w_down=m_ffn_w_down,
                 dn_w_in=m_dn_w_in, dn_conv_w=m_dn_conv_w, dn_a_log=m_dn_a_log, dn_dt_bias=m_dn_dt_bias,
                 dn_norm_g=m_dn_norm_g, dn_w_out=m_dn_w_out, sg_w_in=m_sg_w_in, sg_b_in=m_sg_b_in,
                 sg_ln_g=m_sg_ln_g, sg_ln_b=m_sg_ln_b, sg_w_s=m_sg_w_s, sg_b_s=m_sg_b_s, sg_w_out=m_sg_w_out)
    mom_v = dict(norm_g=v_norm_g, ffn_w_gate=v_ffn_w_gate, ffn_w_up=v_ffn_w_up, ffn_w_down=v_ffn_w_down,
                 dn_w_in=v_dn_w_in, dn_conv_w=v_dn_conv_w, dn_a_log=v_dn_a_log, dn_dt_bias=v_dn_dt_bias,
                 dn_norm_g=v_dn_norm_g, dn_w_out=v_dn_w_out, sg_w_in=v_sg_w_in, sg_b_in=v_sg_b_in,
                 sg_ln_g=v_sg_ln_g, sg_ln_b=v_sg_ln_b, sg_w_s=v_sg_w_s, sg_b_s=v_sg_b_s, sg_w_out=v_sg_w_out)
    order = list(weights)

    xs = x[0]
    S, D = xs.shape
    F8 = ffn_w_gate.shape[-1]
    depth = norm_g.shape[0]
    W = dn_w_out.shape[1] * N_DEV
    H = W // HEAD
    E = sg_ln_g.shape[1] * N_DEV
    G, CH = sg_w_s.shape[1], sg_w_s.shape[2]
    c8 = dn_w_in.shape[2]
    me = _slot(lax.axis_index("x"), lax.axis_index("y"), lax.axis_index("c"))

    assert depth == 2
    small_in, small_offs = _pack_rows([norm_g, dn_conv_w, sg_b_in, sg_ln_g, sg_ln_b])
    wg0a, wu0a, wd0a, small_all = all_gather_multi(
        [_c(ffn_w_gate[0, 0]), _c(ffn_w_up[0, 0]), _c(ffn_w_down[0, 0]), small_in], name="gather_first")
    ffn_shards = lambda l, ab: [_c(ffn_w_gate[l, ab]), _c(ffn_w_up[l, ab]), _c(ffn_w_down[l, ab])]
    gather_dn = Comm("gather", [_c(dn_w_in[0]), _c(dn_w_out[0])])
    gather_mid = Comm("gather", ffn_shards(0, 1) + ffn_shards(1, 0))
    gather_end = Comm("gather", ffn_shards(1, 1))
    gather_sg = Comm("gather", [_c(sg_w_in[0]), _c(sg_w_out[0])])
    per = N_DEV // FFN_SLABS
    wide = lambda tag, g, u, d: (*widen_slabs([g, u], FFN_SLABS, name=f"widen_{tag}"),
                                 d.reshape(FFN_SLABS, per * F8, D))
    ffn_w = {(0, 0): wide("0a", wg0a, wu0a, wd0a)}

    def small_piece(i, shard_shape):
        r0, n = small_offs[i]
        sz = math.prod(shard_shape)
        return small_all[:, r0:r0 + n, :].reshape(N_DEV, n * HEAD)[:, :sz].reshape((N_DEV,) + tuple(shard_shape))

    ng_full = jnp.moveaxis(small_piece(0, norm_g.shape), 0, 2).reshape(depth, 6, D)
    conv_full = jnp.moveaxis(small_piece(1, dn_conv_w.shape[1:]), 0, 1).reshape(CONV_K, 3 * W)
    bin_full = small_piece(2, sg_b_in.shape[1:]).reshape(1, 2 * E)
    lng_full = small_piece(3, sg_ln_g.shape[1:]).reshape(1, E)
    lnb_full = small_piece(4, sg_ln_b.shape[1:]).reshape(1, E)
    gate_lanes = lambda v: jnp.pad(v.reshape(1, H), ((0, 0), (H, HEAD - 2 * H)))
    al_row, dt_row = gate_lanes(dn_a_log), gate_lanes(dn_dt_bias)
    bsT = sg_b_s[0].T
    gvec = lambda l, k: ng_full[l, k].reshape(1, D)

    saved = []
    cur = xs
    for l in range(depth):
        sv = {}
        sv['x0'] = cur
        (cur, sv['hA'], sv['pA'], sv['qA'], sv['tA'], sv['yA']), got = ffn_fwd(
            cur, gvec(l, 0), gvec(l, 1), *ffn_w[l, 0], name=f"ffn_fwd_{l}a", comm=gather_dn if l == 0 else gather_sg)
        sv['x1'] = cur
        if l == 1:
            sg_win = jnp.moveaxis(got[0], 0, 1).reshape(D, 2 * E)
            sg_wout = got[1].reshape(E, D)
        if l == 0:
            dnin_all, dnout_all = got
            dn_wmain, dn_wba = join_columns(dnin_all, 4 * W, name="dn_w_in_join")
            dn_wout = dnout_all.reshape(W, D)
            sv['hM'], sv['proj'], sv['pba'] = rms_mm(cur, gvec(l, 2), dn_wmain, dn_wba, name=f"dn_in_{l}")
            sv['qkv'] = dn_prep(sv['proj'], conv_full, name=f"dn_prep_{l}")
            sv['gates'] = dn_gates(sv['pba'], al_row, dt_row, H, name=f"dn_gates_{l}")
            (sv['o'], sv['states'], sv['T']), got = dn_chunk_fwd(sv['qkv'], sv['gates'], name=f"dn_chunk_{l}",
                                                                 comm=gather_mid)
            ffn_w[0, 1], ffn_w[1, 0] = wide("0b", *got[0:3]), wide("1a", *got[3:6])
            cur, sv['m'], sv['og'] = dn_out(sv['o'], sv['proj'], dn_norm_g, dn_wout, cur, gvec(l, 3), name=f"dn_out_{l}")
        else:
            sv['hM'], sv['pre'] = rms_mm(cur, gvec(l, 2), sg_win, None, name=f"sg_in_{l}")
            cur, sv['m'], sv['gated'], sv['erf'] = sg_mid(sv['pre'], bin_full, lng_full, lnb_full, sg_w_s[0], bsT, sg_wout,
                                                          cur, gvec(l, 3), name=f"sg_mid_{l}")
        sv['x2'] = cur
        (cur, sv['hB'], sv['pB'], sv['qB'], sv['tB'], sv['yB']), got = ffn_fwd(
            cur, gvec(l, 4), gvec(l, 5), *ffn_w[l, 1], name=f"ffn_fwd_{l}b", comm=gather_end if l == 0 else None)
        if l == 0:
            ffn_w[1, 1] = wide("1b", *got[0:3])
        saved.append(sv)

    loss_blk, dcur = loss_head(cur, loss_target[0], name="loss_head")
    loss = lax.psum(loss_blk[0, 0], ("x", "y", "c"))

    dng = [[None] * 6 for _ in range(depth)]
    ffn_dw = {}
    grads, slots = {}, {}

    def ffn_backward(l, ab, dcur, exchange=None):
        sv, s = saved[l], 'AB'[ab]
        (dcur, da, db, dy, dng[l][4 * ab], dng[l][4 * ab + 1]), got = ffn_bwd_dx(
            dcur, sv['x2' if ab else 'x0'], sv['y' + s], sv['p' + s], sv['q' + s], gvec(l, 4 * ab), gvec(l, 4 * ab + 1),
            *ffn_w[l, ab], name=f"ffn_bwd_{l}{'ab'[ab]}", comm=Comm("exchange", exchange) if exchange else None)
        ffn_dw[l, ab], _ = ffn_bwd_dw(sv['h' + s], dy, sv['t' + s], da, db, name=f"ffn_dw_{l}{'ab'[ab]}")
        return dcur, got

    sv = saved[1]
    dcur, _ = ffn_backward(1, 1, dcur)
    dm, dpre, grads['sg_b_in'], grads['sg_ln_g'], grads['sg_ln_b'], grads['sg_w_s'], dbs, dng[1][3] = sg_mid_bwd(
        dcur, sv['m'], gvec(1, 3), sv['pre'], sv['erf'], bin_full, lng_full, lnb_full, sg_w_s[0], bsT, sg_wout,
        name="sg_mid_bwd_1")
    grads['sg_b_s'] = dbs[:, :G].T
    dsg_wout = tn_mm(sv['gated'], dm, name="sg_wout_dw_1").reshape(N_DEV, E // N_DEV, D)
    dsg_win = tn_mm(sv['hM'], dpre, name="sg_win_dw_1", tn=2 * E // N_DEV, slot_major=True)
    (dcur, dng[1][2]), _ = mm_bwd_dx(dcur, sv['x1'], gvec(1, 2), dpre, sg_win, None, None, name="sg_in_bwd_1")
    dcur, l1b = ffn_backward(1, 0, dcur, exchange=list(ffn_dw[1, 1]))
    sv = saved[0]
    dcur, got = ffn_backward(0, 1, dcur, exchange=[dsg_win, dsg_wout])
    slots['sg_w_in'], slots['sg_w_out'] = [got[0]], [got[1]]
    dm, do, dz, grads['dn_norm_g'], dng[0][3] = dn_out_bwd(dcur, sv['m'], gvec(0, 3), sv['o'], sv['proj'], dn_norm_g,
                                                          dn_wout, name="dn_out_bwd_0")
    ddn_wout = tn_mm(sv['og'], dm, name="dn_wout_dw_0").reshape(N_DEV, W // N_DEV, D)
    (dqkv, dgates), got = dn_chunk_bwd(sv['qkv'], sv['gates'], sv['states'], sv['T'], do, name="dn_chunk_bwd_0",
                                       comm=Comm("exchange", list(ffn_dw[1, 0])))
    l1a = got
    dpba, dal, ddt = dn_gates_bwd(sv['pba'], al_row, dt_row, dgates, H, name="dn_gates_bwd_0")
    grads['dn_a_log'] = dal[:, H:2 * H]
    grads['dn_dt_bias'] = ddt[:, H:2 * H]
    (dproj, grads['dn_conv_w']), got = dn_prep_bwd(sv['proj'], conv_full, dqkv, dz, name="dn_prep_bwd_0",
                                                   comm=Comm("exchange", [*ffn_dw[0, 1], ddn_wout]))
    l0b, slots['dn_w_out'] = got[0:3], [got[3]]
    dw_main = tn_mm(sv['hM'], dproj, name="dn_win_dw_0")
    dw_ba = tn_mm(sv['hM'], dpba, name="dn_wba_dw_0", tn=HEAD)
    ddn_win = split_columns(dw_main, dw_ba, c8, name="dn_w_in_split")
    (dcur, dng[0][2]), got = mm_bwd_dx(dcur, sv['x1'], gvec(0, 2), dproj, dn_wmain, dpba, dn_wba, name="dn_in_bwd_0",
                                       comm=Comm("exchange", [ddn_win]))
    slots['dn_w_in'] = [got[0]]
    small_names = ['norm_g', 'dn_conv_w', 'sg_b_in', 'sg_ln_g', 'sg_ln_b', 'sg_w_s', 'sg_b_s', 'dn_a_log',
                   'dn_dt_bias', 'dn_norm_g']
    small = {}

    def gather_small():
        dng_full = jnp.stack([jnp.concatenate(r, axis=0) for r in dng], axis=0)
        small['parts'] = [dng_full, grads['dn_conv_w'], grads['sg_b_in'], grads['sg_ln_g'], grads['sg_ln_b'],
                          grads['sg_w_s'], grads['sg_b_s'], grads['dn_a_log'], grads['dn_dt_bias'], grads['dn_norm_g']]
        pack, small['offs'] = _pack_rows(small['parts'])
        return Comm("gather", [pack])

    (dcur, da, db, dy, dng[0][0], dng[0][1]), _ = ffn_bwd_dx(
        dcur, sv['x0'], sv['yA'], sv['pA'], sv['qA'], gvec(0, 0), gvec(0, 1), *ffn_w[0, 0], name="ffn_bwd_0a")
    grad_x = dcur[None]
    (dg,), (small_slots,) = ffn_bwd_dw_one(sv['hA'], da, False, name="ffn_dw_0a_gate", comm=gather_small())
    (du,), (xg,) = ffn_bwd_dw_one(sv['hA'], db, False, name="ffn_dw_0a_up", comm=Comm("exchange", [dg]))
    (dd,), (xu,) = ffn_bwd_dw_one(dy, sv['tA'], True, name="ffn_dw_0a_down", comm=Comm("exchange", [du]))
    small_parts, offs = small['parts'], small['offs']
    l0a = [xg, xu, exchange_slots([dd], name="exchange_last")[0]]
    for i, nm in enumerate(['ffn_w_gate', 'ffn_w_up', 'ffn_w_down']):
        slots[nm] = [l0a[i], l0b[i], l1a[i], l1b[i]]
    big_names = ['ffn_w_gate', 'ffn_w_up', 'ffn_w_down', 'dn_w_in', 'dn_w_out', 'sg_w_in', 'sg_w_out']
    slots = [slots[nm] for nm in big_names]
    small_sum = sum_slots(small_slots, name="sum_small_grads")

    def small_grad(i):
        r0, n = offs[i]
        p = small_parts[i]
        return small_sum[r0:r0 + n].reshape(-1)[:p.size].reshape(p.shape)

    def my_shard(full, axis, like):
        n = full.shape[axis] // N_DEV
        return lax.dynamic_slice_in_dim(full, me * n, n, axis).reshape(like.shape)

    g_small = {
        'norm_g': my_shard(small_grad(0), 2, norm_g),
        'dn_conv_w': my_shard(small_grad(1), 1, dn_conv_w),
        'sg_b_in': my_shard(small_grad(2), 1, sg_b_in),
        'sg_ln_g': my_shard(small_grad(3), 1, sg_ln_g),
        'sg_ln_b': my_shard(small_grad(4), 1, sg_ln_b),
        'sg_w_s': small_grad(5).reshape(sg_w_s.shape),
        'sg_b_s': small_grad(6).reshape(sg_b_s.shape),
        'dn_a_log': small_grad(7).reshape(dn_a_log.shape),
        'dn_dt_bias': small_grad(8).reshape(dn_dt_bias.shape),
        'dn_norm_g': small_grad(9).reshape(dn_norm_g.shape),
    }

    out_g, out_d, out_m, out_v = {}, {}, {}, {}
    for nm, r in zip(big_names, slots):
        w = weights[nm]
        cols = w.shape[-1]
        rows = w.size // cols
        tr = {'ffn_w_gate': 512, 'ffn_w_up': 512, 'ffn_w_down': F8 // 2, 'dn_w_in': 256, 'sg_w_in': 256}.get(nm, rows)
        pieces = [p.reshape(N_DEV, -1, cols) for p in r]
        g, d, m2, v2 = adam_slots(w.reshape(rows, cols), pieces, mom_m[nm].reshape(rows, cols),
                                  mom_v[nm].reshape(rows, cols), name=f"adam_{nm}", tr=tr)
        out_g[nm], out_d[nm], out_m[nm], out_v[nm] = (t.reshape(w.shape) for t in (g, d, m2, v2))
    for nm in small_names:
        w = weights[nm]
        cols = w.shape[-1]
        rows = w.size // cols
        two = lambda t: t.reshape(rows, cols)
        d, m2, v2 = adam_small(two(w), two(g_small[nm]), two(mom_m[nm]), two(mom_v[nm]), name=f"adam_{nm}")
        out_g[nm] = g_small[nm]
        out_d[nm], out_m[nm], out_v[nm] = (t.reshape(w.shape) for t in (d, m2, v2))

    return (loss, grad_x, *[out_g[n] for n in order], *[out_d[n] for n in order], *[out_m[n] for n in order],
            *[out_v[n] for n in order])
```

```python
import functools
import math

import jax
import jax.numpy as jnp
from jax import lax
from jax.experimental import pallas as pl
from jax.experimental.pallas import tpu as pltpu

f32 = jnp.float32
MXU_DTYPE = jnp.bfloat16
N_DEV = 8
RMS_EPS = 1e-6
LN_EPS = 1e-5
L2_EPS = 1e-6
HEAD = 128
DN_CHUNK = 64
SG_CHUNK = 128
SG_GROUPS = 8
CONV_K = 4
ADAM_LR, ADAM_B1, ADAM_B2, ADAM_EPS, ADAM_WD, ADAM_STEP = 0.001, 0.9, 0.999, 1e-08, 0.01, 10
VMEM_LIMIT = 56 * 1024 * 1024
FFN_ROWS_FWD, FFN_ROWS_BWD, FFN_ROWS_DW = 1024, 512, 2048
MIX_ROWS, SG_BWD_ROWS = 512, 256
PROJ_ROWS, TN_ROWS = 1024, 2048
FFN_SLABS = 4
SDS = jax.ShapeDtypeStruct
MESH = pl.DeviceIdType.MESH


def _params(n_grid):
    return pltpu.CompilerParams(dimension_semantics=("arbitrary",) * n_grid, vmem_limit_bytes=VMEM_LIMIT)


def _row_tile(s, want):
    t = min(s, want)
    assert s % t == 0, (s, t)
    return t


def _rms(x, g):
    return x * lax.rsqrt(jnp.mean(x * x, axis=-1, keepdims=True) + RMS_EPS) * g


def _rms_bwd(x, g, dy):
    r = lax.rsqrt(jnp.mean(x * x, axis=-1, keepdims=True) + RMS_EPS)
    t = dy * g
    dx = t * r - x * (jnp.mean(x * t, axis=-1, keepdims=True) * (r * r * r))
    return dx, jnp.sum(dy * (x * r), axis=0, keepdims=True)


def _silu(a):
    return a * jax.nn.sigmoid(a)


def _mm(a, b):
    return lax.dot_general(a, b, (((1,), (0,)), ((), ())), preferred_element_type=f32)


def _mm_nt(a, b):
    return lax.dot_general(a, b, (((1,), (1,)), ((), ())), preferred_element_type=f32)


def _mm_tn(a, b):
    return lax.dot_general(a, b, (((0,), (0,)), ((), ())), preferred_element_type=f32)


def _c(x):
    return x.astype(MXU_DTYPE)


def _split(a):
    hi = a.astype(MXU_DTYPE)
    lo = (a - hi.astype(f32)).astype(MXU_DTYPE)
    return hi, lo


def _dot3(a, b, dims):
    ah, al = _split(a)
    bh, bl = _split(b)
    d = lambda p, q: lax.dot_general(p, q, (dims, ((), ())), preferred_element_type=f32)
    return d(ah, bh) + (d(ah, bl) + d(al, bh))


NN, NT, TN = ((1,), (0,)), ((1,), (1,)), ((0,), (0,))


def _slot(px, py, pc):
    return 4 * px + 2 * py + pc


def all_gather_multi(arrs, name):
    return Comm("gather", arrs).alone(name)


def exchange_slots(arrs, name):
    return Comm("exchange", arrs).alone(name)


class Comm:
    def __init__(self, kind, arrs):
        self.kind, self.arrs, self.n = kind, list(arrs), len(arrs)
        hbm = pl.BlockSpec(memory_space=pltpu.HBM)
        self.in_specs = [hbm] * self.n
        self.out_specs = [hbm] * self.n
        lead = (N_DEV,) if kind == "gather" else ()
        self.out_shape = [SDS(lead + tuple(a.shape), a.dtype) for a in self.arrs]
        self.scratch = [pltpu.SemaphoreType.DMA((self.n, 7)), pltpu.SemaphoreType.DMA((self.n, 7)),
                        pltpu.SemaphoreType.DMA((self.n,))]

    def phase(self, p, ins, outs, sems):
        (self._gather if self.kind == "gather" else self._exchange)(p, ins, outs, sems)

    def _gather(self, p, ins, outs, sems):
        send_sems, recv_sems, local_sems = sems
        x, y, c = lax.axis_index("x"), lax.axis_index("y"), lax.axis_index("c")
        me, sibling = (x, y, c), (x, y, 1 - c)
        chips = [(1 - x, y), (x, 1 - y), (1 - x, 1 - y)]

        def copy(a, k, block, to, src=None):
            dst = outs[a].at[_slot(*block)]
            return pltpu.make_async_remote_copy(
                src_ref=dst if src is None else src, dst_ref=dst, send_sem=send_sems.at[a, k],
                recv_sem=recv_sems.at[a, k], device_id=to, device_id_type=MESH)

        mine = [pltpu.make_async_copy(ins[a], outs[a].at[_slot(*me)], local_sems.at[a]) for a in range(self.n)]
        first = [[copy(a, 0, me, sibling, src=ins[a])] +
                 [copy(a, 1 + j, me, (*chip, c), src=ins[a]) for j, chip in enumerate(chips)] for a in range(self.n)]
        passed = [[copy(a, 4 + j, (*chip, c), sibling) for j, chip in enumerate(chips)] for a in range(self.n)]
        if p == 0:
            for a in range(self.n):
                mine[a].start()
            for a in range(self.n):
                for cp in first[a]:
                    cp.start()
        elif p == 1:
            for a in range(self.n):
                for j, chip in enumerate(chips):
                    copy(a, 1 + j, (*chip, c), me).wait_recv()
                    passed[a][j].start()
        else:
            for a in range(self.n):
                copy(a, 0, sibling, me).wait_recv()
                for j, chip in enumerate(chips):
                    copy(a, 4 + j, (*chip, 1 - c), me).wait_recv()
            for a in range(self.n):
                for cp in first[a] + passed[a]:
                    cp.wait_send()
                mine[a].wait()

    def _exchange(self, p, ins, outs, sems):
        send_sems, recv_sems, local_sems = sems
        x, y, c = lax.axis_index("x"), lax.axis_index("y"), lax.axis_index("c")
        me = _slot(x, y, c)
        peers = [(x ^ (k >> 2), y ^ ((k >> 1) & 1), c ^ (k & 1)) for k in range(1, N_DEV)]

        def copy(a, k):
            peer = peers[k - 1]
            return pltpu.make_async_remote_copy(
                src_ref=ins[a].at[_slot(*peer)], dst_ref=outs[a].at[me], send_sem=send_sems.at[a, k - 1],
                recv_sem=recv_sems.at[a, k - 1], device_id=peer, device_id_type=MESH)

        def landed(a, k):
            peer = peers[k - 1]
            return pltpu.make_async_remote_copy(
                src_ref=ins[a].at[me], dst_ref=outs[a].at[_slot(*peer)], send_sem=send_sems.at[a, k - 1],
                recv_sem=recv_sems.at[a, k - 1], device_id=peer, device_id_type=MESH)

        local = [pltpu.make_async_copy(ins[a].at[me], outs[a].at[me], local_sems.at[a]) for a in range(self.n)]
        order = [6, 7, 2, 3, 4, 5, 1]
        if p == 0:
            for a in range(self.n):
                local[a].start()
            for a in range(self.n):
                for k in order:
                    copy(a, k).start()
        elif p == 2:
            for a in range(self.n):
                for k in order:
                    copy(a, k).wait_send()
                    landed(a, k).wait_recv()
                local[a].wait()

    def alone(self, name):
        n = self.n

        def body(*refs):
            for p in range(3):
                self.phase(p, refs[:n], refs[n:2 * n], refs[2 * n:])

        return pl.pallas_call(body, name=name, out_shape=tuple(self.out_shape), in_specs=self.in_specs,
                              out_specs=tuple(self.out_specs), scratch_shapes=self.scratch)(*self.arrs)


def hosted_call(body, comm, steps, *, name, grid, in_specs, out_specs, out_shape, scratch_shapes, args):
    if comm is None:
        outs = pl.pallas_call(body, name=name, grid=grid, in_specs=in_specs, out_specs=tuple(out_specs),
                              out_shape=tuple(out_shape), scratch_shapes=scratch_shapes,
                              compiler_params=_params(len(grid)))(*args)
        return outs, None
    ni, no, ns, cn = len(in_specs), len(out_specs), len(scratch_shapes), comm.n

    def both(*refs):
        h_in, c_in = refs[:ni], refs[ni:ni + cn]
        h_out, c_out = refs[ni + cn:ni + cn + no], refs[ni + cn + no:ni + 2 * cn + no]
        h_scr, c_scr = refs[ni + 2 * cn + no:ni + 2 * cn + no + ns], refs[ni + 2 * cn + no + ns:]
        when = steps()
        pl.when(when[0])(lambda: comm.phase(0, c_in, c_out, c_scr))
        body(*h_in, *h_out, *h_scr)
        pl.when(when[1])(lambda: comm.phase(1, c_in, c_out, c_scr))
        pl.when(when[2])(lambda: comm.phase(2, c_in, c_out, c_scr))

    outs = pl.pallas_call(
        both, name=name, grid=grid, in_specs=list(in_specs) + comm.in_specs,
        out_specs=tuple(out_specs) + tuple(comm.out_specs), out_shape=tuple(out_shape) + tuple(comm.out_shape),
        scratch_shapes=list(scratch_shapes) + comm.scratch, compiler_params=_params(len(grid)),
    )(*args, *comm.arrs)
    return outs[:no], outs[no:]


def _grid_steps(n_outer, n_inner=1):
    total = n_outer * n_inner

    def steps():
        t = pl.program_id(0) * n_inner + (pl.program_id(1) if n_inner > 1 else 0)
        return t == 0, t == (total * 5) // 8, t == total - 1
    return steps


def widen_slabs(arrs, ns, name):
    per = N_DEV // ns
    _, R, C = arrs[0].shape
    n = len(arrs)

    def body(*refs):
        for a in range(n):
            for k in range(per):
                refs[n + a][:, k * C:(k + 1) * C] = refs[a][k]

    return pl.pallas_call(
        body, name=name, grid=(ns,), in_specs=[pl.BlockSpec((per, R, C), lambda s: (s, 0, 0))] * n,
        out_specs=tuple(pl.BlockSpec((None, R, per * C), lambda s: (s, 0, 0)) for _ in range(n)),
        out_shape=tuple(SDS((ns, R, per * C), a.dtype) for a in arrs), compiler_params=_params(1))(*arrs)


def join_columns(blocks, n_main, name):
    nb, R, c8 = blocks.shape
    rest = nb * c8 - n_main
    tr = _row_tile(R, 256)

    def body(b_ref, main_ref, rest_ref, full):
        for k in range(nb):
            full[:, k * c8:(k + 1) * c8] = b_ref[k]
        main_ref[...] = full[:, :n_main]
        rest_ref[...] = jnp.zeros_like(rest_ref)
        rest_ref[:, :rest] = full[:, n_main:]

    return pl.pallas_call(
        body, name=name, grid=(R // tr,), in_specs=[pl.BlockSpec((nb, tr, c8), lambda i: (0, i, 0))],
        out_specs=(pl.BlockSpec((tr, n_main), lambda i: (i, 0)), pl.BlockSpec((tr, HEAD), lambda i: (i, 0))),
        out_shape=(SDS((R, n_main), blocks.dtype), SDS((R, HEAD), blocks.dtype)),
        scratch_shapes=[pltpu.VMEM((tr, nb * c8), blocks.dtype)], compiler_params=_params(1))(blocks)


def split_columns(main, rest, c8, name):
    R, n_main = main.shape
    nb = N_DEV
    n_rest = nb * c8 - n_main
    tr = _row_tile(R, 256)

    def body(main_ref, rest_ref, b_ref, full):
        full[:, :n_main] = main_ref[...]
        full[:, n_main:] = rest_ref[:, :n_rest]
        for k in range(nb):
            b_ref[k] = full[:, k * c8:(k + 1) * c8]

    return pl.pallas_call(
        body, name=name, grid=(R // tr,),
        in_specs=[pl.BlockSpec((tr, n_main), lambda i: (i, 0)), pl.BlockSpec((tr, HEAD), lambda i: (i, 0))],
        out_specs=pl.BlockSpec((nb, tr, c8), lambda i: (0, i, 0)), out_shape=SDS((nb, R, c8), main.dtype),
        scratch_shapes=[pltpu.VMEM((tr, nb * c8), main.dtype)], compiler_params=_params(1))(main, rest)


def ffn_fwd(x, gpre, gpost, wg, wu, wd, name, comm=None):
    S, D = x.shape
    nj, F8 = wg.shape[0], wg.shape[-1]
    tm = _row_tile(S, FFN_ROWS_FWD)

    def body(x_ref, gpre_ref, gpost_ref, wg_ref, wu_ref, wd_ref, xo_ref, h_ref, p_ref, q_ref, t_ref, y_ref):
        j = pl.program_id(1)

        @pl.when(j == 0)
        def _():
            h_ref[...] = _rms(x_ref[...], gpre_ref[...]).astype(h_ref.dtype)
            y_ref[...] = jnp.zeros_like(y_ref)

        h = h_ref[...]
        a = _mm(h, wg_ref[...])
        b = _mm(h, wu_ref[...])
        s = jax.nn.sigmoid(a)
        q = a * s
        p_ref[...] = (b * (s + q * (1.0 - s))).astype(p_ref.dtype)
        q_ref[...] = q.astype(q_ref.dtype)
        t = (q * b).astype(t_ref.dtype)
        t_ref[...] = t
        y_ref[...] += _mm(t, wd_ref[...])

        @pl.when(j == nj - 1)
        def _():
            xo_ref[...] = x_ref[...] + 0.5 * _rms(y_ref[...], gpost_ref[...])

    row = pl.BlockSpec((tm, D), lambda i, j: (i, 0))
    vec = pl.BlockSpec((1, D), lambda i, j: (0, 0))
    wcol = pl.BlockSpec((None, D, F8), lambda i, j: (j, 0, 0))
    wrow = pl.BlockSpec((None, F8, D), lambda i, j: (j, 0, 0))
    hid = pl.BlockSpec((None, tm, F8), lambda i, j: (j, i, 0))
    return hosted_call(
        body, comm, _grid_steps(S // tm, nj), name=name, grid=(S // tm, nj),
        in_specs=[row, vec, vec, wcol, wcol, wrow],
        out_specs=(row, row, hid, hid, hid, row),
        out_shape=(SDS((S, D), f32), SDS((S, D), MXU_DTYPE), SDS((nj, S, F8), MXU_DTYPE),
                   SDS((nj, S, F8), MXU_DTYPE), SDS((nj, S, F8), MXU_DTYPE), SDS((S, D), f32)),
        scratch_shapes=[], args=(x, gpre, gpost, wg, wu, wd))


def ffn_bwd_dx(dxo, x, y, p, q, gpre, gpost, wg, wu, wd, name, comm=None):
    S, D = x.shape
    NS, F8 = wg.shape[0], wg.shape[-1]
    sps = 2 if NS % 2 == 0 else 1
    nj = NS // sps
    tm = _row_tile(S, FFN_ROWS_BWD)

    def body(dxo_ref, x_ref, y_ref, p_ref, q_ref, gpre_ref, gpost_ref, wg_ref, wu_ref, wd_ref,
             dx_ref, da_ref, db_ref, dy_ref, dgpre_ref, dgpost_ref, dh_ref):
        i, j = pl.program_id(0), pl.program_id(1)

        @pl.when(j == 0)
        def _():
            @pl.when(i == 0)
            def _():
                dgpre_ref[...] = jnp.zeros_like(dgpre_ref)
                dgpost_ref[...] = jnp.zeros_like(dgpost_ref)

            dy, dg = _rms_bwd(y_ref[...], gpost_ref[...], 0.5 * dxo_ref[...])
            dy_ref[...] = dy.astype(dy_ref.dtype)
            dgpost_ref[...] += dg
            dh_ref[...] = jnp.zeros_like(dh_ref)

        dy = dy_ref[...]
        das, dbs = [], []
        for s in range(sps):
            dt = _mm_nt(dy, wd_ref[s])
            das.append((dt * p_ref[s].astype(f32)).astype(da_ref.dtype))
            dbs.append((dt * q_ref[s].astype(f32)).astype(db_ref.dtype))
            da_ref[s] = das[s]
            db_ref[s] = dbs[s]
        upd = None
        for s in range(sps):
            part = _mm_nt(das[s], wg_ref[s]) + _mm_nt(dbs[s], wu_ref[s])
            upd = part if upd is None else upd + part
        dh_ref[...] += upd

        @pl.when(j == nj - 1)
        def _():
            dxx, dg = _rms_bwd(x_ref[...], gpre_ref[...], dh_ref[...])
            dx_ref[...] = dxo_ref[...] + dxx
            dgpre_ref[...] += dg

    row = pl.BlockSpec((tm, D), lambda i, j: (i, 0))
    vec = pl.BlockSpec((1, D), lambda i, j: (0, 0))
    wcol = pl.BlockSpec((sps, D, F8), lambda i, j: (j, 0, 0))
    wrow = pl.BlockSpec((sps, F8, D), lambda i, j: (j, 0, 0))
    hid = pl.BlockSpec((sps, tm, F8), lambda i, j: (j, i, 0))
    return hosted_call(
        body, comm, _grid_steps(S // tm, nj), name=name, grid=(S // tm, nj),
        in_specs=[row, row, row, hid, hid, vec, vec, wcol, wcol, wrow],
        out_specs=(row, hid, hid, row, vec, vec),
        out_shape=(SDS((S, D), f32), SDS((NS, S, F8), MXU_DTYPE), SDS((NS, S, F8), MXU_DTYPE),
                   SDS((S, D), MXU_DTYPE), SDS((1, D), f32), SDS((1, D), f32)),
        scratch_shapes=[pltpu.VMEM((tm, D), f32)], args=(dxo, x, y, p, q, gpre, gpost, wg, wu, wd))


def ffn_bwd_dw(h, dy, t, da, db, name, comm=None):
    S, D = h.shape
    NS, F8 = t.shape[0], t.shape[-1]
    per = N_DEV // NS
    w8 = F8 // per
    tm = _row_tile(S, FFN_ROWS_DW)
    ni = S // tm

    def body(h_ref, dy_ref, t_ref, da_ref, db_ref, dwg_ref, dwu_ref, dwd_ref, accg, accu, accd):
        i = pl.program_id(1)

        @pl.when(i == 0)
        def _():
            accg[...] = jnp.zeros_like(accg)
            accu[...] = jnp.zeros_like(accu)
            accd[...] = jnp.zeros_like(accd)

        hh = h_ref[...]
        accg[...] += _mm_tn(hh, da_ref[...])
        accu[...] += _mm_tn(hh, db_ref[...])
        accd[...] += _mm_tn(t_ref[...], dy_ref[...])

        @pl.when(i == ni - 1)
        def _():
            for k in range(per):
                ks = slice(k * w8, (k + 1) * w8)
                dwg_ref[k] = accg[:, ks].astype(dwg_ref.dtype)
                dwu_ref[k] = accu[:, ks].astype(dwu_ref.dtype)
                dwd_ref[k] = accd[ks, :].astype(dwd_ref.dtype)

    row = pl.BlockSpec((tm, D), lambda j, i: (i, 0))
    hid = pl.BlockSpec((None, tm, F8), lambda j, i: (j, i, 0))
    wcol = pl.BlockSpec((per, D, w8), lambda j, i: (j, 0, 0))
    wrow = pl.BlockSpec((per, w8, D), lambda j, i: (j, 0, 0))
    return hosted_call(
        body, comm, _grid_steps(NS, ni), name=name, grid=(NS, ni),
        in_specs=[row, row, hid, hid, hid],
        out_specs=(wcol, wcol, wrow),
        out_shape=(SDS((N_DEV, D, w8), MXU_DTYPE), SDS((N_DEV, D, w8), MXU_DTYPE), SDS((N_DEV, w8, D), MXU_DTYPE)),
        scratch_shapes=[pltpu.VMEM((D, F8), f32), pltpu.VMEM((D, F8), f32), pltpu.VMEM((F8, D), f32)],
        args=(h, dy, t, da, db))


def ffn_bwd_dw_one(rows_op, slab_op, hidden_rows, name, comm=None):
    S, D = rows_op.shape
    NS, F8 = slab_op.shape[0], slab_op.shape[-1]
    per = N_DEV // NS
    w8 = F8 // per
    tm = _row_tile(S, FFN_ROWS_DW)
    ni = S // tm

    def body(r_ref, s_ref, o_ref, acc):
        i = pl.program_id(1)

        @pl.when(i == 0)
        def _():
            acc[...] = jnp.zeros_like(acc)

        acc[...] += _mm_tn(s_ref[...], r_ref[...]) if hidden_rows else _mm_tn(r_ref[...], s_ref[...])

        @pl.when(i == ni - 1)
        def _():
            for k in range(per):
                ks = slice(k * w8, (k + 1) * w8)
                o_ref[k] = (acc[ks, :] if hidden_rows else acc[:, ks]).astype(o_ref.dtype)

    blk = (per, w8, D) if hidden_rows else (per, D, w8)
    return hosted_call(
        body, comm, _grid_steps(NS, ni), name=name, grid=(NS, ni),
        in_specs=[pl.BlockSpec((tm, D), lambda j, i: (i, 0)), pl.BlockSpec((None, tm, F8), lambda j, i: (j, i, 0))],
        out_specs=(pl.BlockSpec(blk, lambda j, i: (j, 0, 0)),),
        out_shape=(SDS((N_DEV,) + blk[1:], MXU_DTYPE),),
        scratch_shapes=[pltpu.VMEM((F8, D) if hidden_rows else (D, F8), f32)], args=(rows_op, slab_op))


def rms_mm(x, g, w, w2, name, tn=1024):
    S, D = x.shape
    N = w.shape[1]
    tm = _row_tile(S, PROJ_ROWS)
    tn = _row_tile(N, tn)
    has2 = w2 is not None

    def body(*refs):
        if has2:
            x_ref, g_ref, w_ref, w2_ref, h_ref, o_ref, o2_ref = refs
        else:
            x_ref, g_ref, w_ref, h_ref, o_ref = refs
        j = pl.program_id(1)

        @pl.when(j == 0)
        def _():
            h = _rms(x_ref[...], g_ref[...]).astype(h_ref.dtype)
            h_ref[...] = h
            if has2:
                o2_ref[...] = _mm(h, w2_ref[...])

        o_ref[...] = _mm(h_ref[...], w_ref[...])

    row = pl.BlockSpec((tm, D), lambda i, j: (i, 0))
    in_specs = [row, pl.BlockSpec((1, D), lambda i, j: (0, 0)), pl.BlockSpec((D, tn), lambda i, j: (0, j))]
    out_specs = [row, pl.BlockSpec((tm, tn), lambda i, j: (i, j))]
    out_shape = [SDS((S, D), MXU_DTYPE), SDS((S, N), f32)]
    args = [x, g, w]
    if has2:
        in_specs.append(pl.BlockSpec((D, w2.shape[1]), lambda i, j: (0, 0)))
        out_specs.append(pl.BlockSpec((tm, w2.shape[1]), lambda i, j: (i, 0)))
        out_shape.append(SDS((S, w2.shape[1]), f32))
        args.append(w2)
    return pl.pallas_call(
        body, name=name, grid=(S // tm, N // tn), in_specs=in_specs, out_specs=tuple(out_specs),
        out_shape=tuple(out_shape), compiler_params=_params(2),
    )(*args)


def mm_bwd_dx(dres, x, g, dy, w, dy2, w2, name, tk=1024, comm=None):
    S, D = x.shape
    K = dy.shape[1]
    tm = _row_tile(S, PROJ_ROWS)
    tk = _row_tile(K, tk)
    nk = K // tk
    has2 = dy2 is not None

    def body(*refs):
        if has2:
            dres_ref, x_ref, g_ref, dy_ref, w_ref, dy2_ref, w2_ref, dx_ref, dg_ref, dh_ref = refs
        else:
            dres_ref, x_ref, g_ref, dy_ref, w_ref, dx_ref, dg_ref, dh_ref = refs
        i, k = pl.program_id(0), pl.program_id(1)

        @pl.when(k == 0)
        def _():
            @pl.when(i == 0)
            def _():
                dg_ref[...] = jnp.zeros_like(dg_ref)

            if has2:
                dh_ref[...] = _mm_nt(dy2_ref[...], w2_ref[...])
            else:
                dh_ref[...] = jnp.zeros_like(dh_ref)

        dh_ref[...] += _mm_nt(dy_ref[...], w_ref[...])

        @pl.when(k == nk - 1)
        def _():
            dxx, dg = _rms_bwd(x_ref[...], g_ref[...], dh_ref[...])
            dx_ref[...] = dres_ref[...] + dxx
            dg_ref[...] += dg

    row = pl.BlockSpec((tm, D), lambda i, k: (i, 0))
    vec = pl.BlockSpec((1, D), lambda i, k: (0, 0))
    in_specs = [row, row, vec, pl.BlockSpec((tm, tk), lambda i, k: (i, k)), pl.BlockSpec((D, tk), lambda i, k: (0, k))]
    args = [dres, x, g, dy, w]
    if has2:
        in_specs += [pl.BlockSpec((tm, dy2.shape[1]), lambda i, k: (i, 0)),
                     pl.BlockSpec((D, w2.shape[1]), lambda i, k: (0, 0))]
        args += [dy2, w2]
    return hosted_call(
        body, comm, _grid_steps(S // tm, nk), name=name, grid=(S // tm, nk), in_specs=in_specs, out_specs=(row, vec),
        out_shape=(SDS((S, D), f32), SDS((1, D), f32)), scratch_shapes=[pltpu.VMEM((tm, D), f32)], args=args)


def tn_mm(a, b, name, tn=512, slot_major=False):
    S, K1 = a.shape
    N = b.shape[1]
    tm = _row_tile(S, TN_ROWS)
    tn = _row_tile(N, tn)
    ni = S // tm

    def body(a_ref, b_ref, o_ref, acc):
        i = pl.program_id(1)

        @pl.when(i == 0)
        def _():
            acc[...] = jnp.zeros_like(acc)

        acc[...] += _mm_tn(a_ref[...], b_ref[...])

        @pl.when(i == ni - 1)
        def _():
            o_ref[...] = acc[...].astype(o_ref.dtype)

    if slot_major:
        out_spec, out_shape = pl.BlockSpec((None, K1, tn), lambda j, i: (j, 0, 0)), SDS((N // tn, K1, tn), MXU_DTYPE)
    else:
        out_spec, out_shape = pl.BlockSpec((K1, tn), lambda j, i: (0, j)), SDS((K1, N), MXU_DTYPE)
    return pl.pallas_call(
        body, name=name, grid=(N // tn, ni),
        in_specs=[pl.BlockSpec((tm, K1), lambda j, i: (i, 0)), pl.BlockSpec((tm, tn), lambda j, i: (i, j))],
        out_specs=out_spec, out_shape=out_shape,
        scratch_shapes=[pltpu.VMEM((K1, tn), f32)], compiler_params=_params(2),
    )(a, b)


CONV_ROWS = 512


def _shift_down(cur, prev8, s):
    r = pltpu.roll(cur, s, 0)
    row = lax.broadcasted_iota(jnp.int32, (8, cur.shape[1]), 0)
    top = jnp.where(row < s, pltpu.roll(prev8, s, 0), r[0:8])
    return jnp.concatenate([top, r[8:]], axis=0)


def _shift_up(cur, next8, s):
    n = cur.shape[0]
    r = pltpu.roll(cur, n - s, 0)
    row = lax.broadcasted_iota(jnp.int32, (8, cur.shape[1]), 0)
    bot = jnp.where(row >= 8 - s, pltpu.roll(next8, 8 - s, 0), r[n - 8:])
    return jnp.concatenate([r[:n - 8], bot], axis=0)


def _conv_taps(cur, prev8):
    return [_shift_down(cur, prev8, 3), _shift_down(cur, prev8, 2), _shift_down(cur, prev8, 1), cur]


def _act_qk(c):
    a = _silu(c)
    return a * lax.rsqrt(jnp.sum(a * a, axis=-1, keepdims=True) + L2_EPS)


def dn_prep(proj, conv_w, name):
    S = proj.shape[0]
    W = conv_w.shape[1] // 3
    nh = W // HEAD
    R = _row_tile(S, CONV_ROWS)

    def body(p_ref, w_ref, o_ref):
        j = pl.program_id(0)
        w = w_ref[...]

        def rows(r, prev8):
            cur = p_ref[pl.ds(r, R), :]
            taps = _conv_taps(cur, prev8)
            cv = taps[0] * w[0:1] + taps[1] * w[1:2] + taps[2] * w[2:3] + taps[3] * w[3:4]

            @pl.when(j < 2 * nh)
            def _():
                o_ref[pl.ds(r, R), :] = _act_qk(cv)

            @pl.when(j >= 2 * nh)
            def _():
                o_ref[pl.ds(r, R), :] = _silu(cv)

        rows(0, jnp.zeros((8, HEAD), f32))

        @pl.loop(1, S // R)
        def _(t):
            r = pl.multiple_of(t * R, R)
            rows(r, p_ref[pl.ds(r - 8, 8), :])

    return pl.pallas_call(
        body, name=name, grid=(3 * nh,),
        in_specs=[pl.BlockSpec((S, HEAD), lambda j: (0, j)), pl.BlockSpec((CONV_K, HEAD), lambda j: (0, j))],
        out_specs=pl.BlockSpec((None, S, HEAD), lambda j: (j // nh, 0, j % nh)),
        out_shape=SDS((3, S, W), f32), compiler_params=_params(1),
    )(proj, conv_w)


def dn_prep_bwd(proj, conv_w, dqkv, dz, name, comm=None):
    S = proj.shape[0]
    W = conv_w.shape[1] // 3
    nh = W // HEAD
    nq = 3 * nh
    R = _row_tile(S, CONV_ROWS)
    nr = S // R

    def body(p_ref, w_ref, dq_ref, dz_ref, dp_ref, dw_ref, dc_ref):
        j = pl.program_id(0)

        @pl.when(j >= nq)
        def _():
            dp_ref[...] = dz_ref[...].astype(dp_ref.dtype)

        @pl.when(j < nq)
        def _():
            w = w_ref[...]
            dw_ref[...] = jnp.zeros_like(dw_ref)

            def rows(r, prev8):
                cur = p_ref[pl.ds(r, R), :]
                taps = _conv_taps(cur, prev8)
                cv = taps[0] * w[0:1] + taps[1] * w[1:2] + taps[2] * w[2:3] + taps[3] * w[3:4]
                dn = dq_ref[pl.ds(r, R), :]

                @pl.when(j < 2 * nh)
                def _():
                    dc_ref[pl.ds(r, R), :] = jax.vjp(_act_qk, cv)[1](dn)[0]

                @pl.when(j >= 2 * nh)
                def _():
                    dc_ref[pl.ds(r, R), :] = jax.vjp(_silu, cv)[1](dn)[0]

                dc = dc_ref[pl.ds(r, R), :]
                dw_ref[...] += jnp.concatenate(
                    [jnp.sum(dc * taps[q], axis=0, keepdims=True) for q in range(CONV_K)], axis=0)

            rows(0, jnp.zeros((8, HEAD), f32))

            @pl.loop(1, nr)
            def _(t):
                r = pl.multiple_of(t * R, R)
                rows(r, p_ref[pl.ds(r - 8, 8), :])

            def back(r, next8):
                dc = dc_ref[pl.ds(r, R), :]
                dx = dc * w[3:4]
                for s in (1, 2, 3):
                    dx = dx + _shift_up(dc, next8, s) * w[3 - s:4 - s]
                dp_ref[pl.ds(r, R), :] = dx.astype(dp_ref.dtype)

            @pl.loop(0, nr - 1)
            def _(t):
                r = pl.multiple_of(t * R, R)
                back(r, dc_ref[pl.ds(r + R, 8), :])

            back((nr - 1) * R, jnp.zeros((8, HEAD), f32))

    clamp = lambda j: jnp.minimum(j, nq - 1)
    return hosted_call(
        body, comm, _grid_steps(4 * nh), name=name, grid=(4 * nh,),
        in_specs=[pl.BlockSpec((S, HEAD), lambda j: (0, clamp(j))),
                  pl.BlockSpec((CONV_K, HEAD), lambda j: (0, clamp(j))),
                  pl.BlockSpec((None, S, HEAD), lambda j: (clamp(j) // nh, 0, clamp(j) % nh)),
                  pl.BlockSpec((S, HEAD), lambda j: (0, jnp.maximum(j - nq, 0)))],
        out_specs=(pl.BlockSpec((S, HEAD), lambda j: (0, j)), pl.BlockSpec((CONV_K, HEAD), lambda j: (0, clamp(j)))),
        out_shape=(SDS((S, 4 * W), MXU_DTYPE), SDS((CONV_K, 3 * W), f32)),
        scratch_shapes=[pltpu.VMEM((S, HEAD), f32)], args=(proj, conv_w, dqkv, dz))


def _lane_pick(x, lane):
    sel = lax.broadcasted_iota(jnp.int32, x.shape, 1) == lane
    return jnp.broadcast_to(jnp.sum(jnp.where(sel, x, 0.0), axis=1, keepdims=True), x.shape)


CUM_ROWS = 256


def _sel_mm(m01, x):
    m = _c(m01)
    d = lambda p: lax.dot_general(m, p, (NN, ((), ())), preferred_element_type=f32)
    h1, h2, h3 = _pieces3(x)
    return (d(h1) + d(h2)) + d(h3)


def _chunk_cumsum_matrix(n, transpose):
    r, c = lax.broadcasted_iota(jnp.int32, (n, n), 0), lax.broadcasted_iota(jnp.int32, (n, n), 1)
    sh = int(math.log2(DN_CHUNK))
    same = (r >> sh) == (c >> sh)
    return jnp.where(same & ((r <= c) if transpose else (r >= c)), 1.0, 0.0).astype(f32)


def _gates_by_lane(H, p, al, dt):
    lane = lax.broadcasted_iota(jnp.int32, p.shape, 1)
    g = -jnp.exp(al) * jax.nn.softplus(p + dt)
    return jnp.where(lane < H, jax.nn.sigmoid(p), jnp.where(lane < 2 * H, g, 0.0))


def dn_gates(pba, al, dt, H, name):
    S = pba.shape[0]
    R = _row_tile(S, CUM_ROWS)

    def body(p_ref, al_ref, dt_ref, o_ref):
        raw = _gates_by_lane(H, p_ref[...], al_ref[...], dt_ref[...])
        lane = lax.broadcasted_iota(jnp.int32, raw.shape, 1)
        o_ref[...] = jnp.where(lane < H, raw, _sel_mm(_chunk_cumsum_matrix(R, False), raw))

    blk = pl.BlockSpec((R, HEAD), lambda i: (i, 0))
    par = pl.BlockSpec((1, HEAD), lambda i: (0, 0))
    return pl.pallas_call(body, name=name, grid=(S // R,), in_specs=[blk, par, par], out_specs=blk,
                          out_shape=SDS((S, HEAD), f32), compiler_params=_params(1))(pba, al, dt)


def dn_gates_bwd(pba, al, dt, dgates, H, name):
    S = pba.shape[0]
    R = _row_tile(S, CUM_ROWS)

    def body(p_ref, al_ref, dt_ref, dg_ref, dp_ref, dal_ref, ddt_ref):
        @pl.when(pl.program_id(0) == 0)
        def _():
            dal_ref[...] = jnp.zeros_like(dal_ref)
            ddt_ref[...] = jnp.zeros_like(ddt_ref)

        d = dg_ref[...]
        lane = lax.broadcasted_iota(jnp.int32, d.shape, 1)
        d = jnp.where(lane < H, d, _sel_mm(_chunk_cumsum_matrix(R, True), d))
        _, vjp = jax.vjp(functools.partial(_gates_by_lane, H), p_ref[...], al_ref[...], dt_ref[...])
        dp, dal, ddt = vjp(d)
        dp_ref[...] = dp.astype(dp_ref.dtype)
        dal_ref[...] += dal
        ddt_ref[...] += ddt

    blk = pl.BlockSpec((R, HEAD), lambda i: (i, 0))
    par = pl.BlockSpec((1, HEAD), lambda i: (0, 0))
    return pl.pallas_call(
        body, name=name, grid=(S // R,), in_specs=[blk, par, par, blk], out_specs=(blk, par, par),
        out_shape=(SDS((S, HEAD), MXU_DTYPE), SDS((1, HEAD), f32), SDS((1, HEAD), f32)), compiler_params=_params(1),
    )(pba, al, dt, dgates)


def _bdot(dims):
    back = {NN: ((NT, 'gb'), (TN, 'ag')), NT: ((NN, 'gb'), (TN, 'ga')), TN: ((NT, 'bg'), (NN, 'ag'))}[dims]
    d = lambda p, q, dm: lax.dot_general(_c(p), _c(q), (dm, ((), ())), preferred_element_type=f32)

    @jax.custom_vjp
    def f(a, b):
        return d(a, b, dims)

    def fwd(a, b):
        return d(a, b, dims), (a, b)

    def bwd(res, g):
        v = {'a': res[0], 'b': res[1], 'g': g}
        (da_dims, da_ops), (db_dims, db_ops) = back
        return d(v[da_ops[0]], v[da_ops[1]], da_dims), d(v[db_ops[0]], v[db_ops[1]], db_dims)

    f.defvjp(fwd, bwd)
    return f, lambda a, b: d(a, b, dims)


_BDOT = {dims: _bdot(dims) for dims in (NN, NT, TN)}


def _tri_inv_multi(Ls):
    n = Ls[0].shape[0]
    eye = jnp.where(lax.broadcasted_iota(jnp.int32, (n, n), 0) == lax.broadcasted_iota(jnp.int32, (n, n), 1), 1.0, 0.0)
    P = tuple(-L for L in Ls)
    T = tuple(eye + p for p in P)
    P = tuple(_dot3(p, p, NN) for p in P)
    levels = int(math.log2(n)) - 1
    for lvl in range(levels):
        if lvl == levels - 1:
            T = tuple(t + _dot3(t, p, NN) for t, p in zip(T, P))
        else:
            both = tuple(_dot3(jnp.concatenate([t, p], axis=0), p, NN) for t, p in zip(T, P))
            T = tuple(t + b[:n] for t, b in zip(T, both))
            P = tuple(b[n:] for b in both)
    return T


@jax.custom_vjp
def _tri_inv_kept(Ls, Ts):
    return Ts


def _tri_inv_kept_bwd(T, dT):
    X = tuple(_dot3(d, t, NT) for d, t in zip(dT, T))
    return tuple(-_dot3(t, x, TN) for t, x in zip(T, X)), tuple(jnp.zeros_like(t) for t in T)


_tri_inv_kept.defvjp(lambda Ls, Ts: (Ts, Ts), _tri_inv_kept_bwd)


def _pieces3(x):
    h1 = x.astype(MXU_DTYPE)
    r1 = x - h1.astype(f32)
    h2 = r1.astype(MXU_DTYPE)
    return h1, h2, (r1 - h2.astype(f32)).astype(MXU_DTYPE)


def _row_bcast_impl(sel_row, gc):
    s = _c(sel_row)
    d = lambda p: lax.dot_general(s, p, (NT, ((), ())), preferred_element_type=f32)
    h1, h2, h3 = _pieces3(gc)
    return (d(h1) + d(h2)) + d(h3)


def _row_bcast_bwd(sel_row, d):
    s = _c(sel_row)
    hi, lo = _split(d)
    t = lambda p: lax.dot_general(p, s, (TN, ((), ())), preferred_element_type=f32)
    return jnp.zeros_like(sel_row), t(hi) + t(lo)


_row_bcast = jax.custom_vjp(_row_bcast_impl)
_row_bcast.defvjp(lambda sel_row, gc: (_row_bcast_impl(sel_row, gc), sel_row), _row_bcast_bwd)


def _col_bcast_impl(gc):
    return gc[:, :DN_CHUNK]


def _col_bcast_bwd(_, d):
    return (jnp.broadcast_to(jnp.sum(d, axis=1, keepdims=True) * (1.0 / HEAD), (d.shape[0], HEAD)),)


_col_bcast = jax.custom_vjp(_col_bcast_impl)
_col_bcast.defvjp(lambda gc: (_col_bcast_impl(gc), None), _col_bcast_bwd)


def _last_row_bcast(n):
    def impl(gc):
        return jnp.broadcast_to(gc[DN_CHUNK - 1:DN_CHUNK, :], (n, HEAD))

    def bwd(_, d):
        row = lax.broadcasted_iota(jnp.int32, (DN_CHUNK, HEAD), 0)
        return (jnp.where(row == DN_CHUNK - 1, jnp.sum(d, axis=0, keepdims=True), 0.0),)

    f = jax.custom_vjp(impl)
    f.defvjp(lambda gc: (impl(gc), None), bwd)
    return impl, f


_LAST_C, _LAST_H = _last_row_bcast(DN_CHUNK), _last_row_bcast(HEAD)


def _halves(axis):
    def impl(x):
        n = x.shape[axis] // 2
        return lax.slice_in_dim(x, 0, n, axis=axis), lax.slice_in_dim(x, n, 2 * n, axis=axis)

    f = jax.custom_vjp(impl)
    f.defvjp(lambda x: (impl(x), None), lambda _, g: (jnp.concatenate(g, axis=axis),))
    return impl, f


_ROW_HALVES, _COL_HALVES = _halves(0), _halves(1)


def _chunk_consts():
    C = DN_CHUNK
    io = lambda shape, ax: lax.broadcasted_iota(jnp.int32, shape, ax)
    one = lambda m: jnp.where(m, 1.0, 0.0).astype(f32)
    r, c = io((C, C), 0), io((C, C), 1)
    return dict(causal=r >= c, strict=r > c, sel_row=one(io((C, HEAD), 1) == 0))


def _chunk_fn(kc, kept_T, q, k, v, gc, bB, S0):
    diff = kept_T is not None
    i = 0 if diff else 1
    mm, mm_nt, mm_tn = _BDOT[NN][i], _BDOT[NT][i], _BDOT[TN][i]
    tri = (lambda Ls: _tri_inv_kept(Ls, kept_T)) if diff else _tri_inv_multi
    each = lambda f, *ls: tuple(f(*a) for a in zip(*ls))
    gcol = each(_col_bcast if diff else _col_bcast_impl, gc)
    grow = each(lambda g: (_row_bcast if diff else _row_bcast_impl)(kc['sel_row'], g), gc)
    glc = each(_LAST_C[i ^ 1], gc)
    glh = each(_LAST_H[i ^ 1], gc)
    decay = each(lambda a, b: jnp.where(kc['causal'], jnp.exp(jnp.where(kc['causal'], a - b, 0.0)), 0.0), gcol, grow)
    rows, cols = _ROW_HALVES[i ^ 1], _COL_HALVES[i ^ 1]
    first, second = (lambda ts: tuple(t[0] for t in ts)), (lambda ts: tuple(t[1] for t in ts))
    kb = each(lambda a, b: a * b, k, bB)
    vb = each(lambda a, b: a * b, v, bB)
    egc = each(jnp.exp, gc)
    qs = each(lambda a: a * (HEAD ** -0.5), q)
    kq = each(lambda a, b, kt: rows(mm_nt(jnp.concatenate([a, b], axis=0), kt)), kb, qs, k)
    kk, qk = first(kq), second(kq)
    T = tri(each(lambda a, d: jnp.where(kc['strict'], a * d, 0.0), kk, decay))
    uw = each(lambda t, a, b, e: cols(mm(t, jnp.concatenate([a, b * e], axis=1))), T, vb, kb, egc)
    u, w = first(uw), second(uw)
    attn = each(lambda a, d: jnp.where(kc['causal'], a * d, 0.0), qk, decay)
    wq = each(lambda a, b, e, s: rows(mm(jnp.concatenate([a, b * e], axis=0), s)), w, qs, egc, S0)
    wS, qS = first(wq), second(wq)
    v_new = each(lambda a, b: a - b, u, wS)
    o = each(lambda a, b: a + b, qS, each(mm, attn, v_new))
    kdec = each(lambda a, gl, g: a * jnp.exp(gl - g), k, glc, gc)
    S1 = each(lambda s, gl, kv: s * jnp.exp(gl) + kv, S0, glh, each(mm_tn, kdec, v_new))
    return (o, S1) if diff else (o, S1, T)


def _chunks_per_step(N):
    return 4 if N % 4 == 0 else (2 if N % 2 == 0 else 1)


def _heads_per_block(H):
    return 8 if H % 8 == 0 else (4 if H % 4 == 0 else 1)


def dn_chunk_fwd(qkv, gates, name, comm=None):
    _, S, W = qkv.shape
    H, C = W // HEAD, DN_CHUNK
    N, HB = S // C, _heads_per_block(H)
    assert HB == H
    CPS = _chunks_per_step(N)

    def body(q_ref, k_ref, v_ref, g_ref, o_ref, st_ref, t_ref, s_scr):
        @pl.when(pl.program_id(1) == 0)
        def _():
            s_scr[...] = jnp.zeros_like(s_scr)

        kc = _chunk_consts()
        sls = [slice(hh * HEAD, (hh + 1) * HEAD) for hh in range(HB)]
        St = tuple(s_scr[hh] for hh in range(HB))
        for c in range(CPS):
            rows = slice(c * C, (c + 1) * C)
            heads = lambda ref: tuple(ref[rows, sl] for sl in sls)
            gr = g_ref[rows, :]
            for hh in range(HB):
                st_ref[c, hh] = St[hh]
            o, St, T = _chunk_fn(kc, None, heads(q_ref), heads(k_ref), heads(v_ref),
                                 tuple(_lane_pick(gr, H + hh) for hh in range(HB)),
                                 tuple(_lane_pick(gr, hh) for hh in range(HB)), St)
            for hh in range(HB):
                o_ref[rows, sls[hh]] = o[hh]
                t_ref[c, hh] = T[hh]
        for hh in range(HB):
            s_scr[hh] = St[hh]

    part = lambda p: pl.BlockSpec((None, CPS * C, HB * HEAD), lambda hb, n: (p, n, hb))
    return hosted_call(
        body, comm, _grid_steps(H // HB, N // CPS), name=name, grid=(H // HB, N // CPS),
        in_specs=[part(0), part(1), part(2), pl.BlockSpec((CPS * C, HEAD), lambda hb, n: (n, 0))],
        out_specs=(pl.BlockSpec((CPS * C, HB * HEAD), lambda hb, n: (n, hb)),
                   pl.BlockSpec((CPS, HB, HEAD, HEAD), lambda hb, n: (n, hb, 0, 0)),
                   pl.BlockSpec((CPS, HB, C, C), lambda hb, n: (n, hb, 0, 0))),
        out_shape=(SDS((S, W), f32), SDS((N, H, HEAD, HEAD), f32), SDS((N, H, C, C), f32)),
        scratch_shapes=[pltpu.VMEM((HB, HEAD, HEAD), f32)], args=(qkv, qkv, qkv, gates))


def dn_chunk_bwd(qkv, gates, states, kept_T, do, name, comm=None):
    _, S, W = qkv.shape
    H, C = W // HEAD, DN_CHUNK
    N, HB = S // C, _heads_per_block(H)
    assert HB == H
    CPS = _chunks_per_step(N)
    NB = N // CPS

    def body(q_ref, k_ref, v_ref, g_ref, st_ref, t_ref, do_ref, dqkv_ref, dg_ref, ds_scr):
        @pl.when(pl.program_id(1) == 0)
        def _():
            ds_scr[...] = jnp.zeros_like(ds_scr)

        kc = _chunk_consts()
        sls = [slice(hh * HEAD, (hh + 1) * HEAD) for hh in range(HB)]
        dSt = tuple(ds_scr[hh] for hh in range(HB))
        for c in reversed(range(CPS)):
            rows = slice(c * C, (c + 1) * C)
            heads = lambda ref: tuple(ref[rows, sl] for sl in sls)
            gr = g_ref[rows, :]
            kept = tuple(t_ref[c, hh] for hh in range(HB))
            _, vjp = jax.vjp(functools.partial(_chunk_fn, kc, kept), heads(q_ref), heads(k_ref), heads(v_ref),
                             tuple(_lane_pick(gr, H + hh) for hh in range(HB)),
                             tuple(_lane_pick(gr, hh) for hh in range(HB)), tuple(st_ref[c, hh] for hh in range(HB)))
            dq, dk, dv, dg, db, dSt = vjp((heads(do_ref), dSt))
            lane = lax.broadcasted_iota(jnp.int32, (C, HEAD), 1)
            dgr = jnp.zeros((C, HEAD), f32)
            for hh in range(HB):
                dqkv_ref[0, rows, sls[hh]] = dq[hh]
                dqkv_ref[1, rows, sls[hh]] = dk[hh]
                dqkv_ref[2, rows, sls[hh]] = dv[hh]
                dgr = dgr + jnp.where(lane == hh, jnp.sum(db[hh], axis=1, keepdims=True), 0.0)
                dgr = dgr + jnp.where(lane == H + hh, jnp.sum(dg[hh], axis=1, keepdims=True), 0.0)
            dg_ref[rows, :] = dgr
        for hh in range(HB):
            ds_scr[hh] = dSt[hh]

    rev = lambda n: NB - 1 - n
    part = lambda p: pl.BlockSpec((None, CPS * C, HB * HEAD), lambda hb, n: (p, rev(n), hb))
    gate = pl.BlockSpec((CPS * C, HEAD), lambda hb, n: (rev(n), 0))
    return hosted_call(
        body, comm, _grid_steps(H // HB, NB), name=name, grid=(H // HB, NB),
        in_specs=[part(0), part(1), part(2), gate,
                  pl.BlockSpec((CPS, HB, HEAD, HEAD), lambda hb, n: (rev(n), hb, 0, 0)),
                  pl.BlockSpec((CPS, HB, C, C), lambda hb, n: (rev(n), hb, 0, 0)),
                  pl.BlockSpec((CPS * C, HB * HEAD), lambda hb, n: (rev(n), hb))],
        out_specs=(pl.BlockSpec((3, CPS * C, HB * HEAD), lambda hb, n: (0, rev(n), hb)), gate),
        out_shape=(SDS((3, S, W), f32), SDS((S, HEAD), f32)),
        scratch_shapes=[pltpu.VMEM((HB, HEAD, HEAD), f32)], args=(qkv, qkv, qkv, gates, states, kept_T, do))


def _gate_norm(o, z, ng):
    return _rms(o, ng) * _silu(z)


def dn_out(o, proj, ng, wout, x1, g3, name):
    S, W = o.shape
    D = x1.shape[1]
    nh = W // HEAD
    tm = _row_tile(S, MIX_ROWS)

    def body(o_ref, z_ref, ng_ref, w_ref, x_ref, g_ref, xo_ref, m_ref, og_ref):
        for h in range(nh):
            sl = slice(h * HEAD, (h + 1) * HEAD)
            og_ref[:, sl] = _gate_norm(o_ref[:, sl], z_ref[:, sl], ng_ref[...]).astype(og_ref.dtype)
        m = _mm(og_ref[...], w_ref[...])
        m_ref[...] = m
        xo_ref[...] = x_ref[...] + _rms(m, g_ref[...])

    rw = pl.BlockSpec((tm, W), lambda i: (i, 0))
    rd = pl.BlockSpec((tm, D), lambda i: (i, 0))
    return pl.pallas_call(
        body, name=name, grid=(S // tm,),
        in_specs=[rw, pl.BlockSpec((tm, W), lambda i: (i, 3)), pl.BlockSpec((1, HEAD), lambda i: (0, 0)),
                  pl.BlockSpec((W, D), lambda i: (0, 0)), rd, pl.BlockSpec((1, D), lambda i: (0, 0))],
        out_specs=(rd, rd, rw),
        out_shape=(SDS((S, D), f32), SDS((S, D), f32), SDS((S, W), MXU_DTYPE)), compiler_params=_params(1),
    )(o, proj, ng, wout, x1, g3)


def dn_out_bwd(dxo, m, g3, o, proj, ng, wout, name):
    S, W = o.shape
    D = m.shape[1]
    nh = W // HEAD
    tm = _row_tile(S, MIX_ROWS)

    def body(dxo_ref, m_ref, g_ref, o_ref, z_ref, ng_ref, w_ref, dm_ref, do_ref, dz_ref, dng_ref, dg_ref):
        @pl.when(pl.program_id(0) == 0)
        def _():
            dng_ref[...] = jnp.zeros_like(dng_ref)
            dg_ref[...] = jnp.zeros_like(dg_ref)

        dm, dg = _rms_bwd(m_ref[...], g_ref[...], dxo_ref[...])
        dg_ref[...] += dg
        dmc = dm.astype(dm_ref.dtype)
        dm_ref[...] = dmc
        dog = _mm_nt(dmc, w_ref[...])
        for h in range(nh):
            sl = slice(h * HEAD, (h + 1) * HEAD)
            _, vjp = jax.vjp(_gate_norm, o_ref[:, sl], z_ref[:, sl], ng_ref[...])
            do, dz, dng = vjp(dog[:, sl])
            do_ref[:, sl] = do
            dz_ref[:, sl] = dz.astype(dz_ref.dtype)
            dng_ref[...] += dng

    rw = pl.BlockSpec((tm, W), lambda i: (i, 0))
    rd = pl.BlockSpec((tm, D), lambda i: (i, 0))
    vd = pl.BlockSpec((1, D), lambda i: (0, 0))
    vh = pl.BlockSpec((1, HEAD), lambda i: (0, 0))
    return pl.pallas_call(
        body, name=name, grid=(S // tm,),
        in_specs=[rd, rd, vd, rw, pl.BlockSpec((tm, W), lambda i: (i, 3)), vh, pl.BlockSpec((W, D), lambda i: (0, 0))],
        out_specs=(rd, rw, rw, vh, vd),
        out_shape=(SDS((S, D), MXU_DTYPE), SDS((S, W), f32), SDS((S, W), MXU_DTYPE), SDS((1, HEAD), f32),
                   SDS((1, D), f32)),
        compiler_params=_params(1),
    )(dxo, m, g3, o, proj, ng, wout)


def _erf_arg(x):
    return lax.erf(x * 0.7071067811865476)


@jax.custom_vjp
def _gelu_with_erf(x, e):
    return 0.5 * x * (1.0 + e)


def _gelu_with_erf_bwd(res, g):
    x, e = res
    return g * (0.5 * (1.0 + e) + x * (jnp.exp(-0.5 * x * x) * 0.3989422804014327)), jnp.zeros_like(e)


_gelu_with_erf.defvjp(lambda x, e: (0.5 * x * (1.0 + e), (x, e)), _gelu_with_erf_bwd)


def _layernorm(t, lg, lb):
    tc = t - jnp.mean(t, axis=-1, keepdims=True)
    return tc * lax.rsqrt(jnp.mean(tc * tc, axis=-1, keepdims=True) + LN_EPS) * lg + lb


def _sg_stage1_kept(eu, ev, pu, pv, bu, bv, lg, lb):
    return _gelu_with_erf(pu + bu, eu), _layernorm(_gelu_with_erf(pv + bv, ev), lg, lb)


def _causal_mask(n):
    return lax.broadcasted_iota(jnp.int32, (n, n), 0) >= lax.broadcasted_iota(jnp.int32, (n, n), 1)


def sg_mid(pre, b_in, ln_g, ln_b, w_s, bsT, wout, x1, g3, name):
    S = pre.shape[0]
    E, D = ln_g.shape[1], x1.shape[1]
    G, CH = SG_GROUPS, SG_CHUNK
    Cg = E // G
    tm = _row_tile(S, MIX_ROWS)

    def body(pu_ref, pv_ref, bu_ref, bv_ref, lg_ref, lb_ref, ws_ref, bs_ref, w_ref, x_ref, g_ref,
             xo_ref, m_ref, gt_ref, e_ref):
        xu, xv = pu_ref[...] + bu_ref[...], pv_ref[...] + bv_ref[...]
        eu, ev = _erf_arg(xu), _erf_arg(xv)
        e_ref[:, :E] = eu.astype(e_ref.dtype)
        e_ref[:, E:] = ev.astype(e_ref.dtype)
        u = 0.5 * xu * (1.0 + eu)
        v = _layernorm(0.5 * xv * (1.0 + ev), lg_ref[...], lb_ref[...])
        mask = _causal_mask(CH)
        for g in range(G):
            wc = _c(jnp.where(mask, ws_ref[g], 0.0))
            bcol = bs_ref[:, g:g + 1]
            cs = slice(g * Cg, (g + 1) * Cg)
            for ch in range(tm // CH):
                rs = slice(ch * CH, (ch + 1) * CH)
                mixed = _mm(wc, _c(v[rs, cs])) + bcol
                gt_ref[rs, cs] = (u[rs, cs] * mixed).astype(gt_ref.dtype)
        m = _mm(gt_ref[...], w_ref[...])
        m_ref[...] = m
        xo_ref[...] = x_ref[...] + _rms(m, g_ref[...])

    half = lambda p: pl.BlockSpec((tm, E), lambda i: (i, p))
    vhalf = lambda p: pl.BlockSpec((1, E), lambda i: (0, p))
    ve = pl.BlockSpec((1, E), lambda i: (0, 0))
    rd = pl.BlockSpec((tm, D), lambda i: (i, 0))
    return pl.pallas_call(
        body, name=name, grid=(S // tm,),
        in_specs=[half(0), half(1), vhalf(0), vhalf(1), ve, ve, pl.BlockSpec((G, CH, CH), lambda i: (0, 0, 0)),
                  pl.BlockSpec((CH, G), lambda i: (0, 0)), pl.BlockSpec((E, D), lambda i: (0, 0)), rd,
                  pl.BlockSpec((1, D), lambda i: (0, 0))],
        out_specs=(rd, rd, pl.BlockSpec((tm, E), lambda i: (i, 0)), pl.BlockSpec((tm, 2 * E), lambda i: (i, 0))),
        out_shape=(SDS((S, D), f32), SDS((S, D), f32), SDS((S, E), MXU_DTYPE), SDS((S, 2 * E), MXU_DTYPE)),
        compiler_params=_params(1),
    )(pre, pre, b_in, b_in, ln_g, ln_b, w_s, bsT, wout, x1, g3)


def sg_mid_bwd(dxo, m, g3, pre, kept_erf, b_in, ln_g, ln_b, w_s, bsT, wout, name):
    S = pre.shape[0]
    E, D = ln_g.shape[1], m.shape[1]
    G, CH = SG_GROUPS, SG_CHUNK
    Cg = E // G
    tm = _row_tile(S, SG_BWD_ROWS)

    def body(dxo_ref, m_ref, g_ref, pu_ref, pv_ref, eu_ref, ev_ref, bu_ref, bv_ref, lg_ref, lb_ref, ws_ref, bs_ref,
             w_ref, dm_ref, dpre_ref, dbin_ref, dlg_ref, dlb_ref, dws_ref, dbs_ref, dg_ref, du_scr, dv_scr):
        @pl.when(pl.program_id(0) == 0)
        def _():
            for r in (dbin_ref, dlg_ref, dlb_ref, dws_ref, dbs_ref, dg_ref):
                r[...] = jnp.zeros_like(r)

        dm, dg = _rms_bwd(m_ref[...], g_ref[...], dxo_ref[...])
        dg_ref[...] += dg
        dmc = dm.astype(dm_ref.dtype)
        dm_ref[...] = dmc
        dgated = _mm_nt(dmc, w_ref[...])
        stage1 = functools.partial(_sg_stage1_kept, eu_ref[...].astype(f32), ev_ref[...].astype(f32))
        (u, v), vjp1 = jax.vjp(stage1, pu_ref[...], pv_ref[...], bu_ref[...], bv_ref[...], lg_ref[...], lb_ref[...])
        mask = _causal_mask(CH)
        lane = lax.broadcasted_iota(jnp.int32, (CH, CH), 1)
        for g in range(G):
            wc = _c(jnp.where(mask, ws_ref[g], 0.0))
            bcol = bs_ref[:, g:g + 1]
            cs = slice(g * Cg, (g + 1) * Cg)
            dws = jnp.zeros((CH, CH), f32)
            dbs = jnp.zeros((CH, 1), f32)
            for ch in range(tm // CH):
                rs = slice(ch * CH, (ch + 1) * CH)
                vs = _c(v[rs, cs])
                mixed = _mm(wc, vs) + bcol
                dgt = dgated[rs, cs]
                du_scr[rs, cs] = dgt * mixed
                dmixed = dgt * u[rs, cs]
                dmc2 = _c(dmixed)
                dv_scr[rs, cs] = _mm_tn(wc, dmc2)
                dws = dws + _mm_nt(dmc2, vs)
                dbs = dbs + jnp.sum(dmixed, axis=1, keepdims=True)
            dws_ref[g] += jnp.where(mask, dws, 0.0)
            dbs_ref[...] += jnp.where(lane == g, jnp.broadcast_to(dbs, (CH, CH)), 0.0)
        dpu, dpv, dbu, dbv, dlg, dlb = vjp1((du_scr[...], dv_scr[...]))
        dpre_ref[:, :E] = dpu.astype(dpre_ref.dtype)
        dpre_ref[:, E:] = dpv.astype(dpre_ref.dtype)
        dbin_ref[:, :E] += dbu
        dbin_ref[:, E:] += dbv
        dlg_ref[...] += dlg
        dlb_ref[...] += dlb

    half = lambda p: pl.BlockSpec((tm, E), lambda i: (i, p))
    vhalf = lambda p: pl.BlockSpec((1, E), lambda i: (0, p))
    ve = pl.BlockSpec((1, E), lambda i: (0, 0))
    rd = pl.BlockSpec((tm, D), lambda i: (i, 0))
    vd = pl.BlockSpec((1, D), lambda i: (0, 0))
    wsb = pl.BlockSpec((G, CH, CH), lambda i: (0, 0, 0))
    return pl.pallas_call(
        body, name=name, grid=(S // tm,),
        in_specs=[rd, rd, vd, half(0), half(1), half(0), half(1), vhalf(0), vhalf(1), ve, ve, wsb,
                  pl.BlockSpec((CH, G), lambda i: (0, 0)), pl.BlockSpec((E, D), lambda i: (0, 0))],
        out_specs=(rd, pl.BlockSpec((tm, 2 * E), lambda i: (i, 0)), pl.BlockSpec((1, 2 * E), lambda i: (0, 0)), ve, ve,
                   wsb, pl.BlockSpec((CH, CH), lambda i: (0, 0)), vd),
        out_shape=(SDS((S, D), MXU_DTYPE), SDS((S, 2 * E), MXU_DTYPE), SDS((1, 2 * E), f32), SDS((1, E), f32),
                   SDS((1, E), f32), SDS((G, CH, CH), f32), SDS((CH, CH), f32), SDS((1, D), f32)),
        scratch_shapes=[pltpu.VMEM((tm, E), f32), pltpu.VMEM((tm, E), f32)], compiler_params=_params(1),
    )(dxo, m, g3, pre, pre, kept_erf, kept_erf, b_in, b_in, ln_g, ln_b, w_s, bsT, wout)


def loss_head(y, target, name):
    S, D = y.shape
    tm = _row_tile(S, 512)

    def body(y_ref, t_ref, l_ref, d_ref):
        @pl.when(pl.program_id(0) == 0)
        def _():
            l_ref[...] = jnp.zeros_like(l_ref)

        e = y_ref[...] - t_ref[...]
        d_ref[...] = e * (1.0 / D)
        l_ref[...] += jnp.sum(e * e) * (0.5 / D)

    row = pl.BlockSpec((tm, D), lambda i: (i, 0))
    return pl.pallas_call(
        body, name=name, grid=(S // tm,), in_specs=[row, row],
        out_specs=(pl.BlockSpec((1, HEAD), lambda i: (0, 0)), row),
        out_shape=(SDS((1, HEAD), f32), SDS((S, D), f32)), compiler_params=_params(1),
    )(y, target)


def sum_slots(r, name):
    _, R, C = r.shape
    tr = R // 2 if R % 16 == 0 else R

    def body(r_ref, o_ref):
        acc = r_ref[0].astype(f32)
        for s in range(1, N_DEV):
            acc = acc + r_ref[s].astype(f32)
        o_ref[...] = acc

    return pl.pallas_call(
        body, name=name, grid=(R // tr,), in_specs=[pl.BlockSpec((N_DEV, tr, C), lambda i: (0, i, 0))],
        out_specs=pl.BlockSpec((tr, C), lambda i: (i, 0)), out_shape=SDS((R, C), f32), compiler_params=_params(1),
    )(r)


def _adam_math(w, g, m, v):
    m = ADAM_B1 * m + (1.0 - ADAM_B1) * g
    v = ADAM_B2 * v + (1.0 - ADAM_B2) * (g * g)
    m_hat = m / (1.0 - ADAM_B1 ** ADAM_STEP)
    v_hat = v / (1.0 - ADAM_B2 ** ADAM_STEP)
    delta = -ADAM_LR * (m_hat / (jnp.sqrt(v_hat) + ADAM_EPS) + ADAM_WD * w)
    return delta, m, v


def adam_slots(w, rs, m, v, name, tr):
    R, C = w.shape
    tr = _row_tile(min(r.shape[1] for r in rs), tr)
    blocks = [r.shape[1] // tr for r in rs]
    starts = [sum(blocks[:k]) for k in range(len(rs))]
    assert sum(blocks) * tr == R

    def body(w_ref, *refs):
        r_refs, (m_ref, v_ref, g_ref, d_ref, mo_ref, vo_ref) = refs[:len(rs)], refs[len(rs):]
        i = pl.program_id(0)
        for k, r_ref in enumerate(r_refs):
            @pl.when((i >= starts[k]) & (i < starts[k] + blocks[k]))
            def _():
                g = r_ref[0].astype(f32)
                for s in range(1, N_DEV):
                    g = g + r_ref[s].astype(f32)
                g_ref[...] = g

        d_ref[...], mo_ref[...], vo_ref[...] = _adam_math(w_ref[...], g_ref[...], m_ref[...], v_ref[...])

    row = pl.BlockSpec((tr, C), lambda i: (i, 0))
    piece = lambda k: pl.BlockSpec((N_DEV, tr, C), lambda i: (0, jnp.clip(i - starts[k], 0, blocks[k] - 1), 0))
    return pl.pallas_call(
        body, name=name, grid=(R // tr,), in_specs=[row] + [piece(k) for k in range(len(rs))] + [row, row],
        out_specs=(row, row, row, row), out_shape=tuple(SDS((R, C), f32) for _ in range(4)),
        compiler_params=_params(1),
    )(w, *rs, m, v)


def adam_small(w, g, m, v, name):
    def body(w_ref, g_ref, m_ref, v_ref, d_ref, mo_ref, vo_ref):
        d_ref[...], mo_ref[...], vo_ref[...] = _adam_math(w_ref[...], g_ref[...], m_ref[...], v_ref[...])

    return pl.pallas_call(body, name=name, out_shape=tuple(SDS(w.shape, f32) for _ in range(3)))(w, g, m, v)


def _pack_rows(parts):
    rows, offs, r = [], [], 0
    for p in parts:
        flat = p.reshape(-1)
        n = -(-flat.shape[0] // HEAD)
        flat = jnp.pad(flat, (0, n * HEAD - flat.shape[0]))
        rows.append(flat.reshape(n, HEAD))
        offs.append((r, n))
        r += n
    pad = (-r) % 8
    if pad:
        rows.append(jnp.zeros((pad, HEAD), f32))
    return jnp.concatenate(rows, axis=0), offs


def kernel(x, norm_g, ffn_w_gate, ffn_w_up, ffn_w_down, dn_w_in, dn_conv_w, dn_a_log, dn_dt_bias, dn_norm_g, dn_w_out, sg_w_in, sg_b_in, sg_ln_g, sg_ln_b, sg_w_s, sg_b_s, sg_w_out, loss_target, m_norm_g, m_ffn_w_gate, m_ffn_w_up, m_ffn_w_down, m_dn_w_in, m_dn_conv_w, m_dn_a_log, m_dn_dt_bias, m_dn_norm_g, m_dn_w_out, m_sg_w_in, m_sg_b_in, m_sg_ln_g, m_sg_ln_b, m_sg_w_s, m_sg_b_s, m_sg_w_out, v_norm_g, v_ffn_w_gate, v_ffn_w_up, v_ffn_w_down, v_dn_w_in, v_dn_conv_w, v_dn_a_log, v_dn_dt_bias, v_dn_norm_g, v_dn_w_out, v_sg_w_in, v_sg_b_in, v_sg_ln_g, v_sg_ln_b, v_sg_w_s, v_sg_b_s, v_sg_w_out):
    weights = dict(norm_g=norm_g, ffn_w_gate=ffn_w_gate, ffn_w_up=ffn_w_up, ffn_w_down=ffn_w_down, dn_w_in=dn_w_in,
                   dn_conv_w=dn_conv_w, dn_a_log=dn_a_log, dn_dt_bias=dn_dt_bias, dn_norm_g=dn_norm_g,
                   dn_w_out=dn_w_out, sg_w_in=sg_w_in, sg_b_in=sg_b_in, sg_ln_g=sg_ln_g, sg_ln_b=sg_ln_b,
                   sg_w_s=sg_w_s, sg_b_s=sg_b_s, sg_w_out=sg_w_out)
    mom_m = dict(norm_g=m_norm_g, ffn_w_gate=m_ffn_w_gate, ffn_w_up=m_ffn_w_up, ffn_w_down=m_ffn_w_down,
                 dn_w_in=m_dn_w_in, dn_conv_w=m_dn_conv_w, dn_a_log=m_dn_a_log, dn_dt_bias=m_dn_dt_bias,
                 dn_norm_g=m_dn_norm_g, dn_w_out=m_dn_w_out, sg_w_in=m_sg_w_in, sg_b_in=m_sg_b_in,
                 sg_ln_g=m_sg_ln_g, sg_ln_b=m_sg_ln_b, sg_w_s=m_sg_w_s, sg_b_s=m_sg_b_s, sg_w_out=m_sg_w_out)
    mom_v = dict(norm_g=v_norm_g, ffn_w_gate=v_ffn_w_gate, ffn_w_up=v_ffn_w_up, ffn_w_down=v_ffn_w_down,
                 dn_w_in=v_dn_w_in, dn_conv_w=v_dn_conv_w, dn_a_log=v_dn_a_log, dn_dt_bias=v_dn_dt_bias,
                 dn_norm_g=v_dn_norm_g, dn_w_out=v_dn_w_out, sg_w_in=v_sg_w_in, sg_b_in=v_sg_b_in,
                 sg_ln_g=v_sg_ln_g, sg_ln_b=v_sg_ln_b, sg_w_s=v_sg_w_s, sg_b_s=v_sg_b_s, sg_w_out=v_sg_w_out)
    order = list(weights)

    xs = x[0]
    S, D = xs.shape
    F8 = ffn_w_gate.shape[-1]
    depth = norm_g.shape[0]
    W = dn_w_out.shape[1] * N_DEV
    H = W // HEAD
    E = sg_ln_g.shape[1] * N_DEV
    G, CH = sg_w_s.shape[1], sg_w_s.shape[2]
    c8 = dn_w_in.shape[2]
    me = _slot(lax.axis_index("x"), lax.axis_index("y"), lax.axis_index("c"))

    assert depth == 2
    small_in, small_offs = _pack_rows([norm_g, dn_conv_w, sg_b_in, sg_ln_g, sg_ln_b])
    wg0a, wu0a, wd0a, small_all = all_gather_multi(
        [_c(ffn_w_gate[0, 0]), _c(ffn_w_up[0, 0]), _c(ffn_w_down[0, 0]), small_in], name="gather_first")
    ffn_shards = lambda l, ab: [_c(ffn_w_gate[l, ab]), _c(ffn_w_up[l, ab]), _c(ffn_w_down[l, ab])]
    gather_dn = Comm("gather", [_c(dn_w_in[0]), _c(dn_w_out[0])])
    gather_mid = Comm("gather", ffn_shards(0, 1) + ffn_shards(1, 0))
    gather_end = Comm("gather", ffn_shards(1, 1))
    gather_sg = Comm("gather", [_c(sg_w_in[0]), _c(sg_w_out[0])])
    per = N_DEV // FFN_SLABS
    wide = lambda tag, g, u, d: (*widen_slabs([g, u], FFN_SLABS, name=f"widen_{tag}"),
                                 d.reshape(FFN_SLABS, per * F8, D))
    ffn_w = {(0, 0): wide("0a", wg0a, wu0a, wd0a)}

    def small_piece(i, shard_shape):
        r0, n = small_offs[i]
        sz = math.prod(shard_shape)
        return small_all[:, r0:r0 + n, :].reshape(N_DEV, n * HEAD)[:, :sz].reshape((N_DEV,) + tuple(shard_shape))

    ng_full = jnp.moveaxis(small_piece(0, norm_g.shape), 0, 2).reshape(depth, 6, D)
    conv_full = jnp.moveaxis(small_piece(1, dn_conv_w.shape[1:]), 0, 1).reshape(CONV_K, 3 * W)
    bin_full = small_piece(2, sg_b_in.shape[1:]).reshape(1, 2 * E)
    lng_full = small_piece(3, sg_ln_g.shape[1:]).reshape(1, E)
    lnb_full = small_piece(4, sg_ln_b.shape[1:]).reshape(1, E)
    gate_lanes = lambda v: jnp.pad(v.reshape(1, H), ((0, 0), (H, HEAD - 2 * H)))
    al_row, dt_row = gate_lanes(dn_a_log), gate_lanes(dn_dt_bias)
    bsT = sg_b_s[0].T
    gvec = lambda l, k: ng_full[l, k].reshape(1, D)

    saved = []
    cur = xs
    for l in range(depth):
        sv = {}
        sv['x0'] = cur
        (cur, sv['hA'], sv['pA'], sv['qA'], sv['tA'], sv['yA']), got = ffn_fwd(
            cur, gvec(l, 0), gvec(l, 1), *ffn_w[l, 0], name=f"ffn_fwd_{l}a", comm=gather_dn if l == 0 else gather_sg)
        sv['x1'] = cur
        if l == 1:
            sg_win = jnp.moveaxis(got[0], 0, 1).reshape(D, 2 * E)
            sg_wout = got[1].reshape(E, D)
        if l == 0:
            dnin_all, dnout_all = got
            dn_wmain, dn_wba = join_columns(dnin_all, 4 * W, name="dn_w_in_join")
            dn_wout = dnout_all.reshape(W, D)
            sv['hM'], sv['proj'], sv['pba'] = rms_mm(cur, gvec(l, 2), dn_wmain, dn_wba, name=f"dn_in_{l}")
            sv['qkv'] = dn_prep(sv['proj'], conv_full, name=f"dn_prep_{l}")
            sv['gates'] = dn_gates(sv['pba'], al_row, dt_row, H, name=f"dn_gates_{l}")
            (sv['o'], sv['states'], sv['T']), got = dn_chunk_fwd(sv['qkv'], sv['gates'], name=f"dn_chunk_{l}",
                                                                 comm=gather_mid)
            ffn_w[0, 1], ffn_w[1, 0] = wide("0b", *got[0:3]), wide("1a", *got[3:6])
            cur, sv['m'], sv['og'] = dn_out(sv['o'], sv['proj'], dn_norm_g, dn_wout, cur, gvec(l, 3), name=f"dn_out_{l}")
        else:
            sv['hM'], sv['pre'] = rms_mm(cur, gvec(l, 2), sg_win, None, name=f"sg_in_{l}")
            cur, sv['m'], sv['gated'], sv['erf'] = sg_mid(sv['pre'], bin_full, lng_full, lnb_full, sg_w_s[0], bsT, sg_wout,
                                                          cur, gvec(l, 3), name=f"sg_mid_{l}")
        sv['x2'] = cur
        (cur, sv['hB'], sv['pB'], sv['qB'], sv['tB'], sv['yB']), got = ffn_fwd(
            cur, gvec(l, 4), gvec(l, 5), *ffn_w[l, 1], name=f"ffn_fwd_{l}b", comm=gather_end if l == 0 else None)
        if l == 0:
            ffn_w[1, 1] = wide("1b", *got[0:3])
        saved.append(sv)

    loss_blk, dcur = loss_head(cur, loss_target[0], name="loss_head")
    loss = lax.psum(loss_blk[0, 0], ("x", "y", "c"))

    dng = [[None] * 6 for _ in range(depth)]
    ffn_dw = {}
    grads, slots = {}, {}

    def ffn_backward(l, ab, dcur, exchange=None):
        sv, s = saved[l], 'AB'[ab]
        (dcur, da, db, dy, dng[l][4 * ab], dng[l][4 * ab + 1]), got = ffn_bwd_dx(
            dcur, sv['x2' if ab else 'x0'], sv['y' + s], sv['p' + s], sv['q' + s], gvec(l, 4 * ab), gvec(l, 4 * ab + 1),
            *ffn_w[l, ab], name=f"ffn_bwd_{l}{'ab'[ab]}", comm=Comm("exchange", exchange) if exchange else None)
        ffn_dw[l, ab], _ = ffn_bwd_dw(sv['h' + s], dy, sv['t' + s], da, db, name=f"ffn_dw_{l}{'ab'[ab]}")
        return dcur, got

    sv = saved[1]
    dcur, _ = ffn_backward(1, 1, dcur)
    dm, dpre, grads['sg_b_in'], grads['sg_ln_g'], grads['sg_ln_b'], grads['sg_w_s'], dbs, dng[1][3] = sg_mid_bwd(
        dcur, sv['m'], gvec(1, 3), sv['pre'], sv['erf'], bin_full, lng_full, lnb_full, sg_w_s[0], bsT, sg_wout,
        name="sg_mid_bwd_1")
    grads['sg_b_s'] = dbs[:, :G].T
    dsg_wout = tn_mm(sv['gated'], dm, name="sg_wout_dw_1").reshape(N_DEV, E // N_DEV, D)
    dsg_win = tn_mm(sv['hM'], dpre, name="sg_win_dw_1", tn=2 * E // N_DEV, slot_major=True)
    (dcur, dng[1][2]), _ = mm_bwd_dx(dcur, sv['x1'], gvec(1, 2), dpre, sg_win, None, None, name="sg_in_bwd_1")
    dcur, l1b = ffn_backward(1, 0, dcur, exchange=list(ffn_dw[1, 1]))
    sv = saved[0]
    dcur, got = ffn_backward(0, 1, dcur, exchange=[dsg_win, dsg_wout])
    slots['sg_w_in'], slots['sg_w_out'] = [got[0]], [got[1]]
    dm, do, dz, grads['dn_norm_g'], dng[0][3] = dn_out_bwd(dcur, sv['m'], gvec(0, 3), sv['o'], sv['proj'], dn_norm_g,
                                                          dn_wout, name="dn_out_bwd_0")
    ddn_wout = tn_mm(sv['og'], dm, name="dn_wout_dw_0").reshape(N_DEV, W // N_DEV, D)
    (dqkv, dgates), got = dn_chunk_bwd(sv['qkv'], sv['gates'], sv['states'], sv['T'], do, name="dn_chunk_bwd_0",
                                       comm=Comm("exchange", list(ffn_dw[1, 0])))
    l1a = got
    dpba, dal, ddt = dn_gates_bwd(sv['pba'], al_row, dt_row, dgates, H, name="dn_gates_bwd_0")
    grads['dn_a_log'] = dal[:, H:2 * H]
    grads['dn_dt_bias'] = ddt[:, H:2 * H]
    (dproj, grads['dn_conv_w']), got = dn_prep_bwd(sv['proj'], conv_full, dqkv, dz, name="dn_prep_bwd_0",
                                                   comm=Comm("exchange", [*ffn_dw[0, 1], ddn_wout]))
    l0b, slots['dn_w_out'] = got[0:3], [got[3]]
    dw_main = tn_mm(sv['hM'], dproj, name="dn_win_dw_0")
    dw_ba = tn_mm(sv['hM'], dpba, name="dn_wba_dw_0", tn=HEAD)
    ddn_win = split_columns(dw_main, dw_ba, c8, name="dn_w_in_split")
    (dcur, dng[0][2]), got = mm_bwd_dx(dcur, sv['x1'], gvec(0, 2), dproj, dn_wmain, dpba, dn_wba, name="dn_in_bwd_0",
                                       comm=Comm("exchange", [ddn_win]))
    slots['dn_w_in'] = [got[0]]
    small_names = ['norm_g', 'dn_conv_w', 'sg_b_in', 'sg_ln_g', 'sg_ln_b', 'sg_w_s', 'sg_b_s', 'dn_a_log',
                   'dn_dt_bias', 'dn_norm_g']
    small = {}

    def gather_small():
        dng_full = jnp.stack([jnp.concatenate(r, axis=0) for r in dng], axis=0)
        small['parts'] = [dng_full, grads['dn_conv_w'], grads['sg_b_in'], grads['sg_ln_g'], grads['sg_ln_b'],
                          grads['sg_w_s'], grads['sg_b_s'], grads['dn_a_log'], grads['dn_dt_bias'], grads['dn_norm_g']]
        pack, small['offs'] = _pack_rows(small['parts'])
        return Comm("gather", [pack])

    (dcur, da, db, dy, dng[0][0], dng[0][1]), _ = ffn_bwd_dx(
        dcur, sv['x0'], sv['yA'], sv['pA'], sv['qA'], gvec(0, 0), gvec(0, 1), *ffn_w[0, 0], name="ffn_bwd_0a")
    grad_x = dcur[None]
    (dg,), (small_slots,) = ffn_bwd_dw_one(sv['hA'], da, False, name="ffn_dw_0a_gate", comm=gather_small())
    (du,), (xg,) = ffn_bwd_dw_one(sv['hA'], db, False, name="ffn_dw_0a_up", comm=Comm("exchange", [dg]))
    (dd,), (xu,) = ffn_bwd_dw_one(dy, sv['tA'], True, name="ffn_dw_0a_down", comm=Comm("exchange", [du]))
    small_parts, offs = small['parts'], small['offs']
    l0a = [xg, xu, exchange_slots([dd], name="exchange_last")[0]]
    for i, nm in enumerate(['ffn_w_gate', 'ffn_w_up', 'ffn_w_down']):
        slots[nm] = [l0a[i], l0b[i], l1a[i], l1b[i]]
    big_names = ['ffn_w_gate', 'ffn_w_up', 'ffn_w_down', 'dn_w_in', 'dn_w_out', 'sg_w_in', 'sg_w_out']
    slots = [slots[nm] for nm in big_names]
    small_sum = sum_slots(small_slots, name="sum_small_grads")

    def small_grad(i):
        r0, n = offs[i]
        p = small_parts[i]
        return small_sum[r0:r0 + n].reshape(-1)[:p.size].reshape(p.shape)

    def my_shard(full, axis, like):
        n = full.shape[axis] // N_DEV
        return lax.dynamic_slice_in_dim(full, me * n, n, axis).reshape(like.shape)

    g_small = {
        'norm_g': my_shard(small_grad(0), 2, norm_g),
        'dn_conv_w': my_shard(small_grad(1), 1, dn_conv_w),
        'sg_b_in': my_shard(small_grad(2), 1, sg_b_in),
        'sg_ln_g': my_shard(small_grad(3), 1, sg_ln_g),
        'sg_ln_b': my_shard(small_grad(4), 1, sg_ln_b),
        'sg_w_s': small_grad(5).reshape(sg_w_s.shape),
        'sg_b_s': small_grad(6).reshape(sg_b_s.shape),
        'dn_a_log': small_grad(7).reshape(dn_a_log.shape),
        'dn_dt_bias': small_grad(8).reshape(dn_dt_bias.shape),
        'dn_norm_g': small_grad(9).reshape(dn_norm_g.shape),
    }

    out_g, out_d, out_m, out_v = {}, {}, {}, {}
    for nm, r in zip(big_names, slots):
        w = weights[nm]
        cols = w.shape[-1]
        rows = w.size // cols
        tr = {'ffn_w_gate': 512, 'ffn_w_up': 512, 'ffn_w_down': F8 // 2, 'dn_w_in': 256, 'sg_w_in': 256}.get(nm, rows)
        pieces = [p.reshape(N_DEV, -1, cols) for p in r]
        g, d, m2, v2 = adam_slots(w.reshape(rows, cols), pieces, mom_m[nm].reshape(rows, cols),
                                  mom_v[nm].reshape(rows, cols), name=f"adam_{nm}", tr=tr)
        out_g[nm], out_d[nm], out_m[nm], out_v[nm] = (t.reshape(w.shape) for t in (g, d, m2, v2))
    for nm in small_names:
        w = weights[nm]
        cols = w.shape[-1]
        rows = w.size // cols
        two = lambda t: t.reshape(rows, cols)
        d, m2, v2 = adam_small(two(w), two(g_small[nm]), two(mom_m[nm]), two(mom_v[nm]), name=f"adam_{nm}")
        out_g[nm] = g_small[nm]
        out_d[nm], out_m[nm], out_v[nm] = (t.reshape(w.shape) for t in (d, m2, v2))

    return (loss, grad_x, *[out_g[n] for n in order], *[out_d[n] for n in order], *[out_m[n] for n in order],
            *[out_v[n] for n in order])
```

```python
import functools
import math

import jax
import jax.numpy as jnp
from jax import lax
from jax.experimental import pallas as pl
from jax.experimental.pallas import tpu as pltpu

f32 = jnp.float32
MXU_DTYPE = jnp.bfloat16
N_DEV = 8
RMS_EPS = 1e-6
LN_EPS = 1e-5
L2_EPS = 1e-6
HEAD = 128
DN_CHUNK = 64
SG_CHUNK = 128
SG_GROUPS = 8
CONV_K = 4
ADAM_LR, ADAM_B1, ADAM_B2, ADAM_EPS, ADAM_WD, ADAM_STEP = 0.001, 0.9, 0.999, 1e-08, 0.01, 10
VMEM_LIMIT = 56 * 1024 * 1024
FFN_ROWS_FWD, FFN_ROWS_BWD, FFN_ROWS_DW = 1024, 512, 2048
MIX_ROWS, SG_BWD_ROWS = 512, 256
PROJ_ROWS, TN_ROWS = 1024, 2048
FFN_SLABS = 4
SDS = jax.ShapeDtypeStruct
MESH = pl.DeviceIdType.MESH


def _params(n_grid):
    return pltpu.CompilerParams(dimension_semantics=("arbitrary",) * n_grid, vmem_limit_bytes=VMEM_LIMIT)


def _row_tile(s, want):
    t = min(s, want)
    assert s % t == 0, (s, t)
    return t


def _rms(x, g):
    return x * lax.rsqrt(jnp.mean(x * x, axis=-1, keepdims=True) + RMS_EPS) * g


def _rms_bwd(x, g, dy):
    r = lax.rsqrt(jnp.mean(x * x, axis=-1, keepdims=True) + RMS_EPS)
    t = dy * g
    dx = t * r - x * (jnp.mean(x * t, axis=-1, keepdims=True) * (r * r * r))
    return dx, jnp.sum(dy * (x * r), axis=0, keepdims=True)


def _silu(a):
    return a * jax.nn.sigmoid(a)


def _mm(a, b):
    return lax.dot_general(a, b, (((1,), (0,)), ((), ())), preferred_element_type=f32)


def _mm_nt(a, b):
    return lax.dot_general(a, b, (((1,), (1,)), ((), ())), preferred_element_type=f32)


def _mm_tn(a, b):
    return lax.dot_general(a, b, (((0,), (0,)), ((), ())), preferred_element_type=f32)


def _c(x):
    return x.astype(MXU_DTYPE)


def _split(a):
    hi = a.astype(MXU_DTYPE)
    lo = (a - hi.astype(f32)).astype(MXU_DTYPE)
    return hi, lo


def _dot3(a, b, dims):
    ah, al = _split(a)
    bh, bl = _split(b)
    d = lambda p, q: lax.dot_general(p, q, (dims, ((), ())), preferred_element_type=f32)
    return d(ah, bh) + (d(ah, bl) + d(al, bh))


NN, NT, TN = ((1,), (0,)), ((1,), (1,)), ((0,), (0,))


def _slot(px, py, pc):
    return 4 * px + 2 * py + pc


def all_gather_multi(arrs, name):
    return Comm("gather", arrs).alone(name)


def exchange_slots(arrs, name):
    return Comm("exchange", arrs).alone(name)


class Comm:
    def __init__(self, kind, arrs):
        self.kind, self.arrs, self.n = kind, list(arrs), len(arrs)
        hbm = pl.BlockSpec(memory_space=pltpu.HBM)
        self.in_specs = [hbm] * self.n
        self.out_specs = [hbm] * self.n
        lead = (N_DEV,) if kind == "gather" else ()
        self.out_shape = [SDS(lead + tuple(a.shape), a.dtype) for a in self.arrs]
        self.scratch = [pltpu.SemaphoreType.DMA((self.n, 7)), pltpu.SemaphoreType.DMA((self.n, 7)),
                        pltpu.SemaphoreType.DMA((self.n,))]

    def phase(self, p, ins, outs, sems):
        (self._gather if self.kind == "gather" else self._exchange)(p, ins, outs, sems)

    def _gather(self, p, ins, outs, sems):
        send_sems, recv_sems, local_sems = sems
        x, y, c = lax.axis_index("x"), lax.axis_index("y"), lax.axis_index("c")
        me, sibling = (x, y, c), (x, y, 1 - c)
        chips = [(1 - x, y), (x, 1 - y), (1 - x, 1 - y)]

        def copy(a, k, block, to, src=None):
            dst = outs[a].at[_slot(*block)]
            return pltpu.make_async_remote_copy(
                src_ref=dst if src is None else src, dst_ref=dst, send_sem=send_sems.at[a, k],
                recv_sem=recv_sems.at[a, k], device_id=to, device_id_type=MESH)

        mine = [pltpu.make_async_copy(ins[a], outs[a].at[_slot(*me)], local_sems.at[a]) for a in range(self.n)]
        first = [[copy(a, 0, me, sibling, src=ins[a])] +
                 [copy(a, 1 + j, me, (*chip, c), src=ins[a]) for j, chip in enumerate(chips)] for a in range(self.n)]
        passed = [[copy(a, 4 + j, (*chip, c), sibling) for j, chip in enumerate(chips)] for a in range(self.n)]
        if p == 0:
            for a in range(self.n):
                mine[a].start()
            for a in range(self.n):
                for cp in first[a]:
                    cp.start()
        elif p == 1:
            for a in range(self.n):
                for j, chip in enumerate(chips):
                    copy(a, 1 + j, (*chip, c), me).wait_recv()
                    passed[a][j].start()
        else:
            for a in range(self.n):
                copy(a, 0, sibling, me).wait_recv()
                for j, chip in enumerate(chips):
                    copy(a, 4 + j, (*chip, 1 - c), me).wait_recv()
            for a in range(self.n):
                for cp in first[a] + passed[a]:
                    cp.wait_send()
                mine[a].wait()

    def _exchange(self, p, ins, outs, sems):
        send_sems, recv_sems, local_sems = sems
        x, y, c = lax.axis_index("x"), lax.axis_index("y"), lax.axis_index("c")
        me = _slot(x, y, c)
        peers = [(x ^ (k >> 2), y ^ ((k >> 1) & 1), c ^ (k & 1)) for k in range(1, N_DEV)]

        def copy(a, k):
            peer = peers[k - 1]
            return pltpu.make_async_remote_copy(
                src_ref=ins[a].at[_slot(*peer)], dst_ref=outs[a].at[me], send_sem=send_sems.at[a, k - 1],
                recv_sem=recv_sems.at[a, k - 1], device_id=peer, device_id_type=MESH)

        def landed(a, k):
            peer = peers[k - 1]
            return pltpu.make_async_remote_copy(
                src_ref=ins[a].at[me], dst_ref=outs[a].at[_slot(*peer)], send_sem=send_sems.at[a, k - 1],
                recv_sem=recv_sems.at[a, k - 1], device_id=peer, device_id_type=MESH)

        local = [pltpu.make_async_copy(ins[a].at[me], outs[a].at[me], local_sems.at[a]) for a in range(self.n)]
        order = [6, 7, 2, 3, 4, 5, 1]
        if p == 0:
            for a in range(self.n):
                local[a].start()
            for a in range(self.n):
                for k in order:
                    copy(a, k).start()
        elif p == 2:
            for a in range(self.n):
                for k in order:
                    copy(a, k).wait_send()
                    landed(a, k).wait_recv()
                local[a].wait()

    def alone(self, name):
        n = self.n

        def body(*refs):
            for p in range(3):
                self.phase(p, refs[:n], refs[n:2 * n], refs[2 * n:])

        return pl.pallas_call(body, name=name, out_shape=tuple(self.out_shape), in_specs=self.in_specs,
                              out_specs=tuple(self.out_specs), scratch_shapes=self.scratch)(*self.arrs)


def hosted_call(body, comm, steps, *, name, grid, in_specs, out_specs, out_shape, scratch_shapes, args):
    if comm is None:
        outs = pl.pallas_call(body, name=name, grid=grid, in_specs=in_specs, out_specs=tuple(out_specs),
                              out_shape=tuple(out_shape), scratch_shapes=scratch_shapes,
                              compiler_params=_params(len(grid)))(*args)
        return outs, None
    ni, no, ns, cn = len(in_specs), len(out_specs), len(scratch_shapes), comm.n

    def both(*refs):
        h_in, c_in = refs[:ni], refs[ni:ni + cn]
        h_out, c_out = refs[ni + cn:ni + cn + no], refs[ni + cn + no:ni + 2 * cn + no]
        h_scr, c_scr = refs[ni + 2 * cn + no:ni + 2 * cn + no + ns], refs[ni + 2 * cn + no + ns:]
        when = steps()
        pl.when(when[0])(lambda: comm.phase(0, c_in, c_out, c_scr))
        body(*h_in, *h_out, *h_scr)
        pl.when(when[1])(lambda: comm.phase(1, c_in, c_out, c_scr))
        pl.when(when[2])(lambda: comm.phase(2, c_in, c_out, c_scr))

    outs = pl.pallas_call(
        both, name=name, grid=grid, in_specs=list(in_specs) + comm.in_specs,
        out_specs=tuple(out_specs) + tuple(comm.out_specs), out_shape=tuple(out_shape) + tuple(comm.out_shape),
        scratch_shapes=list(scratch_shapes) + comm.scratch, compiler_params=_params(len(grid)),
    )(*args, *comm.arrs)
    return outs[:no], outs[no:]


def _grid_steps(n_outer, n_inner=1):
    total = n_outer * n_inner

    def steps():
        t = pl.program_id(0) * n_inner + (pl.program_id(1) if n_inner > 1 else 0)
        return t == 0, t == (total * 7) // 8, t == total - 1
    return steps


def widen_slabs(arrs, ns, name):
    per = N_DEV // ns
    _, R, C = arrs[0].shape
    n = len(arrs)

    def body(*refs):
        for a in range(n):
            for k in range(per):
                refs[n + a][:, k * C:(k + 1) * C] = refs[a][k]

    return pl.pallas_call(
        body, name=name, grid=(ns,), in_specs=[pl.BlockSpec((per, R, C), lambda s: (s, 0, 0))] * n,
        out_specs=tuple(pl.BlockSpec((None, R, per * C), lambda s: (s, 0, 0)) for _ in range(n)),
        out_shape=tuple(SDS((ns, R, per * C), a.dtype) for a in arrs), compiler_params=_params(1))(*arrs)


def join_columns(blocks, n_main, name):
    nb, R, c8 = blocks.shape
    rest = nb * c8 - n_main
    tr = _row_tile(R, 256)

    def body(b_ref, main_ref, rest_ref, full):
        for k in range(nb):
            full[:, k * c8:(k + 1) * c8] = b_ref[k]
        main_ref[...] = full[:, :n_main]
        rest_ref[...] = jnp.zeros_like(rest_ref)
        rest_ref[:, :rest] = full[:, n_main:]

    return pl.pallas_call(
        body, name=name, grid=(R // tr,), in_specs=[pl.BlockSpec((nb, tr, c8), lambda i: (0, i, 0))],
        out_specs=(pl.BlockSpec((tr, n_main), lambda i: (i, 0)), pl.BlockSpec((tr, HEAD), lambda i: (i, 0))),
        out_shape=(SDS((R, n_main), blocks.dtype), SDS((R, HEAD), blocks.dtype)),
        scratch_shapes=[pltpu.VMEM((tr, nb * c8), blocks.dtype)], compiler_params=_params(1))(blocks)


def split_columns(main, rest, c8, name):
    R, n_main = main.shape
    nb = N_DEV
    n_rest = nb * c8 - n_main
    tr = _row_tile(R, 256)

    def body(main_ref, rest_ref, b_ref, full):
        full[:, :n_main] = main_ref[...]
        full[:, n_main:] = rest_ref[:, :n_rest]
        for k in range(nb):
            b_ref[k] = full[:, k * c8:(k + 1) * c8]

    return pl.pallas_call(
        body, name=name, grid=(R // tr,),
        in_specs=[pl.BlockSpec((tr, n_main), lambda i: (i, 0)), pl.BlockSpec((tr, HEAD), lambda i: (i, 0))],
        out_specs=pl.BlockSpec((nb, tr, c8), lambda i: (0, i, 0)), out_shape=SDS((nb, R, c8), main.dtype),
        scratch_shapes=[pltpu.VMEM((tr, nb * c8), main.dtype)], compiler_params=_params(1))(main, rest)


def ffn_fwd(x, gpre, gpost, wg, wu, wd, name, comm=None):
    S, D = x.shape
    nj, F8 = wg.shape[0], wg.shape[-1]
    tm = _row_tile(S, FFN_ROWS_FWD)

    def body(x_ref, gpre_ref, gpost_ref, wg_ref, wu_ref, wd_ref, xo_ref, h_ref, p_ref, q_ref, t_ref, y_ref):
        j = pl.program_id(1)

        @pl.when(j == 0)
        def _():
            h_ref[...] = _rms(x_ref[...], gpre_ref[...]).astype(h_ref.dtype)
            y_ref[...] = jnp.zeros_like(y_ref)

        h = h_ref[...]
        a = _mm(h, wg_ref[...])
        b = _mm(h, wu_ref[...])
        s = jax.nn.sigmoid(a)
        q = a * s
        p_ref[...] = (b * (s + q * (1.0 - s))).astype(p_ref.dtype)
        q_ref[...] = q.astype(q_ref.dtype)
        t = (q * b).astype(t_ref.dtype)
        t_ref[...] = t
        y_ref[...] += _mm(t, wd_ref[...])

        @pl.when(j == nj - 1)
        def _():
            xo_ref[...] = x_ref[...] + 0.5 * _rms(y_ref[...], gpost_ref[...])

    row = pl.BlockSpec((tm, D), lambda i, j: (i, 0))
    vec = pl.BlockSpec((1, D), lambda i, j: (0, 0))
    wcol = pl.BlockSpec((None, D, F8), lambda i, j: (j, 0, 0))
    wrow = pl.BlockSpec((None, F8, D), lambda i, j: (j, 0, 0))
    hid = pl.BlockSpec((None, tm, F8), lambda i, j: (j, i, 0))
    return hosted_call(
        body, comm, _grid_steps(S // tm, nj), name=name, grid=(S // tm, nj),
        in_specs=[row, vec, vec, wcol, wcol, wrow],
        out_specs=(row, row, hid, hid, hid, row),
        out_shape=(SDS((S, D), f32), SDS((S, D), MXU_DTYPE), SDS((nj, S, F8), MXU_DTYPE),
                   SDS((nj, S, F8), MXU_DTYPE), SDS((nj, S, F8), MXU_DTYPE), SDS((S, D), f32)),
        scratch_shapes=[], args=(x, gpre, gpost, wg, wu, wd))


def ffn_bwd_dx(dxo, x, y, p, q, gpre, gpost, wg, wu, wd, name, comm=None):
    S, D = x.shape
    NS, F8 = wg.shape[0], wg.shape[-1]
    sps = 2 if NS % 2 == 0 else 1
    nj = NS // sps
    tm = _row_tile(S, FFN_ROWS_BWD)

    def body(dxo_ref, x_ref, y_ref, p_ref, q_ref, gpre_ref, gpost_ref, wg_ref, wu_ref, wd_ref,
             dx_ref, da_ref, db_ref, dy_ref, dgpre_ref, dgpost_ref, dh_ref):
        i, j = pl.program_id(0), pl.program_id(1)

        @pl.when(j == 0)
        def _():
            @pl.when(i == 0)
            def _():
                dgpre_ref[...] = jnp.zeros_like(dgpre_ref)
                dgpost_ref[...] = jnp.zeros_like(dgpost_ref)

            dy, dg = _rms_bwd(y_ref[...], gpost_ref[...], 0.5 * dxo_ref[...])
            dy_ref[...] = dy.astype(dy_ref.dtype)
            dgpost_ref[...] += dg
            dh_ref[...] = jnp.zeros_like(dh_ref)

        dy = dy_ref[...]
        das, dbs = [], []
        for s in range(sps):
            dt = _mm_nt(dy, wd_ref[s])
            das.append((dt * p_ref[s].astype(f32)).astype(da_ref.dtype))
            dbs.append((dt * q_ref[s].astype(f32)).astype(db_ref.dtype))
            da_ref[s] = das[s]
            db_ref[s] = dbs[s]
        upd = None
        for s in range(sps):
            part = _mm_nt(das[s], wg_ref[s]) + _mm_nt(dbs[s], wu_ref[s])
            upd = part if upd is None else upd + part
        dh_ref[...] += upd

        @pl.when(j == nj - 1)
        def _():
            dxx, dg = _rms_bwd(x_ref[...], gpre_ref[...], dh_ref[...])
            dx_ref[...] = dxo_ref[...] + dxx
            dgpre_ref[...] += dg

    row = pl.BlockSpec((tm, D), lambda i, j: (i, 0))
    vec = pl.BlockSpec((1, D), lambda i, j: (0, 0))
    wcol = pl.BlockSpec((sps, D, F8), lambda i, j: (j, 0, 0))
    wrow = pl.BlockSpec((sps, F8, D), lambda i, j: (j, 0, 0))
    hid = pl.BlockSpec((sps, tm, F8), lambda i, j: (j, i, 0))
    return hosted_call(
        body, comm, _grid_steps(S // tm, nj), name=name, grid=(S // tm, nj),
        in_specs=[row, row, row, hid, hid, vec, vec, wcol, wcol, wrow],
        out_specs=(row, hid, hid, row, vec, vec),
        out_shape=(SDS((S, D), f32), SDS((NS, S, F8), MXU_DTYPE), SDS((NS, S, F8), MXU_DTYPE),
                   SDS((S, D), MXU_DTYPE), SDS((1, D), f32), SDS((1, D), f32)),
        scratch_shapes=[pltpu.VMEM((tm, D), f32)], args=(dxo, x, y, p, q, gpre, gpost, wg, wu, wd))


def ffn_bwd_dw(h, dy, t, da, db, name, comm=None):
    S, D = h.shape
    NS, F8 = t.shape[0], t.shape[-1]
    per = N_DEV // NS
    w8 = F8 // per
    tm = _row_tile(S, FFN_ROWS_DW)
    ni = S // tm

    def body(h_ref, dy_ref, t_ref, da_ref, db_ref, dwg_ref, dwu_ref, dwd_ref, accg, accu, accd):
        i = pl.program_id(1)

        @pl.when(i == 0)
        def _():
            accg[...] = jnp.zeros_like(accg)
            accu[...] = jnp.zeros_like(accu)
            accd[...] = jnp.zeros_like(accd)

        hh = h_ref[...]
        accg[...] += _mm_tn(hh, da_ref[...])
        accu[...] += _mm_tn(hh, db_ref[...])
        accd[...] += _mm_tn(t_ref[...], dy_ref[...])

        @pl.when(i == ni - 1)
        def _():
            for k in range(per):
                ks = slice(k * w8, (k + 1) * w8)
                dwg_ref[k] = accg[:, ks].astype(dwg_ref.dtype)
                dwu_ref[k] = accu[:, ks].astype(dwu_ref.dtype)
                dwd_ref[k] = accd[ks, :].astype(dwd_ref.dtype)

    row = pl.BlockSpec((tm, D), lambda j, i: (i, 0))
    hid = pl.BlockSpec((None, tm, F8), lambda j, i: (j, i, 0))
    wcol = pl.BlockSpec((per, D, w8), lambda j, i: (j, 0, 0))
    wrow = pl.BlockSpec((per, w8, D), lambda j, i: (j, 0, 0))
    return hosted_call(
        body, comm, _grid_steps(NS, ni), name=name, grid=(NS, ni),
        in_specs=[row, row, hid, hid, hid],
        out_specs=(wcol, wcol, wrow),
        out_shape=(SDS((N_DEV, D, w8), MXU_DTYPE), SDS((N_DEV, D, w8), MXU_DTYPE), SDS((N_DEV, w8, D), MXU_DTYPE)),
        scratch_shapes=[pltpu.VMEM((D, F8), f32), pltpu.VMEM((D, F8), f32), pltpu.VMEM((F8, D), f32)],
        args=(h, dy, t, da, db))


def ffn_bwd_dw_one(rows_op, slab_op, hidden_rows, name, comm=None):
    S, D = rows_op.shape
    NS, F8 = slab_op.shape[0], slab_op.shape[-1]
    per = N_DEV // NS
    w8 = F8 // per
    tm = _row_tile(S, FFN_ROWS_DW)
    ni = S // tm

    def body(r_ref, s_ref, o_ref, acc):
        i = pl.program_id(1)

        @pl.when(i == 0)
        def _():
            acc[...] = jnp.zeros_like(acc)

        acc[...] += _mm_tn(s_ref[...], r_ref[...]) if hidden_rows else _mm_tn(r_ref[...], s_ref[...])

        @pl.when(i == ni - 1)
        def _():
            for k in range(per):
                ks = slice(k * w8, (k + 1) * w8)
                o_ref[k] = (acc[ks, :] if hidden_rows else acc[:, ks]).astype(o_ref.dtype)

    blk = (per, w8, D) if hidden_rows else (per, D, w8)
    return hosted_call(
        body, comm, _grid_steps(NS, ni), name=name, grid=(NS, ni),
        in_specs=[pl.BlockSpec((tm, D), lambda j, i: (i, 0)), pl.BlockSpec((None, tm, F8), lambda j, i: (j, i, 0))],
        out_specs=(pl.BlockSpec(blk, lambda j, i: (j, 0, 0)),),
        out_shape=(SDS((N_DEV,) + blk[1:], MXU_DTYPE),),
        scratch_shapes=[pltpu.VMEM((F8, D) if hidden_rows else (D, F8), f32)], args=(rows_op, slab_op))


def rms_mm(x, g, w, w2, name, tn=1024):
    S, D = x.shape
    N = w.shape[1]
    tm = _row_tile(S, PROJ_ROWS)
    tn = _row_tile(N, tn)
    has2 = w2 is not None

    def body(*refs):
        if has2:
            x_ref, g_ref, w_ref, w2_ref, h_ref, o_ref, o2_ref = refs
        else:
            x_ref, g_ref, w_ref, h_ref, o_ref = refs
        j = pl.program_id(1)

        @pl.when(j == 0)
        def _():
            h = _rms(x_ref[...], g_ref[...]).astype(h_ref.dtype)
            h_ref[...] = h
            if has2:
                o2_ref[...] = _mm(h, w2_ref[...])

        o_ref[...] = _mm(h_ref[...], w_ref[...])

    row = pl.BlockSpec((tm, D), lambda i, j: (i, 0))
    in_specs = [row, pl.BlockSpec((1, D), lambda i, j: (0, 0)), pl.BlockSpec((D, tn), lambda i, j: (0, j))]
    out_specs = [row, pl.BlockSpec((tm, tn), lambda i, j: (i, j))]
    out_shape = [SDS((S, D), MXU_DTYPE), SDS((S, N), f32)]
    args = [x, g, w]
    if has2:
        in_specs.append(pl.BlockSpec((D, w2.shape[1]), lambda i, j: (0, 0)))
        out_specs.append(pl.BlockSpec((tm, w2.shape[1]), lambda i, j: (i, 0)))
        out_shape.append(SDS((S, w2.shape[1]), f32))
        args.append(w2)
    return pl.pallas_call(
        body, name=name, grid=(S // tm, N // tn), in_specs=in_specs, out_specs=tuple(out_specs),
        out_shape=tuple(out_shape), compiler_params=_params(2),
    )(*args)


def mm_bwd_dx(dres, x, g, dy, w, dy2, w2, name, tk=1024, comm=None):
    S, D = x.shape
    K = dy.shape[1]
    tm = _row_tile(S, PROJ_ROWS)
    tk = _row_tile(K, tk)
    nk = K // tk
    has2 = dy2 is not None

    def body(*refs):
        if has2:
            dres_ref, x_ref, g_ref, dy_ref, w_ref, dy2_ref, w2_ref, dx_ref, dg_ref, dh_ref = refs
        else:
            dres_ref, x_ref, g_ref, dy_ref, w_ref, dx_ref, dg_ref, dh_ref = refs
        i, k = pl.program_id(0), pl.program_id(1)

        @pl.when(k == 0)
        def _():
            @pl.when(i == 0)
            def _():
                dg_ref[...] = jnp.zeros_like(dg_ref)

            if has2:
                dh_ref[...] = _mm_nt(dy2_ref[...], w2_ref[...])
            else:
                dh_ref[...] = jnp.zeros_like(dh_ref)

        dh_ref[...] += _mm_nt(dy_ref[...], w_ref[...])

        @pl.when(k == nk - 1)
        def _():
            dxx, dg = _rms_bwd(x_ref[...], g_ref[...], dh_ref[...])
            dx_ref[...] = dres_ref[...] + dxx
            dg_ref[...] += dg

    row = pl.BlockSpec((tm, D), lambda i, k: (i, 0))
    vec = pl.BlockSpec((1, D), lambda i, k: (0, 0))
    in_specs = [row, row, vec, pl.BlockSpec((tm, tk), lambda i, k: (i, k)), pl.BlockSpec((D, tk), lambda i, k: (0, k))]
    args = [dres, x, g, dy, w]
    if has2:
        in_specs += [pl.BlockSpec((tm, dy2.shape[1]), lambda i, k: (i, 0)),
                     pl.BlockSpec((D, w2.shape[1]), lambda i, k: (0, 0))]
        args += [dy2, w2]
    return hosted_call(
        body, comm, _grid_steps(S // tm, nk), name=name, grid=(S // tm, nk), in_specs=in_specs, out_specs=(row, vec),
        out_shape=(SDS((S, D), f32), SDS((1, D), f32)), scratch_shapes=[pltpu.VMEM((tm, D), f32)], args=args)


def tn_mm(a, b, name, tn=512, slot_major=False):
    S, K1 = a.shape
    N = b.shape[1]
    tm = _row_tile(S, TN_ROWS)
    tn = _row_tile(N, tn)
    ni = S // tm

    def body(a_ref, b_ref, o_ref, acc):
        i = pl.program_id(1)

        @pl.when(i == 0)
        def _():
            acc[...] = jnp.zeros_like(acc)

        acc[...] += _mm_tn(a_ref[...], b_ref[...])

        @pl.when(i == ni - 1)
        def _():
            o_ref[...] = acc[...].astype(o_ref.dtype)

    if slot_major:
        out_spec, out_shape = pl.BlockSpec((None, K1, tn), lambda j, i: (j, 0, 0)), SDS((N // tn, K1, tn), MXU_DTYPE)
    else:
        out_spec, out_shape = pl.BlockSpec((K1, tn), lambda j, i: (0, j)), SDS((K1, N), MXU_DTYPE)
    return pl.pallas_call(
        body, name=name, grid=(N // tn, ni),
        in_specs=[pl.BlockSpec((tm, K1), lambda j, i: (i, 0)), pl.BlockSpec((tm, tn), lambda j, i: (i, j))],
        out_specs=out_spec, out_shape=out_shape,
        scratch_shapes=[pltpu.VMEM((K1, tn), f32)], compiler_params=_params(2),
    )(a, b)


CONV_ROWS = 512


def _shift_down(cur, prev8, s):
    r = pltpu.roll(cur, s, 0)
    row = lax.broadcasted_iota(jnp.int32, (8, cur.shape[1]), 0)
    top = jnp.where(row < s, pltpu.roll(prev8, s, 0), r[0:8])
    return jnp.concatenate([top, r[8:]], axis=0)


def _shift_up(cur, next8, s):
    n = cur.shape[0]
    r = pltpu.roll(cur, n - s, 0)
    row = lax.broadcasted_iota(jnp.int32, (8, cur.shape[1]), 0)
    bot = jnp.where(row >= 8 - s, pltpu.roll(next8, 8 - s, 0), r[n - 8:])
    return jnp.concatenate([r[:n - 8], bot], axis=0)


def _conv_taps(cur, prev8):
    return [_shift_down(cur, prev8, 3), _shift_down(cur, prev8, 2), _shift_down(cur, prev8, 1), cur]


def _act_qk(c):
    a = _silu(c)
    return a * lax.rsqrt(jnp.sum(a * a, axis=-1, keepdims=True) + L2_EPS)


def dn_prep(proj, conv_w, name):
    S = proj.shape[0]
    W = conv_w.shape[1] // 3
    nh = W // HEAD
    R = _row_tile(S, CONV_ROWS)

    def body(p_ref, w_ref, o_ref):
        j = pl.program_id(0)
        w = w_ref[...]

        def rows(r, prev8):
            cur = p_ref[pl.ds(r, R), :]
            taps = _conv_taps(cur, prev8)
            cv = taps[0] * w[0:1] + taps[1] * w[1:2] + taps[2] * w[2:3] + taps[3] * w[3:4]

            @pl.when(j < 2 * nh)
            def _():
                o_ref[pl.ds(r, R), :] = _act_qk(cv)

            @pl.when(j >= 2 * nh)
            def _():
                o_ref[pl.ds(r, R), :] = _silu(cv)

        rows(0, jnp.zeros((8, HEAD), f32))

        @pl.loop(1, S // R)
        def _(t):
            r = pl.multiple_of(t * R, R)
            rows(r, p_ref[pl.ds(r - 8, 8), :])

    return pl.pallas_call(
        body, name=name, grid=(3 * nh,),
        in_specs=[pl.BlockSpec((S, HEAD), lambda j: (0, j)), pl.BlockSpec((CONV_K, HEAD), lambda j: (0, j))],
        out_specs=pl.BlockSpec((None, S, HEAD), lambda j: (j // nh, 0, j % nh)),
        out_shape=SDS((3, S, W), f32), compiler_params=_params(1),
    )(proj, conv_w)


def dn_prep_bwd(proj, conv_w, dqkv, dz, name, comm=None):
    S = proj.shape[0]
    W = conv_w.shape[1] // 3
    nh = W // HEAD
    nq = 3 * nh
    R = _row_tile(S, CONV_ROWS)
    nr = S // R

    def body(p_ref, w_ref, dq_ref, dz_ref, dp_ref, dw_ref, dc_ref):
        j = pl.program_id(0)

        @pl.when(j >= nq)
        def _():
            dp_ref[...] = dz_ref[...].astype(dp_ref.dtype)

        @pl.when(j < nq)
        def _():
            w = w_ref[...]
            dw_ref[...] = jnp.zeros_like(dw_ref)

            def rows(r, prev8):
                cur = p_ref[pl.ds(r, R), :]
                taps = _conv_taps(cur, prev8)
                cv = taps[0] * w[0:1] + taps[1] * w[1:2] + taps[2] * w[2:3] + taps[3] * w[3:4]
                dn = dq_ref[pl.ds(r, R), :]

                @pl.when(j < 2 * nh)
                def _():
                    dc_ref[pl.ds(r, R), :] = jax.vjp(_act_qk, cv)[1](dn)[0]

                @pl.when(j >= 2 * nh)
                def _():
                    dc_ref[pl.ds(r, R), :] = jax.vjp(_silu, cv)[1](dn)[0]

                dc = dc_ref[pl.ds(r, R), :]
                dw_ref[...] += jnp.concatenate(
                    [jnp.sum(dc * taps[q], axis=0, keepdims=True) for q in range(CONV_K)], axis=0)

            rows(0, jnp.zeros((8, HEAD), f32))

            @pl.loop(1, nr)
            def _(t):
                r = pl.multiple_of(t * R, R)
                rows(r, p_ref[pl.ds(r - 8, 8), :])

            def back(r, next8):
                dc = dc_ref[pl.ds(r, R), :]
                dx = dc * w[3:4]
                for s in (1, 2, 3):
                    dx = dx + _shift_up(dc, next8, s) * w[3 - s:4 - s]
                dp_ref[pl.ds(r, R), :] = dx.astype(dp_ref.dtype)

            @pl.loop(0, nr - 1)
            def _(t):
                r = pl.multiple_of(t * R, R)
                back(r, dc_ref[pl.ds(r + R, 8), :])

            back((nr - 1) * R, jnp.zeros((8, HEAD), f32))

    clamp = lambda j: jnp.minimum(j, nq - 1)
    return hosted_call(
        body, comm, _grid_steps(4 * nh), name=name, grid=(4 * nh,),
        in_specs=[pl.BlockSpec((S, HEAD), lambda j: (0, clamp(j))),
                  pl.BlockSpec((CONV_K, HEAD), lambda j: (0, clamp(j))),
                  pl.BlockSpec((None, S, HEAD), lambda j: (clamp(j) // nh, 0, clamp(j) % nh)),
                  pl.BlockSpec((S, HEAD), lambda j: (0, jnp.maximum(j - nq, 0)))],
        out_specs=(pl.BlockSpec((S, HEAD), lambda j: (0, j)), pl.BlockSpec((CONV_K, HEAD), lambda j: (0, clamp(j)))),
        out_shape=(SDS((S, 4 * W), MXU_DTYPE), SDS((CONV_K, 3 * W), f32)),
        scratch_shapes=[pltpu.VMEM((S, HEAD), f32)], args=(proj, conv_w, dqkv, dz))


def _lane_pick(x, lane):
    sel = lax.broadcasted_iota(jnp.int32, x.shape, 1) == lane
    return jnp.broadcast_to(jnp.sum(jnp.where(sel, x, 0.0), axis=1, keepdims=True), x.shape)


CUM_ROWS = 256


def _sel_mm(m01, x):
    m = _c(m01)
    d = lambda p: lax.dot_general(m, p, (NN, ((), ())), preferred_element_type=f32)
    h1, h2, h3 = _pieces3(x)
    return (d(h1) + d(h2)) + d(h3)


def _chunk_cumsum_matrix(n, transpose):
    r, c = lax.broadcasted_iota(jnp.int32, (n, n), 0), lax.broadcasted_iota(jnp.int32, (n, n), 1)
    sh = int(math.log2(DN_CHUNK))
    same = (r >> sh) == (c >> sh)
    return jnp.where(same & ((r <= c) if transpose else (r >= c)), 1.0, 0.0).astype(f32)


def _gates_by_lane(H, p, al, dt):
    lane = lax.broadcasted_iota(jnp.int32, p.shape, 1)
    g = -jnp.exp(al) * jax.nn.softplus(p + dt)
    return jnp.where(lane < H, jax.nn.sigmoid(p), jnp.where(lane < 2 * H, g, 0.0))


def dn_gates(pba, al, dt, H, name):
    S = pba.shape[0]
    R = _row_tile(S, CUM_ROWS)

    def body(p_ref, al_ref, dt_ref, o_ref):
        raw = _gates_by_lane(H, p_ref[...], al_ref[...], dt_ref[...])
        lane = lax.broadcasted_iota(jnp.int32, raw.shape, 1)
        o_ref[...] = jnp.where(lane < H, raw, _sel_mm(_chunk_cumsum_matrix(R, False), raw))

    blk = pl.BlockSpec((R, HEAD), lambda i: (i, 0))
    par = pl.BlockSpec((1, HEAD), lambda i: (0, 0))
    return pl.pallas_call(body, name=name, grid=(S // R,), in_specs=[blk, par, par], out_specs=blk,
                          out_shape=SDS((S, HEAD), f32), compiler_params=_params(1))(pba, al, dt)


def dn_gates_bwd(pba, al, dt, dgates, H, name):
    S = pba.shape[0]
    R = _row_tile(S, CUM_ROWS)

    def body(p_ref, al_ref, dt_ref, dg_ref, dp_ref, dal_ref, ddt_ref):
        @pl.when(pl.program_id(0) == 0)
        def _():
            dal_ref[...] = jnp.zeros_like(dal_ref)
            ddt_ref[...] = jnp.zeros_like(ddt_ref)

        d = dg_ref[...]
        lane = lax.broadcasted_iota(jnp.int32, d.shape, 1)
        d = jnp.where(lane < H, d, _sel_mm(_chunk_cumsum_matrix(R, True), d))
        _, vjp = jax.vjp(functools.partial(_gates_by_lane, H), p_ref[...], al_ref[...], dt_ref[...])
        dp, dal, ddt = vjp(d)
        dp_ref[...] = dp.astype(dp_ref.dtype)
        dal_ref[...] += dal
        ddt_ref[...] += ddt

    blk = pl.BlockSpec((R, HEAD), lambda i: (i, 0))
    par = pl.BlockSpec((1, HEAD), lambda i: (0, 0))
    return pl.pallas_call(
        body, name=name, grid=(S // R,), in_specs=[blk, par, par, blk], out_specs=(blk, par, par),
        out_shape=(SDS((S, HEAD), MXU_DTYPE), SDS((1, HEAD), f32), SDS((1, HEAD), f32)), compiler_params=_params(1),
    )(pba, al, dt, dgates)


def _bdot(dims):
    back = {NN: ((NT, 'gb'), (TN, 'ag')), NT: ((NN, 'gb'), (TN, 'ga')), TN: ((NT, 'bg'), (NN, 'ag'))}[dims]
    d = lambda p, q, dm: lax.dot_general(_c(p), _c(q), (dm, ((), ())), preferred_element_type=f32)

    @jax.custom_vjp
    def f(a, b):
        return d(a, b, dims)

    def fwd(a, b):
        return d(a, b, dims), (a, b)

    def bwd(res, g):
        v = {'a': res[0], 'b': res[1], 'g': g}
        (da_dims, da_ops), (db_dims, db_ops) = back
        return d(v[da_ops[0]], v[da_ops[1]], da_dims), d(v[db_ops[0]], v[db_ops[1]], db_dims)

    f.defvjp(fwd, bwd)
    return f, lambda a, b: d(a, b, dims)


_BDOT = {dims: _bdot(dims) for dims in (NN, NT, TN)}


def _tri_inv_multi(Ls):
    n = Ls[0].shape[0]
    eye = jnp.where(lax.broadcasted_iota(jnp.int32, (n, n), 0) == lax.broadcasted_iota(jnp.int32, (n, n), 1), 1.0, 0.0)
    P = tuple(-L for L in Ls)
    T = tuple(eye + p for p in P)
    P = tuple(_dot3(p, p, NN) for p in P)
    levels = int(math.log2(n)) - 1
    for lvl in range(levels):
        if lvl == levels - 1:
            T = tuple(t + _dot3(t, p, NN) for t, p in zip(T, P))
        else:
            both = tuple(_dot3(jnp.concatenate([t, p], axis=0), p, NN) for t, p in zip(T, P))
            T = tuple(t + b[:n] for t, b in zip(T, both))
            P = tuple(b[n:] for b in both)
    return T


@jax.custom_vjp
def _tri_inv_kept(Ls, Ts):
    return Ts


def _tri_inv_kept_bwd(T, dT):
    X = tuple(_dot3(d, t, NT) for d, t in zip(dT, T))
    return tuple(-_dot3(t, x, TN) for t, x in zip(T, X)), tuple(jnp.zeros_like(t) for t in T)


_tri_inv_kept.defvjp(lambda Ls, Ts: (Ts, Ts), _tri_inv_kept_bwd)


def _pieces3(x):
    h1 = x.astype(MXU_DTYPE)
    r1 = x - h1.astype(f32)
    h2 = r1.astype(MXU_DTYPE)
    return h1, h2, (r1 - h2.astype(f32)).astype(MXU_DTYPE)


def _row_bcast_impl(sel_row, gc):
    s = _c(sel_row)
    d = lambda p: lax.dot_general(s, p, (NT, ((), ())), preferred_element_type=f32)
    h1, h2, h3 = _pieces3(gc)
    return (d(h1) + d(h2)) + d(h3)


def _row_bcast_bwd(sel_row, d):
    s = _c(sel_row)
    hi, lo = _split(d)
    t = lambda p: lax.dot_general(p, s, (TN, ((), ())), preferred_element_type=f32)
    return jnp.zeros_like(sel_row), t(hi) + t(lo)


_row_bcast = jax.custom_vjp(_row_bcast_impl)
_row_bcast.defvjp(lambda sel_row, gc: (_row_bcast_impl(sel_row, gc), sel_row), _row_bcast_bwd)


def _col_bcast_impl(gc):
    return gc[:, :DN_CHUNK]


def _col_bcast_bwd(_, d):
    return (jnp.broadcast_to(jnp.sum(d, axis=1, keepdims=True) * (1.0 / HEAD), (d.shape[0], HEAD)),)


_col_bcast = jax.custom_vjp(_col_bcast_impl)
_col_bcast.defvjp(lambda gc: (_col_bcast_impl(gc), None), _col_bcast_bwd)


def _last_row_bcast(n):
    def impl(gc):
        return jnp.broadcast_to(gc[DN_CHUNK - 1:DN_CHUNK, :], (n, HEAD))

    def bwd(_, d):
        row = lax.broadcasted_iota(jnp.int32, (DN_CHUNK, HEAD), 0)
        return (jnp.where(row == DN_CHUNK - 1, jnp.sum(d, axis=0, keepdims=True), 0.0),)

    f = jax.custom_vjp(impl)
    f.defvjp(lambda gc: (impl(gc), None), bwd)
    return impl, f


_LAST_C, _LAST_H = _last_row_bcast(DN_CHUNK), _last_row_bcast(HEAD)


def _halves(axis):
    def impl(x):
        n = x.shape[axis] // 2
        return lax.slice_in_dim(x, 0, n, axis=axis), lax.slice_in_dim(x, n, 2 * n, axis=axis)

    f = jax.custom_vjp(impl)
    f.defvjp(lambda x: (impl(x), None), lambda _, g: (jnp.concatenate(g, axis=axis),))
    return impl, f


_ROW_HALVES, _COL_HALVES = _halves(0), _halves(1)


def _chunk_consts():
    C = DN_CHUNK
    io = lambda shape, ax: lax.broadcasted_iota(jnp.int32, shape, ax)
    one = lambda m: jnp.where(m, 1.0, 0.0).astype(f32)
    r, c = io((C, C), 0), io((C, C), 1)
    return dict(causal=r >= c, strict=r > c, sel_row=one(io((C, HEAD), 1) == 0))


def _chunk_fn(kc, kept_T, q, k, v, gc, bB, S0):
    diff = kept_T is not None
    i = 0 if diff else 1
    mm, mm_nt, mm_tn = _BDOT[NN][i], _BDOT[NT][i], _BDOT[TN][i]
    tri = (lambda Ls: _tri_inv_kept(Ls, kept_T)) if diff else _tri_inv_multi
    each = lambda f, *ls: tuple(f(*a) for a in zip(*ls))
    gcol = each(_col_bcast if diff else _col_bcast_impl, gc)
    grow = each(lambda g: (_row_bcast if diff else _row_bcast_impl)(kc['sel_row'], g), gc)
    glc = each(_LAST_C[i ^ 1], gc)
    glh = each(_LAST_H[i ^ 1], gc)
    decay = each(lambda a, b: jnp.where(kc['causal'], jnp.exp(jnp.where(kc['causal'], a - b, 0.0)), 0.0), gcol, grow)
    rows, cols = _ROW_HALVES[i ^ 1], _COL_HALVES[i ^ 1]
    first, second = (lambda ts: tuple(t[0] for t in ts)), (lambda ts: tuple(t[1] for t in ts))
    kb = each(lambda a, b: a * b, k, bB)
    vb = each(lambda a, b: a * b, v, bB)
    egc = each(jnp.exp, gc)
    qs = each(lambda a: a * (HEAD ** -0.5), q)
    kq = each(lambda a, b, kt: rows(mm_nt(jnp.concatenate([a, b], axis=0), kt)), kb, qs, k)
    kk, qk = first(kq), second(kq)
    T = tri(each(lambda a, d: jnp.where(kc['strict'], a * d, 0.0), kk, decay))
    uw = each(lambda t, a, b, e: cols(mm(t, jnp.concatenate([a, b * e], axis=1))), T, vb, kb, egc)
    u, w = first(uw), second(uw)
    attn = each(lambda a, d: jnp.where(kc['causal'], a * d, 0.0), qk, decay)
    wq = each(lambda a, b, e, s: rows(mm(jnp.concatenate([a, b * e], axis=0), s)), w, qs, egc, S0)
    wS, qS = first(wq), second(wq)
    v_new = each(lambda a, b: a - b, u, wS)
    o = each(lambda a, b: a + b, qS, each(mm, attn, v_new))
    kdec = each(lambda a, gl, g: a * jnp.exp(gl - g), k, glc, gc)
    S1 = each(lambda s, gl, kv: s * jnp.exp(gl) + kv, S0, glh, each(mm_tn, kdec, v_new))
    return (o, S1) if diff else (o, S1, T)


def _chunks_per_step(N):
    return 4 if N % 4 == 0 else (2 if N % 2 == 0 else 1)


def _heads_per_block(H):
    return 8 if H % 8 == 0 else (4 if H % 4 == 0 else 1)


def dn_chunk_fwd(qkv, gates, name, comm=None):
    _, S, W = qkv.shape
    H, C = W // HEAD, DN_CHUNK
    N, HB = S // C, _heads_per_block(H)
    assert HB == H
    CPS = _chunks_per_step(N)

    def body(q_ref, k_ref, v_ref, g_ref, o_ref, st_ref, t_ref, s_scr):
        @pl.when(pl.program_id(1) == 0)
        def _():
            s_scr[...] = jnp.zeros_like(s_scr)

        kc = _chunk_consts()
        sls = [slice(hh * HEAD, (hh + 1) * HEAD) for hh in range(HB)]
        St = tuple(s_scr[hh] for hh in range(HB))
        for c in range(CPS):
            rows = slice(c * C, (c + 1) * C)
            heads = lambda ref: tuple(ref[rows, sl] for sl in sls)
            gr = g_ref[rows, :]
            for hh in range(HB):
                st_ref[c, hh] = St[hh]
            o, St, T = _chunk_fn(kc, None, heads(q_ref), heads(k_ref), heads(v_ref),
                                 tuple(_lane_pick(gr, H + hh) for hh in range(HB)),
                                 tuple(_lane_pick(gr, hh) for hh in range(HB)), St)
            for hh in range(HB):
                o_ref[rows, sls[hh]] = o[hh]
                t_ref[c, hh] = T[hh]
        for hh in range(HB):
            s_scr[hh] = St[hh]

    part = lambda p: pl.BlockSpec((None, CPS * C, HB * HEAD), lambda hb, n: (p, n, hb))
    return hosted_call(
        body, comm, _grid_steps(H // HB, N // CPS), name=name, grid=(H // HB, N // CPS),
        in_specs=[part(0), part(1), part(2), pl.BlockSpec((CPS * C, HEAD), lambda hb, n: (n, 0))],
        out_specs=(pl.BlockSpec((CPS * C, HB * HEAD), lambda hb, n: (n, hb)),
                   pl.BlockSpec((CPS, HB, HEAD, HEAD), lambda hb, n: (n, hb, 0, 0)),
                   pl.BlockSpec((CPS, HB, C, C), lambda hb, n: (n, hb, 0, 0))),
        out_shape=(SDS((S, W), f32), SDS((N, H, HEAD, HEAD), f32), SDS((N, H, C, C), f32)),
        scratch_shapes=[pltpu.VMEM((HB, HEAD, HEAD), f32)], args=(qkv, qkv, qkv, gates))


def dn_chunk_bwd(qkv, gates, states, kept_T, do, name, comm=None):
    _, S, W = qkv.shape
    H, C = W // HEAD, DN_CHUNK
    N, HB = S // C, _heads_per_block(H)
    assert HB == H
    CPS = _chunks_per_step(N)
    NB = N // CPS

    def body(q_ref, k_ref, v_ref, g_ref, st_ref, t_ref, do_ref, dqkv_ref, dg_ref, ds_scr):
        @pl.when(pl.program_id(1) == 0)
        def _():
            ds_scr[...] = jnp.zeros_like(ds_scr)

        kc = _chunk_consts()
        sls = [slice(hh * HEAD, (hh + 1) * HEAD) for hh in range(HB)]
        dSt = tuple(ds_scr[hh] for hh in range(HB))
        for c in reversed(range(CPS)):
            rows = slice(c * C, (c + 1) * C)
            heads = lambda ref: tuple(ref[rows, sl] for sl in sls)
            gr = g_ref[rows, :]
            kept = tuple(t_ref[c, hh] for hh in range(HB))
            _, vjp = jax.vjp(functools.partial(_chunk_fn, kc, kept), heads(q_ref), heads(k_ref), heads(v_ref),
                             tuple(_lane_pick(gr, H + hh) for hh in range(HB)),
                             tuple(_lane_pick(gr, hh) for hh in range(HB)), tuple(st_ref[c, hh] for hh in range(HB)))
            dq, dk, dv, dg, db, dSt = vjp((heads(do_ref), dSt))
            lane = lax.broadcasted_iota(jnp.int32, (C, HEAD), 1)
            dgr = jnp.zeros((C, HEAD), f32)
            for hh in range(HB):
                dqkv_ref[0, rows, sls[hh]] = dq[hh]
                dqkv_ref[1, rows, sls[hh]] = dk[hh]
                dqkv_ref[2, rows, sls[hh]] = dv[hh]
                dgr = dgr + jnp.where(lane == hh, jnp.sum(db[hh], axis=1, keepdims=True), 0.0)
                dgr = dgr + jnp.where(lane == H + hh, jnp.sum(dg[hh], axis=1, keepdims=True), 0.0)
            dg_ref[rows, :] = dgr
        for hh in range(HB):
            ds_scr[hh] = dSt[hh]

    rev = lambda n: NB - 1 - n
    part = lambda p: pl.BlockSpec((None, CPS * C, HB * HEAD), lambda hb, n: (p, rev(n), hb))
    gate = pl.BlockSpec((CPS * C, HEAD), lambda hb, n: (rev(n), 0))
    return hosted_call(
        body, comm, _grid_steps(H // HB, NB), name=name, grid=(H // HB, NB),
        in_specs=[part(0), part(1), part(2), gate,
                  pl.BlockSpec((CPS, HB, HEAD, HEAD), lambda hb, n: (rev(n), hb, 0, 0)),
                  pl.BlockSpec((CPS, HB, C, C), lambda hb, n: (rev(n), hb, 0, 0)),
                  pl.BlockSpec((CPS * C, HB * HEAD), lambda hb, n: (rev(n), hb))],
        out_specs=(pl.BlockSpec((3, CPS * C, HB * HEAD), lambda hb, n: (0, rev(n), hb)), gate),
        out_shape=(SDS((3, S, W), f32), SDS((S, HEAD), f32)),
        scratch_shapes=[pltpu.VMEM((HB, HEAD, HEAD), f32)], args=(qkv, qkv, qkv, gates, states, kept_T, do))


def _gate_norm(o, z, ng):
    return _rms(o, ng) * _silu(z)


def dn_out(o, proj, ng, wout, x1, g3, name):
    S, W = o.shape
    D = x1.shape[1]
    nh = W // HEAD
    tm = _row_tile(S, MIX_ROWS)

    def body(o_ref, z_ref, ng_ref, w_ref, x_ref, g_ref, xo_ref, m_ref, og_ref):
        for h in range(nh):
            sl = slice(h * HEAD, (h + 1) * HEAD)
            og_ref[:, sl] = _gate_norm(o_ref[:, sl], z_ref[:, sl], ng_ref[...]).astype(og_ref.dtype)
        m = _mm(og_ref[...], w_ref[...])
        m_ref[...] = m
        xo_ref[...] = x_ref[...] + _rms(m, g_ref[...])

    rw = pl.BlockSpec((tm, W), lambda i: (i, 0))
    rd = pl.BlockSpec((tm, D), lambda i: (i, 0))
    return pl.pallas_call(
        body, name=name, grid=(S // tm,),
        in_specs=[rw, pl.BlockSpec((tm, W), lambda i: (i, 3)), pl.BlockSpec((1, HEAD), lambda i: (0, 0)),
                  pl.BlockSpec((W, D), lambda i: (0, 0)), rd, pl.BlockSpec((1, D), lambda i: (0, 0))],
        out_specs=(rd, rd, rw),
        out_shape=(SDS((S, D), f32), SDS((S, D), f32), SDS((S, W), MXU_DTYPE)), compiler_params=_params(1),
    )(o, proj, ng, wout, x1, g3)


def dn_out_bwd(dxo, m, g3, o, proj, ng, wout, name):
    S, W = o.shape
    D = m.shape[1]
    nh = W // HEAD
    tm = _row_tile(S, MIX_ROWS)

    def body(dxo_ref, m_ref, g_ref, o_ref, z_ref, ng_ref, w_ref, dm_ref, do_ref, dz_ref, dng_ref, dg_ref):
        @pl.when(pl.program_id(0) == 0)
        def _():
            dng_ref[...] = jnp.zeros_like(dng_ref)
            dg_ref[...] = jnp.zeros_like(dg_ref)

        dm, dg = _rms_bwd(m_ref[...], g_ref[...], dxo_ref[...])
        dg_ref[...] += dg
        dmc = dm.astype(dm_ref.dtype)
        dm_ref[...] = dmc
        dog = _mm_nt(dmc, w_ref[...])
        for h in range(nh):
            sl = slice(h * HEAD, (h + 1) * HEAD)
            _, vjp = jax.vjp(_gate_norm, o_ref[:, sl], z_ref[:, sl], ng_ref[...])
            do, dz, dng = vjp(dog[:, sl])
            do_ref[:, sl] = do
            dz_ref[:, sl] = dz.astype(dz_ref.dtype)
            dng_ref[...] += dng

    rw = pl.BlockSpec((tm, W), lambda i: (i, 0))
    rd = pl.BlockSpec((tm, D), lambda i: (i, 0))
    vd = pl.BlockSpec((1, D), lambda i: (0, 0))
    vh = pl.BlockSpec((1, HEAD), lambda i: (0, 0))
    return pl.pallas_call(
        body, name=name, grid=(S // tm,),
        in_specs=[rd, rd, vd, rw, pl.BlockSpec((tm, W), lambda i: (i, 3)), vh, pl.BlockSpec((W, D), lambda i: (0, 0))],
        out_specs=(rd, rw, rw, vh, vd),
        out_shape=(SDS((S, D), MXU_DTYPE), SDS((S, W), f32), SDS((S, W), MXU_DTYPE), SDS((1, HEAD), f32),
                   SDS((1, D), f32)),
        compiler_params=_params(1),
    )(dxo, m, g3, o, proj, ng, wout)


def _erf_arg(x):
    return lax.erf(x * 0.7071067811865476)


@jax.custom_vjp
def _gelu_with_erf(x, e):
    return 0.5 * x * (1.0 + e)


def _gelu_with_erf_bwd(res, g):
    x, e = res
    return g * (0.5 * (1.0 + e) + x * (jnp.exp(-0.5 * x * x) * 0.3989422804014327)), jnp.zeros_like(e)


_gelu_with_erf.defvjp(lambda x, e: (0.5 * x * (1.0 + e), (x, e)), _gelu_with_erf_bwd)


def _layernorm(t, lg, lb):
    tc = t - jnp.mean(t, axis=-1, keepdims=True)
    return tc * lax.rsqrt(jnp.mean(tc * tc, axis=-1, keepdims=True) + LN_EPS) * lg + lb


def _sg_stage1_kept(eu, ev, pu, pv, bu, bv, lg, lb):
    return _gelu_with_erf(pu + bu, eu), _layernorm(_gelu_with_erf(pv + bv, ev), lg, lb)


def _causal_mask(n):
    return lax.broadcasted_iota(jnp.int32, (n, n), 0) >= lax.broadcasted_iota(jnp.int32, (n, n), 1)


def sg_mid(pre, b_in, ln_g, ln_b, w_s, bsT, wout, x1, g3, name):
    S = pre.shape[0]
    E, D = ln_g.shape[1], x1.shape[1]
    G, CH = SG_GROUPS, SG_CHUNK
    Cg = E // G
    tm = _row_tile(S, MIX_ROWS)

    def body(pu_ref, pv_ref, bu_ref, bv_ref, lg_ref, lb_ref, ws_ref, bs_ref, w_ref, x_ref, g_ref,
             xo_ref, m_ref, gt_ref, e_ref):
        xu, xv = pu_ref[...] + bu_ref[...], pv_ref[...] + bv_ref[...]
        eu, ev = _erf_arg(xu), _erf_arg(xv)
        e_ref[:, :E] = eu.astype(e_ref.dtype)
        e_ref[:, E:] = ev.astype(e_ref.dtype)
        u = 0.5 * xu * (1.0 + eu)
        v = _layernorm(0.5 * xv * (1.0 + ev), lg_ref[...], lb_ref[...])
        mask = _causal_mask(CH)
        for g in range(G):
            wc = _c(jnp.where(mask, ws_ref[g], 0.0))
            bcol = bs_ref[:, g:g + 1]
            cs = slice(g * Cg, (g + 1) * Cg)
            for ch in range(tm // CH):
                rs = slice(ch * CH, (ch + 1) * CH)
                mixed = _mm(wc, _c(v[rs, cs])) + bcol
                gt_ref[rs, cs] = (u[rs, cs] * mixed).astype(gt_ref.dtype)
        m = _mm(gt_ref[...], w_ref[...])
        m_ref[...] = m
        xo_ref[...] = x_ref[...] + _rms(m, g_ref[...])

    half = lambda p: pl.BlockSpec((tm, E), lambda i: (i, p))
    vhalf = lambda p: pl.BlockSpec((1, E), lambda i: (0, p))
    ve = pl.BlockSpec((1, E), lambda i: (0, 0))
    rd = pl.BlockSpec((tm, D), lambda i: (i, 0))
    return pl.pallas_call(
        body, name=name, grid=(S // tm,),
        in_specs=[half(0), half(1), vhalf(0), vhalf(1), ve, ve, pl.BlockSpec((G, CH, CH), lambda i: (0, 0, 0)),
                  pl.BlockSpec((CH, G), lambda i: (0, 0)), pl.BlockSpec((E, D), lambda i: (0, 0)), rd,
                  pl.BlockSpec((1, D), lambda i: (0, 0))],
        out_specs=(rd, rd, pl.BlockSpec((tm, E), lambda i: (i, 0)), pl.BlockSpec((tm, 2 * E), lambda i: (i, 0))),
        out_shape=(SDS((S, D), f32), SDS((S, D), f32), SDS((S, E), MXU_DTYPE), SDS((S, 2 * E), MXU_DTYPE)),
        compiler_params=_params(1),
    )(pre, pre, b_in, b_in, ln_g, ln_b, w_s, bsT, wout, x1, g3)


def sg_mid_bwd(dxo, m, g3, pre, kept_erf, b_in, ln_g, ln_b, w_s, bsT, wout, name):
    S = pre.shape[0]
    E, D = ln_g.shape[1], m.shape[1]
    G, CH = SG_GROUPS, SG_CHUNK
    Cg = E // G
    tm = _row_tile(S, SG_BWD_ROWS)

    def body(dxo_ref, m_ref, g_ref, pu_ref, pv_ref, eu_ref, ev_ref, bu_ref, bv_ref, lg_ref, lb_ref, ws_ref, bs_ref,
             w_ref, dm_ref, dpre_ref, dbin_ref, dlg_ref, dlb_ref, dws_ref, dbs_ref, dg_ref, du_scr, dv_scr):
        @pl.when(pl.program_id(0) == 0)
        def _():
            for r in (dbin_ref, dlg_ref, dlb_ref, dws_ref, dbs_ref, dg_ref):
                r[...] = jnp.zeros_like(r)

        dm, dg = _rms_bwd(m_ref[...], g_ref[...], dxo_ref[...])
        dg_ref[...] += dg
        dmc = dm.astype(dm_ref.dtype)
        dm_ref[...] = dmc
        dgated = _mm_nt(dmc, w_ref[...])
        stage1 = functools.partial(_sg_stage1_kept, eu_ref[...].astype(f32), ev_ref[...].astype(f32))
        (u, v), vjp1 = jax.vjp(stage1, pu_ref[...], pv_ref[...], bu_ref[...], bv_ref[...], lg_ref[...], lb_ref[...])
        mask = _causal_mask(CH)
        lane = lax.broadcasted_iota(jnp.int32, (CH, CH), 1)
        for g in range(G):
            wc = _c(jnp.where(mask, ws_ref[g], 0.0))
            bcol = bs_ref[:, g:g + 1]
            cs = slice(g * Cg, (g + 1) * Cg)
            dws = jnp.zeros((CH, CH), f32)
            dbs = jnp.zeros((CH, 1), f32)
            for ch in range(tm // CH):
                rs = slice(ch * CH, (ch + 1) * CH)
                vs = _c(v[rs, cs])
                mixed = _mm(wc, vs) + bcol
                dgt = dgated[rs, cs]
                du_scr[rs, cs] = dgt * mixed
                dmixed = dgt * u[rs, cs]
                dmc2 = _c(dmixed)
                dv_scr[rs, cs] = _mm_tn(wc, dmc2)
                dws = dws + _mm_nt(dmc2, vs)
                dbs = dbs + jnp.sum(dmixed, axis=1, keepdims=True)
            dws_ref[g] += jnp.where(mask, dws, 0.0)
            dbs_ref[...] += jnp.where(lane == g, jnp.broadcast_to(dbs, (CH, CH)), 0.0)
        dpu, dpv, dbu, dbv, dlg, dlb = vjp1((du_scr[...], dv_scr[...]))
        dpre_ref[:, :E] = dpu.astype(dpre_ref.dtype)
        dpre_ref[:, E:] = dpv.astype(dpre_ref.dtype)
        dbin_ref[:, :E] += dbu
        dbin_ref[:, E:] += dbv
        dlg_ref[...] += dlg
        dlb_ref[...] += dlb

    half = lambda p: pl.BlockSpec((tm, E), lambda i: (i, p))
    vhalf = lambda p: pl.BlockSpec((1, E), lambda i: (0, p))
    ve = pl.BlockSpec((1, E), lambda i: (0, 0))
    rd = pl.BlockSpec((tm, D), lambda i: (i, 0))
    vd = pl.BlockSpec((1, D), lambda i: (0, 0))
    wsb = pl.BlockSpec((G, CH, CH), lambda i: (0, 0, 0))
    return pl.pallas_call(
        body, name=name, grid=(S // tm,),
        in_specs=[rd, rd, vd, half(0), half(1), half(0), half(1), vhalf(0), vhalf(1), ve, ve, wsb,
                  pl.BlockSpec((CH, G), lambda i: (0, 0)), pl.BlockSpec((E, D), lambda i: (0, 0))],
        out_specs=(rd, pl.BlockSpec((tm, 2 * E), lambda i: (i, 0)), pl.BlockSpec((1, 2 * E), lambda i: (0, 0)), ve, ve,
                   wsb, pl.BlockSpec((CH, CH), lambda i: (0, 0)), vd),
        out_shape=(SDS((S, D), MXU_DTYPE), SDS((S, 2 * E), MXU_DTYPE), SDS((1, 2 * E), f32), SDS((1, E), f32),
                   SDS((1, E), f32), SDS((G, CH, CH), f32), SDS((CH, CH), f32), SDS((1, D), f32)),
        scratch_shapes=[pltpu.VMEM((tm, E), f32), pltpu.VMEM((tm, E), f32)], compiler_params=_params(1),
    )(dxo, m, g3, pre, pre, kept_erf, kept_erf, b_in, b_in, ln_g, ln_b, w_s, bsT, wout)


def loss_head(y, target, name):
    S, D = y.shape
    tm = _row_tile(S, 512)

    def body(y_ref, t_ref, l_ref, d_ref):
        @pl.when(pl.program_id(0) == 0)
        def _():
            l_ref[...] = jnp.zeros_like(l_ref)

        e = y_ref[...] - t_ref[...]
        d_ref[...] = e * (1.0 / D)
        l_ref[...] += jnp.sum(e * e) * (0.5 / D)

    row = pl.BlockSpec((tm, D), lambda i: (i, 0))
    return pl.pallas_call(
        body, name=name, grid=(S // tm,), in_specs=[row, row],
        out_specs=(pl.BlockSpec((1, HEAD), lambda i: (0, 0)), row),
        out_shape=(SDS((1, HEAD), f32), SDS((S, D), f32)), compiler_params=_params(1),
    )(y, target)


def sum_slots(r, name):
    _, R, C = r.shape
    tr = R // 2 if R % 16 == 0 else R

    def body(r_ref, o_ref):
        acc = r_ref[0].astype(f32)
        for s in range(1, N_DEV):
            acc = acc + r_ref[s].astype(f32)
        o_ref[...] = acc

    return pl.pallas_call(
        body, name=name, grid=(R // tr,), in_specs=[pl.BlockSpec((N_DEV, tr, C), lambda i: (0, i, 0))],
        out_specs=pl.BlockSpec((tr, C), lambda i: (i, 0)), out_shape=SDS((R, C), f32), compiler_params=_params(1),
    )(r)


def _adam_math(w, g, m, v):
    m = ADAM_B1 * m + (1.0 - ADAM_B1) * g
    v = ADAM_B2 * v + (1.0 - ADAM_B2) * (g * g)
    m_hat = m / (1.0 - ADAM_B1 ** ADAM_STEP)
    v_hat = v / (1.0 - ADAM_B2 ** ADAM_STEP)
    delta = -ADAM_LR * (m_hat / (jnp.sqrt(v_hat) + ADAM_EPS) + ADAM_WD * w)
    return delta, m, v


def adam_slots(w, rs, m, v, name, tr):
    R, C = w.shape
    tr = _row_tile(min(r.shape[1] for r in rs), tr)
    blocks = [r.shape[1] // tr for r in rs]
    starts = [sum(blocks[:k]) for k in range(len(rs))]
    assert sum(blocks) * tr == R

    def body(w_ref, *refs):
        r_refs, (m_ref, v_ref, g_ref, d_ref, mo_ref, vo_ref) = refs[:len(rs)], refs[len(rs):]
        i = pl.program_id(0)
        for k, r_ref in enumerate(r_refs):
            @pl.when((i >= starts[k]) & (i < starts[k] + blocks[k]))
            def _():
                g = r_ref[0].astype(f32)
                for s in range(1, N_DEV):
                    g = g + r_ref[s].astype(f32)
                g_ref[...] = g

        d_ref[...], mo_ref[...], vo_ref[...] = _adam_math(w_ref[...], g_ref[...], m_ref[...], v_ref[...])

    row = pl.BlockSpec((tr, C), lambda i: (i, 0))
    piece = lambda k: pl.BlockSpec((N_DEV, tr, C), lambda i: (0, jnp.clip(i - starts[k], 0, blocks[k] - 1), 0))
    return pl.pallas_call(
        body, name=name, grid=(R // tr,), in_specs=[row] + [piece(k) for k in range(len(rs))] + [row, row],
        out_specs=(row, row, row, row), out_shape=tuple(SDS((R, C), f32) for _ in range(4)),
        compiler_params=_params(1),
    )(w, *rs, m, v)


def adam_small(w, g, m, v, name):
    def body(w_ref, g_ref, m_ref, v_ref, d_ref, mo_ref, vo_ref):
        d_ref[...], mo_ref[...], vo_ref[...] = _adam_math(w_ref[...], g_ref[...], m_ref[...], v_ref[...])

    return pl.pallas_call(body, name=name, out_shape=tuple(SDS(w.shape, f32) for _ in range(3)))(w, g, m, v)


def _pack_rows(parts):
    rows, offs, r = [], [], 0
    for p in parts:
        flat = p.reshape(-1)
        n = -(-flat.shape[0] // HEAD)
        flat = jnp.pad(flat, (0, n * HEAD - flat.shape[0]))
        rows.append(flat.reshape(n, HEAD))
        offs.append((r, n))
        r += n
    pad = (-r) % 8
    if pad:
        rows.append(jnp.zeros((pad, HEAD), f32))
    return jnp.concatenate(rows, axis=0), offs


def kernel(x, norm_g, ffn_w_gate, ffn_w_up, ffn_w_down, dn_w_in, dn_conv_w, dn_a_log, dn_dt_bias, dn_norm_g, dn_w_out, sg_w_in, sg_b_in, sg_ln_g, sg_ln_b, sg_w_s, sg_b_s, sg_w_out, loss_target, m_norm_g, m_ffn_w_gate, m_ffn_w_up, m_ffn_w_down, m_dn_w_in, m_dn_conv_w, m_dn_a_log, m_dn_dt_bias, m_dn_norm_g, m_dn_w_out, m_sg_w_in, m_sg_b_in, m_sg_ln_g, m_sg_ln_b, m_sg_w_s, m_sg_b_s, m_sg_w_out, v_norm_g, v_ffn_w_gate, v_ffn_w_up, v_ffn_w_down, v_dn_w_in, v_dn_conv_w, v_dn_a_log, v_dn_dt_bias, v_dn_norm_g, v_dn_w_out, v_sg_w_in, v_sg_b_in, v_sg_ln_g, v_sg_ln_b, v_sg_w_s, v_sg_b_s, v_sg_w_out):
    weights = dict(norm_g=norm_g, ffn_w_gate=ffn_w_gate, ffn_w_up=ffn_w_up, ffn_w_down=ffn_w_down, dn_w_in=dn_w_in,
                   dn_conv_w=dn_conv_w, dn_a_log=dn_a_log, dn_dt_bias=dn_dt_bias, dn_norm_g=dn_norm_g,
                   dn_w_out=dn_w_out, sg_w_in=sg_w_in, sg_b_in=sg_b_in, sg_ln_g=sg_ln_g, sg_ln_b=sg_ln_b,
                   sg_w_s=sg_w_s, sg_b_s=sg_b_s, sg_w_out=sg_w_out)
    mom_m = dict(norm_g=m_norm_g, ffn_w_gate=m_ffn_w_gate, ffn_w_up=m_ffn_w_up, ffn_w_down=m_ffn_w_down,
                 dn_w_in=m_dn_w_in, dn_conv_w=m_dn_conv_w, dn_a_log=m_dn_a_log, dn_dt_bias=m_dn_dt_bias,
                 dn_norm_g=m_dn_norm_g, dn_w_out=m_dn_w_out, sg_w_in=m_sg_w_in, sg_b_in=m_sg_b_in,
                 sg_ln_g=m_sg_ln_g, sg_ln_b=m_sg_ln_b, sg_w_s=m_sg_w_s, sg_b_s=m_sg_b_s, sg_w_out=m_sg_w_out)
    mom_v = dict(norm_g=v_norm_g, ffn_w_gate=v_ffn_w_gate, ffn_w_up=v_ffn_w_up, ffn_w_down=v_ffn_w_down,
                 dn_w_in=v_dn_w_in, dn_conv_w=v_dn_conv_w, dn_a_log=v_dn_a_log, dn_dt_bias=v_dn_dt_bias,
                 dn_norm_g=v_dn_norm_g, dn_w_out=v_dn_w_out, sg_w_in=v_sg_w_in, sg_b_in=v_sg_b_in,
                 sg_ln_g=v_sg_ln_g, sg_ln_b=v_sg_ln_b, sg_w_s=v_sg_w_s, sg_b_s=v_sg_b_s, sg_w_out=v_sg_w_out)
    order = list(weights)

    xs = x[0]
    S, D = xs.shape
    F8 = ffn_w_gate.shape[-1]
    depth = norm_g.shape[0]
    W = dn_w_out.shape[1] * N_DEV
    H = W // HEAD
    E = sg_ln_g.shape[1] * N_DEV
    G, CH = sg_w_s.shape[1], sg_w_s.shape[2]
    c8 = dn_w_in.shape[2]
    me = _slot(lax.axis_index("x"), lax.axis_index("y"), lax.axis_index("c"))

    assert depth == 2
    small_in, small_offs = _pack_rows([norm_g, dn_conv_w, sg_b_in, sg_ln_g, sg_ln_b])
    wg0a, wu0a, wd0a, small_all = all_gather_multi(
        [_c(ffn_w_gate[0, 0]), _c(ffn_w_up[0, 0]), _c(ffn_w_down[0, 0]), small_in], name="gather_first")
    ffn_shards = lambda l, ab: [_c(ffn_w_gate[l, ab]), _c(ffn_w_up[l, ab]), _c(ffn_w_down[l, ab])]
    gather_dn = Comm("gather", [_c(dn_w_in[0]), _c(dn_w_out[0])])
    gather_mid = Comm("gather", ffn_shards(0, 1) + ffn_shards(1, 0))
    gather_end = Comm("gather", ffn_shards(1, 1))
    gather_sg = Comm("gather", [_c(sg_w_in[0]), _c(sg_w_out[0])])
    per = N_DEV // FFN_SLABS
    wide = lambda tag, g, u, d: (*widen_slabs([g, u], FFN_SLABS, name=f"widen_{tag}"),
                                 d.reshape(FFN_SLABS, per * F8, D))
    ffn_w = {(0, 0): wide("0a", wg0a, wu0a, wd0a)}

    def small_piece(i, shard_shape):
        r0, n = small_offs[i]
        sz = math.prod(shard_shape)
        return small_all[:, r0:r0 + n, :].reshape(N_DEV, n * HEAD)[:, :sz].reshape((N_DEV,) + tuple(shard_shape))

    ng_full = jnp.moveaxis(small_piece(0, norm_g.shape), 0, 2).reshape(depth, 6, D)
    conv_full = jnp.moveaxis(small_piece(1, dn_conv_w.shape[1:]), 0, 1).reshape(CONV_K, 3 * W)
    bin_full = small_piece(2, sg_b_in.shape[1:]).reshape(1, 2 * E)
    lng_full = small_piece(3, sg_ln_g.shape[1:]).reshape(1, E)
    lnb_full = small_piece(4, sg_ln_b.shape[1:]).reshape(1, E)
    gate_lanes = lambda v: jnp.pad(v.reshape(1, H), ((0, 0), (H, HEAD - 2 * H)))
    al_row, dt_row = gate_lanes(dn_a_log), gate_lanes(dn_dt_bias)
    bsT = sg_b_s[0].T
    gvec = lambda l, k: ng_full[l, k].reshape(1, D)

    saved = []
    cur = xs
    for l in range(depth):
        sv = {}
        sv['x0'] = cur
        (cur, sv['hA'], sv['pA'], sv['qA'], sv['tA'], sv['yA']), got = ffn_fwd(
            cur, gvec(l, 0), gvec(l, 1), *ffn_w[l, 0], name=f"ffn_fwd_{l}a", comm=gather_dn if l == 0 else gather_sg)
        sv['x1'] = cur
        if l == 1:
            sg_win = jnp.moveaxis(got[0], 0, 1).reshape(D, 2 * E)
            sg_wout = got[1].reshape(E, D)
        if l == 0:
            dnin_all, dnout_all = got
            dn_wmain, dn_wba = join_columns(dnin_all, 4 * W, name="dn_w_in_join")
            dn_wout = dnout_all.reshape(W, D)
            sv['hM'], sv['proj'], sv['pba'] = rms_mm(cur, gvec(l, 2), dn_wmain, dn_wba, name=f"dn_in_{l}")
            sv['qkv'] = dn_prep(sv['proj'], conv_full, name=f"dn_prep_{l}")
            sv['gates'] = dn_gates(sv['pba'], al_row, dt_row, H, name=f"dn_gates_{l}")
            (sv['o'], sv['states'], sv['T']), got = dn_chunk_fwd(sv['qkv'], sv['gates'], name=f"dn_chunk_{l}",
                                                                 comm=gather_mid)
            ffn_w[0, 1], ffn_w[1, 0] = wide("0b", *got[0:3]), wide("1a", *got[3:6])
            cur, sv['m'], sv['og'] = dn_out(sv['o'], sv['proj'], dn_norm_g, dn_wout, cur, gvec(l, 3), name=f"dn_out_{l}")
        else:
            sv['hM'], sv['pre'] = rms_mm(cur, gvec(l, 2), sg_win, None, name=f"sg_in_{l}")
            cur, sv['m'], sv['gated'], sv['erf'] = sg_mid(sv['pre'], bin_full, lng_full, lnb_full, sg_w_s[0], bsT, sg_wout,
                                                          cur, gvec(l, 3), name=f"sg_mid_{l}")
        sv['x2'] = cur
        (cur, sv['hB'], sv['pB'], sv['qB'], sv['tB'], sv['yB']), got = ffn_fwd(
            cur, gvec(l, 4), gvec(l, 5), *ffn_w[l, 1], name=f"ffn_fwd_{l}b", comm=gather_end if l == 0 else None)
        if l == 0:
            ffn_w[1, 1] = wide("1b", *got[0:3])
        saved.append(sv)

    loss_blk, dcur = loss_head(cur, loss_target[0], name="loss_head")
    loss = lax.psum(loss_blk[0, 0], ("x", "y", "c"))

    dng = [[None] * 6 for _ in range(depth)]
    ffn_dw = {}
    grads, slots = {}, {}

    def ffn_backward(l, ab, dcur, exchange=None):
        sv, s = saved[l], 'AB'[ab]
        (dcur, da, db, dy, dng[l][4 * ab], dng[l][4 * ab + 1]), got = ffn_bwd_dx(
            dcur, sv['x2' if ab else 'x0'], sv['y' + s], sv['p' + s], sv['q' + s], gvec(l, 4 * ab), gvec(l, 4 * ab + 1),
            *ffn_w[l, ab], name=f"ffn_bwd_{l}{'ab'[ab]}", comm=Comm("exchange", exchange) if exchange else None)
        ffn_dw[l, ab], _ = ffn_bwd_dw(sv['h' + s], dy, sv['t' + s], da, db, name=f"ffn_dw_{l}{'ab'[ab]}")
        return dcur, got

    sv = saved[1]
    dcur, _ = ffn_backward(1, 1, dcur)
    dm, dpre, grads['sg_b_in'], grads['sg_ln_g'], grads['sg_ln_b'], grads['sg_w_s'], dbs, dng[1][3] = sg_mid_bwd(
        dcur, sv['m'], gvec(1, 3), sv['pre'], sv['erf'], bin_full, lng_full, lnb_full, sg_w_s[0], bsT, sg_wout,
        name="sg_mid_bwd_1")
    grads['sg_b_s'] = dbs[:, :G].T
    dsg_wout = tn_mm(sv['gated'], dm, name="sg_wout_dw_1").reshape(N_DEV, E // N_DEV, D)
    dsg_win = tn_mm(sv['hM'], dpre, name="sg_win_dw_1", tn=2 * E // N_DEV, slot_major=True)
    (dcur, dng[1][2]), _ = mm_bwd_dx(dcur, sv['x1'], gvec(1, 2), dpre, sg_win, None, None, name="sg_in_bwd_1")
    dcur, l1b = ffn_backward(1, 0, dcur, exchange=list(ffn_dw[1, 1]))
    sv = saved[0]
    dcur, got = ffn_backward(0, 1, dcur, exchange=[dsg_win, dsg_wout])
    slots['sg_w_in'], slots['sg_w_out'] = [got[0]], [got[1]]
    dm, do, dz, grads['dn_norm_g'], dng[0][3] = dn_out_bwd(dcur, sv['m'], gvec(0, 3), sv['o'], sv['proj'], dn_norm_g,
                                                          dn_wout, name="dn_out_bwd_0")
    ddn_wout = tn_mm(sv['og'], dm, name="dn_wout_dw_0").reshape(N_DEV, W // N_DEV, D)
    (dqkv, dgates), got = dn_chunk_bwd(sv['qkv'], sv['gates'], sv['states'], sv['T'], do, name="dn_chunk_bwd_0",
                                       comm=Comm("exchange", list(ffn_dw[1, 0])))
    l1a = got
    dpba, dal, ddt = dn_gates_bwd(sv['pba'], al_row, dt_row, dgates, H, name="dn_gates_bwd_0")
    grads['dn_a_log'] = dal[:, H:2 * H]
    grads['dn_dt_bias'] = ddt[:, H:2 * H]
    (dproj, grads['dn_conv_w']), got = dn_prep_bwd(sv['proj'], conv_full, dqkv, dz, name="dn_prep_bwd_0",
                                                   comm=Comm("exchange", [*ffn_dw[0, 1], ddn_wout]))
    l0b, slots['dn_w_out'] = got[0:3], [got[3]]
    dw_main = tn_mm(sv['hM'], dproj, name="dn_win_dw_0")
    dw_ba = tn_mm(sv['hM'], dpba, name="dn_wba_dw_0", tn=HEAD)
    ddn_win = split_columns(dw_main, dw_ba, c8, name="dn_w_in_split")
    (dcur, dng[0][2]), got = mm_bwd_dx(dcur, sv['x1'], gvec(0, 2), dproj, dn_wmain, dpba, dn_wba, name="dn_in_bwd_0",
                                       comm=Comm("exchange", [ddn_win]))
    slots['dn_w_in'] = [got[0]]
    small_names = ['norm_g', 'dn_conv_w', 'sg_b_in', 'sg_ln_g', 'sg_ln_b', 'sg_w_s', 'sg_b_s', 'dn_a_log',
                   'dn_dt_bias', 'dn_norm_g']
    small = {}

    def gather_small():
        dng_full = jnp.stack([jnp.concatenate(r, axis=0) for r in dng], axis=0)
        small['parts'] = [dng_full, grads['dn_conv_w'], grads['sg_b_in'], grads['sg_ln_g'], grads['sg_ln_b'],
                          grads['sg_w_s'], grads['sg_b_s'], grads['dn_a_log'], grads['dn_dt_bias'], grads['dn_norm_g']]
        pack, small['offs'] = _pack_rows(small['parts'])
        return Comm("gather", [pack])

    (dcur, da, db, dy, dng[0][0], dng[0][1]), _ = ffn_bwd_dx(
        dcur, sv['x0'], sv['yA'], sv['pA'], sv['qA'], gvec(0, 0), gvec(0, 1), *ffn_w[0, 0], name="ffn_bwd_0a")
    grad_x = dcur[None]
    (dg,), (small_slots,) = ffn_bwd_dw_one(sv['hA'], da, False, name="ffn_dw_0a_gate", comm=gather_small())
    (du,), (xg,) = ffn_bwd_dw_one(sv['hA'], db, False, name="ffn_dw_0a_up", comm=Comm("exchange", [dg]))
    (dd,), (xu,) = ffn_bwd_dw_one(dy, sv['tA'], True, name="ffn_dw_0a_down", comm=Comm("exchange", [du]))
    small_parts, offs = small['parts'], small['offs']
    l0a = [xg, xu, exchange_slots([dd], name="exchange_last")[0]]
    for i, nm in enumerate(['ffn_w_gate', 'ffn_w_up', 'ffn_w_down']):
        slots[nm] = [l0a[i], l0b[i], l1a[i], l1b[i]]
    big_names = ['ffn_w_gate', 'ffn_w_up', 'ffn_w_down', 'dn_w_in', 'dn_w_out', 'sg_w_in', 'sg_w_out']
    slots = [slots[nm] for nm in big_names]
    small_sum = sum_slots(small_slots, name="sum_small_grads")

    def small_grad(i):
        r0, n = offs[i]
        p = small_parts[i]
        return small_sum[r0:r0 + n].reshape(-1)[:p.size].reshape(p.shape)

    def my_shard(full, axis, like):
        n = full.shape[axis] // N_DEV
        return lax.dynamic_slice_in_dim(full, me * n, n, axis).reshape(like.shape)

    g_small = {
        'norm_g': my_shard(small_grad(0), 2, norm_g),
        'dn_conv_w': my_shard(small_grad(1), 1, dn_conv_w),
        'sg_b_in': my_shard(small_grad(2), 1, sg_b_in),
        'sg_ln_g': my_shard(small_grad(3), 1, sg_ln_g),
        'sg_ln_b': my_shard(small_grad(4), 1, sg_ln_b),
        'sg_w_s': small_grad(5).reshape(sg_w_s.shape),
        'sg_b_s': small_grad(6).reshape(sg_b_s.shape),
        'dn_a_log': small_grad(7).reshape(dn_a_log.shape),
        'dn_dt_bias': small_grad(8).reshape(dn_dt_bias.shape),
        'dn_norm_g': small_grad(9).reshape(dn_norm_g.shape),
    }

    out_g, out_d, out_m, out_v = {}, {}, {}, {}
    for nm, r in zip(big_names, slots):
        w = weights[nm]
        cols = w.shape[-1]
        rows = w.size // cols
        tr = {'ffn_w_gate': 512, 'ffn_w_up': 512, 'ffn_w_down': F8 // 2, 'dn_w_in': 256, 'sg_w_in': 256}.get(nm, rows)
        pieces = [p.reshape(N_DEV, -1, cols) for p in r]
        g, d, m2, v2 = adam_slots(w.reshape(rows, cols), pieces, mom_m[nm].reshape(rows, cols),
                                  mom_v[nm].reshape(rows, cols), name=f"adam_{nm}", tr=tr)
        out_g[nm], out_d[nm], out_m[nm], out_v[nm] = (t.reshape(w.shape) for t in (g, d, m2, v2))
    for nm in small_names:
        w = weights[nm]
        cols = w.shape[-1]
        rows = w.size // cols
        two = lambda t: t.reshape(rows, cols)
        d, m2, v2 = adam_small(two(w), two(g_small[nm]), two(mom_m[nm]), two(mom_v[nm]), name=f"adam_{nm}")
        out_g[nm] = g_small[nm]
        out_d[nm], out_m[nm], out_v[nm] = (t.reshape(w.shape) for t in (d, m2, v2))

    return (loss, grad_x, *[out_g[n] for n in order], *[out_d[n] for n in order], *[out_m[n] for n in order],
            *[out_v[n] for n in order])
```

```python
import functools
import math

import jax
import jax.numpy as jnp
from jax import lax
from jax.experimental import pallas as pl
from jax.experimental.pallas import tpu as pltpu

f32 = jnp.float32
MXU_DTYPE = jnp.bfloat16
N_DEV = 8
RMS_EPS = 1e-6
LN_EPS = 1e-5
L2_EPS = 1e-6
HEAD = 128
DN_CHUNK = 64
SG_CHUNK = 128
SG_GROUPS = 8
CONV_K = 4
ADAM_LR, ADAM_B1, ADAM_B2, ADAM_EPS, ADAM_WD, ADAM_STEP = 0.001, 0.9, 0.999, 1e-08, 0.01, 10
VMEM_LIMIT = 56 * 1024 * 1024
FFN_ROWS_FWD, FFN_ROWS_BWD, FFN_ROWS_DW = 1024, 512, 2048
MIX_ROWS, SG_BWD_ROWS = 512, 256
PROJ_ROWS, TN_ROWS = 1024, 2048
FFN_SLABS = 4
SDS = jax.ShapeDtypeStruct
MESH = pl.DeviceIdType.MESH


def _params(n_grid):
    return pltpu.CompilerParams(dimension_semantics=("arbitrary",) * n_grid, vmem_limit_bytes=VMEM_LIMIT)


def _row_tile(s, want):
    t = min(s, want)
    assert s % t == 0, (s, t)
    return t


def _rms(x, g):
    return x * lax.rsqrt(jnp.mean(x * x, axis=-1, keepdims=True) + RMS_EPS) * g


def _rms_bwd(x, g, dy):
    r = lax.rsqrt(jnp.mean(x * x, axis=-1, keepdims=True) + RMS_EPS)
    t = dy * g
    dx = t * r - x * (jnp.mean(x * t, axis=-1, keepdims=True) * (r * r * r))
    return dx, jnp.sum(dy * (x * r), axis=0, keepdims=True)


def _silu(a):
    return a * jax.nn.sigmoid(a)


def _mm(a, b):
    return lax.dot_general(a, b, (((1,), (0,)), ((), ())), preferred_element_type=f32)


def _mm_nt(a, b):
    return lax.dot_general(a, b, (((1,), (1,)), ((), ())), preferred_element_type=f32)


def _mm_tn(a, b):
    return lax.dot_general(a, b, (((0,), (0,)), ((), ())), preferred_element_type=f32)


def _c(x):
    return x.astype(MXU_DTYPE)


def _split(a):
    hi = a.astype(MXU_DTYPE)
    lo = (a - hi.astype(f32)).astype(MXU_DTYPE)
    return hi, lo


def _dot3(a, b, dims):
    ah, al = _split(a)
    bh, bl = _split(b)
    d = lambda p, q: lax.dot_general(p, q, (dims, ((), ())), preferred_element_type=f32)
    return d(ah, bh) + (d(ah, bl) + d(al, bh))


NN, NT, TN = ((1,), (0,)), ((1,), (1,)), ((0,), (0,))


def _slot(px, py, pc):
    return 4 * px + 2 * py + pc


def all_gather_multi(arrs, name):
    return Comm("gather", arrs).alone(name)


def exchange_slots(arrs, name):
    return Comm("exchange", arrs).alone(name)


class Comm:
    def __init__(self, kind, arrs):
        self.kind, self.arrs, self.n = kind, list(arrs), len(arrs)
        hbm = pl.BlockSpec(memory_space=pltpu.HBM)
        self.in_specs = [hbm] * self.n
        self.out_specs = [hbm] * self.n
        lead = (N_DEV,) if kind == "gather" else ()
        self.out_shape = [SDS(lead + tuple(a.shape), a.dtype) for a in self.arrs]
        self.scratch = [pltpu.SemaphoreType.DMA((self.n, 7)), pltpu.SemaphoreType.DMA((self.n, 7)),
                        pltpu.SemaphoreType.DMA((self.n,))]

    def phase(self, p, ins, outs, sems):
        (self._gather if self.kind == "gather" else self._exchange)(p, ins, outs, sems)

    def _gather(self, p, ins, outs, sems):
        send_sems, recv_sems, local_sems = sems
        x, y, c = lax.axis_index("x"), lax.axis_index("y"), lax.axis_index("c")
        me, sibling = (x, y, c), (x, y, 1 - c)
        chips = [(1 - x, y), (x, 1 - y), (1 - x, 1 - y)]

        def copy(a, k, block, to, src=None):
            dst = outs[a].at[_slot(*block)]
            return pltpu.make_async_remote_copy(
                src_ref=dst if src is None else src, dst_ref=dst, send_sem=send_sems.at[a, k],
                recv_sem=recv_sems.at[a, k], device_id=to, device_id_type=MESH)

        mine = [pltpu.make_async_copy(ins[a], outs[a].at[_slot(*me)], local_sems.at[a]) for a in range(self.n)]
        first = [[copy(a, 0, me, sibling, src=ins[a])] +
                 [copy(a, 1 + j, me, (*chip, c), src=ins[a]) for j, chip in enumerate(chips)] for a in range(self.n)]
        passed = [[copy(a, 4 + j, (*chip, c), sibling) for j, chip in enumerate(chips)] for a in range(self.n)]
        if p == 0:
            for a in range(self.n):
                mine[a].start()
            for a in range(self.n):
                for cp in first[a]:
                    cp.start()
        elif p == 1:
            for a in range(self.n):
                for j, chip in enumerate(chips):
                    copy(a, 1 + j, (*chip, c), me).wait_recv()
                    passed[a][j].start()
        else:
            for a in range(self.n):
                copy(a, 0, sibling, me).wait_recv()
                for j, chip in enumerate(chips):
                    copy(a, 4 + j, (*chip, 1 - c), me).wait_recv()
            for a in range(self.n):
                for cp in first[a] + passed[a]:
                    cp.wait_send()
                mine[a].wait()

    def _exchange(self, p, ins, outs, sems):
        send_sems, recv_sems, local_sems = sems
        x, y, c = lax.axis_index("x"), lax.axis_index("y"), lax.axis_index("c")
        me = _slot(x, y, c)
        peers = [(x ^ (k >> 2), y ^ ((k >> 1) & 1), c ^ (k & 1)) for k in range(1, N_DEV)]

        def copy(a, k):
            peer = peers[k - 1]
            return pltpu.make_async_remote_copy(
                src_ref=ins[a].at[_slot(*peer)], dst_ref=outs[a].at[me], send_sem=send_sems.at[a, k - 1],
                recv_sem=recv_sems.at[a, k - 1], device_id=peer, device_id_type=MESH)

        def landed(a, k):
            peer = peers[k - 1]
            return pltpu.make_async_remote_copy(
                src_ref=ins[a].at[me], dst_ref=outs[a].at[_slot(*peer)], send_sem=send_sems.at[a, k - 1],
                recv_sem=recv_sems.at[a, k - 1], device_id=peer, device_id_type=MESH)

        local = [pltpu.make_async_copy(ins[a].at[me], outs[a].at[me], local_sems.at[a]) for a in range(self.n)]
        order = [6, 7, 2, 3, 4, 5, 1]
        if p == 0:
            for a in range(self.n):
                local[a].start()
            for a in range(self.n):
                for k in order:
                    copy(a, k).start()
        elif p == 2:
            for a in range(self.n):
                for k in order:
                    copy(a, k).wait_send()
                    landed(a, k).wait_recv()
                local[a].wait()

    def alone(self, name):
        n = self.n

        def body(*refs):
            for p in range(3):
                self.phase(p, refs[:n], refs[n:2 * n], refs[2 * n:])

        return pl.pallas_call(body, name=name, out_shape=tuple(self.out_shape), in_specs=self.in_specs,
                              out_specs=tuple(self.out_specs), scratch_shapes=self.scratch)(*self.arrs)


def hosted_call(body, comm, steps, *, name, grid, in_specs, out_specs, out_shape, scratch_shapes, args):
    if comm is None:
        outs = pl.pallas_call(body, name=name, grid=grid, in_specs=in_specs, out_specs=tuple(out_specs),
                              out_shape=tuple(out_shape), scratch_shapes=scratch_shapes,
                              compiler_params=_params(len(grid)))(*args)
        return outs, None
    ni, no, ns, cn = len(in_specs), len(out_specs), len(scratch_shapes), comm.n

    def both(*refs):
        h_in, c_in = refs[:ni], refs[ni:ni + cn]
        h_out, c_out = refs[ni + cn:ni + cn + no], refs[ni + cn + no:ni + 2 * cn + no]
        h_scr, c_scr = refs[ni + 2 * cn + no:ni + 2 * cn + no + ns], refs[ni + 2 * cn + no + ns:]
        when = steps()
        pl.when(when[0])(lambda: comm.phase(0, c_in, c_out, c_scr))
        body(*h_in, *h_out, *h_scr)
        pl.when(when[1])(lambda: comm.phase(1, c_in, c_out, c_scr))
        pl.when(when[2])(lambda: comm.phase(2, c_in, c_out, c_scr))

    outs = pl.pallas_call(
        both, name=name, grid=grid, in_specs=list(in_specs) + comm.in_specs,
        out_specs=tuple(out_specs) + tuple(comm.out_specs), out_shape=tuple(out_shape) + tuple(comm.out_shape),
        scratch_shapes=list(scratch_shapes) + comm.scratch, compiler_params=_params(len(grid)),
    )(*args, *comm.arrs)
    return outs[:no], outs[no:]


def _grid_steps(n_outer, n_inner=1):
    total = n_outer * n_inner

    def steps():
        t = pl.program_id(0) * n_inner + (pl.program_id(1) if n_inner > 1 else 0)
        return t == 0, t == (total * 7) // 8, t == total - 1
    return steps


def widen_slabs(arrs, ns, name):
    per = N_DEV // ns
    _, R, C = arrs[0].shape
    n = len(arrs)

    def body(*refs):
        for a in range(n):
            for k in range(per):
                refs[n + a][:, k * C:(k + 1) * C] = refs[a][k]

    return pl.pallas_call(
        body, name=name, grid=(ns,), in_specs=[pl.BlockSpec((per, R, C), lambda s: (s, 0, 0))] * n,
        out_specs=tuple(pl.BlockSpec((None, R, per * C), lambda s: (s, 0, 0)) for _ in range(n)),
        out_shape=tuple(SDS((ns, R, per * C), a.dtype) for a in arrs), compiler_params=_params(1))(*arrs)


def join_columns(blocks, n_main, name):
    nb, R, c8 = blocks.shape
    rest = nb * c8 - n_main
    tr = _row_tile(R, 256)

    def body(b_ref, main_ref, rest_ref, full):
        for k in range(nb):
            full[:, k * c8:(k + 1) * c8] = b_ref[k]
        main_ref[...] = full[:, :n_main]
        rest_ref[...] = jnp.zeros_like(rest_ref)
        rest_ref[:, :rest] = full[:, n_main:]

    return pl.pallas_call(
        body, name=name, grid=(R // tr,), in_specs=[pl.BlockSpec((nb, tr, c8), lambda i: (0, i, 0))],
        out_specs=(pl.BlockSpec((tr, n_main), lambda i: (i, 0)), pl.BlockSpec((tr, HEAD), lambda i: (i, 0))),
        out_shape=(SDS((R, n_main), blocks.dtype), SDS((R, HEAD), blocks.dtype)),
        scratch_shapes=[pltpu.VMEM((tr, nb * c8), blocks.dtype)], compiler_params=_params(1))(blocks)


def split_columns(main, rest, c8, name):
    R, n_main = main.shape
    nb = N_DEV
    n_rest = nb * c8 - n_main
    tr = _row_tile(R, 256)

    def body(main_ref, rest_ref, b_ref, full):
        full[:, :n_main] = main_ref[...]
        full[:, n_main:] = rest_ref[:, :n_rest]
        for k in range(nb):
            b_ref[k] = full[:, k * c8:(k + 1) * c8]

    return pl.pallas_call(
        body, name=name, grid=(R // tr,),
        in_specs=[pl.BlockSpec((tr, n_main), lambda i: (i, 0)), pl.BlockSpec((tr, HEAD), lambda i: (i, 0))],
        out_specs=pl.BlockSpec((nb, tr, c8), lambda i: (0, i, 0)), out_shape=SDS((nb, R, c8), main.dtype),
        scratch_shapes=[pltpu.VMEM((tr, nb * c8), main.dtype)], compiler_params=_params(1))(main, rest)


def ffn_fwd(x, gpre, gpost, wg, wu, wd, name, comm=None):
    S, D = x.shape
    nj, F8 = wg.shape[0], wg.shape[-1]
    tm = _row_tile(S, FFN_ROWS_FWD)

    def body(x_ref, gpre_ref, gpost_ref, wg_ref, wu_ref, wd_ref, xo_ref, h_ref, p_ref, q_ref, t_ref, y_ref):
        j = pl.program_id(1)

        @pl.when(j == 0)
        def _():
            h_ref[...] = _rms(x_ref[...], gpre_ref[...]).astype(h_ref.dtype)
            y_ref[...] = jnp.zeros_like(y_ref)

        h = h_ref[...]
        a = _mm(h, wg_ref[...])
        b = _mm(h, wu_ref[...])
        s = jax.nn.sigmoid(a)
        q = a * s
        p_ref[...] = (b * (s + q * (1.0 - s))).astype(p_ref.dtype)
        q_ref[...] = q.astype(q_ref.dtype)
        t = (q * b).astype(t_ref.dtype)
        t_ref[...] = t
        y_ref[...] += _mm(t, wd_ref[...])

        @pl.when(j == nj - 1)
        def _():
            xo_ref[...] = x_ref[...] + 0.5 * _rms(y_ref[...], gpost_ref[...])

    row = pl.BlockSpec((tm, D), lambda i, j: (i, 0))
    vec = pl.BlockSpec((1, D), lambda i, j: (0, 0))
    wcol = pl.BlockSpec((None, D, F8), lambda i, j: (j, 0, 0))
    wrow = pl.BlockSpec((None, F8, D), lambda i, j: (j, 0, 0))
    hid = pl.BlockSpec((None, tm, F8), lambda i, j: (j, i, 0))
    return hosted_call(
        body, comm, _grid_steps(S // tm, nj), name=name, grid=(S // tm, nj),
        in_specs=[row, vec, vec, wcol, wcol, wrow],
        out_specs=(row, row, hid, hid, hid, row),
        out_shape=(SDS((S, D), f32), SDS((S, D), MXU_DTYPE), SDS((nj, S, F8), MXU_DTYPE),
                   SDS((nj, S, F8), MXU_DTYPE), SDS((nj, S, F8), MXU_DTYPE), SDS((S, D), f32)),
        scratch_shapes=[], args=(x, gpre, gpost, wg, wu, wd))


def ffn_bwd_dx(dxo, x, y, p, q, gpre, gpost, wg, wu, wd, name, comm=None):
    S, D = x.shape
    NS, F8 = wg.shape[0], wg.shape[-1]
    sps = 2 if NS % 2 == 0 else 1
    nj = NS // sps
    tm = _row_tile(S, FFN_ROWS_BWD)

    def body(dxo_ref, x_ref, y_ref, p_ref, q_ref, gpre_ref, gpost_ref, wg_ref, wu_ref, wd_ref,
             dx_ref, da_ref, db_ref, dy_ref, dgpre_ref, dgpost_ref, dh_ref):
        i, j = pl.program_id(0), pl.program_id(1)

        @pl.when(j == 0)
        def _():
            @pl.when(i == 0)
            def _():
                dgpre_ref[...] = jnp.zeros_like(dgpre_ref)
                dgpost_ref[...] = jnp.zeros_like(dgpost_ref)

            dy, dg = _rms_bwd(y_ref[...], gpost_ref[...], 0.5 * dxo_ref[...])
            dy_ref[...] = dy.astype(dy_ref.dtype)
            dgpost_ref[...] += dg
            dh_ref[...] = jnp.zeros_like(dh_ref)

        dy = dy_ref[...]
        das, dbs = [], []
        for s in range(sps):
            dt = _mm_nt(dy, wd_ref[s])
            das.append((dt * p_ref[s].astype(f32)).astype(da_ref.dtype))
            dbs.append((dt * q_ref[s].astype(f32)).astype(db_ref.dtype))
            da_ref[s] = das[s]
            db_ref[s] = dbs[s]
        upd = None
        for s in range(sps):
            part = _mm_nt(das[s], wg_ref[s]) + _mm_nt(dbs[s], wu_ref[s])
            upd = part if upd is None else upd + part
        dh_ref[...] += upd

        @pl.when(j == nj - 1)
        def _():
            dxx, dg = _rms_bwd(x_ref[...], gpre_ref[...], dh_ref[...])
            dx_ref[...] = dxo_ref[...] + dxx
            dgpre_ref[...] += dg

    row = pl.BlockSpec((tm, D), lambda i, j: (i, 0))
    vec = pl.BlockSpec((1, D), lambda i, j: (0, 0))
    wcol = pl.BlockSpec((sps, D, F8), lambda i, j: (j, 0, 0))
    wrow = pl.BlockSpec((sps, F8, D), lambda i, j: (j, 0, 0))
    hid = pl.BlockSpec((sps, tm, F8), lambda i, j: (j, i, 0))
    return hosted_call(
        body, comm, _grid_steps(S // tm, nj), name=name, grid=(S // tm, nj),
        in_specs=[row, row, row, hid, hid, vec, vec, wcol, wcol, wrow],
        out_specs=(row, hid, hid, row, vec, vec),
        out_shape=(SDS((S, D), f32), SDS((NS, S, F8), MXU_DTYPE), SDS((NS, S, F8), MXU_DTYPE),
                   SDS((S, D), MXU_DTYPE), SDS((1, D), f32), SDS((1, D), f32)),
        scratch_shapes=[pltpu.VMEM((tm, D), f32)], args=(dxo, x, y, p, q, gpre, gpost, wg, wu, wd))


def ffn_bwd_dw(h, dy, t, da, db, name, comm=None):
    S, D = h.shape
    NS, F8 = t.shape[0], t.shape[-1]
    per = N_DEV // NS
    w8 = F8 // per
    tm = _row_tile(S, FFN_ROWS_DW)
    ni = S // tm

    def body(h_ref, dy_ref, t_ref, da_ref, db_ref, dwg_ref, dwu_ref, dwd_ref, accg, accu, accd):
        i = pl.program_id(1)

        @pl.when(i == 0)
        def _():
            accg[...] = jnp.zeros_like(accg)
            accu[...] = jnp.zeros_like(accu)
            accd[...] = jnp.zeros_like(accd)

        hh = h_ref[...]
        accg[...] += _mm_tn(hh, da_ref[...])
        accu[...] += _mm_tn(hh, db_ref[...])
        accd[...] += _mm_tn(t_ref[...], dy_ref[...])

        @pl.when(i == ni - 1)
        def _():
            for k in range(per):
                ks = slice(k * w8, (k + 1) * w8)
                dwg_ref[k] = accg[:, ks].astype(dwg_ref.dtype)
                dwu_ref[k] = accu[:, ks].astype(dwu_ref.dtype)
                dwd_ref[k] = accd[ks, :].astype(dwd_ref.dtype)

    row = pl.BlockSpec((tm, D), lambda j, i: (i, 0))
    hid = pl.BlockSpec((None, tm, F8), lambda j, i: (j, i, 0))
    wcol = pl.BlockSpec((per, D, w8), lambda j, i: (j, 0, 0))
    wrow = pl.BlockSpec((per, w8, D), lambda j, i: (j, 0, 0))
    return hosted_call(
        body, comm, _grid_steps(NS, ni), name=name, grid=(NS, ni),
        in_specs=[row, row, hid, hid, hid],
        out_specs=(wcol, wcol, wrow),
        out_shape=(SDS((N_DEV, D, w8), MXU_DTYPE), SDS((N_DEV, D, w8), MXU_DTYPE), SDS((N_DEV, w8, D), MXU_DTYPE)),
        scratch_shapes=[pltpu.VMEM((D, F8), f32), pltpu.VMEM((D, F8), f32), pltpu.VMEM((F8, D), f32)],
        args=(h, dy, t, da, db))


def ffn_bwd_dw_one(rows_op, slab_op, hidden_rows, name, comm=None):
    S, D = rows_op.shape
    NS, F8 = slab_op.shape[0], slab_op.shape[-1]
    per = N_DEV // NS
    w8 = F8 // per
    tm = _row_tile(S, FFN_ROWS_DW)
    ni = S // tm

    def body(r_ref, s_ref, o_ref, acc):
        i = pl.program_id(1)

        @pl.when(i == 0)
        def _():
            acc[...] = jnp.zeros_like(acc)

        acc[...] += _mm_tn(s_ref[...], r_ref[...]) if hidden_rows else _mm_tn(r_ref[...], s_ref[...])

        @pl.when(i == ni - 1)
        def _():
            for k in range(per):
                ks = slice(k * w8, (k + 1) * w8)
                o_ref[k] = (acc[ks, :] if hidden_rows else acc[:, ks]).astype(o_ref.dtype)

    blk = (per, w8, D) if hidden_rows else (per, D, w8)
    return hosted_call(
        body, comm, _grid_steps(NS, ni), name=name, grid=(NS, ni),
        in_specs=[pl.BlockSpec((tm, D), lambda j, i: (i, 0)), pl.BlockSpec((None, tm, F8), lambda j, i: (j, i, 0))],
        out_specs=(pl.BlockSpec(blk, lambda j, i: (j, 0, 0)),),
        out_shape=(SDS((N_DEV,) + blk[1:], MXU_DTYPE),),
        scratch_shapes=[pltpu.VMEM((F8, D) if hidden_rows else (D, F8), f32)], args=(rows_op, slab_op))


def rms_mm(x, g, w, w2, name, tn=1024):
    S, D = x.shape
    N = w.shape[1]
    tm = _row_tile(S, PROJ_ROWS)
    tn = _row_tile(N, tn)
    has2 = w2 is not None

    def body(*refs):
        if has2:
            x_ref, g_ref, w_ref, w2_ref, h_ref, o_ref, o2_ref = refs
        else:
            x_ref, g_ref, w_ref, h_ref, o_ref = refs
        j = pl.program_id(1)

        @pl.when(j == 0)
        def _():
            h = _rms(x_ref[...], g_ref[...]).astype(h_ref.dtype)
            h_ref[...] = h
            if has2:
                o2_ref[...] = _mm(h, w2_ref[...])

        o_ref[...] = _mm(h_ref[...], w_ref[...])

    row = pl.BlockSpec((tm, D), lambda i, j: (i, 0))
    in_specs = [row, pl.BlockSpec((1, D), lambda i, j: (0, 0)), pl.BlockSpec((D, tn), lambda i, j: (0, j))]
    out_specs = [row, pl.BlockSpec((tm, tn), lambda i, j: (i, j))]
    out_shape = [SDS((S, D), MXU_DTYPE), SDS((S, N), f32)]
    args = [x, g, w]
    if has2:
        in_specs.append(pl.BlockSpec((D, w2.shape[1]), lambda i, j: (0, 0)))
        out_specs.append(pl.BlockSpec((tm, w2.shape[1]), lambda i, j: (i, 0)))
        out_shape.append(SDS((S, w2.shape[1]), f32))
        args.append(w2)
    return pl.pallas_call(
        body, name=name, grid=(S // tm, N // tn), in_specs=in_specs, out_specs=tuple(out_specs),
        out_shape=tuple(out_shape), compiler_params=_params(2),
    )(*args)


def mm_bwd_dx(dres, x, g, dy, w, dy2, w2, name, tk=1024, comm=None):
    S, D = x.shape
    K = dy.shape[1]
    tm = _row_tile(S, PROJ_ROWS)
    tk = _row_tile(K, tk)
    nk = K // tk
    has2 = dy2 is not None

    def body(*refs):
        if has2:
            dres_ref, x_ref, g_ref, dy_ref, w_ref, dy2_ref, w2_ref, dx_ref, dg_ref, dh_ref = refs
        else:
            dres_ref, x_ref, g_ref, dy_ref, w_ref, dx_ref, dg_ref, dh_ref = refs
        i, k = pl.program_id(0), pl.program_id(1)

        @pl.when(k == 0)
        def _():
            @pl.when(i == 0)
            def _():
                dg_ref[...] = jnp.zeros_like(dg_ref)

            if has2:
                dh_ref[...] = _mm_nt(dy2_ref[...], w2_ref[...])
            else:
                dh_ref[...] = jnp.zeros_like(dh_ref)

        dh_ref[...] += _mm_nt(dy_ref[...], w_ref[...])

        @pl.when(k == nk - 1)
        def _():
            dxx, dg = _rms_bwd(x_ref[...], g_ref[...], dh_ref[...])
            dx_ref[...] = dres_ref[...] + dxx
            dg_ref[...] += dg

    row = pl.BlockSpec((tm, D), lambda i, k: (i, 0))
    vec = pl.BlockSpec((1, D), lambda i, k: (0, 0))
    in_specs = [row, row, vec, pl.BlockSpec((tm, tk), lambda i, k: (i, k)), pl.BlockSpec((D, tk), lambda i, k: (0, k))]
    args = [dres, x, g, dy, w]
    if has2:
        in_specs += [pl.BlockSpec((tm, dy2.shape[1]), lambda i, k: (i, 0)),
                     pl.BlockSpec((D, w2.shape[1]), lambda i, k: (0, 0))]
        args += [dy2, w2]
    return hosted_call(
        body, comm, _grid_steps(S // tm, nk), name=name, grid=(S // tm, nk), in_specs=in_specs, out_specs=(row, vec),
        out_shape=(SDS((S, D), f32), SDS((1, D), f32)), scratch_shapes=[pltpu.VMEM((tm, D), f32)], args=args)


def tn_mm(a, b, name, tn=512, slot_major=False):
    S, K1 = a.shape
    N = b.shape[1]
    tm = _row_tile(S, TN_ROWS)
    tn = _row_tile(N, tn)
    ni = S // tm

    def body(a_ref, b_ref, o_ref, acc):
        i = pl.program_id(1)

        @pl.when(i == 0)
        def _():
            acc[...] = jnp.zeros_like(acc)

        acc[...] += _mm_tn(a_ref[...], b_ref[...])

        @pl.when(i == ni - 1)
        def _():
            o_ref[...] = acc[...].astype(o_ref.dtype)

    if slot_major:
        out_spec, out_shape = pl.BlockSpec((None, K1, tn), lambda j, i: (j, 0, 0)), SDS((N // tn, K1, tn), MXU_DTYPE)
    else:
        out_spec, out_shape = pl.BlockSpec((K1, tn), lambda j, i: (0, j)), SDS((K1, N), MXU_DTYPE)
    return pl.pallas_call(
        body, name=name, grid=(N // tn, ni),
        in_specs=[pl.BlockSpec((tm, K1), lambda j, i: (i, 0)), pl.BlockSpec((tm, tn), lambda j, i: (i, j))],
        out_specs=out_spec, out_shape=out_shape,
        scratch_shapes=[pltpu.VMEM((K1, tn), f32)], compiler_params=_params(2),
    )(a, b)


CONV_ROWS = 512


def _shift_down(cur, prev8, s):
    r = pltpu.roll(cur, s, 0)
    row = lax.broadcasted_iota(jnp.int32, (8, cur.shape[1]), 0)
    top = jnp.where(row < s, pltpu.roll(prev8, s, 0), r[0:8])
    return jnp.concatenate([top, r[8:]], axis=0)


def _shift_up(cur, next8, s):
    n = cur.shape[0]
    r = pltpu.roll(cur, n - s, 0)
    row = lax.broadcasted_iota(jnp.int32, (8, cur.shape[1]), 0)
    bot = jnp.where(row >= 8 - s, pltpu.roll(next8, 8 - s, 0), r[n - 8:])
    return jnp.concatenate([r[:n - 8], bot], axis=0)


def _conv_taps(cur, prev8):
    return [_shift_down(cur, prev8, 3), _shift_down(cur, prev8, 2), _shift_down(cur, prev8, 1), cur]


def _act_qk(c):
    a = _silu(c)
    return a * lax.rsqrt(jnp.sum(a * a, axis=-1, keepdims=True) + L2_EPS)


def dn_prep(proj, conv_w, name):
    S = proj.shape[0]
    W = conv_w.shape[1] // 3
    nh = W // HEAD
    R = _row_tile(S, CONV_ROWS)

    def body(p_ref, w_ref, o_ref):
        j = pl.program_id(0)
        w = w_ref[...]

        def rows(r, prev8):
            cur = p_ref[pl.ds(r, R), :]
            taps = _conv_taps(cur, prev8)
            cv = taps[0] * w[0:1] + taps[1] * w[1:2] + taps[2] * w[2:3] + taps[3] * w[3:4]

            @pl.when(j < 2 * nh)
            def _():
                o_ref[pl.ds(r, R), :] = _act_qk(cv)

            @pl.when(j >= 2 * nh)
            def _():
                o_ref[pl.ds(r, R), :] = _silu(cv)

        rows(0, jnp.zeros((8, HEAD), f32))

        @pl.loop(1, S // R)
        def _(t):
            r = pl.multiple_of(t * R, R)
            rows(r, p_ref[pl.ds(r - 8, 8), :])

    return pl.pallas_call(
        body, name=name, grid=(3 * nh,),
        in_specs=[pl.BlockSpec((S, HEAD), lambda j: (0, j)), pl.BlockSpec((CONV_K, HEAD), lambda j: (0, j))],
        out_specs=pl.BlockSpec((None, S, HEAD), lambda j: (j // nh, 0, j % nh)),
        out_shape=SDS((3, S, W), f32), compiler_params=_params(1),
    )(proj, conv_w)


def dn_prep_bwd(proj, conv_w, dqkv, dz, name, comm=None):
    S = proj.shape[0]
    W = conv_w.shape[1] // 3
    nh = W // HEAD
    nq = 3 * nh
    R = _row_tile(S, CONV_ROWS)
    nr = S // R

    def body(p_ref, w_ref, dq_ref, dz_ref, dp_ref, dw_ref, dc_ref):
        j = pl.program_id(0)

        @pl.when(j >= nq)
        def _():
            dp_ref[...] = dz_ref[...].astype(dp_ref.dtype)

        @pl.when(j < nq)
        def _():
            w = w_ref[...]
            dw_ref[...] = jnp.zeros_like(dw_ref)

            def rows(r, prev8):
                cur = p_ref[pl.ds(r, R), :]
                taps = _conv_taps(cur, prev8)
                cv = taps[0] * w[0:1] + taps[1] * w[1:2] + taps[2] * w[2:3] + taps[3] * w[3:4]
                dn = dq_ref[pl.ds(r, R), :]

                @pl.when(j < 2 * nh)
                def _():
                    dc_ref[pl.ds(r, R), :] = jax.vjp(_act_qk, cv)[1](dn)[0]

                @pl.when(j >= 2 * nh)
                def _():
                    dc_ref[pl.ds(r, R), :] = jax.vjp(_silu, cv)[1](dn)[0]

                dc = dc_ref[pl.ds(r, R), :]
                dw_ref[...] += jnp.concatenate(
                    [jnp.sum(dc * taps[q], axis=0, keepdims=True) for q in range(CONV_K)], axis=0)

            rows(0, jnp.zeros((8, HEAD), f32))

            @pl.loop(1, nr)
            def _(t):
                r = pl.multiple_of(t * R, R)
                rows(r, p_ref[pl.ds(r - 8, 8), :])

            def back(r, next8):
                dc = dc_ref[pl.ds(r, R), :]
                dx = dc * w[3:4]
                for s in (1, 2, 3):
                    dx = dx + _shift_up(dc, next8, s) * w[3 - s:4 - s]
                dp_ref[pl.ds(r, R), :] = dx.astype(dp_ref.dtype)

            @pl.loop(0, nr - 1)
            def _(t):
                r = pl.multiple_of(t * R, R)
                back(r, dc_ref[pl.ds(r + R, 8), :])

            back((nr - 1) * R, jnp.zeros((8, HEAD), f32))

    clamp = lambda j: jnp.minimum(j, nq - 1)
    return hosted_call(
        body, comm, _grid_steps(4 * nh), name=name, grid=(4 * nh,),
        in_specs=[pl.BlockSpec((S, HEAD), lambda j: (0, clamp(j))),
                  pl.BlockSpec((CONV_K, HEAD), lambda j: (0, clamp(j))),
                  pl.BlockSpec((None, S, HEAD), lambda j: (clamp(j) // nh, 0, clamp(j) % nh)),
                  pl.BlockSpec((S, HEAD), lambda j: (0, jnp.maximum(j - nq, 0)))],
        out_specs=(pl.BlockSpec((S, HEAD), lambda j: (0, j)), pl.BlockSpec((CONV_K, HEAD), lambda j: (0, clamp(j)))),
        out_shape=(SDS((S, 4 * W), MXU_DTYPE), SDS((CONV_K, 3 * W), f32)),
        scratch_shapes=[pltpu.VMEM((S, HEAD), f32)], args=(proj, conv_w, dqkv, dz))


def _lane_pick(x, lane):
    sel = lax.broadcasted_iota(jnp.int32, x.shape, 1) == lane
    return jnp.broadcast_to(jnp.sum(jnp.where(sel, x, 0.0), axis=1, keepdims=True), x.shape)


CUM_ROWS = 256


def _sel_mm(m01, x):
    m = _c(m01)
    d = lambda p: lax.dot_general(m, p, (NN, ((), ())), preferred_element_type=f32)
    h1, h2, h3 = _pieces3(x)
    return (d(h1) + d(h2)) + d(h3)


def _chunk_cumsum_matrix(n, transpose):
    r, c = lax.broadcasted_iota(jnp.int32, (n, n), 0), lax.broadcasted_iota(jnp.int32, (n, n), 1)
    sh = int(math.log2(DN_CHUNK))
    same = (r >> sh) == (c >> sh)
    return jnp.where(same & ((r <= c) if transpose else (r >= c)), 1.0, 0.0).astype(f32)


def _gates_by_lane(H, p, al, dt):
    lane = lax.broadcasted_iota(jnp.int32, p.shape, 1)
    g = -jnp.exp(al) * jax.nn.softplus(p + dt)
    return jnp.where(lane < H, jax.nn.sigmoid(p), jnp.where(lane < 2 * H, g, 0.0))


def dn_gates(pba, al, dt, H, name):
    S = pba.shape[0]
    R = _row_tile(S, CUM_ROWS)

    def body(p_ref, al_ref, dt_ref, o_ref):
        raw = _gates_by_lane(H, p_ref[...], al_ref[...], dt_ref[...])
        lane = lax.broadcasted_iota(jnp.int32, raw.shape, 1)
        o_ref[...] = jnp.where(lane < H, raw, _sel_mm(_chunk_cumsum_matrix(R, False), raw))

    blk = pl.BlockSpec((R, HEAD), lambda i: (i, 0))
    par = pl.BlockSpec((1, HEAD), lambda i: (0, 0))
    return pl.pallas_call(body, name=name, grid=(S // R,), in_specs=[blk, par, par], out_specs=blk,
                          out_shape=SDS((S, HEAD), f32), compiler_params=_params(1))(pba, al, dt)


def dn_gates_bwd(pba, al, dt, dgates, H, name):
    S = pba.shape[0]
    R = _row_tile(S, CUM_ROWS)

    def body(p_ref, al_ref, dt_ref, dg_ref, dp_ref, dal_ref, ddt_ref):
        @pl.when(pl.program_id(0) == 0)
        def _():
            dal_ref[...] = jnp.zeros_like(dal_ref)
            ddt_ref[...] = jnp.zeros_like(ddt_ref)

        d = dg_ref[...]
        lane = lax.broadcasted_iota(jnp.int32, d.shape, 1)
        d = jnp.where(lane < H, d, _sel_mm(_chunk_cumsum_matrix(R, True), d))
        _, vjp = jax.vjp(functools.partial(_gates_by_lane, H), p_ref[...], al_ref[...], dt_ref[...])
        dp, dal, ddt = vjp(d)
        dp_ref[...] = dp.astype(dp_ref.dtype)
        dal_ref[...] += dal
        ddt_ref[...] += ddt

    blk = pl.BlockSpec((R, HEAD), lambda i: (i, 0))
    par = pl.BlockSpec((1, HEAD), lambda i: (0, 0))
    return pl.pallas_call(
        body, name=name, grid=(S // R,), in_specs=[blk, par, par, blk], out_specs=(blk, par, par),
        out_shape=(SDS((S, HEAD), MXU_DTYPE), SDS((1, HEAD), f32), SDS((1, HEAD), f32)), compiler_params=_params(1),
    )(pba, al, dt, dgates)


def _bdot(dims):
    back = {NN: ((NT, 'gb'), (TN, 'ag')), NT: ((NN, 'gb'), (TN, 'ga')), TN: ((NT, 'bg'), (NN, 'ag'))}[dims]
    d = lambda p, q, dm: lax.dot_general(_c(p), _c(q), (dm, ((), ())), preferred_element_type=f32)

    @jax.custom_vjp
    def f(a, b):
        return d(a, b, dims)

    def fwd(a, b):
        return d(a, b, dims), (a, b)

    def bwd(res, g):
        v = {'a': res[0], 'b': res[1], 'g': g}
        (da_dims, da_ops), (db_dims, db_ops) = back
        return d(v[da_ops[0]], v[da_ops[1]], da_dims), d(v[db_ops[0]], v[db_ops[1]], db_dims)

    f.defvjp(fwd, bwd)
    return f, lambda a, b: d(a, b, dims)


_BDOT = {dims: _bdot(dims) for dims in (NN, NT, TN)}


def _tri_inv_multi(Ls):
    n = Ls[0].shape[0]
    eye = jnp.where(lax.broadcasted_iota(jnp.int32, (n, n), 0) == lax.broadcasted_iota(jnp.int32, (n, n), 1), 1.0, 0.0)
    P = tuple(-L for L in Ls)
    T = tuple(eye + p for p in P)
    P = tuple(_dot3(p, p, NN) for p in P)
    levels = int(math.log2(n)) - 1
    for lvl in range(levels):
        if lvl == levels - 1:
            T = tuple(t + _dot3(t, p, NN) for t, p in zip(T, P))
        else:
            both = tuple(_dot3(jnp.concatenate([t, p], axis=0), p, NN) for t, p in zip(T, P))
            T = tuple(t + b[:n] for t, b in zip(T, both))
            P = tuple(b[n:] for b in both)
    return T


@jax.custom_vjp
def _tri_inv_kept(Ls, Ts):
    return Ts


def _tri_inv_kept_bwd(T, dT):
    X = tuple(_dot3(d, t, NT) for d, t in zip(dT, T))
    return tuple(-_dot3(t, x, TN) for t, x in zip(T, X)), tuple(jnp.zeros_like(t) for t in T)


_tri_inv_kept.defvjp(lambda Ls, Ts: (Ts, Ts), _tri_inv_kept_bwd)


def _pieces3(x):
    h1 = x.astype(MXU_DTYPE)
    r1 = x - h1.astype(f32)
    h2 = r1.astype(MXU_DTYPE)
    return h1, h2, (r1 - h2.astype(f32)).astype(MXU_DTYPE)


def _row_bcast_impl(sel_row, gc):
    s = _c(sel_row)
    d = lambda p: lax.dot_general(s, p, (NT, ((), ())), preferred_element_type=f32)
    h1, h2, h3 = _pieces3(gc)
    return (d(h1) + d(h2)) + d(h3)


def _row_bcast_bwd(sel_row, d):
    s = _c(sel_row)
    hi, lo = _split(d)
    t = lambda p: lax.dot_general(p, s, (TN, ((), ())), preferred_element_type=f32)
    return jnp.zeros_like(sel_row), t(hi) + t(lo)


_row_bcast = jax.custom_vjp(_row_bcast_impl)
_row_bcast.defvjp(lambda sel_row, gc: (_row_bcast_impl(sel_row, gc), sel_row), _row_bcast_bwd)


def _col_bcast_impl(gc):
    return gc[:, :DN_CHUNK]


def _col_bcast_bwd(_, d):
    return (jnp.broadcast_to(jnp.sum(d, axis=1, keepdims=True) * (1.0 / HEAD), (d.shape[0], HEAD)),)


_col_bcast = jax.custom_vjp(_col_bcast_impl)
_col_bcast.defvjp(lambda gc: (_col_bcast_impl(gc), None), _col_bcast_bwd)


def _last_row_bcast(n):
    def impl(gc):
        return jnp.broadcast_to(gc[DN_CHUNK - 1:DN_CHUNK, :], (n, HEAD))

    def bwd(_, d):
        row = lax.broadcasted_iota(jnp.int32, (DN_CHUNK, HEAD), 0)
        return (jnp.where(row == DN_CHUNK - 1, jnp.sum(d, axis=0, keepdims=True), 0.0),)

    f = jax.custom_vjp(impl)
    f.defvjp(lambda gc: (impl(gc), None), bwd)
    return impl, f


_LAST_C, _LAST_H = _last_row_bcast(DN_CHUNK), _last_row_bcast(HEAD)


def _halves(axis):
    def impl(x):
        n = x.shape[axis] // 2
        return lax.slice_in_dim(x, 0, n, axis=axis), lax.slice_in_dim(x, n, 2 * n, axis=axis)

    f = jax.custom_vjp(impl)
    f.defvjp(lambda x: (impl(x), None), lambda _, g: (jnp.concatenate(g, axis=axis),))
    return impl, f


_ROW_HALVES, _COL_HALVES = _halves(0), _halves(1)


def _chunk_consts():
    C = DN_CHUNK
    io = lambda shape, ax: lax.broadcasted_iota(jnp.int32, shape, ax)
    one = lambda m: jnp.where(m, 1.0, 0.0).astype(f32)
    r, c = io((C, C), 0), io((C, C), 1)
    return dict(causal=r >= c, strict=r > c, sel_row=one(io((C, HEAD), 1) == 0))


def _chunk_fn(kc, kept_T, q, k, v, gc, bB, S0):
    diff = kept_T is not None
    i = 0 if diff else 1
    mm, mm_nt, mm_tn = _BDOT[NN][i], _BDOT[NT][i], _BDOT[TN][i]
    tri = (lambda Ls: _tri_inv_kept(Ls, kept_T)) if diff else _tri_inv_multi
    each = lambda f, *ls: tuple(f(*a) for a in zip(*ls))
    gcol = each(_col_bcast if diff else _col_bcast_impl, gc)
    grow = each(lambda g: (_row_bcast if diff else _row_bcast_impl)(kc['sel_row'], g), gc)
    glc = each(_LAST_C[i ^ 1], gc)
    glh = each(_LAST_H[i ^ 1], gc)
    decay = each(lambda a, b: jnp.where(kc['causal'], jnp.exp(jnp.where(kc['causal'], a - b, 0.0)), 0.0), gcol, grow)
    rows, cols = _ROW_HALVES[i ^ 1], _COL_HALVES[i ^ 1]
    first, second = (lambda ts: tuple(t[0] for t in ts)), (lambda ts: tuple(t[1] for t in ts))
    kb = each(lambda a, b: a * b, k, bB)
    vb = each(lambda a, b: a * b, v, bB)
    egc = each(jnp.exp, gc)
    qs = each(lambda a: a * (HEAD ** -0.5), q)
    kq = each(lambda a, b, kt: rows(mm_nt(jnp.concatenate([a, b], axis=0), kt)), kb, qs, k)
    kk, qk = first(kq), second(kq)
    T = tri(each(lambda a, d: jnp.where(kc['strict'], a * d, 0.0), kk, decay))
    uw = each(lambda t, a, b, e: cols(mm(t, jnp.concatenate([a, b * e], axis=1))), T, vb, kb, egc)
    u, w = first(uw), second(uw)
    attn = each(lambda a, d: jnp.where(kc['causal'], a * d, 0.0), qk, decay)
    wq = each(lambda a, b, e, s: rows(mm(jnp.concatenate([a, b * e], axis=0), s)), w, qs, egc, S0)
    wS, qS = first(wq), second(wq)
    v_new = each(lambda a, b: a - b, u, wS)
    o = each(lambda a, b: a + b, qS, each(mm, attn, v_new))
    kdec = each(lambda a, gl, g: a * jnp.exp(gl - g), k, glc, gc)
    S1 = each(lambda s, gl, kv: s * jnp.exp(gl) + kv, S0, glh, each(mm_tn, kdec, v_new))
    return (o, S1) if diff else (o, S1, T)


def _chunks_per_step(N):
    return 4 if N % 4 == 0 else (2 if N % 2 == 0 else 1)


def _heads_per_block(H):
    return 8 if H % 8 == 0 else (4 if H % 4 == 0 else 1)


def dn_chunk_fwd(qkv, gates, name, comm=None):
    _, S, W = qkv.shape
    H, C = W // HEAD, DN_CHUNK
    N, HB = S // C, _heads_per_block(H)
    assert HB == H
    CPS = _chunks_per_step(N)

    def body(q_ref, k_ref, v_ref, g_ref, o_ref, st_ref, t_ref, s_scr):
        @pl.when(pl.program_id(1) == 0)
        def _():
            s_scr[...] = jnp.zeros_like(s_scr)

        kc = _chunk_consts()
        sls = [slice(hh * HEAD, (hh + 1) * HEAD) for hh in range(HB)]
        St = tuple(s_scr[hh] for hh in range(HB))
        for c in range(CPS):
            rows = slice(c * C, (c + 1) * C)
            heads = lambda ref: tuple(ref[rows, sl] for sl in sls)
            gr = g_ref[rows, :]
            for hh in range(HB):
                st_ref[c, hh] = St[hh]
            o, St, T = _chunk_fn(kc, None, heads(q_ref), heads(k_ref), heads(v_ref),
                                 tuple(_lane_pick(gr, H + hh) for hh in range(HB)),
                                 tuple(_lane_pick(gr, hh) for hh in range(HB)), St)
            for hh in range(HB):
                o_ref[rows, sls[hh]] = o[hh]
                t_ref[c, hh] = T[hh]
        for hh in range(HB):
            s_scr[hh] = St[hh]

    part = lambda p: pl.BlockSpec((None, CPS * C, HB * HEAD), lambda hb, n: (p, n, hb))
    return hosted_call(
        body, comm, _grid_steps(H // HB, N // CPS), name=name, grid=(H // HB, N // CPS),
        in_specs=[part(0), part(1), part(2), pl.BlockSpec((CPS * C, HEAD), lambda hb, n: (n, 0))],
        out_specs=(pl.BlockSpec((CPS * C, HB * HEAD), lambda hb, n: (n, hb)),
                   pl.BlockSpec((CPS, HB, HEAD, HEAD), lambda hb, n: (n, hb, 0, 0)),
                   pl.BlockSpec((CPS, HB, C, C), lambda hb, n: (n, hb, 0, 0))),
        out_shape=(SDS((S, W), f32), SDS((N, H, HEAD, HEAD), f32), SDS((N, H, C, C), f32)),
        scratch_shapes=[pltpu.VMEM((HB, HEAD, HEAD), f32)], args=(qkv, qkv, qkv, gates))


def dn_chunk_bwd(qkv, gates, states, kept_T, do, name, comm=None):
    _, S, W = qkv.shape
    H, C = W // HEAD, DN_CHUNK
    N, HB = S // C, _heads_per_block(H)
    assert HB == H
    CPS = _chunks_per_step(N)
    NB = N // CPS

    def body(q_ref, k_ref, v_ref, g_ref, st_ref, t_ref, do_ref, dqkv_ref, dg_ref, ds_scr):
        @pl.when(pl.program_id(1) == 0)
        def _():
            ds_scr[...] = jnp.zeros_like(ds_scr)

        kc = _chunk_consts()
        sls = [slice(hh * HEAD, (hh + 1) * HEAD) for hh in range(HB)]
        dSt = tuple(ds_scr[hh] for hh in range(HB))
        for c in reversed(range(CPS)):
            rows = slice(c * C, (c + 1) * C)
            heads = lambda ref: tuple(ref[rows, sl] for sl in sls)
            gr = g_ref[rows, :]
            kept = tuple(t_ref[c, hh] for hh in range(HB))
            _, vjp = jax.vjp(functools.partial(_chunk_fn, kc, kept), heads(q_ref), heads(k_ref), heads(v_ref),
                             tuple(_lane_pick(gr, H + hh) for hh in range(HB)),
                             tuple(_lane_pick(gr, hh) for hh in range(HB)), tuple(st_ref[c, hh] for hh in range(HB)))
            dq, dk, dv, dg, db, dSt = vjp((heads(do_ref), dSt))
            lane = lax.broadcasted_iota(jnp.int32, (C, HEAD), 1)
            dgr = jnp.zeros((C, HEAD), f32)
            for hh in range(HB):
                dqkv_ref[0, rows, sls[hh]] = dq[hh]
                dqkv_ref[1, rows, sls[hh]] = dk[hh]
                dqkv_ref[2, rows, sls[hh]] = dv[hh]
                dgr = dgr + jnp.where(lane == hh, jnp.sum(db[hh], axis=1, keepdims=True), 0.0)
                dgr = dgr + jnp.where(lane == H + hh, jnp.sum(dg[hh], axis=1, keepdims=True), 0.0)
            dg_ref[rows, :] = dgr
        for hh in range(HB):
            ds_scr[hh] = dSt[hh]

    rev = lambda n: NB - 1 - n
    part = lambda p: pl.BlockSpec((None, CPS * C, HB * HEAD), lambda hb, n: (p, rev(n), hb))
    gate = pl.BlockSpec((CPS * C, HEAD), lambda hb, n: (rev(n), 0))
    return hosted_call(
        body, comm, _grid_steps(H // HB, NB), name=name, grid=(H // HB, NB),
        in_specs=[part(0), part(1), part(2), gate,
                  pl.BlockSpec((CPS, HB, HEAD, HEAD), lambda hb, n: (rev(n), hb, 0, 0)),
                  pl.BlockSpec((CPS, HB, C, C), lambda hb, n: (rev(n), hb, 0, 0)),
                  pl.BlockSpec((CPS * C, HB * HEAD), lambda hb, n: (rev(n), hb))],
        out_specs=(pl.BlockSpec((3, CPS * C, HB * HEAD), lambda hb, n: (0, rev(n), hb)), gate),
        out_shape=(SDS((3, S, W), f32), SDS((S, HEAD), f32)),
        scratch_shapes=[pltpu.VMEM((HB, HEAD, HEAD), f32)], args=(qkv, qkv, qkv, gates, states, kept_T, do))


def _gate_norm(o, z, ng):
    return _rms(o, ng) * _silu(z)


def dn_out(o, proj, ng, wout, x1, g3, name):
    S, W = o.shape
    D = x1.shape[1]
    nh = W // HEAD
    tm = _row_tile(S, MIX_ROWS)

    def body(o_ref, z_ref, ng_ref, w_ref, x_ref, g_ref, xo_ref, m_ref, og_ref):
        for h in range(nh):
            sl = slice(h * HEAD, (h + 1) * HEAD)
            og_ref[:, sl] = _gate_norm(o_ref[:, sl], z_ref[:, sl], ng_ref[...]).astype(og_ref.dtype)
        m = _mm(og_ref[...], w_ref[...])
        m_ref[...] = m
        xo_ref[...] = x_ref[...] + _rms(m, g_ref[...])

    rw = pl.BlockSpec((tm, W), lambda i: (i, 0))
    rd = pl.BlockSpec((tm, D), lambda i: (i, 0))
    return pl.pallas_call(
        body, name=name, grid=(S // tm,),
        in_specs=[rw, pl.BlockSpec((tm, W), lambda i: (i, 3)), pl.BlockSpec((1, HEAD), lambda i: (0, 0)),
                  pl.BlockSpec((W, D), lambda i: (0, 0)), rd, pl.BlockSpec((1, D), lambda i: (0, 0))],
        out_specs=(rd, rd, rw),
        out_shape=(SDS((S, D), f32), SDS((S, D), f32), SDS((S, W), MXU_DTYPE)), compiler_params=_params(1),
    )(o, proj, ng, wout, x1, g3)


def dn_out_bwd(dxo, m, g3, o, proj, ng, wout, name):
    S, W = o.shape
    D = m.shape[1]
    nh = W // HEAD
    tm = _row_tile(S, MIX_ROWS)

    def body(dxo_ref, m_ref, g_ref, o_ref, z_ref, ng_ref, w_ref, dm_ref, do_ref, dz_ref, dng_ref, dg_ref):
        @pl.when(pl.program_id(0) == 0)
        def _():
            dng_ref[...] = jnp.zeros_like(dng_ref)
            dg_ref[...] = jnp.zeros_like(dg_ref)

        dm, dg = _rms_bwd(m_ref[...], g_ref[...], dxo_ref[...])
        dg_ref[...] += dg
        dmc = dm.astype(dm_ref.dtype)
        dm_ref[...] = dmc
        dog = _mm_nt(dmc, w_ref[...])
        for h in range(nh):
            sl = slice(h * HEAD, (h + 1) * HEAD)
            _, vjp = jax.vjp(_gate_norm, o_ref[:, sl], z_ref[:, sl], ng_ref[...])
            do, dz, dng = vjp(dog[:, sl])
            do_ref[:, sl] = do
            dz_ref[:, sl] = dz.astype(dz_ref.dtype)
            dng_ref[...] += dng

    rw = pl.BlockSpec((tm, W), lambda i: (i, 0))
    rd = pl.BlockSpec((tm, D), lambda i: (i, 0))
    vd = pl.BlockSpec((1, D), lambda i: (0, 0))
    vh = pl.BlockSpec((1, HEAD), lambda i: (0, 0))
    return pl.pallas_call(
        body, name=name, grid=(S // tm,),
        in_specs=[rd, rd, vd, rw, pl.BlockSpec((tm, W), lambda i: (i, 3)), vh, pl.BlockSpec((W, D), lambda i: (0, 0))],
        out_specs=(rd, rw, rw, vh, vd),
        out_shape=(SDS((S, D), MXU_DTYPE), SDS((S, W), f32), SDS((S, W), MXU_DTYPE), SDS((1, HEAD), f32),
                   SDS((1, D), f32)),
        compiler_params=_params(1),
    )(dxo, m, g3, o, proj, ng, wout)


def _erf_arg(x):
    return lax.erf(x * 0.7071067811865476)


@jax.custom_vjp
def _gelu_with_erf(x, e):
    return 0.5 * x * (1.0 + e)


def _gelu_with_erf_bwd(res, g):
    x, e = res
    return g * (0.5 * (1.0 + e) + x * (jnp.exp(-0.5 * x * x) * 0.3989422804014327)), jnp.zeros_like(e)


_gelu_with_erf.defvjp(lambda x, e: (0.5 * x * (1.0 + e), (x, e)), _gelu_with_erf_bwd)


def _layernorm(t, lg, lb):
    tc = t - jnp.mean(t, axis=-1, keepdims=True)
    return tc * lax.rsqrt(jnp.mean(tc * tc, axis=-1, keepdims=True) + LN_EPS) * lg + lb


def _sg_stage1_kept(eu, ev, pu, pv, bu, bv, lg, lb):
    return _gelu_with_erf(pu + bu, eu), _layernorm(_gelu_with_erf(pv + bv, ev), lg, lb)


def _causal_mask(n):
    return lax.broadcasted_iota(jnp.int32, (n, n), 0) >= lax.broadcasted_iota(jnp.int32, (n, n), 1)


def sg_mid(pre, b_in, ln_g, ln_b, w_s, bsT, wout, x1, g3, name):
    S = pre.shape[0]
    E, D = ln_g.shape[1], x1.shape[1]
    G, CH = SG_GROUPS, SG_CHUNK
    Cg = E // G
    tm = _row_tile(S, MIX_ROWS)

    def body(pu_ref, pv_ref, bu_ref, bv_ref, lg_ref, lb_ref, ws_ref, bs_ref, w_ref, x_ref, g_ref,
             xo_ref, m_ref, gt_ref, e_ref):
        xu, xv = pu_ref[...] + bu_ref[...], pv_ref[...] + bv_ref[...]
        eu, ev = _erf_arg(xu), _erf_arg(xv)
        e_ref[:, :E] = eu.astype(e_ref.dtype)
        e_ref[:, E:] = ev.astype(e_ref.dtype)
        u = 0.5 * xu * (1.0 + eu)
        v = _layernorm(0.5 * xv * (1.0 + ev), lg_ref[...], lb_ref[...])
        mask = _causal_mask(CH)
        for g in range(G):
            wc = _c(jnp.where(mask, ws_ref[g], 0.0))
            bcol = bs_ref[:, g:g + 1]
            cs = slice(g * Cg, (g + 1) * Cg)
            for ch in range(tm // CH):
                rs = slice(ch * CH, (ch + 1) * CH)
                mixed = _mm(wc, _c(v[rs, cs])) + bcol
                gt_ref[rs, cs] = (u[rs, cs] * mixed).astype(gt_ref.dtype)
        m = _mm(gt_ref[...], w_ref[...])
        m_ref[...] = m
        xo_ref[...] = x_ref[...] + _rms(m, g_ref[...])

    half = lambda p: pl.BlockSpec((tm, E), lambda i: (i, p))
    vhalf = lambda p: pl.BlockSpec((1, E), lambda i: (0, p))
    ve = pl.BlockSpec((1, E), lambda i: (0, 0))
    rd = pl.BlockSpec((tm, D), lambda i: (i, 0))
    return pl.pallas_call(
        body, name=name, grid=(S // tm,),
        in_specs=[half(0), half(1), vhalf(0), vhalf(1), ve, ve, pl.BlockSpec((G, CH, CH), lambda i: (0, 0, 0)),
                  pl.BlockSpec((CH, G), lambda i: (0, 0)), pl.BlockSpec((E, D), lambda i: (0, 0)), rd,
                  pl.BlockSpec((1, D), lambda i: (0, 0))],
        out_specs=(rd, rd, pl.BlockSpec((tm, E), lambda i: (i, 0)), pl.BlockSpec((tm, 2 * E), lambda i: (i, 0))),
        out_shape=(SDS((S, D), f32), SDS((S, D), f32), SDS((S, E), MXU_DTYPE), SDS((S, 2 * E), MXU_DTYPE)),
        compiler_params=_params(1),
    )(pre, pre, b_in, b_in, ln_g, ln_b, w_s, bsT, wout, x1, g3)


def sg_mid_bwd(dxo, m, g3, pre, kept_erf, b_in, ln_g, ln_b, w_s, bsT, wout, name):
    S = pre.shape[0]
    E, D = ln_g.shape[1], m.shape[1]
    G, CH = SG_GROUPS, SG_CHUNK
    Cg = E // G
    tm = _row_tile(S, SG_BWD_ROWS)

    def body(dxo_ref, m_ref, g_ref, pu_ref, pv_ref, eu_ref, ev_ref, bu_ref, bv_ref, lg_ref, lb_ref, ws_ref, bs_ref,
             w_ref, dm_ref, dpre_ref, dbin_ref, dlg_ref, dlb_ref, dws_ref, dbs_ref, dg_ref, du_scr, dv_scr, v_scr):
        @pl.when(pl.program_id(0) == 0)
        def _():
            for r in (dbin_ref, dlg_ref, dlb_ref, dws_ref, dbs_ref, dg_ref):
                r[...] = jnp.zeros_like(r)

        dm, dg = _rms_bwd(m_ref[...], g_ref[...], dxo_ref[...])
        dg_ref[...] += dg
        dmc = dm.astype(dm_ref.dtype)
        dm_ref[...] = dmc
        du_scr[...] = _mm_nt(dmc, w_ref[...])
        stage1 = functools.partial(_sg_stage1_kept, eu_ref[...].astype(f32), ev_ref[...].astype(f32))
        (u, v), vjp1 = jax.vjp(stage1, pu_ref[...], pv_ref[...], bu_ref[...], bv_ref[...], lg_ref[...], lb_ref[...])
        dv_scr[...] = u
        v_scr[...] = _c(v)
        mask = _causal_mask(CH)
        lane = lax.broadcasted_iota(jnp.int32, (CH, CH), 1)
        for g in range(G):
            wc = _c(jnp.where(mask, ws_ref[g], 0.0))
            bcol = bs_ref[:, g:g + 1]
            cs = slice(g * Cg, (g + 1) * Cg)
            dws = jnp.zeros((CH, CH), f32)
            dbs = jnp.zeros((CH, 1), f32)
            for ch in range(tm // CH):
                rs = slice(ch * CH, (ch + 1) * CH)
                vs = v_scr[rs, cs]
                mixed = _mm(wc, vs) + bcol
                dgt = du_scr[rs, cs]
                du_scr[rs, cs] = dgt * mixed
                dmixed = dgt * dv_scr[rs, cs]
                dmc2 = _c(dmixed)
                dv_scr[rs, cs] = _mm_tn(wc, dmc2)
                dws = dws + _mm_nt(dmc2, vs)
                dbs = dbs + jnp.sum(dmixed, axis=1, keepdims=True)
            dws_ref[g] += jnp.where(mask, dws, 0.0)
            dbs_ref[...] += jnp.where(lane == g, jnp.broadcast_to(dbs, (CH, CH)), 0.0)
        dpu, dpv, dbu, dbv, dlg, dlb = vjp1((du_scr[...], dv_scr[...]))
        dpre_ref[:, :E] = dpu.astype(dpre_ref.dtype)
        dpre_ref[:, E:] = dpv.astype(dpre_ref.dtype)
        dbin_ref[:, :E] += dbu
        dbin_ref[:, E:] += dbv
        dlg_ref[...] += dlg
        dlb_ref[...] += dlb

    half = lambda p: pl.BlockSpec((tm, E), lambda i: (i, p))
    vhalf = lambda p: pl.BlockSpec((1, E), lambda i: (0, p))
    ve = pl.BlockSpec((1, E), lambda i: (0, 0))
    rd = pl.BlockSpec((tm, D), lambda i: (i, 0))
    vd = pl.BlockSpec((1, D), lambda i: (0, 0))
    wsb = pl.BlockSpec((G, CH, CH), lambda i: (0, 0, 0))
    return pl.pallas_call(
        body, name=name, grid=(S // tm,),
        in_specs=[rd, rd, vd, half(0), half(1), half(0), half(1), vhalf(0), vhalf(1), ve, ve, wsb,
                  pl.BlockSpec((CH, G), lambda i: (0, 0)), pl.BlockSpec((E, D), lambda i: (0, 0))],
        out_specs=(rd, pl.BlockSpec((tm, 2 * E), lambda i: (i, 0)), pl.BlockSpec((1, 2 * E), lambda i: (0, 0)), ve, ve,
                   wsb, pl.BlockSpec((CH, CH), lambda i: (0, 0)), vd),
        out_shape=(SDS((S, D), MXU_DTYPE), SDS((S, 2 * E), MXU_DTYPE), SDS((1, 2 * E), f32), SDS((1, E), f32),
                   SDS((1, E), f32), SDS((G, CH, CH), f32), SDS((CH, CH), f32), SDS((1, D), f32)),
        scratch_shapes=[pltpu.VMEM((tm, E), f32), pltpu.VMEM((tm, E), f32), pltpu.VMEM((tm, E), MXU_DTYPE)],
        compiler_params=_params(1),
    )(dxo, m, g3, pre, pre, kept_erf, kept_erf, b_in, b_in, ln_g, ln_b, w_s, bsT, wout)


def loss_head(y, target, name):
    S, D = y.shape
    tm = _row_tile(S, 512)

    def body(y_ref, t_ref, l_ref, d_ref):
        @pl.when(pl.program_id(0) == 0)
        def _():
            l_ref[...] = jnp.zeros_like(l_ref)

        e = y_ref[...] - t_ref[...]
        d_ref[...] = e * (1.0 / D)
        l_ref[...] += jnp.sum(e * e) * (0.5 / D)

    row = pl.BlockSpec((tm, D), lambda i: (i, 0))
    return pl.pallas_call(
        body, name=name, grid=(S // tm,), in_specs=[row, row],
        out_specs=(pl.BlockSpec((1, HEAD), lambda i: (0, 0)), row),
        out_shape=(SDS((1, HEAD), f32), SDS((S, D), f32)), compiler_params=_params(1),
    )(y, target)


def sum_slots(r, name):
    _, R, C = r.shape
    tr = R // 2 if R % 16 == 0 else R

    def body(r_ref, o_ref):
        acc = r_ref[0].astype(f32)
        for s in range(1, N_DEV):
            acc = acc + r_ref[s].astype(f32)
        o_ref[...] = acc

    return pl.pallas_call(
        body, name=name, grid=(R // tr,), in_specs=[pl.BlockSpec((N_DEV, tr, C), lambda i: (0, i, 0))],
        out_specs=pl.BlockSpec((tr, C), lambda i: (i, 0)), out_shape=SDS((R, C), f32), compiler_params=_params(1),
    )(r)


def _adam_math(w, g, m, v):
    m = ADAM_B1 * m + (1.0 - ADAM_B1) * g
    v = ADAM_B2 * v + (1.0 - ADAM_B2) * (g * g)
    m_hat = m / (1.0 - ADAM_B1 ** ADAM_STEP)
    v_hat = v / (1.0 - ADAM_B2 ** ADAM_STEP)
    delta = -ADAM_LR * (m_hat / (jnp.sqrt(v_hat) + ADAM_EPS) + ADAM_WD * w)
    return delta, m, v


def adam_slots(w, rs, m, v, name, tr):
    R, C = w.shape
    tr = _row_tile(min(r.shape[1] for r in rs), tr)
    blocks = [r.shape[1] // tr for r in rs]
    starts = [sum(blocks[:k]) for k in range(len(rs))]
    assert sum(blocks) * tr == R

    def body(w_ref, *refs):
        r_refs, (m_ref, v_ref, g_ref, d_ref, mo_ref, vo_ref) = refs[:len(rs)], refs[len(rs):]
        i = pl.program_id(0)
        for k, r_ref in enumerate(r_refs):
            @pl.when((i >= starts[k]) & (i < starts[k] + blocks[k]))
            def _():
                g = r_ref[0].astype(f32)
                for s in range(1, N_DEV):
                    g = g + r_ref[s].astype(f32)
                g_ref[...] = g

        d_ref[...], mo_ref[...], vo_ref[...] = _adam_math(w_ref[...], g_ref[...], m_ref[...], v_ref[...])

    row = pl.BlockSpec((tr, C), lambda i: (i, 0))
    piece = lambda k: pl.BlockSpec((N_DEV, tr, C), lambda i: (0, jnp.clip(i - starts[k], 0, blocks[k] - 1), 0))
    return pl.pallas_call(
        body, name=name, grid=(R // tr,), in_specs=[row] + [piece(k) for k in range(len(rs))] + [row, row],
        out_specs=(row, row, row, row), out_shape=tuple(SDS((R, C), f32) for _ in range(4)),
        compiler_params=_params(1),
    )(w, *rs, m, v)


def adam_small(w, g, m, v, name):
    def body(w_ref, g_ref, m_ref, v_ref, d_ref, mo_ref, vo_ref):
        d_ref[...], mo_ref[...], vo_ref[...] = _adam_math(w_ref[...], g_ref[...], m_ref[...], v_ref[...])

    return pl.pallas_call(body, name=name, out_shape=tuple(SDS(w.shape, f32) for _ in range(3)))(w, g, m, v)


def _pack_rows(parts):
    rows, offs, r = [], [], 0
    for p in parts:
        flat = p.reshape(-1)
        n = -(-flat.shape[0] // HEAD)
        flat = jnp.pad(flat, (0, n * HEAD - flat.shape[0]))
        rows.append(flat.reshape(n, HEAD))
        offs.append((r, n))
        r += n
    pad = (-r) % 8
    if pad:
        rows.append(jnp.zeros((pad, HEAD), f32))
    return jnp.concatenate(rows, axis=0), offs


def kernel(x, norm_g, ffn_w_gate, ffn_w_up, ffn_w_down, dn_w_in, dn_conv_w, dn_a_log, dn_dt_bias, dn_norm_g, dn_w_out, sg_w_in, sg_b_in, sg_ln_g, sg_ln_b, sg_w_s, sg_b_s, sg_w_out, loss_target, m_norm_g, m_ffn_w_gate, m_ffn_w_up, m_ffn_w_down, m_dn_w_in, m_dn_conv_w, m_dn_a_log, m_dn_dt_bias, m_dn_norm_g, m_dn_w_out, m_sg_w_in, m_sg_b_in, m_sg_ln_g, m_sg_ln_b, m_sg_w_s, m_sg_b_s, m_sg_w_out, v_norm_g, v_ffn_w_gate, v_ffn_w_up, v_ffn_w_down, v_dn_w_in, v_dn_conv_w, v_dn_a_log, v_dn_dt_bias, v_dn_norm_g, v_dn_w_out, v_sg_w_in, v_sg_b_in, v_sg_ln_g, v_sg_ln_b, v_sg_w_s, v_sg_b_s, v_sg_w_out):
    weights = dict(norm_g=norm_g, ffn_w_gate=ffn_w_gate, ffn_w_up=ffn_w_up, ffn_w_down=ffn_w_down, dn_w_in=dn_w_in,
                   dn_conv_w=dn_conv_w, dn_a_log=dn_a_log, dn_dt_bias=dn_dt_bias, dn_norm_g=dn_norm_g,
                   dn_w_out=dn_w_out, sg_w_in=sg_w_in, sg_b_in=sg_b_in, sg_ln_g=sg_ln_g, sg_ln_b=sg_ln_b,
                   sg_w_s=sg_w_s, sg_b_s=sg_b_s, sg_w_out=sg_w_out)
    mom_m = dict(norm_g=m_norm_g, ffn_w_gate=m_ffn_w_gate, ffn_w_up=m_ffn_w_up, ffn_w_down=m_ffn_w_down,
                 dn_w_in=m_dn_w_in, dn_conv_w=m_dn_conv_w, dn_a_log=m_dn_a_log, dn_dt_bias=m_dn_dt_bias,
                 dn_norm_g=m_dn_norm_g, dn_w_out=m_dn_w_out, sg_w_in=m_sg_w_in, sg_b_in=m_sg_b_in,
                 sg_ln_g=m_sg_ln_g, sg_ln_b=m_sg_ln_b, sg_w_s=m_sg_w_s, sg_b_s=m_sg_b_s, sg_w_out=m_sg_w_out)
    mom_v = dict(norm_g=v_norm_g, ffn_w_gate=v_ffn_w_gate, ffn_w_up=v_ffn_w_up, ffn_w_down=v_ffn_w_down,
                 dn_w_in=v_dn_w_in, dn_conv_w=v_dn_conv_w, dn_a_log=v_dn_a_log, dn_dt_bias=v_dn_dt_bias,
                 dn_norm_g=v_dn_norm_g, dn_w_out=v_dn_w_out, sg_w_in=v_sg_w_in, sg_b_in=v_sg_b_in,
                 sg_ln_g=v_sg_ln_g, sg_ln_b=v_sg_ln_b, sg_w_s=v_sg_w_s, sg_b_s=v_sg_b_s, sg_w_out=v_sg_w_out)
    order = list(weights)

    xs = x[0]
    S, D = xs.shape
    F8 = ffn_w_gate.shape[-1]
    depth = norm_g.shape[0]
    W = dn_w_out.shape[1] * N_DEV
    H = W // HEAD
    E = sg_ln_g.shape[1] * N_DEV
    G, CH = sg_w_s.shape[1], sg_w_s.shape[2]
    c8 = dn_w_in.shape[2]
    me = _slot(lax.axis_index("x"), lax.axis_index("y"), lax.axis_index("c"))

    assert depth == 2
    small_in, small_offs = _pack_rows([norm_g, dn_conv_w, sg_b_in, sg_ln_g, sg_ln_b])
    wg0a, wu0a, wd0a, small_all = all_gather_multi(
        [_c(ffn_w_gate[0, 0]), _c(ffn_w_up[0, 0]), _c(ffn_w_down[0, 0]), small_in], name="gather_first")
    ffn_shards = lambda l, ab: [_c(ffn_w_gate[l, ab]), _c(ffn_w_up[l, ab]), _c(ffn_w_down[l, ab])]
    gather_dn = Comm("gather", [_c(dn_w_in[0]), _c(dn_w_out[0])])
    gather_mid = Comm("gather", ffn_shards(0, 1) + ffn_shards(1, 0))
    gather_end = Comm("gather", ffn_shards(1, 1))
    gather_sg = Comm("gather", [_c(sg_w_in[0]), _c(sg_w_out[0])])
    per = N_DEV // FFN_SLABS
    wide = lambda tag, g, u, d: (*widen_slabs([g, u], FFN_SLABS, name=f"widen_{tag}"),
                                 d.reshape(FFN_SLABS, per * F8, D))
    ffn_w = {(0, 0): wide("0a", wg0a, wu0a, wd0a)}

    def small_piece(i, shard_shape):
        r0, n = small_offs[i]
        sz = math.prod(shard_shape)
        return small_all[:, r0:r0 + n, :].reshape(N_DEV, n * HEAD)[:, :sz].reshape((N_DEV,) + tuple(shard_shape))

    ng_full = jnp.moveaxis(small_piece(0, norm_g.shape), 0, 2).reshape(depth, 6, D)
    conv_full = jnp.moveaxis(small_piece(1, dn_conv_w.shape[1:]), 0, 1).reshape(CONV_K, 3 * W)
    bin_full = small_piece(2, sg_b_in.shape[1:]).reshape(1, 2 * E)
    lng_full = small_piece(3, sg_ln_g.shape[1:]).reshape(1, E)
    lnb_full = small_piece(4, sg_ln_b.shape[1:]).reshape(1, E)
    gate_lanes = lambda v: jnp.pad(v.reshape(1, H), ((0, 0), (H, HEAD - 2 * H)))
    al_row, dt_row = gate_lanes(dn_a_log), gate_lanes(dn_dt_bias)
    bsT = sg_b_s[0].T
    gvec = lambda l, k: ng_full[l, k].reshape(1, D)

    saved = []
    cur = xs
    for l in range(depth):
        sv = {}
        sv['x0'] = cur
        (cur, sv['hA'], sv['pA'], sv['qA'], sv['tA'], sv['yA']), got = ffn_fwd(
            cur, gvec(l, 0), gvec(l, 1), *ffn_w[l, 0], name=f"ffn_fwd_{l}a", comm=gather_dn if l == 0 else gather_sg)
        sv['x1'] = cur
        if l == 1:
            sg_win = jnp.moveaxis(got[0], 0, 1).reshape(D, 2 * E)
            sg_wout = got[1].reshape(E, D)
        if l == 0:
            dnin_all, dnout_all = got
            dn_wmain, dn_wba = join_columns(dnin_all, 4 * W, name="dn_w_in_join")
            dn_wout = dnout_all.reshape(W, D)
            sv['hM'], sv['proj'], sv['pba'] = rms_mm(cur, gvec(l, 2), dn_wmain, dn_wba, name=f"dn_in_{l}")
            sv['qkv'] = dn_prep(sv['proj'], conv_full, name=f"dn_prep_{l}")
            sv['gates'] = dn_gates(sv['pba'], al_row, dt_row, H, name=f"dn_gates_{l}")
            (sv['o'], sv['states'], sv['T']), got = dn_chunk_fwd(sv['qkv'], sv['gates'], name=f"dn_chunk_{l}",
                                                                 comm=gather_mid)
            ffn_w[0, 1], ffn_w[1, 0] = wide("0b", *got[0:3]), wide("1a", *got[3:6])
            cur, sv['m'], sv['og'] = dn_out(sv['o'], sv['proj'], dn_norm_g, dn_wout, cur, gvec(l, 3), name=f"dn_out_{l}")
        else:
            sv['hM'], sv['pre'] = rms_mm(cur, gvec(l, 2), sg_win, None, name=f"sg_in_{l}")
            cur, sv['m'], sv['gated'], sv['erf'] = sg_mid(sv['pre'], bin_full, lng_full, lnb_full, sg_w_s[0], bsT, sg_wout,
                                                          cur, gvec(l, 3), name=f"sg_mid_{l}")
        sv['x2'] = cur
        (cur, sv['hB'], sv['pB'], sv['qB'], sv['tB'], sv['yB']), got = ffn_fwd(
            cur, gvec(l, 4), gvec(l, 5), *ffn_w[l, 1], name=f"ffn_fwd_{l}b", comm=gather_end if l == 0 else None)
        if l == 0:
            ffn_w[1, 1] = wide("1b", *got[0:3])
        saved.append(sv)

    loss_blk, dcur = loss_head(cur, loss_target[0], name="loss_head")
    loss = lax.psum(loss_blk[0, 0], ("x", "y", "c"))

    dng = [[None] * 6 for _ in range(depth)]
    ffn_dw = {}
    grads, slots = {}, {}

    def ffn_backward(l, ab, dcur, exchange=None):
        sv, s = saved[l], 'AB'[ab]
        (dcur, da, db, dy, dng[l][4 * ab], dng[l][4 * ab + 1]), got = ffn_bwd_dx(
            dcur, sv['x2' if ab else 'x0'], sv['y' + s], sv['p' + s], sv['q' + s], gvec(l, 4 * ab), gvec(l, 4 * ab + 1),
            *ffn_w[l, ab], name=f"ffn_bwd_{l}{'ab'[ab]}", comm=Comm("exchange", exchange) if exchange else None)
        ffn_dw[l, ab], _ = ffn_bwd_dw(sv['h' + s], dy, sv['t' + s], da, db, name=f"ffn_dw_{l}{'ab'[ab]}")
        return dcur, got

    sv = saved[1]
    dcur, _ = ffn_backward(1, 1, dcur)
    dm, dpre, grads['sg_b_in'], grads['sg_ln_g'], grads['sg_ln_b'], grads['sg_w_s'], dbs, dng[1][3] = sg_mid_bwd(
        dcur, sv['m'], gvec(1, 3), sv['pre'], sv['erf'], bin_full, lng_full, lnb_full, sg_w_s[0], bsT, sg_wout,
        name="sg_mid_bwd_1")
    grads['sg_b_s'] = dbs[:, :G].T
    dsg_wout = tn_mm(sv['gated'], dm, name="sg_wout_dw_1").reshape(N_DEV, E // N_DEV, D)
    dsg_win = tn_mm(sv['hM'], dpre, name="sg_win_dw_1", tn=2 * E // N_DEV, slot_major=True)
    (dcur, dng[1][2]), _ = mm_bwd_dx(dcur, sv['x1'], gvec(1, 2), dpre, sg_win, None, None, name="sg_in_bwd_1")
    dcur, l1b = ffn_backward(1, 0, dcur, exchange=list(ffn_dw[1, 1]))
    sv = saved[0]
    dcur, got = ffn_backward(0, 1, dcur, exchange=[dsg_win, dsg_wout])
    slots['sg_w_in'], slots['sg_w_out'] = [got[0]], [got[1]]
    dm, do, dz, grads['dn_norm_g'], dng[0][3] = dn_out_bwd(dcur, sv['m'], gvec(0, 3), sv['o'], sv['proj'], dn_norm_g,
                                                          dn_wout, name="dn_out_bwd_0")
    ddn_wout = tn_mm(sv['og'], dm, name="dn_wout_dw_0").reshape(N_DEV, W // N_DEV, D)
    (dqkv, dgates), got = dn_chunk_bwd(sv['qkv'], sv['gates'], sv['states'], sv['T'], do, name="dn_chunk_bwd_0",
                                       comm=Comm("exchange", list(ffn_dw[1, 0])))
    l1a = got
    dpba, dal, ddt = dn_gates_bwd(sv['pba'], al_row, dt_row, dgates, H, name="dn_gates_bwd_0")
    grads['dn_a_log'] = dal[:, H:2 * H]
    grads['dn_dt_bias'] = ddt[:, H:2 * H]
    (dproj, grads['dn_conv_w']), got = dn_prep_bwd(sv['proj'], conv_full, dqkv, dz, name="dn_prep_bwd_0",
                                                   comm=Comm("exchange", [*ffn_dw[0, 1], ddn_wout]))
    l0b, slots['dn_w_out'] = got[0:3], [got[3]]
    dw_main = tn_mm(sv['hM'], dproj, name="dn_win_dw_0")
    dw_ba = tn_mm(sv['hM'], dpba, name="dn_wba_dw_0", tn=HEAD)
    ddn_win = split_columns(dw_main, dw_ba, c8, name="dn_w_in_split")
    (dcur, dng[0][2]), got = mm_bwd_dx(dcur, sv['x1'], gvec(0, 2), dproj, dn_wmain, dpba, dn_wba, name="dn_in_bwd_0",
                                       comm=Comm("exchange", [ddn_win]))
    slots['dn_w_in'] = [got[0]]
    small_names = ['norm_g', 'dn_conv_w', 'sg_b_in', 'sg_ln_g', 'sg_ln_b', 'sg_w_s', 'sg_b_s', 'dn_a_log',
                   'dn_dt_bias', 'dn_norm_g']
    small = {}

    def gather_small():
        dng_full = jnp.stack([jnp.concatenate(r, axis=0) for r in dng], axis=0)
        small['parts'] = [dng_full, grads['dn_conv_w'], grads['sg_b_in'], grads['sg_ln_g'], grads['sg_ln_b'],
                          grads['sg_w_s'], grads['sg_b_s'], grads['dn_a_log'], grads['dn_dt_bias'], grads['dn_norm_g']]
        pack, small['offs'] = _pack_rows(small['parts'])
        return Comm("gather", [pack])

    (dcur, da, db, dy, dng[0][0], dng[0][1]), _ = ffn_bwd_dx(
        dcur, sv['x0'], sv['yA'], sv['pA'], sv['qA'], gvec(0, 0), gvec(0, 1), *ffn_w[0, 0], name="ffn_bwd_0a")
    grad_x = dcur[None]
    (dg,), (small_slots,) = ffn_bwd_dw_one(sv['hA'], da, False, name="ffn_dw_0a_gate", comm=gather_small())
    (du,), (xg,) = ffn_bwd_dw_one(sv['hA'], db, False, name="ffn_dw_0a_up", comm=Comm("exchange", [dg]))
    (dd,), (xu,) = ffn_bwd_dw_one(dy, sv['tA'], True, name="ffn_dw_0a_down", comm=Comm("exchange", [du]))
    small_parts, offs = small['parts'], small['offs']
    l0a = [xg, xu, exchange_slots([dd], name="exchange_last")[0]]
    for i, nm in enumerate(['ffn_w_gate', 'ffn_w_up', 'ffn_w_down']):
        slots[nm] = [l0a[i], l0b[i], l1a[i], l1b[i]]
    big_names = ['ffn_w_gate', 'ffn_w_up', 'ffn_w_down', 'dn_w_in', 'dn_w_out', 'sg_w_in', 'sg_w_out']
    slots = [slots[nm] for nm in big_names]
    small_sum = sum_slots(small_slots, name="sum_small_grads")

    def small_grad(i):
        r0, n = offs[i]
        p = small_parts[i]
        return small_sum[r0:r0 + n].reshape(-1)[:p.size].reshape(p.shape)

    def my_shard(full, axis, like):
        n = full.shape[axis] // N_DEV
        return lax.dynamic_slice_in_dim(full, me * n, n, axis).reshape(like.shape)

    g_small = {
        'norm_g': my_shard(small_grad(0), 2, norm_g),
        'dn_conv_w': my_shard(small_grad(1), 1, dn_conv_w),
        'sg_b_in': my_shard(small_grad(2), 1, sg_b_in),
        'sg_ln_g': my_shard(small_grad(3), 1, sg_ln_g),
        'sg_ln_b': my_shard(small_grad(4), 1, sg_ln_b),
        'sg_w_s': small_grad(5).reshape(sg_w_s.shape),
        'sg_b_s': small_grad(6).reshape(sg_b_s.shape),
        'dn_a_log': small_grad(7).reshape(dn_a_log.shape),
        'dn_dt_bias': small_grad(8).reshape(dn_dt_bias.shape),
        'dn_norm_g': small_grad(9).reshape(dn_norm_g.shape),
    }

    out_g, out_d, out_m, out_v = {}, {}, {}, {}
    for nm, r in zip(big_names, slots):
        w = weights[nm]
        cols = w.shape[-1]
        rows = w.size // cols
        tr = {'ffn_w_gate': 512, 'ffn_w_up': 512, 'ffn_w_down': F8 // 2, 'dn_w_in': 256, 'sg_w_in': 256}.get(nm, rows)
        pieces = [p.reshape(N_DEV, -1, cols) for p in r]
        g, d, m2, v2 = adam_slots(w.reshape(rows, cols), pieces, mom_m[nm].reshape(rows, cols),
                                  mom_v[nm].reshape(rows, cols), name=f"adam_{nm}", tr=tr)
        out_g[nm], out_d[nm], out_m[nm], out_v[nm] = (t.reshape(w.shape) for t in (g, d, m2, v2))
    for nm in small_names:
        w = weights[nm]
        cols = w.shape[-1]
        rows = w.size // cols
        two = lambda t: t.reshape(rows, cols)
        d, m2, v2 = adam_small(two(w), two(g_small[nm]), two(mom_m[nm]), two(mom_v[nm]), name=f"adam_{nm}")
        out_g[nm] = g_small[nm]
        out_d[nm], out_m[nm], out_v[nm] = (t.reshape(w.shape) for t in (d, m2, v2))

    return (loss, grad_x, *[out_g[n] for n in order], *[out_d[n] for n in order], *[out_m[n] for n in order],
            *[out_v[n] for n in order])
```

```python
import functools
import math

import jax
import jax.numpy as jnp
from jax import lax
from jax.experimental import pallas as pl
from jax.experimental.pallas import tpu as pltpu

f32 = jnp.float32
MXU_DTYPE = jnp.bfloat16
N_DEV = 8
RMS_EPS = 1e-6
LN_EPS = 1e-5
L2_EPS = 1e-6
HEAD = 128
DN_CHUNK = 64
SG_CHUNK = 128
SG_GROUPS = 8
CONV_K = 4
ADAM_LR, ADAM_B1, ADAM_B2, ADAM_EPS, ADAM_WD, ADAM_STEP = 0.001, 0.9, 0.999, 1e-08, 0.01, 10
VMEM_LIMIT = 56 * 1024 * 1024
FFN_ROWS_FWD, FFN_ROWS_BWD, FFN_ROWS_DW = 1024, 512, 2048
MIX_ROWS, SG_BWD_ROWS = 512, 256
PROJ_ROWS, TN_ROWS = 1024, 2048
FFN_SLABS = 4
SDS = jax.ShapeDtypeStruct
MESH = pl.DeviceIdType.MESH


def _params(n_grid):
    return pltpu.CompilerParams(dimension_semantics=("arbitrary",) * n_grid, vmem_limit_bytes=VMEM_LIMIT)


def _row_tile(s, want):
    t = min(s, want)
    assert s % t == 0, (s, t)
    return t


def _rms(x, g):
    return x * lax.rsqrt(jnp.mean(x * x, axis=-1, keepdims=True) + RMS_EPS) * g


def _rms_bwd(x, g, dy):
    r = lax.rsqrt(jnp.mean(x * x, axis=-1, keepdims=True) + RMS_EPS)
    t = dy * g
    dx = t * r - x * (jnp.mean(x * t, axis=-1, keepdims=True) * (r * r * r))
    return dx, jnp.sum(dy * (x * r), axis=0, keepdims=True)


def _silu(a):
    return a * jax.nn.sigmoid(a)


def _mm(a, b):
    return lax.dot_general(a, b, (((1,), (0,)), ((), ())), preferred_element_type=f32)


def _mm_nt(a, b):
    return lax.dot_general(a, b, (((1,), (1,)), ((), ())), preferred_element_type=f32)


def _mm_tn(a, b):
    return lax.dot_general(a, b, (((0,), (0,)), ((), ())), preferred_element_type=f32)


def _c(x):
    return x.astype(MXU_DTYPE)


def _split(a):
    hi = a.astype(MXU_DTYPE)
    lo = (a - hi.astype(f32)).astype(MXU_DTYPE)
    return hi, lo


def _dot3(a, b, dims):
    ah, al = _split(a)
    bh, bl = _split(b)
    d = lambda p, q: lax.dot_general(p, q, (dims, ((), ())), preferred_element_type=f32)
    return d(ah, bh) + (d(ah, bl) + d(al, bh))


NN, NT, TN = ((1,), (0,)), ((1,), (1,)), ((0,), (0,))


def _slot(px, py, pc):
    return 4 * px + 2 * py + pc


def all_gather_multi(arrs, name):
    return Comm("gather", arrs).alone(name)


def exchange_slots(arrs, name):
    return Comm("exchange", arrs).alone(name)


class Comm:
    def __init__(self, kind, arrs):
        self.kind, self.arrs, self.n = kind, list(arrs), len(arrs)
        hbm = pl.BlockSpec(memory_space=pltpu.HBM)
        self.in_specs = [hbm] * self.n
        self.out_specs = [hbm] * self.n
        lead = (N_DEV,) if kind == "gather" else ()
        self.out_shape = [SDS(lead + tuple(a.shape), a.dtype) for a in self.arrs]
        self.scratch = [pltpu.SemaphoreType.DMA((self.n, 7)), pltpu.SemaphoreType.DMA((self.n, 7)),
                        pltpu.SemaphoreType.DMA((self.n,))]

    def phase(self, p, ins, outs, sems):
        (self._gather if self.kind == "gather" else self._exchange)(p, ins, outs, sems)

    def _gather(self, p, ins, outs, sems):
        send_sems, recv_sems, local_sems = sems
        x, y, c = lax.axis_index("x"), lax.axis_index("y"), lax.axis_index("c")
        me, sibling = (x, y, c), (x, y, 1 - c)
        chips = [(1 - x, y), (x, 1 - y), (1 - x, 1 - y)]

        def copy(a, k, block, to, src=None):
            dst = outs[a].at[_slot(*block)]
            return pltpu.make_async_remote_copy(
                src_ref=dst if src is None else src, dst_ref=dst, send_sem=send_sems.at[a, k],
                recv_sem=recv_sems.at[a, k], device_id=to, device_id_type=MESH)

        mine = [pltpu.make_async_copy(ins[a], outs[a].at[_slot(*me)], local_sems.at[a]) for a in range(self.n)]
        first = [[copy(a, 0, me, sibling, src=ins[a])] +
                 [copy(a, 1 + j, me, (*chip, c), src=ins[a]) for j, chip in enumerate(chips)] for a in range(self.n)]
        passed = [[copy(a, 4 + j, (*chip, c), sibling) for j, chip in enumerate(chips)] for a in range(self.n)]
        if p == 0:
            for a in range(self.n):
                mine[a].start()
            for a in range(self.n):
                for cp in first[a]:
                    cp.start()
        elif p == 1:
            for a in range(self.n):
                for j, chip in enumerate(chips):
                    copy(a, 1 + j, (*chip, c), me).wait_recv()
                    passed[a][j].start()
        else:
            for a in range(self.n):
                copy(a, 0, sibling, me).wait_recv()
                for j, chip in enumerate(chips):
                    copy(a, 4 + j, (*chip, 1 - c), me).wait_recv()
            for a in range(self.n):
                for cp in first[a] + passed[a]:
                    cp.wait_send()
                mine[a].wait()

    def _exchange(self, p, ins, outs, sems):
        send_sems, recv_sems, local_sems = sems
        x, y, c = lax.axis_index("x"), lax.axis_index("y"), lax.axis_index("c")
        me = _slot(x, y, c)
        peers = [(x ^ (k >> 2), y ^ ((k >> 1) & 1), c ^ (k & 1)) for k in range(1, N_DEV)]

        def copy(a, k):
            peer = peers[k - 1]
            return pltpu.make_async_remote_copy(
                src_ref=ins[a].at[_slot(*peer)], dst_ref=outs[a].at[me], send_sem=send_sems.at[a, k - 1],
                recv_sem=recv_sems.at[a, k - 1], device_id=peer, device_id_type=MESH)

        def landed(a, k):
            peer = peers[k - 1]
            return pltpu.make_async_remote_copy(
                src_ref=ins[a].at[me], dst_ref=outs[a].at[_slot(*peer)], send_sem=send_sems.at[a, k - 1],
                recv_sem=recv_sems.at[a, k - 1], device_id=peer, device_id_type=MESH)

        local = [pltpu.make_async_copy(ins[a].at[me], outs[a].at[me], local_sems.at[a]) for a in range(self.n)]
        order = [6, 7, 2, 3, 4, 5, 1]
        if p == 0:
            for a in range(self.n):
                local[a].start()
            for a in range(self.n):
                for k in order:
                    copy(a, k).start()
        elif p == 2:
            for a in range(self.n):
                for k in order:
                    copy(a, k).wait_send()
                    landed(a, k).wait_recv()
                local[a].wait()

    def alone(self, name):
        n = self.n

        def body(*refs):
            for p in range(3):
                self.phase(p, refs[:n], refs[n:2 * n], refs[2 * n:])

        return pl.pallas_call(body, name=name, out_shape=tuple(self.out_shape), in_specs=self.in_specs,
                              out_specs=tuple(self.out_specs), scratch_shapes=self.scratch)(*self.arrs)


def hosted_call(body, comm, steps, *, name, grid, in_specs, out_specs, out_shape, scratch_shapes, args):
    if comm is None:
        outs = pl.pallas_call(body, name=name, grid=grid, in_specs=in_specs, out_specs=tuple(out_specs),
                              out_shape=tuple(out_shape), scratch_shapes=scratch_shapes,
                              compiler_params=_params(len(grid)))(*args)
        return outs, None
    ni, no, ns, cn = len(in_specs), len(out_specs), len(scratch_shapes), comm.n

    def both(*refs):
        h_in, c_in = refs[:ni], refs[ni:ni + cn]
        h_out, c_out = refs[ni + cn:ni + cn + no], refs[ni + cn + no:ni + 2 * cn + no]
        h_scr, c_scr = refs[ni + 2 * cn + no:ni + 2 * cn + no + ns], refs[ni + 2 * cn + no + ns:]
        when = steps()
        pl.when(when[0])(lambda: comm.phase(0, c_in, c_out, c_scr))
        body(*h_in, *h_out, *h_scr)
        pl.when(when[1])(lambda: comm.phase(1, c_in, c_out, c_scr))
        pl.when(when[2])(lambda: comm.phase(2, c_in, c_out, c_scr))

    outs = pl.pallas_call(
        both, name=name, grid=grid, in_specs=list(in_specs) + comm.in_specs,
        out_specs=tuple(out_specs) + tuple(comm.out_specs), out_shape=tuple(out_shape) + tuple(comm.out_shape),
        scratch_shapes=list(scratch_shapes) + comm.scratch, compiler_params=_params(len(grid)),
    )(*args, *comm.arrs)
    return outs[:no], outs[no:]


def _grid_steps(n_outer, n_inner=1):
    total = n_outer * n_inner

    def steps():
        t = pl.program_id(0) * n_inner + (pl.program_id(1) if n_inner > 1 else 0)
        return t == 0, t == (total * 7) // 8, t == total - 1
    return steps


def widen_slabs(arrs, ns, name):
    per = N_DEV // ns
    _, R, C = arrs[0].shape
    n = len(arrs)

    def body(*refs):
        for a in range(n):
            for k in range(per):
                refs[n + a][:, k * C:(k + 1) * C] = refs[a][k]

    return pl.pallas_call(
        body, name=name, grid=(ns,), in_specs=[pl.BlockSpec((per, R, C), lambda s: (s, 0, 0))] * n,
        out_specs=tuple(pl.BlockSpec((None, R, per * C), lambda s: (s, 0, 0)) for _ in range(n)),
        out_shape=tuple(SDS((ns, R, per * C), a.dtype) for a in arrs), compiler_params=_params(1))(*arrs)


def join_columns(blocks, n_main, name):
    nb, R, c8 = blocks.shape
    rest = nb * c8 - n_main
    tr = _row_tile(R, 256)

    def body(b_ref, main_ref, rest_ref, full):
        for k in range(nb):
            full[:, k * c8:(k + 1) * c8] = b_ref[k]
        main_ref[...] = full[:, :n_main]
        rest_ref[...] = jnp.zeros_like(rest_ref)
        rest_ref[:, :rest] = full[:, n_main:]

    return pl.pallas_call(
        body, name=name, grid=(R // tr,), in_specs=[pl.BlockSpec((nb, tr, c8), lambda i: (0, i, 0))],
        out_specs=(pl.BlockSpec((tr, n_main), lambda i: (i, 0)), pl.BlockSpec((tr, HEAD), lambda i: (i, 0))),
        out_shape=(SDS((R, n_main), blocks.dtype), SDS((R, HEAD), blocks.dtype)),
        scratch_shapes=[pltpu.VMEM((tr, nb * c8), blocks.dtype)], compiler_params=_params(1))(blocks)


def split_columns(main, rest, c8, name):
    R, n_main = main.shape
    nb = N_DEV
    n_rest = nb * c8 - n_main
    tr = _row_tile(R, 256)

    def body(main_ref, rest_ref, b_ref, full):
        full[:, :n_main] = main_ref[...]
        full[:, n_main:] = rest_ref[:, :n_rest]
        for k in range(nb):
            b_ref[k] = full[:, k * c8:(k + 1) * c8]

    return pl.pallas_call(
        body, name=name, grid=(R // tr,),
        in_specs=[pl.BlockSpec((tr, n_main), lambda i: (i, 0)), pl.BlockSpec((tr, HEAD), lambda i: (i, 0))],
        out_specs=pl.BlockSpec((nb, tr, c8), lambda i: (0, i, 0)), out_shape=SDS((nb, R, c8), main.dtype),
        scratch_shapes=[pltpu.VMEM((tr, nb * c8), main.dtype)], compiler_params=_params(1))(main, rest)


def ffn_fwd(x, gpre, gpost, wg, wu, wd, name, comm=None):
    S, D = x.shape
    nj, F8 = wg.shape[0], wg.shape[-1]
    tm = _row_tile(S, FFN_ROWS_FWD)

    def body(x_ref, gpre_ref, gpost_ref, wg_ref, wu_ref, wd_ref, xo_ref, h_ref, p_ref, q_ref, t_ref, y_ref):
        j = pl.program_id(1)

        @pl.when(j == 0)
        def _():
            h_ref[...] = _rms(x_ref[...], gpre_ref[...]).astype(h_ref.dtype)
            y_ref[...] = jnp.zeros_like(y_ref)

        h = h_ref[...]
        a = _mm(h, wg_ref[...])
        b = _mm(h, wu_ref[...])
        s = jax.nn.sigmoid(a)
        q = a * s
        p_ref[...] = (b * (s + q * (1.0 - s))).astype(p_ref.dtype)
        q_ref[...] = q.astype(q_ref.dtype)
        t = (q * b).astype(t_ref.dtype)
        t_ref[...] = t
        y_ref[...] += _mm(t, wd_ref[...])

        @pl.when(j == nj - 1)
        def _():
            xo_ref[...] = x_ref[...] + 0.5 * _rms(y_ref[...], gpost_ref[...])

    row = pl.BlockSpec((tm, D), lambda i, j: (i, 0))
    vec = pl.BlockSpec((1, D), lambda i, j: (0, 0))
    wcol = pl.BlockSpec((None, D, F8), lambda i, j: (j, 0, 0))
    wrow = pl.BlockSpec((None, F8, D), lambda i, j: (j, 0, 0))
    hid = pl.BlockSpec((None, tm, F8), lambda i, j: (j, i, 0))
    return hosted_call(
        body, comm, _grid_steps(S // tm, nj), name=name, grid=(S // tm, nj),
        in_specs=[row, vec, vec, wcol, wcol, wrow],
        out_specs=(row, row, hid, hid, hid, row),
        out_shape=(SDS((S, D), f32), SDS((S, D), MXU_DTYPE), SDS((nj, S, F8), MXU_DTYPE),
                   SDS((nj, S, F8), MXU_DTYPE), SDS((nj, S, F8), MXU_DTYPE), SDS((S, D), f32)),
        scratch_shapes=[], args=(x, gpre, gpost, wg, wu, wd))


def ffn_bwd_dx(dxo, x, y, p, q, gpre, gpost, wg, wu, wd, name, comm=None):
    S, D = x.shape
    NS, F8 = wg.shape[0], wg.shape[-1]
    sps = 2 if NS % 2 == 0 else 1
    nj = NS // sps
    tm = _row_tile(S, FFN_ROWS_BWD)

    def body(dxo_ref, x_ref, y_ref, p_ref, q_ref, gpre_ref, gpost_ref, wg_ref, wu_ref, wd_ref,
             dx_ref, da_ref, db_ref, dy_ref, dgpre_ref, dgpost_ref, dh_ref):
        i, j = pl.program_id(0), pl.program_id(1)

        @pl.when(j == 0)
        def _():
            @pl.when(i == 0)
            def _():
                dgpre_ref[...] = jnp.zeros_like(dgpre_ref)
                dgpost_ref[...] = jnp.zeros_like(dgpost_ref)

            dy, dg = _rms_bwd(y_ref[...], gpost_ref[...], 0.5 * dxo_ref[...])
            dy_ref[...] = dy.astype(dy_ref.dtype)
            dgpost_ref[...] += dg
            dh_ref[...] = jnp.zeros_like(dh_ref)

        dy = dy_ref[...]
        das, dbs = [], []
        for s in range(sps):
            dt = _mm_nt(dy, wd_ref[s])
            das.append((dt * p_ref[s].astype(f32)).astype(da_ref.dtype))
            dbs.append((dt * q_ref[s].astype(f32)).astype(db_ref.dtype))
            da_ref[s] = das[s]
            db_ref[s] = dbs[s]
        upd = None
        for s in range(sps):
            part = _mm_nt(das[s], wg_ref[s]) + _mm_nt(dbs[s], wu_ref[s])
            upd = part if upd is None else upd + part
        dh_ref[...] += upd

        @pl.when(j == nj - 1)
        def _():
            dxx, dg = _rms_bwd(x_ref[...], gpre_ref[...], dh_ref[...])
            dx_ref[...] = dxo_ref[...] + dxx
            dgpre_ref[...] += dg

    row = pl.BlockSpec((tm, D), lambda i, j: (i, 0))
    vec = pl.BlockSpec((1, D), lambda i, j: (0, 0))
    wcol = pl.BlockSpec((sps, D, F8), lambda i, j: (j, 0, 0))
    wrow = pl.BlockSpec((sps, F8, D), lambda i, j: (j, 0, 0))
    hid = pl.BlockSpec((sps, tm, F8), lambda i, j: (j, i, 0))
    return hosted_call(
        body, comm, _grid_steps(S // tm, nj), name=name, grid=(S // tm, nj),
        in_specs=[row, row, row, hid, hid, vec, vec, wcol, wcol, wrow],
        out_specs=(row, hid, hid, row, vec, vec),
        out_shape=(SDS((S, D), f32), SDS((NS, S, F8), MXU_DTYPE), SDS((NS, S, F8), MXU_DTYPE),
                   SDS((S, D), MXU_DTYPE), SDS((1, D), f32), SDS((1, D), f32)),
        scratch_shapes=[pltpu.VMEM((tm, D), f32)], args=(dxo, x, y, p, q, gpre, gpost, wg, wu, wd))


def ffn_bwd_dw(h, dy, t, da, db, name, comm=None):
    S, D = h.shape
    NS, F8 = t.shape[0], t.shape[-1]
    per = N_DEV // NS
    w8 = F8 // per
    tm = _row_tile(S, FFN_ROWS_DW)
    ni = S // tm

    def body(h_ref, dy_ref, t_ref, da_ref, db_ref, dwg_ref, dwu_ref, dwd_ref, accg, accu, accd):
        i = pl.program_id(1)

        @pl.when(i == 0)
        def _():
            accg[...] = jnp.zeros_like(accg)
            accu[...] = jnp.zeros_like(accu)
            accd[...] = jnp.zeros_like(accd)

        hh = h_ref[...]
        accg[...] += _mm_tn(hh, da_ref[...])
        accu[...] += _mm_tn(hh, db_ref[...])
        accd[...] += _mm_tn(t_ref[...], dy_ref[...])

        @pl.when(i == ni - 1)
        def _():
            for k in range(per):
                ks = slice(k * w8, (k + 1) * w8)
                dwg_ref[k] = accg[:, ks].astype(dwg_ref.dtype)
                dwu_ref[k] = accu[:, ks].astype(dwu_ref.dtype)
                dwd_ref[k] = accd[ks, :].astype(dwd_ref.dtype)

    row = pl.BlockSpec((tm, D), lambda j, i: (i, 0))
    hid = pl.BlockSpec((None, tm, F8), lambda j, i: (j, i, 0))
    wcol = pl.BlockSpec((per, D, w8), lambda j, i: (j, 0, 0))
    wrow = pl.BlockSpec((per, w8, D), lambda j, i: (j, 0, 0))
    return hosted_call(
        body, comm, _grid_steps(NS, ni), name=name, grid=(NS, ni),
        in_specs=[row, row, hid, hid, hid],
        out_specs=(wcol, wcol, wrow),
        out_shape=(SDS((N_DEV, D, w8), MXU_DTYPE), SDS((N_DEV, D, w8), MXU_DTYPE), SDS((N_DEV, w8, D), MXU_DTYPE)),
        scratch_shapes=[pltpu.VMEM((D, F8), f32), pltpu.VMEM((D, F8), f32), pltpu.VMEM((F8, D), f32)],
        args=(h, dy, t, da, db))


def ffn_bwd_dw_one(rows_op, slab_op, hidden_rows, name, comm=None):
    S, D = rows_op.shape
    NS, F8 = slab_op.shape[0], slab_op.shape[-1]
    per = N_DEV // NS
    w8 = F8 // per
    tm = _row_tile(S, FFN_ROWS_DW)
    ni = S // tm

    def body(r_ref, s_ref, o_ref, acc):
        i = pl.program_id(1)

        @pl.when(i == 0)
        def _():
            acc[...] = jnp.zeros_like(acc)

        acc[...] += _mm_tn(s_ref[...], r_ref[...]) if hidden_rows else _mm_tn(r_ref[...], s_ref[...])

        @pl.when(i == ni - 1)
        def _():
            for k in range(per):
                ks = slice(k * w8, (k + 1) * w8)
                o_ref[k] = (acc[ks, :] if hidden_rows else acc[:, ks]).astype(o_ref.dtype)

    blk = (per, w8, D) if hidden_rows else (per, D, w8)
    return hosted_call(
        body, comm, _grid_steps(NS, ni), name=name, grid=(NS, ni),
        in_specs=[pl.BlockSpec((tm, D), lambda j, i: (i, 0)), pl.BlockSpec((None, tm, F8), lambda j, i: (j, i, 0))],
        out_specs=(pl.BlockSpec(blk, lambda j, i: (j, 0, 0)),),
        out_shape=(SDS((N_DEV,) + blk[1:], MXU_DTYPE),),
        scratch_shapes=[pltpu.VMEM((F8, D) if hidden_rows else (D, F8), f32)], args=(rows_op, slab_op))


def rms_mm(x, g, w, w2, name, tn=2048):
    S, D = x.shape
    N = w.shape[1]
    tm = _row_tile(S, PROJ_ROWS)
    tn = _row_tile(N, tn)
    has2 = w2 is not None

    def body(*refs):
        if has2:
            x_ref, g_ref, w_ref, w2_ref, h_ref, o_ref, o2_ref = refs
        else:
            x_ref, g_ref, w_ref, h_ref, o_ref = refs
        j = pl.program_id(1)

        @pl.when(j == 0)
        def _():
            h = _rms(x_ref[...], g_ref[...]).astype(h_ref.dtype)
            h_ref[...] = h
            if has2:
                o2_ref[...] = _mm(h, w2_ref[...])

        o_ref[...] = _mm(h_ref[...], w_ref[...])

    row = pl.BlockSpec((tm, D), lambda i, j: (i, 0))
    in_specs = [row, pl.BlockSpec((1, D), lambda i, j: (0, 0)), pl.BlockSpec((D, tn), lambda i, j: (0, j))]
    out_specs = [row, pl.BlockSpec((tm, tn), lambda i, j: (i, j))]
    out_shape = [SDS((S, D), MXU_DTYPE), SDS((S, N), f32)]
    args = [x, g, w]
    if has2:
        in_specs.append(pl.BlockSpec((D, w2.shape[1]), lambda i, j: (0, 0)))
        out_specs.append(pl.BlockSpec((tm, w2.shape[1]), lambda i, j: (i, 0)))
        out_shape.append(SDS((S, w2.shape[1]), f32))
        args.append(w2)
    return pl.pallas_call(
        body, name=name, grid=(S // tm, N // tn), in_specs=in_specs, out_specs=tuple(out_specs),
        out_shape=tuple(out_shape), compiler_params=_params(2),
    )(*args)


def mm_bwd_dx(dres, x, g, dy, w, dy2, w2, name, tk=2048, comm=None):
    S, D = x.shape
    K = dy.shape[1]
    tm = _row_tile(S, PROJ_ROWS)
    tk = _row_tile(K, tk)
    nk = K // tk
    has2 = dy2 is not None

    def body(*refs):
        if has2:
            dres_ref, x_ref, g_ref, dy_ref, w_ref, dy2_ref, w2_ref, dx_ref, dg_ref, dh_ref = refs
        else:
            dres_ref, x_ref, g_ref, dy_ref, w_ref, dx_ref, dg_ref, dh_ref = refs
        i, k = pl.program_id(0), pl.program_id(1)

        @pl.when(k == 0)
        def _():
            @pl.when(i == 0)
            def _():
                dg_ref[...] = jnp.zeros_like(dg_ref)

            if has2:
                dh_ref[...] = _mm_nt(dy2_ref[...], w2_ref[...])
            else:
                dh_ref[...] = jnp.zeros_like(dh_ref)

        dh_ref[...] += _mm_nt(dy_ref[...], w_ref[...])

        @pl.when(k == nk - 1)
        def _():
            dxx, dg = _rms_bwd(x_ref[...], g_ref[...], dh_ref[...])
            dx_ref[...] = dres_ref[...] + dxx
            dg_ref[...] += dg

    row = pl.BlockSpec((tm, D), lambda i, k: (i, 0))
    vec = pl.BlockSpec((1, D), lambda i, k: (0, 0))
    in_specs = [row, row, vec, pl.BlockSpec((tm, tk), lambda i, k: (i, k)), pl.BlockSpec((D, tk), lambda i, k: (0, k))]
    args = [dres, x, g, dy, w]
    if has2:
        in_specs += [pl.BlockSpec((tm, dy2.shape[1]), lambda i, k: (i, 0)),
                     pl.BlockSpec((D, w2.shape[1]), lambda i, k: (0, 0))]
        args += [dy2, w2]
    return hosted_call(
        body, comm, _grid_steps(S // tm, nk), name=name, grid=(S // tm, nk), in_specs=in_specs, out_specs=(row, vec),
        out_shape=(SDS((S, D), f32), SDS((1, D), f32)), scratch_shapes=[pltpu.VMEM((tm, D), f32)], args=args)


def tn_mm(a, b, name, tn=512, slot_major=False):
    S, K1 = a.shape
    N = b.shape[1]
    tm = _row_tile(S, TN_ROWS)
    tn = _row_tile(N, tn)
    ni = S // tm

    def body(a_ref, b_ref, o_ref, acc):
        i = pl.program_id(1)

        @pl.when(i == 0)
        def _():
            acc[...] = jnp.zeros_like(acc)

        acc[...] += _mm_tn(a_ref[...], b_ref[...])

        @pl.when(i == ni - 1)
        def _():
            o_ref[...] = acc[...].astype(o_ref.dtype)

    if slot_major:
        out_spec, out_shape = pl.BlockSpec((None, K1, tn), lambda j, i: (j, 0, 0)), SDS((N // tn, K1, tn), MXU_DTYPE)
    else:
        out_spec, out_shape = pl.BlockSpec((K1, tn), lambda j, i: (0, j)), SDS((K1, N), MXU_DTYPE)
    return pl.pallas_call(
        body, name=name, grid=(N // tn, ni),
        in_specs=[pl.BlockSpec((tm, K1), lambda j, i: (i, 0)), pl.BlockSpec((tm, tn), lambda j, i: (i, j))],
        out_specs=out_spec, out_shape=out_shape,
        scratch_shapes=[pltpu.VMEM((K1, tn), f32)], compiler_params=_params(2),
    )(a, b)


CONV_ROWS = 512


def _shift_down(cur, prev8, s):
    r = pltpu.roll(cur, s, 0)
    row = lax.broadcasted_iota(jnp.int32, (8, cur.shape[1]), 0)
    top = jnp.where(row < s, pltpu.roll(prev8, s, 0), r[0:8])
    return jnp.concatenate([top, r[8:]], axis=0)


def _shift_up(cur, next8, s):
    n = cur.shape[0]
    r = pltpu.roll(cur, n - s, 0)
    row = lax.broadcasted_iota(jnp.int32, (8, cur.shape[1]), 0)
    bot = jnp.where(row >= 8 - s, pltpu.roll(next8, 8 - s, 0), r[n - 8:])
    return jnp.concatenate([r[:n - 8], bot], axis=0)


def _conv_taps(cur, prev8):
    return [_shift_down(cur, prev8, 3), _shift_down(cur, prev8, 2), _shift_down(cur, prev8, 1), cur]


def _act_qk(c):
    a = _silu(c)
    return a * lax.rsqrt(jnp.sum(a * a, axis=-1, keepdims=True) + L2_EPS)


def dn_prep(proj, conv_w, name):
    S = proj.shape[0]
    W = conv_w.shape[1] // 3
    nh = W // HEAD
    R = _row_tile(S, CONV_ROWS)

    def body(p_ref, w_ref, o_ref):
        j = pl.program_id(0)
        w = w_ref[...]

        def rows(r, prev8):
            cur = p_ref[pl.ds(r, R), :]
            taps = _conv_taps(cur, prev8)
            cv = taps[0] * w[0:1] + taps[1] * w[1:2] + taps[2] * w[2:3] + taps[3] * w[3:4]

            @pl.when(j < 2 * nh)
            def _():
                o_ref[pl.ds(r, R), :] = _act_qk(cv)

            @pl.when(j >= 2 * nh)
            def _():
                o_ref[pl.ds(r, R), :] = _silu(cv)

        rows(0, jnp.zeros((8, HEAD), f32))

        @pl.loop(1, S // R)
        def _(t):
            r = pl.multiple_of(t * R, R)
            rows(r, p_ref[pl.ds(r - 8, 8), :])

    return pl.pallas_call(
        body, name=name, grid=(3 * nh,),
        in_specs=[pl.BlockSpec((S, HEAD), lambda j: (0, j)), pl.BlockSpec((CONV_K, HEAD), lambda j: (0, j))],
        out_specs=pl.BlockSpec((None, S, HEAD), lambda j: (j // nh, 0, j % nh)),
        out_shape=SDS((3, S, W), f32), compiler_params=_params(1),
    )(proj, conv_w)


def dn_prep_bwd(proj, conv_w, dqkv, dz, name, comm=None):
    S = proj.shape[0]
    W = conv_w.shape[1] // 3
    nh = W // HEAD
    nq = 3 * nh
    R = _row_tile(S, CONV_ROWS)
    nr = S // R

    def body(p_ref, w_ref, dq_ref, dz_ref, dp_ref, dw_ref, dc_ref):
        j = pl.program_id(0)

        @pl.when(j >= nq)
        def _():
            dp_ref[...] = dz_ref[...].astype(dp_ref.dtype)

        @pl.when(j < nq)
        def _():
            w = w_ref[...]
            dw_ref[...] = jnp.zeros_like(dw_ref)

            def rows(r, prev8):
                cur = p_ref[pl.ds(r, R), :]
                taps = _conv_taps(cur, prev8)
                cv = taps[0] * w[0:1] + taps[1] * w[1:2] + taps[2] * w[2:3] + taps[3] * w[3:4]
                dn = dq_ref[pl.ds(r, R), :]

                @pl.when(j < 2 * nh)
                def _():
                    dc_ref[pl.ds(r, R), :] = jax.vjp(_act_qk, cv)[1](dn)[0]

                @pl.when(j >= 2 * nh)
                def _():
                    dc_ref[pl.ds(r, R), :] = jax.vjp(_silu, cv)[1](dn)[0]

                dc = dc_ref[pl.ds(r, R), :]
                dw_ref[...] += jnp.concatenate(
                    [jnp.sum(dc * taps[q], axis=0, keepdims=True) for q in range(CONV_K)], axis=0)

            rows(0, jnp.zeros((8, HEAD), f32))

            @pl.loop(1, nr)
            def _(t):
                r = pl.multiple_of(t * R, R)
                rows(r, p_ref[pl.ds(r - 8, 8), :])

            def back(r, next8):
                dc = dc_ref[pl.ds(r, R), :]
                dx = dc * w[3:4]
                for s in (1, 2, 3):
                    dx = dx + _shift_up(dc, next8, s) * w[3 - s:4 - s]
                dp_ref[pl.ds(r, R), :] = dx.astype(dp_ref.dtype)

            @pl.loop(0, nr - 1)
            def _(t):
                r = pl.multiple_of(t * R, R)
                back(r, dc_ref[pl.ds(r + R, 8), :])

            back((nr - 1) * R, jnp.zeros((8, HEAD), f32))

    clamp = lambda j: jnp.minimum(j, nq - 1)
    return hosted_call(
        body, comm, _grid_steps(4 * nh), name=name, grid=(4 * nh,),
        in_specs=[pl.BlockSpec((S, HEAD), lambda j: (0, clamp(j))),
                  pl.BlockSpec((CONV_K, HEAD), lambda j: (0, clamp(j))),
                  pl.BlockSpec((None, S, HEAD), lambda j: (clamp(j) // nh, 0, clamp(j) % nh)),
                  pl.BlockSpec((S, HEAD), lambda j: (0, jnp.maximum(j - nq, 0)))],
        out_specs=(pl.BlockSpec((S, HEAD), lambda j: (0, j)), pl.BlockSpec((CONV_K, HEAD), lambda j: (0, clamp(j)))),
        out_shape=(SDS((S, 4 * W), MXU_DTYPE), SDS((CONV_K, 3 * W), f32)),
        scratch_shapes=[pltpu.VMEM((S, HEAD), f32)], args=(proj, conv_w, dqkv, dz))


def _lane_pick(x, lane):
    sel = lax.broadcasted_iota(jnp.int32, x.shape, 1) == lane
    return jnp.broadcast_to(jnp.sum(jnp.where(sel, x, 0.0), axis=1, keepdims=True), x.shape)


CUM_ROWS = 256


def _sel_mm(m01, x):
    m = _c(m01)
    d = lambda p: lax.dot_general(m, p, (NN, ((), ())), preferred_element_type=f32)
    h1, h2, h3 = _pieces3(x)
    return (d(h1) + d(h2)) + d(h3)


def _chunk_cumsum_matrix(n, transpose):
    r, c = lax.broadcasted_iota(jnp.int32, (n, n), 0), lax.broadcasted_iota(jnp.int32, (n, n), 1)
    sh = int(math.log2(DN_CHUNK))
    same = (r >> sh) == (c >> sh)
    return jnp.where(same & ((r <= c) if transpose else (r >= c)), 1.0, 0.0).astype(f32)


def _gates_by_lane(H, p, al, dt):
    lane = lax.broadcasted_iota(jnp.int32, p.shape, 1)
    g = -jnp.exp(al) * jax.nn.softplus(p + dt)
    return jnp.where(lane < H, jax.nn.sigmoid(p), jnp.where(lane < 2 * H, g, 0.0))


def dn_gates(pba, al, dt, H, name):
    S = pba.shape[0]
    R = _row_tile(S, CUM_ROWS)

    def body(p_ref, al_ref, dt_ref, o_ref):
        raw = _gates_by_lane(H, p_ref[...], al_ref[...], dt_ref[...])
        lane = lax.broadcasted_iota(jnp.int32, raw.shape, 1)
        o_ref[...] = jnp.where(lane < H, raw, _sel_mm(_chunk_cumsum_matrix(R, False), raw))

    blk = pl.BlockSpec((R, HEAD), lambda i: (i, 0))
    par = pl.BlockSpec((1, HEAD), lambda i: (0, 0))
    return pl.pallas_call(body, name=name, grid=(S // R,), in_specs=[blk, par, par], out_specs=blk,
                          out_shape=SDS((S, HEAD), f32), compiler_params=_params(1))(pba, al, dt)


def dn_gates_bwd(pba, al, dt, dgates, H, name):
    S = pba.shape[0]
    R = _row_tile(S, CUM_ROWS)

    def body(p_ref, al_ref, dt_ref, dg_ref, dp_ref, dal_ref, ddt_ref):
        @pl.when(pl.program_id(0) == 0)
        def _():
            dal_ref[...] = jnp.zeros_like(dal_ref)
            ddt_ref[...] = jnp.zeros_like(ddt_ref)

        d = dg_ref[...]
        lane = lax.broadcasted_iota(jnp.int32, d.shape, 1)
        d = jnp.where(lane < H, d, _sel_mm(_chunk_cumsum_matrix(R, True), d))
        _, vjp = jax.vjp(functools.partial(_gates_by_lane, H), p_ref[...], al_ref[...], dt_ref[...])
        dp, dal, ddt = vjp(d)
        dp_ref[...] = dp.astype(dp_ref.dtype)
        dal_ref[...] += dal
        ddt_ref[...] += ddt

    blk = pl.BlockSpec((R, HEAD), lambda i: (i, 0))
    par = pl.BlockSpec((1, HEAD), lambda i: (0, 0))
    return pl.pallas_call(
        body, name=name, grid=(S // R,), in_specs=[blk, par, par, blk], out_specs=(blk, par, par),
        out_shape=(SDS((S, HEAD), MXU_DTYPE), SDS((1, HEAD), f32), SDS((1, HEAD), f32)), compiler_params=_params(1),
    )(pba, al, dt, dgates)


def _bdot(dims):
    back = {NN: ((NT, 'gb'), (TN, 'ag')), NT: ((NN, 'gb'), (TN, 'ga')), TN: ((NT, 'bg'), (NN, 'ag'))}[dims]
    d = lambda p, q, dm: lax.dot_general(_c(p), _c(q), (dm, ((), ())), preferred_element_type=f32)

    @jax.custom_vjp
    def f(a, b):
        return d(a, b, dims)

    def fwd(a, b):
        return d(a, b, dims), (a, b)

    def bwd(res, g):
        v = {'a': res[0], 'b': res[1], 'g': g}
        (da_dims, da_ops), (db_dims, db_ops) = back
        return d(v[da_ops[0]], v[da_ops[1]], da_dims), d(v[db_ops[0]], v[db_ops[1]], db_dims)

    f.defvjp(fwd, bwd)
    return f, lambda a, b: d(a, b, dims)


_BDOT = {dims: _bdot(dims) for dims in (NN, NT, TN)}


def _tri_inv_multi(Ls):
    n = Ls[0].shape[0]
    eye = jnp.where(lax.broadcasted_iota(jnp.int32, (n, n), 0) == lax.broadcasted_iota(jnp.int32, (n, n), 1), 1.0, 0.0)
    P = tuple(-L for L in Ls)
    T = tuple(eye + p for p in P)
    P = tuple(_dot3(p, p, NN) for p in P)
    levels = int(math.log2(n)) - 1
    for lvl in range(levels):
        if lvl == levels - 1:
            T = tuple(t + _dot3(t, p, NN) for t, p in zip(T, P))
        else:
            both = tuple(_dot3(jnp.concatenate([t, p], axis=0), p, NN) for t, p in zip(T, P))
            T = tuple(t + b[:n] for t, b in zip(T, both))
            P = tuple(b[n:] for b in both)
    return T


@jax.custom_vjp
def _tri_inv_kept(Ls, Ts):
    return Ts


def _tri_inv_kept_bwd(T, dT):
    X = tuple(_dot3(d, t, NT) for d, t in zip(dT, T))
    return tuple(-_dot3(t, x, TN) for t, x in zip(T, X)), tuple(jnp.zeros_like(t) for t in T)


_tri_inv_kept.defvjp(lambda Ls, Ts: (Ts, Ts), _tri_inv_kept_bwd)


def _pieces3(x):
    h1 = x.astype(MXU_DTYPE)
    r1 = x - h1.astype(f32)
    h2 = r1.astype(MXU_DTYPE)
    return h1, h2, (r1 - h2.astype(f32)).astype(MXU_DTYPE)


def _row_bcast_impl(sel_row, gc):
    s = _c(sel_row)
    d = lambda p: lax.dot_general(s, p, (NT, ((), ())), preferred_element_type=f32)
    h1, h2, h3 = _pieces3(gc)
    return (d(h1) + d(h2)) + d(h3)


def _row_bcast_bwd(sel_row, d):
    s = _c(sel_row)
    hi, lo = _split(d)
    t = lambda p: lax.dot_general(p, s, (TN, ((), ())), preferred_element_type=f32)
    return jnp.zeros_like(sel_row), t(hi) + t(lo)


_row_bcast = jax.custom_vjp(_row_bcast_impl)
_row_bcast.defvjp(lambda sel_row, gc: (_row_bcast_impl(sel_row, gc), sel_row), _row_bcast_bwd)


def _col_bcast_impl(gc):
    return gc[:, :DN_CHUNK]


def _col_bcast_bwd(_, d):
    return (jnp.broadcast_to(jnp.sum(d, axis=1, keepdims=True) * (1.0 / HEAD), (d.shape[0], HEAD)),)


_col_bcast = jax.custom_vjp(_col_bcast_impl)
_col_bcast.defvjp(lambda gc: (_col_bcast_impl(gc), None), _col_bcast_bwd)


def _last_row_bcast(n):
    def impl(gc):
        return jnp.broadcast_to(gc[DN_CHUNK - 1:DN_CHUNK, :], (n, HEAD))

    def bwd(_, d):
        row = lax.broadcasted_iota(jnp.int32, (DN_CHUNK, HEAD), 0)
        return (jnp.where(row == DN_CHUNK - 1, jnp.sum(d, axis=0, keepdims=True), 0.0),)

    f = jax.custom_vjp(impl)
    f.defvjp(lambda gc: (impl(gc), None), bwd)
    return impl, f


_LAST_C, _LAST_H = _last_row_bcast(DN_CHUNK), _last_row_bcast(HEAD)


def _halves(axis):
    def impl(x):
        n = x.shape[axis] // 2
        return lax.slice_in_dim(x, 0, n, axis=axis), lax.slice_in_dim(x, n, 2 * n, axis=axis)

    f = jax.custom_vjp(impl)
    f.defvjp(lambda x: (impl(x), None), lambda _, g: (jnp.concatenate(g, axis=axis),))
    return impl, f


_ROW_HALVES, _COL_HALVES = _halves(0), _halves(1)


def _chunk_consts():
    C = DN_CHUNK
    io = lambda shape, ax: lax.broadcasted_iota(jnp.int32, shape, ax)
    one = lambda m: jnp.where(m, 1.0, 0.0).astype(f32)
    r, c = io((C, C), 0), io((C, C), 1)
    return dict(causal=r >= c, strict=r > c, sel_row=one(io((C, HEAD), 1) == 0))


def _chunk_fn(kc, kept_T, q, k, v, gc, bB, S0):
    diff = kept_T is not None
    i = 0 if diff else 1
    mm, mm_nt, mm_tn = _BDOT[NN][i], _BDOT[NT][i], _BDOT[TN][i]
    tri = (lambda Ls: _tri_inv_kept(Ls, kept_T)) if diff else _tri_inv_multi
    each = lambda f, *ls: tuple(f(*a) for a in zip(*ls))
    gcol = each(_col_bcast if diff else _col_bcast_impl, gc)
    grow = each(lambda g: (_row_bcast if diff else _row_bcast_impl)(kc['sel_row'], g), gc)
    glc = each(_LAST_C[i ^ 1], gc)
    glh = each(_LAST_H[i ^ 1], gc)
    decay = each(lambda a, b: jnp.where(kc['causal'], jnp.exp(jnp.where(kc['causal'], a - b, 0.0)), 0.0), gcol, grow)
    rows, cols = _ROW_HALVES[i ^ 1], _COL_HALVES[i ^ 1]
    first, second = (lambda ts: tuple(t[0] for t in ts)), (lambda ts: tuple(t[1] for t in ts))
    kb = each(lambda a, b: a * b, k, bB)
    vb = each(lambda a, b: a * b, v, bB)
    egc = each(jnp.exp, gc)
    qs = each(lambda a: a * (HEAD ** -0.5), q)
    kq = each(lambda a, b, kt: rows(mm_nt(jnp.concatenate([a, b], axis=0), kt)), kb, qs, k)
    kk, qk = first(kq), second(kq)
    T = tri(each(lambda a, d: jnp.where(kc['strict'], a * d, 0.0), kk, decay))
    uw = each(lambda t, a, b, e: cols(mm(t, jnp.concatenate([a, b * e], axis=1))), T, vb, kb, egc)
    u, w = first(uw), second(uw)
    attn = each(lambda a, d: jnp.where(kc['causal'], a * d, 0.0), qk, decay)
    wq = each(lambda a, b, e, s: rows(mm(jnp.concatenate([a, b * e], axis=0), s)), w, qs, egc, S0)
    wS, qS = first(wq), second(wq)
    v_new = each(lambda a, b: a - b, u, wS)
    o = each(lambda a, b: a + b, qS, each(mm, attn, v_new))
    kdec = each(lambda a, gl, g: a * jnp.exp(gl - g), k, glc, gc)
    S1 = each(lambda s, gl, kv: s * jnp.exp(gl) + kv, S0, glh, each(mm_tn, kdec, v_new))
    return (o, S1) if diff else (o, S1, T)


def _chunks_per_step(N):
    return 4 if N % 4 == 0 else (2 if N % 2 == 0 else 1)


def _heads_per_block(H):
    return 8 if H % 8 == 0 else (4 if H % 4 == 0 else 1)


def dn_chunk_fwd(qkv, gates, name, comm=None):
    _, S, W = qkv.shape
    H, C = W // HEAD, DN_CHUNK
    N, HB = S // C, _heads_per_block(H)
    assert HB == H
    CPS = _chunks_per_step(N)

    def body(q_ref, k_ref, v_ref, g_ref, o_ref, st_ref, t_ref, s_scr):
        @pl.when(pl.program_id(1) == 0)
        def _():
            s_scr[...] = jnp.zeros_like(s_scr)

        kc = _chunk_consts()
        sls = [slice(hh * HEAD, (hh + 1) * HEAD) for hh in range(HB)]
        St = tuple(s_scr[hh] for hh in range(HB))
        for c in range(CPS):
            rows = slice(c * C, (c + 1) * C)
            heads = lambda ref: tuple(ref[rows, sl] for sl in sls)
            gr = g_ref[rows, :]
            for hh in range(HB):
                st_ref[c, hh] = St[hh]
            o, St, T = _chunk_fn(kc, None, heads(q_ref), heads(k_ref), heads(v_ref),
                                 tuple(_lane_pick(gr, H + hh) for hh in range(HB)),
                                 tuple(_lane_pick(gr, hh) for hh in range(HB)), St)
            for hh in range(HB):
                o_ref[rows, sls[hh]] = o[hh]
                t_ref[c, hh] = T[hh]
        for hh in range(HB):
            s_scr[hh] = St[hh]

    part = lambda p: pl.BlockSpec((None, CPS * C, HB * HEAD), lambda hb, n: (p, n, hb))
    return hosted_call(
        body, comm, _grid_steps(H // HB, N // CPS), name=name, grid=(H // HB, N // CPS),
        in_specs=[part(0), part(1), part(2), pl.BlockSpec((CPS * C, HEAD), lambda hb, n: (n, 0))],
        out_specs=(pl.BlockSpec((CPS * C, HB * HEAD), lambda hb, n: (n, hb)),
                   pl.BlockSpec((CPS, HB, HEAD, HEAD), lambda hb, n: (n, hb, 0, 0)),
                   pl.BlockSpec((CPS, HB, C, C), lambda hb, n: (n, hb, 0, 0))),
        out_shape=(SDS((S, W), f32), SDS((N, H, HEAD, HEAD), f32), SDS((N, H, C, C), f32)),
        scratch_shapes=[pltpu.VMEM((HB, HEAD, HEAD), f32)], args=(qkv, qkv, qkv, gates))


def dn_chunk_bwd(qkv, gates, states, kept_T, do, name, comm=None):
    _, S, W = qkv.shape
    H, C = W // HEAD, DN_CHUNK
    N, HB = S // C, _heads_per_block(H)
    assert HB == H
    CPS = _chunks_per_step(N)
    NB = N // CPS

    def body(q_ref, k_ref, v_ref, g_ref, st_ref, t_ref, do_ref, dqkv_ref, dg_ref, ds_scr):
        @pl.when(pl.program_id(1) == 0)
        def _():
            ds_scr[...] = jnp.zeros_like(ds_scr)

        kc = _chunk_consts()
        sls = [slice(hh * HEAD, (hh + 1) * HEAD) for hh in range(HB)]
        dSt = tuple(ds_scr[hh] for hh in range(HB))
        for c in reversed(range(CPS)):
            rows = slice(c * C, (c + 1) * C)
            heads = lambda ref: tuple(ref[rows, sl] for sl in sls)
            gr = g_ref[rows, :]
            kept = tuple(t_ref[c, hh] for hh in range(HB))
            _, vjp = jax.vjp(functools.partial(_chunk_fn, kc, kept), heads(q_ref), heads(k_ref), heads(v_ref),
                             tuple(_lane_pick(gr, H + hh) for hh in range(HB)),
                             tuple(_lane_pick(gr, hh) for hh in range(HB)), tuple(st_ref[c, hh] for hh in range(HB)))
            dq, dk, dv, dg, db, dSt = vjp((heads(do_ref), dSt))
            lane = lax.broadcasted_iota(jnp.int32, (C, HEAD), 1)
            dgr = jnp.zeros((C, HEAD), f32)
            for hh in range(HB):
                dqkv_ref[0, rows, sls[hh]] = dq[hh]
                dqkv_ref[1, rows, sls[hh]] = dk[hh]
                dqkv_ref[2, rows, sls[hh]] = dv[hh]
                dgr = dgr + jnp.where(lane == hh, jnp.sum(db[hh], axis=1, keepdims=True), 0.0)
                dgr = dgr + jnp.where(lane == H + hh, jnp.sum(dg[hh], axis=1, keepdims=True), 0.0)
            dg_ref[rows, :] = dgr
        for hh in range(HB):
            ds_scr[hh] = dSt[hh]

    rev = lambda n: NB - 1 - n
    part = lambda p: pl.BlockSpec((None, CPS * C, HB * HEAD), lambda hb, n: (p, rev(n), hb))
    gate = pl.BlockSpec((CPS * C, HEAD), lambda hb, n: (rev(n), 0))
    return hosted_call(
        body, comm, _grid_steps(H // HB, NB), name=name, grid=(H // HB, NB),
        in_specs=[part(0), part(1), part(2), gate,
                  pl.BlockSpec((CPS, HB, HEAD, HEAD), lambda hb, n: (rev(n), hb, 0, 0)),
                  pl.BlockSpec((CPS, HB, C, C), lambda hb, n: (rev(n), hb, 0, 0)),
                  pl.BlockSpec((CPS * C, HB * HEAD), lambda hb, n: (rev(n), hb))],
        out_specs=(pl.BlockSpec((3, CPS * C, HB * HEAD), lambda hb, n: (0, rev(n), hb)), gate),
        out_shape=(SDS((3, S, W), f32), SDS((S, HEAD), f32)),
        scratch_shapes=[pltpu.VMEM((HB, HEAD, HEAD), f32)], args=(qkv, qkv, qkv, gates, states, kept_T, do))


def _gate_norm(o, z, ng):
    return _rms(o, ng) * _silu(z)


def dn_out(o, proj, ng, wout, x1, g3, name):
    S, W = o.shape
    D = x1.shape[1]
    nh = W // HEAD
    tm = _row_tile(S, MIX_ROWS)

    def body(o_ref, z_ref, ng_ref, w_ref, x_ref, g_ref, xo_ref, m_ref, og_ref):
        for h in range(nh):
            sl = slice(h * HEAD, (h + 1) * HEAD)
            og_ref[:, sl] = _gate_norm(o_ref[:, sl], z_ref[:, sl], ng_ref[...]).astype(og_ref.dtype)
        m = _mm(og_ref[...], w_ref[...])
        m_ref[...] = m
        xo_ref[...] = x_ref[...] + _rms(m, g_ref[...])

    rw = pl.BlockSpec((tm, W), lambda i: (i, 0))
    rd = pl.BlockSpec((tm, D), lambda i: (i, 0))
    return pl.pallas_call(
        body, name=name, grid=(S // tm,),
        in_specs=[rw, pl.BlockSpec((tm, W), lambda i: (i, 3)), pl.BlockSpec((1, HEAD), lambda i: (0, 0)),
                  pl.BlockSpec((W, D), lambda i: (0, 0)), rd, pl.BlockSpec((1, D), lambda i: (0, 0))],
        out_specs=(rd, rd, rw),
        out_shape=(SDS((S, D), f32), SDS((S, D), f32), SDS((S, W), MXU_DTYPE)), compiler_params=_params(1),
    )(o, proj, ng, wout, x1, g3)


def dn_out_bwd(dxo, m, g3, o, proj, ng, wout, name):
    S, W = o.shape
    D = m.shape[1]
    nh = W // HEAD
    tm = _row_tile(S, MIX_ROWS)

    def body(dxo_ref, m_ref, g_ref, o_ref, z_ref, ng_ref, w_ref, dm_ref, do_ref, dz_ref, dng_ref, dg_ref):
        @pl.when(pl.program_id(0) == 0)
        def _():
            dng_ref[...] = jnp.zeros_like(dng_ref)
            dg_ref[...] = jnp.zeros_like(dg_ref)

        dm, dg = _rms_bwd(m_ref[...], g_ref[...], dxo_ref[...])
        dg_ref[...] += dg
        dmc = dm.astype(dm_ref.dtype)
        dm_ref[...] = dmc
        dog = _mm_nt(dmc, w_ref[...])
        for h in range(nh):
            sl = slice(h * HEAD, (h + 1) * HEAD)
            _, vjp = jax.vjp(_gate_norm, o_ref[:, sl], z_ref[:, sl], ng_ref[...])
            do, dz, dng = vjp(dog[:, sl])
            do_ref[:, sl] = do
            dz_ref[:, sl] = dz.astype(dz_ref.dtype)
            dng_ref[...] += dng

    rw = pl.BlockSpec((tm, W), lambda i: (i, 0))
    rd = pl.BlockSpec((tm, D), lambda i: (i, 0))
    vd = pl.BlockSpec((1, D), lambda i: (0, 0))
    vh = pl.BlockSpec((1, HEAD), lambda i: (0, 0))
    return pl.pallas_call(
        body, name=name, grid=(S // tm,),
        in_specs=[rd, rd, vd, rw, pl.BlockSpec((tm, W), lambda i: (i, 3)), vh, pl.BlockSpec((W, D), lambda i: (0, 0))],
        out_specs=(rd, rw, rw, vh, vd),
        out_shape=(SDS((S, D), MXU_DTYPE), SDS((S, W), f32), SDS((S, W), MXU_DTYPE), SDS((1, HEAD), f32),
                   SDS((1, D), f32)),
        compiler_params=_params(1),
    )(dxo, m, g3, o, proj, ng, wout)


def _erf_arg(x):
    return lax.erf(x * 0.7071067811865476)


@jax.custom_vjp
def _gelu_with_erf(x, e):
    return 0.5 * x * (1.0 + e)


def _gelu_with_erf_bwd(res, g):
    x, e = res
    return g * (0.5 * (1.0 + e) + x * (jnp.exp(-0.5 * x * x) * 0.3989422804014327)), jnp.zeros_like(e)


_gelu_with_erf.defvjp(lambda x, e: (0.5 * x * (1.0 + e), (x, e)), _gelu_with_erf_bwd)


def _layernorm(t, lg, lb):
    tc = t - jnp.mean(t, axis=-1, keepdims=True)
    return tc * lax.rsqrt(jnp.mean(tc * tc, axis=-1, keepdims=True) + LN_EPS) * lg + lb


def _sg_stage1_kept(eu, ev, pu, pv, bu, bv, lg, lb):
    return _gelu_with_erf(pu + bu, eu), _layernorm(_gelu_with_erf(pv + bv, ev), lg, lb)


def _causal_mask(n):
    return lax.broadcasted_iota(jnp.int32, (n, n), 0) >= lax.broadcasted_iota(jnp.int32, (n, n), 1)


def sg_mid(pre, b_in, ln_g, ln_b, w_s, bsT, wout, x1, g3, name):
    S = pre.shape[0]
    E, D = ln_g.shape[1], x1.shape[1]
    G, CH = SG_GROUPS, SG_CHUNK
    Cg = E // G
    tm = _row_tile(S, MIX_ROWS)

    def body(pu_ref, pv_ref, bu_ref, bv_ref, lg_ref, lb_ref, ws_ref, bs_ref, w_ref, x_ref, g_ref,
             xo_ref, m_ref, gt_ref, e_ref):
        xu, xv = pu_ref[...] + bu_ref[...], pv_ref[...] + bv_ref[...]
        eu, ev = _erf_arg(xu), _erf_arg(xv)
        e_ref[:, :E] = eu.astype(e_ref.dtype)
        e_ref[:, E:] = ev.astype(e_ref.dtype)
        u = 0.5 * xu * (1.0 + eu)
        v = _layernorm(0.5 * xv * (1.0 + ev), lg_ref[...], lb_ref[...])
        mask = _causal_mask(CH)
        for g in range(G):
            wc = _c(jnp.where(mask, ws_ref[g], 0.0))
            bcol = bs_ref[:, g:g + 1]
            cs = slice(g * Cg, (g + 1) * Cg)
            for ch in range(tm // CH):
                rs = slice(ch * CH, (ch + 1) * CH)
                mixed = _mm(wc, _c(v[rs, cs])) + bcol
                gt_ref[rs, cs] = (u[rs, cs] * mixed).astype(gt_ref.dtype)
        m = _mm(gt_ref[...], w_ref[...])
        m_ref[...] = m
        xo_ref[...] = x_ref[...] + _rms(m, g_ref[...])

    half = lambda p: pl.BlockSpec((tm, E), lambda i: (i, p))
    vhalf = lambda p: pl.BlockSpec((1, E), lambda i: (0, p))
    ve = pl.BlockSpec((1, E), lambda i: (0, 0))
    rd = pl.BlockSpec((tm, D), lambda i: (i, 0))
    return pl.pallas_call(
        body, name=name, grid=(S // tm,),
        in_specs=[half(0), half(1), vhalf(0), vhalf(1), ve, ve, pl.BlockSpec((G, CH, CH), lambda i: (0, 0, 0)),
                  pl.BlockSpec((CH, G), lambda i: (0, 0)), pl.BlockSpec((E, D), lambda i: (0, 0)), rd,
                  pl.BlockSpec((1, D), lambda i: (0, 0))],
        out_specs=(rd, rd, pl.BlockSpec((tm, E), lambda i: (i, 0)), pl.BlockSpec((tm, 2 * E), lambda i: (i, 0))),
        out_shape=(SDS((S, D), f32), SDS((S, D), f32), SDS((S, E), MXU_DTYPE), SDS((S, 2 * E), MXU_DTYPE)),
        compiler_params=_params(1),
    )(pre, pre, b_in, b_in, ln_g, ln_b, w_s, bsT, wout, x1, g3)


def sg_mid_bwd(dxo, m, g3, pre, kept_erf, b_in, ln_g, ln_b, w_s, bsT, wout, name):
    S = pre.shape[0]
    E, D = ln_g.shape[1], m.shape[1]
    G, CH = SG_GROUPS, SG_CHUNK
    Cg = E // G
    tm = _row_tile(S, SG_BWD_ROWS)

    def body(dxo_ref, m_ref, g_ref, pu_ref, pv_ref, eu_ref, ev_ref, bu_ref, bv_ref, lg_ref, lb_ref, ws_ref, bs_ref,
             w_ref, dm_ref, dpre_ref, dbin_ref, dlg_ref, dlb_ref, dws_ref, dbs_ref, dg_ref, du_scr, dv_scr):
        @pl.when(pl.program_id(0) == 0)
        def _():
            for r in (dbin_ref, dlg_ref, dlb_ref, dws_ref, dbs_ref, dg_ref):
                r[...] = jnp.zeros_like(r)

        dm, dg = _rms_bwd(m_ref[...], g_ref[...], dxo_ref[...])
        dg_ref[...] += dg
        dmc = dm.astype(dm_ref.dtype)
        dm_ref[...] = dmc
        dgated = _mm_nt(dmc, w_ref[...])
        stage1 = functools.partial(_sg_stage1_kept, eu_ref[...].astype(f32), ev_ref[...].astype(f32))
        (u, v), vjp1 = jax.vjp(stage1, pu_ref[...], pv_ref[...], bu_ref[...], bv_ref[...], lg_ref[...], lb_ref[...])
        mask = _causal_mask(CH)
        lane = lax.broadcasted_iota(jnp.int32, (CH, CH), 1)
        for g in range(G):
            wc = _c(jnp.where(mask, ws_ref[g], 0.0))
            bcol = bs_ref[:, g:g + 1]
            cs = slice(g * Cg, (g + 1) * Cg)
            dws = jnp.zeros((CH, CH), f32)
            dbs = jnp.zeros((CH, 1), f32)
            for ch in range(tm // CH):
                rs = slice(ch * CH, (ch + 1) * CH)
                vs = _c(v[rs, cs])
                mixed = _mm(wc, vs) + bcol
                dgt = dgated[rs, cs]
                du_scr[rs, cs] = dgt * mixed
                dmixed = dgt * u[rs, cs]
                dmc2 = _c(dmixed)
                dv_scr[rs, cs] = _mm_tn(wc, dmc2)
                dws = dws + _mm_nt(dmc2, vs)
                dbs = dbs + jnp.sum(dmixed, axis=1, keepdims=True)
            dws_ref[g] += jnp.where(mask, dws, 0.0)
            dbs_ref[...] += jnp.where(lane == g, jnp.broadcast_to(dbs, (CH, CH)), 0.0)
        dpu, dpv, dbu, dbv, dlg, dlb = vjp1((du_scr[...], dv_scr[...]))
        dpre_ref[:, :E] = dpu.astype(dpre_ref.dtype)
        dpre_ref[:, E:] = dpv.astype(dpre_ref.dtype)
        dbin_ref[:, :E] += dbu
        dbin_ref[:, E:] += dbv
        dlg_ref[...] += dlg
        dlb_ref[...] += dlb

    half = lambda p: pl.BlockSpec((tm, E), lambda i: (i, p))
    vhalf = lambda p: pl.BlockSpec((1, E), lambda i: (0, p))
    ve = pl.BlockSpec((1, E), lambda i: (0, 0))
    rd = pl.BlockSpec((tm, D), lambda i: (i, 0))
    vd = pl.BlockSpec((1, D), lambda i: (0, 0))
    wsb = pl.BlockSpec((G, CH, CH), lambda i: (0, 0, 0))
    return pl.pallas_call(
        body, name=name, grid=(S // tm,),
        in_specs=[rd, rd, vd, half(0), half(1), half(0), half(1), vhalf(0), vhalf(1), ve, ve, wsb,
                  pl.BlockSpec((CH, G), lambda i: (0, 0)), pl.BlockSpec((E, D), lambda i: (0, 0))],
        out_specs=(rd, pl.BlockSpec((tm, 2 * E), lambda i: (i, 0)), pl.BlockSpec((1, 2 * E), lambda i: (0, 0)), ve, ve,
                   wsb, pl.BlockSpec((CH, CH), lambda i: (0, 0)), vd),
        out_shape=(SDS((S, D), MXU_DTYPE), SDS((S, 2 * E), MXU_DTYPE), SDS((1, 2 * E), f32), SDS((1, E), f32),
                   SDS((1, E), f32), SDS((G, CH, CH), f32), SDS((CH, CH), f32), SDS((1, D), f32)),
        scratch_shapes=[pltpu.VMEM((tm, E), f32), pltpu.VMEM((tm, E), f32)], compiler_params=_params(1),
    )(dxo, m, g3, pre, pre, kept_erf, kept_erf, b_in, b_in, ln_g, ln_b, w_s, bsT, wout)


def loss_head(y, target, name):
    S, D = y.shape
    tm = _row_tile(S, 512)

    def body(y_ref, t_ref, l_ref, d_ref):
        @pl.when(pl.program_id(0) == 0)
        def _():
            l_ref[...] = jnp.zeros_like(l_ref)

        e = y_ref[...] - t_ref[...]
        d_ref[...] = e * (1.0 / D)
        l_ref[...] += jnp.sum(e * e) * (0.5 / D)

    row = pl.BlockSpec((tm, D), lambda i: (i, 0))
    return pl.pallas_call(
        body, name=name, grid=(S // tm,), in_specs=[row, row],
        out_specs=(pl.BlockSpec((1, HEAD), lambda i: (0, 0)), row),
        out_shape=(SDS((1, HEAD), f32), SDS((S, D), f32)), compiler_params=_params(1),
    )(y, target)


def sum_slots(r, name):
    _, R, C = r.shape
    tr = R // 2 if R % 16 == 0 else R

    def body(r_ref, o_ref):
        acc = r_ref[0].astype(f32)
        for s in range(1, N_DEV):
            acc = acc + r_ref[s].astype(f32)
        o_ref[...] = acc

    return pl.pallas_call(
        body, name=name, grid=(R // tr,), in_specs=[pl.BlockSpec((N_DEV, tr, C), lambda i: (0, i, 0))],
        out_specs=pl.BlockSpec((tr, C), lambda i: (i, 0)), out_shape=SDS((R, C), f32), compiler_params=_params(1),
    )(r)


def _adam_math(w, g, m, v):
    m = ADAM_B1 * m + (1.0 - ADAM_B1) * g
    v = ADAM_B2 * v + (1.0 - ADAM_B2) * (g * g)
    m_hat = m / (1.0 - ADAM_B1 ** ADAM_STEP)
    v_hat = v / (1.0 - ADAM_B2 ** ADAM_STEP)
    delta = -ADAM_LR * (m_hat / (jnp.sqrt(v_hat) + ADAM_EPS) + ADAM_WD * w)
    return delta, m, v


def adam_slots(w, rs, m, v, name, tr):
    R, C = w.shape
    tr = _row_tile(min(r.shape[1] for r in rs), tr)
    blocks = [r.shape[1] // tr for r in rs]
    starts = [sum(blocks[:k]) for k in range(len(rs))]
    assert sum(blocks) * tr == R

    def body(w_ref, *refs):
        r_refs, (m_ref, v_ref, g_ref, d_ref, mo_ref, vo_ref) = refs[:len(rs)], refs[len(rs):]
        i = pl.program_id(0)
        for k, r_ref in enumerate(r_refs):
            @pl.when((i >= starts[k]) & (i < starts[k] + blocks[k]))
            def _():
                g = r_ref[0].astype(f32)
                for s in range(1, N_DEV):
                    g = g + r_ref[s].astype(f32)
                g_ref[...] = g

        d_ref[...], mo_ref[...], vo_ref[...] = _adam_math(w_ref[...], g_ref[...], m_ref[...], v_ref[...])

    row = pl.BlockSpec((tr, C), lambda i: (i, 0))
    piece = lambda k: pl.BlockSpec((N_DEV, tr, C), lambda i: (0, jnp.clip(i - starts[k], 0, blocks[k] - 1), 0))
    return pl.pallas_call(
        body, name=name, grid=(R // tr,), in_specs=[row] + [piece(k) for k in range(len(rs))] + [row, row],
        out_specs=(row, row, row, row), out_shape=tuple(SDS((R, C), f32) for _ in range(4)),
        compiler_params=_params(1),
    )(w, *rs, m, v)


def adam_small(w, g, m, v, name):
    def body(w_ref, g_ref, m_ref, v_ref, d_ref, mo_ref, vo_ref):
        d_ref[...], mo_ref[...], vo_ref[...] = _adam_math(w_ref[...], g_ref[...], m_ref[...], v_ref[...])

    return pl.pallas_call(body, name=name, out_shape=tuple(SDS(w.shape, f32) for _ in range(3)))(w, g, m, v)


def _pack_rows(parts):
    rows, offs, r = [], [], 0
    for p in parts:
        flat = p.reshape(-1)
        n = -(-flat.shape[0] // HEAD)
        flat = jnp.pad(flat, (0, n * HEAD - flat.shape[0]))
        rows.append(flat.reshape(n, HEAD))
        offs.append((r, n))
        r += n
    pad = (-r) % 8
    if pad:
        rows.append(jnp.zeros((pad, HEAD), f32))
    return jnp.concatenate(rows, axis=0), offs


def kernel(x, norm_g, ffn_w_gate, ffn_w_up, ffn_w_down, dn_w_in, dn_conv_w, dn_a_log, dn_dt_bias, dn_norm_g, dn_w_out, sg_w_in, sg_b_in, sg_ln_g, sg_ln_b, sg_w_s, sg_b_s, sg_w_out, loss_target, m_norm_g, m_ffn_w_gate, m_ffn_w_up, m_ffn_w_down, m_dn_w_in, m_dn_conv_w, m_dn_a_log, m_dn_dt_bias, m_dn_norm_g, m_dn_w_out, m_sg_w_in, m_sg_b_in, m_sg_ln_g, m_sg_ln_b, m_sg_w_s, m_sg_b_s, m_sg_w_out, v_norm_g, v_ffn_w_gate, v_ffn_w_up, v_ffn_w_down, v_dn_w_in, v_dn_conv_w, v_dn_a_log, v_dn_dt_bias, v_dn_norm_g, v_dn_w_out, v_sg_w_in, v_sg_b_in, v_sg_ln_g, v_sg_ln_b, v_sg_w_s, v_sg_b_s, v_sg_w_out):
    weights = dict(norm_g=norm_g, ffn_w_gate=ffn_w_gate, ffn_w_up=ffn_w_up, ffn_w_down=ffn_w_down, dn_w_in=dn_w_in,
                   dn_conv_w=dn_conv_w, dn_a_log=dn_a_log, dn_dt_bias=dn_dt_bias, dn_norm_g=dn_norm_g,
                   dn_w_out=dn_w_out, sg_w_in=sg_w_in, sg_b_in=sg_b_in, sg_ln_g=sg_ln_g, sg_ln_b=sg_ln_b,
                   sg_w_s=sg_w_s, sg_b_s=sg_b_s, sg_w_out=sg_w_out)
    mom_m = dict(norm_g=m_norm_g, ffn_w_gate=m_ffn_w_gate, ffn_w_up=m_ffn_w_up, ffn_w_down=m_ffn_w_down,
                 dn_w_in=m_dn_w_in, dn_conv_w=m_dn_conv_w, dn_a_log=m_dn_a_log, dn_dt_bias=m_dn_dt_bias,
                 dn_norm_g=m_dn_norm_g, dn_w_out=m_dn_w_out, sg_w_in=m_sg_w_in, sg_b_in=m_sg_b_in,
                 sg_ln_g=m_sg_ln_g, sg_ln_b=m_sg_ln_b, sg_w_s=m_sg_w_s, sg_b_s=m_sg_b_s, sg_w_out=m_sg_w_out)
    mom_v = dict(norm_g=v_norm_g, ffn_w_gate=v_ffn_w_gate, ffn_w_up=v_ffn_w_up, ffn_w_down=v_ffn_w_down,
                 dn_w_in=v_dn_w_in, dn_conv_w=v_dn_conv_w, dn_a_log=v_dn_a_log, dn_dt_bias=v_dn_dt_bias,
                 dn_norm_g=v_dn_norm_g, dn_w_out=v_dn_w_out, sg_w_in=v_sg_w_in, sg_b_in=v_sg_b_in,
                 sg_ln_g=v_sg_ln_g, sg_ln_b=v_sg_ln_b, sg_w_s=v_sg_w_s, sg_b_s=v_sg_b_s, sg_w_out=v_sg_w_out)
    order = list(weights)

    xs = x[0]
    S, D = xs.shape
    F8 = ffn_w_gate.shape[-1]
    depth = norm_g.shape[0]
    W = dn_w_out.shape[1] * N_DEV
    H = W // HEAD
    E = sg_ln_g.shape[1] * N_DEV
    G, CH = sg_w_s.shape[1], sg_w_s.shape[2]
    c8 = dn_w_in.shape[2]
    me = _slot(lax.axis_index("x"), lax.axis_index("y"), lax.axis_index("c"))

    assert depth == 2
    small_in, small_offs = _pack_rows([norm_g, dn_conv_w, sg_b_in, sg_ln_g, sg_ln_b])
    wg0a, wu0a, wd0a, small_all = all_gather_multi(
        [_c(ffn_w_gate[0, 0]), _c(ffn_w_up[0, 0]), _c(ffn_w_down[0, 0]), small_in], name="gather_first")
    ffn_shards = lambda l, ab: [_c(ffn_w_gate[l, ab]), _c(ffn_w_up[l, ab]), _c(ffn_w_down[l, ab])]
    gather_dn = Comm("gather", [_c(dn_w_in[0]), _c(dn_w_out[0])])
    gather_mid = Comm("gather", ffn_shards(0, 1) + ffn_shards(1, 0))
    gather_end = Comm("gather", ffn_shards(1, 1))
    gather_sg = Comm("gather", [_c(sg_w_in[0]), _c(sg_w_out[0])])
    per = N_DEV // FFN_SLABS
    wide = lambda tag, g, u, d: (*widen_slabs([g, u], FFN_SLABS, name=f"widen_{tag}"),
                                 d.reshape(FFN_SLABS, per * F8, D))
    ffn_w = {(0, 0): wide("0a", wg0a, wu0a, wd0a)}

    def small_piece(i, shard_shape):
        r0, n = small_offs[i]
        sz = math.prod(shard_shape)
        return small_all[:, r0:r0 + n, :].reshape(N_DEV, n * HEAD)[:, :sz].reshape((N_DEV,) + tuple(shard_shape))

    ng_full = jnp.moveaxis(small_piece(0, norm_g.shape), 0, 2).reshape(depth, 6, D)
    conv_full = jnp.moveaxis(small_piece(1, dn_conv_w.shape[1:]), 0, 1).reshape(CONV_K, 3 * W)
    bin_full = small_piece(2, sg_b_in.shape[1:]).reshape(1, 2 * E)
    lng_full = small_piece(3, sg_ln_g.shape[1:]).reshape(1, E)
    lnb_full = small_piece(4, sg_ln_b.shape[1:]).reshape(1, E)
    gate_lanes = lambda v: jnp.pad(v.reshape(1, H), ((0, 0), (H, HEAD - 2 * H)))
    al_row, dt_row = gate_lanes(dn_a_log), gate_lanes(dn_dt_bias)
    bsT = sg_b_s[0].T
    gvec = lambda l, k: ng_full[l, k].reshape(1, D)

    saved = []
    cur = xs
    for l in range(depth):
        sv = {}
        sv['x0'] = cur
        (cur, sv['hA'], sv['pA'], sv['qA'], sv['tA'], sv['yA']), got = ffn_fwd(
            cur, gvec(l, 0), gvec(l, 1), *ffn_w[l, 0], name=f"ffn_fwd_{l}a", comm=gather_dn if l == 0 else gather_sg)
        sv['x1'] = cur
        if l == 1:
            sg_win = jnp.moveaxis(got[0], 0, 1).reshape(D, 2 * E)
            sg_wout = got[1].reshape(E, D)
        if l == 0:
            dnin_all, dnout_all = got
            dn_wmain, dn_wba = join_columns(dnin_all, 4 * W, name="dn_w_in_join")
            dn_wout = dnout_all.reshape(W, D)
            sv['hM'], sv['proj'], sv['pba'] = rms_mm(cur, gvec(l, 2), dn_wmain, dn_wba, name=f"dn_in_{l}")
            sv['qkv'] = dn_prep(sv['proj'], conv_full, name=f"dn_prep_{l}")
            sv['gates'] = dn_gates(sv['pba'], al_row, dt_row, H, name=f"dn_gates_{l}")
            (sv['o'], sv['states'], sv['T']), got = dn_chunk_fwd(sv['qkv'], sv['gates'], name=f"dn_chunk_{l}",
                                                                 comm=gather_mid)
            ffn_w[0, 1], ffn_w[1, 0] = wide("0b", *got[0:3]), wide("1a", *got[3:6])
            cur, sv['m'], sv['og'] = dn_out(sv['o'], sv['proj'], dn_norm_g, dn_wout, cur, gvec(l, 3), name=f"dn_out_{l}")
        else:
            sv['hM'], sv['pre'] = rms_mm(cur, gvec(l, 2), sg_win, None, name=f"sg_in_{l}")
            cur, sv['m'], sv['gated'], sv['erf'] = sg_mid(sv['pre'], bin_full, lng_full, lnb_full, sg_w_s[0], bsT, sg_wout,
                                                          cur, gvec(l, 3), name=f"sg_mid_{l}")
        sv['x2'] = cur
        (cur, sv['hB'], sv['pB'], sv['qB'], sv['tB'], sv['yB']), got = ffn_fwd(
            cur, gvec(l, 4), gvec(l, 5), *ffn_w[l, 1], name=f"ffn_fwd_{l}b", comm=gather_end if l == 0 else None)
        if l == 0:
            ffn_w[1, 1] = wide("1b", *got[0:3])
        saved.append(sv)

    loss_blk, dcur = loss_head(cur, loss_target[0], name="loss_head")
    loss = lax.psum(loss_blk[0, 0], ("x", "y", "c"))

    dng = [[None] * 6 for _ in range(depth)]
    ffn_dw = {}
    grads, slots = {}, {}

    def ffn_backward(l, ab, dcur, exchange=None):
        sv, s = saved[l], 'AB'[ab]
        (dcur, da, db, dy, dng[l][4 * ab], dng[l][4 * ab + 1]), got = ffn_bwd_dx(
            dcur, sv['x2' if ab else 'x0'], sv['y' + s], sv['p' + s], sv['q' + s], gvec(l, 4 * ab), gvec(l, 4 * ab + 1),
            *ffn_w[l, ab], name=f"ffn_bwd_{l}{'ab'[ab]}", comm=Comm("exchange", exchange) if exchange else None)
        ffn_dw[l, ab], _ = ffn_bwd_dw(sv['h' + s], dy, sv['t' + s], da, db, name=f"ffn_dw_{l}{'ab'[ab]}")
        return dcur, got

    sv = saved[1]
    dcur, _ = ffn_backward(1, 1, dcur)
    dm, dpre, grads['sg_b_in'], grads['sg_ln_g'], grads['sg_ln_b'], grads['sg_w_s'], dbs, dng[1][3] = sg_mid_bwd(
        dcur, sv['m'], gvec(1, 3), sv['pre'], sv['erf'], bin_full, lng_full, lnb_full, sg_w_s[0], bsT, sg_wout,
        name="sg_mid_bwd_1")
    grads['sg_b_s'] = dbs[:, :G].T
    dsg_wout = tn_mm(sv['gated'], dm, name="sg_wout_dw_1").reshape(N_DEV, E // N_DEV, D)
    dsg_win = tn_mm(sv['hM'], dpre, name="sg_win_dw_1", tn=2 * E // N_DEV, slot_major=True)
    (dcur, dng[1][2]), _ = mm_bwd_dx(dcur, sv['x1'], gvec(1, 2), dpre, sg_win, None, None, name="sg_in_bwd_1")
    dcur, l1b = ffn_backward(1, 0, dcur, exchange=list(ffn_dw[1, 1]))
    sv = saved[0]
    dcur, got = ffn_backward(0, 1, dcur, exchange=[dsg_win, dsg_wout])
    slots['sg_w_in'], slots['sg_w_out'] = [got[0]], [got[1]]
    dm, do, dz, grads['dn_norm_g'], dng[0][3] = dn_out_bwd(dcur, sv['m'], gvec(0, 3), sv['o'], sv['proj'], dn_norm_g,
                                                          dn_wout, name="dn_out_bwd_0")
    ddn_wout = tn_mm(sv['og'], dm, name="dn_wout_dw_0").reshape(N_DEV, W // N_DEV, D)
    (dqkv, dgates), got = dn_chunk_bwd(sv['qkv'], sv['gates'], sv['states'], sv['T'], do, name="dn_chunk_bwd_0",
                                       comm=Comm("exchange", list(ffn_dw[1, 0])))
    l1a = got
    dpba, dal, ddt = dn_gates_bwd(sv['pba'], al_row, dt_row, dgates, H, name="dn_gates_bwd_0")
    grads['dn_a_log'] = dal[:, H:2 * H]
    grads['dn_dt_bias'] = ddt[:, H:2 * H]
    (dproj, grads['dn_conv_w']), got = dn_prep_bwd(sv['proj'], conv_full, dqkv, dz, name="dn_prep_bwd_0",
                                                   comm=Comm("exchange", [*ffn_dw[0, 1], ddn_wout]))
    l0b, slots['dn_w_out'] = got[0:3], [got[3]]
    dw_main = tn_mm(sv['hM'], dproj, name="dn_win_dw_0")
    dw_ba = tn_mm(sv['hM'], dpba, name="dn_wba_dw_0", tn=HEAD)
    ddn_win = split_columns(dw_main, dw_ba, c8, name="dn_w_in_split")
    (dcur, dng[0][2]), got = mm_bwd_dx(dcur, sv['x1'], gvec(0, 2), dproj, dn_wmain, dpba, dn_wba, name="dn_in_bwd_0",
                                       comm=Comm("exchange", [ddn_win]))
    slots['dn_w_in'] = [got[0]]
    small_names = ['norm_g', 'dn_conv_w', 'sg_b_in', 'sg_ln_g', 'sg_ln_b', 'sg_w_s', 'sg_b_s', 'dn_a_log',
                   'dn_dt_bias', 'dn_norm_g']
    small = {}

    def gather_small():
        dng_full = jnp.stack([jnp.concatenate(r, axis=0) for r in dng], axis=0)
        small['parts'] = [dng_full, grads['dn_conv_w'], grads['sg_b_in'], grads['sg_ln_g'], grads['sg_ln_b'],
                          grads['sg_w_s'], grads['sg_b_s'], grads['dn_a_log'], grads['dn_dt_bias'], grads['dn_norm_g']]
        pack, small['offs'] = _pack_rows(small['parts'])
        return Comm("gather", [pack])

    (dcur, da, db, dy, dng[0][0], dng[0][1]), _ = ffn_bwd_dx(
        dcur, sv['x0'], sv['yA'], sv['pA'], sv['qA'], gvec(0, 0), gvec(0, 1), *ffn_w[0, 0], name="ffn_bwd_0a")
    grad_x = dcur[None]
    (dg,), (small_slots,) = ffn_bwd_dw_one(sv['hA'], da, False, name="ffn_dw_0a_gate", comm=gather_small())
    (du,), (xg,) = ffn_bwd_dw_one(sv['hA'], db, False, name="ffn_dw_0a_up", comm=Comm("exchange", [dg]))
    (dd,), (xu,) = ffn_bwd_dw_one(dy, sv['tA'], True, name="ffn_dw_0a_down", comm=Comm("exchange", [du]))
    small_parts, offs = small['parts'], small['offs']
    l0a = [xg, xu, exchange_slots([dd], name="exchange_last")[0]]
    for i, nm in enumerate(['ffn_w_gate', 'ffn_w_up', 'ffn_w_down']):
        slots[nm] = [l0a[i], l0b[i], l1a[i], l1b[i]]
    big_names = ['ffn_w_gate', 'ffn_w_up', 'ffn_w_down', 'dn_w_in', 'dn_w_out', 'sg_w_in', 'sg_w_out']
    slots = [slots[nm] for nm in big_names]
    small_sum = sum_slots(small_slots, name="sum_small_grads")

    def small_grad(i):
        r0, n = offs[i]
        p = small_parts[i]
        return small_sum[r0:r0 + n].reshape(-1)[:p.size].reshape(p.shape)

    def my_shard(full, axis, like):
        n = full.shape[axis] // N_DEV
        return lax.dynamic_slice_in_dim(full, me * n, n, axis).reshape(like.shape)

    g_small = {
        'norm_g': my_shard(small_grad(0), 2, norm_g),
        'dn_conv_w': my_shard(small_grad(1), 1, dn_conv_w),
        'sg_b_in': my_shard(small_grad(2), 1, sg_b_in),
        'sg_ln_g': my_shard(small_grad(3), 1, sg_ln_g),
        'sg_ln_b': my_shard(small_grad(4), 1, sg_ln_b),
        'sg_w_s': small_grad(5).reshape(sg_w_s.shape),
        'sg_b_s': small_grad(6).reshape(sg_b_s.shape),
        'dn_a_log': small_grad(7).reshape(dn_a_log.shape),
        'dn_dt_bias': small_grad(8).reshape(dn_dt_bias.shape),
        'dn_norm_g': small_grad(9).reshape(dn_norm_g.shape),
    }

    out_g, out_d, out_m, out_v = {}, {}, {}, {}
    for nm, r in zip(big_names, slots):
        w = weights[nm]
        cols = w.shape[-1]
        rows = w.size // cols
        tr = {'ffn_w_gate': 512, 'ffn_w_up': 512, 'ffn_w_down': F8 // 2, 'dn_w_in': 256, 'sg_w_in': 256}.get(nm, rows)
        pieces = [p.reshape(N_DEV, -1, cols) for p in r]
        g, d, m2, v2 = adam_slots(w.reshape(rows, cols), pieces, mom_m[nm].reshape(rows, cols),
                                  mom_v[nm].reshape(rows, cols), name=f"adam_{nm}", tr=tr)
        out_g[nm], out_d[nm], out_m[nm], out_v[nm] = (t.reshape(w.shape) for t in (g, d, m2, v2))
    for nm in small_names:
        w = weights[nm]
        cols = w.shape[-1]
        rows = w.size // cols
        two = lambda t: t.reshape(rows, cols)
        d, m2, v2 = adam_small(two(w), two(g_small[nm]), two(mom_m[nm]), two(mom_v[nm]), name=f"adam_{nm}")
        out_g[nm] = g_small[nm]
        out_d[nm], out_m[nm], out_v[nm] = (t.reshape(w.shape) for t in (d, m2, v2))

    return (loss, grad_x, *[out_g[n] for n in order], *[out_d[n] for n in order], *[out_m[n] for n in order],
            *[out_v[n] for n in order])
```
